```python
import math
import jax, jax.numpy as jnp
from jax import lax
import numpy as np

D_MODEL = 1024
BATCH = 8
SEQ = 2048
DEPTH = 4

MEM_LEN = 256
EPS = 1e-6
SSM_HEADS = 16
SSM_HEAD_DIM = 64
D_SSM = SSM_HEADS * SSM_HEAD_DIM
SSM_GROUPS = 4
SSM_STATE = 128
SSM_CONV = 4
SSM_CHUNK = 128
CONV_CH = D_SSM + 2 * SSM_GROUPS * SSM_STATE
MLA_HEADS = 16
QK_NOPE = 64
QK_ROPE = 32
V_DIM = 64
Q_LORA = 384
KV_LORA = 256
D_ATTN = MLA_HEADS * V_DIM
ROPE_THETA = 10000.0
Q_BLOCK = 128
D_MIX = D_SSM + D_ATTN
_O1 = D_SSM
_O2 = _O1 + CONV_CH
_O3 = _O2 + SSM_HEADS
_O4 = _O3 + Q_LORA
_O5 = _O4 + KV_LORA
D_IN = _O5 + QK_ROPE
IN_SPLITS = (_O1, _O2, _O3, _O4, _O5)
MEM_HEADS = 4
MEM_HEAD_DIM = D_MODEL // MEM_HEADS
D_FF = 2816
FFN_CONV = 3

kernel_name = "hymba_ssd_mla_memxattn_convffn"


def rmsnorm(x, g):
    xf = x.astype(jnp.float32)
    var = jnp.mean(xf * xf, axis=-1, keepdims=True)
    return (xf * lax.rsqrt(var + EPS) * g.astype(jnp.float32)).astype(x.dtype)


def causal_dwconv(x, w, b):
    k = w.shape[0]
    s = x.shape[1]
    xp = jnp.pad(x, ((0, 0), (k - 1, 0), (0, 0)))
    y = xp[:, 0:s] * w[0]
    for j in range(1, k):
        y = y + xp[:, j:j + s] * w[j]
    return y + b


def rope_tables(positions):
    inv_freq = 1.0 / (ROPE_THETA ** (jnp.arange(0, QK_ROPE, 2, dtype=jnp.float32) / QK_ROPE))
    ang = positions.astype(jnp.float32)[..., None] * inv_freq
    return jnp.cos(ang), jnp.sin(ang)


def apply_rope(t, cos, sin):
    half = t.shape[-1] // 2
    t1, t2 = t[..., :half], t[..., half:]
    out = jnp.concatenate([t1 * cos - t2 * sin, t2 * cos + t1 * sin], axis=-1)
    return out.astype(t.dtype)


def segsum_exp(a):
    t = a.shape[-1]
    cs = jnp.cumsum(a, axis=-1)
    diff = cs[..., :, None] - cs[..., None, :]
    mask = jnp.tril(jnp.ones((t, t), dtype=bool))
    return jnp.exp(jnp.where(mask, diff, -jnp.inf))


def ssd_scan(x, dt, a, bm, cm):
    out_dtype = x.dtype
    x = x.astype(jnp.float32)
    bm = bm.astype(jnp.float32)
    cm = cm.astype(jnp.float32)
    b, l, h, p = x.shape
    g, n = bm.shape[-2:]
    r = h // g
    c = l // SSM_CHUNK
    xd = (x * dt[..., None]).reshape(b, c, SSM_CHUNK, g, r, p)
    ad = jnp.moveaxis((dt * a).reshape(b, c, SSM_CHUNK, g, r), 2, -1)
    a_cs = jnp.cumsum(ad, axis=-1)
    bc = bm.reshape(b, c, SSM_CHUNK, g, n)
    cc = cm.reshape(b, c, SSM_CHUNK, g, n)
    lmat = segsum_exp(ad)
    cb = jnp.einsum('bclgn,bcsgn->bcgls', cc, bc)
    y_diag = jnp.einsum('bcgls,bcgrls,bcsgrp->bclgrp', cb, lmat, xd)
    decay_states = jnp.exp(a_cs[..., -1:] - a_cs)
    states = jnp.einsum('bclgn,bcgrl,bclgrp->bcgrpn', bc, decay_states, xd)
    chunk_decay = jnp.exp(a_cs[..., -1])

    def step(prev, inp):
        st, dec = inp
        return prev * dec[..., None, None] + st, prev

    init = jnp.zeros((b, g, r, p, n), jnp.float32)
    _, prev_states = lax.scan(step, init, (jnp.moveaxis(states, 1, 0), jnp.moveaxis(chunk_decay, 1, 0)))
    prev_states = jnp.moveaxis(prev_states, 0, 1)
    y_off = jnp.einsum('bclgn,bcgrpn,bcgrl->bclgrp', cc, prev_states, jnp.exp(a_cs))
    return (y_diag + y_off).reshape(b, l, h, p).astype(out_dtype)


def mla_attention(c_q, c_kv, k_rope, q_norm, w_uq, kv_norm, w_ukv, cos, sin):
    b, s, _ = c_q.shape
    q = (rmsnorm(c_q, q_norm) @ w_uq).reshape(b, s, MLA_HEADS, QK_NOPE + QK_ROPE)
    q_nope = q[..., :QK_NOPE]
    q_pe = apply_rope(q[..., QK_NOPE:], cos[:, :, None, :], sin[:, :, None, :])
    kv = (rmsnorm(c_kv, kv_norm) @ w_ukv).reshape(b, s, MLA_HEADS, QK_NOPE + V_DIM)
    k_nope, v = kv[..., :QK_NOPE], kv[..., QK_NOPE:]
    k_pe = apply_rope(k_rope, cos, sin)
    scale = (QK_NOPE + QK_ROPE) ** -0.5
    outs = []
    for i in range(s // Q_BLOCK):
        q0 = i * Q_BLOCK
        kend = q0 + Q_BLOCK
        sc = (jnp.einsum('bqhd,bkhd->bhqk', q_nope[:, q0:kend], k_nope[:, :kend])
              + jnp.einsum('bqhr,bkr->bhqk', q_pe[:, q0:kend], k_pe[:, :kend]))
        sc = sc.astype(jnp.float32) * scale
        mask = (q0 + jnp.arange(Q_BLOCK))[:, None] >= jnp.arange(kend)[None, :]
        pr = jax.nn.softmax(jnp.where(mask, sc, -jnp.inf), axis=-1).astype(v.dtype)
        outs.append(jnp.einsum('bhqk,bkhd->bqhd', pr, v[:, :kend]))
    return jnp.concatenate(outs, axis=1).reshape(b, s, D_ATTN)


def memory_attention(h, m, w_q, w_k, w_v, w_o):
    b, s, _ = h.shape
    ml = m.shape[1]
    q = (h @ w_q).reshape(b, s, MEM_HEADS, MEM_HEAD_DIM)
    k = (m @ w_k).reshape(b, ml, MEM_HEADS, MEM_HEAD_DIM)
    v = (m @ w_v).reshape(b, ml, MEM_HEADS, MEM_HEAD_DIM)
    sc = jnp.einsum('bqhd,bkhd->bhqk', q, k).astype(jnp.float32) * (MEM_HEAD_DIM ** -0.5)
    pr = jax.nn.softmax(sc, axis=-1).astype(v.dtype)
    o = jnp.einsum('bhqk,bkhd->bqhd', pr, v).reshape(b, s, D_MODEL)
    return o @ w_o


def conv_glu_ffn(h, w_up, conv_w, conv_b, w_down):
    u = causal_dwconv(h @ w_up, conv_w, conv_b)
    gate, val = u[..., :D_FF], u[..., D_FF:]
    return (jax.nn.silu(gate) * val) @ w_down


def _fwd_setup_inputs(seed: int = 0) -> dict:
    key = jax.random.key(seed)
    ks = iter(jax.random.split(key, 64))

    def nrm(shape, scale):
        return jax.random.normal(next(ks), shape, jnp.float32) * scale

    def gain(shape):
        return 1.0 + nrm(shape, 0.02)

    L = DEPTH
    dt0 = jnp.exp(jax.random.uniform(next(ks), (L, SSM_HEADS), jnp.float32,
                                     math.log(1e-3), math.log(1e-1)))
    dt_bias = dt0 + jnp.log(-jnp.expm1(-dt0))
    a_log = jnp.log(jax.random.uniform(next(ks), (L, SSM_HEADS), jnp.float32, 1.0, 16.0))
    offsets = jax.random.randint(next(ks), (BATCH, 1), 0, 1024, dtype=jnp.int32)
    positions = offsets + jnp.arange(SEQ, dtype=jnp.int32)[None, :]
    return {
        "x": nrm((BATCH, SEQ, D_MODEL), 1.0),
        "mem": nrm((BATCH, MEM_LEN, D_MODEL), 1.0),
        "positions": positions,
        "norm_mix": gain((L, D_MODEL)),
        "w_in": nrm((L, D_MODEL, D_IN), D_MODEL ** -0.5),
        "ssm_conv_w": nrm((L, SSM_CONV, CONV_CH), SSM_CONV ** -0.5),
        "ssm_conv_b": nrm((L, CONV_CH), 0.02),
        "dt_bias": dt_bias,
        "a_log": a_log,
        "d_skip": 1.0 + nrm((L, SSM_HEADS), 0.1),
        "ssm_norm": gain((L, D_SSM)),
        "q_norm": gain((L, Q_LORA)),
        "w_uq": nrm((L, Q_LORA, MLA_HEADS * (QK_NOPE + QK_ROPE)), Q_LORA ** -0.5),
        "kv_norm": gain((L, KV_LORA)),
        "w_ukv": nrm((L, KV_LORA, MLA_HEADS * (QK_NOPE + V_DIM)), KV_LORA ** -0.5),
        "attn_out_norm": gain((L, D_ATTN)),
        "w_out": nrm((L, D_MIX, D_MODEL), D_MIX ** -0.5),
        "norm_mem_q": gain((L, D_MODEL)),
        "norm_mem_kv": gain((L, D_MODEL)),
        "w_mq": nrm((L, D_MODEL, D_MODEL), D_MODEL ** -0.5),
        "w_mk": nrm((L, D_MODEL, D_MODEL), D_MODEL ** -0.5),
        "w_mv": nrm((L, D_MODEL, D_MODEL), D_MODEL ** -0.5),
        "w_mo": nrm((L, D_MODEL, D_MODEL), D_MODEL ** -0.5),
        "norm_ffn": gain((L, D_MODEL)),
        "w_up": nrm((L, D_MODEL, 2 * D_FF), D_MODEL ** -0.5),
        "ffn_conv_w": nrm((L, FFN_CONV, 2 * D_FF), FFN_CONV ** -0.5),
        "ffn_conv_b": nrm((L, 2 * D_FF), 0.02),
        "w_down": nrm((L, D_FF, D_MODEL), D_FF ** -0.5),
        "final_norm": gain((D_MODEL,)),
    }


def _fwd_reference(x, mem, positions, norm_mix, w_in, ssm_conv_w, ssm_conv_b, dt_bias, a_log,
              d_skip, ssm_norm, q_norm, w_uq, kv_norm, w_ukv, attn_out_norm, w_out,
              norm_mem_q, norm_mem_kv, w_mq, w_mk, w_mv, w_mo, norm_ffn, w_up,
              ffn_conv_w, ffn_conv_b, w_down, final_norm):
    b, s, _ = x.shape
    cos, sin = rope_tables(positions)
    for i in range(DEPTH):
        h = rmsnorm(x, norm_mix[i])
        proj = h @ w_in[i]
        z, xbc, dt_raw, c_q, c_kv, k_rope = jnp.split(proj, IN_SPLITS, axis=-1)
        xbc = jax.nn.silu(causal_dwconv(xbc, ssm_conv_w[i], ssm_conv_b[i]))
        xs = xbc[..., :D_SSM].reshape(b, s, SSM_HEADS, SSM_HEAD_DIM)
        bm = xbc[..., D_SSM:D_SSM + SSM_GROUPS * SSM_STATE].reshape(b, s, SSM_GROUPS, SSM_STATE)
        cm = xbc[..., D_SSM + SSM_GROUPS * SSM_STATE:].reshape(b, s, SSM_GROUPS, SSM_STATE)
        dt = jax.nn.softplus(dt_raw.astype(jnp.float32) + dt_bias[i].astype(jnp.float32))
        a = -jnp.exp(a_log[i].astype(jnp.float32))
        y = ssd_scan(xs, dt, a, bm, cm) + xs * d_skip[i][:, None]
        y_ssm = rmsnorm(y.reshape(b, s, D_SSM) * jax.nn.silu(z), ssm_norm[i])
        y_att = mla_attention(c_q, c_kv, k_rope, q_norm[i], w_uq[i], kv_norm[i], w_ukv[i], cos, sin)
        y_att = rmsnorm(y_att, attn_out_norm[i])
        x = x + jnp.concatenate([y_ssm, y_att], axis=-1) @ w_out[i]
        x = x + memory_attention(rmsnorm(x, norm_mem_q[i]), rmsnorm(mem, norm_mem_kv[i]),
                                 w_mq[i], w_mk[i], w_mv[i], w_mo[i])
        x = x + conv_glu_ffn(rmsnorm(x, norm_ffn[i]), w_up[i], ffn_conv_w[i], ffn_conv_b[i], w_down[i])
    return rmsnorm(x, final_norm)


import jax as _jax
import jax.numpy as _jnp

TWIN_FORMAT = 'train_step'
FWD_PARAMS = ['x', 'mem', 'positions', 'norm_mix', 'w_in', 'ssm_conv_w', 'ssm_conv_b', 'dt_bias', 'a_log', 'd_skip', 'ssm_norm', 'q_norm', 'w_uq', 'kv_norm', 'w_ukv', 'attn_out_norm', 'w_out', 'norm_mem_q', 'norm_mem_kv', 'w_mq', 'w_mk', 'w_mv', 'w_mo', 'norm_ffn', 'w_up', 'ffn_conv_w', 'ffn_conv_b', 'w_down', 'final_norm']
TWIN_WEIGHTS = ['norm_mix', 'w_in', 'ssm_conv_w', 'ssm_conv_b', 'dt_bias', 'a_log', 'd_skip', 'ssm_norm', 'q_norm', 'w_uq', 'kv_norm', 'w_ukv', 'attn_out_norm', 'w_out', 'norm_mem_q', 'norm_mem_kv', 'w_mq', 'w_mk', 'w_mv', 'w_mo', 'norm_ffn', 'w_up', 'ffn_conv_w', 'ffn_conv_b', 'w_down', 'final_norm']
TWIN_DIFF_INPUT = 'x'
TWIN_INPUTS = ['x', 'mem', 'positions', 'norm_mix', 'w_in', 'ssm_conv_w', 'ssm_conv_b', 'dt_bias', 'a_log', 'd_skip', 'ssm_norm', 'q_norm', 'w_uq', 'kv_norm', 'w_ukv', 'attn_out_norm', 'w_out', 'norm_mem_q', 'norm_mem_kv', 'w_mq', 'w_mk', 'w_mv', 'w_mo', 'norm_ffn', 'w_up', 'ffn_conv_w', 'ffn_conv_b', 'w_down', 'final_norm', 'loss_target', 'm_norm_mix', 'm_w_in', 'm_ssm_conv_w', 'm_ssm_conv_b', 'm_dt_bias', 'm_a_log', 'm_d_skip', 'm_ssm_norm', 'm_q_norm', 'm_w_uq', 'm_kv_norm', 'm_w_ukv', 'm_attn_out_norm', 'm_w_out', 'm_norm_mem_q', 'm_norm_mem_kv', 'm_w_mq', 'm_w_mk', 'm_w_mv', 'm_w_mo', 'm_norm_ffn', 'm_w_up', 'm_ffn_conv_w', 'm_ffn_conv_b', 'm_w_down', 'm_final_norm', 'v_norm_mix', 'v_w_in', 'v_ssm_conv_w', 'v_ssm_conv_b', 'v_dt_bias', 'v_a_log', 'v_d_skip', 'v_ssm_norm', 'v_q_norm', 'v_w_uq', 'v_kv_norm', 'v_w_ukv', 'v_attn_out_norm', 'v_w_out', 'v_norm_mem_q', 'v_norm_mem_kv', 'v_w_mq', 'v_w_mk', 'v_w_mv', 'v_w_mo', 'v_norm_ffn', 'v_w_up', 'v_ffn_conv_w', 'v_ffn_conv_b', 'v_w_down', 'v_final_norm']
TWIN_OUTPUTS = ['loss', 'grad_x', 'grad_norm_mix', 'grad_w_in', 'grad_ssm_conv_w', 'grad_ssm_conv_b', 'grad_dt_bias', 'grad_a_log', 'grad_d_skip', 'grad_ssm_norm', 'grad_q_norm', 'grad_w_uq', 'grad_kv_norm', 'grad_w_ukv', 'grad_attn_out_norm', 'grad_w_out', 'grad_norm_mem_q', 'grad_norm_mem_kv', 'grad_w_mq', 'grad_w_mk', 'grad_w_mv', 'grad_w_mo', 'grad_norm_ffn', 'grad_w_up', 'grad_ffn_conv_w', 'grad_ffn_conv_b', 'grad_w_down', 'grad_final_norm', 'delta_norm_mix', 'delta_w_in', 'delta_ssm_conv_w', 'delta_ssm_conv_b', 'delta_dt_bias', 'delta_a_log', 'delta_d_skip', 'delta_ssm_norm', 'delta_q_norm', 'delta_w_uq', 'delta_kv_norm', 'delta_w_ukv', 'delta_attn_out_norm', 'delta_w_out', 'delta_norm_mem_q', 'delta_norm_mem_kv', 'delta_w_mq', 'delta_w_mk', 'delta_w_mv', 'delta_w_mo', 'delta_norm_ffn', 'delta_w_up', 'delta_ffn_conv_w', 'delta_ffn_conv_b', 'delta_w_down', 'delta_final_norm', 'new_m_norm_mix', 'new_m_w_in', 'new_m_ssm_conv_w', 'new_m_ssm_conv_b', 'new_m_dt_bias', 'new_m_a_log', 'new_m_d_skip', 'new_m_ssm_norm', 'new_m_q_norm', 'new_m_w_uq', 'new_m_kv_norm', 'new_m_w_ukv', 'new_m_attn_out_norm', 'new_m_w_out', 'new_m_norm_mem_q', 'new_m_norm_mem_kv', 'new_m_w_mq', 'new_m_w_mk', 'new_m_w_mv', 'new_m_w_mo', 'new_m_norm_ffn', 'new_m_w_up', 'new_m_ffn_conv_w', 'new_m_ffn_conv_b', 'new_m_w_down', 'new_m_final_norm', 'new_v_norm_mix', 'new_v_w_in', 'new_v_ssm_conv_w', 'new_v_ssm_conv_b', 'new_v_dt_bias', 'new_v_a_log', 'new_v_d_skip', 'new_v_ssm_norm', 'new_v_q_norm', 'new_v_w_uq', 'new_v_kv_norm', 'new_v_w_ukv', 'new_v_attn_out_norm', 'new_v_w_out', 'new_v_norm_mem_q', 'new_v_norm_mem_kv', 'new_v_w_mq', 'new_v_w_mk', 'new_v_w_mv', 'new_v_w_mo', 'new_v_norm_ffn', 'new_v_w_up', 'new_v_ffn_conv_w', 'new_v_ffn_conv_b', 'new_v_w_down', 'new_v_final_norm']
TWIN_LEAF_KINDS = {'loss': 'loss', 'grad_x': 'grad_x', 'grad_norm_mix': 'grad_w', 'grad_w_in': 'grad_w', 'grad_ssm_conv_w': 'grad_w', 'grad_ssm_conv_b': 'grad_w', 'grad_dt_bias': 'grad_w', 'grad_a_log': 'grad_w', 'grad_d_skip': 'grad_w', 'grad_ssm_norm': 'grad_w', 'grad_q_norm': 'grad_w', 'grad_w_uq': 'grad_w', 'grad_kv_norm': 'grad_w', 'grad_w_ukv': 'grad_w', 'grad_attn_out_norm': 'grad_w', 'grad_w_out': 'grad_w', 'grad_norm_mem_q': 'grad_w', 'grad_norm_mem_kv': 'grad_w', 'grad_w_mq': 'grad_w', 'grad_w_mk': 'grad_w', 'grad_w_mv': 'grad_w', 'grad_w_mo': 'grad_w', 'grad_norm_ffn': 'grad_w', 'grad_w_up': 'grad_w', 'grad_ffn_conv_w': 'grad_w', 'grad_ffn_conv_b': 'grad_w', 'grad_w_down': 'grad_w', 'grad_final_norm': 'grad_w', 'delta_norm_mix': 'delta_w', 'delta_w_in': 'delta_w', 'delta_ssm_conv_w': 'delta_w', 'delta_ssm_conv_b': 'delta_w', 'delta_dt_bias': 'delta_w', 'delta_a_log': 'delta_w', 'delta_d_skip': 'delta_w', 'delta_ssm_norm': 'delta_w', 'delta_q_norm': 'delta_w', 'delta_w_uq': 'delta_w', 'delta_kv_norm': 'delta_w', 'delta_w_ukv': 'delta_w', 'delta_attn_out_norm': 'delta_w', 'delta_w_out': 'delta_w', 'delta_norm_mem_q': 'delta_w', 'delta_norm_mem_kv': 'delta_w', 'delta_w_mq': 'delta_w', 'delta_w_mk': 'delta_w', 'delta_w_mv': 'delta_w', 'delta_w_mo': 'delta_w', 'delta_norm_ffn': 'delta_w', 'delta_w_up': 'delta_w', 'delta_ffn_conv_w': 'delta_w', 'delta_ffn_conv_b': 'delta_w', 'delta_w_down': 'delta_w', 'delta_final_norm': 'delta_w', 'new_m_norm_mix': 'new_m', 'new_m_w_in': 'new_m', 'new_m_ssm_conv_w': 'new_m', 'new_m_ssm_conv_b': 'new_m', 'new_m_dt_bias': 'new_m', 'new_m_a_log': 'new_m', 'new_m_d_skip': 'new_m', 'new_m_ssm_norm': 'new_m', 'new_m_q_norm': 'new_m', 'new_m_w_uq': 'new_m', 'new_m_kv_norm': 'new_m', 'new_m_w_ukv': 'new_m', 'new_m_attn_out_norm': 'new_m', 'new_m_w_out': 'new_m', 'new_m_norm_mem_q': 'new_m', 'new_m_norm_mem_kv': 'new_m', 'new_m_w_mq': 'new_m', 'new_m_w_mk': 'new_m', 'new_m_w_mv': 'new_m', 'new_m_w_mo': 'new_m', 'new_m_norm_ffn': 'new_m', 'new_m_w_up': 'new_m', 'new_m_ffn_conv_w': 'new_m', 'new_m_ffn_conv_b': 'new_m', 'new_m_w_down': 'new_m', 'new_m_final_norm': 'new_m', 'new_v_norm_mix': 'new_v', 'new_v_w_in': 'new_v', 'new_v_ssm_conv_w': 'new_v', 'new_v_ssm_conv_b': 'new_v', 'new_v_dt_bias': 'new_v', 'new_v_a_log': 'new_v', 'new_v_d_skip': 'new_v', 'new_v_ssm_norm': 'new_v', 'new_v_q_norm': 'new_v', 'new_v_w_uq': 'new_v', 'new_v_kv_norm': 'new_v', 'new_v_w_ukv': 'new_v', 'new_v_attn_out_norm': 'new_v', 'new_v_w_out': 'new_v', 'new_v_norm_mem_q': 'new_v', 'new_v_norm_mem_kv': 'new_v', 'new_v_w_mq': 'new_v', 'new_v_w_mk': 'new_v', 'new_v_w_mv': 'new_v', 'new_v_w_mo': 'new_v', 'new_v_norm_ffn': 'new_v', 'new_v_w_up': 'new_v', 'new_v_ffn_conv_w': 'new_v', 'new_v_ffn_conv_b': 'new_v', 'new_v_w_down': 'new_v', 'new_v_final_norm': 'new_v'}


def _forward(args):
    return _fwd_reference(*[args[k] for k in FWD_PARAMS])


def _output_shape():
    out = _jax.eval_shape(lambda: _forward(_fwd_setup_inputs(0)))
    return out.shape, out.dtype

N_MICROBATCH = 1
ADAM_LR = 0.001
ADAM_B1 = 0.9
ADAM_B2 = 0.999
ADAM_EPS = 1e-08
ADAM_WD = 0.01
ADAM_STEP = 10
PER_EXAMPLE_BATCH_AXIS = {'x': 0, 'mem': 0, 'positions': 0, 'loss_target': 0}
SHARED_INPUTS = []
_WEIGHT_DTYPES = {'norm_mix': _jnp.float32, 'w_in': _jnp.float32, 'ssm_conv_w': _jnp.float32, 'ssm_conv_b': _jnp.float32, 'dt_bias': _jnp.float32, 'a_log': _jnp.float32, 'd_skip': _jnp.float32, 'ssm_norm': _jnp.float32, 'q_norm': _jnp.float32, 'w_uq': _jnp.float32, 'kv_norm': _jnp.float32, 'w_ukv': _jnp.float32, 'attn_out_norm': _jnp.float32, 'w_out': _jnp.float32, 'norm_mem_q': _jnp.float32, 'norm_mem_kv': _jnp.float32, 'w_mq': _jnp.float32, 'w_mk': _jnp.float32, 'w_mv': _jnp.float32, 'w_mo': _jnp.float32, 'norm_ffn': _jnp.float32, 'w_up': _jnp.float32, 'ffn_conv_w': _jnp.float32, 'ffn_conv_b': _jnp.float32, 'w_down': _jnp.float32, 'final_norm': _jnp.float32}
MOMENT_SCALE = {'norm_mix': 1.330712e-01, 'w_in': 6.914014e-02, 'ssm_conv_w': 4.659167e-02, 'ssm_conv_b': 7.518202e-02, 'dt_bias': 1.773518e-01, 'a_log': 3.177557e-01, 'd_skip': 2.425020e-01, 'ssm_norm': 6.375649e-02, 'q_norm': 7.143357e-02, 'w_uq': 3.635017e-02, 'kv_norm': 2.031936e-01, 'w_ukv': 6.620084e-02, 'attn_out_norm': 8.267042e-02, 'w_out': 1.068867e-01, 'norm_mem_q': 9.221829e-03, 'norm_mem_kv': 1.761679e-02, 'w_mq': 8.991719e-03, 'w_mk': 9.005074e-03, 'w_mv': 1.478240e-02, 'w_mo': 1.447432e-02, 'norm_ffn': 6.535609e-02, 'w_up': 2.799377e-02, 'ffn_conv_w': 2.828507e-02, 'ffn_conv_b': 3.599710e-02, 'w_down': 4.611487e-02, 'final_norm': 1.622203e+01}


def _to_microbatches(a, axis):
    t = _jnp.moveaxis(a, axis, 0)
    t = t.reshape((N_MICROBATCH, t.shape[0] // N_MICROBATCH) + t.shape[1:])
    return _jnp.moveaxis(t, 1, axis + 1)


def setup_inputs(seed: int = 0) -> dict:
    inp = _fwd_setup_inputs(seed)
    key = _jax.random.fold_in(_jax.random.key(seed), 7919)
    shape, _ = _output_shape()
    out = dict(inp)
    out["loss_target"] = _jax.random.normal(_jax.random.fold_in(key, 0), shape, _jnp.float32)
    for i, name in enumerate(TWIN_WEIGHTS):
        w = inp[name].astype(_jnp.float32)
        if MOMENT_SCALE is None:
            s = _jnp.sqrt(_jnp.mean(_jnp.square(w)) + 1e-30)
        else:
            s = MOMENT_SCALE[name]
        km, kv = _jax.random.split(_jax.random.fold_in(key, i + 1))
        out[name] = w
        out["m_" + name] = s * _jax.random.normal(km, w.shape, _jnp.float32)
        out["v_" + name] = (s * s) * _jax.random.uniform(kv, w.shape, _jnp.float32, 0.5, 1.5)
    if N_MICROBATCH > 1:
        for name, axis in PER_EXAMPLE_BATCH_AXIS.items():
            out[name] = _to_microbatches(out[name], axis)
    return {'x': out['x'], 'mem': out['mem'], 'positions': out['positions'], 'norm_mix': out['norm_mix'], 'w_in': out['w_in'], 'ssm_conv_w': out['ssm_conv_w'], 'ssm_conv_b': out['ssm_conv_b'], 'dt_bias': out['dt_bias'], 'a_log': out['a_log'], 'd_skip': out['d_skip'], 'ssm_norm': out['ssm_norm'], 'q_norm': out['q_norm'], 'w_uq': out['w_uq'], 'kv_norm': out['kv_norm'], 'w_ukv': out['w_ukv'], 'attn_out_norm': out['attn_out_norm'], 'w_out': out['w_out'], 'norm_mem_q': out['norm_mem_q'], 'norm_mem_kv': out['norm_mem_kv'], 'w_mq': out['w_mq'], 'w_mk': out['w_mk'], 'w_mv': out['w_mv'], 'w_mo': out['w_mo'], 'norm_ffn': out['norm_ffn'], 'w_up': out['w_up'], 'ffn_conv_w': out['ffn_conv_w'], 'ffn_conv_b': out['ffn_conv_b'], 'w_down': out['w_down'], 'final_norm': out['final_norm'], 'loss_target': out['loss_target'], 'm_norm_mix': out['m_norm_mix'], 'm_w_in': out['m_w_in'], 'm_ssm_conv_w': out['m_ssm_conv_w'], 'm_ssm_conv_b': out['m_ssm_conv_b'], 'm_dt_bias': out['m_dt_bias'], 'm_a_log': out['m_a_log'], 'm_d_skip': out['m_d_skip'], 'm_ssm_norm': out['m_ssm_norm'], 'm_q_norm': out['m_q_norm'], 'm_w_uq': out['m_w_uq'], 'm_kv_norm': out['m_kv_norm'], 'm_w_ukv': out['m_w_ukv'], 'm_attn_out_norm': out['m_attn_out_norm'], 'm_w_out': out['m_w_out'], 'm_norm_mem_q': out['m_norm_mem_q'], 'm_norm_mem_kv': out['m_norm_mem_kv'], 'm_w_mq': out['m_w_mq'], 'm_w_mk': out['m_w_mk'], 'm_w_mv': out['m_w_mv'], 'm_w_mo': out['m_w_mo'], 'm_norm_ffn': out['m_norm_ffn'], 'm_w_up': out['m_w_up'], 'm_ffn_conv_w': out['m_ffn_conv_w'], 'm_ffn_conv_b': out['m_ffn_conv_b'], 'm_w_down': out['m_w_down'], 'm_final_norm': out['m_final_norm'], 'v_norm_mix': out['v_norm_mix'], 'v_w_in': out['v_w_in'], 'v_ssm_conv_w': out['v_ssm_conv_w'], 'v_ssm_conv_b': out['v_ssm_conv_b'], 'v_dt_bias': out['v_dt_bias'], 'v_a_log': out['v_a_log'], 'v_d_skip': out['v_d_skip'], 'v_ssm_norm': out['v_ssm_norm'], 'v_q_norm': out['v_q_norm'], 'v_w_uq': out['v_w_uq'], 'v_kv_norm': out['v_kv_norm'], 'v_w_ukv': out['v_w_ukv'], 'v_attn_out_norm': out['v_attn_out_norm'], 'v_w_out': out['v_w_out'], 'v_norm_mem_q': out['v_norm_mem_q'], 'v_norm_mem_kv': out['v_norm_mem_kv'], 'v_w_mq': out['v_w_mq'], 'v_w_mk': out['v_w_mk'], 'v_w_mv': out['v_w_mv'], 'v_w_mo': out['v_w_mo'], 'v_norm_ffn': out['v_norm_ffn'], 'v_w_up': out['v_w_up'], 'v_ffn_conv_w': out['v_ffn_conv_w'], 'v_ffn_conv_b': out['v_ffn_conv_b'], 'v_w_down': out['v_w_down'], 'v_final_norm': out['v_final_norm']}


def _loss(weights, diff, rest, loss_target):
    with _jax.named_scope("forward"):
        args = {**rest, TWIN_DIFF_INPUT: diff, **{k: w.astype(_WEIGHT_DTYPES[k]) for k, w in weights.items()}}
        y = _forward(args)
    with _jax.named_scope("loss_head"):
        err = _jnp.square(y.astype(_jnp.float32) - loss_target)
        return 0.5 * _jnp.sum(_jnp.mean(err, axis=-1)) if err.ndim else 0.5 * err


def _adamw(w, g, m, v):
    m = ADAM_B1 * m + (1.0 - ADAM_B1) * g
    v = ADAM_B2 * v + (1.0 - ADAM_B2) * _jnp.square(g)
    m_hat = m / (1.0 - ADAM_B1 ** ADAM_STEP)
    v_hat = v / (1.0 - ADAM_B2 ** ADAM_STEP)
    delta = -ADAM_LR * (m_hat / (_jnp.sqrt(v_hat) + ADAM_EPS) + ADAM_WD * w)
    return delta, m, v


def reference(x, mem, positions, norm_mix, w_in, ssm_conv_w, ssm_conv_b, dt_bias, a_log, d_skip, ssm_norm, q_norm, w_uq, kv_norm, w_ukv, attn_out_norm, w_out, norm_mem_q, norm_mem_kv, w_mq, w_mk, w_mv, w_mo, norm_ffn, w_up, ffn_conv_w, ffn_conv_b, w_down, final_norm, loss_target, m_norm_mix, m_w_in, m_ssm_conv_w, m_ssm_conv_b, m_dt_bias, m_a_log, m_d_skip, m_ssm_norm, m_q_norm, m_w_uq, m_kv_norm, m_w_ukv, m_attn_out_norm, m_w_out, m_norm_mem_q, m_norm_mem_kv, m_w_mq, m_w_mk, m_w_mv, m_w_mo, m_norm_ffn, m_w_up, m_ffn_conv_w, m_ffn_conv_b, m_w_down, m_final_norm, v_norm_mix, v_w_in, v_ssm_conv_w, v_ssm_conv_b, v_dt_bias, v_a_log, v_d_skip, v_ssm_norm, v_q_norm, v_w_uq, v_kv_norm, v_w_ukv, v_attn_out_norm, v_w_out, v_norm_mem_q, v_norm_mem_kv, v_w_mq, v_w_mk, v_w_mv, v_w_mo, v_norm_ffn, v_w_up, v_ffn_conv_w, v_ffn_conv_b, v_w_down, v_final_norm):
    given = dict(x=x, mem=mem, positions=positions, norm_mix=norm_mix, w_in=w_in, ssm_conv_w=ssm_conv_w, ssm_conv_b=ssm_conv_b, dt_bias=dt_bias, a_log=a_log, d_skip=d_skip, ssm_norm=ssm_norm, q_norm=q_norm, w_uq=w_uq, kv_norm=kv_norm, w_ukv=w_ukv, attn_out_norm=attn_out_norm, w_out=w_out, norm_mem_q=norm_mem_q, norm_mem_kv=norm_mem_kv, w_mq=w_mq, w_mk=w_mk, w_mv=w_mv, w_mo=w_mo, norm_ffn=norm_ffn, w_up=w_up, ffn_conv_w=ffn_conv_w, ffn_conv_b=ffn_conv_b, w_down=w_down, final_norm=final_norm, loss_target=loss_target, m_norm_mix=m_norm_mix, m_w_in=m_w_in, m_ssm_conv_w=m_ssm_conv_w, m_ssm_conv_b=m_ssm_conv_b, m_dt_bias=m_dt_bias, m_a_log=m_a_log, m_d_skip=m_d_skip, m_ssm_norm=m_ssm_norm, m_q_norm=m_q_norm, m_w_uq=m_w_uq, m_kv_norm=m_kv_norm, m_w_ukv=m_w_ukv, m_attn_out_norm=m_attn_out_norm, m_w_out=m_w_out, m_norm_mem_q=m_norm_mem_q, m_norm_mem_kv=m_norm_mem_kv, m_w_mq=m_w_mq, m_w_mk=m_w_mk, m_w_mv=m_w_mv, m_w_mo=m_w_mo, m_norm_ffn=m_norm_ffn, m_w_up=m_w_up, m_ffn_conv_w=m_ffn_conv_w, m_ffn_conv_b=m_ffn_conv_b, m_w_down=m_w_down, m_final_norm=m_final_norm, v_norm_mix=v_norm_mix, v_w_in=v_w_in, v_ssm_conv_w=v_ssm_conv_w, v_ssm_conv_b=v_ssm_conv_b, v_dt_bias=v_dt_bias, v_a_log=v_a_log, v_d_skip=v_d_skip, v_ssm_norm=v_ssm_norm, v_q_norm=v_q_norm, v_w_uq=v_w_uq, v_kv_norm=v_kv_norm, v_w_ukv=v_w_ukv, v_attn_out_norm=v_attn_out_norm, v_w_out=v_w_out, v_norm_mem_q=v_norm_mem_q, v_norm_mem_kv=v_norm_mem_kv, v_w_mq=v_w_mq, v_w_mk=v_w_mk, v_w_mv=v_w_mv, v_w_mo=v_w_mo, v_norm_ffn=v_norm_ffn, v_w_up=v_w_up, v_ffn_conv_w=v_ffn_conv_w, v_ffn_conv_b=v_ffn_conv_b, v_w_down=v_w_down, v_final_norm=v_final_norm)
    weights = {n: given[n] for n in TWIN_WEIGHTS}
    shared = {n: given[n] for n in SHARED_INPUTS}
    per_example = {n: given[n] for n in ['x', 'mem', 'positions']}
    grad_fn = _jax.value_and_grad(_loss, argnums=(0, 1))

    def one_microbatch(ex, loss_target):
        ex = dict(ex)
        diff = ex.pop(TWIN_DIFF_INPUT)
        return grad_fn(weights, diff, {**shared, **ex}, loss_target)

    if N_MICROBATCH == 1:
        loss, (grad_w, grad_x) = one_microbatch(per_example, given["loss_target"])
    else:
        def body(carry, xs):
            loss_sum, grad_sum = carry
            l_k, (gw_k, gx_k) = one_microbatch(xs[0], xs[1])
            with _jax.named_scope("update"):
                return (loss_sum + l_k, _jax.tree.map(_jnp.add, grad_sum, gw_k)), gx_k

        init = (_jnp.zeros((), _jnp.float32), _jax.tree.map(_jnp.zeros_like, weights))
        (loss, grad_w), grad_x = _jax.lax.scan(body, init, (per_example, given["loss_target"]))
    with _jax.named_scope("update"):
        delta_w, new_m, new_v = {}, {}, {}
        for n in TWIN_WEIGHTS:
            delta_w[n], new_m[n], new_v[n] = _adamw(weights[n], grad_w[n], given["m_" + n], given["v_" + n])
    return (loss, grad_x, *[grad_w[n] for n in TWIN_WEIGHTS], *[delta_w[n] for n in TWIN_WEIGHTS],
            *[new_m[n] for n in TWIN_WEIGHTS], *[new_v[n] for n in TWIN_WEIGHTS])
```

```python
import functools
import math
from typing import Any, NamedTuple, Optional

import jax
import jax.numpy as jnp
from jax import lax
from jax.experimental import pallas as pl
from jax.experimental.pallas import tpu as pltpu

F32 = jnp.float32
BF16 = jnp.bfloat16

D_MODEL = 1024
DEPTH = 4
MEM_LEN = 256
EPS = 1e-6
SSM_HEADS = 16
SSM_HEAD_DIM = 64
D_SSM = 1024
SSM_GROUPS = 4
SSM_STATE = 128
SSM_CONV = 4
SSM_CHUNK = 128
CONV_CH = 2048
MLA_HEADS = 16
QK_NOPE = 64
QK_ROPE = 32
V_DIM = 64
Q_LORA = 384
KV_LORA = 256
ROPE_THETA = 10000.0
MEM_HEADS = 4
MEM_HEAD_DIM = 256
D_FF = 2816
FFN_CONV = 3
D_IN = 3760
D_MIX = 2048
ADAM_LR = 0.001
ADAM_B1 = 0.9
ADAM_B2 = 0.999
ADAM_EPS = 1e-08
ADAM_WD = 0.01
ADAM_STEP = 10

N_DEV = 8
N_CHIP = 4
LANES = 128
HEAD_PAD = 128
PROJ_W = 3840
OFF_Z, OFF_XBC, OFF_CQ, OFF_SMALL, OFF_CKV = 0, 1024, 3072, 3456, 3584
ROPE_LANE0 = 64
VMEM_LIMIT = 56 * 1024 * 1024

WEIGHT_NAMES = ['norm_mix', 'w_in', 'ssm_conv_w', 'ssm_conv_b', 'dt_bias', 'a_log', 'd_skip', 'ssm_norm', 'q_norm',
                'w_uq', 'kv_norm', 'w_ukv', 'attn_out_norm', 'w_out', 'norm_mem_q', 'norm_mem_kv', 'w_mq', 'w_mk',
                'w_mv', 'w_mo', 'norm_ffn', 'w_up', 'ffn_conv_w', 'ffn_conv_b', 'w_down', 'final_norm']
BIG = {'w_in': (1024, 470), 'w_uq': (384, 192), 'w_ukv': (256, 256), 'w_up': (1024, 704), 'w_out': (256, 1024),
       'w_mq': (128, 1024), 'w_mk': (128, 1024), 'w_mv': (128, 1024), 'w_mo': (128, 1024), 'w_down': (352, 1024)}
BIG_NAMES = list(BIG)
PROJ_SEGS = [(0, 1024, OFF_Z), (1024, 3072, OFF_XBC), (3072, 3088, OFF_SMALL), (3088, 3472, OFF_CQ),
             (3472, 3728, OFF_CKV), (3728, 3760, OFF_SMALL + ROPE_LANE0)]
SMALL_SEGS = [('norm_mix', 1024), ('ssm_norm', 1024), ('attn_out_norm', 1024), ('norm_mem_q', 1024),
              ('norm_mem_kv', 1024), ('norm_ffn', 1024), ('q_norm', 384), ('kv_norm', 256), ('ssm_conv_b', 2048),
              ('ffn_conv_b', 5632), ('dt_bias', 128), ('a_log', 128), ('d_skip', 128),
              ('ssm_conv_w', SSM_CONV * CONV_CH), ('ffn_conv_w', FFN_CONV * 2 * D_FF), ('final_norm', 1024)]
SMALL_OFF = {}
_o = 0
for _n, _w in SMALL_SEGS:
    SMALL_OFF[_n] = _o
    _o += _w
SMALL_W = _o


def _params(**kw):
    return pltpu.CompilerParams(vmem_limit_bytes=VMEM_LIMIT, **kw)


def _pick(n, cap):
    if n <= cap:
        return n
    best = None
    for t in range(LANES, cap + 1, LANES):
        if n % t == 0:
            best = t
    assert best is not None, (n, cap)
    return best


def _row_tile(a, cap=256):
    if a <= cap:
        return a
    best = None
    for t in range(16, cap + 1, 16):
        if a % t == 0:
            best = t
    assert best is not None, (a, cap)
    return best


class Opnd(NamedTuple):
    arr: Any
    lead: Optional[int] = None
    r0: int = 0
    c0: int = 0
    shape: Optional[tuple] = None


def _opnd(x):
    return x if isinstance(x, Opnd) else Opnd(x)


def _lshape(o):
    return tuple(o.shape) if o.shape is not None else tuple(o.arr.shape[-2:])


def _spec(o, br, bc, bi, bj):
    rr, cc = _lshape(o)
    assert rr % br == 0 and cc % bc == 0, (rr, cc, br, bc)
    ro, co = o.r0 * (rr // br), o.c0 * (cc // bc)
    if o.lead is None:
        return pl.BlockSpec((br, bc), lambda i, j: (ro + bi(i, j), co + bj(i, j)))
    return pl.BlockSpec((None, br, bc), lambda i, j: (o.lead, ro + bi(i, j), co + bj(i, j)))


_DIMS = {'nn': (((1,), (0,)), ((), ())), 'nt': (((1,), (1,)), ((), ())), 'tn': (((0,), (0,)), ((), ()))}
_ROW = lambda i, j: i
_COL = lambda i, j: j
_ZERO = lambda i, j: 0


def matmul(pairs, mode, out_dtype, name, add=None):
    pairs = [(_opnd(a), _opnd(b)) for a, b in pairs]
    a0, b0 = pairs[0]
    if mode == 'nn':
        m, n = _lshape(a0)[0], _lshape(b0)[1]
    elif mode == 'nt':
        m, n = _lshape(a0)[0], _lshape(b0)[0]
    else:
        m, n = _lshape(a0)[1], _lshape(b0)[1]
    kmax = max(_lshape(a)[0] if mode == 'tn' else _lshape(a)[1] for a, _ in pairs)
    if mode == 'tn':
        tm, tn = _pick(m, 512), _pick(n, 512)
    else:
        tm = _pick(m, 512)
        tn = _pick(n, max(LANES, (3 * 1024 * 1024 // (2 * kmax)) // LANES * LANES))
    npairs = len(pairs)

    def body(*refs):
        o_ref = refs[-1]
        acc = None
        for p in range(npairs):
            a = refs[2 * p][...].astype(BF16)
            b = refs[2 * p + 1][...].astype(BF16)
            d = lax.dot_general(a, b, _DIMS[mode], preferred_element_type=F32)
            acc = d if acc is None else acc + d
        if add is not None:
            acc = acc + refs[2 * npairs][...].astype(F32)
        o_ref[...] = acc.astype(out_dtype)

    in_specs, args = [], []
    for a, b in pairs:
        if mode == 'nn':
            k = _lshape(a)[1]
            in_specs += [_spec(a, tm, k, _ROW, _ZERO), _spec(b, k, tn, _ZERO, _COL)]
        elif mode == 'nt':
            k = _lshape(a)[1]
            in_specs += [_spec(a, tm, k, _ROW, _ZERO), _spec(b, tn, k, _COL, _ZERO)]
        else:
            k = _lshape(a)[0]
            in_specs += [_spec(a, k, tm, _ZERO, _ROW), _spec(b, k, tn, _ZERO, _COL)]
        args += [a.arr, b.arr]
    if add is not None:
        in_specs.append(pl.BlockSpec((tm, tn), lambda i, j: (i, j)))
        args.append(add)
    return pl.pallas_call(
        body, name=name, grid=(m // tm, n // tn), in_specs=in_specs,
        out_specs=pl.BlockSpec((tm, tn), lambda i, j: (i, j)),
        out_shape=jax.ShapeDtypeStruct((m, n), out_dtype),
        compiler_params=_params(dimension_semantics=("arbitrary", "arbitrary")),
    )(*args)


def rowwise(fn, rows, fulls, outs, accs, name, tm=256, into=None):
    s = rows[0][0].shape[0]
    nrow, nfull, nout, nacc = len(rows), len(fulls), len(outs), len(accs)
    nin = nrow + nfull

    def body(*refs):
        ins = [r[...] for r in refs[:nin]]
        res = fn(*ins)
        if not isinstance(res, (tuple, list)):
            res = (res,)
        orefs = refs[nin + (1 if into is not None else 0):]
        for k in range(nout):
            orefs[k][...] = res[k].astype(orefs[k].dtype)
        if nacc:
            @pl.when(pl.program_id(0) == 0)
            def _():
                for k in range(nacc):
                    orefs[nout + k][...] = jnp.zeros_like(orefs[nout + k])

            for k in range(nacc):
                orefs[nout + k][...] += res[nout + k].astype(orefs[nout + k].dtype)

    in_specs = [pl.BlockSpec((tm, w), lambda i, cb=cb: (i, cb)) for _, w, cb in rows]
    in_specs += [pl.BlockSpec((None,) + f.shape[1:], lambda i, ld=ld, nd=f.ndim - 1: (ld,) + (0,) * nd) for f, ld in fulls]
    args = [r[0] for r in rows] + [f for f, _ in fulls]
    aliases = {}
    if into is not None:
        in_specs.append(pl.BlockSpec(memory_space=pl.ANY))
        args.append(into[0])
        aliases = {nin: into[1]}
    out_specs, out_shape = [], []
    for o in outs:
        w, dt = o[0], o[1]
        total, cb = (o[2], o[3]) if len(o) == 4 else (w, 0)
        out_specs.append(pl.BlockSpec((tm, w), lambda i, cb=cb: (i, cb)))
        out_shape.append(jax.ShapeDtypeStruct((s, total), dt))
    for shp, dt in accs:
        out_specs.append(pl.BlockSpec(shp, lambda i, nd=len(shp): (0,) * nd))
        out_shape.append(jax.ShapeDtypeStruct(shp, dt))
    return pl.pallas_call(
        body, name=name, grid=(s // tm,), in_specs=in_specs, out_specs=out_specs, out_shape=out_shape,
        input_output_aliases=aliases, compiler_params=_params(dimension_semantics=("arbitrary",)),
    )(*args)


def _rms(x, g):
    xf = x.astype(F32)
    var = jnp.mean(xf * xf, axis=-1, keepdims=True)
    return xf * lax.rsqrt(var + EPS) * g


def rmsnorm_fwd(x, g, name, width=None, colblock=0, out=None, into=None):
    w = width or x.shape[1]
    return rowwise(lambda xt, gt: _rms(xt, gt), [(x, w, colblock)], [g], [out or (w, BF16)], [], name, into=into)[0]


def rmsnorm_bwd(x, g, dh, name, resid=None, width=None, colblock=0, dh_colblock=0, dx_dtype=F32):
    w = width or x.shape[1]

    def fn(xt, dht, *rest):
        gt = rest[-1]
        _, vjp = jax.vjp(_rms, xt.astype(F32), gt)
        dx, dg = vjp(dht.astype(F32))
        if resid is not None:
            dx = dx + rest[0]
        return dx, dg

    rows = [(x, w, colblock), (dh, w, dh_colblock)] + ([(resid, w, 0)] if resid is not None else [])
    return rowwise(fn, rows, [g], [(w, dx_dtype)], [((1, w), F32)], name)


def _shift_down(u, k, row):
    return jnp.where(row >= k, pltpu.roll(u, k, 0), 0.0)


def _shift_up(u, k, row):
    s = u.shape[0]
    return jnp.where(row < s - k, pltpu.roll(u, s - k, 0), 0.0)


def _conv_fwd_tile(u, w_ref, b_ref, kw):
    row = lax.broadcasted_iota(jnp.int32, u.shape, 0)
    y = u * w_ref[kw - 1:kw, :] + b_ref[...]
    for k in range(1, kw):
        y = y + _shift_down(u, k, row) * w_ref[kw - 1 - k:kw - k, :]
    return y


def _conv_bwd_tile(u, dpre, w_ref, dw_ref, db_ref, kw):
    row = lax.broadcasted_iota(jnp.int32, u.shape, 0)
    du = dpre * w_ref[kw - 1:kw, :]
    dw_ref[kw - 1:kw, :] = jnp.sum(dpre * u, axis=0, keepdims=True)
    for k in range(1, kw):
        du = du + _shift_up(dpre, k, row) * w_ref[kw - 1 - k:kw - k, :]
        dw_ref[kw - 1 - k:kw - k, :] = jnp.sum(dpre * _shift_down(u, k, row), axis=0, keepdims=True)
    db_ref[...] = jnp.sum(dpre, axis=0, keepdims=True)
    return du


def _silu(x):
    return x * jax.nn.sigmoid(x)


def _dsilu(x):
    s = jax.nn.sigmoid(x)
    return s * (1.0 + x * (1.0 - s))


SSM_TC = 256


def ssm_conv_fwd(proj, cw, cb, l):
    s = proj.shape[0]
    off = OFF_XBC // SSM_TC

    def body(u_ref, w_ref, b_ref, o_ref):
        o_ref[...] = _silu(_conv_fwd_tile(u_ref[...], w_ref, b_ref, SSM_CONV))

    return pl.pallas_call(
        body, name="ssm_conv_fwd", grid=(CONV_CH // SSM_TC,),
        in_specs=[pl.BlockSpec((s, SSM_TC), lambda j: (0, off + j)),
                  pl.BlockSpec((None, SSM_CONV, SSM_TC), lambda j: (l, 0, j)),
                  pl.BlockSpec((None, 1, SSM_TC), lambda j: (l, 0, j))],
        out_specs=pl.BlockSpec((s, SSM_TC), lambda j: (0, j)),
        out_shape=jax.ShapeDtypeStruct((s, CONV_CH), F32),
        compiler_params=_params(dimension_semantics=("arbitrary",)),
    )(proj, cw, cb)


def ssm_conv_bwd(proj, cw, cb, l, dact):
    s = proj.shape[0]
    off = OFF_XBC // SSM_TC

    def body(u_ref, w_ref, b_ref, d_ref, du_ref, dw_ref, db_ref):
        u = u_ref[...]
        pre = _conv_fwd_tile(u, w_ref, b_ref, SSM_CONV)
        dpre = d_ref[...] * _dsilu(pre)
        du_ref[...] = _conv_bwd_tile(u, dpre, w_ref, dw_ref, db_ref, SSM_CONV).astype(du_ref.dtype)

    return pl.pallas_call(
        body, name="ssm_conv_bwd", grid=(CONV_CH // SSM_TC,),
        in_specs=[pl.BlockSpec((s, SSM_TC), lambda j: (0, off + j)),
                  pl.BlockSpec((None, SSM_CONV, SSM_TC), lambda j: (l, 0, j)),
                  pl.BlockSpec((None, 1, SSM_TC), lambda j: (l, 0, j)), pl.BlockSpec((s, SSM_TC), lambda j: (0, j))],
        out_specs=[pl.BlockSpec((s, SSM_TC), lambda j: (0, j)), pl.BlockSpec((SSM_CONV, SSM_TC), lambda j: (0, j)),
                   pl.BlockSpec((1, SSM_TC), lambda j: (0, j))],
        out_shape=[jax.ShapeDtypeStruct((s, CONV_CH), BF16), jax.ShapeDtypeStruct((SSM_CONV, CONV_CH), F32),
                   jax.ShapeDtypeStruct((1, CONV_CH), F32)],
        compiler_params=_params(dimension_semantics=("arbitrary",)),
    )(proj, cw, cb, dact)


FFN_TC = 256
FFN_NT = D_FF // FFN_TC


def _ffn_specs(s, l):
    blk = pl.BlockSpec((s, FFN_TC), lambda j: (0, j))
    wg = pl.BlockSpec((None, FFN_CONV, FFN_TC), lambda j: (l, 0, j))
    wv = pl.BlockSpec((None, FFN_CONV, FFN_TC), lambda j: (l, 0, FFN_NT + j))
    bg = pl.BlockSpec((None, 1, FFN_TC), lambda j: (l, 0, j))
    bv = pl.BlockSpec((None, 1, FFN_TC), lambda j: (l, 0, FFN_NT + j))
    return blk, wg, wv, bg, bv


def ffn_act_fwd(ug, uv, cw, cb, l):
    s = ug.shape[0]

    def body(g_ref, v_ref, wg_ref, wv_ref, bg_ref, bv_ref, o_ref):
        cg = _conv_fwd_tile(g_ref[...], wg_ref, bg_ref, FFN_CONV)
        cv = _conv_fwd_tile(v_ref[...], wv_ref, bv_ref, FFN_CONV)
        o_ref[...] = (_silu(cg) * cv).astype(o_ref.dtype)

    blk, wg, wv, bg, bv = _ffn_specs(s, l)
    return pl.pallas_call(
        body, name="ffn_act_fwd", grid=(FFN_NT,), in_specs=[blk, blk, wg, wv, bg, bv],
        out_specs=blk, out_shape=jax.ShapeDtypeStruct((s, D_FF), BF16),
        compiler_params=_params(dimension_semantics=("arbitrary",)),
    )(ug, uv, cw, cw, cb, cb)


def ffn_act_bwd(ug, uv, cw, cb, l, da):
    s = ug.shape[0]

    def body(g_ref, v_ref, wg_ref, wv_ref, bg_ref, bv_ref, da_ref, dg_ref, dv_ref, dwg_ref, dwv_ref, dbg_ref, dbv_ref):
        g, v = g_ref[...], v_ref[...]
        cg = _conv_fwd_tile(g, wg_ref, bg_ref, FFN_CONV)
        cv = _conv_fwd_tile(v, wv_ref, bv_ref, FFN_CONV)
        da_t = da_ref[...].astype(F32)
        dcg = da_t * cv * _dsilu(cg)
        dcv = da_t * _silu(cg)
        dg_ref[...] = _conv_bwd_tile(g, dcg, wg_ref, dwg_ref, dbg_ref, FFN_CONV).astype(dg_ref.dtype)
        dv_ref[...] = _conv_bwd_tile(v, dcv, wv_ref, dwv_ref, dbv_ref, FFN_CONV).astype(dv_ref.dtype)

    blk, wg, wv, bg, bv = _ffn_specs(s, l)
    wblk = pl.BlockSpec((FFN_CONV, FFN_TC), lambda j: (0, j))
    bblk = pl.BlockSpec((1, FFN_TC), lambda j: (0, j))
    return pl.pallas_call(
        body, name="ffn_act_bwd", grid=(FFN_NT,), in_specs=[blk, blk, wg, wv, bg, bv, blk],
        out_specs=[blk, blk, wblk, wblk, bblk, bblk],
        out_shape=[jax.ShapeDtypeStruct((s, D_FF), BF16), jax.ShapeDtypeStruct((s, D_FF), BF16),
                   jax.ShapeDtypeStruct((FFN_CONV, D_FF), F32), jax.ShapeDtypeStruct((FFN_CONV, D_FF), F32),
                   jax.ShapeDtypeStruct((1, D_FF), F32), jax.ShapeDtypeStruct((1, D_FF), F32)],
        compiler_params=_params(dimension_semantics=("arbitrary",)),
    )(ug, uv, cw, cw, cb, cb, da)


def _dot(a, b, mode):
    return lax.dot_general(a.astype(BF16), b.astype(BF16), _DIMS[mode], preferred_element_type=F32)


@jax.custom_vjp
def mm_nn(a, b):
    return _dot(a, b, 'nn')


@jax.custom_vjp
def mm_nt(a, b):
    return _dot(a, b, 'nt')


@jax.custom_vjp
def mm_tn(a, b):
    return _dot(a, b, 'tn')


mm_nn.defvjp(lambda a, b: (_dot(a, b, 'nn'), (a, b)), lambda r, g: (_dot(g, r[1], 'nt'), _dot(r[0], g, 'tn')))
mm_nt.defvjp(lambda a, b: (_dot(a, b, 'nt'), (a, b)), lambda r, g: (_dot(g, r[1], 'nn'), _dot(g, r[0], 'tn')))
mm_tn.defvjp(lambda a, b: (_dot(a, b, 'tn'), (a, b)), lambda r, g: (_dot(r[1], g, 'nt'), _dot(r[0], g, 'nn')))


def _tri(n, lower):
    r = lax.broadcasted_iota(jnp.int32, (n, n), 0)
    c = lax.broadcasted_iota(jnp.int32, (n, n), 1)
    return jnp.where((r >= c) if lower else (r <= c), 1.0, 0.0).astype(F32)


def _tri_dot(a, lower):
    return jnp.dot(_tri(a.shape[0], lower), a, precision=lax.Precision.HIGHEST, preferred_element_type=F32)


@jax.custom_vjp
def _cumsum_rows(a):
    return _tri_dot(a, True)


_cumsum_rows.defvjp(lambda a: (_tri_dot(a, True), None), lambda _, g: (_tri_dot(g, False),))


def _softplus(x):
    return jnp.maximum(x, 0.0) + jnp.log(1.0 + jnp.exp(-jnp.abs(x)))


def _ssd_chunk(xs, bs, cs, small, dtb, alog, dsk, prev):
    ln = small.shape[0]
    lane = lax.broadcasted_iota(jnp.int32, (ln, LANES), 1)
    lane1 = lax.broadcasted_iota(jnp.int32, (1, LANES), 1)
    sub = lax.broadcasted_iota(jnp.int32, (LANES, ln), 0)
    rowi = lax.broadcasted_iota(jnp.int32, (ln, LANES), 0)
    tril = lax.broadcasted_iota(jnp.int32, (ln, ln), 0) >= lax.broadcasted_iota(jnp.int32, (ln, ln), 1)
    first = lane < SSM_HEAD_DIM
    first1 = lane1 < SSM_HEAD_DIM

    dt = _softplus(small + dtb)
    acs = _cumsum_rows(dt * (-jnp.exp(alog)))
    acs_t = acs.T
    last = jnp.sum(jnp.where(rowi == ln - 1, acs, 0.0), axis=0, keepdims=True)

    def col(a, h):
        return jnp.sum(jnp.where(lane == h, a, 0.0), axis=1, keepdims=True)

    def one(a, h):
        return jnp.sum(jnp.where(lane1 == h, a, 0.0), axis=1, keepdims=True)

    def rowv(at, h):
        return jnp.sum(jnp.where(sub == h, at, 0.0), axis=0, keepdims=True)

    cb = [mm_nt(cs[g], bs[g]) for g in range(SSM_GROUPS)]
    ys, news = [], []
    for j in range(SSM_HEADS // 2):
        g = j // 2
        h0, h1 = 2 * j, 2 * j + 1
        xd = xs[j] * jnp.where(first, col(dt, h0), col(dt, h1))
        yd, st, ea, cd = None, None, [], []
        for h, xdh in ((h0, jnp.where(first, xd, 0.0)), (h1, jnp.where(first, 0.0, xd))):
            ac = col(acs, h)
            la = one(last, h)
            lmat = jnp.exp(jnp.where(tril, ac - rowv(acs_t, h), -jnp.inf))
            yh = mm_nn(cb[g] * lmat, xdh)
            sh = mm_tn(bs[g] * jnp.exp(la - ac), xdh)
            yd = yh if yd is None else yd + yh
            st = sh if st is None else st + sh
            ea.append(jnp.exp(ac))
            cd.append(jnp.exp(la))
        yoff = mm_nn(cs[g], prev[j]) * jnp.where(first, ea[0], ea[1])
        ys.append(yd + yoff + xs[j] * jnp.where(first1, one(dsk, h0), one(dsk, h1)))
        news.append(prev[j] * jnp.where(first1, cd[0], cd[1]) + st)
    return ys, news


N_PAIR = SSM_HEADS // 2


def ssd_fwd(xbc, proj, ptile, l):
    s = xbc.shape[0]
    nch = s // SSM_CHUNK

    def body(xbc_ref, small_ref, p_ref, y_ref, prev_ref, state_ref):
        @pl.when(pl.program_id(0) == 0)
        def _():
            state_ref[...] = jnp.zeros_like(state_ref)

        xs = [xbc_ref[:, LANES * j:LANES * (j + 1)] for j in range(N_PAIR)]
        bs = [xbc_ref[:, D_SSM + LANES * g:D_SSM + LANES * (g + 1)] for g in range(SSM_GROUPS)]
        cs = [xbc_ref[:, D_SSM + 512 + LANES * g:D_SSM + 512 + LANES * (g + 1)] for g in range(SSM_GROUPS)]
        prev = [state_ref[j] for j in range(N_PAIR)]
        ys, news = _ssd_chunk(xs, bs, cs, small_ref[...], p_ref[0:1, :], p_ref[1:2, :], p_ref[2:3, :], prev)
        for j in range(N_PAIR):
            y_ref[:, LANES * j:LANES * (j + 1)] = ys[j]
            prev_ref[0, j] = prev[j]
            state_ref[j] = news[j]

    return pl.pallas_call(
        body, name="ssd_fwd", grid=(nch,),
        in_specs=[pl.BlockSpec((SSM_CHUNK, CONV_CH), lambda c: (c, 0)),
                  pl.BlockSpec((SSM_CHUNK, LANES), lambda c: (c, OFF_SMALL // LANES)),
                  pl.BlockSpec((None, 8, LANES), lambda c: (l, 0, 0))],
        out_specs=[pl.BlockSpec((SSM_CHUNK, D_SSM), lambda c: (c, 0)),
                   pl.BlockSpec((1, N_PAIR, SSM_STATE, LANES), lambda c: (c, 0, 0, 0))],
        out_shape=[jax.ShapeDtypeStruct((s, D_SSM), F32), jax.ShapeDtypeStruct((nch, N_PAIR, SSM_STATE, LANES), F32)],
        scratch_shapes=[pltpu.VMEM((N_PAIR, SSM_STATE, LANES), F32)],
        compiler_params=_params(dimension_semantics=("arbitrary",)),
    )(xbc, proj, ptile)


def ssd_bwd(xbc, proj, ptile, l, prevs, dy):
    s = xbc.shape[0]
    nch = s // SSM_CHUNK

    def body(xbc_ref, small_ref, p_ref, prev_ref, dy_ref, dxbc_ref, dsmall_ref, dp_ref, dstate_ref):
        @pl.when(pl.program_id(0) == 0)
        def _():
            dstate_ref[...] = jnp.zeros_like(dstate_ref)
            dp_ref[...] = jnp.zeros_like(dp_ref)

        xs = [xbc_ref[:, LANES * j:LANES * (j + 1)] for j in range(N_PAIR)]
        bs = [xbc_ref[:, D_SSM + LANES * g:D_SSM + LANES * (g + 1)] for g in range(SSM_GROUPS)]
        cs = [xbc_ref[:, D_SSM + 512 + LANES * g:D_SSM + 512 + LANES * (g + 1)] for g in range(SSM_GROUPS)]
        prev = [prev_ref[0, j] for j in range(N_PAIR)]
        dys = [dy_ref[:, LANES * j:LANES * (j + 1)] for j in range(N_PAIR)]
        dnew = [dstate_ref[j] for j in range(N_PAIR)]
        _, vjp = jax.vjp(_ssd_chunk, xs, bs, cs, small_ref[...], p_ref[0:1, :], p_ref[1:2, :], p_ref[2:3, :], prev)
        dxs, dbs, dcs, dsmall, ddtb, dalog, ddsk, dprev = vjp((dys, dnew))
        for j in range(N_PAIR):
            dxbc_ref[:, LANES * j:LANES * (j + 1)] = dxs[j]
            dstate_ref[j] = dprev[j]
        for g in range(SSM_GROUPS):
            dxbc_ref[:, D_SSM + LANES * g:D_SSM + LANES * (g + 1)] = dbs[g]
            dxbc_ref[:, D_SSM + 512 + LANES * g:D_SSM + 512 + LANES * (g + 1)] = dcs[g]
        dsmall_ref[...] = dsmall
        dp_ref[0:1, :] += ddtb
        dp_ref[1:2, :] += dalog
        dp_ref[2:3, :] += ddsk

    rev = lambda c: nch - 1 - c
    return pl.pallas_call(
        body, name="ssd_bwd", grid=(nch,),
        in_specs=[pl.BlockSpec((SSM_CHUNK, CONV_CH), lambda c: (rev(c), 0)),
                  pl.BlockSpec((SSM_CHUNK, LANES), lambda c: (rev(c), OFF_SMALL // LANES)),
                  pl.BlockSpec((None, 8, LANES), lambda c: (l, 0, 0)),
                  pl.BlockSpec((1, N_PAIR, SSM_STATE, LANES), lambda c: (rev(c), 0, 0, 0)),
                  pl.BlockSpec((SSM_CHUNK, D_SSM), lambda c: (rev(c), 0))],
        out_specs=[pl.BlockSpec((SSM_CHUNK, CONV_CH), lambda c: (rev(c), 0)),
                   pl.BlockSpec((SSM_CHUNK, LANES), lambda c: (rev(c), 0)),
                   pl.BlockSpec((8, LANES), lambda c: (0, 0))],
        out_shape=[jax.ShapeDtypeStruct((s, CONV_CH), F32), jax.ShapeDtypeStruct((s, LANES), F32),
                   jax.ShapeDtypeStruct((8, LANES), F32)],
        scratch_shapes=[pltpu.VMEM((N_PAIR, SSM_STATE, LANES), F32)],
        compiler_params=_params(dimension_semantics=("arbitrary",)),
    )(xbc, proj, ptile, prevs, dy)


ROPE_TM = 256


def _rope_tile(t, cosm, sinm):
    lane = lax.broadcasted_iota(jnp.int32, t.shape, 1)
    half = QK_ROPE // 2
    partner = jnp.where(lane < ROPE_LANE0 + half, pltpu.roll(t, LANES - half, 1), pltpu.roll(t, half, 1))
    return t * cosm + partner * sinm


def _in_rope(shape):
    lane = lax.broadcasted_iota(jnp.int32, shape, 1)
    return jnp.logical_and(lane >= ROPE_LANE0, lane < ROPE_LANE0 + QK_ROPE)


def rope_q(q, cosm, sinm, name):
    s, w = q.shape

    def body(q_ref, c_ref, s_ref, o_ref):
        c, sn = c_ref[...], s_ref[...]
        for h in range(MLA_HEADS):
            sl = slice(HEAD_PAD * h, HEAD_PAD * (h + 1))
            o_ref[:, sl] = _rope_tile(q_ref[:, sl].astype(F32), c, sn).astype(o_ref.dtype)

    row = pl.BlockSpec((ROPE_TM, w), lambda i: (i, 0))
    tab = pl.BlockSpec((ROPE_TM, LANES), lambda i: (i, 0))
    return pl.pallas_call(
        body, name=name, grid=(s // ROPE_TM,), in_specs=[row, tab, tab], out_specs=row,
        out_shape=jax.ShapeDtypeStruct((s, w), BF16), compiler_params=_params(dimension_semantics=("arbitrary",)),
    )(q, cosm, sinm)


def build_k(kn, proj, cosm, sinm):
    s, w = kn.shape

    def body(k_ref, small_ref, c_ref, s_ref, o_ref):
        small = small_ref[...]
        inrope = _in_rope(small.shape)
        kpe = jnp.where(inrope, _rope_tile(jnp.where(inrope, small, 0.0), c_ref[...], s_ref[...]), 0.0)
        for h in range(MLA_HEADS):
            sl = slice(HEAD_PAD * h, HEAD_PAD * (h + 1))
            o_ref[:, sl] = (k_ref[:, sl].astype(F32) + kpe).astype(o_ref.dtype)

    row = pl.BlockSpec((ROPE_TM, w), lambda i: (i, 0))
    tab = pl.BlockSpec((ROPE_TM, LANES), lambda i: (i, 0))
    return pl.pallas_call(
        body, name="build_k", grid=(s // ROPE_TM,),
        in_specs=[row, pl.BlockSpec((ROPE_TM, LANES), lambda i: (i, OFF_SMALL // LANES)), tab, tab], out_specs=row,
        out_shape=jax.ShapeDtypeStruct((s, w), BF16), compiler_params=_params(dimension_semantics=("arbitrary",)),
    )(kn, proj, cosm, sinm)


def dsmall_bwd(dk, dsmall_ssd, cosm, sinm_neg):
    def fn(dkt, ds, c, sn):
        inrope = _in_rope(ds.shape)
        tot = dkt[:, 0:HEAD_PAD]
        for h in range(1, MLA_HEADS):
            tot = tot + dkt[:, HEAD_PAD * h:HEAD_PAD * (h + 1)]
        tot = jnp.where(inrope, tot, 0.0)
        return ds + jnp.where(inrope, _rope_tile(tot, c, sn), 0.0)

    return rowwise(fn, [(dk, MLA_HEADS * HEAD_PAD, 0), (dsmall_ssd, LANES, 0), (cosm, LANES, 0), (sinm_neg, LANES, 0)],
                   [], [(LANES, BF16)], [], "dsmall_bwd")[0]


ATT_TQ = 256
ATT_SCALE = (QK_NOPE + QK_ROPE) ** -0.5


def _att_probs(qh, kh, q0, normalize):
    s = lax.dot_general(qh, kh, _DIMS['nt'], preferred_element_type=F32) * ATT_SCALE
    r = lax.broadcasted_iota(jnp.int32, s.shape, 0) + q0
    c = lax.broadcasted_iota(jnp.int32, s.shape, 1)
    s = jnp.where(c <= r, s, -1e30)
    m = jnp.max(s, axis=1, keepdims=True)
    p = jnp.exp(s - m)
    l = jnp.sum(p, axis=1, keepdims=True)
    if normalize:
        return p / l, l
    return p, l


def mla_fwd(q, k, v):
    s = q.shape[0]

    def body(q_ref, k_ref, v_ref, o_ref):
        q0 = pl.program_id(1) * ATT_TQ
        v_t = v_ref[...]
        lane = lax.broadcasted_iota(jnp.int32, v_t.shape, 1)
        acc = None
        for h in range(2):
            sel = (lane < V_DIM) if h == 0 else (lane >= V_DIM)
            p, l = _att_probs(q_ref[:, HEAD_PAD * h:HEAD_PAD * (h + 1)], k_ref[:, HEAD_PAD * h:HEAD_PAD * (h + 1)], q0, False)
            vh = jnp.where(sel, v_t, jnp.zeros_like(v_t))
            oh = lax.dot_general(p.astype(BF16), vh, _DIMS['nn'], preferred_element_type=F32) / l
            acc = oh if acc is None else acc + oh
        o_ref[...] = acc

    return pl.pallas_call(
        body, name="mla_fwd", grid=(MLA_HEADS // 2, s // ATT_TQ),
        in_specs=[pl.BlockSpec((ATT_TQ, 2 * HEAD_PAD), lambda p, i: (i, p)),
                  pl.BlockSpec((s, 2 * HEAD_PAD), lambda p, i: (0, p)),
                  pl.BlockSpec((s, LANES), lambda p, i: (0, p))],
        out_specs=pl.BlockSpec((ATT_TQ, LANES), lambda p, i: (i, p)),
        out_shape=jax.ShapeDtypeStruct((s, MLA_HEADS * V_DIM), F32),
        compiler_params=_params(dimension_semantics=("arbitrary", "arbitrary")),
    )(q, k, v)


def mla_bwd(q, k, v, o, do):
    s = q.shape[0]

    def body(q_ref, k_ref, v_ref, o_ref, do_ref, dq_ref, dk_ref, dv_ref):
        @pl.when(pl.program_id(1) == 0)
        def _():
            dk_ref[...] = jnp.zeros_like(dk_ref)
            dv_ref[...] = jnp.zeros_like(dv_ref)

        q0 = pl.program_id(1) * ATT_TQ
        v_t = v_ref[...]
        o_t = o_ref[...]
        do_t = do_ref[...]
        lane = lax.broadcasted_iota(jnp.int32, do_t.shape, 1)
        for h in range(2):
            sel = (lane < V_DIM) if h == 0 else (lane >= V_DIM)
            qh = q_ref[:, HEAD_PAD * h:HEAD_PAD * (h + 1)]
            kh = k_ref[:, HEAD_PAD * h:HEAD_PAD * (h + 1)]
            p, _ = _att_probs(qh, kh, q0, True)
            doh = jnp.where(sel, do_t, 0.0)
            delta = jnp.sum(doh * o_t, axis=1, keepdims=True)
            doh_b = doh.astype(BF16)
            dv_ref[...] += lax.dot_general(p.astype(BF16), doh_b, _DIMS['tn'], preferred_element_type=F32)
            dp = lax.dot_general(doh_b, v_t, _DIMS['nt'], preferred_element_type=F32)
            ds = (p * (dp - delta) * ATT_SCALE).astype(BF16)
            dq_ref[:, HEAD_PAD * h:HEAD_PAD * (h + 1)] = lax.dot_general(
                ds, kh, _DIMS['nn'], preferred_element_type=F32).astype(dq_ref.dtype)
            dk_ref[:, HEAD_PAD * h:HEAD_PAD * (h + 1)] += lax.dot_general(ds, qh, _DIMS['tn'], preferred_element_type=F32)

    return pl.pallas_call(
        body, name="mla_bwd", grid=(MLA_HEADS // 2, s // ATT_TQ),
        in_specs=[pl.BlockSpec((ATT_TQ, 2 * HEAD_PAD), lambda p, i: (i, p)),
                  pl.BlockSpec((s, 2 * HEAD_PAD), lambda p, i: (0, p)),
                  pl.BlockSpec((s, LANES), lambda p, i: (0, p)),
                  pl.BlockSpec((ATT_TQ, LANES), lambda p, i: (i, p)),
                  pl.BlockSpec((ATT_TQ, LANES), lambda p, i: (i, p))],
        out_specs=[pl.BlockSpec((ATT_TQ, 2 * HEAD_PAD), lambda p, i: (i, p)),
                   pl.BlockSpec((s, 2 * HEAD_PAD), lambda p, i: (0, p)),
                   pl.BlockSpec((s, LANES), lambda p, i: (0, p))],
        out_shape=[jax.ShapeDtypeStruct((s, MLA_HEADS * HEAD_PAD), F32),
                   jax.ShapeDtypeStruct((s, MLA_HEADS * HEAD_PAD), F32),
                   jax.ShapeDtypeStruct((s, MLA_HEADS * V_DIM), F32)],
        compiler_params=_params(dimension_semantics=("arbitrary", "arbitrary")),
    )(q, k, v, o, do)


MEM_TQ = 256
MEM_SCALE = MEM_HEAD_DIM ** -0.5


def _mem_probs(qh, kh):
    s = lax.dot_general(qh, kh, _DIMS['nt'], preferred_element_type=F32) * MEM_SCALE
    p = jnp.exp(s - jnp.max(s, axis=1, keepdims=True))
    return p / jnp.sum(p, axis=1, keepdims=True)


def mem_fwd(q, k, v):
    s = q.shape[0]

    def body(q_ref, k_ref, v_ref, o_ref):
        for h in range(MEM_HEADS):
            sl = slice(MEM_HEAD_DIM * h, MEM_HEAD_DIM * (h + 1))
            p = _mem_probs(q_ref[:, sl], k_ref[:, sl])
            o_ref[:, sl] = lax.dot_general(p.astype(BF16), v_ref[:, sl], _DIMS['nn'],
                                           preferred_element_type=F32).astype(o_ref.dtype)

    full = pl.BlockSpec((MEM_LEN, D_MODEL), lambda i: (0, 0))
    return pl.pallas_call(
        body, name="mem_fwd", grid=(s // MEM_TQ,),
        in_specs=[pl.BlockSpec((MEM_TQ, D_MODEL), lambda i: (i, 0)), full, full],
        out_specs=pl.BlockSpec((MEM_TQ, D_MODEL), lambda i: (i, 0)),
        out_shape=jax.ShapeDtypeStruct((s, D_MODEL), BF16),
        compiler_params=_params(dimension_semantics=("arbitrary",)),
    )(q, k, v)


def mem_bwd(q, k, v, do):
    s = q.shape[0]

    def body(q_ref, k_ref, v_ref, do_ref, dq_ref, dk_ref, dv_ref):
        @pl.when(pl.program_id(0) == 0)
        def _():
            dk_ref[...] = jnp.zeros_like(dk_ref)
            dv_ref[...] = jnp.zeros_like(dv_ref)

        for h in range(MEM_HEADS):
            sl = slice(MEM_HEAD_DIM * h, MEM_HEAD_DIM * (h + 1))
            qh, kh, vh = q_ref[:, sl], k_ref[:, sl], v_ref[:, sl]
            doh = do_ref[:, sl].astype(BF16)
            p = _mem_probs(qh, kh)
            dv_ref[:, sl] += lax.dot_general(p.astype(BF16), doh, _DIMS['tn'], preferred_element_type=F32)
            dp = lax.dot_general(doh, vh, _DIMS['nt'], preferred_element_type=F32)
            ds = (p * (dp - jnp.sum(p * dp, axis=1, keepdims=True)) * MEM_SCALE).astype(BF16)
            dq_ref[:, sl] = lax.dot_general(ds, kh, _DIMS['nn'], preferred_element_type=F32).astype(dq_ref.dtype)
            dk_ref[:, sl] += lax.dot_general(ds, qh, _DIMS['tn'], preferred_element_type=F32)

    full = pl.BlockSpec((MEM_LEN, D_MODEL), lambda i: (0, 0))
    row = pl.BlockSpec((MEM_TQ, D_MODEL), lambda i: (i, 0))
    return pl.pallas_call(
        body, name="mem_bwd", grid=(s // MEM_TQ,),
        in_specs=[row, full, full, row], out_specs=[row, full, full],
        out_shape=[jax.ShapeDtypeStruct((s, D_MODEL), BF16), jax.ShapeDtypeStruct((MEM_LEN, D_MODEL), F32),
                   jax.ShapeDtypeStruct((MEM_LEN, D_MODEL), F32)],
        compiler_params=_params(dimension_semantics=("arbitrary",)),
    )(q, k, v, do)


def _gate_norm(y, z, g):
    return _rms(y * _silu(z), g)


def gate_norm_fwd(y, proj, g):
    return rowwise(_gate_norm, [(y, D_SSM, 0), (proj, D_SSM, OFF_Z // D_SSM)], [g], [(D_SSM, BF16, D_MIX, 0)], [],
                   "gate_norm_fwd")[0]


def gate_norm_bwd(y, proj, g, dmix):
    def fn(yt, zt, dt_, gt):
        _, vjp = jax.vjp(_gate_norm, yt, zt, gt)
        return vjp(dt_.astype(F32))

    return rowwise(fn, [(y, D_SSM, 0), (proj, D_SSM, OFF_Z // D_SSM), (dmix, D_SSM, 0)], [g],
                   [(D_SSM, F32), (D_SSM, BF16)], [((1, D_SSM), F32)], "gate_norm_bwd")


def loss_head(x, g, target):
    def fn(xt, tt, gt):
        def f(x_, g_):
            err = _rms(x_, g_) - tt
            return 0.5 * jnp.sum(jnp.mean(err * err, axis=-1))

        lv, (dx, dg) = jax.value_and_grad(f, argnums=(0, 1))(xt, gt)
        return dx, dg, jnp.full((1, LANES), lv, F32)

    return rowwise(fn, [(x, D_MODEL, 0), (target, D_MODEL, 0)], [g], [(D_MODEL, F32)],
                   [((1, D_MODEL), F32), ((1, LANES), F32)], "loss_head")


def _proj_runs(d):
    lo, hi = (D_IN // N_DEV) * d, (D_IN // N_DEV) * (d + 1)
    runs = []
    for a, b, new in PROJ_SEGS:
        s0, s1 = max(a, lo), min(b, hi)
        if s0 < s1:
            runs.append((s0 - lo, new + s0 - a, s1 - s0))
    return runs


LAYOUT_TM = 256


def assemble_proj(g, l):
    def body(g_ref, o_ref):
        o_ref[:, OFF_SMALL:OFF_SMALL + LANES] = jnp.zeros((LAYOUT_TM, LANES), o_ref.dtype)
        for d in range(N_DEV):
            for src, dst, n in _proj_runs(d):
                o_ref[:, dst:dst + n] = g_ref[d, :, src:src + n]

    return pl.pallas_call(
        body, name="assemble_proj", grid=(D_MODEL // LAYOUT_TM,),
        in_specs=[pl.BlockSpec((None, N_DEV, LAYOUT_TM, D_IN // N_DEV), lambda i: (l, 0, i, 0))],
        out_specs=pl.BlockSpec((LAYOUT_TM, PROJ_W), lambda i: (i, 0)),
        out_shape=jax.ShapeDtypeStruct((D_MODEL, PROJ_W), g.dtype),
        compiler_params=_params(dimension_semantics=("arbitrary",)),
    )(g)


def extract_proj(dz, dxbc, dcq, dsmall, dckv):
    pieces = [(OFF_Z, 1024), (OFF_XBC, 2048), (OFF_CQ, Q_LORA), (OFF_SMALL, LANES), (OFF_CKV, KV_LORA)]

    def body(*refs):
        o_ref = refs[-1]
        for d in range(N_DEV):
            for src, dst, n in _proj_runs(d):
                for p, (off, w) in enumerate(pieces):
                    if off <= dst < off + w:
                        o_ref[d, :, src:src + n] = refs[p][:, dst - off:dst - off + n].astype(o_ref.dtype)

    return pl.pallas_call(
        body, name="extract_proj", grid=(D_MODEL // LAYOUT_TM,),
        in_specs=[pl.BlockSpec((LAYOUT_TM, w), lambda i: (i, 0)) for _, w in pieces],
        out_specs=pl.BlockSpec((N_DEV, LAYOUT_TM, D_IN // N_DEV), lambda i: (0, i, 0)),
        out_shape=jax.ShapeDtypeStruct((N_DEV, D_MODEL, D_IN // N_DEV), BF16),
        compiler_params=_params(dimension_semantics=("arbitrary",)),
    )(dz, dxbc, dcq, dsmall, dckv)


_QW = QK_NOPE + QK_ROPE


def assemble_uq(g, l):
    def body(g_ref, o_ref):
        o_ref[...] = jnp.zeros_like(o_ref)
        for d in range(N_DEV):
            for e in range(2):
                dst = HEAD_PAD * (2 * d + e)
                o_ref[:, dst:dst + _QW] = g_ref[d, :, _QW * e:_QW * (e + 1)]

    return pl.pallas_call(
        body, name="assemble_uq", grid=(1,),
        in_specs=[pl.BlockSpec((None, N_DEV, Q_LORA, 2 * _QW), lambda i: (l, 0, 0, 0))],
        out_specs=pl.BlockSpec((Q_LORA, MLA_HEADS * HEAD_PAD), lambda i: (0, 0)),
        out_shape=jax.ShapeDtypeStruct((Q_LORA, MLA_HEADS * HEAD_PAD), g.dtype),
        compiler_params=_params(dimension_semantics=("arbitrary",)),
    )(g)


def extract_uq(dw):
    def body(w_ref, o_ref):
        for d in range(N_DEV):
            for e in range(2):
                src = HEAD_PAD * (2 * d + e)
                o_ref[d, :, _QW * e:_QW * (e + 1)] = w_ref[:, src:src + _QW].astype(o_ref.dtype)

    return pl.pallas_call(
        body, name="extract_uq", grid=(1,),
        in_specs=[pl.BlockSpec((Q_LORA, MLA_HEADS * HEAD_PAD), lambda i: (0, 0))],
        out_specs=pl.BlockSpec((N_DEV, Q_LORA, 2 * _QW), lambda i: (0, 0, 0)),
        out_shape=jax.ShapeDtypeStruct((N_DEV, Q_LORA, 2 * _QW), BF16),
        compiler_params=_params(dimension_semantics=("arbitrary",)),
    )(dw)


def assemble_ukv(g, l):
    def body(g_ref, kn_ref, v_ref):
        kn_ref[...] = jnp.zeros_like(kn_ref)
        for d in range(N_DEV):
            for e in range(2):
                h = 2 * d + e
                kn_ref[:, HEAD_PAD * h:HEAD_PAD * h + QK_NOPE] = g_ref[d, :, 128 * e:128 * e + QK_NOPE]
                v_ref[:, V_DIM * h:V_DIM * (h + 1)] = g_ref[d, :, 128 * e + QK_NOPE:128 * (e + 1)]

    return pl.pallas_call(
        body, name="assemble_ukv", grid=(1,),
        in_specs=[pl.BlockSpec((None, N_DEV, KV_LORA, 256), lambda i: (l, 0, 0, 0))],
        out_specs=[pl.BlockSpec((KV_LORA, MLA_HEADS * HEAD_PAD), lambda i: (0, 0)),
                   pl.BlockSpec((KV_LORA, MLA_HEADS * V_DIM), lambda i: (0, 0))],
        out_shape=[jax.ShapeDtypeStruct((KV_LORA, MLA_HEADS * HEAD_PAD), g.dtype),
                   jax.ShapeDtypeStruct((KV_LORA, MLA_HEADS * V_DIM), g.dtype)],
        compiler_params=_params(dimension_semantics=("arbitrary",)),
    )(g)


def extract_ukv(dkn, dv):
    def body(kn_ref, v_ref, o_ref):
        for d in range(N_DEV):
            for e in range(2):
                h = 2 * d + e
                o_ref[d, :, 128 * e:128 * e + QK_NOPE] = kn_ref[:, HEAD_PAD * h:HEAD_PAD * h + QK_NOPE].astype(o_ref.dtype)
                o_ref[d, :, 128 * e + QK_NOPE:128 * (e + 1)] = v_ref[:, V_DIM * h:V_DIM * (h + 1)].astype(o_ref.dtype)

    return pl.pallas_call(
        body, name="extract_ukv", grid=(1,),
        in_specs=[pl.BlockSpec((KV_LORA, MLA_HEADS * HEAD_PAD), lambda i: (0, 0)),
                  pl.BlockSpec((KV_LORA, MLA_HEADS * V_DIM), lambda i: (0, 0))],
        out_specs=pl.BlockSpec((N_DEV, KV_LORA, 256), lambda i: (0, 0, 0)),
        out_shape=jax.ShapeDtypeStruct((N_DEV, KV_LORA, 256), BF16),
        compiler_params=_params(dimension_semantics=("arbitrary",)),
    )(dkn, dv)


_UPW = 2 * D_FF // N_DEV


def assemble_up(g, l):
    def body(g_ref, wg_ref, wv_ref):
        for d in range(N_DEV):
            ref = wg_ref if d < N_DEV // 2 else wv_ref
            off = _UPW * (d % (N_DEV // 2))
            ref[:, off:off + _UPW] = g_ref[d]

    half = pl.BlockSpec((LAYOUT_TM, D_FF), lambda i: (i, 0))
    return pl.pallas_call(
        body, name="assemble_up", grid=(D_MODEL // LAYOUT_TM,),
        in_specs=[pl.BlockSpec((None, N_DEV, LAYOUT_TM, _UPW), lambda i: (l, 0, i, 0))],
        out_specs=[half, half], out_shape=[jax.ShapeDtypeStruct((D_MODEL, D_FF), g.dtype)] * 2,
        compiler_params=_params(dimension_semantics=("arbitrary",)),
    )(g)


def extract_up(dwg, dwv):
    def body(wg_ref, wv_ref, o_ref):
        for d in range(N_DEV):
            ref = wg_ref if d < N_DEV // 2 else wv_ref
            off = _UPW * (d % (N_DEV // 2))
            o_ref[d] = ref[:, off:off + _UPW].astype(o_ref.dtype)

    half = pl.BlockSpec((LAYOUT_TM, D_FF), lambda i: (i, 0))
    return pl.pallas_call(
        body, name="extract_up", grid=(D_MODEL // LAYOUT_TM,), in_specs=[half, half],
        out_specs=pl.BlockSpec((N_DEV, LAYOUT_TM, _UPW), lambda i: (0, i, 0)),
        out_shape=jax.ShapeDtypeStruct((N_DEV, D_MODEL, _UPW), BF16),
        compiler_params=_params(dimension_semantics=("arbitrary",)),
    )(dwg, dwv)


MESH = pl.DeviceIdType.MESH
ANY = pl.BlockSpec(memory_space=pl.ANY)


def _place():
    mx, my, mc = lax.axis_index("x"), lax.axis_index("y"), lax.axis_index("c")
    return mx, my, mc, [(1 - mx, my), (mx, 1 - my), (1 - mx, 1 - my)]


def all_gather_blocks(xs):
    n = len(xs)

    def body(*refs):
        x_refs, out_refs = refs[:n], refs[n:2 * n]
        send_sems, recv_sems, local_sems = refs[2 * n:]
        mx, my, mc, chips = _place()
        me, sibling = (mx, my, mc), (mx, my, 1 - mc)

        def rows(t, px, py, pc):
            return out_refs[t].at[:, 4 * px + 2 * py + pc]

        def copy(t, k, block, to, src=None):
            return pltpu.make_async_remote_copy(
                src_ref=rows(t, *block) if src is None else src, dst_ref=rows(t, *block),
                send_sem=send_sems.at[t, k], recv_sem=recv_sems.at[t, k], device_id=to, device_id_type=MESH)

        mine = [pltpu.make_async_copy(x_refs[t], rows(t, *me), local_sems.at[t]) for t in range(n)]
        for cp in mine:
            cp.start()
        first = []
        for t in range(n):
            first.append(copy(t, 0, me, sibling, src=x_refs[t]))
            first += [copy(t, 1 + j, me, (*chip, mc), src=x_refs[t]) for j, chip in enumerate(chips)]
        for cp in first:
            cp.start()
        passed = []
        for j, chip in enumerate(chips):
            for t in range(n):
                copy(t, 1 + j, (*chip, mc), me).wait_recv()
                cp = copy(t, 4 + j, (*chip, mc), sibling)
                cp.start()
                passed.append(cp)
        for t in range(n):
            copy(t, 0, sibling, me).wait_recv()
            for j, chip in enumerate(chips):
                copy(t, 4 + j, (*chip, 1 - mc), me).wait_recv()
        for cp in first + passed:
            cp.wait_send()
        for cp in mine:
            cp.wait()

    return pl.pallas_call(
        body, name="all_gather_blocks",
        out_shape=[jax.ShapeDtypeStruct((x.shape[0], N_DEV) + x.shape[1:], x.dtype) for x in xs],
        in_specs=[ANY] * n, out_specs=[ANY] * n,
        scratch_shapes=[pltpu.SemaphoreType.DMA((n, 7)), pltpu.SemaphoreType.DMA((n, 7)), pltpu.SemaphoreType.DMA((n,))],
    )(*xs)


def _whole(ref, send_sem, recv_sem, me):
    return pltpu.make_async_remote_copy(src_ref=ref, dst_ref=ref, send_sem=send_sem, recv_sem=recv_sem,
                                        device_id=me, device_id_type=MESH)


def sibling_exchange(es):
    nt, depth = len(es), len(es[0])

    def body(*refs):
        e_refs = [refs[t * depth:(t + 1) * depth] for t in range(nt)]
        out_refs = refs[nt * depth:nt * depth + nt]
        send_sems, recv_sems = refs[nt * depth + nt:]
        mx, my, mc, _ = _place()
        sibling = (mx, my, 1 - mc)
        for t in range(nt):
            for l in range(depth):
                for k in range(N_CHIP):
                    pltpu.make_async_remote_copy(
                        src_ref=e_refs[t][l].at[2 * k + 1 - mc], dst_ref=out_refs[t].at[l, k],
                        send_sem=send_sems.at[t], recv_sem=recv_sems.at[t], device_id=sibling, device_id_type=MESH).start()
        for t in range(nt):
            _whole(out_refs[t], send_sems.at[t], recv_sems.at[t], (mx, my, mc)).wait_recv()
        for t in range(nt):
            _whole(out_refs[t], send_sems.at[t], recv_sems.at[t], (mx, my, mc)).wait_send()

    return pl.pallas_call(
        body, name="rs_sibling_exchange",
        out_shape=[jax.ShapeDtypeStruct((depth, N_CHIP) + e[0].shape[1:], e[0].dtype) for e in es],
        in_specs=[ANY] * (nt * depth), out_specs=[ANY] * nt,
        scratch_shapes=[pltpu.SemaphoreType.DMA((nt,)), pltpu.SemaphoreType.DMA((nt,))],
    )(*[e for t in es for e in t])


def chip_sum(e, a_buf, core):
    depth = len(e)
    _, a, b = e[0].shape
    ta = _row_tile(a)

    def body(c_ref, *refs):
        a_ref, o_ref = refs[depth], refs[depth + 1]
        for l in range(depth):
            o_ref[l] = (refs[l][...].astype(F32) + a_ref[l].astype(F32)).astype(o_ref.dtype)

    buf = pl.BlockSpec((depth, None, ta, b), lambda k, i, c: (0, k, i, 0))
    return pl.pallas_call(
        body, name="rs_chip_sum",
        grid_spec=pltpu.PrefetchScalarGridSpec(
            num_scalar_prefetch=1, grid=(N_CHIP, a // ta),
            in_specs=[pl.BlockSpec((None, ta, b), lambda k, i, c: (2 * k + c[0], i, 0))] * depth + [buf],
            out_specs=buf),
        out_shape=jax.ShapeDtypeStruct(a_buf.shape, a_buf.dtype),
        compiler_params=_params(dimension_semantics=("arbitrary", "arbitrary")),
    )(core, *e, a_buf)


def chip_exchange(ps):
    nt = len(ps)

    def body(*refs):
        p_refs, out_refs = refs[:nt], refs[nt:2 * nt]
        send_sems, recv_sems, local_sems = refs[2 * nt:]
        mx, my, mc, chips = _place()
        mychip = 2 * mx + my
        mine = [pltpu.make_async_copy(p_refs[t].at[:, mychip], out_refs[t].at[:, mychip], local_sems.at[t]) for t in range(nt)]
        for cp in mine:
            cp.start()

        def copy(t, j):
            cx, cy = chips[j]
            return pltpu.make_async_remote_copy(
                src_ref=p_refs[t].at[:, 2 * cx + cy], dst_ref=out_refs[t].at[:, mychip], send_sem=send_sems.at[t, j],
                recv_sem=recv_sems.at[t, j], device_id=(cx, cy, mc), device_id_type=MESH)

        sends = [copy(t, j) for t in range(nt) for j in range(3)]
        for cp in sends:
            cp.start()
        for t in range(nt):
            for j, (cx, cy) in enumerate(chips):
                pltpu.make_async_remote_copy(
                    src_ref=p_refs[t].at[:, mychip], dst_ref=out_refs[t].at[:, 2 * cx + cy], send_sem=send_sems.at[t, j],
                    recv_sem=recv_sems.at[t, j], device_id=(cx, cy, mc), device_id_type=MESH).wait_recv()
        for cp in sends:
            cp.wait_send()
        for cp in mine:
            cp.wait()

    return pl.pallas_call(
        body, name="rs_chip_exchange", out_shape=[jax.ShapeDtypeStruct(p.shape, p.dtype) for p in ps],
        in_specs=[ANY] * nt, out_specs=[ANY] * nt,
        scratch_shapes=[pltpu.SemaphoreType.DMA((nt, 3)), pltpu.SemaphoreType.DMA((nt, 3)), pltpu.SemaphoreType.DMA((nt,))],
    )(*ps)


def _adam(g, w, m, v):
    nm = ADAM_B1 * m + (1.0 - ADAM_B1) * g
    nv = ADAM_B2 * v + (1.0 - ADAM_B2) * jnp.square(g)
    m_hat = nm / (1.0 - ADAM_B1 ** ADAM_STEP)
    v_hat = nv / (1.0 - ADAM_B2 ** ADAM_STEP)
    return -ADAM_LR * (m_hat / (jnp.sqrt(v_hat) + ADAM_EPS) + ADAM_WD * w), nm, nv


def adamw_big(parts, w, m, v, name):
    depth, _, a, b = parts.shape
    ta = _row_tile(a)

    def body(p_ref, w_ref, m_ref, v_ref, g_ref, d_ref, nm_ref, nv_ref):
        g = p_ref[0].astype(F32)
        for k in range(1, N_CHIP):
            g = g + p_ref[k].astype(F32)
        g_ref[...] = g
        d_ref[...], nm_ref[...], nv_ref[...] = _adam(g, w_ref[...], m_ref[...], v_ref[...])

    blk = pl.BlockSpec((None, ta, b), lambda l, i: (l, i, 0))
    return pl.pallas_call(
        body, name=name, grid=(depth, a // ta),
        in_specs=[pl.BlockSpec((None, N_CHIP, ta, b), lambda l, i: (l, 0, i, 0)), blk, blk, blk], out_specs=[blk] * 4,
        out_shape=[jax.ShapeDtypeStruct((depth, a, b), F32)] * 4,
        compiler_params=_params(dimension_semantics=("arbitrary", "arbitrary")),
    )(parts, w, m, v)


SMALL_VIEW = {'norm_mix': (DEPTH, 1024), 'ssm_norm': (DEPTH, 1024), 'attn_out_norm': (DEPTH, 1024),
              'norm_mem_q': (DEPTH, 1024), 'norm_mem_kv': (DEPTH, 1024), 'norm_ffn': (DEPTH, 1024),
              'q_norm': (DEPTH, 384), 'kv_norm': (DEPTH, 256), 'ssm_conv_b': (DEPTH, 2048), 'ffn_conv_b': (DEPTH, 5632),
              'dt_bias': (DEPTH, SSM_HEADS), 'a_log': (DEPTH, SSM_HEADS), 'd_skip': (DEPTH, SSM_HEADS),
              'ssm_conv_w': (DEPTH, SSM_CONV * CONV_CH // N_DEV), 'ffn_conv_w': (DEPTH, FFN_CONV * 2 * D_FF // N_DEV),
              'final_norm': (1, 1024)}
SMALL_NAMES = list(SMALL_VIEW)
SMALL_SHARDED = {'ssm_conv_w': (SSM_CONV, CONV_CH // N_DEV, CONV_CH), 'ffn_conv_w': (FFN_CONV, 2 * D_FF // N_DEV, 2 * D_FF)}


def adamw_small(gathered, ws, ms, vs):
    nsm = len(SMALL_NAMES)

    def body(*refs):
        g8_ref = refs[0]
        w_refs, m_refs, v_refs = refs[1:1 + nsm], refs[1 + nsm:1 + 2 * nsm], refs[1 + 2 * nsm:1 + 3 * nsm]
        outs = refs[1 + 3 * nsm:1 + 7 * nsm]
        sum_ref = refs[1 + 7 * nsm]
        shard_bufs = refs[2 + 7 * nsm:]
        tot = g8_ref[:, 0, :]
        for d in range(1, N_DEV):
            tot = tot + g8_ref[:, d, :]
        sum_ref[...] = tot
        mx, my, mc, _ = _place()
        dev = 4 * mx + 2 * my + mc

        def update(i, g):
            d, nm, nv = _adam(g, w_refs[i][...], m_refs[i][...], v_refs[i][...])
            outs[i][...] = g
            outs[nsm + i][...] = d
            outs[2 * nsm + i][...] = nm
            outs[3 * nsm + i][...] = nv

        for i, name in enumerate(SMALL_NAMES):
            rows, cols = SMALL_VIEW[name]
            off = SMALL_OFF[name]
            if name in SMALL_SHARDED:
                taps, per, full = SMALL_SHARDED[name]
                buf = shard_bufs[list(SMALL_SHARDED).index(name)]
                for d in range(N_DEV):
                    @pl.when(dev == d)
                    def _(d=d, taps=taps, per=per, full=full, off=off, buf=buf):
                        for k in range(taps):
                            buf[:, per * k:per * (k + 1)] = sum_ref[:, off + full * k + per * d:off + full * k + per * (d + 1)]
                update(i, buf[...])
            else:
                update(i, sum_ref[0:rows, off:off + cols])

    views = [jax.ShapeDtypeStruct(SMALL_VIEW[n], F32) for n in SMALL_NAMES]
    vmem = pl.BlockSpec(memory_space=pltpu.VMEM)
    res = pl.pallas_call(
        body, name="adamw_small", out_shape=views * 4, in_specs=[vmem] * (1 + 3 * nsm), out_specs=[vmem] * (4 * nsm),
        scratch_shapes=[pltpu.VMEM((DEPTH, SMALL_W), F32)] + [pltpu.VMEM(SMALL_VIEW[n], F32) for n in SMALL_SHARDED],
        compiler_params=_params(),
    )(gathered, *[ws[n] for n in SMALL_NAMES], *[ms[n] for n in SMALL_NAMES], *[vs[n] for n in SMALL_NAMES])
    return [dict(zip(SMALL_NAMES, res[k * nsm:(k + 1) * nsm])) for k in range(4)]


def _layer_weights(gathered, l):
    w_kn, w_v = assemble_ukv(gathered['w_ukv'], l)
    w_g, w_vv = assemble_up(gathered['w_up'], l)
    stacked = lambda n: Opnd(gathered[n].reshape(DEPTH, N_DEV * BIG[n][0], BIG[n][1]), lead=l)
    return dict(w_proj=assemble_proj(gathered['w_in'], l), w_uq=assemble_uq(gathered['w_uq'], l), w_kn=w_kn, w_v=w_v,
                w_g=w_g, w_vv=w_vv, w_out=stacked('w_out'), w_mq=stacked('w_mq'), w_mk=stacked('w_mk'),
                w_mv=stacked('w_mv'), w_mo=stacked('w_mo'), w_down=stacked('w_down'))


def layer_fwd(x0, mem, cosm, sinm, w, sm, l):
    gain = lambda n: (sm[n], l)
    sv = dict(x0=x0)
    sv['h1'] = rmsnorm_fwd(x0, gain('norm_mix'), "norm_mix_fwd")
    proj = sv['proj'] = matmul([(sv['h1'], w['w_proj'])], 'nn', F32, "proj_fwd")
    sv['xbc'] = ssm_conv_fwd(proj, sm['ssm_conv_w'], sm['ssm_conv_b'], l)
    sv['y'], sv['prevs'] = ssd_fwd(sv['xbc'], proj, sm['ptile'], l)
    mix = gate_norm_fwd(sv['y'], proj, gain('ssm_norm'))
    sv['cqn'] = rmsnorm_fwd(proj, gain('q_norm'), "q_norm_fwd", Q_LORA, OFF_CQ // Q_LORA)
    sv['ckvn'] = rmsnorm_fwd(proj, gain('kv_norm'), "kv_norm_fwd", KV_LORA, OFF_CKV // KV_LORA)
    q = matmul([(sv['cqn'], w['w_uq'])], 'nn', F32, "uq_fwd")
    sv['q'] = rope_q(q, cosm, sinm, "rope_q_fwd")
    kn = matmul([(sv['ckvn'], w['w_kn'])], 'nn', BF16, "kn_fwd")
    sv['k'] = build_k(kn, proj, cosm, sinm)
    sv['v'] = matmul([(sv['ckvn'], w['w_v'])], 'nn', BF16, "v_fwd")
    sv['o'] = mla_fwd(sv['q'], sv['k'], sv['v'])
    mix = sv['mix'] = rmsnorm_fwd(sv['o'], gain('attn_out_norm'), "attn_out_norm_fwd", out=(D_SSM, BF16, D_MIX, 1),
                                  into=(mix, 0))
    x1 = sv['x1'] = matmul([(mix, w['w_out'])], 'nn', F32, "out_fwd", add=x0)
    sv['hq'] = rmsnorm_fwd(x1, gain('norm_mem_q'), "norm_mem_q_fwd")
    sv['mn'] = rmsnorm_fwd(mem, gain('norm_mem_kv'), "norm_mem_kv_fwd")
    sv['mq'] = matmul([(sv['hq'], w['w_mq'])], 'nn', BF16, "mq_fwd")
    sv['mk'] = matmul([(sv['mn'], w['w_mk'])], 'nn', BF16, "mk_fwd")
    sv['mv'] = matmul([(sv['mn'], w['w_mv'])], 'nn', BF16, "mv_fwd")
    sv['om'] = mem_fwd(sv['mq'], sv['mk'], sv['mv'])
    x2 = sv['x2'] = matmul([(sv['om'], w['w_mo'])], 'nn', F32, "mo_fwd", add=x1)
    sv['h3'] = rmsnorm_fwd(x2, gain('norm_ffn'), "norm_ffn_fwd")
    sv['ug'] = matmul([(sv['h3'], w['w_g'])], 'nn', F32, "up_g_fwd")
    sv['uv'] = matmul([(sv['h3'], w['w_vv'])], 'nn', F32, "up_v_fwd")
    sv['a'] = ffn_act_fwd(sv['ug'], sv['uv'], sm['ffn_conv_w'], sm['ffn_conv_b'], l)
    x3 = matmul([(sv['a'], w['w_down'])], 'nn', F32, "down_fwd", add=x2)
    return x3, sv


def layer_bwd(dx3, mem, cosm, sinm_neg, w, sm, l, sv):
    gain = lambda n: (sm[n], l)
    big, small = {}, {}
    proj = sv['proj']
    da = matmul([(dx3, w['w_down'])], 'nt', BF16, "down_bwd_a")
    big['w_down'] = matmul([(sv['a'], dx3)], 'tn', BF16, "down_bwd_w")
    dug, duv, dcwg, dcwv, dcbg, dcbv = ffn_act_bwd(sv['ug'], sv['uv'], sm['ffn_conv_w'], sm['ffn_conv_b'], l, da)
    small['ffn_conv_w'] = jnp.concatenate([dcwg, dcwv], axis=1)
    small['ffn_conv_b'] = jnp.concatenate([dcbg, dcbv], axis=1)
    dh3 = matmul([(dug, w['w_g']), (duv, w['w_vv'])], 'nt', BF16, "up_bwd_h")
    big['w_up'] = extract_up(matmul([(sv['h3'], dug)], 'tn', BF16, "up_g_bwd_w"),
                             matmul([(sv['h3'], duv)], 'tn', BF16, "up_v_bwd_w"))
    dx2, small['norm_ffn'] = rmsnorm_bwd(sv['x2'], gain('norm_ffn'), dh3, "norm_ffn_bwd", resid=dx3)
    dom = matmul([(dx2, w['w_mo'])], 'nt', BF16, "mo_bwd_a")
    big['w_mo'] = matmul([(sv['om'], dx2)], 'tn', BF16, "mo_bwd_w")
    dmq, dmk, dmv = mem_bwd(sv['mq'], sv['mk'], sv['mv'], dom)
    dhq = matmul([(dmq, w['w_mq'])], 'nt', BF16, "mq_bwd_a")
    big['w_mq'] = matmul([(sv['hq'], dmq)], 'tn', BF16, "mq_bwd_w")
    dmn = matmul([(dmk, w['w_mk']), (dmv, w['w_mv'])], 'nt', BF16, "mkv_bwd_a")
    big['w_mk'] = matmul([(sv['mn'], dmk)], 'tn', BF16, "mk_bwd_w")
    big['w_mv'] = matmul([(sv['mn'], dmv)], 'tn', BF16, "mv_bwd_w")
    _, small['norm_mem_kv'] = rmsnorm_bwd(mem, gain('norm_mem_kv'), dmn, "norm_mem_kv_bwd", dx_dtype=BF16)
    dx1, small['norm_mem_q'] = rmsnorm_bwd(sv['x1'], gain('norm_mem_q'), dhq, "norm_mem_q_bwd", resid=dx2)
    dmix = matmul([(dx1, w['w_out'])], 'nt', BF16, "out_bwd_a")
    big['w_out'] = matmul([(sv['mix'], dx1)], 'tn', BF16, "out_bwd_w")
    dy, dz, small['ssm_norm'] = gate_norm_bwd(sv['y'], proj, gain('ssm_norm'), dmix)
    dxbc_act, dsmall_ssd, small['ptile'] = ssd_bwd(sv['xbc'], proj, sm['ptile'], l, sv['prevs'], dy)
    dxbc, small['ssm_conv_w'], small['ssm_conv_b'] = ssm_conv_bwd(proj, sm['ssm_conv_w'], sm['ssm_conv_b'], l, dxbc_act)
    do, small['attn_out_norm'] = rmsnorm_bwd(sv['o'], gain('attn_out_norm'), dmix, "attn_out_norm_bwd", dh_colblock=1)
    dq_rot, dk, dv = mla_bwd(sv['q'], sv['k'], sv['v'], sv['o'], do)
    dq = rope_q(dq_rot, cosm, sinm_neg, "rope_q_bwd")
    dsmall = dsmall_bwd(dk, dsmall_ssd, cosm, sinm_neg)
    dcqn = matmul([(dq, w['w_uq'])], 'nt', BF16, "uq_bwd_a")
    big['w_uq'] = extract_uq(matmul([(sv['cqn'], dq)], 'tn', BF16, "uq_bwd_w"))
    dckvn = matmul([(dk, w['w_kn']), (dv, w['w_v'])], 'nt', BF16, "ukv_bwd_a")
    big['w_ukv'] = extract_ukv(matmul([(sv['ckvn'], dk)], 'tn', BF16, "kn_bwd_w"),
                               matmul([(sv['ckvn'], dv)], 'tn', BF16, "v_bwd_w"))
    dcq, small['q_norm'] = rmsnorm_bwd(proj, gain('q_norm'), dcqn, "q_norm_bwd", width=Q_LORA,
                                       colblock=OFF_CQ // Q_LORA, dx_dtype=BF16)
    dckv, small['kv_norm'] = rmsnorm_bwd(proj, gain('kv_norm'), dckvn, "kv_norm_bwd", width=KV_LORA,
                                         colblock=OFF_CKV // KV_LORA, dx_dtype=BF16)
    wp = w['w_proj']
    xbc_half = lambda c: Opnd(dxbc, c0=c, shape=(dxbc.shape[0], 1024))
    wwin = lambda off, width: Opnd(wp, c0=off // width, shape=(D_MODEL, width))
    dh1 = matmul([(dz, wwin(OFF_Z, 1024)), (xbc_half(0), wwin(OFF_XBC, 1024)), (xbc_half(1), wwin(OFF_XBC + 1024, 1024)),
                  (dcq, wwin(OFF_CQ, Q_LORA)), (dsmall, wwin(OFF_SMALL, LANES)), (dckv, wwin(OFF_CKV, KV_LORA))],
                 'nt', BF16, "proj_bwd_a")
    h1 = sv['h1']
    big['w_in'] = extract_proj(
        matmul([(h1, dz)], 'tn', BF16, "proj_z_bwd_w"), matmul([(h1, dxbc)], 'tn', BF16, "proj_xbc_bwd_w"),
        matmul([(h1, dcq)], 'tn', BF16, "proj_cq_bwd_w"), matmul([(h1, dsmall)], 'tn', BF16, "proj_small_bwd_w"),
        matmul([(h1, dckv)], 'tn', BF16, "proj_ckv_bwd_w"))
    dx0, small['norm_mix'] = rmsnorm_bwd(sv['x0'], gain('norm_mix'), dh1, "norm_mix_bwd", resid=dx1)
    for n in ('w_down', 'w_mo', 'w_mq', 'w_mk', 'w_mv', 'w_out'):
        big[n] = big[n].reshape((N_DEV,) + BIG[n])
    return dx0, big, small


def _small_row(small, final=None):
    pt = small['ptile']
    parts = []
    for n, wd in SMALL_SEGS:
        if n in ('dt_bias', 'a_log', 'd_skip'):
            parts.append(pt[('dt_bias', 'a_log', 'd_skip').index(n)][None, :])
        elif n in SMALL_SHARDED:
            parts.append(small[n].reshape(1, wd))
        elif n == 'final_norm':
            parts.append(final if final is not None else jnp.zeros((1, wd), F32))
        else:
            parts.append(small[n])
    return jnp.concatenate(parts, axis=1)


def _rope_tables(positions):
    inv_freq = 1.0 / (ROPE_THETA ** (jnp.arange(0, QK_ROPE, 2, dtype=F32) / QK_ROPE))
    ang = positions.astype(F32)[:, None] * inv_freq
    cos, sin = jnp.cos(ang), jnp.sin(ang)
    s = positions.shape[0]
    pad = jnp.zeros((s, LANES - ROPE_LANE0 - QK_ROPE), F32)
    cosm = jnp.concatenate([jnp.ones((s, ROPE_LANE0), F32), cos, cos, pad], axis=1)
    sinm = jnp.concatenate([jnp.zeros((s, ROPE_LANE0), F32), -sin, sin, pad], axis=1)
    return cosm, sinm


def _small_views(rep, conv_full):
    sm = {n: rep[n].reshape(DEPTH, 1, -1) for n in ('norm_mix', 'ssm_norm', 'attn_out_norm', 'norm_mem_q',
                                                    'norm_mem_kv', 'norm_ffn', 'q_norm', 'kv_norm', 'ssm_conv_b',
                                                    'ffn_conv_b')}
    sm.update(conv_full)
    rows = jnp.stack([rep['dt_bias'], rep['a_log'], rep['d_skip']], axis=1)
    sm['ptile'] = jnp.pad(rows, ((0, 0), (0, 8 - 3), (0, LANES - SSM_HEADS)))
    return sm


def local_step(x, mem, positions, target, gathered, sm, final_norm):
    cosm, sinm = _rope_tables(positions)
    sinm_neg = -sinm
    saved, ws = [], []
    h = x
    for l in range(DEPTH):
        ws.append(_layer_weights(gathered, l))
        h, sv = layer_fwd(h, mem, cosm, sinm, ws[l], sm, l)
        saved.append(sv)
    dx, dfinal, lossv = loss_head(h, (final_norm.reshape(1, 1, -1), 0), target)
    bigs, rows = [None] * DEPTH, [None] * DEPTH
    for l in reversed(range(DEPTH)):
        dx, bigs[l], small = layer_bwd(dx, mem, cosm, sinm_neg, ws[l], sm, l, saved[l])
        rows[l] = _small_row(small, dfinal if l == 0 else None)
    return lossv[0, 0], dx, bigs, jnp.concatenate(rows, axis=0)


def kernel(x, mem, positions, norm_mix, w_in, ssm_conv_w, ssm_conv_b, dt_bias, a_log, d_skip, ssm_norm, q_norm, w_uq, kv_norm, w_ukv, attn_out_norm, w_out, norm_mem_q, norm_mem_kv, w_mq, w_mk, w_mv, w_mo, norm_ffn, w_up, ffn_conv_w, ffn_conv_b, w_down, final_norm, loss_target, m_norm_mix, m_w_in, m_ssm_conv_w, m_ssm_conv_b, m_dt_bias, m_a_log, m_d_skip, m_ssm_norm, m_q_norm, m_w_uq, m_kv_norm, m_w_ukv, m_attn_out_norm, m_w_out, m_norm_mem_q, m_norm_mem_kv, m_w_mq, m_w_mk, m_w_mv, m_w_mo, m_norm_ffn, m_w_up, m_ffn_conv_w, m_ffn_conv_b, m_w_down, m_final_norm, v_norm_mix, v_w_in, v_ssm_conv_w, v_ssm_conv_b, v_dt_bias, v_a_log, v_d_skip, v_ssm_norm, v_q_norm, v_w_uq, v_kv_norm, v_w_ukv, v_attn_out_norm, v_w_out, v_norm_mem_q, v_norm_mem_kv, v_w_mq, v_w_mk, v_w_mv, v_w_mo, v_norm_ffn, v_w_up, v_ffn_conv_w, v_ffn_conv_b, v_w_down, v_final_norm):
    args = locals()
    wts = {n: args[n] for n in WEIGHT_NAMES}
    ms = {n: args['m_' + n] for n in WEIGHT_NAMES}
    vs = {n: args['v_' + n] for n in WEIGHT_NAMES}

    send = [wts[n].astype(BF16) for n in BIG_NAMES] + [wts[n] for n in SMALL_SHARDED]
    got = all_gather_blocks(send)
    gathered = dict(zip(BIG_NAMES, got[:len(BIG_NAMES)]))
    conv_full = {}
    for n, g in zip(SMALL_SHARDED, got[len(BIG_NAMES):]):
        taps, per, full = SMALL_SHARDED[n]
        conv_full[n] = jnp.moveaxis(g, 1, 2).reshape(DEPTH, taps, full)
    sm = _small_views(wts, conv_full)

    loss_local, dx, bigs, small_rows = local_step(x[0], mem[0], positions[0], loss_target[0], gathered, sm, final_norm)

    es = [[bigs[l][n] for l in range(DEPTH)] for n in BIG_NAMES]
    core = lax.axis_index("c").astype(jnp.int32).reshape(1)
    from_sibling = sibling_exchange(es)
    chip_parts = [chip_sum(es[t], from_sibling[t], core) for t in range(len(BIG_NAMES))]
    parts = chip_exchange(chip_parts)
    outs = [{}, {}, {}, {}]
    for t, n in enumerate(BIG_NAMES):
        res = adamw_big(parts[t], wts[n], ms[n], vs[n], "adamw_" + n)
        for k in range(4):
            outs[k][n] = res[k]

    small_all = all_gather_blocks([small_rows])[0]
    view = lambda d: {n: d[n].reshape(SMALL_VIEW[n]) for n in SMALL_NAMES}
    res = adamw_small(small_all, view(wts), view(ms), view(vs))
    for k in range(4):
        for n in SMALL_NAMES:
            outs[k][n] = res[k][n].reshape(wts[n].shape)

    loss = lax.psum(loss_local, ("x", "y", "c"))
    return (loss, dx[None], *[outs[0][n] for n in WEIGHT_NAMES], *[outs[1][n] for n in WEIGHT_NAMES],
            *[outs[2][n] for n in WEIGHT_NAMES], *[outs[3][n] for n in WEIGHT_NAMES])
```

```python
import functools
import math
from typing import Any, NamedTuple, Optional

import jax
import jax.numpy as jnp
from jax import lax
from jax.experimental import pallas as pl
from jax.experimental.pallas import tpu as pltpu

F32 = jnp.float32
BF16 = jnp.bfloat16

D_MODEL = 1024
DEPTH = 4
MEM_LEN = 256
EPS = 1e-6
SSM_HEADS = 16
SSM_HEAD_DIM = 64
D_SSM = 1024
SSM_GROUPS = 4
SSM_STATE = 128
SSM_CONV = 4
SSM_CHUNK = 128
CONV_CH = 2048
MLA_HEADS = 16
QK_NOPE = 64
QK_ROPE = 32
V_DIM = 64
Q_LORA = 384
KV_LORA = 256
ROPE_THETA = 10000.0
MEM_HEADS = 4
MEM_HEAD_DIM = 256
D_FF = 2816
FFN_CONV = 3
D_IN = 3760
D_MIX = 2048
ADAM_LR = 0.001
ADAM_B1 = 0.9
ADAM_B2 = 0.999
ADAM_EPS = 1e-08
ADAM_WD = 0.01
ADAM_STEP = 10

N_DEV = 8
N_CHIP = 4
LANES = 128
HEAD_PAD = 128
PROJ_W = 3840
OFF_Z, OFF_XBC, OFF_CQ, OFF_SMALL, OFF_CKV = 0, 1024, 3072, 3456, 3584
ROPE_LANE0 = 64
VMEM_LIMIT = 56 * 1024 * 1024

WEIGHT_NAMES = ['norm_mix', 'w_in', 'ssm_conv_w', 'ssm_conv_b', 'dt_bias', 'a_log', 'd_skip', 'ssm_norm', 'q_norm',
                'w_uq', 'kv_norm', 'w_ukv', 'attn_out_norm', 'w_out', 'norm_mem_q', 'norm_mem_kv', 'w_mq', 'w_mk',
                'w_mv', 'w_mo', 'norm_ffn', 'w_up', 'ffn_conv_w', 'ffn_conv_b', 'w_down', 'final_norm']
BIG = {'w_in': (1024, 470), 'w_uq': (384, 192), 'w_ukv': (256, 256), 'w_up': (1024, 704), 'w_out': (256, 1024),
       'w_mq': (128, 1024), 'w_mk': (128, 1024), 'w_mv': (128, 1024), 'w_mo': (128, 1024), 'w_down': (352, 1024)}
BIG_NAMES = list(BIG)
PROJ_SEGS = [(0, 1024, OFF_Z), (1024, 3072, OFF_XBC), (3072, 3088, OFF_SMALL), (3088, 3472, OFF_CQ),
             (3472, 3728, OFF_CKV), (3728, 3760, OFF_SMALL + ROPE_LANE0)]
SMALL_SEGS = [('norm_mix', 1024), ('ssm_norm', 1024), ('attn_out_norm', 1024), ('norm_mem_q', 1024),
              ('norm_mem_kv', 1024), ('norm_ffn', 1024), ('q_norm', 384), ('kv_norm', 256), ('ssm_conv_b', 2048),
              ('ffn_conv_b', 5632), ('dt_bias', 128), ('a_log', 128), ('d_skip', 128),
              ('ssm_conv_w', SSM_CONV * CONV_CH), ('ffn_conv_w', FFN_CONV * 2 * D_FF), ('final_norm', 1024)]
SMALL_OFF = {}
_o = 0
for _n, _w in SMALL_SEGS:
    SMALL_OFF[_n] = _o
    _o += _w
SMALL_W = _o


def _params(**kw):
    return pltpu.CompilerParams(vmem_limit_bytes=VMEM_LIMIT, **kw)


def _pick(n, cap):
    if n <= cap:
        return n
    best = None
    for t in range(LANES, cap + 1, LANES):
        if n % t == 0:
            best = t
    assert best is not None, (n, cap)
    return best


def _row_tile(a, cap=256):
    if a <= cap:
        return a
    best = None
    for t in range(16, cap + 1, 16):
        if a % t == 0:
            best = t
    assert best is not None, (a, cap)
    return best


class Opnd(NamedTuple):
    arr: Any
    lead: Optional[int] = None
    r0: int = 0
    c0: int = 0
    shape: Optional[tuple] = None


def _opnd(x):
    return x if isinstance(x, Opnd) else Opnd(x)


def _lshape(o):
    return tuple(o.shape) if o.shape is not None else tuple(o.arr.shape[-2:])


def _spec(o, br, bc, bi, bj):
    rr, cc = _lshape(o)
    assert rr % br == 0 and cc % bc == 0, (rr, cc, br, bc)
    ro, co = o.r0 * (rr // br), o.c0 * (cc // bc)
    if o.lead is None:
        return pl.BlockSpec((br, bc), lambda i, j: (ro + bi(i, j), co + bj(i, j)))
    return pl.BlockSpec((None, br, bc), lambda i, j: (o.lead, ro + bi(i, j), co + bj(i, j)))


_DIMS = {'nn': (((1,), (0,)), ((), ())), 'nt': (((1,), (1,)), ((), ())), 'tn': (((0,), (0,)), ((), ()))}
_ROW = lambda i, j: i
_COL = lambda i, j: j
_ZERO = lambda i, j: 0


def matmul(pairs, mode, out_dtype, name, add=None, tie=None):
    pairs = [(_opnd(a), _opnd(b)) for a, b in pairs]
    a0, b0 = pairs[0]
    if mode == 'nn':
        m, n = _lshape(a0)[0], _lshape(b0)[1]
    elif mode == 'nt':
        m, n = _lshape(a0)[0], _lshape(b0)[0]
    else:
        m, n = _lshape(a0)[1], _lshape(b0)[1]
    kmax = max(_lshape(a)[0] if mode == 'tn' else _lshape(a)[1] for a, _ in pairs)
    if mode == 'tn':
        tm, tn = _pick(m, 512), _pick(n, 512)
    else:
        tm = _pick(m, 512)
        tn = _pick(n, max(LANES, (3 * 1024 * 1024 // (2 * kmax)) // LANES * LANES))
    npairs = len(pairs)

    def body(*refs):
        o_ref = refs[-1]
        acc = None
        for p in range(npairs):
            a = refs[2 * p][...].astype(BF16)
            b = refs[2 * p + 1][...].astype(BF16)
            d = lax.dot_general(a, b, _DIMS[mode], preferred_element_type=F32)
            acc = d if acc is None else acc + d
        if add is not None:
            acc = acc + refs[2 * npairs][...].astype(F32)
        o_ref[...] = acc.astype(out_dtype)

    tie_specs = [pl.BlockSpec(memory_space=pl.ANY)] if tie is not None else []
    tie_args = [tie] if tie is not None else []

    in_specs, args = [], []
    for a, b in pairs:
        if mode == 'nn':
            k = _lshape(a)[1]
            in_specs += [_spec(a, tm, k, _ROW, _ZERO), _spec(b, k, tn, _ZERO, _COL)]
        elif mode == 'nt':
            k = _lshape(a)[1]
            in_specs += [_spec(a, tm, k, _ROW, _ZERO), _spec(b, tn, k, _COL, _ZERO)]
        else:
            k = _lshape(a)[0]
            in_specs += [_spec(a, k, tm, _ZERO, _ROW), _spec(b, k, tn, _ZERO, _COL)]
        args += [a.arr, b.arr]
    if add is not None:
        in_specs.append(pl.BlockSpec((tm, tn), lambda i, j: (i, j)))
        args.append(add)
    return pl.pallas_call(
        body, name=name, grid=(m // tm, n // tn), in_specs=in_specs + tie_specs,
        out_specs=pl.BlockSpec((tm, tn), lambda i, j: (i, j)),
        out_shape=jax.ShapeDtypeStruct((m, n), out_dtype),
        compiler_params=_params(dimension_semantics=("arbitrary", "arbitrary")),
    )(*args, *tie_args)


def rowwise(fn, rows, fulls, outs, accs, name, tm=256, into=None, tie=None):
    s = rows[0][0].shape[0]
    nrow, nfull, nout, nacc = len(rows), len(fulls), len(outs), len(accs)
    nin = nrow + nfull

    def body(*refs):
        ins = [r[...] for r in refs[:nin]]
        res = fn(*ins)
        if not isinstance(res, (tuple, list)):
            res = (res,)
        orefs = refs[nin + (1 if into is not None else 0) + (1 if tie is not None else 0):]
        for k in range(nout):
            orefs[k][...] = res[k].astype(orefs[k].dtype)
        if nacc:
            @pl.when(pl.program_id(0) == 0)
            def _():
                for k in range(nacc):
                    orefs[nout + k][...] = jnp.zeros_like(orefs[nout + k])

            for k in range(nacc):
                orefs[nout + k][...] += res[nout + k].astype(orefs[nout + k].dtype)

    in_specs = [pl.BlockSpec((tm, w), lambda i, cb=cb: (i, cb)) for _, w, cb in rows]
    in_specs += [pl.BlockSpec((None,) + f.shape[1:], lambda i, ld=ld, nd=f.ndim - 1: (ld,) + (0,) * nd) for f, ld in fulls]
    args = [r[0] for r in rows] + [f for f, _ in fulls]
    aliases = {}
    if into is not None:
        in_specs.append(pl.BlockSpec(memory_space=pl.ANY))
        args.append(into[0])
        aliases = {nin: into[1]}
    if tie is not None:
        in_specs.append(pl.BlockSpec(memory_space=pl.ANY))
        args.append(tie)
    out_specs, out_shape = [], []
    for o in outs:
        w, dt = o[0], o[1]
        total, cb = (o[2], o[3]) if len(o) == 4 else (w, 0)
        out_specs.append(pl.BlockSpec((tm, w), lambda i, cb=cb: (i, cb)))
        out_shape.append(jax.ShapeDtypeStruct((s, total), dt))
    for shp, dt in accs:
        out_specs.append(pl.BlockSpec(shp, lambda i, nd=len(shp): (0,) * nd))
        out_shape.append(jax.ShapeDtypeStruct(shp, dt))
    return pl.pallas_call(
        body, name=name, grid=(s // tm,), in_specs=in_specs, out_specs=out_specs, out_shape=out_shape,
        input_output_aliases=aliases, compiler_params=_params(dimension_semantics=("arbitrary",)),
    )(*args)


def _rms(x, g):
    xf = x.astype(F32)
    var = jnp.mean(xf * xf, axis=-1, keepdims=True)
    return xf * lax.rsqrt(var + EPS) * g


def rmsnorm_fwd(x, g, name, width=None, colblock=0, out=None, into=None, tie=None):
    w = width or x.shape[1]
    return rowwise(lambda xt, gt: _rms(xt, gt), [(x, w, colblock)], [g], [out or (w, BF16)], [], name, into=into,
                   tie=tie)[0]


def rmsnorm_bwd(x, g, dh, name, resid=None, width=None, colblock=0, dh_colblock=0, dx_dtype=F32):
    w = width or x.shape[1]

    def fn(xt, dht, *rest):
        gt = rest[-1]
        _, vjp = jax.vjp(_rms, xt.astype(F32), gt)
        dx, dg = vjp(dht.astype(F32))
        if resid is not None:
            dx = dx + rest[0]
        return dx, dg

    rows = [(x, w, colblock), (dh, w, dh_colblock)] + ([(resid, w, 0)] if resid is not None else [])
    return rowwise(fn, rows, [g], [(w, dx_dtype)], [((1, w), F32)], name)


def _shift_down(u, k, row):
    return jnp.where(row >= k, pltpu.roll(u, k, 0), 0.0)


def _shift_up(u, k, row):
    s = u.shape[0]
    return jnp.where(row < s - k, pltpu.roll(u, s - k, 0), 0.0)


def _conv_fwd_tile(u, w_ref, b_ref, kw):
    row = lax.broadcasted_iota(jnp.int32, u.shape, 0)
    y = u * w_ref[kw - 1:kw, :] + b_ref[...]
    for k in range(1, kw):
        y = y + _shift_down(u, k, row) * w_ref[kw - 1 - k:kw - k, :]
    return y


def _conv_bwd_tile(u, dpre, w_ref, dw_ref, db_ref, kw):
    row = lax.broadcasted_iota(jnp.int32, u.shape, 0)
    du = dpre * w_ref[kw - 1:kw, :]
    dw_ref[kw - 1:kw, :] = jnp.sum(dpre * u, axis=0, keepdims=True)
    for k in range(1, kw):
        du = du + _shift_up(dpre, k, row) * w_ref[kw - 1 - k:kw - k, :]
        dw_ref[kw - 1 - k:kw - k, :] = jnp.sum(dpre * _shift_down(u, k, row), axis=0, keepdims=True)
    db_ref[...] = jnp.sum(dpre, axis=0, keepdims=True)
    return du


def _silu(x):
    return x * jax.nn.sigmoid(x)


def _dsilu(x):
    s = jax.nn.sigmoid(x)
    return s * (1.0 + x * (1.0 - s))


SSM_TC = 256


def ssm_conv_fwd(proj, cw, cb, l):
    s = proj.shape[0]
    off = OFF_XBC // SSM_TC

    def body(u_ref, w_ref, b_ref, o_ref):
        o_ref[...] = _silu(_conv_fwd_tile(u_ref[...], w_ref, b_ref, SSM_CONV))

    return pl.pallas_call(
        body, name="ssm_conv_fwd", grid=(CONV_CH // SSM_TC,),
        in_specs=[pl.BlockSpec((s, SSM_TC), lambda j: (0, off + j)),
                  pl.BlockSpec((None, SSM_CONV, SSM_TC), lambda j: (l, 0, j)),
                  pl.BlockSpec((None, 1, SSM_TC), lambda j: (l, 0, j))],
        out_specs=pl.BlockSpec((s, SSM_TC), lambda j: (0, j)),
        out_shape=jax.ShapeDtypeStruct((s, CONV_CH), F32),
        compiler_params=_params(dimension_semantics=("arbitrary",)),
    )(proj, cw, cb)


def ssm_conv_bwd(proj, cw, cb, l, dact):
    s = proj.shape[0]
    off = OFF_XBC // SSM_TC

    def body(u_ref, w_ref, b_ref, d_ref, du_ref, dw_ref, db_ref):
        u = u_ref[...]
        pre = _conv_fwd_tile(u, w_ref, b_ref, SSM_CONV)
        dpre = d_ref[...] * _dsilu(pre)
        du_ref[...] = _conv_bwd_tile(u, dpre, w_ref, dw_ref, db_ref, SSM_CONV).astype(du_ref.dtype)

    return pl.pallas_call(
        body, name="ssm_conv_bwd", grid=(CONV_CH // SSM_TC,),
        in_specs=[pl.BlockSpec((s, SSM_TC), lambda j: (0, off + j)),
                  pl.BlockSpec((None, SSM_CONV, SSM_TC), lambda j: (l, 0, j)),
                  pl.BlockSpec((None, 1, SSM_TC), lambda j: (l, 0, j)), pl.BlockSpec((s, SSM_TC), lambda j: (0, j))],
        out_specs=[pl.BlockSpec((s, SSM_TC), lambda j: (0, j)), pl.BlockSpec((SSM_CONV, SSM_TC), lambda j: (0, j)),
                   pl.BlockSpec((1, SSM_TC), lambda j: (0, j))],
        out_shape=[jax.ShapeDtypeStruct((s, CONV_CH), BF16), jax.ShapeDtypeStruct((SSM_CONV, CONV_CH), F32),
                   jax.ShapeDtypeStruct((1, CONV_CH), F32)],
        compiler_params=_params(dimension_semantics=("arbitrary",)),
    )(proj, cw, cb, dact)


FFN_TC = 256
FFN_NT = D_FF // FFN_TC


def _ffn_specs(s, l):
    blk = pl.BlockSpec((s, FFN_TC), lambda j: (0, j))
    wg = pl.BlockSpec((None, FFN_CONV, FFN_TC), lambda j: (l, 0, j))
    wv = pl.BlockSpec((None, FFN_CONV, FFN_TC), lambda j: (l, 0, FFN_NT + j))
    bg = pl.BlockSpec((None, 1, FFN_TC), lambda j: (l, 0, j))
    bv = pl.BlockSpec((None, 1, FFN_TC), lambda j: (l, 0, FFN_NT + j))
    return blk, wg, wv, bg, bv


def ffn_act_fwd(ug, uv, cw, cb, l):
    s = ug.shape[0]

    def body(g_ref, v_ref, wg_ref, wv_ref, bg_ref, bv_ref, o_ref):
        cg = _conv_fwd_tile(g_ref[...], wg_ref, bg_ref, FFN_CONV)
        cv = _conv_fwd_tile(v_ref[...], wv_ref, bv_ref, FFN_CONV)
        o_ref[...] = (_silu(cg) * cv).astype(o_ref.dtype)

    blk, wg, wv, bg, bv = _ffn_specs(s, l)
    return pl.pallas_call(
        body, name="ffn_act_fwd", grid=(FFN_NT,), in_specs=[blk, blk, wg, wv, bg, bv],
        out_specs=blk, out_shape=jax.ShapeDtypeStruct((s, D_FF), BF16),
        compiler_params=_params(dimension_semantics=("arbitrary",)),
    )(ug, uv, cw, cw, cb, cb)


def ffn_act_bwd(ug, uv, cw, cb, l, da):
    s = ug.shape[0]

    def body(g_ref, v_ref, wg_ref, wv_ref, bg_ref, bv_ref, da_ref, dg_ref, dv_ref, dwg_ref, dwv_ref, dbg_ref, dbv_ref):
        g, v = g_ref[...], v_ref[...]
        cg = _conv_fwd_tile(g, wg_ref, bg_ref, FFN_CONV)
        cv = _conv_fwd_tile(v, wv_ref, bv_ref, FFN_CONV)
        da_t = da_ref[...].astype(F32)
        dcg = da_t * cv * _dsilu(cg)
        dcv = da_t * _silu(cg)
        dg_ref[...] = _conv_bwd_tile(g, dcg, wg_ref, dwg_ref, dbg_ref, FFN_CONV).astype(dg_ref.dtype)
        dv_ref[...] = _conv_bwd_tile(v, dcv, wv_ref, dwv_ref, dbv_ref, FFN_CONV).astype(dv_ref.dtype)

    blk, wg, wv, bg, bv = _ffn_specs(s, l)
    wblk = pl.BlockSpec((FFN_CONV, FFN_TC), lambda j: (0, j))
    bblk = pl.BlockSpec((1, FFN_TC), lambda j: (0, j))
    return pl.pallas_call(
        body, name="ffn_act_bwd", grid=(FFN_NT,), in_specs=[blk, blk, wg, wv, bg, bv, blk],
        out_specs=[blk, blk, wblk, wblk, bblk, bblk],
        out_shape=[jax.ShapeDtypeStruct((s, D_FF), BF16), jax.ShapeDtypeStruct((s, D_FF), BF16),
                   jax.ShapeDtypeStruct((FFN_CONV, D_FF), F32), jax.ShapeDtypeStruct((FFN_CONV, D_FF), F32),
                   jax.ShapeDtypeStruct((1, D_FF), F32), jax.ShapeDtypeStruct((1, D_FF), F32)],
        compiler_params=_params(dimension_semantics=("arbitrary",)),
    )(ug, uv, cw, cw, cb, cb, da)


def _dot(a, b, mode):
    return lax.dot_general(a.astype(BF16), b.astype(BF16), _DIMS[mode], preferred_element_type=F32)


@jax.custom_vjp
def mm_nn(a, b):
    return _dot(a, b, 'nn')


@jax.custom_vjp
def mm_nt(a, b):
    return _dot(a, b, 'nt')


@jax.custom_vjp
def mm_tn(a, b):
    return _dot(a, b, 'tn')


mm_nn.defvjp(lambda a, b: (_dot(a, b, 'nn'), (a, b)), lambda r, g: (_dot(g, r[1], 'nt'), _dot(r[0], g, 'tn')))
mm_nt.defvjp(lambda a, b: (_dot(a, b, 'nt'), (a, b)), lambda r, g: (_dot(g, r[1], 'nn'), _dot(g, r[0], 'tn')))
mm_tn.defvjp(lambda a, b: (_dot(a, b, 'tn'), (a, b)), lambda r, g: (_dot(r[1], g, 'nt'), _dot(r[0], g, 'nn')))


def _tri(n, lower):
    r = lax.broadcasted_iota(jnp.int32, (n, n), 0)
    c = lax.broadcasted_iota(jnp.int32, (n, n), 1)
    return jnp.where((r >= c) if lower else (r <= c), 1.0, 0.0).astype(F32)


def _tri_dot(a, lower):
    return jnp.dot(_tri(a.shape[0], lower), a, precision=lax.Precision.HIGHEST, preferred_element_type=F32)


@jax.custom_vjp
def _cumsum_rows(a):
    return _tri_dot(a, True)


_cumsum_rows.defvjp(lambda a: (_tri_dot(a, True), None), lambda _, g: (_tri_dot(g, False),))


def _softplus(x):
    return jnp.maximum(x, 0.0) + jnp.log(1.0 + jnp.exp(-jnp.abs(x)))


def _ssd_chunk(xs, bs, cs, small, dtb, alog, dsk, prev):
    ln = small.shape[0]
    lane = lax.broadcasted_iota(jnp.int32, (ln, LANES), 1)
    lane1 = lax.broadcasted_iota(jnp.int32, (1, LANES), 1)
    sub = lax.broadcasted_iota(jnp.int32, (LANES, ln), 0)
    rowi = lax.broadcasted_iota(jnp.int32, (ln, LANES), 0)
    tril = lax.broadcasted_iota(jnp.int32, (ln, ln), 0) >= lax.broadcasted_iota(jnp.int32, (ln, ln), 1)
    first = lane < SSM_HEAD_DIM
    first1 = lane1 < SSM_HEAD_DIM

    dt = _softplus(small + dtb)
    acs = _cumsum_rows(dt * (-jnp.exp(alog)))
    acs_t = acs.T
    last = jnp.sum(jnp.where(rowi == ln - 1, acs, 0.0), axis=0, keepdims=True)

    def col(a, h):
        return jnp.sum(jnp.where(lane == h, a, 0.0), axis=1, keepdims=True)

    def one(a, h):
        return jnp.sum(jnp.where(lane1 == h, a, 0.0), axis=1, keepdims=True)

    def rowv(at, h):
        return jnp.sum(jnp.where(sub == h, at, 0.0), axis=0, keepdims=True)

    cb = [mm_nt(cs[g], bs[g]) for g in range(SSM_GROUPS)]
    ys, news = [], []
    for j in range(SSM_HEADS // 2):
        g = j // 2
        h0, h1 = 2 * j, 2 * j + 1
        xd = xs[j] * jnp.where(first, col(dt, h0), col(dt, h1))
        yd, st, ea, cd = None, None, [], []
        for h, xdh in ((h0, jnp.where(first, xd, 0.0)), (h1, jnp.where(first, 0.0, xd))):
            ac = col(acs, h)
            la = one(last, h)
            lmat = jnp.exp(jnp.where(tril, ac - rowv(acs_t, h), -jnp.inf))
            yh = mm_nn(cb[g] * lmat, xdh)
            sh = mm_tn(bs[g] * jnp.exp(la - ac), xdh)
            yd = yh if yd is None else yd + yh
            st = sh if st is None else st + sh
            ea.append(jnp.exp(ac))
            cd.append(jnp.exp(la))
        yoff = mm_nn(cs[g], prev[j]) * jnp.where(first, ea[0], ea[1])
        ys.append(yd + yoff + xs[j] * jnp.where(first1, one(dsk, h0), one(dsk, h1)))
        news.append(prev[j] * jnp.where(first1, cd[0], cd[1]) + st)
    return ys, news


N_PAIR = SSM_HEADS // 2


def ssd_fwd(xbc, proj, ptile, l):
    s = xbc.shape[0]
    nch = s // SSM_CHUNK

    def body(xbc_ref, small_ref, p_ref, y_ref, prev_ref, state_ref):
        @pl.when(pl.program_id(0) == 0)
        def _():
            state_ref[...] = jnp.zeros_like(state_ref)

        xs = [xbc_ref[:, LANES * j:LANES * (j + 1)] for j in range(N_PAIR)]
        bs = [xbc_ref[:, D_SSM + LANES * g:D_SSM + LANES * (g + 1)] for g in range(SSM_GROUPS)]
        cs = [xbc_ref[:, D_SSM + 512 + LANES * g:D_SSM + 512 + LANES * (g + 1)] for g in range(SSM_GROUPS)]
        prev = [state_ref[j] for j in range(N_PAIR)]
        ys, news = _ssd_chunk(xs, bs, cs, small_ref[...], p_ref[0:1, :], p_ref[1:2, :], p_ref[2:3, :], prev)
        for j in range(N_PAIR):
            y_ref[:, LANES * j:LANES * (j + 1)] = ys[j]
            prev_ref[0, j] = prev[j]
            state_ref[j] = news[j]

    return pl.pallas_call(
        body, name="ssd_fwd", grid=(nch,),
        in_specs=[pl.BlockSpec((SSM_CHUNK, CONV_CH), lambda c: (c, 0)),
                  pl.BlockSpec((SSM_CHUNK, LANES), lambda c: (c, OFF_SMALL // LANES)),
                  pl.BlockSpec((None, 8, LANES), lambda c: (l, 0, 0))],
        out_specs=[pl.BlockSpec((SSM_CHUNK, D_SSM), lambda c: (c, 0)),
                   pl.BlockSpec((1, N_PAIR, SSM_STATE, LANES), lambda c: (c, 0, 0, 0))],
        out_shape=[jax.ShapeDtypeStruct((s, D_SSM), F32), jax.ShapeDtypeStruct((nch, N_PAIR, SSM_STATE, LANES), F32)],
        scratch_shapes=[pltpu.VMEM((N_PAIR, SSM_STATE, LANES), F32)],
        compiler_params=_params(dimension_semantics=("arbitrary",)),
    )(xbc, proj, ptile)


def ssd_bwd(xbc, proj, ptile, l, prevs, dy):
    s = xbc.shape[0]
    nch = s // SSM_CHUNK

    def body(xbc_ref, small_ref, p_ref, prev_ref, dy_ref, dxbc_ref, dsmall_ref, dp_ref, dstate_ref):
        @pl.when(pl.program_id(0) == 0)
        def _():
            dstate_ref[...] = jnp.zeros_like(dstate_ref)
            dp_ref[...] = jnp.zeros_like(dp_ref)

        xs = [xbc_ref[:, LANES * j:LANES * (j + 1)] for j in range(N_PAIR)]
        bs = [xbc_ref[:, D_SSM + LANES * g:D_SSM + LANES * (g + 1)] for g in range(SSM_GROUPS)]
        cs = [xbc_ref[:, D_SSM + 512 + LANES * g:D_SSM + 512 + LANES * (g + 1)] for g in range(SSM_GROUPS)]
        prev = [prev_ref[0, j] for j in range(N_PAIR)]
        dys = [dy_ref[:, LANES * j:LANES * (j + 1)] for j in range(N_PAIR)]
        dnew = [dstate_ref[j] for j in range(N_PAIR)]
        _, vjp = jax.vjp(_ssd_chunk, xs, bs, cs, small_ref[...], p_ref[0:1, :], p_ref[1:2, :], p_ref[2:3, :], prev)
        dxs, dbs, dcs, dsmall, ddtb, dalog, ddsk, dprev = vjp((dys, dnew))
        for j in range(N_PAIR):
            dxbc_ref[:, LANES * j:LANES * (j + 1)] = dxs[j]
            dstate_ref[j] = dprev[j]
        for g in range(SSM_GROUPS):
            dxbc_ref[:, D_SSM + LANES * g:D_SSM + LANES * (g + 1)] = dbs[g]
            dxbc_ref[:, D_SSM + 512 + LANES * g:D_SSM + 512 + LANES * (g + 1)] = dcs[g]
        dsmall_ref[...] = dsmall
        dp_ref[0:1, :] += ddtb
        dp_ref[1:2, :] += dalog
        dp_ref[2:3, :] += ddsk

    rev = lambda c: nch - 1 - c
    return pl.pallas_call(
        body, name="ssd_bwd", grid=(nch,),
        in_specs=[pl.BlockSpec((SSM_CHUNK, CONV_CH), lambda c: (rev(c), 0)),
                  pl.BlockSpec((SSM_CHUNK, LANES), lambda c: (rev(c), OFF_SMALL // LANES)),
                  pl.BlockSpec((None, 8, LANES), lambda c: (l, 0, 0)),
                  pl.BlockSpec((1, N_PAIR, SSM_STATE, LANES), lambda c: (rev(c), 0, 0, 0)),
                  pl.BlockSpec((SSM_CHUNK, D_SSM), lambda c: (rev(c), 0))],
        out_specs=[pl.BlockSpec((SSM_CHUNK, CONV_CH), lambda c: (rev(c), 0)),
                   pl.BlockSpec((SSM_CHUNK, LANES), lambda c: (rev(c), 0)),
                   pl.BlockSpec((8, LANES), lambda c: (0, 0))],
        out_shape=[jax.ShapeDtypeStruct((s, CONV_CH), F32), jax.ShapeDtypeStruct((s, LANES), F32),
                   jax.ShapeDtypeStruct((8, LANES), F32)],
        scratch_shapes=[pltpu.VMEM((N_PAIR, SSM_STATE, LANES), F32)],
        compiler_params=_params(dimension_semantics=("arbitrary",)),
    )(xbc, proj, ptile, prevs, dy)


ROPE_TM = 256


def _rope_tile(t, cosm, sinm):
    lane = lax.broadcasted_iota(jnp.int32, t.shape, 1)
    half = QK_ROPE // 2
    partner = jnp.where(lane < ROPE_LANE0 + half, pltpu.roll(t, LANES - half, 1), pltpu.roll(t, half, 1))
    return t * cosm + partner * sinm


def _in_rope(shape):
    lane = lax.broadcasted_iota(jnp.int32, shape, 1)
    return jnp.logical_and(lane >= ROPE_LANE0, lane < ROPE_LANE0 + QK_ROPE)


def rope_q(q, cosm, sinm, name):
    s, w = q.shape

    def body(q_ref, c_ref, s_ref, o_ref):
        c, sn = c_ref[...], s_ref[...]
        for h in range(MLA_HEADS):
            sl = slice(HEAD_PAD * h, HEAD_PAD * (h + 1))
            o_ref[:, sl] = _rope_tile(q_ref[:, sl].astype(F32), c, sn).astype(o_ref.dtype)

    row = pl.BlockSpec((ROPE_TM, w), lambda i: (i, 0))
    tab = pl.BlockSpec((ROPE_TM, LANES), lambda i: (i, 0))
    return pl.pallas_call(
        body, name=name, grid=(s // ROPE_TM,), in_specs=[row, tab, tab], out_specs=row,
        out_shape=jax.ShapeDtypeStruct((s, w), BF16), compiler_params=_params(dimension_semantics=("arbitrary",)),
    )(q, cosm, sinm)


def build_k(kn, proj, cosm, sinm):
    s, w = kn.shape

    def body(k_ref, small_ref, c_ref, s_ref, o_ref):
        small = small_ref[...]
        inrope = _in_rope(small.shape)
        kpe = jnp.where(inrope, _rope_tile(jnp.where(inrope, small, 0.0), c_ref[...], s_ref[...]), 0.0)
        for h in range(MLA_HEADS):
            sl = slice(HEAD_PAD * h, HEAD_PAD * (h + 1))
            o_ref[:, sl] = (k_ref[:, sl].astype(F32) + kpe).astype(o_ref.dtype)

    row = pl.BlockSpec((ROPE_TM, w), lambda i: (i, 0))
    tab = pl.BlockSpec((ROPE_TM, LANES), lambda i: (i, 0))
    return pl.pallas_call(
        body, name="build_k", grid=(s // ROPE_TM,),
        in_specs=[row, pl.BlockSpec((ROPE_TM, LANES), lambda i: (i, OFF_SMALL // LANES)), tab, tab], out_specs=row,
        out_shape=jax.ShapeDtypeStruct((s, w), BF16), compiler_params=_params(dimension_semantics=("arbitrary",)),
    )(kn, proj, cosm, sinm)


def dsmall_bwd(dk, dsmall_ssd, cosm, sinm_neg):
    def fn(dkt, ds, c, sn):
        inrope = _in_rope(ds.shape)
        tot = dkt[:, 0:HEAD_PAD]
        for h in range(1, MLA_HEADS):
            tot = tot + dkt[:, HEAD_PAD * h:HEAD_PAD * (h + 1)]
        tot = jnp.where(inrope, tot, 0.0)
        return ds + jnp.where(inrope, _rope_tile(tot, c, sn), 0.0)

    return rowwise(fn, [(dk, MLA_HEADS * HEAD_PAD, 0), (dsmall_ssd, LANES, 0), (cosm, LANES, 0), (sinm_neg, LANES, 0)],
                   [], [(LANES, BF16)], [], "dsmall_bwd")[0]


ATT_TQ = 256
ATT_SCALE = (QK_NOPE + QK_ROPE) ** -0.5


def _att_scores(qh, kh, diag):
    s = lax.dot_general(qh, kh, _DIMS['nt'], preferred_element_type=F32) * ATT_SCALE
    if diag:
        r = lax.broadcasted_iota(jnp.int32, s.shape, 0)
        c = lax.broadcasted_iota(jnp.int32, s.shape, 1)
        s = jnp.where(c <= r, s, -1e30)
    return s


def mla_fwd(q, k, v):
    s = q.shape[0]

    def body(q_ref, k_ref, v_ref, o_ref, lse_ref):
        i = pl.program_id(1)
        lane = lax.broadcasted_iota(jnp.int32, (ATT_TQ, LANES), 1)
        o_tot, lse_tot = None, None
        for h in range(2):
            hs = slice(HEAD_PAD * h, HEAD_PAD * (h + 1))
            sel = (lane < V_DIM) if h == 0 else (lane >= V_DIM)
            qh = q_ref[:, hs]

            def chunk(c, carry, diag, hs=hs, sel=sel, qh=qh):
                m, l, acc = carry
                k0 = pl.multiple_of(c * ATT_TQ, ATT_TQ)
                sc = _att_scores(qh, k_ref[pl.ds(k0, ATT_TQ), hs], diag)
                vc = v_ref[pl.ds(k0, ATT_TQ), :]
                vh = jnp.where(sel, vc, jnp.zeros_like(vc))
                m_new = jnp.maximum(m, jnp.max(sc, axis=1, keepdims=True))
                alpha = jnp.exp(m - m_new)
                p = jnp.exp(sc - m_new)
                l = alpha * l + jnp.sum(p, axis=1, keepdims=True)
                acc = alpha * acc + lax.dot_general(p.astype(BF16), vh, _DIMS['nn'], preferred_element_type=F32)
                return m_new, l, acc

            init = (jnp.full((ATT_TQ, 1), -1e30, F32), jnp.zeros((ATT_TQ, 1), F32), jnp.zeros((ATT_TQ, LANES), F32))
            carry = lax.fori_loop(0, i, lambda c, cr: chunk(c, cr, False), init)
            m, l, acc = chunk(i, carry, True)
            oh = acc / l
            lse_h = jnp.where(sel, m + jnp.log(l), 0.0)
            o_tot = oh if o_tot is None else o_tot + oh
            lse_tot = lse_h if lse_tot is None else lse_tot + lse_h
        o_ref[...] = o_tot
        lse_ref[...] = lse_tot

    tile = pl.BlockSpec((ATT_TQ, LANES), lambda p, i: (i, p))
    return pl.pallas_call(
        body, name="mla_fwd", grid=(MLA_HEADS // 2, s // ATT_TQ),
        in_specs=[pl.BlockSpec((ATT_TQ, 2 * HEAD_PAD), lambda p, i: (i, p)),
                  pl.BlockSpec((s, 2 * HEAD_PAD), lambda p, i: (0, p)),
                  pl.BlockSpec((s, LANES), lambda p, i: (0, p))],
        out_specs=[tile, tile],
        out_shape=[jax.ShapeDtypeStruct((s, MLA_HEADS * V_DIM), F32)] * 2,
        compiler_params=_params(dimension_semantics=("arbitrary", "arbitrary")),
    )(q, k, v)


def mla_bwd(q, k, v, o, lse, do):
    s = q.shape[0]

    def body(q_ref, k_ref, v_ref, o_ref, lse_ref, do_ref, dq_ref, dk_ref, dv_ref):
        i = pl.program_id(1)

        @pl.when(i == 0)
        def _():
            dk_ref[...] = jnp.zeros_like(dk_ref)
            dv_ref[...] = jnp.zeros_like(dv_ref)

        o_t = o_ref[...]
        do_t = do_ref[...]
        lse_t = lse_ref[...]
        lane = lax.broadcasted_iota(jnp.int32, do_t.shape, 1)
        for h in range(2):
            hs = slice(HEAD_PAD * h, HEAD_PAD * (h + 1))
            sel = (lane < V_DIM) if h == 0 else (lane >= V_DIM)
            qh = q_ref[:, hs]
            doh = jnp.where(sel, do_t, 0.0)
            delta = jnp.sum(doh * o_t, axis=1, keepdims=True)
            lse_h = jnp.max(jnp.where(sel, lse_t, -jnp.inf), axis=1, keepdims=True)
            doh_b = doh.astype(BF16)

            def chunk(c, dq, diag, hs=hs, qh=qh, delta=delta, lse_h=lse_h, doh_b=doh_b):
                k0 = pl.multiple_of(c * ATT_TQ, ATT_TQ)
                kh = k_ref[pl.ds(k0, ATT_TQ), hs]
                p = jnp.exp(_att_scores(qh, kh, diag) - lse_h)
                dv_ref[pl.ds(k0, ATT_TQ), :] += lax.dot_general(p.astype(BF16), doh_b, _DIMS['tn'], preferred_element_type=F32)
                dp = lax.dot_general(doh_b, v_ref[pl.ds(k0, ATT_TQ), :], _DIMS['nt'], preferred_element_type=F32)
                ds = (p * (dp - delta) * ATT_SCALE).astype(BF16)
                dk_ref[pl.ds(k0, ATT_TQ), hs] += lax.dot_general(ds, qh, _DIMS['tn'], preferred_element_type=F32)
                return dq + lax.dot_general(ds, kh, _DIMS['nn'], preferred_element_type=F32)

            dq = lax.fori_loop(0, i, lambda c, a: chunk(c, a, False), jnp.zeros((ATT_TQ, HEAD_PAD), F32))
            dq_ref[:, hs] = chunk(i, dq, True).astype(dq_ref.dtype)

    tile = pl.BlockSpec((ATT_TQ, LANES), lambda p, i: (i, p))
    return pl.pallas_call(
        body, name="mla_bwd", grid=(MLA_HEADS // 2, s // ATT_TQ),
        in_specs=[pl.BlockSpec((ATT_TQ, 2 * HEAD_PAD), lambda p, i: (i, p)),
                  pl.BlockSpec((s, 2 * HEAD_PAD), lambda p, i: (0, p)),
                  pl.BlockSpec((s, LANES), lambda p, i: (0, p)), tile, tile, tile],
        out_specs=[pl.BlockSpec((ATT_TQ, 2 * HEAD_PAD), lambda p, i: (i, p)),
                   pl.BlockSpec((s, 2 * HEAD_PAD), lambda p, i: (0, p)),
                   pl.BlockSpec((s, LANES), lambda p, i: (0, p))],
        out_shape=[jax.ShapeDtypeStruct((s, MLA_HEADS * HEAD_PAD), F32),
                   jax.ShapeDtypeStruct((s, MLA_HEADS * HEAD_PAD), F32),
                   jax.ShapeDtypeStruct((s, MLA_HEADS * V_DIM), F32)],
        compiler_params=_params(dimension_semantics=("arbitrary", "arbitrary")),
    )(q, k, v, o, lse, do)


MEM_TQ = 256
MEM_SCALE = MEM_HEAD_DIM ** -0.5


def _mem_probs(qh, kh):
    s = lax.dot_general(qh, kh, _DIMS['nt'], preferred_element_type=F32) * MEM_SCALE
    p = jnp.exp(s - jnp.max(s, axis=1, keepdims=True))
    return p / jnp.sum(p, axis=1, keepdims=True)


def mem_fwd(q, k, v):
    s = q.shape[0]

    def body(q_ref, k_ref, v_ref, o_ref):
        for h in range(MEM_HEADS):
            sl = slice(MEM_HEAD_DIM * h, MEM_HEAD_DIM * (h + 1))
            p = _mem_probs(q_ref[:, sl], k_ref[:, sl])
            o_ref[:, sl] = lax.dot_general(p.astype(BF16), v_ref[:, sl], _DIMS['nn'],
                                           preferred_element_type=F32).astype(o_ref.dtype)

    full = pl.BlockSpec((MEM_LEN, D_MODEL), lambda i: (0, 0))
    return pl.pallas_call(
        body, name="mem_fwd", grid=(s // MEM_TQ,),
        in_specs=[pl.BlockSpec((MEM_TQ, D_MODEL), lambda i: (i, 0)), full, full],
        out_specs=pl.BlockSpec((MEM_TQ, D_MODEL), lambda i: (i, 0)),
        out_shape=jax.ShapeDtypeStruct((s, D_MODEL), BF16),
        compiler_params=_params(dimension_semantics=("arbitrary",)),
    )(q, k, v)


def mem_bwd(q, k, v, do):
    s = q.shape[0]

    def body(q_ref, k_ref, v_ref, do_ref, dq_ref, dk_ref, dv_ref):
        @pl.when(pl.program_id(0) == 0)
        def _():
            dk_ref[...] = jnp.zeros_like(dk_ref)
            dv_ref[...] = jnp.zeros_like(dv_ref)

        for h in range(MEM_HEADS):
            sl = slice(MEM_HEAD_DIM * h, MEM_HEAD_DIM * (h + 1))
            qh, kh, vh = q_ref[:, sl], k_ref[:, sl], v_ref[:, sl]
            doh = do_ref[:, sl].astype(BF16)
            p = _mem_probs(qh, kh)
            dv_ref[:, sl] += lax.dot_general(p.astype(BF16), doh, _DIMS['tn'], preferred_element_type=F32)
            dp = lax.dot_general(doh, vh, _DIMS['nt'], preferred_element_type=F32)
            ds = (p * (dp - jnp.sum(p * dp, axis=1, keepdims=True)) * MEM_SCALE).astype(BF16)
            dq_ref[:, sl] = lax.dot_general(ds, kh, _DIMS['nn'], preferred_element_type=F32).astype(dq_ref.dtype)
            dk_ref[:, sl] += lax.dot_general(ds, qh, _DIMS['tn'], preferred_element_type=F32)

    full = pl.BlockSpec((MEM_LEN, D_MODEL), lambda i: (0, 0))
    row = pl.BlockSpec((MEM_TQ, D_MODEL), lambda i: (i, 0))
    return pl.pallas_call(
        body, name="mem_bwd", grid=(s // MEM_TQ,),
        in_specs=[row, full, full, row], out_specs=[row, full, full],
        out_shape=[jax.ShapeDtypeStruct((s, D_MODEL), BF16), jax.ShapeDtypeStruct((MEM_LEN, D_MODEL), F32),
                   jax.ShapeDtypeStruct((MEM_LEN, D_MODEL), F32)],
        compiler_params=_params(dimension_semantics=("arbitrary",)),
    )(q, k, v, do)


def _gate_norm(y, z, g):
    return _rms(y * _silu(z), g)


def gate_norm_fwd(y, proj, g):
    return rowwise(_gate_norm, [(y, D_SSM, 0), (proj, D_SSM, OFF_Z // D_SSM)], [g], [(D_SSM, BF16, D_MIX, 0)], [],
                   "gate_norm_fwd")[0]


def gate_norm_bwd(y, proj, g, dmix):
    def fn(yt, zt, dt_, gt):
        _, vjp = jax.vjp(_gate_norm, yt, zt, gt)
        return vjp(dt_.astype(F32))

    return rowwise(fn, [(y, D_SSM, 0), (proj, D_SSM, OFF_Z // D_SSM), (dmix, D_SSM, 0)], [g],
                   [(D_SSM, F32), (D_SSM, BF16)], [((1, D_SSM), F32)], "gate_norm_bwd")


def loss_head(x, g, target):
    def fn(xt, tt, gt):
        def f(x_, g_):
            err = _rms(x_, g_) - tt
            return 0.5 * jnp.sum(jnp.mean(err * err, axis=-1))

        lv, (dx, dg) = jax.value_and_grad(f, argnums=(0, 1))(xt, gt)
        return dx, dg, jnp.full((1, LANES), lv, F32)

    return rowwise(fn, [(x, D_MODEL, 0), (target, D_MODEL, 0)], [g], [(D_MODEL, F32)],
                   [((1, D_MODEL), F32), ((1, LANES), F32)], "loss_head")


def _proj_runs(d):
    lo, hi = (D_IN // N_DEV) * d, (D_IN // N_DEV) * (d + 1)
    runs = []
    for a, b, new in PROJ_SEGS:
        s0, s1 = max(a, lo), min(b, hi)
        if s0 < s1:
            runs.append((s0 - lo, new + s0 - a, s1 - s0))
    return runs


LAYOUT_TM = 256


def assemble_proj(g):
    def body(g_ref, o_ref):
        o_ref[:, OFF_SMALL:OFF_SMALL + LANES] = jnp.zeros((LAYOUT_TM, LANES), o_ref.dtype)
        for d in range(N_DEV):
            for src, dst, n in _proj_runs(d):
                o_ref[:, dst:dst + n] = g_ref[d, :, src:src + n]

    return pl.pallas_call(
        body, name="assemble_proj", grid=(D_MODEL // LAYOUT_TM,),
        in_specs=[pl.BlockSpec((N_DEV, LAYOUT_TM, D_IN // N_DEV), lambda i: (0, i, 0))],
        out_specs=pl.BlockSpec((LAYOUT_TM, PROJ_W), lambda i: (i, 0)),
        out_shape=jax.ShapeDtypeStruct((D_MODEL, PROJ_W), g.dtype),
        compiler_params=_params(dimension_semantics=("arbitrary",)),
    )(g)


def extract_proj(dz, dxbc, dcq, dsmall, dckv):
    pieces = [(OFF_Z, 1024), (OFF_XBC, 2048), (OFF_CQ, Q_LORA), (OFF_SMALL, LANES), (OFF_CKV, KV_LORA)]

    def body(*refs):
        o_ref = refs[-1]
        for d in range(N_DEV):
            for src, dst, n in _proj_runs(d):
                for p, (off, w) in enumerate(pieces):
                    if off <= dst < off + w:
                        o_ref[d, :, src:src + n] = refs[p][:, dst - off:dst - off + n].astype(o_ref.dtype)

    return pl.pallas_call(
        body, name="extract_proj", grid=(D_MODEL // LAYOUT_TM,),
        in_specs=[pl.BlockSpec((LAYOUT_TM, w), lambda i: (i, 0)) for _, w in pieces],
        out_specs=pl.BlockSpec((N_DEV, LAYOUT_TM, D_IN // N_DEV), lambda i: (0, i, 0)),
        out_shape=jax.ShapeDtypeStruct((N_DEV, D_MODEL, D_IN // N_DEV), BF16),
        compiler_params=_params(dimension_semantics=("arbitrary",)),
    )(dz, dxbc, dcq, dsmall, dckv)


_QW = QK_NOPE + QK_ROPE


def assemble_uq(g):
    def body(g_ref, o_ref):
        o_ref[...] = jnp.zeros_like(o_ref)
        for d in range(N_DEV):
            for e in range(2):
                dst = HEAD_PAD * (2 * d + e)
                o_ref[:, dst:dst + _QW] = g_ref[d, :, _QW * e:_QW * (e + 1)]

    return pl.pallas_call(
        body, name="assemble_uq", grid=(1,),
        in_specs=[pl.BlockSpec((N_DEV, Q_LORA, 2 * _QW), lambda i: (0, 0, 0))],
        out_specs=pl.BlockSpec((Q_LORA, MLA_HEADS * HEAD_PAD), lambda i: (0, 0)),
        out_shape=jax.ShapeDtypeStruct((Q_LORA, MLA_HEADS * HEAD_PAD), g.dtype),
        compiler_params=_params(dimension_semantics=("arbitrary",)),
    )(g)


def extract_uq(dw):
    def body(w_ref, o_ref):
        for d in range(N_DEV):
            for e in range(2):
                src = HEAD_PAD * (2 * d + e)
                o_ref[d, :, _QW * e:_QW * (e + 1)] = w_ref[:, src:src + _QW].astype(o_ref.dtype)

    return pl.pallas_call(
        body, name="extract_uq", grid=(1,),
        in_specs=[pl.BlockSpec((Q_LORA, MLA_HEADS * HEAD_PAD), lambda i: (0, 0))],
        out_specs=pl.BlockSpec((N_DEV, Q_LORA, 2 * _QW), lambda i: (0, 0, 0)),
        out_shape=jax.ShapeDtypeStruct((N_DEV, Q_LORA, 2 * _QW), BF16),
        compiler_params=_params(dimension_semantics=("arbitrary",)),
    )(dw)


def assemble_ukv(g):
    def body(g_ref, kn_ref, v_ref):
        kn_ref[...] = jnp.zeros_like(kn_ref)
        for d in range(N_DEV):
            for e in range(2):
                h = 2 * d + e
                kn_ref[:, HEAD_PAD * h:HEAD_PAD * h + QK_NOPE] = g_ref[d, :, 128 * e:128 * e + QK_NOPE]
                v_ref[:, V_DIM * h:V_DIM * (h + 1)] = g_ref[d, :, 128 * e + QK_NOPE:128 * (e + 1)]

    return pl.pallas_call(
        body, name="assemble_ukv", grid=(1,),
        in_specs=[pl.BlockSpec((N_DEV, KV_LORA, 256), lambda i: (0, 0, 0))],
        out_specs=[pl.BlockSpec((KV_LORA, MLA_HEADS * HEAD_PAD), lambda i: (0, 0)),
                   pl.BlockSpec((KV_LORA, MLA_HEADS * V_DIM), lambda i: (0, 0))],
        out_shape=[jax.ShapeDtypeStruct((KV_LORA, MLA_HEADS * HEAD_PAD), g.dtype),
                   jax.ShapeDtypeStruct((KV_LORA, MLA_HEADS * V_DIM), g.dtype)],
        compiler_params=_params(dimension_semantics=("arbitrary",)),
    )(g)


def extract_ukv(dkn, dv):
    def body(kn_ref, v_ref, o_ref):
        for d in range(N_DEV):
            for e in range(2):
                h = 2 * d + e
                o_ref[d, :, 128 * e:128 * e + QK_NOPE] = kn_ref[:, HEAD_PAD * h:HEAD_PAD * h + QK_NOPE].astype(o_ref.dtype)
                o_ref[d, :, 128 * e + QK_NOPE:128 * (e + 1)] = v_ref[:, V_DIM * h:V_DIM * (h + 1)].astype(o_ref.dtype)

    return pl.pallas_call(
        body, name="extract_ukv", grid=(1,),
        in_specs=[pl.BlockSpec((KV_LORA, MLA_HEADS * HEAD_PAD), lambda i: (0, 0)),
                  pl.BlockSpec((KV_LORA, MLA_HEADS * V_DIM), lambda i: (0, 0))],
        out_specs=pl.BlockSpec((N_DEV, KV_LORA, 256), lambda i: (0, 0, 0)),
        out_shape=jax.ShapeDtypeStruct((N_DEV, KV_LORA, 256), BF16),
        compiler_params=_params(dimension_semantics=("arbitrary",)),
    )(dkn, dv)


_UPW = 2 * D_FF // N_DEV


def assemble_up(g):
    def body(g_ref, wg_ref, wv_ref):
        for d in range(N_DEV):
            ref = wg_ref if d < N_DEV // 2 else wv_ref
            off = _UPW * (d % (N_DEV // 2))
            ref[:, off:off + _UPW] = g_ref[d]

    half = pl.BlockSpec((LAYOUT_TM, D_FF), lambda i: (i, 0))
    return pl.pallas_call(
        body, name="assemble_up", grid=(D_MODEL // LAYOUT_TM,),
        in_specs=[pl.BlockSpec((N_DEV, LAYOUT_TM, _UPW), lambda i: (0, i, 0))],
        out_specs=[half, half], out_shape=[jax.ShapeDtypeStruct((D_MODEL, D_FF), g.dtype)] * 2,
        compiler_params=_params(dimension_semantics=("arbitrary",)),
    )(g)


def extract_up(dwg, dwv):
    def body(wg_ref, wv_ref, o_ref):
        for d in range(N_DEV):
            ref = wg_ref if d < N_DEV // 2 else wv_ref
            off = _UPW * (d % (N_DEV // 2))
            o_ref[d] = ref[:, off:off + _UPW].astype(o_ref.dtype)

    half = pl.BlockSpec((LAYOUT_TM, D_FF), lambda i: (i, 0))
    return pl.pallas_call(
        body, name="extract_up", grid=(D_MODEL // LAYOUT_TM,), in_specs=[half, half],
        out_specs=pl.BlockSpec((N_DEV, LAYOUT_TM, _UPW), lambda i: (0, i, 0)),
        out_shape=jax.ShapeDtypeStruct((N_DEV, D_MODEL, _UPW), BF16),
        compiler_params=_params(dimension_semantics=("arbitrary",)),
    )(dwg, dwv)


MESH = pl.DeviceIdType.MESH
ANY = pl.BlockSpec(memory_space=pl.ANY)


def _place():
    mx, my, mc = lax.axis_index("x"), lax.axis_index("y"), lax.axis_index("c")
    return mx, my, mc, [(1 - mx, my), (mx, 1 - my), (1 - mx, 1 - my)]


def all_gather_blocks(xs, first_only=()):
    n = len(xs)

    def body(*refs):
        x_refs, out_refs = refs[:n], refs[n:2 * n]
        send_sems, recv_sems, local_sems = refs[2 * n:]
        mx, my, mc, chips = _place()
        me, sibling = (mx, my, mc), (mx, my, 1 - mc)
        x_refs = [x_refs[t].at[0] if t in first_only else x_refs[t] for t in range(n)]

        def rows(t, px, py, pc):
            dev = 4 * px + 2 * py + pc
            return out_refs[t].at[dev] if t in first_only else out_refs[t].at[:, dev]

        def copy(t, k, block, to, src=None):
            return pltpu.make_async_remote_copy(
                src_ref=rows(t, *block) if src is None else src, dst_ref=rows(t, *block),
                send_sem=send_sems.at[t, k], recv_sem=recv_sems.at[t, k], device_id=to, device_id_type=MESH)

        mine = [pltpu.make_async_copy(x_refs[t], rows(t, *me), local_sems.at[t]) for t in range(n)]
        for cp in mine:
            cp.start()
        first = []
        for t in range(n):
            first.append(copy(t, 0, me, sibling, src=x_refs[t]))
            first += [copy(t, 1 + j, me, (*chip, mc), src=x_refs[t]) for j, chip in enumerate(chips)]
        for cp in first:
            cp.start()
        passed = []
        for j, chip in enumerate(chips):
            for t in range(n):
                copy(t, 1 + j, (*chip, mc), me).wait_recv()
                cp = copy(t, 4 + j, (*chip, mc), sibling)
                cp.start()
                passed.append(cp)
        for t in range(n):
            copy(t, 0, sibling, me).wait_recv()
            for j, chip in enumerate(chips):
                copy(t, 4 + j, (*chip, 1 - mc), me).wait_recv()
        for cp in first + passed:
            cp.wait_send()
        for cp in mine:
            cp.wait()

    return pl.pallas_call(
        body, name="all_gather_blocks",
        out_shape=[jax.ShapeDtypeStruct(((N_DEV,) if t in first_only else (x.shape[0], N_DEV)) + x.shape[1:], x.dtype)
                   for t, x in enumerate(xs)],
        in_specs=[ANY] * n, out_specs=[ANY] * n,
        scratch_shapes=[pltpu.SemaphoreType.DMA((n, 7)), pltpu.SemaphoreType.DMA((n, 7)), pltpu.SemaphoreType.DMA((n,))],
    )(*xs)


HBM = pl.BlockSpec(memory_space=pltpu.HBM)
SEM = pl.BlockSpec(memory_space=pltpu.SEMAPHORE)
EFFECT = pltpu.SideEffectType.DATAFLOW_SIDE_EFFECTING
ALL_DEVICES = [(px, py, pc) for px in range(2) for py in range(2) for pc in range(2)]


def _hbm(x):
    return pltpu.with_memory_space_constraint(x, pltpu.HBM)


def _split_start(body, name, srcs, lands):
    n = len(srcs)

    def full_body(*refs):
        body(refs[:n], refs[n:2 * n], refs[2 * n], refs[2 * n + 1])
        refs[-1][...] = jnp.zeros_like(refs[-1])

    res = pl.pallas_call(
        full_body, name=name,
        out_shape=(pltpu.SemaphoreType.DMA((n,)), pltpu.SemaphoreType.DMA((n,)),
                   *[pltpu.HBM(x.shape, x.dtype) for x in srcs], *[pltpu.HBM(x.shape, x.dtype) for x in lands],
                   jax.ShapeDtypeStruct((8, LANES), F32)),
        in_specs=[HBM] * (2 * n), out_specs=(SEM, SEM, *[HBM] * (2 * n), pl.BlockSpec(memory_space=pltpu.VMEM)),
        input_output_aliases={i: 2 + i for i in range(2 * n)},
        compiler_params=pltpu.CompilerParams(has_side_effects=EFFECT),
    )(*[_hbm(x) for x in srcs], *[_hbm(x) for x in lands])
    return res[0], res[1], list(res[2:2 + n]), list(res[2 + n:2 + 2 * n]), res[-1]


def _split_wait(name, send_sems, recv_sems, srcs, lands, after, sent, landed):
    n = len(srcs)

    def body(*refs):
        src_refs, land_refs, ssem, rsem = refs[:n], refs[n:2 * n], refs[2 * n], refs[2 * n + 1]
        mx, my, mc, _ = _place()
        for t in range(n):
            out = sent(src_refs[t], land_refs[t])
            inn = landed(land_refs[t])
            pltpu.make_async_remote_copy(src_ref=out, dst_ref=out, send_sem=ssem.at[t], recv_sem=rsem.at[t],
                                         device_id=(mx, my, mc), device_id_type=MESH).wait_send()
            pltpu.make_async_remote_copy(src_ref=inn, dst_ref=inn, send_sem=ssem.at[t], recv_sem=rsem.at[t],
                                         device_id=(mx, my, mc), device_id_type=MESH).wait_recv()

    res = pl.pallas_call(
        body, name=name,
        out_shape=(*[pltpu.HBM(x.shape, x.dtype) for x in srcs], *[pltpu.HBM(x.shape, x.dtype) for x in lands]),
        in_specs=[HBM] * (2 * n) + [SEM, SEM, ANY], out_specs=[HBM] * (2 * n),
        input_output_aliases={i: i for i in range(2 * n)},
        compiler_params=pltpu.CompilerParams(has_side_effects=EFFECT),
    )(*srcs, *lands, send_sems, recv_sems, after)
    return list(res[:n]), list(res[n:])


def gather_start(srcs, l):
    lands = [lax.empty((N_DEV,) + x.shape[1:], x.dtype) for x in srcs]

    def body(src_refs, land_refs, send_sems, recv_sems):
        mx, my, mc, _ = _place()
        me = 4 * mx + 2 * my + mc
        for t in range(len(srcs)):
            for to in ALL_DEVICES:
                pltpu.make_async_remote_copy(
                    src_ref=src_refs[t].at[l], dst_ref=land_refs[t].at[me], send_sem=send_sems.at[t],
                    recv_sem=recv_sems.at[t], device_id=to, device_id_type=MESH).start()

    return _split_start(body, "gather_start_%d" % l, srcs, lands)


def gather_wait(l, send_sems, recv_sems, srcs, lands, after):
    return _split_wait("gather_wait_%d" % l, send_sems, recv_sems, srcs, lands, after,
                       sent=lambda s, d: d, landed=lambda d: d)


def chip_exchange_start(ps, lands, l):
    if lands is None:
        lands = [lax.empty((DEPTH,) + p.shape, p.dtype) for p in ps]

    def body(p_refs, land_refs, send_sems, recv_sems):
        mx, my, mc, _ = _place()
        mychip = 2 * mx + my
        for t in range(len(ps)):
            for px in range(2):
                for py in range(2):
                    pltpu.make_async_remote_copy(
                        src_ref=p_refs[t].at[2 * px + py], dst_ref=land_refs[t].at[l, mychip], send_sem=send_sems.at[t],
                        recv_sem=recv_sems.at[t], device_id=(px, py, mc), device_id_type=MESH).start()

    return _split_start(body, "chip_exchange_start_%d" % l, ps, lands)


def chip_exchange_wait(l, send_sems, recv_sems, ps, lands, after):
    return _split_wait("chip_exchange_wait_%d" % l, send_sems, recv_sems, ps, lands, after,
                       sent=lambda s, d: s, landed=lambda d: d.at[l])


def _whole(ref, send_sem, recv_sem, me):
    return pltpu.make_async_remote_copy(src_ref=ref, dst_ref=ref, send_sem=send_sem, recv_sem=recv_sem,
                                        device_id=me, device_id_type=MESH)


def sibling_exchange(es):
    nt, depth = len(es), len(es[0])

    def body(*refs):
        e_refs = [refs[t * depth:(t + 1) * depth] for t in range(nt)]
        out_refs = refs[nt * depth:nt * depth + nt]
        send_sems, recv_sems = refs[nt * depth + nt:]
        mx, my, mc, _ = _place()
        sibling = (mx, my, 1 - mc)
        for t in range(nt):
            for l in range(depth):
                for k in range(N_CHIP):
                    pltpu.make_async_remote_copy(
                        src_ref=e_refs[t][l].at[2 * k + 1 - mc], dst_ref=out_refs[t].at[l, k],
                        send_sem=send_sems.at[t], recv_sem=recv_sems.at[t], device_id=sibling, device_id_type=MESH).start()
        for t in range(nt):
            _whole(out_refs[t], send_sems.at[t], recv_sems.at[t], (mx, my, mc)).wait_recv()
        for t in range(nt):
            _whole(out_refs[t], send_sems.at[t], recv_sems.at[t], (mx, my, mc)).wait_send()

    return pl.pallas_call(
        body, name="rs_sibling_exchange",
        out_shape=[jax.ShapeDtypeStruct((depth, N_CHIP) + e[0].shape[1:], e[0].dtype) for e in es],
        in_specs=[ANY] * (nt * depth), out_specs=[ANY] * nt,
        scratch_shapes=[pltpu.SemaphoreType.DMA((nt,)), pltpu.SemaphoreType.DMA((nt,))],
    )(*[e for t in es for e in t])


def chip_sum(e, a_buf, core):
    depth = len(e)
    _, a, b = e[0].shape
    ta = _row_tile(a)

    def body(c_ref, *refs):
        a_ref, o_ref = refs[depth], refs[depth + 1]
        for l in range(depth):
            o_ref[l] = (refs[l][...].astype(F32) + a_ref[l].astype(F32)).astype(o_ref.dtype)

    buf = pl.BlockSpec((depth, None, ta, b), lambda k, i, c: (0, k, i, 0))
    return pl.pallas_call(
        body, name="rs_chip_sum",
        grid_spec=pltpu.PrefetchScalarGridSpec(
            num_scalar_prefetch=1, grid=(N_CHIP, a // ta),
            in_specs=[pl.BlockSpec((None, ta, b), lambda k, i, c: (2 * k + c[0], i, 0))] * depth + [buf],
            out_specs=buf),
        out_shape=jax.ShapeDtypeStruct(a_buf.shape, a_buf.dtype),
        compiler_params=_params(dimension_semantics=("arbitrary", "arbitrary")),
    )(core, *e, a_buf)


def _adam(g, w, m, v):
    nm = ADAM_B1 * m + (1.0 - ADAM_B1) * g
    nv = ADAM_B2 * v + (1.0 - ADAM_B2) * jnp.square(g)
    m_hat = nm / (1.0 - ADAM_B1 ** ADAM_STEP)
    v_hat = nv / (1.0 - ADAM_B2 ** ADAM_STEP)
    return -ADAM_LR * (m_hat / (jnp.sqrt(v_hat) + ADAM_EPS) + ADAM_WD * w), nm, nv


def adamw_big(parts, w, m, v, name):
    depth, _, a, b = parts.shape
    ta = _row_tile(a)

    def body(p_ref, w_ref, m_ref, v_ref, g_ref, d_ref, nm_ref, nv_ref):
        g = p_ref[0].astype(F32)
        for k in range(1, N_CHIP):
            g = g + p_ref[k].astype(F32)
        g_ref[...] = g
        d_ref[...], nm_ref[...], nv_ref[...] = _adam(g, w_ref[...], m_ref[...], v_ref[...])

    blk = pl.BlockSpec((None, ta, b), lambda l, i: (l, i, 0))
    return pl.pallas_call(
        body, name=name, grid=(depth, a // ta),
        in_specs=[pl.BlockSpec((None, N_CHIP, ta, b), lambda l, i: (l, 0, i, 0)), blk, blk, blk], out_specs=[blk] * 4,
        out_shape=[jax.ShapeDtypeStruct((depth, a, b), F32)] * 4,
        compiler_params=_params(dimension_semantics=("arbitrary", "arbitrary")),
    )(parts, w, m, v)


SMALL_VIEW = {'norm_mix': (DEPTH, 1024), 'ssm_norm': (DEPTH, 1024), 'attn_out_norm': (DEPTH, 1024),
              'norm_mem_q': (DEPTH, 1024), 'norm_mem_kv': (DEPTH, 1024), 'norm_ffn': (DEPTH, 1024),
              'q_norm': (DEPTH, 384), 'kv_norm': (DEPTH, 256), 'ssm_conv_b': (DEPTH, 2048), 'ffn_conv_b': (DEPTH, 5632),
              'dt_bias': (DEPTH, SSM_HEADS), 'a_log': (DEPTH, SSM_HEADS), 'd_skip': (DEPTH, SSM_HEADS),
              'ssm_conv_w': (DEPTH, SSM_CONV * CONV_CH // N_DEV), 'ffn_conv_w': (DEPTH, FFN_CONV * 2 * D_FF // N_DEV),
              'final_norm': (1, 1024)}
SMALL_NAMES = list(SMALL_VIEW)
SMALL_SHARDED = {'ssm_conv_w': (SSM_CONV, CONV_CH // N_DEV, CONV_CH), 'ffn_conv_w': (FFN_CONV, 2 * D_FF // N_DEV, 2 * D_FF)}


def adamw_small(gathered, ws, ms, vs):
    nsm = len(SMALL_NAMES)

    def body(*refs):
        g8_ref = refs[0]
        w_refs, m_refs, v_refs = refs[1:1 + nsm], refs[1 + nsm:1 + 2 * nsm], refs[1 + 2 * nsm:1 + 3 * nsm]
        outs = refs[1 + 3 * nsm:1 + 7 * nsm]
        sum_ref = refs[1 + 7 * nsm]
        shard_bufs = refs[2 + 7 * nsm:]
        tot = g8_ref[:, 0, :]
        for d in range(1, N_DEV):
            tot = tot + g8_ref[:, d, :]
        sum_ref[...] = tot
        mx, my, mc, _ = _place()
        dev = 4 * mx + 2 * my + mc

        def update(i, g):
            d, nm, nv = _adam(g, w_refs[i][...], m_refs[i][...], v_refs[i][...])
            outs[i][...] = g
            outs[nsm + i][...] = d
            outs[2 * nsm + i][...] = nm
            outs[3 * nsm + i][...] = nv

        for i, name in enumerate(SMALL_NAMES):
            rows, cols = SMALL_VIEW[name]
            off = SMALL_OFF[name]
            if name in SMALL_SHARDED:
                taps, per, full = SMALL_SHARDED[name]
                buf = shard_bufs[list(SMALL_SHARDED).index(name)]
                for d in range(N_DEV):
                    @pl.when(dev == d)
                    def _(d=d, taps=taps, per=per, full=full, off=off, buf=buf):
                        for k in range(taps):
                            buf[:, per * k:per * (k + 1)] = sum_ref[:, off + full * k + per * d:off + full * k + per * (d + 1)]
                update(i, buf[...])
            else:
                update(i, sum_ref[0:rows, off:off + cols])

    views = [jax.ShapeDtypeStruct(SMALL_VIEW[n], F32) for n in SMALL_NAMES]
    vmem = pl.BlockSpec(memory_space=pltpu.VMEM)
    res = pl.pallas_call(
        body, name="adamw_small", out_shape=views * 4, in_specs=[vmem] * (1 + 3 * nsm), out_specs=[vmem] * (4 * nsm),
        scratch_shapes=[pltpu.VMEM((DEPTH, SMALL_W), F32)] + [pltpu.VMEM(SMALL_VIEW[n], F32) for n in SMALL_SHARDED],
        compiler_params=_params(),
    )(gathered, *[ws[n] for n in SMALL_NAMES], *[ms[n] for n in SMALL_NAMES], *[vs[n] for n in SMALL_NAMES])
    return [dict(zip(SMALL_NAMES, res[k * nsm:(k + 1) * nsm])) for k in range(4)]


def _layer_weights(gathered):
    w_kn, w_v = assemble_ukv(gathered['w_ukv'])
    w_g, w_vv = assemble_up(gathered['w_up'])
    stacked = lambda n: gathered[n].reshape(N_DEV * BIG[n][0], BIG[n][1])
    return dict(w_proj=assemble_proj(gathered['w_in']), w_uq=assemble_uq(gathered['w_uq']), w_kn=w_kn, w_v=w_v,
                w_g=w_g, w_vv=w_vv, w_out=stacked('w_out'), w_mq=stacked('w_mq'), w_mk=stacked('w_mk'),
                w_mv=stacked('w_mv'), w_mo=stacked('w_mo'), w_down=stacked('w_down'))


def layer_fwd(x0, mem, cosm, sinm, w, sm, l, tie=None):
    gain = lambda n: (sm[n], l)
    sv = dict(x0=x0)
    sv['h1'] = rmsnorm_fwd(x0, gain('norm_mix'), "norm_mix_fwd", tie=tie)
    proj = sv['proj'] = matmul([(sv['h1'], w['w_proj'])], 'nn', F32, "proj_fwd")
    sv['xbc'] = ssm_conv_fwd(proj, sm['ssm_conv_w'], sm['ssm_conv_b'], l)
    sv['y'], sv['prevs'] = ssd_fwd(sv['xbc'], proj, sm['ptile'], l)
    mix = gate_norm_fwd(sv['y'], proj, gain('ssm_norm'))
    sv['cqn'] = rmsnorm_fwd(proj, gain('q_norm'), "q_norm_fwd", Q_LORA, OFF_CQ // Q_LORA)
    sv['ckvn'] = rmsnorm_fwd(proj, gain('kv_norm'), "kv_norm_fwd", KV_LORA, OFF_CKV // KV_LORA)
    q = matmul([(sv['cqn'], w['w_uq'])], 'nn', F32, "uq_fwd")
    sv['q'] = rope_q(q, cosm, sinm, "rope_q_fwd")
    kn = matmul([(sv['ckvn'], w['w_kn'])], 'nn', BF16, "kn_fwd")
    sv['k'] = build_k(kn, proj, cosm, sinm)
    sv['v'] = matmul([(sv['ckvn'], w['w_v'])], 'nn', BF16, "v_fwd")
    sv['o'], sv['lse'] = mla_fwd(sv['q'], sv['k'], sv['v'])
    mix = sv['mix'] = rmsnorm_fwd(sv['o'], gain('attn_out_norm'), "attn_out_norm_fwd", out=(D_SSM, BF16, D_MIX, 1),
                                  into=(mix, 0))
    x1 = sv['x1'] = matmul([(mix, w['w_out'])], 'nn', F32, "out_fwd", add=x0)
    sv['hq'] = rmsnorm_fwd(x1, gain('norm_mem_q'), "norm_mem_q_fwd")
    sv['mn'] = rmsnorm_fwd(mem, gain('norm_mem_kv'), "norm_mem_kv_fwd")
    sv['mq'] = matmul([(sv['hq'], w['w_mq'])], 'nn', BF16, "mq_fwd")
    sv['mk'] = matmul([(sv['mn'], w['w_mk'])], 'nn', BF16, "mk_fwd")
    sv['mv'] = matmul([(sv['mn'], w['w_mv'])], 'nn', BF16, "mv_fwd")
    sv['om'] = mem_fwd(sv['mq'], sv['mk'], sv['mv'])
    x2 = sv['x2'] = matmul([(sv['om'], w['w_mo'])], 'nn', F32, "mo_fwd", add=x1)
    sv['h3'] = rmsnorm_fwd(x2, gain('norm_ffn'), "norm_ffn_fwd")
    sv['ug'] = matmul([(sv['h3'], w['w_g'])], 'nn', F32, "up_g_fwd")
    sv['uv'] = matmul([(sv['h3'], w['w_vv'])], 'nn', F32, "up_v_fwd")
    sv['a'] = ffn_act_fwd(sv['ug'], sv['uv'], sm['ffn_conv_w'], sm['ffn_conv_b'], l)
    x3 = matmul([(sv['a'], w['w_down'])], 'nn', F32, "down_fwd", add=x2)
    return x3, sv


def layer_bwd(dx3, mem, cosm, sinm_neg, w, sm, l, sv, tie=None):
    gain = lambda n: (sm[n], l)
    big, small = {}, {}
    proj = sv['proj']
    da = matmul([(dx3, w['w_down'])], 'nt', BF16, "down_bwd_a", tie=tie)
    big['w_down'] = matmul([(sv['a'], dx3)], 'tn', BF16, "down_bwd_w")
    dug, duv, dcwg, dcwv, dcbg, dcbv = ffn_act_bwd(sv['ug'], sv['uv'], sm['ffn_conv_w'], sm['ffn_conv_b'], l, da)
    small['ffn_conv_w'] = jnp.concatenate([dcwg, dcwv], axis=1)
    small['ffn_conv_b'] = jnp.concatenate([dcbg, dcbv], axis=1)
    dh3 = matmul([(dug, w['w_g']), (duv, w['w_vv'])], 'nt', BF16, "up_bwd_h")
    big['w_up'] = extract_up(matmul([(sv['h3'], dug)], 'tn', BF16, "up_g_bwd_w"),
                             matmul([(sv['h3'], duv)], 'tn', BF16, "up_v_bwd_w"))
    dx2, small['norm_ffn'] = rmsnorm_bwd(sv['x2'], gain('norm_ffn'), dh3, "norm_ffn_bwd", resid=dx3)
    dom = matmul([(dx2, w['w_mo'])], 'nt', BF16, "mo_bwd_a")
    big['w_mo'] = matmul([(sv['om'], dx2)], 'tn', BF16, "mo_bwd_w")
    dmq, dmk, dmv = mem_bwd(sv['mq'], sv['mk'], sv['mv'], dom)
    dhq = matmul([(dmq, w['w_mq'])], 'nt', BF16, "mq_bwd_a")
    big['w_mq'] = matmul([(sv['hq'], dmq)], 'tn', BF16, "mq_bwd_w")
    dmn = matmul([(dmk, w['w_mk']), (dmv, w['w_mv'])], 'nt', BF16, "mkv_bwd_a")
    big['w_mk'] = matmul([(sv['mn'], dmk)], 'tn', BF16, "mk_bwd_w")
    big['w_mv'] = matmul([(sv['mn'], dmv)], 'tn', BF16, "mv_bwd_w")
    _, small['norm_mem_kv'] = rmsnorm_bwd(mem, gain('norm_mem_kv'), dmn, "norm_mem_kv_bwd", dx_dtype=BF16)
    dx1, small['norm_mem_q'] = rmsnorm_bwd(sv['x1'], gain('norm_mem_q'), dhq, "norm_mem_q_bwd", resid=dx2)
    dmix = matmul([(dx1, w['w_out'])], 'nt', BF16, "out_bwd_a")
    big['w_out'] = matmul([(sv['mix'], dx1)], 'tn', BF16, "out_bwd_w")
    dy, dz, small['ssm_norm'] = gate_norm_bwd(sv['y'], proj, gain('ssm_norm'), dmix)
    dxbc_act, dsmall_ssd, small['ptile'] = ssd_bwd(sv['xbc'], proj, sm['ptile'], l, sv['prevs'], dy)
    dxbc, small['ssm_conv_w'], small['ssm_conv_b'] = ssm_conv_bwd(proj, sm['ssm_conv_w'], sm['ssm_conv_b'], l, dxbc_act)
    do, small['attn_out_norm'] = rmsnorm_bwd(sv['o'], gain('attn_out_norm'), dmix, "attn_out_norm_bwd", dh_colblock=1)
    dq_rot, dk, dv = mla_bwd(sv['q'], sv['k'], sv['v'], sv['o'], sv['lse'], do)
    dq = rope_q(dq_rot, cosm, sinm_neg, "rope_q_bwd")
    dsmall = dsmall_bwd(dk, dsmall_ssd, cosm, sinm_neg)
    dcqn = matmul([(dq, w['w_uq'])], 'nt', BF16, "uq_bwd_a")
    big['w_uq'] = extract_uq(matmul([(sv['cqn'], dq)], 'tn', BF16, "uq_bwd_w"))
    dckvn = matmul([(dk, w['w_kn']), (dv, w['w_v'])], 'nt', BF16, "ukv_bwd_a")
    big['w_ukv'] = extract_ukv(matmul([(sv['ckvn'], dk)], 'tn', BF16, "kn_bwd_w"),
                               matmul([(sv['ckvn'], dv)], 'tn', BF16, "v_bwd_w"))
    dcq, small['q_norm'] = rmsnorm_bwd(proj, gain('q_norm'), dcqn, "q_norm_bwd", width=Q_LORA,
                                       colblock=OFF_CQ // Q_LORA, dx_dtype=BF16)
    dckv, small['kv_norm'] = rmsnorm_bwd(proj, gain('kv_norm'), dckvn, "kv_norm_bwd", width=KV_LORA,
                                         colblock=OFF_CKV // KV_LORA, dx_dtype=BF16)
    wp = w['w_proj']
    xbc_half = lambda c: Opnd(dxbc, c0=c, shape=(dxbc.shape[0], 1024))
    wwin = lambda off, width: Opnd(wp, c0=off // width, shape=(D_MODEL, width))
    dh1 = matmul([(dz, wwin(OFF_Z, 1024)), (xbc_half(0), wwin(OFF_XBC, 1024)), (xbc_half(1), wwin(OFF_XBC + 1024, 1024)),
                  (dcq, wwin(OFF_CQ, Q_LORA)), (dsmall, wwin(OFF_SMALL, LANES)), (dckv, wwin(OFF_CKV, KV_LORA))],
                 'nt', BF16, "proj_bwd_a")
    h1 = sv['h1']
    big['w_in'] = extract_proj(
        matmul([(h1, dz)], 'tn', BF16, "proj_z_bwd_w"), matmul([(h1, dxbc)], 'tn', BF16, "proj_xbc_bwd_w"),
        matmul([(h1, dcq)], 'tn', BF16, "proj_cq_bwd_w"), matmul([(h1, dsmall)], 'tn', BF16, "proj_small_bwd_w"),
        matmul([(h1, dckv)], 'tn', BF16, "proj_ckv_bwd_w"))
    dx0, small['norm_mix'] = rmsnorm_bwd(sv['x0'], gain('norm_mix'), dh1, "norm_mix_bwd", resid=dx1)
    for n in ('w_down', 'w_mo', 'w_mq', 'w_mk', 'w_mv', 'w_out'):
        big[n] = big[n].reshape((N_DEV,) + BIG[n])
    return dx0, big, small


def _small_row(small, final=None):
    pt = small['ptile']
    parts = []
    for n, wd in SMALL_SEGS:
        if n in ('dt_bias', 'a_log', 'd_skip'):
            parts.append(pt[('dt_bias', 'a_log', 'd_skip').index(n)][None, :])
        elif n in SMALL_SHARDED:
            parts.append(small[n].reshape(1, wd))
        elif n == 'final_norm':
            parts.append(final if final is not None else jnp.zeros((1, wd), F32))
        else:
            parts.append(small[n])
    return jnp.concatenate(parts, axis=1)


def _rope_tables(positions):
    inv_freq = 1.0 / (ROPE_THETA ** (jnp.arange(0, QK_ROPE, 2, dtype=F32) / QK_ROPE))
    ang = positions.astype(F32)[:, None] * inv_freq
    cos, sin = jnp.cos(ang), jnp.sin(ang)
    s = positions.shape[0]
    pad = jnp.zeros((s, LANES - ROPE_LANE0 - QK_ROPE), F32)
    cosm = jnp.concatenate([jnp.ones((s, ROPE_LANE0), F32), cos, cos, pad], axis=1)
    sinm = jnp.concatenate([jnp.zeros((s, ROPE_LANE0), F32), -sin, sin, pad], axis=1)
    return cosm, sinm


def _small_views(rep, conv_full):
    sm = {n: rep[n].reshape(DEPTH, 1, -1) for n in ('norm_mix', 'ssm_norm', 'attn_out_norm', 'norm_mem_q',
                                                    'norm_mem_kv', 'norm_ffn', 'q_norm', 'kv_norm', 'ssm_conv_b',
                                                    'ffn_conv_b')}
    sm.update(conv_full)
    rows = jnp.stack([rep['dt_bias'], rep['a_log'], rep['d_skip']], axis=1)
    sm['ptile'] = jnp.pad(rows, ((0, 0), (0, 8 - 3), (0, LANES - SSM_HEADS)))
    return sm


def local_step(x, mem, positions, target, sm, final_norm, weights_of, on_grads):
    cosm, sinm = _rope_tables(positions)
    sinm_neg = -sinm
    saved, ws = [], []
    h = x
    for l in range(DEPTH):
        gathered, tie = weights_of(l, h)
        ws.append(_layer_weights(gathered))
        h, sv = layer_fwd(h, mem, cosm, sinm, ws[l], sm, l, tie=tie)
        saved.append(sv)
    dx, dfinal, lossv = loss_head(h, (final_norm.reshape(1, 1, -1), 0), target)
    rows = [None] * DEPTH
    tie = None
    for l in reversed(range(DEPTH)):
        dx, big, small = layer_bwd(dx, mem, cosm, sinm_neg, ws[l], sm, l, saved[l], tie=tie)
        tie = on_grads(l, big)
        rows[l] = _small_row(small, dfinal if l == 0 else None)
    return lossv[0, 0], dx, jnp.concatenate(rows, axis=0)


def kernel(x, mem, positions, norm_mix, w_in, ssm_conv_w, ssm_conv_b, dt_bias, a_log, d_skip, ssm_norm, q_norm, w_uq, kv_norm, w_ukv, attn_out_norm, w_out, norm_mem_q, norm_mem_kv, w_mq, w_mk, w_mv, w_mo, norm_ffn, w_up, ffn_conv_w, ffn_conv_b, w_down, final_norm, loss_target, m_norm_mix, m_w_in, m_ssm_conv_w, m_ssm_conv_b, m_dt_bias, m_a_log, m_d_skip, m_ssm_norm, m_q_norm, m_w_uq, m_kv_norm, m_w_ukv, m_attn_out_norm, m_w_out, m_norm_mem_q, m_norm_mem_kv, m_w_mq, m_w_mk, m_w_mv, m_w_mo, m_norm_ffn, m_w_up, m_ffn_conv_w, m_ffn_conv_b, m_w_down, m_final_norm, v_norm_mix, v_w_in, v_ssm_conv_w, v_ssm_conv_b, v_dt_bias, v_a_log, v_d_skip, v_ssm_norm, v_q_norm, v_w_uq, v_kv_norm, v_w_ukv, v_attn_out_norm, v_w_out, v_norm_mem_q, v_norm_mem_kv, v_w_mq, v_w_mk, v_w_mv, v_w_mo, v_norm_ffn, v_w_up, v_ffn_conv_w, v_ffn_conv_b, v_w_down, v_final_norm):
    args = locals()
    wts = {n: args[n] for n in WEIGHT_NAMES}
    ms = {n: args['m_' + n] for n in WEIGHT_NAMES}
    vs = {n: args['v_' + n] for n in WEIGHT_NAMES}

    nbig = len(BIG_NAMES)
    srcs = [wts[n].astype(BF16) for n in BIG_NAMES]
    got = all_gather_blocks(srcs + [wts[n] for n in SMALL_SHARDED], first_only=tuple(range(nbig)))
    conv_full = {}
    for n, g in zip(SMALL_SHARDED, got[nbig:]):
        taps, per, full = SMALL_SHARDED[n]
        conv_full[n] = jnp.moveaxis(g, 1, 2).reshape(DEPTH, taps, full)
    sm = _small_views(wts, conv_full)
    core = lax.axis_index("c").astype(jnp.int32).reshape(1)
    st = dict(srcs=srcs, gather=None, lands=None, exchanges=[])

    def weights_of(l, h):
        if l == 0:
            lands = got[:nbig]
        else:
            send_sems, recv_sems, thru, lands = st['gather']
            st['srcs'], lands = gather_wait(l, send_sems, recv_sems, thru, lands, h)
        tie = None
        if l + 1 < DEPTH:
            send_sems, recv_sems, thru, nxt, tie = gather_start(st['srcs'], l + 1)
            st['gather'] = (send_sems, recv_sems, thru, nxt)
        return dict(zip(BIG_NAMES, lands)), tie

    def on_grads(l, big):
        es = [[big[n]] for n in BIG_NAMES]
        from_sibling = sibling_exchange(es)
        ps = [chip_sum(es[t], from_sibling[t], core).reshape((N_CHIP,) + BIG[n]) for t, n in enumerate(BIG_NAMES)]
        send_sems, recv_sems, thru, st['lands'], tie = chip_exchange_start(ps, st['lands'], l)
        st['exchanges'].append((l, send_sems, recv_sems, thru))
        return tie

    loss_local, dx, small_rows = local_step(x[0], mem[0], positions[0], loss_target[0], sm, final_norm, weights_of,
                                            on_grads)
    for l, send_sems, recv_sems, thru in st['exchanges']:
        _, st['lands'] = chip_exchange_wait(l, send_sems, recv_sems, thru, st['lands'], dx)
    parts = st['lands']
    outs = [{}, {}, {}, {}]
    for t, n in enumerate(BIG_NAMES):
        res = adamw_big(parts[t], wts[n], ms[n], vs[n], "adamw_" + n)
        for k in range(4):
            outs[k][n] = res[k]

    small_all = all_gather_blocks([small_rows])[0]
    view = lambda d: {n: d[n].reshape(SMALL_VIEW[n]) for n in SMALL_NAMES}
    res = adamw_small(small_all, view(wts), view(ms), view(vs))
    for k in range(4):
        for n in SMALL_NAMES:
            outs[k][n] = res[k][n].reshape(wts[n].shape)

    loss = lax.psum(loss_local, ("x", "y", "c"))
    return (loss, dx[None], *[outs[0][n] for n in WEIGHT_NAMES], *[outs[1][n] for n in WEIGHT_NAMES],
            *[outs[2][n] for n in WEIGHT_NAMES], *[outs[3][n] for n in WEIGHT_NAMES])
```

```python
import functools
import math
from typing import Any, NamedTuple, Optional

import jax
import jax.numpy as jnp
from jax import lax
from jax.experimental import pallas as pl
from jax.experimental.pallas import tpu as pltpu

F32 = jnp.float32
BF16 = jnp.bfloat16

D_MODEL = 1024
DEPTH = 4
MEM_LEN = 256
EPS = 1e-6
SSM_HEADS = 16
SSM_HEAD_DIM = 64
D_SSM = 1024
SSM_GROUPS = 4
SSM_STATE = 128
SSM_CONV = 4
SSM_CHUNK = 128
CONV_CH = 2048
MLA_HEADS = 16
QK_NOPE = 64
QK_ROPE = 32
V_DIM = 64
Q_LORA = 384
KV_LORA = 256
ROPE_THETA = 10000.0
MEM_HEADS = 4
MEM_HEAD_DIM = 256
D_FF = 2816
FFN_CONV = 3
D_IN = 3760
D_MIX = 2048
ADAM_LR = 0.001
ADAM_B1 = 0.9
ADAM_B2 = 0.999
ADAM_EPS = 1e-08
ADAM_WD = 0.01
ADAM_STEP = 10

N_DEV = 8
N_CHIP = 4
LANES = 128
HEAD_PAD = 128
PROJ_W = 3840
OFF_Z, OFF_XBC, OFF_CQ, OFF_SMALL, OFF_CKV = 0, 1024, 3072, 3456, 3584
ROPE_LANE0 = 64
VMEM_LIMIT = 56 * 1024 * 1024

WEIGHT_NAMES = ['norm_mix', 'w_in', 'ssm_conv_w', 'ssm_conv_b', 'dt_bias', 'a_log', 'd_skip', 'ssm_norm', 'q_norm',
                'w_uq', 'kv_norm', 'w_ukv', 'attn_out_norm', 'w_out', 'norm_mem_q', 'norm_mem_kv', 'w_mq', 'w_mk',
                'w_mv', 'w_mo', 'norm_ffn', 'w_up', 'ffn_conv_w', 'ffn_conv_b', 'w_down', 'final_norm']
BIG = {'w_in': (1024, 470), 'w_uq': (384, 192), 'w_ukv': (256, 256), 'w_up': (1024, 704), 'w_out': (256, 1024),
       'w_mq': (128, 1024), 'w_mk': (128, 1024), 'w_mv': (128, 1024), 'w_mo': (128, 1024), 'w_down': (352, 1024)}
BIG_NAMES = list(BIG)
PROJ_SEGS = [(0, 1024, OFF_Z), (1024, 3072, OFF_XBC), (3072, 3088, OFF_SMALL), (3088, 3472, OFF_CQ),
             (3472, 3728, OFF_CKV), (3728, 3760, OFF_SMALL + ROPE_LANE0)]
SMALL_SEGS = [('norm_mix', 1024), ('ssm_norm', 1024), ('attn_out_norm', 1024), ('norm_mem_q', 1024),
              ('norm_mem_kv', 1024), ('norm_ffn', 1024), ('q_norm', 384), ('kv_norm', 256), ('ssm_conv_b', 2048),
              ('ffn_conv_b', 5632), ('dt_bias', 128), ('a_log', 128), ('d_skip', 128),
              ('ssm_conv_w', SSM_CONV * CONV_CH), ('ffn_conv_w', FFN_CONV * 2 * D_FF), ('final_norm', 1024)]
SMALL_OFF = {}
_o = 0
for _n, _w in SMALL_SEGS:
    SMALL_OFF[_n] = _o
    _o += _w
SMALL_W = _o


def _params(**kw):
    return pltpu.CompilerParams(vmem_limit_bytes=VMEM_LIMIT, **kw)


def _pick(n, cap):
    if n <= cap:
        return n
    best = None
    for t in range(LANES, cap + 1, LANES):
        if n % t == 0:
            best = t
    assert best is not None, (n, cap)
    return best


def _row_tile(a, cap=256):
    if a <= cap:
        return a
    best = None
    for t in range(16, cap + 1, 16):
        if a % t == 0:
            best = t
    assert best is not None, (a, cap)
    return best


class Opnd(NamedTuple):
    arr: Any
    lead: Optional[int] = None
    r0: int = 0
    c0: int = 0
    shape: Optional[tuple] = None


def _opnd(x):
    return x if isinstance(x, Opnd) else Opnd(x)


def _lshape(o):
    return tuple(o.shape) if o.shape is not None else tuple(o.arr.shape[-2:])


def _spec(o, br, bc, bi, bj):
    rr, cc = _lshape(o)
    assert rr % br == 0 and cc % bc == 0, (rr, cc, br, bc)
    ro, co = o.r0 * (rr // br), o.c0 * (cc // bc)
    if o.lead is None:
        return pl.BlockSpec((br, bc), lambda i, j: (ro + bi(i, j), co + bj(i, j)))
    return pl.BlockSpec((None, br, bc), lambda i, j: (o.lead, ro + bi(i, j), co + bj(i, j)))


_DIMS = {'nn': (((1,), (0,)), ((), ())), 'nt': (((1,), (1,)), ((), ())), 'tn': (((0,), (0,)), ((), ()))}
_ROW = lambda i, j: i
_COL = lambda i, j: j
_ZERO = lambda i, j: 0


def matmul(pairs, mode, out_dtype, name, add=None, tie=None):
    pairs = [(_opnd(a), _opnd(b)) for a, b in pairs]
    a0, b0 = pairs[0]
    if mode == 'nn':
        m, n = _lshape(a0)[0], _lshape(b0)[1]
    elif mode == 'nt':
        m, n = _lshape(a0)[0], _lshape(b0)[0]
    else:
        m, n = _lshape(a0)[1], _lshape(b0)[1]
    kmax = max(_lshape(a)[0] if mode == 'tn' else _lshape(a)[1] for a, _ in pairs)
    if mode == 'tn':
        tm, tn = _pick(m, 512), _pick(n, 512)
    else:
        tm = _pick(m, 512)
        tn = _pick(n, max(LANES, (3 * 1024 * 1024 // (2 * kmax)) // LANES * LANES))
    npairs = len(pairs)

    def body(*refs):
        o_ref = refs[-1]
        acc = None
        for p in range(npairs):
            a = refs[2 * p][...].astype(BF16)
            b = refs[2 * p + 1][...].astype(BF16)
            d = lax.dot_general(a, b, _DIMS[mode], preferred_element_type=F32)
            acc = d if acc is None else acc + d
        if add is not None:
            acc = acc + refs[2 * npairs][...].astype(F32)
        o_ref[...] = acc.astype(out_dtype)

    tie_specs = [pl.BlockSpec(memory_space=pl.ANY)] if tie is not None else []
    tie_args = [tie] if tie is not None else []

    in_specs, args = [], []
    for a, b in pairs:
        if mode == 'nn':
            k = _lshape(a)[1]
            in_specs += [_spec(a, tm, k, _ROW, _ZERO), _spec(b, k, tn, _ZERO, _COL)]
        elif mode == 'nt':
            k = _lshape(a)[1]
            in_specs += [_spec(a, tm, k, _ROW, _ZERO), _spec(b, tn, k, _COL, _ZERO)]
        else:
            k = _lshape(a)[0]
            in_specs += [_spec(a, k, tm, _ZERO, _ROW), _spec(b, k, tn, _ZERO, _COL)]
        args += [a.arr, b.arr]
    if add is not None:
        in_specs.append(pl.BlockSpec((tm, tn), lambda i, j: (i, j)))
        args.append(add)
    return pl.pallas_call(
        body, name=name, grid=(m // tm, n // tn), in_specs=in_specs + tie_specs,
        out_specs=pl.BlockSpec((tm, tn), lambda i, j: (i, j)),
        out_shape=jax.ShapeDtypeStruct((m, n), out_dtype),
        compiler_params=_params(dimension_semantics=("arbitrary", "arbitrary")),
    )(*args, *tie_args)


def rowwise(fn, rows, fulls, outs, accs, name, tm=256, into=None, tie=None):
    s = rows[0][0].shape[0]
    nrow, nfull, nout, nacc = len(rows), len(fulls), len(outs), len(accs)
    nin = nrow + nfull

    def body(*refs):
        ins = [r[...] for r in refs[:nin]]
        res = fn(*ins)
        if not isinstance(res, (tuple, list)):
            res = (res,)
        orefs = refs[nin + (1 if into is not None else 0) + (1 if tie is not None else 0):]
        for k in range(nout):
            orefs[k][...] = res[k].astype(orefs[k].dtype)
        if nacc:
            @pl.when(pl.program_id(0) == 0)
            def _():
                for k in range(nacc):
                    orefs[nout + k][...] = jnp.zeros_like(orefs[nout + k])

            for k in range(nacc):
                orefs[nout + k][...] += res[nout + k].astype(orefs[nout + k].dtype)

    in_specs = [pl.BlockSpec((tm, w), lambda i, cb=cb: (i, cb)) for _, w, cb in rows]
    in_specs += [pl.BlockSpec((None,) + f.shape[1:], lambda i, ld=ld, nd=f.ndim - 1: (ld,) + (0,) * nd) for f, ld in fulls]
    args = [r[0] for r in rows] + [f for f, _ in fulls]
    aliases = {}
    if into is not None:
        in_specs.append(pl.BlockSpec(memory_space=pl.ANY))
        args.append(into[0])
        aliases = {nin: into[1]}
    if tie is not None:
        in_specs.append(pl.BlockSpec(memory_space=pl.ANY))
        args.append(tie)
    out_specs, out_shape = [], []
    for o in outs:
        w, dt = o[0], o[1]
        total, cb = (o[2], o[3]) if len(o) == 4 else (w, 0)
        out_specs.append(pl.BlockSpec((tm, w), lambda i, cb=cb: (i, cb)))
        out_shape.append(jax.ShapeDtypeStruct((s, total), dt))
    for shp, dt in accs:
        out_specs.append(pl.BlockSpec(shp, lambda i, nd=len(shp): (0,) * nd))
        out_shape.append(jax.ShapeDtypeStruct(shp, dt))
    return pl.pallas_call(
        body, name=name, grid=(s // tm,), in_specs=in_specs, out_specs=out_specs, out_shape=out_shape,
        input_output_aliases=aliases, compiler_params=_params(dimension_semantics=("arbitrary",)),
    )(*args)


def _rms(x, g):
    xf = x.astype(F32)
    var = jnp.mean(xf * xf, axis=-1, keepdims=True)
    return xf * lax.rsqrt(var + EPS) * g


def rmsnorm_fwd(x, g, name, width=None, colblock=0, out=None, into=None, tie=None):
    w = width or x.shape[1]
    return rowwise(lambda xt, gt: _rms(xt, gt), [(x, w, colblock)], [g], [out or (w, BF16)], [], name, into=into,
                   tie=tie)[0]


def rmsnorm_bwd(x, g, dh, name, resid=None, width=None, colblock=0, dh_colblock=0, dx_dtype=F32):
    w = width or x.shape[1]

    def fn(xt, dht, *rest):
        gt = rest[-1]
        _, vjp = jax.vjp(_rms, xt.astype(F32), gt)
        dx, dg = vjp(dht.astype(F32))
        if resid is not None:
            dx = dx + rest[0]
        return dx, dg

    rows = [(x, w, colblock), (dh, w, dh_colblock)] + ([(resid, w, 0)] if resid is not None else [])
    return rowwise(fn, rows, [g], [(w, dx_dtype)], [((1, w), F32)], name)


def _shift_down(u, k, row):
    return jnp.where(row >= k, pltpu.roll(u, k, 0), 0.0)


def _shift_up(u, k, row):
    s = u.shape[0]
    return jnp.where(row < s - k, pltpu.roll(u, s - k, 0), 0.0)


def _conv_fwd_tile(u, w_ref, b_ref, kw):
    row = lax.broadcasted_iota(jnp.int32, u.shape, 0)
    y = u * w_ref[kw - 1:kw, :] + b_ref[...]
    for k in range(1, kw):
        y = y + _shift_down(u, k, row) * w_ref[kw - 1 - k:kw - k, :]
    return y


def _conv_bwd_tile(u, dpre, w_ref, dw_ref, db_ref, kw):
    row = lax.broadcasted_iota(jnp.int32, u.shape, 0)
    du = dpre * w_ref[kw - 1:kw, :]
    dw_ref[kw - 1:kw, :] = jnp.sum(dpre * u, axis=0, keepdims=True)
    for k in range(1, kw):
        du = du + _shift_up(dpre, k, row) * w_ref[kw - 1 - k:kw - k, :]
        dw_ref[kw - 1 - k:kw - k, :] = jnp.sum(dpre * _shift_down(u, k, row), axis=0, keepdims=True)
    db_ref[...] = jnp.sum(dpre, axis=0, keepdims=True)
    return du


def _silu(x):
    return x * jax.nn.sigmoid(x)


def _dsilu(x):
    s = jax.nn.sigmoid(x)
    return s * (1.0 + x * (1.0 - s))


SSM_TC = 256


def ssm_conv_fwd(proj, cw, cb, l):
    s = proj.shape[0]
    off = OFF_XBC // SSM_TC

    def body(u_ref, w_ref, b_ref, o_ref):
        o_ref[...] = _silu(_conv_fwd_tile(u_ref[...], w_ref, b_ref, SSM_CONV))

    return pl.pallas_call(
        body, name="ssm_conv_fwd", grid=(CONV_CH // SSM_TC,),
        in_specs=[pl.BlockSpec((s, SSM_TC), lambda j: (0, off + j)),
                  pl.BlockSpec((None, SSM_CONV, SSM_TC), lambda j: (l, 0, j)),
                  pl.BlockSpec((None, 1, SSM_TC), lambda j: (l, 0, j))],
        out_specs=pl.BlockSpec((s, SSM_TC), lambda j: (0, j)),
        out_shape=jax.ShapeDtypeStruct((s, CONV_CH), F32),
        compiler_params=_params(dimension_semantics=("arbitrary",)),
    )(proj, cw, cb)


def ssm_conv_bwd(proj, cw, cb, l, dact):
    s = proj.shape[0]
    off = OFF_XBC // SSM_TC

    def body(u_ref, w_ref, b_ref, d_ref, du_ref, dw_ref, db_ref):
        u = u_ref[...]
        pre = _conv_fwd_tile(u, w_ref, b_ref, SSM_CONV)
        dpre = d_ref[...] * _dsilu(pre)
        du_ref[...] = _conv_bwd_tile(u, dpre, w_ref, dw_ref, db_ref, SSM_CONV).astype(du_ref.dtype)

    return pl.pallas_call(
        body, name="ssm_conv_bwd", grid=(CONV_CH // SSM_TC,),
        in_specs=[pl.BlockSpec((s, SSM_TC), lambda j: (0, off + j)),
                  pl.BlockSpec((None, SSM_CONV, SSM_TC), lambda j: (l, 0, j)),
                  pl.BlockSpec((None, 1, SSM_TC), lambda j: (l, 0, j)), pl.BlockSpec((s, SSM_TC), lambda j: (0, j))],
        out_specs=[pl.BlockSpec((s, SSM_TC), lambda j: (0, j)), pl.BlockSpec((SSM_CONV, SSM_TC), lambda j: (0, j)),
                   pl.BlockSpec((1, SSM_TC), lambda j: (0, j))],
        out_shape=[jax.ShapeDtypeStruct((s, CONV_CH), BF16), jax.ShapeDtypeStruct((SSM_CONV, CONV_CH), F32),
                   jax.ShapeDtypeStruct((1, CONV_CH), F32)],
        compiler_params=_params(dimension_semantics=("arbitrary",)),
    )(proj, cw, cb, dact)


FFN_TC = 256
FFN_NT = D_FF // FFN_TC


def _ffn_specs(s, l):
    blk = pl.BlockSpec((s, FFN_TC), lambda j: (0, j))
    wg = pl.BlockSpec((None, FFN_CONV, FFN_TC), lambda j: (l, 0, j))
    wv = pl.BlockSpec((None, FFN_CONV, FFN_TC), lambda j: (l, 0, FFN_NT + j))
    bg = pl.BlockSpec((None, 1, FFN_TC), lambda j: (l, 0, j))
    bv = pl.BlockSpec((None, 1, FFN_TC), lambda j: (l, 0, FFN_NT + j))
    return blk, wg, wv, bg, bv


def ffn_act_fwd(ug, uv, cw, cb, l):
    s = ug.shape[0]

    def body(g_ref, v_ref, wg_ref, wv_ref, bg_ref, bv_ref, o_ref):
        cg = _conv_fwd_tile(g_ref[...], wg_ref, bg_ref, FFN_CONV)
        cv = _conv_fwd_tile(v_ref[...], wv_ref, bv_ref, FFN_CONV)
        o_ref[...] = (_silu(cg) * cv).astype(o_ref.dtype)

    blk, wg, wv, bg, bv = _ffn_specs(s, l)
    return pl.pallas_call(
        body, name="ffn_act_fwd", grid=(FFN_NT,), in_specs=[blk, blk, wg, wv, bg, bv],
        out_specs=blk, out_shape=jax.ShapeDtypeStruct((s, D_FF), BF16),
        compiler_params=_params(dimension_semantics=("arbitrary",)),
    )(ug, uv, cw, cw, cb, cb)


def ffn_act_bwd(ug, uv, cw, cb, l, da):
    s = ug.shape[0]

    def body(g_ref, v_ref, wg_ref, wv_ref, bg_ref, bv_ref, da_ref, dg_ref, dv_ref, dwg_ref, dwv_ref, dbg_ref, dbv_ref):
        g, v = g_ref[...], v_ref[...]
        cg = _conv_fwd_tile(g, wg_ref, bg_ref, FFN_CONV)
        cv = _conv_fwd_tile(v, wv_ref, bv_ref, FFN_CONV)
        da_t = da_ref[...].astype(F32)
        dcg = da_t * cv * _dsilu(cg)
        dcv = da_t * _silu(cg)
        dg_ref[...] = _conv_bwd_tile(g, dcg, wg_ref, dwg_ref, dbg_ref, FFN_CONV).astype(dg_ref.dtype)
        dv_ref[...] = _conv_bwd_tile(v, dcv, wv_ref, dwv_ref, dbv_ref, FFN_CONV).astype(dv_ref.dtype)

    blk, wg, wv, bg, bv = _ffn_specs(s, l)
    wblk = pl.BlockSpec((FFN_CONV, FFN_TC), lambda j: (0, j))
    bblk = pl.BlockSpec((1, FFN_TC), lambda j: (0, j))
    return pl.pallas_call(
        body, name="ffn_act_bwd", grid=(FFN_NT,), in_specs=[blk, blk, wg, wv, bg, bv, blk],
        out_specs=[blk, blk, wblk, wblk, bblk, bblk],
        out_shape=[jax.ShapeDtypeStruct((s, D_FF), BF16), jax.ShapeDtypeStruct((s, D_FF), BF16),
                   jax.ShapeDtypeStruct((FFN_CONV, D_FF), F32), jax.ShapeDtypeStruct((FFN_CONV, D_FF), F32),
                   jax.ShapeDtypeStruct((1, D_FF), F32), jax.ShapeDtypeStruct((1, D_FF), F32)],
        compiler_params=_params(dimension_semantics=("arbitrary",)),
    )(ug, uv, cw, cw, cb, cb, da)


def _dot(a, b, mode):
    return lax.dot_general(a.astype(BF16), b.astype(BF16), _DIMS[mode], preferred_element_type=F32)


@jax.custom_vjp
def mm_nn(a, b):
    return _dot(a, b, 'nn')


@jax.custom_vjp
def mm_nt(a, b):
    return _dot(a, b, 'nt')


@jax.custom_vjp
def mm_tn(a, b):
    return _dot(a, b, 'tn')


mm_nn.defvjp(lambda a, b: (_dot(a, b, 'nn'), (a, b)), lambda r, g: (_dot(g, r[1], 'nt'), _dot(r[0], g, 'tn')))
mm_nt.defvjp(lambda a, b: (_dot(a, b, 'nt'), (a, b)), lambda r, g: (_dot(g, r[1], 'nn'), _dot(g, r[0], 'tn')))
mm_tn.defvjp(lambda a, b: (_dot(a, b, 'tn'), (a, b)), lambda r, g: (_dot(r[1], g, 'nt'), _dot(r[0], g, 'nn')))


def _tri(n, lower):
    r = lax.broadcasted_iota(jnp.int32, (n, n), 0)
    c = lax.broadcasted_iota(jnp.int32, (n, n), 1)
    return jnp.where((r >= c) if lower else (r <= c), 1.0, 0.0).astype(F32)


def _tri_dot(a, lower):
    return jnp.dot(_tri(a.shape[0], lower), a, precision=lax.Precision.HIGHEST, preferred_element_type=F32)


@jax.custom_vjp
def _cumsum_rows(a):
    return _tri_dot(a, True)


_cumsum_rows.defvjp(lambda a: (_tri_dot(a, True), None), lambda _, g: (_tri_dot(g, False),))


def _softplus(x):
    return jnp.maximum(x, 0.0) + jnp.log(1.0 + jnp.exp(-jnp.abs(x)))


def _ssd_chunk(xs, bs, cs, small, dtb, alog, dsk, prev):
    ln = small.shape[0]
    lane = lax.broadcasted_iota(jnp.int32, (ln, LANES), 1)
    lane1 = lax.broadcasted_iota(jnp.int32, (1, LANES), 1)
    sub = lax.broadcasted_iota(jnp.int32, (LANES, ln), 0)
    rowi = lax.broadcasted_iota(jnp.int32, (ln, LANES), 0)
    tril = lax.broadcasted_iota(jnp.int32, (ln, ln), 0) >= lax.broadcasted_iota(jnp.int32, (ln, ln), 1)
    first = lane < SSM_HEAD_DIM
    first1 = lane1 < SSM_HEAD_DIM

    dt = _softplus(small + dtb)
    acs = _cumsum_rows(dt * (-jnp.exp(alog)))
    acs_t = acs.T
    last = jnp.sum(jnp.where(rowi == ln - 1, acs, 0.0), axis=0, keepdims=True)

    def col(a, h):
        return jnp.sum(jnp.where(lane == h, a, 0.0), axis=1, keepdims=True)

    def one(a, h):
        return jnp.sum(jnp.where(lane1 == h, a, 0.0), axis=1, keepdims=True)

    def rowv(at, h):
        return jnp.sum(jnp.where(sub == h, at, 0.0), axis=0, keepdims=True)

    cb = [mm_nt(cs[g], bs[g]) for g in range(SSM_GROUPS)]
    ys, news = [], []
    for j in range(SSM_HEADS // 2):
        g = j // 2
        h0, h1 = 2 * j, 2 * j + 1
        xd = xs[j] * jnp.where(first, col(dt, h0), col(dt, h1))
        yd, st, ea, cd = None, None, [], []
        for h, xdh in ((h0, jnp.where(first, xd, 0.0)), (h1, jnp.where(first, 0.0, xd))):
            ac = col(acs, h)
            la = one(last, h)
            lmat = jnp.exp(jnp.where(tril, ac - rowv(acs_t, h), -jnp.inf))
            yh = mm_nn(cb[g] * lmat, xdh)
            sh = mm_tn(bs[g] * jnp.exp(la - ac), xdh)
            yd = yh if yd is None else yd + yh
            st = sh if st is None else st + sh
            ea.append(jnp.exp(ac))
            cd.append(jnp.exp(la))
        yoff = mm_nn(cs[g], prev[j]) * jnp.where(first, ea[0], ea[1])
        ys.append(yd + yoff + xs[j] * jnp.where(first1, one(dsk, h0), one(dsk, h1)))
        news.append(prev[j] * jnp.where(first1, cd[0], cd[1]) + st)
    return ys, news


N_PAIR = SSM_HEADS // 2


def ssd_fwd(xbc, proj, ptile, l):
    s = xbc.shape[0]
    nch = s // SSM_CHUNK

    def body(xbc_ref, small_ref, p_ref, y_ref, prev_ref, state_ref):
        @pl.when(pl.program_id(0) == 0)
        def _():
            state_ref[...] = jnp.zeros_like(state_ref)

        xs = [xbc_ref[:, LANES * j:LANES * (j + 1)] for j in range(N_PAIR)]
        bs = [xbc_ref[:, D_SSM + LANES * g:D_SSM + LANES * (g + 1)] for g in range(SSM_GROUPS)]
        cs = [xbc_ref[:, D_SSM + 512 + LANES * g:D_SSM + 512 + LANES * (g + 1)] for g in range(SSM_GROUPS)]
        prev = [state_ref[j] for j in range(N_PAIR)]
        ys, news = _ssd_chunk(xs, bs, cs, small_ref[...], p_ref[0:1, :], p_ref[1:2, :], p_ref[2:3, :], prev)
        for j in range(N_PAIR):
            y_ref[:, LANES * j:LANES * (j + 1)] = ys[j]
            prev_ref[0, j] = prev[j]
            state_ref[j] = news[j]

    return pl.pallas_call(
        body, name="ssd_fwd", grid=(nch,),
        in_specs=[pl.BlockSpec((SSM_CHUNK, CONV_CH), lambda c: (c, 0)),
                  pl.BlockSpec((SSM_CHUNK, LANES), lambda c: (c, OFF_SMALL // LANES)),
                  pl.BlockSpec((None, 8, LANES), lambda c: (l, 0, 0))],
        out_specs=[pl.BlockSpec((SSM_CHUNK, D_SSM), lambda c: (c, 0)),
                   pl.BlockSpec((1, N_PAIR, SSM_STATE, LANES), lambda c: (c, 0, 0, 0))],
        out_shape=[jax.ShapeDtypeStruct((s, D_SSM), F32), jax.ShapeDtypeStruct((nch, N_PAIR, SSM_STATE, LANES), F32)],
        scratch_shapes=[pltpu.VMEM((N_PAIR, SSM_STATE, LANES), F32)],
        compiler_params=_params(dimension_semantics=("arbitrary",)),
    )(xbc, proj, ptile)


def ssd_bwd(xbc, proj, ptile, l, prevs, dy):
    s = xbc.shape[0]
    nch = s // SSM_CHUNK

    def body(xbc_ref, small_ref, p_ref, prev_ref, dy_ref, dxbc_ref, dsmall_ref, dp_ref, dstate_ref):
        @pl.when(pl.program_id(0) == 0)
        def _():
            dstate_ref[...] = jnp.zeros_like(dstate_ref)
            dp_ref[...] = jnp.zeros_like(dp_ref)

        xs = [xbc_ref[:, LANES * j:LANES * (j + 1)] for j in range(N_PAIR)]
        bs = [xbc_ref[:, D_SSM + LANES * g:D_SSM + LANES * (g + 1)] for g in range(SSM_GROUPS)]
        cs = [xbc_ref[:, D_SSM + 512 + LANES * g:D_SSM + 512 + LANES * (g + 1)] for g in range(SSM_GROUPS)]
        prev = [prev_ref[0, j] for j in range(N_PAIR)]
        dys = [dy_ref[:, LANES * j:LANES * (j + 1)] for j in range(N_PAIR)]
        dnew = [dstate_ref[j] for j in range(N_PAIR)]
        _, vjp = jax.vjp(_ssd_chunk, xs, bs, cs, small_ref[...], p_ref[0:1, :], p_ref[1:2, :], p_ref[2:3, :], prev)
        dxs, dbs, dcs, dsmall, ddtb, dalog, ddsk, dprev = vjp((dys, dnew))
        for j in range(N_PAIR):
            dxbc_ref[:, LANES * j:LANES * (j + 1)] = dxs[j]
            dstate_ref[j] = dprev[j]
        for g in range(SSM_GROUPS):
            dxbc_ref[:, D_SSM + LANES * g:D_SSM + LANES * (g + 1)] = dbs[g]
            dxbc_ref[:, D_SSM + 512 + LANES * g:D_SSM + 512 + LANES * (g + 1)] = dcs[g]
        dsmall_ref[...] = dsmall
        dp_ref[0:1, :] += ddtb
        dp_ref[1:2, :] += dalog
        dp_ref[2:3, :] += ddsk

    rev = lambda c: nch - 1 - c
    return pl.pallas_call(
        body, name="ssd_bwd", grid=(nch,),
        in_specs=[pl.BlockSpec((SSM_CHUNK, CONV_CH), lambda c: (rev(c), 0)),
                  pl.BlockSpec((SSM_CHUNK, LANES), lambda c: (rev(c), OFF_SMALL // LANES)),
                  pl.BlockSpec((None, 8, LANES), lambda c: (l, 0, 0)),
                  pl.BlockSpec((1, N_PAIR, SSM_STATE, LANES), lambda c: (rev(c), 0, 0, 0)),
                  pl.BlockSpec((SSM_CHUNK, D_SSM), lambda c: (rev(c), 0))],
        out_specs=[pl.BlockSpec((SSM_CHUNK, CONV_CH), lambda c: (rev(c), 0)),
                   pl.BlockSpec((SSM_CHUNK, LANES), lambda c: (rev(c), 0)),
                   pl.BlockSpec((8, LANES), lambda c: (0, 0))],
        out_shape=[jax.ShapeDtypeStruct((s, CONV_CH), F32), jax.ShapeDtypeStruct((s, LANES), F32),
                   jax.ShapeDtypeStruct((8, LANES), F32)],
        scratch_shapes=[pltpu.VMEM((N_PAIR, SSM_STATE, LANES), F32)],
        compiler_params=_params(dimension_semantics=("arbitrary",)),
    )(xbc, proj, ptile, prevs, dy)


ROPE_TM = 256


def _rope_tile(t, cosm, sinm):
    lane = lax.broadcasted_iota(jnp.int32, t.shape, 1)
    half = QK_ROPE // 2
    partner = jnp.where(lane < ROPE_LANE0 + half, pltpu.roll(t, LANES - half, 1), pltpu.roll(t, half, 1))
    return t * cosm + partner * sinm


def _in_rope(shape):
    lane = lax.broadcasted_iota(jnp.int32, shape, 1)
    return jnp.logical_and(lane >= ROPE_LANE0, lane < ROPE_LANE0 + QK_ROPE)


def rope_q(q, cosm, sinm, name):
    s, w = q.shape

    def body(q_ref, c_ref, s_ref, o_ref):
        c, sn = c_ref[...], s_ref[...]
        for h in range(MLA_HEADS):
            sl = slice(HEAD_PAD * h, HEAD_PAD * (h + 1))
            o_ref[:, sl] = _rope_tile(q_ref[:, sl].astype(F32), c, sn).astype(o_ref.dtype)

    row = pl.BlockSpec((ROPE_TM, w), lambda i: (i, 0))
    tab = pl.BlockSpec((ROPE_TM, LANES), lambda i: (i, 0))
    return pl.pallas_call(
        body, name=name, grid=(s // ROPE_TM,), in_specs=[row, tab, tab], out_specs=row,
        out_shape=jax.ShapeDtypeStruct((s, w), BF16), compiler_params=_params(dimension_semantics=("arbitrary",)),
    )(q, cosm, sinm)


def build_k(kn, proj, cosm, sinm):
    s, w = kn.shape

    def body(k_ref, small_ref, c_ref, s_ref, o_ref):
        small = small_ref[...]
        inrope = _in_rope(small.shape)
        kpe = jnp.where(inrope, _rope_tile(jnp.where(inrope, small, 0.0), c_ref[...], s_ref[...]), 0.0)
        for h in range(MLA_HEADS):
            sl = slice(HEAD_PAD * h, HEAD_PAD * (h + 1))
            o_ref[:, sl] = (k_ref[:, sl].astype(F32) + kpe).astype(o_ref.dtype)

    row = pl.BlockSpec((ROPE_TM, w), lambda i: (i, 0))
    tab = pl.BlockSpec((ROPE_TM, LANES), lambda i: (i, 0))
    return pl.pallas_call(
        body, name="build_k", grid=(s // ROPE_TM,),
        in_specs=[row, pl.BlockSpec((ROPE_TM, LANES), lambda i: (i, OFF_SMALL // LANES)), tab, tab], out_specs=row,
        out_shape=jax.ShapeDtypeStruct((s, w), BF16), compiler_params=_params(dimension_semantics=("arbitrary",)),
    )(kn, proj, cosm, sinm)


def dsmall_bwd(dk, dsmall_ssd, cosm, sinm_neg):
    def fn(dkt, ds, c, sn):
        inrope = _in_rope(ds.shape)
        tot = dkt[:, 0:HEAD_PAD]
        for h in range(1, MLA_HEADS):
            tot = tot + dkt[:, HEAD_PAD * h:HEAD_PAD * (h + 1)]
        tot = jnp.where(inrope, tot, 0.0)
        return ds + jnp.where(inrope, _rope_tile(tot, c, sn), 0.0)

    return rowwise(fn, [(dk, MLA_HEADS * HEAD_PAD, 0), (dsmall_ssd, LANES, 0), (cosm, LANES, 0), (sinm_neg, LANES, 0)],
                   [], [(LANES, BF16)], [], "dsmall_bwd")[0]


ATT_TQ = 256
ATT_SCALE = (QK_NOPE + QK_ROPE) ** -0.5


def _att_scores(qh, kh, q0):
    s = lax.dot_general(qh, kh, _DIMS['nt'], preferred_element_type=F32) * ATT_SCALE
    r = lax.broadcasted_iota(jnp.int32, s.shape, 0) + q0
    c = lax.broadcasted_iota(jnp.int32, s.shape, 1)
    return jnp.where(c <= r, s, -1e30)


def mla_fwd(q, k, v):
    s = q.shape[0]

    def body(q_ref, k_ref, v_ref, o_ref, lse_ref):
        lane = lax.broadcasted_iota(jnp.int32, (ATT_TQ, LANES), 1)

        def block(ib):
            n = ATT_TQ * (ib + 1)
            v_t = v_ref[0:n, :]
            vlane = lax.broadcasted_iota(jnp.int32, v_t.shape, 1)
            o_tot, lse_tot = None, None
            for h in range(2):
                hs = slice(HEAD_PAD * h, HEAD_PAD * (h + 1))
                sc = _att_scores(q_ref[:, hs], k_ref[0:n, hs], ATT_TQ * ib)
                m = jnp.max(sc, axis=1, keepdims=True)
                p = jnp.exp(sc - m)
                l = jnp.sum(p, axis=1, keepdims=True)
                vh = jnp.where((vlane < V_DIM) if h == 0 else (vlane >= V_DIM), v_t, jnp.zeros_like(v_t))
                oh = lax.dot_general(p.astype(BF16), vh, _DIMS['nn'], preferred_element_type=F32) / l
                lse_h = jnp.where((lane < V_DIM) if h == 0 else (lane >= V_DIM), m + jnp.log(l), 0.0)
                o_tot = oh if o_tot is None else o_tot + oh
                lse_tot = lse_h if lse_tot is None else lse_tot + lse_h
            o_ref[...] = o_tot
            lse_ref[...] = lse_tot

        for ib in range(s // ATT_TQ):
            pl.when(pl.program_id(1) == ib)(functools.partial(block, ib))

    tile = pl.BlockSpec((ATT_TQ, LANES), lambda p, i: (i, p))
    return pl.pallas_call(
        body, name="mla_fwd", grid=(MLA_HEADS // 2, s // ATT_TQ),
        in_specs=[pl.BlockSpec((ATT_TQ, 2 * HEAD_PAD), lambda p, i: (i, p)),
                  pl.BlockSpec((s, 2 * HEAD_PAD), lambda p, i: (0, p)),
                  pl.BlockSpec((s, LANES), lambda p, i: (0, p))],
        out_specs=[tile, tile],
        out_shape=[jax.ShapeDtypeStruct((s, MLA_HEADS * V_DIM), F32)] * 2,
        compiler_params=_params(dimension_semantics=("arbitrary", "arbitrary")),
    )(q, k, v)


def mla_bwd(q, k, v, o, lse, do):
    s = q.shape[0]

    def body(q_ref, k_ref, v_ref, o_ref, lse_ref, do_ref, dq_ref, dk_ref, dv_ref):
        i = pl.program_id(1)

        @pl.when(i == 0)
        def _():
            dk_ref[...] = jnp.zeros_like(dk_ref)
            dv_ref[...] = jnp.zeros_like(dv_ref)

        def block(ib):
            n = ATT_TQ * (ib + 1)
            o_t = o_ref[...]
            do_t = do_ref[...]
            lse_t = lse_ref[...]
            v_t = v_ref[0:n, :]
            lane = lax.broadcasted_iota(jnp.int32, do_t.shape, 1)
            for h in range(2):
                hs = slice(HEAD_PAD * h, HEAD_PAD * (h + 1))
                sel = (lane < V_DIM) if h == 0 else (lane >= V_DIM)
                qh = q_ref[:, hs]
                kh = k_ref[0:n, hs]
                doh = jnp.where(sel, do_t, 0.0)
                delta = jnp.sum(doh * o_t, axis=1, keepdims=True)
                lse_h = jnp.max(jnp.where(sel, lse_t, -jnp.inf), axis=1, keepdims=True)
                doh_b = doh.astype(BF16)
                p = jnp.exp(_att_scores(qh, kh, ATT_TQ * ib) - lse_h)
                dv_ref[0:n, :] += lax.dot_general(p.astype(BF16), doh_b, _DIMS['tn'], preferred_element_type=F32)
                dp = lax.dot_general(doh_b, v_t, _DIMS['nt'], preferred_element_type=F32)
                ds = (p * (dp - delta) * ATT_SCALE).astype(BF16)
                dk_ref[0:n, hs] += lax.dot_general(ds, qh, _DIMS['tn'], preferred_element_type=F32)
                dq_ref[:, hs] = lax.dot_general(ds, kh, _DIMS['nn'], preferred_element_type=F32).astype(dq_ref.dtype)

        for ib in range(s // ATT_TQ):
            pl.when(i == ib)(functools.partial(block, ib))

    tile = pl.BlockSpec((ATT_TQ, LANES), lambda p, i: (i, p))
    return pl.pallas_call(
        body, name="mla_bwd", grid=(MLA_HEADS // 2, s // ATT_TQ),
        in_specs=[pl.BlockSpec((ATT_TQ, 2 * HEAD_PAD), lambda p, i: (i, p)),
                  pl.BlockSpec((s, 2 * HEAD_PAD), lambda p, i: (0, p)),
                  pl.BlockSpec((s, LANES), lambda p, i: (0, p)), tile, tile, tile],
        out_specs=[pl.BlockSpec((ATT_TQ, 2 * HEAD_PAD), lambda p, i: (i, p)),
                   pl.BlockSpec((s, 2 * HEAD_PAD), lambda p, i: (0, p)),
                   pl.BlockSpec((s, LANES), lambda p, i: (0, p))],
        out_shape=[jax.ShapeDtypeStruct((s, MLA_HEADS * HEAD_PAD), F32),
                   jax.ShapeDtypeStruct((s, MLA_HEADS * HEAD_PAD), F32),
                   jax.ShapeDtypeStruct((s, MLA_HEADS * V_DIM), F32)],
        compiler_params=_params(dimension_semantics=("arbitrary", "arbitrary")),
    )(q, k, v, o, lse, do)


MEM_TQ = 256
MEM_SCALE = MEM_HEAD_DIM ** -0.5


def _mem_probs(qh, kh):
    s = lax.dot_general(qh, kh, _DIMS['nt'], preferred_element_type=F32) * MEM_SCALE
    p = jnp.exp(s - jnp.max(s, axis=1, keepdims=True))
    return p / jnp.sum(p, axis=1, keepdims=True)


def mem_fwd(q, k, v):
    s = q.shape[0]

    def body(q_ref, k_ref, v_ref, o_ref):
        for h in range(MEM_HEADS):
            sl = slice(MEM_HEAD_DIM * h, MEM_HEAD_DIM * (h + 1))
            p = _mem_probs(q_ref[:, sl], k_ref[:, sl])
            o_ref[:, sl] = lax.dot_general(p.astype(BF16), v_ref[:, sl], _DIMS['nn'],
                                           preferred_element_type=F32).astype(o_ref.dtype)

    full = pl.BlockSpec((MEM_LEN, D_MODEL), lambda i: (0, 0))
    return pl.pallas_call(
        body, name="mem_fwd", grid=(s // MEM_TQ,),
        in_specs=[pl.BlockSpec((MEM_TQ, D_MODEL), lambda i: (i, 0)), full, full],
        out_specs=pl.BlockSpec((MEM_TQ, D_MODEL), lambda i: (i, 0)),
        out_shape=jax.ShapeDtypeStruct((s, D_MODEL), BF16),
        compiler_params=_params(dimension_semantics=("arbitrary",)),
    )(q, k, v)


def mem_bwd(q, k, v, do):
    s = q.shape[0]

    def body(q_ref, k_ref, v_ref, do_ref, dq_ref, dk_ref, dv_ref):
        @pl.when(pl.program_id(0) == 0)
        def _():
            dk_ref[...] = jnp.zeros_like(dk_ref)
            dv_ref[...] = jnp.zeros_like(dv_ref)

        for h in range(MEM_HEADS):
            sl = slice(MEM_HEAD_DIM * h, MEM_HEAD_DIM * (h + 1))
            qh, kh, vh = q_ref[:, sl], k_ref[:, sl], v_ref[:, sl]
            doh = do_ref[:, sl].astype(BF16)
            p = _mem_probs(qh, kh)
            dv_ref[:, sl] += lax.dot_general(p.astype(BF16), doh, _DIMS['tn'], preferred_element_type=F32)
            dp = lax.dot_general(doh, vh, _DIMS['nt'], preferred_element_type=F32)
            ds = (p * (dp - jnp.sum(p * dp, axis=1, keepdims=True)) * MEM_SCALE).astype(BF16)
            dq_ref[:, sl] = lax.dot_general(ds, kh, _DIMS['nn'], preferred_element_type=F32).astype(dq_ref.dtype)
            dk_ref[:, sl] += lax.dot_general(ds, qh, _DIMS['tn'], preferred_element_type=F32)

    full = pl.BlockSpec((MEM_LEN, D_MODEL), lambda i: (0, 0))
    row = pl.BlockSpec((MEM_TQ, D_MODEL), lambda i: (i, 0))
    return pl.pallas_call(
        body, name="mem_bwd", grid=(s // MEM_TQ,),
        in_specs=[row, full, full, row], out_specs=[row, full, full],
        out_shape=[jax.ShapeDtypeStruct((s, D_MODEL), BF16), jax.ShapeDtypeStruct((MEM_LEN, D_MODEL), F32),
                   jax.ShapeDtypeStruct((MEM_LEN, D_MODEL), F32)],
        compiler_params=_params(dimension_semantics=("arbitrary",)),
    )(q, k, v, do)


def _gate_norm(y, z, g):
    return _rms(y * _silu(z), g)


def gate_norm_fwd(y, proj, g):
    return rowwise(_gate_norm, [(y, D_SSM, 0), (proj, D_SSM, OFF_Z // D_SSM)], [g], [(D_SSM, BF16, D_MIX, 0)], [],
                   "gate_norm_fwd")[0]


def gate_norm_bwd(y, proj, g, dmix, tie=None):
    def fn(yt, zt, dt_, gt):
        _, vjp = jax.vjp(_gate_norm, yt, zt, gt)
        return vjp(dt_.astype(F32))

    return rowwise(fn, [(y, D_SSM, 0), (proj, D_SSM, OFF_Z // D_SSM), (dmix, D_SSM, 0)], [g],
                   [(D_SSM, F32), (D_SSM, BF16)], [((1, D_SSM), F32)], "gate_norm_bwd", tie=tie)


def loss_head(x, g, target):
    def fn(xt, tt, gt):
        def f(x_, g_):
            err = _rms(x_, g_) - tt
            return 0.5 * jnp.sum(jnp.mean(err * err, axis=-1))

        lv, (dx, dg) = jax.value_and_grad(f, argnums=(0, 1))(xt, gt)
        return dx, dg, jnp.full((1, LANES), lv, F32)

    return rowwise(fn, [(x, D_MODEL, 0), (target, D_MODEL, 0)], [g], [(D_MODEL, F32)],
                   [((1, D_MODEL), F32), ((1, LANES), F32)], "loss_head")


def _proj_runs(d):
    lo, hi = (D_IN // N_DEV) * d, (D_IN // N_DEV) * (d + 1)
    runs = []
    for a, b, new in PROJ_SEGS:
        s0, s1 = max(a, lo), min(b, hi)
        if s0 < s1:
            runs.append((s0 - lo, new + s0 - a, s1 - s0))
    return runs


LAYOUT_TM = 256


def assemble_proj(g):
    def body(g_ref, o_ref):
        o_ref[:, OFF_SMALL:OFF_SMALL + LANES] = jnp.zeros((LAYOUT_TM, LANES), o_ref.dtype)
        for d in range(N_DEV):
            for src, dst, n in _proj_runs(d):
                o_ref[:, dst:dst + n] = g_ref[d, :, src:src + n]

    return pl.pallas_call(
        body, name="assemble_proj", grid=(D_MODEL // LAYOUT_TM,),
        in_specs=[pl.BlockSpec((N_DEV, LAYOUT_TM, D_IN // N_DEV), lambda i: (0, i, 0))],
        out_specs=pl.BlockSpec((LAYOUT_TM, PROJ_W), lambda i: (i, 0)),
        out_shape=jax.ShapeDtypeStruct((D_MODEL, PROJ_W), g.dtype),
        compiler_params=_params(dimension_semantics=("arbitrary",)),
    )(g)


def extract_proj(dz, dxbc, dcq, dsmall, dckv):
    pieces = [(OFF_Z, 1024), (OFF_XBC, 2048), (OFF_CQ, Q_LORA), (OFF_SMALL, LANES), (OFF_CKV, KV_LORA)]

    def body(*refs):
        o_ref = refs[-1]
        for d in range(N_DEV):
            for src, dst, n in _proj_runs(d):
                for p, (off, w) in enumerate(pieces):
                    if off <= dst < off + w:
                        o_ref[d, :, src:src + n] = refs[p][:, dst - off:dst - off + n].astype(o_ref.dtype)

    return pl.pallas_call(
        body, name="extract_proj", grid=(D_MODEL // LAYOUT_TM,),
        in_specs=[pl.BlockSpec((LAYOUT_TM, w), lambda i: (i, 0)) for _, w in pieces],
        out_specs=pl.BlockSpec((N_DEV, LAYOUT_TM, D_IN // N_DEV), lambda i: (0, i, 0)),
        out_shape=jax.ShapeDtypeStruct((N_DEV, D_MODEL, D_IN // N_DEV), BF16),
        compiler_params=_params(dimension_semantics=("arbitrary",)),
    )(dz, dxbc, dcq, dsmall, dckv)


_QW = QK_NOPE + QK_ROPE


def assemble_uq(g):
    def body(g_ref, o_ref):
        o_ref[...] = jnp.zeros_like(o_ref)
        for d in range(N_DEV):
            for e in range(2):
                dst = HEAD_PAD * (2 * d + e)
                o_ref[:, dst:dst + _QW] = g_ref[d, :, _QW * e:_QW * (e + 1)]

    return pl.pallas_call(
        body, name="assemble_uq", grid=(1,),
        in_specs=[pl.BlockSpec((N_DEV, Q_LORA, 2 * _QW), lambda i: (0, 0, 0))],
        out_specs=pl.BlockSpec((Q_LORA, MLA_HEADS * HEAD_PAD), lambda i: (0, 0)),
        out_shape=jax.ShapeDtypeStruct((Q_LORA, MLA_HEADS * HEAD_PAD), g.dtype),
        compiler_params=_params(dimension_semantics=("arbitrary",)),
    )(g)


def extract_uq(dw):
    def body(w_ref, o_ref):
        for d in range(N_DEV):
            for e in range(2):
                src = HEAD_PAD * (2 * d + e)
                o_ref[d, :, _QW * e:_QW * (e + 1)] = w_ref[:, src:src + _QW].astype(o_ref.dtype)

    return pl.pallas_call(
        body, name="extract_uq", grid=(1,),
        in_specs=[pl.BlockSpec((Q_LORA, MLA_HEADS * HEAD_PAD), lambda i: (0, 0))],
        out_specs=pl.BlockSpec((N_DEV, Q_LORA, 2 * _QW), lambda i: (0, 0, 0)),
        out_shape=jax.ShapeDtypeStruct((N_DEV, Q_LORA, 2 * _QW), BF16),
        compiler_params=_params(dimension_semantics=("arbitrary",)),
    )(dw)


def assemble_ukv(g):
    def body(g_ref, kn_ref, v_ref):
        kn_ref[...] = jnp.zeros_like(kn_ref)
        for d in range(N_DEV):
            for e in range(2):
                h = 2 * d + e
                kn_ref[:, HEAD_PAD * h:HEAD_PAD * h + QK_NOPE] = g_ref[d, :, 128 * e:128 * e + QK_NOPE]
                v_ref[:, V_DIM * h:V_DIM * (h + 1)] = g_ref[d, :, 128 * e + QK_NOPE:128 * (e + 1)]

    return pl.pallas_call(
        body, name="assemble_ukv", grid=(1,),
        in_specs=[pl.BlockSpec((N_DEV, KV_LORA, 256), lambda i: (0, 0, 0))],
        out_specs=[pl.BlockSpec((KV_LORA, MLA_HEADS * HEAD_PAD), lambda i: (0, 0)),
                   pl.BlockSpec((KV_LORA, MLA_HEADS * V_DIM), lambda i: (0, 0))],
        out_shape=[jax.ShapeDtypeStruct((KV_LORA, MLA_HEADS * HEAD_PAD), g.dtype),
                   jax.ShapeDtypeStruct((KV_LORA, MLA_HEADS * V_DIM), g.dtype)],
        compiler_params=_params(dimension_semantics=("arbitrary",)),
    )(g)


def extract_ukv(dkn, dv):
    def body(kn_ref, v_ref, o_ref):
        for d in range(N_DEV):
            for e in range(2):
                h = 2 * d + e
                o_ref[d, :, 128 * e:128 * e + QK_NOPE] = kn_ref[:, HEAD_PAD * h:HEAD_PAD * h + QK_NOPE].astype(o_ref.dtype)
                o_ref[d, :, 128 * e + QK_NOPE:128 * (e + 1)] = v_ref[:, V_DIM * h:V_DIM * (h + 1)].astype(o_ref.dtype)

    return pl.pallas_call(
        body, name="extract_ukv", grid=(1,),
        in_specs=[pl.BlockSpec((KV_LORA, MLA_HEADS * HEAD_PAD), lambda i: (0, 0)),
                  pl.BlockSpec((KV_LORA, MLA_HEADS * V_DIM), lambda i: (0, 0))],
        out_specs=pl.BlockSpec((N_DEV, KV_LORA, 256), lambda i: (0, 0, 0)),
        out_shape=jax.ShapeDtypeStruct((N_DEV, KV_LORA, 256), BF16),
        compiler_params=_params(dimension_semantics=("arbitrary",)),
    )(dkn, dv)


_UPW = 2 * D_FF // N_DEV


def assemble_up(g):
    def body(g_ref, wg_ref, wv_ref):
        for d in range(N_DEV):
            ref = wg_ref if d < N_DEV // 2 else wv_ref
            off = _UPW * (d % (N_DEV // 2))
            ref[:, off:off + _UPW] = g_ref[d]

    half = pl.BlockSpec((LAYOUT_TM, D_FF), lambda i: (i, 0))
    return pl.pallas_call(
        body, name="assemble_up", grid=(D_MODEL // LAYOUT_TM,),
        in_specs=[pl.BlockSpec((N_DEV, LAYOUT_TM, _UPW), lambda i: (0, i, 0))],
        out_specs=[half, half], out_shape=[jax.ShapeDtypeStruct((D_MODEL, D_FF), g.dtype)] * 2,
        compiler_params=_params(dimension_semantics=("arbitrary",)),
    )(g)


def extract_up(dwg, dwv):
    def body(wg_ref, wv_ref, o_ref):
        for d in range(N_DEV):
            ref = wg_ref if d < N_DEV // 2 else wv_ref
            off = _UPW * (d % (N_DEV // 2))
            o_ref[d] = ref[:, off:off + _UPW].astype(o_ref.dtype)

    half = pl.BlockSpec((LAYOUT_TM, D_FF), lambda i: (i, 0))
    return pl.pallas_call(
        body, name="extract_up", grid=(D_MODEL // LAYOUT_TM,), in_specs=[half, half],
        out_specs=pl.BlockSpec((N_DEV, LAYOUT_TM, _UPW), lambda i: (0, i, 0)),
        out_shape=jax.ShapeDtypeStruct((N_DEV, D_MODEL, _UPW), BF16),
        compiler_params=_params(dimension_semantics=("arbitrary",)),
    )(dwg, dwv)


MESH = pl.DeviceIdType.MESH
ANY = pl.BlockSpec(memory_space=pl.ANY)


def _place():
    mx, my, mc = lax.axis_index("x"), lax.axis_index("y"), lax.axis_index("c")
    return mx, my, mc, [(1 - mx, my), (mx, 1 - my), (1 - mx, 1 - my)]


def all_gather_blocks(xs, first_only=()):
    n = len(xs)

    def body(*refs):
        x_refs, out_refs = refs[:n], refs[n:2 * n]
        send_sems, recv_sems, local_sems = refs[2 * n:]
        mx, my, mc, chips = _place()
        me, sibling = (mx, my, mc), (mx, my, 1 - mc)
        x_refs = [x_refs[t].at[0] if t in first_only else x_refs[t] for t in range(n)]

        def rows(t, px, py, pc):
            dev = 4 * px + 2 * py + pc
            return out_refs[t].at[dev] if t in first_only else out_refs[t].at[:, dev]

        def copy(t, k, block, to, src=None):
            return pltpu.make_async_remote_copy(
                src_ref=rows(t, *block) if src is None else src, dst_ref=rows(t, *block),
                send_sem=send_sems.at[t, k], recv_sem=recv_sems.at[t, k], device_id=to, device_id_type=MESH)

        mine = [pltpu.make_async_copy(x_refs[t], rows(t, *me), local_sems.at[t]) for t in range(n)]
        for cp in mine:
            cp.start()
        first = []
        for t in range(n):
            first.append(copy(t, 0, me, sibling, src=x_refs[t]))
            first += [copy(t, 1 + j, me, (*chip, mc), src=x_refs[t]) for j, chip in enumerate(chips)]
        for cp in first:
            cp.start()
        passed = []
        for j, chip in enumerate(chips):
            for t in range(n):
                copy(t, 1 + j, (*chip, mc), me).wait_recv()
                cp = copy(t, 4 + j, (*chip, mc), sibling)
                cp.start()
                passed.append(cp)
        for t in range(n):
            copy(t, 0, sibling, me).wait_recv()
            for j, chip in enumerate(chips):
                copy(t, 4 + j, (*chip, 1 - mc), me).wait_recv()
        for cp in first + passed:
            cp.wait_send()
        for cp in mine:
            cp.wait()

    return pl.pallas_call(
        body, name="all_gather_blocks",
        out_shape=[jax.ShapeDtypeStruct(((N_DEV,) if t in first_only else (x.shape[0], N_DEV)) + x.shape[1:], x.dtype)
                   for t, x in enumerate(xs)],
        in_specs=[ANY] * n, out_specs=[ANY] * n,
        scratch_shapes=[pltpu.SemaphoreType.DMA((n, 7)), pltpu.SemaphoreType.DMA((n, 7)), pltpu.SemaphoreType.DMA((n,))],
    )(*xs)


HBM = pl.BlockSpec(memory_space=pltpu.HBM)
SEM = pl.BlockSpec(memory_space=pltpu.SEMAPHORE)
EFFECT = pltpu.SideEffectType.DATAFLOW_SIDE_EFFECTING
ALL_DEVICES = [(px, py, pc) for px in range(2) for py in range(2) for pc in range(2)]


def _hbm(x):
    return pltpu.with_memory_space_constraint(x, pltpu.HBM)


def _split_start(body, name, srcs, lands):
    n = len(srcs)

    def full_body(*refs):
        body(refs[:n], refs[n:2 * n], refs[2 * n], refs[2 * n + 1])
        refs[-1][...] = jnp.zeros_like(refs[-1])

    res = pl.pallas_call(
        full_body, name=name,
        out_shape=(pltpu.SemaphoreType.DMA((n,)), pltpu.SemaphoreType.DMA((n,)),
                   *[pltpu.HBM(x.shape, x.dtype) for x in srcs], *[pltpu.HBM(x.shape, x.dtype) for x in lands],
                   jax.ShapeDtypeStruct((8, LANES), F32)),
        in_specs=[HBM] * (2 * n), out_specs=(SEM, SEM, *[HBM] * (2 * n), pl.BlockSpec(memory_space=pltpu.VMEM)),
        input_output_aliases={i: 2 + i for i in range(2 * n)},
        compiler_params=pltpu.CompilerParams(has_side_effects=EFFECT),
    )(*[_hbm(x) for x in srcs], *[_hbm(x) for x in lands])
    return res[0], res[1], list(res[2:2 + n]), list(res[2 + n:2 + 2 * n]), res[-1]


def _split_wait(name, send_sems, recv_sems, srcs, lands, after, sent, landed):
    n = len(srcs)

    def body(*refs):
        src_refs, land_refs, ssem, rsem = refs[:n], refs[n:2 * n], refs[2 * n], refs[2 * n + 1]
        mx, my, mc, _ = _place()
        for t in range(n):
            out = sent(src_refs[t], land_refs[t])
            inn = landed(land_refs[t])
            pltpu.make_async_remote_copy(src_ref=out, dst_ref=out, send_sem=ssem.at[t], recv_sem=rsem.at[t],
                                         device_id=(mx, my, mc), device_id_type=MESH).wait_send()
            pltpu.make_async_remote_copy(src_ref=inn, dst_ref=inn, send_sem=ssem.at[t], recv_sem=rsem.at[t],
                                         device_id=(mx, my, mc), device_id_type=MESH).wait_recv()

    res = pl.pallas_call(
        body, name=name,
        out_shape=(*[pltpu.HBM(x.shape, x.dtype) for x in srcs], *[pltpu.HBM(x.shape, x.dtype) for x in lands]),
        in_specs=[HBM] * (2 * n) + [SEM, SEM, ANY], out_specs=[HBM] * (2 * n),
        input_output_aliases={i: i for i in range(2 * n)},
        compiler_params=pltpu.CompilerParams(has_side_effects=EFFECT),
    )(*srcs, *lands, send_sems, recv_sems, after)
    return list(res[:n]), list(res[n:])


def gather_start(srcs, l):
    lands = [lax.empty((N_DEV,) + x.shape[1:], x.dtype) for x in srcs]

    def body(src_refs, land_refs, send_sems, recv_sems):
        mx, my, mc, _ = _place()
        me = 4 * mx + 2 * my + mc
        for t in range(len(srcs)):
            for to in ALL_DEVICES:
                pltpu.make_async_remote_copy(
                    src_ref=src_refs[t].at[l], dst_ref=land_refs[t].at[me], send_sem=send_sems.at[t],
                    recv_sem=recv_sems.at[t], device_id=to, device_id_type=MESH).start()

    return _split_start(body, "gather_start_%d" % l, srcs, lands)


def gather_wait(l, send_sems, recv_sems, srcs, lands, after):
    return _split_wait("gather_wait_%d" % l, send_sems, recv_sems, srcs, lands, after,
                       sent=lambda s, d: d, landed=lambda d: d)


def grad_exchange_start(es, lands, l, tag):
    def body(e_refs, land_refs, send_sems, recv_sems):
        mx, my, mc, _ = _place()
        me = 4 * mx + 2 * my + mc
        for t in range(len(es)):
            for px, py, pc in ALL_DEVICES:
                pltpu.make_async_remote_copy(
                    src_ref=e_refs[t].at[4 * px + 2 * py + pc], dst_ref=land_refs[t].at[l, me], send_sem=send_sems.at[t],
                    recv_sem=recv_sems.at[t], device_id=(px, py, pc), device_id_type=MESH).start()

    return _split_start(body, "grad_exchange_start_%d%s" % (l, tag), es, lands)


def grad_exchange_wait(l, tag, send_sems, recv_sems, es, lands, after):
    return _split_wait("grad_exchange_wait_%d%s" % (l, tag), send_sems, recv_sems, es, lands, after,
                       sent=lambda s, d: s, landed=lambda d: d.at[l])


def _adam(g, w, m, v):
    nm = ADAM_B1 * m + (1.0 - ADAM_B1) * g
    nv = ADAM_B2 * v + (1.0 - ADAM_B2) * jnp.square(g)
    m_hat = nm / (1.0 - ADAM_B1 ** ADAM_STEP)
    v_hat = nv / (1.0 - ADAM_B2 ** ADAM_STEP)
    return -ADAM_LR * (m_hat / (jnp.sqrt(v_hat) + ADAM_EPS) + ADAM_WD * w), nm, nv


def adamw_big(parts, w, m, v, name):
    depth, _, a, b = parts.shape
    ta = _row_tile(a)

    def body(p_ref, w_ref, m_ref, v_ref, g_ref, d_ref, nm_ref, nv_ref):
        g = p_ref[0].astype(F32)
        for k in range(1, N_DEV):
            g = g + p_ref[k].astype(F32)
        g_ref[...] = g
        d_ref[...], nm_ref[...], nv_ref[...] = _adam(g, w_ref[...], m_ref[...], v_ref[...])

    blk = pl.BlockSpec((None, ta, b), lambda l, i: (l, i, 0))
    return pl.pallas_call(
        body, name=name, grid=(depth, a // ta),
        in_specs=[pl.BlockSpec((None, N_DEV, ta, b), lambda l, i: (l, 0, i, 0)), blk, blk, blk], out_specs=[blk] * 4,
        out_shape=[jax.ShapeDtypeStruct((depth, a, b), F32)] * 4,
        compiler_params=_params(dimension_semantics=("arbitrary", "arbitrary")),
    )(parts, w, m, v)


SMALL_VIEW = {'norm_mix': (DEPTH, 1024), 'ssm_norm': (DEPTH, 1024), 'attn_out_norm': (DEPTH, 1024),
              'norm_mem_q': (DEPTH, 1024), 'norm_mem_kv': (DEPTH, 1024), 'norm_ffn': (DEPTH, 1024),
              'q_norm': (DEPTH, 384), 'kv_norm': (DEPTH, 256), 'ssm_conv_b': (DEPTH, 2048), 'ffn_conv_b': (DEPTH, 5632),
              'dt_bias': (DEPTH, SSM_HEADS), 'a_log': (DEPTH, SSM_HEADS), 'd_skip': (DEPTH, SSM_HEADS),
              'ssm_conv_w': (DEPTH, SSM_CONV * CONV_CH // N_DEV), 'ffn_conv_w': (DEPTH, FFN_CONV * 2 * D_FF // N_DEV),
              'final_norm': (1, 1024)}
SMALL_NAMES = list(SMALL_VIEW)
SMALL_SHARDED = {'ssm_conv_w': (SSM_CONV, CONV_CH // N_DEV, CONV_CH), 'ffn_conv_w': (FFN_CONV, 2 * D_FF // N_DEV, 2 * D_FF)}


def adamw_small(gathered, ws, ms, vs):
    nsm = len(SMALL_NAMES)

    def body(*refs):
        g8_ref = refs[0]
        w_refs, m_refs, v_refs = refs[1:1 + nsm], refs[1 + nsm:1 + 2 * nsm], refs[1 + 2 * nsm:1 + 3 * nsm]
        outs = refs[1 + 3 * nsm:1 + 7 * nsm]
        sum_ref = refs[1 + 7 * nsm]
        shard_bufs = refs[2 + 7 * nsm:]
        tot = g8_ref[:, 0, :]
        for d in range(1, N_DEV):
            tot = tot + g8_ref[:, d, :]
        sum_ref[...] = tot
        mx, my, mc, _ = _place()
        dev = 4 * mx + 2 * my + mc

        def update(i, g):
            d, nm, nv = _adam(g, w_refs[i][...], m_refs[i][...], v_refs[i][...])
            outs[i][...] = g
            outs[nsm + i][...] = d
            outs[2 * nsm + i][...] = nm
            outs[3 * nsm + i][...] = nv

        for i, name in enumerate(SMALL_NAMES):
            rows, cols = SMALL_VIEW[name]
            off = SMALL_OFF[name]
            if name in SMALL_SHARDED:
                taps, per, full = SMALL_SHARDED[name]
                buf = shard_bufs[list(SMALL_SHARDED).index(name)]
                for d in range(N_DEV):
                    @pl.when(dev == d)
                    def _(d=d, taps=taps, per=per, full=full, off=off, buf=buf):
                        for k in range(taps):
                            buf[:, per * k:per * (k + 1)] = sum_ref[:, off + full * k + per * d:off + full * k + per * (d + 1)]
                update(i, buf[...])
            else:
                update(i, sum_ref[0:rows, off:off + cols])

    views = [jax.ShapeDtypeStruct(SMALL_VIEW[n], F32) for n in SMALL_NAMES]
    vmem = pl.BlockSpec(memory_space=pltpu.VMEM)
    res = pl.pallas_call(
        body, name="adamw_small", out_shape=views * 4, in_specs=[vmem] * (1 + 3 * nsm), out_specs=[vmem] * (4 * nsm),
        scratch_shapes=[pltpu.VMEM((DEPTH, SMALL_W), F32)] + [pltpu.VMEM(SMALL_VIEW[n], F32) for n in SMALL_SHARDED],
        compiler_params=_params(),
    )(gathered, *[ws[n] for n in SMALL_NAMES], *[ms[n] for n in SMALL_NAMES], *[vs[n] for n in SMALL_NAMES])
    return [dict(zip(SMALL_NAMES, res[k * nsm:(k + 1) * nsm])) for k in range(4)]


def _layer_weights(gathered):
    w_kn, w_v = assemble_ukv(gathered['w_ukv'])
    w_g, w_vv = assemble_up(gathered['w_up'])
    stacked = lambda n: gathered[n].reshape(N_DEV * BIG[n][0], BIG[n][1])
    return dict(w_proj=assemble_proj(gathered['w_in']), w_uq=assemble_uq(gathered['w_uq']), w_kn=w_kn, w_v=w_v,
                w_g=w_g, w_vv=w_vv, w_out=stacked('w_out'), w_mq=stacked('w_mq'), w_mk=stacked('w_mk'),
                w_mv=stacked('w_mv'), w_mo=stacked('w_mo'), w_down=stacked('w_down'))


def layer_fwd(x0, mem, cosm, sinm, w, sm, l, tie=None):
    gain = lambda n: (sm[n], l)
    sv = dict(x0=x0)
    sv['h1'] = rmsnorm_fwd(x0, gain('norm_mix'), "norm_mix_fwd", tie=tie)
    proj = sv['proj'] = matmul([(sv['h1'], w['w_proj'])], 'nn', F32, "proj_fwd")
    sv['xbc'] = ssm_conv_fwd(proj, sm['ssm_conv_w'], sm['ssm_conv_b'], l)
    sv['y'], sv['prevs'] = ssd_fwd(sv['xbc'], proj, sm['ptile'], l)
    mix = gate_norm_fwd(sv['y'], proj, gain('ssm_norm'))
    sv['cqn'] = rmsnorm_fwd(proj, gain('q_norm'), "q_norm_fwd", Q_LORA, OFF_CQ // Q_LORA)
    sv['ckvn'] = rmsnorm_fwd(proj, gain('kv_norm'), "kv_norm_fwd", KV_LORA, OFF_CKV // KV_LORA)
    q = matmul([(sv['cqn'], w['w_uq'])], 'nn', F32, "uq_fwd")
    sv['q'] = rope_q(q, cosm, sinm, "rope_q_fwd")
    kn = matmul([(sv['ckvn'], w['w_kn'])], 'nn', BF16, "kn_fwd")
    sv['k'] = build_k(kn, proj, cosm, sinm)
    sv['v'] = matmul([(sv['ckvn'], w['w_v'])], 'nn', BF16, "v_fwd")
    sv['o'], sv['lse'] = mla_fwd(sv['q'], sv['k'], sv['v'])
    mix = sv['mix'] = rmsnorm_fwd(sv['o'], gain('attn_out_norm'), "attn_out_norm_fwd", out=(D_SSM, BF16, D_MIX, 1),
                                  into=(mix, 0))
    x1 = sv['x1'] = matmul([(mix, w['w_out'])], 'nn', F32, "out_fwd", add=x0)
    sv['hq'] = rmsnorm_fwd(x1, gain('norm_mem_q'), "norm_mem_q_fwd")
    sv['mn'] = rmsnorm_fwd(mem, gain('norm_mem_kv'), "norm_mem_kv_fwd")
    sv['mq'] = matmul([(sv['hq'], w['w_mq'])], 'nn', BF16, "mq_fwd")
    sv['mk'] = matmul([(sv['mn'], w['w_mk'])], 'nn', BF16, "mk_fwd")
    sv['mv'] = matmul([(sv['mn'], w['w_mv'])], 'nn', BF16, "mv_fwd")
    sv['om'] = mem_fwd(sv['mq'], sv['mk'], sv['mv'])
    x2 = sv['x2'] = matmul([(sv['om'], w['w_mo'])], 'nn', F32, "mo_fwd", add=x1)
    sv['h3'] = rmsnorm_fwd(x2, gain('norm_ffn'), "norm_ffn_fwd")
    sv['ug'] = matmul([(sv['h3'], w['w_g'])], 'nn', F32, "up_g_fwd")
    sv['uv'] = matmul([(sv['h3'], w['w_vv'])], 'nn', F32, "up_v_fwd")
    sv['a'] = ffn_act_fwd(sv['ug'], sv['uv'], sm['ffn_conv_w'], sm['ffn_conv_b'], l)
    x3 = matmul([(sv['a'], w['w_down'])], 'nn', F32, "down_fwd", add=x2)
    return x3, sv


EARLY_GRADS = ('w_down', 'w_up', 'w_mo', 'w_mq', 'w_mk', 'w_mv', 'w_out')
LATE_GRADS = ('w_uq', 'w_ukv', 'w_in')


def layer_bwd(dx3, mem, cosm, sinm_neg, w, sm, l, sv, on_grads, tie=None):
    gain = lambda n: (sm[n], l)
    big, small = {}, {}
    proj = sv['proj']
    da = matmul([(dx3, w['w_down'])], 'nt', BF16, "down_bwd_a", tie=tie)
    big['w_down'] = matmul([(sv['a'], dx3)], 'tn', BF16, "down_bwd_w")
    dug, duv, dcwg, dcwv, dcbg, dcbv = ffn_act_bwd(sv['ug'], sv['uv'], sm['ffn_conv_w'], sm['ffn_conv_b'], l, da)
    small['ffn_conv_w'] = jnp.concatenate([dcwg, dcwv], axis=1)
    small['ffn_conv_b'] = jnp.concatenate([dcbg, dcbv], axis=1)
    dh3 = matmul([(dug, w['w_g']), (duv, w['w_vv'])], 'nt', BF16, "up_bwd_h")
    big['w_up'] = extract_up(matmul([(sv['h3'], dug)], 'tn', BF16, "up_g_bwd_w"),
                             matmul([(sv['h3'], duv)], 'tn', BF16, "up_v_bwd_w"))
    dx2, small['norm_ffn'] = rmsnorm_bwd(sv['x2'], gain('norm_ffn'), dh3, "norm_ffn_bwd", resid=dx3)
    dom = matmul([(dx2, w['w_mo'])], 'nt', BF16, "mo_bwd_a")
    big['w_mo'] = matmul([(sv['om'], dx2)], 'tn', BF16, "mo_bwd_w")
    dmq, dmk, dmv = mem_bwd(sv['mq'], sv['mk'], sv['mv'], dom)
    dhq = matmul([(dmq, w['w_mq'])], 'nt', BF16, "mq_bwd_a")
    big['w_mq'] = matmul([(sv['hq'], dmq)], 'tn', BF16, "mq_bwd_w")
    dmn = matmul([(dmk, w['w_mk']), (dmv, w['w_mv'])], 'nt', BF16, "mkv_bwd_a")
    big['w_mk'] = matmul([(sv['mn'], dmk)], 'tn', BF16, "mk_bwd_w")
    big['w_mv'] = matmul([(sv['mn'], dmv)], 'tn', BF16, "mv_bwd_w")
    _, small['norm_mem_kv'] = rmsnorm_bwd(mem, gain('norm_mem_kv'), dmn, "norm_mem_kv_bwd", dx_dtype=BF16)
    dx1, small['norm_mem_q'] = rmsnorm_bwd(sv['x1'], gain('norm_mem_q'), dhq, "norm_mem_q_bwd", resid=dx2)
    dmix = matmul([(dx1, w['w_out'])], 'nt', BF16, "out_bwd_a")
    big['w_out'] = matmul([(sv['mix'], dx1)], 'tn', BF16, "out_bwd_w")
    early = {n: big.pop(n).reshape((N_DEV,) + BIG[n]) if n != 'w_up' else big.pop(n) for n in EARLY_GRADS}
    tie = on_grads(l, 'a', early)
    dy, dz, small['ssm_norm'] = gate_norm_bwd(sv['y'], proj, gain('ssm_norm'), dmix, tie=tie)
    dxbc_act, dsmall_ssd, small['ptile'] = ssd_bwd(sv['xbc'], proj, sm['ptile'], l, sv['prevs'], dy)
    dxbc, small['ssm_conv_w'], small['ssm_conv_b'] = ssm_conv_bwd(proj, sm['ssm_conv_w'], sm['ssm_conv_b'], l, dxbc_act)
    do, small['attn_out_norm'] = rmsnorm_bwd(sv['o'], gain('attn_out_norm'), dmix, "attn_out_norm_bwd", dh_colblock=1)
    dq_rot, dk, dv = mla_bwd(sv['q'], sv['k'], sv['v'], sv['o'], sv['lse'], do)
    dq = rope_q(dq_rot, cosm, sinm_neg, "rope_q_bwd")
    dsmall = dsmall_bwd(dk, dsmall_ssd, cosm, sinm_neg)
    dcqn = matmul([(dq, w['w_uq'])], 'nt', BF16, "uq_bwd_a")
    big['w_uq'] = extract_uq(matmul([(sv['cqn'], dq)], 'tn', BF16, "uq_bwd_w"))
    dckvn = matmul([(dk, w['w_kn']), (dv, w['w_v'])], 'nt', BF16, "ukv_bwd_a")
    big['w_ukv'] = extract_ukv(matmul([(sv['ckvn'], dk)], 'tn', BF16, "kn_bwd_w"),
                               matmul([(sv['ckvn'], dv)], 'tn', BF16, "v_bwd_w"))
    dcq, small['q_norm'] = rmsnorm_bwd(proj, gain('q_norm'), dcqn, "q_norm_bwd", width=Q_LORA,
                                       colblock=OFF_CQ // Q_LORA, dx_dtype=BF16)
    dckv, small['kv_norm'] = rmsnorm_bwd(proj, gain('kv_norm'), dckvn, "kv_norm_bwd", width=KV_LORA,
                                         colblock=OFF_CKV // KV_LORA, dx_dtype=BF16)
    wp = w['w_proj']
    xbc_half = lambda c: Opnd(dxbc, c0=c, shape=(dxbc.shape[0], 1024))
    wwin = lambda off, width: Opnd(wp, c0=off // width, shape=(D_MODEL, width))
    dh1 = matmul([(dz, wwin(OFF_Z, 1024)), (xbc_half(0), wwin(OFF_XBC, 1024)), (xbc_half(1), wwin(OFF_XBC + 1024, 1024)),
                  (dcq, wwin(OFF_CQ, Q_LORA)), (dsmall, wwin(OFF_SMALL, LANES)), (dckv, wwin(OFF_CKV, KV_LORA))],
                 'nt', BF16, "proj_bwd_a")
    h1 = sv['h1']
    big['w_in'] = extract_proj(
        matmul([(h1, dz)], 'tn', BF16, "proj_z_bwd_w"), matmul([(h1, dxbc)], 'tn', BF16, "proj_xbc_bwd_w"),
        matmul([(h1, dcq)], 'tn', BF16, "proj_cq_bwd_w"), matmul([(h1, dsmall)], 'tn', BF16, "proj_small_bwd_w"),
        matmul([(h1, dckv)], 'tn', BF16, "proj_ckv_bwd_w"))
    dx0, small['norm_mix'] = rmsnorm_bwd(sv['x0'], gain('norm_mix'), dh1, "norm_mix_bwd", resid=dx1)
    return dx0, on_grads(l, 'b', big), small


def _small_row(small, final=None):
    pt = small['ptile']
    parts = []
    for n, wd in SMALL_SEGS:
        if n in ('dt_bias', 'a_log', 'd_skip'):
            parts.append(pt[('dt_bias', 'a_log', 'd_skip').index(n)][None, :])
        elif n in SMALL_SHARDED:
            parts.append(small[n].reshape(1, wd))
        elif n == 'final_norm':
            parts.append(final if final is not None else jnp.zeros((1, wd), F32))
        else:
            parts.append(small[n])
    return jnp.concatenate(parts, axis=1)


def _rope_tables(positions):
    inv_freq = 1.0 / (ROPE_THETA ** (jnp.arange(0, QK_ROPE, 2, dtype=F32) / QK_ROPE))
    ang = positions.astype(F32)[:, None] * inv_freq
    cos, sin = jnp.cos(ang), jnp.sin(ang)
    s = positions.shape[0]
    pad = jnp.zeros((s, LANES - ROPE_LANE0 - QK_ROPE), F32)
    cosm = jnp.concatenate([jnp.ones((s, ROPE_LANE0), F32), cos, cos, pad], axis=1)
    sinm = jnp.concatenate([jnp.zeros((s, ROPE_LANE0), F32), -sin, sin, pad], axis=1)
    return cosm, sinm


def _small_views(rep, conv_full):
    sm = {n: rep[n].reshape(DEPTH, 1, -1) for n in ('norm_mix', 'ssm_norm', 'attn_out_norm', 'norm_mem_q',
                                                    'norm_mem_kv', 'norm_ffn', 'q_norm', 'kv_norm', 'ssm_conv_b',
                                                    'ffn_conv_b')}
    sm.update(conv_full)
    rows = jnp.stack([rep['dt_bias'], rep['a_log'], rep['d_skip']], axis=1)
    sm['ptile'] = jnp.pad(rows, ((0, 0), (0, 8 - 3), (0, LANES - SSM_HEADS)))
    return sm


def local_step(x, mem, positions, target, sm, final_norm, weights_of, on_grads):
    cosm, sinm = _rope_tables(positions)
    sinm_neg = -sinm
    saved, ws = [], []
    h = x
    for l in range(DEPTH):
        gathered, tie = weights_of(l, h)
        ws.append(_layer_weights(gathered))
        h, sv = layer_fwd(h, mem, cosm, sinm, ws[l], sm, l, tie=tie)
        saved.append(sv)
    dx, dfinal, lossv = loss_head(h, (final_norm.reshape(1, 1, -1), 0), target)
    rows = [None] * DEPTH
    tie = None
    for l in reversed(range(DEPTH)):
        dx, tie, small = layer_bwd(dx, mem, cosm, sinm_neg, ws[l], sm, l, saved[l], on_grads, tie=tie)
        rows[l] = _small_row(small, dfinal if l == 0 else None)
    return lossv[0, 0], dx, jnp.concatenate(rows, axis=0)


def kernel(x, mem, positions, norm_mix, w_in, ssm_conv_w, ssm_conv_b, dt_bias, a_log, d_skip, ssm_norm, q_norm, w_uq, kv_norm, w_ukv, attn_out_norm, w_out, norm_mem_q, norm_mem_kv, w_mq, w_mk, w_mv, w_mo, norm_ffn, w_up, ffn_conv_w, ffn_conv_b, w_down, final_norm, loss_target, m_norm_mix, m_w_in, m_ssm_conv_w, m_ssm_conv_b, m_dt_bias, m_a_log, m_d_skip, m_ssm_norm, m_q_norm, m_w_uq, m_kv_norm, m_w_ukv, m_attn_out_norm, m_w_out, m_norm_mem_q, m_norm_mem_kv, m_w_mq, m_w_mk, m_w_mv, m_w_mo, m_norm_ffn, m_w_up, m_ffn_conv_w, m_ffn_conv_b, m_w_down, m_final_norm, v_norm_mix, v_w_in, v_ssm_conv_w, v_ssm_conv_b, v_dt_bias, v_a_log, v_d_skip, v_ssm_norm, v_q_norm, v_w_uq, v_kv_norm, v_w_ukv, v_attn_out_norm, v_w_out, v_norm_mem_q, v_norm_mem_kv, v_w_mq, v_w_mk, v_w_mv, v_w_mo, v_norm_ffn, v_w_up, v_ffn_conv_w, v_ffn_conv_b, v_w_down, v_final_norm):
    args = locals()
    wts = {n: args[n] for n in WEIGHT_NAMES}
    ms = {n: args['m_' + n] for n in WEIGHT_NAMES}
    vs = {n: args['v_' + n] for n in WEIGHT_NAMES}

    nbig = len(BIG_NAMES)
    srcs = [wts[n].astype(BF16) for n in BIG_NAMES]
    got = all_gather_blocks(srcs + [wts[n] for n in SMALL_SHARDED], first_only=tuple(range(nbig)))
    conv_full = {}
    for n, g in zip(SMALL_SHARDED, got[nbig:]):
        taps, per, full = SMALL_SHARDED[n]
        conv_full[n] = jnp.moveaxis(g, 1, 2).reshape(DEPTH, taps, full)
    sm = _small_views(wts, conv_full)
    st = dict(srcs=srcs, gather=None, exchanges=[],
              lands={n: lax.empty((DEPTH, N_DEV) + BIG[n], BF16) for n in BIG_NAMES})

    def weights_of(l, h):
        if l == 0:
            lands = got[:nbig]
        else:
            send_sems, recv_sems, thru, lands = st['gather']
            st['srcs'], lands = gather_wait(l, send_sems, recv_sems, thru, lands, h)
        tie = None
        if l + 1 < DEPTH:
            send_sems, recv_sems, thru, nxt, tie = gather_start(st['srcs'], l + 1)
            st['gather'] = (send_sems, recv_sems, thru, nxt)
        return dict(zip(BIG_NAMES, lands)), tie

    def on_grads(l, tag, big):
        names = list(big)
        send_sems, recv_sems, thru, lands, tie = grad_exchange_start(
            [big[n] for n in names], [st['lands'][n] for n in names], l, tag)
        st['lands'].update(zip(names, lands))
        st['exchanges'].append((l, tag, names, send_sems, recv_sems, thru))
        return tie

    loss_local, dx, small_rows = local_step(x[0], mem[0], positions[0], loss_target[0], sm, final_norm, weights_of,
                                            on_grads)
    outs = [{}, {}, {}, {}]

    small_all = all_gather_blocks([small_rows])[0]
    view = lambda d: {n: d[n].reshape(SMALL_VIEW[n]) for n in SMALL_NAMES}
    res = adamw_small(small_all, view(wts), view(ms), view(vs))
    for k in range(4):
        for n in SMALL_NAMES:
            outs[k][n] = res[k][n].reshape(wts[n].shape)

    after = res[0]['final_norm']
    for l, tag, names, send_sems, recv_sems, thru in st['exchanges']:
        _, lands = grad_exchange_wait(l, tag, send_sems, recv_sems, thru, [st['lands'][n] for n in names], after)
        st['lands'].update(zip(names, lands))
    for n in BIG_NAMES:
        res_n = adamw_big(st['lands'][n], wts[n], ms[n], vs[n], "adamw_" + n)
        for k in range(4):
            outs[k][n] = res_n[k]

    loss = lax.psum(loss_local, ("x", "y", "c"))
    return (loss, dx[None], *[outs[0][n] for n in WEIGHT_NAMES], *[outs[1][n] for n in WEIGHT_NAMES],
            *[outs[2][n] for n in WEIGHT_NAMES], *[outs[3][n] for n in WEIGHT_NAMES])
```

```python
import functools
import math
from typing import Any, NamedTuple, Optional

import jax
import jax.numpy as jnp
from jax import lax
from jax.experimental import pallas as pl
from jax.experimental.pallas import tpu as pltpu

F32 = jnp.float32
BF16 = jnp.bfloat16

D_MODEL = 1024
DEPTH = 4
MEM_LEN = 256
EPS = 1e-6
SSM_HEADS = 16
SSM_HEAD_DIM = 64
D_SSM = 1024
SSM_GROUPS = 4
SSM_STATE = 128
SSM_CONV = 4
SSM_CHUNK = 128
CONV_CH = 2048
MLA_HEADS = 16
QK_NOPE = 64
QK_ROPE = 32
V_DIM = 64
Q_LORA = 384
KV_LORA = 256
ROPE_THETA = 10000.0
MEM_HEADS = 4
MEM_HEAD_DIM = 256
D_FF = 2816
FFN_CONV = 3
D_IN = 3760
D_MIX = 2048
ADAM_LR = 0.001
ADAM_B1 = 0.9
ADAM_B2 = 0.999
ADAM_EPS = 1e-08
ADAM_WD = 0.01
ADAM_STEP = 10

N_DEV = 8
N_CHIP = 4
LANES = 128
HEAD_PAD = 128
PROJ_W = 3840
OFF_Z, OFF_XBC, OFF_CQ, OFF_SMALL, OFF_CKV = 0, 1024, 3072, 3456, 3584
ROPE_LANE0 = 64
VMEM_LIMIT = 56 * 1024 * 1024
MM_BLOCK_BYTES = 4 * 1024 * 1024
WEIGHT_NAMES = ['norm_mix', 'w_in', 'ssm_conv_w', 'ssm_conv_b', 'dt_bias', 'a_log', 'd_skip', 'ssm_norm', 'q_norm',
                'w_uq', 'kv_norm', 'w_ukv', 'attn_out_norm', 'w_out', 'norm_mem_q', 'norm_mem_kv', 'w_mq', 'w_mk',
                'w_mv', 'w_mo', 'norm_ffn', 'w_up', 'ffn_conv_w', 'ffn_conv_b', 'w_down', 'final_norm']
BIG = {'w_in': (1024, 470), 'w_uq': (384, 192), 'w_ukv': (256, 256), 'w_up': (1024, 704), 'w_out': (256, 1024),
       'w_mq': (128, 1024), 'w_mk': (128, 1024), 'w_mv': (128, 1024), 'w_mo': (128, 1024), 'w_down': (352, 1024)}
BIG_NAMES = list(BIG)
PROJ_SEGS = [(0, 1024, OFF_Z), (1024, 3072, OFF_XBC), (3072, 3088, OFF_SMALL), (3088, 3472, OFF_CQ),
             (3472, 3728, OFF_CKV), (3728, 3760, OFF_SMALL + ROPE_LANE0)]
SMALL_SEGS = [('norm_mix', 1024), ('ssm_norm', 1024), ('attn_out_norm', 1024), ('norm_mem_q', 1024),
              ('norm_mem_kv', 1024), ('norm_ffn', 1024), ('q_norm', 384), ('kv_norm', 256), ('ssm_conv_b', 2048),
              ('ffn_conv_b', 5632), ('dt_bias', 128), ('a_log', 128), ('d_skip', 128),
              ('ssm_conv_w', SSM_CONV * CONV_CH), ('ffn_conv_w', FFN_CONV * 2 * D_FF), ('final_norm', 1024)]
SMALL_OFF = {}
_o = 0
for _n, _w in SMALL_SEGS:
    SMALL_OFF[_n] = _o
    _o += _w
SMALL_W = _o


def _params(**kw):
    return pltpu.CompilerParams(vmem_limit_bytes=VMEM_LIMIT, **kw)


def _pick(n, cap):
    if n <= cap:
        return n
    best = None
    for t in range(LANES, cap + 1, LANES):
        if n % t == 0:
            best = t
    assert best is not None, (n, cap)
    return best


def _row_tile(a, cap=256):
    if a <= cap:
        return a
    best = None
    for t in range(16, cap + 1, 16):
        if a % t == 0:
            best = t
    assert best is not None, (a, cap)
    return best


class Opnd(NamedTuple):
    arr: Any
    lead: Optional[int] = None
    r0: int = 0
    c0: int = 0
    shape: Optional[tuple] = None


def _opnd(x):
    return x if isinstance(x, Opnd) else Opnd(x)


def _lshape(o):
    return tuple(o.shape) if o.shape is not None else tuple(o.arr.shape[-2:])


def _spec(o, br, bc, bi, bj):
    rr, cc = _lshape(o)
    assert rr % br == 0 and cc % bc == 0, (rr, cc, br, bc)
    ro, co = o.r0 * (rr // br), o.c0 * (cc // bc)
    if o.lead is None:
        return pl.BlockSpec((br, bc), lambda i, j: (ro + bi(i, j), co + bj(i, j)))
    return pl.BlockSpec((None, br, bc), lambda i, j: (o.lead, ro + bi(i, j), co + bj(i, j)))


_DIMS = {'nn': (((1,), (0,)), ((), ())), 'nt': (((1,), (1,)), ((), ())), 'tn': (((0,), (0,)), ((), ()))}
_ROW = lambda i, j: i
_COL = lambda i, j: j
_ZERO = lambda i, j: 0


def matmul(pairs, mode, out_dtype, name, add=None, tie=None):
    pairs = [(_opnd(a), _opnd(b)) for a, b in pairs]
    a0, b0 = pairs[0]
    if mode == 'nn':
        m, n = _lshape(a0)[0], _lshape(b0)[1]
    elif mode == 'nt':
        m, n = _lshape(a0)[0], _lshape(b0)[0]
    else:
        m, n = _lshape(a0)[1], _lshape(b0)[1]
    isz = lambda o: jnp.dtype(o.arr.dtype).itemsize
    osz = jnp.dtype(out_dtype).itemsize
    cap = lambda budget, per: max(LANES, budget // per // LANES * LANES)
    if mode == 'tn':
        ktok = _lshape(a0)[0]
        tm = _pick(m, cap(3 * MM_BLOCK_BYTES // 2, ktok * isz(a0)))
        tn = _pick(n, cap(3 * MM_BLOCK_BYTES // 2, ktok * isz(b0)))
    else:
        tm = _pick(m, min(2048, cap(2 * MM_BLOCK_BYTES, sum(_lshape(a)[1] * isz(a) for a, _ in pairs))))
        tn = _pick(n, min(cap(3 * MM_BLOCK_BYTES // 2, sum(_lshape(a)[1] * isz(b) for a, b in pairs)),
                          cap(MM_BLOCK_BYTES, tm * osz), n // 2 if n >= 1024 else n))
    npairs = len(pairs)

    def body(*refs):
        o_ref = refs[-1]
        acc = None
        for p in range(npairs):
            a = refs[2 * p][...].astype(BF16)
            b = refs[2 * p + 1][...].astype(BF16)
            d = lax.dot_general(a, b, _DIMS[mode], preferred_element_type=F32)
            acc = d if acc is None else acc + d
        if add is not None:
            acc = acc + refs[2 * npairs][...].astype(F32)
        o_ref[...] = acc.astype(out_dtype)

    tie_specs = [pl.BlockSpec(memory_space=pl.ANY)] if tie is not None else []
    tie_args = [tie] if tie is not None else []

    in_specs, args = [], []
    for a, b in pairs:
        if mode == 'nn':
            k = _lshape(a)[1]
            in_specs += [_spec(a, tm, k, _ROW, _ZERO), _spec(b, k, tn, _ZERO, _COL)]
        elif mode == 'nt':
            k = _lshape(a)[1]
            in_specs += [_spec(a, tm, k, _ROW, _ZERO), _spec(b, tn, k, _COL, _ZERO)]
        else:
            k = _lshape(a)[0]
            in_specs += [_spec(a, k, tm, _ZERO, _ROW), _spec(b, k, tn, _ZERO, _COL)]
        args += [a.arr, b.arr]
    if add is not None:
        in_specs.append(pl.BlockSpec((tm, tn), lambda i, j: (i, j)))
        args.append(add)
    return pl.pallas_call(
        body, name=name, grid=(m // tm, n // tn), in_specs=in_specs + tie_specs,
        out_specs=pl.BlockSpec((tm, tn), lambda i, j: (i, j)),
        out_shape=jax.ShapeDtypeStruct((m, n), out_dtype),
        compiler_params=_params(dimension_semantics=("arbitrary", "arbitrary")),
    )(*args, *tie_args)


def rowwise(fn, rows, fulls, outs, accs, name, tm=256, into=None, tie=None):
    s = rows[0][0].shape[0]
    nrow, nfull, nout, nacc = len(rows), len(fulls), len(outs), len(accs)
    nin = nrow + nfull

    def body(*refs):
        ins = [r[...] for r in refs[:nin]]
        res = fn(*ins)
        if not isinstance(res, (tuple, list)):
            res = (res,)
        orefs = refs[nin + (1 if into is not None else 0) + (1 if tie is not None else 0):]
        for k in range(nout):
            orefs[k][...] = res[k].astype(orefs[k].dtype)
        if nacc:
            @pl.when(pl.program_id(0) == 0)
            def _():
                for k in range(nacc):
                    orefs[nout + k][...] = jnp.zeros_like(orefs[nout + k])

            for k in range(nacc):
                orefs[nout + k][...] += res[nout + k].astype(orefs[nout + k].dtype)

    in_specs = [pl.BlockSpec((tm, w), lambda i, cb=cb: (i, cb)) for _, w, cb in rows]
    in_specs += [pl.BlockSpec((None,) + f.shape[1:], lambda i, ld=ld, nd=f.ndim - 1: (ld,) + (0,) * nd) for f, ld in fulls]
    args = [r[0] for r in rows] + [f for f, _ in fulls]
    aliases = {}
    if into is not None:
        in_specs.append(pl.BlockSpec(memory_space=pl.ANY))
        args.append(into[0])
        aliases = {nin: into[1]}
    if tie is not None:
        in_specs.append(pl.BlockSpec(memory_space=pl.ANY))
        args.append(tie)
    out_specs, out_shape = [], []
    for o in outs:
        w, dt = o[0], o[1]
        total, cb = (o[2], o[3]) if len(o) == 4 else (w, 0)
        out_specs.append(pl.BlockSpec((tm, w), lambda i, cb=cb: (i, cb)))
        out_shape.append(jax.ShapeDtypeStruct((s, total), dt))
    for shp, dt in accs:
        out_specs.append(pl.BlockSpec(shp, lambda i, nd=len(shp): (0,) * nd))
        out_shape.append(jax.ShapeDtypeStruct(shp, dt))
    return pl.pallas_call(
        body, name=name, grid=(s // tm,), in_specs=in_specs, out_specs=out_specs, out_shape=out_shape,
        input_output_aliases=aliases, compiler_params=_params(dimension_semantics=("arbitrary",)),
    )(*args)


def _rms(x, g):
    xf = x.astype(F32)
    var = jnp.mean(xf * xf, axis=-1, keepdims=True)
    return xf * lax.rsqrt(var + EPS) * g


def rmsnorm_fwd(x, g, name, width=None, colblock=0, out=None, into=None, tie=None):
    w = width or x.shape[1]
    return rowwise(lambda xt, gt: _rms(xt, gt), [(x, w, colblock)], [g], [out or (w, BF16)], [], name, into=into,
                   tie=tie)[0]


def rmsnorm_bwd(x, g, dh, name, resid=None, width=None, colblock=0, dh_colblock=0, dx_dtype=F32):
    w = width or x.shape[1]

    def fn(xt, dht, *rest):
        gt = rest[-1]
        _, vjp = jax.vjp(_rms, xt.astype(F32), gt)
        dx, dg = vjp(dht.astype(F32))
        if resid is not None:
            dx = dx + rest[0]
        return dx, dg

    rows = [(x, w, colblock), (dh, w, dh_colblock)] + ([(resid, w, 0)] if resid is not None else [])
    return rowwise(fn, rows, [g], [(w, dx_dtype)], [((1, w), F32)], name)


CONV_R = 64
HALO = 8


def _ext_rows(ref, i, nchunk, above, below):
    r0 = pl.multiple_of(i * CONV_R, CONV_R)
    s = ref.shape[0]
    parts = []
    if above:
        top = ref[pl.ds(pl.multiple_of(jnp.maximum(r0 - HALO, 0), HALO), HALO), :].astype(F32)
        parts.append(jnp.where(i > 0, top, 0.0))
    parts.append(ref[pl.ds(r0, CONV_R), :].astype(F32))
    if below:
        tile = 2 * HALO if ref.dtype == BF16 else HALO
        bot = ref[pl.ds(pl.multiple_of(jnp.minimum(r0 + CONV_R, s - tile), tile), tile), :].astype(F32)[0:HALO]
        parts.append(jnp.where(i < nchunk - 1, bot, 0.0))
    return jnp.concatenate(parts, axis=0)


def _conv_ext(ext, w_ref, b_ref, kw):
    y = ext[HALO:] * w_ref[kw - 1:kw, :] + b_ref[...]
    for k in range(1, kw):
        y = y + pltpu.roll(ext, k, 0)[HALO:] * w_ref[kw - 1 - k:kw - k, :]
    return y


def _conv_t_ext(d, w_ref, kw):
    n = d.shape[0]
    y = d[:n - HALO] * w_ref[kw - 1:kw, :]
    for k in range(1, kw):
        y = y + pltpu.roll(d, n - k, 0)[:n - HALO] * w_ref[kw - 1 - k:kw - k, :]
    return y


def _conv_wgrad(dp, ext, kw):
    out = [jnp.sum(dp, axis=0, keepdims=True), jnp.sum(dp * ext[HALO:HALO + CONV_R], axis=0, keepdims=True)]
    for k in range(1, kw):
        out.append(jnp.sum(dp * pltpu.roll(ext, k, 0)[HALO:HALO + CONV_R], axis=0, keepdims=True))
    return out


def _store_wgrad(res, dw_ref, db_ref, kw):
    db_ref[...] = res[0]
    for k in range(kw):
        dw_ref[kw - 1 - k:kw - k, :] = res[1 + k]


def _silu(x):
    return x * jax.nn.sigmoid(x)


def _dsilu(x):
    s = jax.nn.sigmoid(x)
    return s * (1.0 + x * (1.0 - s))


SSM_TC = 256


def ssm_conv_fwd(proj, cw, cb, l):
    s = proj.shape[0]
    off = OFF_XBC // SSM_TC

    def body(u_ref, w_ref, b_ref, o_ref):
        nchunk = s // CONV_R

        def step(i, carry):
            ext = _ext_rows(u_ref, i, nchunk, True, False)
            o_ref[pl.ds(pl.multiple_of(i * CONV_R, CONV_R), CONV_R), :] = _silu(_conv_ext(ext, w_ref, b_ref, SSM_CONV))
            return carry

        lax.fori_loop(0, nchunk, step, 0)

    return pl.pallas_call(
        body, name="ssm_conv_fwd", grid=(CONV_CH // SSM_TC,),
        in_specs=[pl.BlockSpec((s, SSM_TC), lambda j: (0, off + j)),
                  pl.BlockSpec((None, SSM_CONV, SSM_TC), lambda j: (l, 0, j)),
                  pl.BlockSpec((None, 1, SSM_TC), lambda j: (l, 0, j))],
        out_specs=pl.BlockSpec((s, SSM_TC), lambda j: (0, j)),
        out_shape=jax.ShapeDtypeStruct((s, CONV_CH), F32),
        compiler_params=_params(dimension_semantics=("arbitrary",)),
    )(proj, cw, cb)


def ssm_conv_bwd(proj, cw, cb, l, dact):
    s = proj.shape[0]
    off = OFF_XBC // SSM_TC

    def body(u_ref, w_ref, b_ref, d_ref, du_ref, dw_ref, db_ref):
        nchunk = s // CONV_R

        def step(i, carry):
            ext = _ext_rows(u_ref, i, nchunk, True, True)
            dpre = _ext_rows(d_ref, i, nchunk, False, True) * _dsilu(_conv_ext(ext, w_ref, b_ref, SSM_CONV))
            du_ref[pl.ds(pl.multiple_of(i * CONV_R, CONV_R), CONV_R), :] = _conv_t_ext(dpre, w_ref, SSM_CONV).astype(du_ref.dtype)
            return tuple(c + g for c, g in zip(carry, _conv_wgrad(dpre[:CONV_R], ext, SSM_CONV)))

        zero = jnp.zeros((1, SSM_TC), F32)
        _store_wgrad(lax.fori_loop(0, nchunk, step, (zero,) * (SSM_CONV + 1)), dw_ref, db_ref, SSM_CONV)

    return pl.pallas_call(
        body, name="ssm_conv_bwd", grid=(CONV_CH // SSM_TC,),
        in_specs=[pl.BlockSpec((s, SSM_TC), lambda j: (0, off + j)),
                  pl.BlockSpec((None, SSM_CONV, SSM_TC), lambda j: (l, 0, j)),
                  pl.BlockSpec((None, 1, SSM_TC), lambda j: (l, 0, j)), pl.BlockSpec((s, SSM_TC), lambda j: (0, j))],
        out_specs=[pl.BlockSpec((s, SSM_TC), lambda j: (0, j)), pl.BlockSpec((SSM_CONV, SSM_TC), lambda j: (0, j)),
                   pl.BlockSpec((1, SSM_TC), lambda j: (0, j))],
        out_shape=[jax.ShapeDtypeStruct((s, CONV_CH), BF16), jax.ShapeDtypeStruct((SSM_CONV, CONV_CH), F32),
                   jax.ShapeDtypeStruct((1, CONV_CH), F32)],
        compiler_params=_params(dimension_semantics=("arbitrary",)),
    )(proj, cw, cb, dact)


FFN_TC = 256
FFN_NT = D_FF // FFN_TC


def _ffn_specs(s, l):
    blk = pl.BlockSpec((s, FFN_TC), lambda j: (0, j))
    wg = pl.BlockSpec((None, FFN_CONV, FFN_TC), lambda j: (l, 0, j))
    wv = pl.BlockSpec((None, FFN_CONV, FFN_TC), lambda j: (l, 0, FFN_NT + j))
    bg = pl.BlockSpec((None, 1, FFN_TC), lambda j: (l, 0, j))
    bv = pl.BlockSpec((None, 1, FFN_TC), lambda j: (l, 0, FFN_NT + j))
    return blk, wg, wv, bg, bv


def ffn_act_fwd(ug, uv, cw, cb, l):
    s = ug.shape[0]

    def body(g_ref, v_ref, wg_ref, wv_ref, bg_ref, bv_ref, o_ref):
        nchunk = s // CONV_R

        def step(i, carry):
            cg = _conv_ext(_ext_rows(g_ref, i, nchunk, True, False), wg_ref, bg_ref, FFN_CONV)
            cv = _conv_ext(_ext_rows(v_ref, i, nchunk, True, False), wv_ref, bv_ref, FFN_CONV)
            o_ref[pl.ds(pl.multiple_of(i * CONV_R, CONV_R), CONV_R), :] = (_silu(cg) * cv).astype(o_ref.dtype)
            return carry

        lax.fori_loop(0, nchunk, step, 0)

    blk, wg, wv, bg, bv = _ffn_specs(s, l)
    return pl.pallas_call(
        body, name="ffn_act_fwd", grid=(FFN_NT,), in_specs=[blk, blk, wg, wv, bg, bv],
        out_specs=blk, out_shape=jax.ShapeDtypeStruct((s, D_FF), BF16),
        compiler_params=_params(dimension_semantics=("arbitrary",)),
    )(ug, uv, cw, cw, cb, cb)


def ffn_act_bwd(ug, uv, cw, cb, l, da):
    s = ug.shape[0]

    def body(g_ref, v_ref, wg_ref, wv_ref, bg_ref, bv_ref, da_ref, dg_ref, dv_ref, dwg_ref, dwv_ref, dbg_ref, dbv_ref):
        nchunk = s // CONV_R

        def step(i, carry):
            rows = pl.ds(pl.multiple_of(i * CONV_R, CONV_R), CONV_R)
            eg = _ext_rows(g_ref, i, nchunk, True, True)
            ev = _ext_rows(v_ref, i, nchunk, True, True)
            cg = _conv_ext(eg, wg_ref, bg_ref, FFN_CONV)
            cv = _conv_ext(ev, wv_ref, bv_ref, FFN_CONV)
            da_t = _ext_rows(da_ref, i, nchunk, False, True)
            dcg = da_t * cv * _dsilu(cg)
            dcv = da_t * _silu(cg)
            dg_ref[rows, :] = _conv_t_ext(dcg, wg_ref, FFN_CONV).astype(dg_ref.dtype)
            dv_ref[rows, :] = _conv_t_ext(dcv, wv_ref, FFN_CONV).astype(dv_ref.dtype)
            grads = _conv_wgrad(dcg[:CONV_R], eg, FFN_CONV) + _conv_wgrad(dcv[:CONV_R], ev, FFN_CONV)
            return tuple(c + g for c, g in zip(carry, grads))

        zero = jnp.zeros((1, FFN_TC), F32)
        res = lax.fori_loop(0, nchunk, step, (zero,) * (2 * FFN_CONV + 2))
        _store_wgrad(res[:FFN_CONV + 1], dwg_ref, dbg_ref, FFN_CONV)
        _store_wgrad(res[FFN_CONV + 1:], dwv_ref, dbv_ref, FFN_CONV)

    blk, wg, wv, bg, bv = _ffn_specs(s, l)
    wblk = pl.BlockSpec((FFN_CONV, FFN_TC), lambda j: (0, j))
    bblk = pl.BlockSpec((1, FFN_TC), lambda j: (0, j))
    return pl.pallas_call(
        body, name="ffn_act_bwd", grid=(FFN_NT,), in_specs=[blk, blk, wg, wv, bg, bv, blk],
        out_specs=[blk, blk, wblk, wblk, bblk, bblk],
        out_shape=[jax.ShapeDtypeStruct((s, D_FF), BF16), jax.ShapeDtypeStruct((s, D_FF), BF16),
                   jax.ShapeDtypeStruct((FFN_CONV, D_FF), F32), jax.ShapeDtypeStruct((FFN_CONV, D_FF), F32),
                   jax.ShapeDtypeStruct((1, D_FF), F32), jax.ShapeDtypeStruct((1, D_FF), F32)],
        compiler_params=_params(dimension_semantics=("arbitrary",)),
    )(ug, uv, cw, cw, cb, cb, da)


def _dot(a, b, mode):
    return lax.dot_general(a.astype(BF16), b.astype(BF16), _DIMS[mode], preferred_element_type=F32)


@jax.custom_vjp
def mm_nn(a, b):
    return _dot(a, b, 'nn')


@jax.custom_vjp
def mm_nt(a, b):
    return _dot(a, b, 'nt')


@jax.custom_vjp
def mm_tn(a, b):
    return _dot(a, b, 'tn')


mm_nn.defvjp(lambda a, b: (_dot(a, b, 'nn'), (a, b)), lambda r, g: (_dot(g, r[1], 'nt'), _dot(r[0], g, 'tn')))
mm_nt.defvjp(lambda a, b: (_dot(a, b, 'nt'), (a, b)), lambda r, g: (_dot(g, r[1], 'nn'), _dot(g, r[0], 'tn')))
mm_tn.defvjp(lambda a, b: (_dot(a, b, 'tn'), (a, b)), lambda r, g: (_dot(r[1], g, 'nt'), _dot(r[0], g, 'nn')))


def _tri(n, lower):
    r = lax.broadcasted_iota(jnp.int32, (n, n), 0)
    c = lax.broadcasted_iota(jnp.int32, (n, n), 1)
    return jnp.where((r >= c) if lower else (r <= c), 1.0, 0.0).astype(F32)


def _tri_dot(a, lower):
    return jnp.dot(_tri(a.shape[0], lower), a, precision=lax.Precision.HIGHEST, preferred_element_type=F32)


@jax.custom_vjp
def _cumsum_rows(a):
    return _tri_dot(a, True)


_cumsum_rows.defvjp(lambda a: (_tri_dot(a, True), None), lambda _, g: (_tri_dot(g, False),))


def _softplus(x):
    return jnp.maximum(x, 0.0) + jnp.log(1.0 + jnp.exp(-jnp.abs(x)))


def _ssd_chunk(xs, bs, cs, small, dtb, alog, dsk, prev):
    ln = small.shape[0]
    lane = lax.broadcasted_iota(jnp.int32, (ln, LANES), 1)
    lane1 = lax.broadcasted_iota(jnp.int32, (1, LANES), 1)
    sub = lax.broadcasted_iota(jnp.int32, (LANES, ln), 0)
    rowi = lax.broadcasted_iota(jnp.int32, (ln, LANES), 0)
    tril = lax.broadcasted_iota(jnp.int32, (ln, ln), 0) >= lax.broadcasted_iota(jnp.int32, (ln, ln), 1)
    first = lane < SSM_HEAD_DIM
    first1 = lane1 < SSM_HEAD_DIM

    dt = _softplus(small + dtb)
    acs = _cumsum_rows(dt * (-jnp.exp(alog)))
    acs_t = acs.T
    last = jnp.sum(jnp.where(rowi == ln - 1, acs, 0.0), axis=0, keepdims=True)

    def col(a, h):
        return jnp.sum(jnp.where(lane == h, a, 0.0), axis=1, keepdims=True)

    def one(a, h):
        return jnp.sum(jnp.where(lane1 == h, a, 0.0), axis=1, keepdims=True)

    def rowv(at, h):
        return jnp.sum(jnp.where(sub == h, at, 0.0), axis=0, keepdims=True)

    cb = [mm_nt(cs[g], bs[g]) for g in range(SSM_GROUPS)]
    ys, news = [], []
    for j in range(SSM_HEADS // 2):
        g = j // 2
        h0, h1 = 2 * j, 2 * j + 1
        xd = xs[j] * jnp.where(first, col(dt, h0), col(dt, h1))
        yd, st, ea, cd = None, None, [], []
        for h, xdh in ((h0, jnp.where(first, xd, 0.0)), (h1, jnp.where(first, 0.0, xd))):
            ac = col(acs, h)
            la = one(last, h)
            lmat = jnp.exp(jnp.where(tril, ac - rowv(acs_t, h), -jnp.inf))
            yh = mm_nn(cb[g] * lmat, xdh)
            sh = mm_tn(bs[g] * jnp.exp(la - ac), xdh)
            yd = yh if yd is None else yd + yh
            st = sh if st is None else st + sh
            ea.append(jnp.exp(ac))
            cd.append(jnp.exp(la))
        yoff = mm_nn(cs[g], prev[j]) * jnp.where(first, ea[0], ea[1])
        ys.append(yd + yoff + xs[j] * jnp.where(first1, one(dsk, h0), one(dsk, h1)))
        news.append(prev[j] * jnp.where(first1, cd[0], cd[1]) + st)
    return ys, news


N_PAIR = SSM_HEADS // 2


def ssd_fwd(xbc, proj, ptile, l):
    s = xbc.shape[0]
    nch = s // SSM_CHUNK

    def body(xbc_ref, small_ref, p_ref, y_ref, prev_ref, state_ref):
        @pl.when(pl.program_id(0) == 0)
        def _():
            state_ref[...] = jnp.zeros_like(state_ref)

        xs = [xbc_ref[:, LANES * j:LANES * (j + 1)] for j in range(N_PAIR)]
        bs = [xbc_ref[:, D_SSM + LANES * g:D_SSM + LANES * (g + 1)] for g in range(SSM_GROUPS)]
        cs = [xbc_ref[:, D_SSM + 512 + LANES * g:D_SSM + 512 + LANES * (g + 1)] for g in range(SSM_GROUPS)]
        prev = [state_ref[j] for j in range(N_PAIR)]
        ys, news = _ssd_chunk(xs, bs, cs, small_ref[...], p_ref[0:1, :], p_ref[1:2, :], p_ref[2:3, :], prev)
        for j in range(N_PAIR):
            y_ref[:, LANES * j:LANES * (j + 1)] = ys[j]
            prev_ref[0, j] = prev[j]
            state_ref[j] = news[j]

    return pl.pallas_call(
        body, name="ssd_fwd", grid=(nch,),
        in_specs=[pl.BlockSpec((SSM_CHUNK, CONV_CH), lambda c: (c, 0)),
                  pl.BlockSpec((SSM_CHUNK, LANES), lambda c: (c, OFF_SMALL // LANES)),
                  pl.BlockSpec((None, 8, LANES), lambda c: (l, 0, 0))],
        out_specs=[pl.BlockSpec((SSM_CHUNK, D_SSM), lambda c: (c, 0)),
                   pl.BlockSpec((1, N_PAIR, SSM_STATE, LANES), lambda c: (c, 0, 0, 0))],
        out_shape=[jax.ShapeDtypeStruct((s, D_SSM), F32), jax.ShapeDtypeStruct((nch, N_PAIR, SSM_STATE, LANES), F32)],
        scratch_shapes=[pltpu.VMEM((N_PAIR, SSM_STATE, LANES), F32)],
        compiler_params=_params(dimension_semantics=("arbitrary",)),
    )(xbc, proj, ptile)


def ssd_bwd(xbc, proj, ptile, l, prevs, dy):
    s = xbc.shape[0]
    nch = s // SSM_CHUNK

    def body(xbc_ref, small_ref, p_ref, prev_ref, dy_ref, dxbc_ref, dsmall_ref, dp_ref, dstate_ref):
        @pl.when(pl.program_id(0) == 0)
        def _():
            dstate_ref[...] = jnp.zeros_like(dstate_ref)
            dp_ref[...] = jnp.zeros_like(dp_ref)

        xs = [xbc_ref[:, LANES * j:LANES * (j + 1)] for j in range(N_PAIR)]
        bs = [xbc_ref[:, D_SSM + LANES * g:D_SSM + LANES * (g + 1)] for g in range(SSM_GROUPS)]
        cs = [xbc_ref[:, D_SSM + 512 + LANES * g:D_SSM + 512 + LANES * (g + 1)] for g in range(SSM_GROUPS)]
        prev = [prev_ref[0, j] for j in range(N_PAIR)]
        dys = [dy_ref[:, LANES * j:LANES * (j + 1)] for j in range(N_PAIR)]
        dnew = [dstate_ref[j] for j in range(N_PAIR)]
        _, vjp = jax.vjp(_ssd_chunk, xs, bs, cs, small_ref[...], p_ref[0:1, :], p_ref[1:2, :], p_ref[2:3, :], prev)
        dxs, dbs, dcs, dsmall, ddtb, dalog, ddsk, dprev = vjp((dys, dnew))
        for j in range(N_PAIR):
            dxbc_ref[:, LANES * j:LANES * (j + 1)] = dxs[j]
            dstate_ref[j] = dprev[j]
        for g in range(SSM_GROUPS):
            dxbc_ref[:, D_SSM + LANES * g:D_SSM + LANES * (g + 1)] = dbs[g]
            dxbc_ref[:, D_SSM + 512 + LANES * g:D_SSM + 512 + LANES * (g + 1)] = dcs[g]
        dsmall_ref[...] = dsmall
        dp_ref[0:1, :] += ddtb
        dp_ref[1:2, :] += dalog
        dp_ref[2:3, :] += ddsk

    rev = lambda c: nch - 1 - c
    return pl.pallas_call(
        body, name="ssd_bwd", grid=(nch,),
        in_specs=[pl.BlockSpec((SSM_CHUNK, CONV_CH), lambda c: (rev(c), 0)),
                  pl.BlockSpec((SSM_CHUNK, LANES), lambda c: (rev(c), OFF_SMALL // LANES)),
                  pl.BlockSpec((None, 8, LANES), lambda c: (l, 0, 0)),
                  pl.BlockSpec((1, N_PAIR, SSM_STATE, LANES), lambda c: (rev(c), 0, 0, 0)),
                  pl.BlockSpec((SSM_CHUNK, D_SSM), lambda c: (rev(c), 0))],
        out_specs=[pl.BlockSpec((SSM_CHUNK, CONV_CH), lambda c: (rev(c), 0)),
                   pl.BlockSpec((SSM_CHUNK, LANES), lambda c: (rev(c), 0)),
                   pl.BlockSpec((8, LANES), lambda c: (0, 0))],
        out_shape=[jax.ShapeDtypeStruct((s, CONV_CH), F32), jax.ShapeDtypeStruct((s, LANES), F32),
                   jax.ShapeDtypeStruct((8, LANES), F32)],
        scratch_shapes=[pltpu.VMEM((N_PAIR, SSM_STATE, LANES), F32)],
        compiler_params=_params(dimension_semantics=("arbitrary",)),
    )(xbc, proj, ptile, prevs, dy)


ROPE_TM = 256


def _rope_tile(t, cosm, sinm):
    lane = lax.broadcasted_iota(jnp.int32, t.shape, 1)
    half = QK_ROPE // 2
    partner = jnp.where(lane < ROPE_LANE0 + half, pltpu.roll(t, LANES - half, 1), pltpu.roll(t, half, 1))
    return t * cosm + partner * sinm


def _in_rope(shape):
    lane = lax.broadcasted_iota(jnp.int32, shape, 1)
    return jnp.logical_and(lane >= ROPE_LANE0, lane < ROPE_LANE0 + QK_ROPE)


def rope_q(q, cosm, sinm, name):
    s, w = q.shape

    def body(q_ref, c_ref, s_ref, o_ref):
        c, sn = c_ref[...], s_ref[...]
        for h in range(MLA_HEADS):
            sl = slice(HEAD_PAD * h, HEAD_PAD * (h + 1))
            o_ref[:, sl] = _rope_tile(q_ref[:, sl].astype(F32), c, sn).astype(o_ref.dtype)

    row = pl.BlockSpec((ROPE_TM, w), lambda i: (i, 0))
    tab = pl.BlockSpec((ROPE_TM, LANES), lambda i: (i, 0))
    return pl.pallas_call(
        body, name=name, grid=(s // ROPE_TM,), in_specs=[row, tab, tab], out_specs=row,
        out_shape=jax.ShapeDtypeStruct((s, w), BF16), compiler_params=_params(dimension_semantics=("arbitrary",)),
    )(q, cosm, sinm)


def build_k(kn, proj, cosm, sinm):
    s, w = kn.shape

    def body(k_ref, small_ref, c_ref, s_ref, o_ref):
        small = small_ref[...]
        inrope = _in_rope(small.shape)
        kpe = jnp.where(inrope, _rope_tile(jnp.where(inrope, small, 0.0), c_ref[...], s_ref[...]), 0.0)
        for h in range(MLA_HEADS):
            sl = slice(HEAD_PAD * h, HEAD_PAD * (h + 1))
            o_ref[:, sl] = (k_ref[:, sl].astype(F32) + kpe).astype(o_ref.dtype)

    row = pl.BlockSpec((ROPE_TM, w), lambda i: (i, 0))
    tab = pl.BlockSpec((ROPE_TM, LANES), lambda i: (i, 0))
    return pl.pallas_call(
        body, name="build_k", grid=(s // ROPE_TM,),
        in_specs=[row, pl.BlockSpec((ROPE_TM, LANES), lambda i: (i, OFF_SMALL // LANES)), tab, tab], out_specs=row,
        out_shape=jax.ShapeDtypeStruct((s, w), BF16), compiler_params=_params(dimension_semantics=("arbitrary",)),
    )(kn, proj, cosm, sinm)


def dsmall_bwd(dk, dsmall_ssd, cosm, sinm_neg):
    def fn(dkt, ds, c, sn):
        inrope = _in_rope(ds.shape)
        tot = dkt[:, 0:HEAD_PAD]
        for h in range(1, MLA_HEADS):
            tot = tot + dkt[:, HEAD_PAD * h:HEAD_PAD * (h + 1)]
        tot = jnp.where(inrope, tot, 0.0)
        return ds + jnp.where(inrope, _rope_tile(tot, c, sn), 0.0)

    return rowwise(fn, [(dk, MLA_HEADS * HEAD_PAD, 0), (dsmall_ssd, LANES, 0), (cosm, LANES, 0), (sinm_neg, LANES, 0)],
                   [], [(LANES, BF16)], [], "dsmall_bwd")[0]


ATT_TQ = 256
ATT_SCALE = (QK_NOPE + QK_ROPE) ** -0.5


def _att_scores(qh, kh, q0):
    s = lax.dot_general(qh, kh, _DIMS['nt'], preferred_element_type=F32) * ATT_SCALE
    r = lax.broadcasted_iota(jnp.int32, s.shape, 0) + q0
    c = lax.broadcasted_iota(jnp.int32, s.shape, 1)
    return jnp.where(c <= r, s, -1e30)


def mla_fwd(q, k, v):
    s = q.shape[0]

    def body(q_ref, k_ref, v_ref, o_ref, lse_ref):
        lane = lax.broadcasted_iota(jnp.int32, (ATT_TQ, LANES), 1)

        def block(ib):
            n = ATT_TQ * (ib + 1)
            v_t = v_ref[0:n, :]
            vlane = lax.broadcasted_iota(jnp.int32, v_t.shape, 1)
            o_tot, lse_tot = None, None
            for h in range(2):
                hs = slice(HEAD_PAD * h, HEAD_PAD * (h + 1))
                sc = _att_scores(q_ref[:, hs], k_ref[0:n, hs], ATT_TQ * ib)
                m = jnp.max(sc, axis=1, keepdims=True)
                p = jnp.exp(sc - m)
                l = jnp.sum(p, axis=1, keepdims=True)
                vh = jnp.where((vlane < V_DIM) if h == 0 else (vlane >= V_DIM), v_t, jnp.zeros_like(v_t))
                oh = lax.dot_general(p.astype(BF16), vh, _DIMS['nn'], preferred_element_type=F32) / l
                lse_h = jnp.where((lane < V_DIM) if h == 0 else (lane >= V_DIM), m + jnp.log(l), 0.0)
                o_tot = oh if o_tot is None else o_tot + oh
                lse_tot = lse_h if lse_tot is None else lse_tot + lse_h
            o_ref[...] = o_tot
            lse_ref[...] = lse_tot

        for ib in range(s // ATT_TQ):
            pl.when(pl.program_id(1) == ib)(functools.partial(block, ib))

    tile = pl.BlockSpec((ATT_TQ, LANES), lambda p, i: (i, p))
    return pl.pallas_call(
        body, name="mla_fwd", grid=(MLA_HEADS // 2, s // ATT_TQ),
        in_specs=[pl.BlockSpec((ATT_TQ, 2 * HEAD_PAD), lambda p, i: (i, p)),
                  pl.BlockSpec((s, 2 * HEAD_PAD), lambda p, i: (0, p)),
                  pl.BlockSpec((s, LANES), lambda p, i: (0, p))],
        out_specs=[tile, tile],
        out_shape=[jax.ShapeDtypeStruct((s, MLA_HEADS * V_DIM), F32)] * 2,
        compiler_params=_params(dimension_semantics=("arbitrary", "arbitrary")),
    )(q, k, v)


def mla_bwd(q, k, v, o, lse, do):
    s = q.shape[0]

    def body(q_ref, k_ref, v_ref, o_ref, lse_ref, do_ref, dq_ref, dk_ref, dv_ref):
        i = pl.program_id(1)

        @pl.when(i == 0)
        def _():
            dk_ref[...] = jnp.zeros_like(dk_ref)
            dv_ref[...] = jnp.zeros_like(dv_ref)

        def block(ib):
            n = ATT_TQ * (ib + 1)
            o_t = o_ref[...]
            do_t = do_ref[...]
            lse_t = lse_ref[...]
            v_t = v_ref[0:n, :]
            lane = lax.broadcasted_iota(jnp.int32, do_t.shape, 1)
            for h in range(2):
                hs = slice(HEAD_PAD * h, HEAD_PAD * (h + 1))
                sel = (lane < V_DIM) if h == 0 else (lane >= V_DIM)
                qh = q_ref[:, hs]
                kh = k_ref[0:n, hs]
                doh = jnp.where(sel, do_t, 0.0)
                delta = jnp.sum(doh * o_t, axis=1, keepdims=True)
                lse_h = jnp.max(jnp.where(sel, lse_t, -jnp.inf), axis=1, keepdims=True)
                doh_b = doh.astype(BF16)
                p = jnp.exp(_att_scores(qh, kh, ATT_TQ * ib) - lse_h)
                dv_ref[0:n, :] += lax.dot_general(p.astype(BF16), doh_b, _DIMS['tn'], preferred_element_type=F32)
                dp = lax.dot_general(doh_b, v_t, _DIMS['nt'], preferred_element_type=F32)
                ds = (p * (dp - delta) * ATT_SCALE).astype(BF16)
                dk_ref[0:n, hs] += lax.dot_general(ds, qh, _DIMS['tn'], preferred_element_type=F32)
                dq_ref[:, hs] = lax.dot_general(ds, kh, _DIMS['nn'], preferred_element_type=F32).astype(dq_ref.dtype)

        for ib in range(s // ATT_TQ):
            pl.when(i == ib)(functools.partial(block, ib))

    tile = pl.BlockSpec((ATT_TQ, LANES), lambda p, i: (i, p))
    return pl.pallas_call(
        body, name="mla_bwd", grid=(MLA_HEADS // 2, s // ATT_TQ),
        in_specs=[pl.BlockSpec((ATT_TQ, 2 * HEAD_PAD), lambda p, i: (i, p)),
                  pl.BlockSpec((s, 2 * HEAD_PAD), lambda p, i: (0, p)),
                  pl.BlockSpec((s, LANES), lambda p, i: (0, p)), tile, tile, tile],
        out_specs=[pl.BlockSpec((ATT_TQ, 2 * HEAD_PAD), lambda p, i: (i, p)),
                   pl.BlockSpec((s, 2 * HEAD_PAD), lambda p, i: (0, p)),
                   pl.BlockSpec((s, LANES), lambda p, i: (0, p))],
        out_shape=[jax.ShapeDtypeStruct((s, MLA_HEADS * HEAD_PAD), F32),
                   jax.ShapeDtypeStruct((s, MLA_HEADS * HEAD_PAD), F32),
                   jax.ShapeDtypeStruct((s, MLA_HEADS * V_DIM), F32)],
        compiler_params=_params(dimension_semantics=("arbitrary", "arbitrary")),
    )(q, k, v, o, lse, do)


MEM_TQ = 256
MEM_SCALE = MEM_HEAD_DIM ** -0.5


def _mem_probs(qh, kh):
    s = lax.dot_general(qh, kh, _DIMS['nt'], preferred_element_type=F32) * MEM_SCALE
    p = jnp.exp(s - jnp.max(s, axis=1, keepdims=True))
    return p / jnp.sum(p, axis=1, keepdims=True)


def mem_fwd(q, k, v):
    s = q.shape[0]

    def body(q_ref, k_ref, v_ref, o_ref):
        for h in range(MEM_HEADS):
            sl = slice(MEM_HEAD_DIM * h, MEM_HEAD_DIM * (h + 1))
            p = _mem_probs(q_ref[:, sl], k_ref[:, sl])
            o_ref[:, sl] = lax.dot_general(p.astype(BF16), v_ref[:, sl], _DIMS['nn'],
                                           preferred_element_type=F32).astype(o_ref.dtype)

    full = pl.BlockSpec((MEM_LEN, D_MODEL), lambda i: (0, 0))
    return pl.pallas_call(
        body, name="mem_fwd", grid=(s // MEM_TQ,),
        in_specs=[pl.BlockSpec((MEM_TQ, D_MODEL), lambda i: (i, 0)), full, full],
        out_specs=pl.BlockSpec((MEM_TQ, D_MODEL), lambda i: (i, 0)),
        out_shape=jax.ShapeDtypeStruct((s, D_MODEL), BF16),
        compiler_params=_params(dimension_semantics=("arbitrary",)),
    )(q, k, v)


def mem_bwd(q, k, v, do):
    s = q.shape[0]

    def body(q_ref, k_ref, v_ref, do_ref, dq_ref, dk_ref, dv_ref):
        @pl.when(pl.program_id(0) == 0)
        def _():
            dk_ref[...] = jnp.zeros_like(dk_ref)
            dv_ref[...] = jnp.zeros_like(dv_ref)

        for h in range(MEM_HEADS):
            sl = slice(MEM_HEAD_DIM * h, MEM_HEAD_DIM * (h + 1))
            qh, kh, vh = q_ref[:, sl], k_ref[:, sl], v_ref[:, sl]
            doh = do_ref[:, sl].astype(BF16)
            p = _mem_probs(qh, kh)
            dv_ref[:, sl] += lax.dot_general(p.astype(BF16), doh, _DIMS['tn'], preferred_element_type=F32)
            dp = lax.dot_general(doh, vh, _DIMS['nt'], preferred_element_type=F32)
            ds = (p * (dp - jnp.sum(p * dp, axis=1, keepdims=True)) * MEM_SCALE).astype(BF16)
            dq_ref[:, sl] = lax.dot_general(ds, kh, _DIMS['nn'], preferred_element_type=F32).astype(dq_ref.dtype)
            dk_ref[:, sl] += lax.dot_general(ds, qh, _DIMS['tn'], preferred_element_type=F32)

    full = pl.BlockSpec((MEM_LEN, D_MODEL), lambda i: (0, 0))
    row = pl.BlockSpec((MEM_TQ, D_MODEL), lambda i: (i, 0))
    return pl.pallas_call(
        body, name="mem_bwd", grid=(s // MEM_TQ,),
        in_specs=[row, full, full, row], out_specs=[row, full, full],
        out_shape=[jax.ShapeDtypeStruct((s, D_MODEL), BF16), jax.ShapeDtypeStruct((MEM_LEN, D_MODEL), F32),
                   jax.ShapeDtypeStruct((MEM_LEN, D_MODEL), F32)],
        compiler_params=_params(dimension_semantics=("arbitrary",)),
    )(q, k, v, do)


def _gate_norm(y, z, g):
    return _rms(y * _silu(z), g)


def gate_norm_fwd(y, proj, g):
    return rowwise(_gate_norm, [(y, D_SSM, 0), (proj, D_SSM, OFF_Z // D_SSM)], [g], [(D_SSM, BF16, D_MIX, 0)], [],
                   "gate_norm_fwd")[0]


def gate_norm_bwd(y, proj, g, dmix, tie=None):
    def fn(yt, zt, dt_, gt):
        _, vjp = jax.vjp(_gate_norm, yt, zt, gt)
        return vjp(dt_.astype(F32))

    return rowwise(fn, [(y, D_SSM, 0), (proj, D_SSM, OFF_Z // D_SSM), (dmix, D_SSM, 0)], [g],
                   [(D_SSM, F32), (D_SSM, BF16)], [((1, D_SSM), F32)], "gate_norm_bwd", tie=tie)


def loss_head(x, g, target):
    def fn(xt, tt, gt):
        def f(x_, g_):
            err = _rms(x_, g_) - tt
            return 0.5 * jnp.sum(jnp.mean(err * err, axis=-1))

        lv, (dx, dg) = jax.value_and_grad(f, argnums=(0, 1))(xt, gt)
        return dx, dg, jnp.full((1, LANES), lv, F32)

    return rowwise(fn, [(x, D_MODEL, 0), (target, D_MODEL, 0)], [g], [(D_MODEL, F32)],
                   [((1, D_MODEL), F32), ((1, LANES), F32)], "loss_head")


def _proj_runs(d):
    lo, hi = (D_IN // N_DEV) * d, (D_IN // N_DEV) * (d + 1)
    runs = []
    for a, b, new in PROJ_SEGS:
        s0, s1 = max(a, lo), min(b, hi)
        if s0 < s1:
            runs.append((s0 - lo, new + s0 - a, s1 - s0))
    return runs


LAYOUT_TM = 256


def assemble_proj(g):
    def body(g_ref, o_ref):
        o_ref[:, OFF_SMALL:OFF_SMALL + LANES] = jnp.zeros((LAYOUT_TM, LANES), o_ref.dtype)
        for d in range(N_DEV):
            for src, dst, n in _proj_runs(d):
                o_ref[:, dst:dst + n] = g_ref[d, :, src:src + n]

    return pl.pallas_call(
        body, name="assemble_proj", grid=(D_MODEL // LAYOUT_TM,),
        in_specs=[pl.BlockSpec((N_DEV, LAYOUT_TM, D_IN // N_DEV), lambda i: (0, i, 0))],
        out_specs=pl.BlockSpec((LAYOUT_TM, PROJ_W), lambda i: (i, 0)),
        out_shape=jax.ShapeDtypeStruct((D_MODEL, PROJ_W), g.dtype),
        compiler_params=_params(dimension_semantics=("arbitrary",)),
    )(g)


def extract_proj(dz, dxbc, dcq, dsmall, dckv):
    pieces = [(OFF_Z, 1024), (OFF_XBC, 2048), (OFF_CQ, Q_LORA), (OFF_SMALL, LANES), (OFF_CKV, KV_LORA)]

    def body(*refs):
        o_ref = refs[-1]
        for d in range(N_DEV):
            for src, dst, n in _proj_runs(d):
                for p, (off, w) in enumerate(pieces):
                    if off <= dst < off + w:
                        o_ref[d, :, src:src + n] = refs[p][:, dst - off:dst - off + n].astype(o_ref.dtype)

    return pl.pallas_call(
        body, name="extract_proj", grid=(D_MODEL // LAYOUT_TM,),
        in_specs=[pl.BlockSpec((LAYOUT_TM, w), lambda i: (i, 0)) for _, w in pieces],
        out_specs=pl.BlockSpec((N_DEV, LAYOUT_TM, D_IN // N_DEV), lambda i: (0, i, 0)),
        out_shape=jax.ShapeDtypeStruct((N_DEV, D_MODEL, D_IN // N_DEV), BF16),
        compiler_params=_params(dimension_semantics=("arbitrary",)),
    )(dz, dxbc, dcq, dsmall, dckv)


_QW = QK_NOPE + QK_ROPE


def assemble_uq(g):
    def body(g_ref, o_ref):
        o_ref[...] = jnp.zeros_like(o_ref)
        for d in range(N_DEV):
            for e in range(2):
                dst = HEAD_PAD * (2 * d + e)
                o_ref[:, dst:dst + _QW] = g_ref[d, :, _QW * e:_QW * (e + 1)]

    return pl.pallas_call(
        body, name="assemble_uq", grid=(1,),
        in_specs=[pl.BlockSpec((N_DEV, Q_LORA, 2 * _QW), lambda i: (0, 0, 0))],
        out_specs=pl.BlockSpec((Q_LORA, MLA_HEADS * HEAD_PAD), lambda i: (0, 0)),
        out_shape=jax.ShapeDtypeStruct((Q_LORA, MLA_HEADS * HEAD_PAD), g.dtype),
        compiler_params=_params(dimension_semantics=("arbitrary",)),
    )(g)


def extract_uq(dw):
    def body(w_ref, o_ref):
        for d in range(N_DEV):
            for e in range(2):
                src = HEAD_PAD * (2 * d + e)
                o_ref[d, :, _QW * e:_QW * (e + 1)] = w_ref[:, src:src + _QW].astype(o_ref.dtype)

    return pl.pallas_call(
        body, name="extract_uq", grid=(1,),
        in_specs=[pl.BlockSpec((Q_LORA, MLA_HEADS * HEAD_PAD), lambda i: (0, 0))],
        out_specs=pl.BlockSpec((N_DEV, Q_LORA, 2 * _QW), lambda i: (0, 0, 0)),
        out_shape=jax.ShapeDtypeStruct((N_DEV, Q_LORA, 2 * _QW), BF16),
        compiler_params=_params(dimension_semantics=("arbitrary",)),
    )(dw)


def assemble_ukv(g):
    def body(g_ref, kn_ref, v_ref):
        kn_ref[...] = jnp.zeros_like(kn_ref)
        for d in range(N_DEV):
            for e in range(2):
                h = 2 * d + e
                kn_ref[:, HEAD_PAD * h:HEAD_PAD * h + QK_NOPE] = g_ref[d, :, 128 * e:128 * e + QK_NOPE]
                v_ref[:, V_DIM * h:V_DIM * (h + 1)] = g_ref[d, :, 128 * e + QK_NOPE:128 * (e + 1)]

    return pl.pallas_call(
        body, name="assemble_ukv", grid=(1,),
        in_specs=[pl.BlockSpec((N_DEV, KV_LORA, 256), lambda i: (0, 0, 0))],
        out_specs=[pl.BlockSpec((KV_LORA, MLA_HEADS * HEAD_PAD), lambda i: (0, 0)),
                   pl.BlockSpec((KV_LORA, MLA_HEADS * V_DIM), lambda i: (0, 0))],
        out_shape=[jax.ShapeDtypeStruct((KV_LORA, MLA_HEADS * HEAD_PAD), g.dtype),
                   jax.ShapeDtypeStruct((KV_LORA, MLA_HEADS * V_DIM), g.dtype)],
        compiler_params=_params(dimension_semantics=("arbitrary",)),
    )(g)


def extract_ukv(dkn, dv):
    def body(kn_ref, v_ref, o_ref):
        for d in range(N_DEV):
            for e in range(2):
                h = 2 * d + e
                o_ref[d, :, 128 * e:128 * e + QK_NOPE] = kn_ref[:, HEAD_PAD * h:HEAD_PAD * h + QK_NOPE].astype(o_ref.dtype)
                o_ref[d, :, 128 * e + QK_NOPE:128 * (e + 1)] = v_ref[:, V_DIM * h:V_DIM * (h + 1)].astype(o_ref.dtype)

    return pl.pallas_call(
        body, name="extract_ukv", grid=(1,),
        in_specs=[pl.BlockSpec((KV_LORA, MLA_HEADS * HEAD_PAD), lambda i: (0, 0)),
                  pl.BlockSpec((KV_LORA, MLA_HEADS * V_DIM), lambda i: (0, 0))],
        out_specs=pl.BlockSpec((N_DEV, KV_LORA, 256), lambda i: (0, 0, 0)),
        out_shape=jax.ShapeDtypeStruct((N_DEV, KV_LORA, 256), BF16),
        compiler_params=_params(dimension_semantics=("arbitrary",)),
    )(dkn, dv)


_UPW = 2 * D_FF // N_DEV


def assemble_up(g):
    def body(g_ref, wg_ref, wv_ref):
        for d in range(N_DEV):
            ref = wg_ref if d < N_DEV // 2 else wv_ref
            off = _UPW * (d % (N_DEV // 2))
            ref[:, off:off + _UPW] = g_ref[d]

    half = pl.BlockSpec((LAYOUT_TM, D_FF), lambda i: (i, 0))
    return pl.pallas_call(
        body, name="assemble_up", grid=(D_MODEL // LAYOUT_TM,),
        in_specs=[pl.BlockSpec((N_DEV, LAYOUT_TM, _UPW), lambda i: (0, i, 0))],
        out_specs=[half, half], out_shape=[jax.ShapeDtypeStruct((D_MODEL, D_FF), g.dtype)] * 2,
        compiler_params=_params(dimension_semantics=("arbitrary",)),
    )(g)


def extract_up(dwg, dwv):
    def body(wg_ref, wv_ref, o_ref):
        for d in range(N_DEV):
            ref = wg_ref if d < N_DEV // 2 else wv_ref
            off = _UPW * (d % (N_DEV // 2))
            o_ref[d] = ref[:, off:off + _UPW].astype(o_ref.dtype)

    half = pl.BlockSpec((LAYOUT_TM, D_FF), lambda i: (i, 0))
    return pl.pallas_call(
        body, name="extract_up", grid=(D_MODEL // LAYOUT_TM,), in_specs=[half, half],
        out_specs=pl.BlockSpec((N_DEV, LAYOUT_TM, _UPW), lambda i: (0, i, 0)),
        out_shape=jax.ShapeDtypeStruct((N_DEV, D_MODEL, _UPW), BF16),
        compiler_params=_params(dimension_semantics=("arbitrary",)),
    )(dwg, dwv)


MESH = pl.DeviceIdType.MESH
ANY = pl.BlockSpec(memory_space=pl.ANY)


def _place():
    mx, my, mc = lax.axis_index("x"), lax.axis_index("y"), lax.axis_index("c")
    return mx, my, mc, [(1 - mx, my), (mx, 1 - my), (1 - mx, 1 - my)]


def all_gather_blocks(xs, first_only=()):
    n = len(xs)

    def body(*refs):
        x_refs, out_refs = refs[:n], refs[n:2 * n]
        send_sems, recv_sems, local_sems = refs[2 * n:]
        mx, my, mc, chips = _place()
        me, sibling = (mx, my, mc), (mx, my, 1 - mc)
        x_refs = [x_refs[t].at[0] if t in first_only else x_refs[t] for t in range(n)]

        def rows(t, px, py, pc):
            dev = 4 * px + 2 * py + pc
            return out_refs[t].at[dev] if t in first_only else out_refs[t].at[:, dev]

        def copy(t, k, block, to, src=None):
            return pltpu.make_async_remote_copy(
                src_ref=rows(t, *block) if src is None else src, dst_ref=rows(t, *block),
                send_sem=send_sems.at[t, k], recv_sem=recv_sems.at[t, k], device_id=to, device_id_type=MESH)

        mine = [pltpu.make_async_copy(x_refs[t], rows(t, *me), local_sems.at[t]) for t in range(n)]
        for cp in mine:
            cp.start()
        first = []
        for t in range(n):
            first.append(copy(t, 0, me, sibling, src=x_refs[t]))
            first += [copy(t, 1 + j, me, (*chip, mc), src=x_refs[t]) for j, chip in enumerate(chips)]
        for cp in first:
            cp.start()
        passed = []
        for j, chip in enumerate(chips):
            for t in range(n):
                copy(t, 1 + j, (*chip, mc), me).wait_recv()
                cp = copy(t, 4 + j, (*chip, mc), sibling)
                cp.start()
                passed.append(cp)
        for t in range(n):
            copy(t, 0, sibling, me).wait_recv()
            for j, chip in enumerate(chips):
                copy(t, 4 + j, (*chip, 1 - mc), me).wait_recv()
        for cp in first + passed:
            cp.wait_send()
        for cp in mine:
            cp.wait()

    return pl.pallas_call(
        body, name="all_gather_blocks",
        out_shape=[jax.ShapeDtypeStruct(((N_DEV,) if t in first_only else (x.shape[0], N_DEV)) + x.shape[1:], x.dtype)
                   for t, x in enumerate(xs)],
        in_specs=[ANY] * n, out_specs=[ANY] * n,
        scratch_shapes=[pltpu.SemaphoreType.DMA((n, 7)), pltpu.SemaphoreType.DMA((n, 7)), pltpu.SemaphoreType.DMA((n,))],
    )(*xs)


HBM = pl.BlockSpec(memory_space=pltpu.HBM)
SEM = pl.BlockSpec(memory_space=pltpu.SEMAPHORE)
EFFECT = pltpu.SideEffectType.DATAFLOW_SIDE_EFFECTING
ALL_DEVICES = [(px, py, pc) for px in range(2) for py in range(2) for pc in range(2)]


def _hbm(x):
    return pltpu.with_memory_space_constraint(x, pltpu.HBM)


def _split_start(body, name, srcs, lands):
    n = len(srcs)

    def full_body(*refs):
        body(refs[:n], refs[n:2 * n], refs[2 * n], refs[2 * n + 1])
        refs[-1][...] = jnp.zeros_like(refs[-1])

    res = pl.pallas_call(
        full_body, name=name,
        out_shape=(pltpu.SemaphoreType.DMA((n,)), pltpu.SemaphoreType.DMA((n,)),
                   *[pltpu.HBM(x.shape, x.dtype) for x in srcs], *[pltpu.HBM(x.shape, x.dtype) for x in lands],
                   jax.ShapeDtypeStruct((8, LANES), F32)),
        in_specs=[HBM] * (2 * n), out_specs=(SEM, SEM, *[HBM] * (2 * n), pl.BlockSpec(memory_space=pltpu.VMEM)),
        input_output_aliases={i: 2 + i for i in range(2 * n)},
        compiler_params=pltpu.CompilerParams(has_side_effects=EFFECT),
    )(*[_hbm(x) for x in srcs], *[_hbm(x) for x in lands])
    return res[0], res[1], list(res[2:2 + n]), list(res[2 + n:2 + 2 * n]), res[-1]


def _split_wait(name, send_sems, recv_sems, srcs, lands, after, sent, landed):
    n = len(srcs)

    def body(*refs):
        src_refs, land_refs, ssem, rsem = refs[:n], refs[n:2 * n], refs[2 * n], refs[2 * n + 1]
        mx, my, mc, _ = _place()
        for t in range(n):
            out = sent(src_refs[t], land_refs[t])
            inn = landed(land_refs[t])
            pltpu.make_async_remote_copy(src_ref=out, dst_ref=out, send_sem=ssem.at[t], recv_sem=rsem.at[t],
                                         device_id=(mx, my, mc), device_id_type=MESH).wait_send()
            pltpu.make_async_remote_copy(src_ref=inn, dst_ref=inn, send_sem=ssem.at[t], recv_sem=rsem.at[t],
                                         device_id=(mx, my, mc), device_id_type=MESH).wait_recv()

    res = pl.pallas_call(
        body, name=name,
        out_shape=(*[pltpu.HBM(x.shape, x.dtype) for x in srcs], *[pltpu.HBM(x.shape, x.dtype) for x in lands]),
        in_specs=[HBM] * (2 * n) + [SEM, SEM, ANY], out_specs=[HBM] * (2 * n),
        input_output_aliases={i: i for i in range(2 * n)},
        compiler_params=pltpu.CompilerParams(has_side_effects=EFFECT),
    )(*srcs, *lands, send_sems, recv_sems, after)
    return list(res[:n]), list(res[n:])


def gather_start(srcs, l, tag):
    lands = [lax.empty((N_DEV,) + x.shape[1:], x.dtype) for x in srcs]

    def body(src_refs, land_refs, send_sems, recv_sems):
        mx, my, mc, _ = _place()
        me = 4 * mx + 2 * my + mc
        for t in range(len(srcs)):
            for to in ALL_DEVICES:
                pltpu.make_async_remote_copy(
                    src_ref=src_refs[t].at[l], dst_ref=land_refs[t].at[me], send_sem=send_sems.at[t],
                    recv_sem=recv_sems.at[t], device_id=to, device_id_type=MESH).start()

    return _split_start(body, "gather_start_%d%s" % (l, tag), srcs, lands)


def gather_wait(l, tag, send_sems, recv_sems, srcs, lands, after):
    return _split_wait("gather_wait_%d%s" % (l, tag), send_sems, recv_sems, srcs, lands, after,
                       sent=lambda s, d: d, landed=lambda d: d)


def grad_exchange_start(es, lands, l, tag):
    def body(e_refs, land_refs, send_sems, recv_sems):
        mx, my, mc, _ = _place()
        me = 4 * mx + 2 * my + mc
        for t in range(len(es)):
            for px, py, pc in ALL_DEVICES:
                pltpu.make_async_remote_copy(
                    src_ref=e_refs[t].at[4 * px + 2 * py + pc], dst_ref=land_refs[t].at[l, me], send_sem=send_sems.at[t],
                    recv_sem=recv_sems.at[t], device_id=(px, py, pc), device_id_type=MESH).start()

    return _split_start(body, "grad_exchange_start_%d%s" % (l, tag), es, lands)


def grad_exchange_wait(l, tag, send_sems, recv_sems, es, lands, after):
    return _split_wait("grad_exchange_wait_%d%s" % (l, tag), send_sems, recv_sems, es, lands, after,
                       sent=lambda s, d: s, landed=lambda d: d.at[l])


def _adam(g, w, m, v):
    nm = ADAM_B1 * m + (1.0 - ADAM_B1) * g
    nv = ADAM_B2 * v + (1.0 - ADAM_B2) * jnp.square(g)
    m_hat = nm / (1.0 - ADAM_B1 ** ADAM_STEP)
    v_hat = nv / (1.0 - ADAM_B2 ** ADAM_STEP)
    return -ADAM_LR * (m_hat / (jnp.sqrt(v_hat) + ADAM_EPS) + ADAM_WD * w), nm, nv


def adamw_big(parts, w, m, v, name):
    depth, _, a, b = parts.shape
    ta = _row_tile(a)

    def body(p_ref, w_ref, m_ref, v_ref, g_ref, d_ref, nm_ref, nv_ref):
        g = p_ref[0].astype(F32)
        for k in range(1, N_DEV):
            g = g + p_ref[k].astype(F32)
        g_ref[...] = g
        d_ref[...], nm_ref[...], nv_ref[...] = _adam(g, w_ref[...], m_ref[...], v_ref[...])

    blk = pl.BlockSpec((None, ta, b), lambda l, i: (l, i, 0))
    return pl.pallas_call(
        body, name=name, grid=(depth, a // ta),
        in_specs=[pl.BlockSpec((None, N_DEV, ta, b), lambda l, i: (l, 0, i, 0)), blk, blk, blk], out_specs=[blk] * 4,
        out_shape=[jax.ShapeDtypeStruct((depth, a, b), F32)] * 4,
        compiler_params=_params(dimension_semantics=("arbitrary", "arbitrary")),
    )(parts, w, m, v)


SMALL_VIEW = {'norm_mix': (DEPTH, 1024), 'ssm_norm': (DEPTH, 1024), 'attn_out_norm': (DEPTH, 1024),
              'norm_mem_q': (DEPTH, 1024), 'norm_mem_kv': (DEPTH, 1024), 'norm_ffn': (DEPTH, 1024),
              'q_norm': (DEPTH, 384), 'kv_norm': (DEPTH, 256), 'ssm_conv_b': (DEPTH, 2048), 'ffn_conv_b': (DEPTH, 5632),
              'dt_bias': (DEPTH, SSM_HEADS), 'a_log': (DEPTH, SSM_HEADS), 'd_skip': (DEPTH, SSM_HEADS),
              'ssm_conv_w': (DEPTH, SSM_CONV * CONV_CH // N_DEV), 'ffn_conv_w': (DEPTH, FFN_CONV * 2 * D_FF // N_DEV),
              'final_norm': (1, 1024)}
SMALL_NAMES = list(SMALL_VIEW)
SMALL_SHARDED = {'ssm_conv_w': (SSM_CONV, CONV_CH // N_DEV, CONV_CH), 'ffn_conv_w': (FFN_CONV, 2 * D_FF // N_DEV, 2 * D_FF)}


def adamw_small(gathered, ws, ms, vs):
    nsm = len(SMALL_NAMES)

    def body(*refs):
        g8_ref = refs[0]
        w_refs, m_refs, v_refs = refs[1:1 + nsm], refs[1 + nsm:1 + 2 * nsm], refs[1 + 2 * nsm:1 + 3 * nsm]
        outs = refs[1 + 3 * nsm:1 + 7 * nsm]
        sum_ref = refs[1 + 7 * nsm]
        shard_bufs = refs[2 + 7 * nsm:]
        tot = g8_ref[:, 0, :]
        for d in range(1, N_DEV):
            tot = tot + g8_ref[:, d, :]
        sum_ref[...] = tot
        mx, my, mc, _ = _place()
        dev = 4 * mx + 2 * my + mc

        def update(i, g):
            d, nm, nv = _adam(g, w_refs[i][...], m_refs[i][...], v_refs[i][...])
            outs[i][...] = g
            outs[nsm + i][...] = d
            outs[2 * nsm + i][...] = nm
            outs[3 * nsm + i][...] = nv

        for i, name in enumerate(SMALL_NAMES):
            rows, cols = SMALL_VIEW[name]
            off = SMALL_OFF[name]
            if name in SMALL_SHARDED:
                taps, per, full = SMALL_SHARDED[name]
                buf = shard_bufs[list(SMALL_SHARDED).index(name)]
                for d in range(N_DEV):
                    @pl.when(dev == d)
                    def _(d=d, taps=taps, per=per, full=full, off=off, buf=buf):
                        for k in range(taps):
                            buf[:, per * k:per * (k + 1)] = sum_ref[:, off + full * k + per * d:off + full * k + per * (d + 1)]
                update(i, buf[...])
            else:
                update(i, sum_ref[0:rows, off:off + cols])

    views = [jax.ShapeDtypeStruct(SMALL_VIEW[n], F32) for n in SMALL_NAMES]
    vmem = pl.BlockSpec(memory_space=pltpu.VMEM)
    res = pl.pallas_call(
        body, name="adamw_small", out_shape=views * 4, in_specs=[vmem] * (1 + 3 * nsm), out_specs=[vmem] * (4 * nsm),
        scratch_shapes=[pltpu.VMEM((DEPTH, SMALL_W), F32)] + [pltpu.VMEM(SMALL_VIEW[n], F32) for n in SMALL_SHARDED],
        compiler_params=_params(),
    )(gathered, *[ws[n] for n in SMALL_NAMES], *[ms[n] for n in SMALL_NAMES], *[vs[n] for n in SMALL_NAMES])
    return [dict(zip(SMALL_NAMES, res[k * nsm:(k + 1) * nsm])) for k in range(4)]


def _layer_weights(gathered):
    w = {}
    for n, g in gathered.items():
        if n == 'w_in':
            w['w_proj'] = assemble_proj(g)
        elif n == 'w_uq':
            w['w_uq'] = assemble_uq(g)
        elif n == 'w_ukv':
            w['w_kn'], w['w_v'] = assemble_ukv(g)
        elif n == 'w_up':
            w['w_g'], w['w_vv'] = assemble_up(g)
        else:
            w[n] = g.reshape(N_DEV * BIG[n][0], BIG[n][1])
    return w


def layer_fwd(x0, mem, cosm, sinm, w, sm, l, tie=None):
    gain = lambda n: (sm[n], l)
    sv = dict(x0=x0)
    sv['h1'] = rmsnorm_fwd(x0, gain('norm_mix'), "norm_mix_fwd", tie=tie)
    proj = sv['proj'] = matmul([(sv['h1'], w['w_proj'])], 'nn', F32, "proj_fwd")
    sv['xbc'] = ssm_conv_fwd(proj, sm['ssm_conv_w'], sm['ssm_conv_b'], l)
    sv['y'], sv['prevs'] = ssd_fwd(sv['xbc'], proj, sm['ptile'], l)
    mix = gate_norm_fwd(sv['y'], proj, gain('ssm_norm'))
    sv['cqn'] = rmsnorm_fwd(proj, gain('q_norm'), "q_norm_fwd", Q_LORA, OFF_CQ // Q_LORA)
    sv['ckvn'] = rmsnorm_fwd(proj, gain('kv_norm'), "kv_norm_fwd", KV_LORA, OFF_CKV // KV_LORA)
    q = matmul([(sv['cqn'], w['w_uq'])], 'nn', F32, "uq_fwd")
    sv['q'] = rope_q(q, cosm, sinm, "rope_q_fwd")
    kn = matmul([(sv['ckvn'], w['w_kn'])], 'nn', BF16, "kn_fwd")
    sv['k'] = build_k(kn, proj, cosm, sinm)
    sv['v'] = matmul([(sv['ckvn'], w['w_v'])], 'nn', BF16, "v_fwd")
    sv['o'], sv['lse'] = mla_fwd(sv['q'], sv['k'], sv['v'])
    mix = sv['mix'] = rmsnorm_fwd(sv['o'], gain('attn_out_norm'), "attn_out_norm_fwd", out=(D_SSM, BF16, D_MIX, 1),
                                  into=(mix, 0))
    x1 = sv['x1'] = matmul([(mix, w['w_out'])], 'nn', F32, "out_fwd", add=x0)
    sv['hq'] = rmsnorm_fwd(x1, gain('norm_mem_q'), "norm_mem_q_fwd")
    sv['mn'] = rmsnorm_fwd(mem, gain('norm_mem_kv'), "norm_mem_kv_fwd")
    sv['mq'] = matmul([(sv['hq'], w['w_mq'])], 'nn', BF16, "mq_fwd")
    sv['mk'] = matmul([(sv['mn'], w['w_mk'])], 'nn', BF16, "mk_fwd")
    sv['mv'] = matmul([(sv['mn'], w['w_mv'])], 'nn', BF16, "mv_fwd")
    sv['om'] = mem_fwd(sv['mq'], sv['mk'], sv['mv'])
    x2 = sv['x2'] = matmul([(sv['om'], w['w_mo'])], 'nn', F32, "mo_fwd", add=x1)
    sv['h3'] = rmsnorm_fwd(x2, gain('norm_ffn'), "norm_ffn_fwd")
    sv['ug'] = matmul([(sv['h3'], w['w_g'])], 'nn', F32, "up_g_fwd")
    sv['uv'] = matmul([(sv['h3'], w['w_vv'])], 'nn', F32, "up_v_fwd")
    sv['a'] = ffn_act_fwd(sv['ug'], sv['uv'], sm['ffn_conv_w'], sm['ffn_conv_b'], l)
    x3 = matmul([(sv['a'], w['w_down'])], 'nn', F32, "down_fwd", add=x2)
    return x3, sv


EARLY_GRADS = ('w_down', 'w_up', 'w_mo', 'w_mq', 'w_mk', 'w_mv', 'w_out')
LATE_GRADS = ('w_uq', 'w_ukv', 'w_in')


def layer_bwd(dx3, mem, cosm, sinm_neg, w, sm, l, sv, on_grads, tie=None):
    gain = lambda n: (sm[n], l)
    big, small = {}, {}
    proj = sv['proj']
    da = matmul([(dx3, w['w_down'])], 'nt', BF16, "down_bwd_a", tie=tie)
    big['w_down'] = matmul([(sv['a'], dx3)], 'tn', BF16, "down_bwd_w")
    dug, duv, dcwg, dcwv, dcbg, dcbv = ffn_act_bwd(sv['ug'], sv['uv'], sm['ffn_conv_w'], sm['ffn_conv_b'], l, da)
    small['ffn_conv_w'] = jnp.concatenate([dcwg, dcwv], axis=1)
    small['ffn_conv_b'] = jnp.concatenate([dcbg, dcbv], axis=1)
    dh3 = matmul([(dug, w['w_g']), (duv, w['w_vv'])], 'nt', BF16, "up_bwd_h")
    big['w_up'] = extract_up(matmul([(sv['h3'], dug)], 'tn', BF16, "up_g_bwd_w"),
                             matmul([(sv['h3'], duv)], 'tn', BF16, "up_v_bwd_w"))
    dx2, small['norm_ffn'] = rmsnorm_bwd(sv['x2'], gain('norm_ffn'), dh3, "norm_ffn_bwd", resid=dx3)
    dom = matmul([(dx2, w['w_mo'])], 'nt', BF16, "mo_bwd_a")
    big['w_mo'] = matmul([(sv['om'], dx2)], 'tn', BF16, "mo_bwd_w")
    dmq, dmk, dmv = mem_bwd(sv['mq'], sv['mk'], sv['mv'], dom)
    dhq = matmul([(dmq, w['w_mq'])], 'nt', BF16, "mq_bwd_a")
    big['w_mq'] = matmul([(sv['hq'], dmq)], 'tn', BF16, "mq_bwd_w")
    dmn = matmul([(dmk, w['w_mk']), (dmv, w['w_mv'])], 'nt', BF16, "mkv_bwd_a")
    big['w_mk'] = matmul([(sv['mn'], dmk)], 'tn', BF16, "mk_bwd_w")
    big['w_mv'] = matmul([(sv['mn'], dmv)], 'tn', BF16, "mv_bwd_w")
    _, small['norm_mem_kv'] = rmsnorm_bwd(mem, gain('norm_mem_kv'), dmn, "norm_mem_kv_bwd", dx_dtype=BF16)
    dx1, small['norm_mem_q'] = rmsnorm_bwd(sv['x1'], gain('norm_mem_q'), dhq, "norm_mem_q_bwd", resid=dx2)
    dmix = matmul([(dx1, w['w_out'])], 'nt', BF16, "out_bwd_a")
    big['w_out'] = matmul([(sv['mix'], dx1)], 'tn', BF16, "out_bwd_w")
    early = {n: big.pop(n).reshape((N_DEV,) + BIG[n]) if n != 'w_up' else big.pop(n) for n in EARLY_GRADS}
    tie = on_grads(l, 'a', early)
    dy, dz, small['ssm_norm'] = gate_norm_bwd(sv['y'], proj, gain('ssm_norm'), dmix, tie=tie)
    dxbc_act, dsmall_ssd, small['ptile'] = ssd_bwd(sv['xbc'], proj, sm['ptile'], l, sv['prevs'], dy)
    dxbc, small['ssm_conv_w'], small['ssm_conv_b'] = ssm_conv_bwd(proj, sm['ssm_conv_w'], sm['ssm_conv_b'], l, dxbc_act)
    do, small['attn_out_norm'] = rmsnorm_bwd(sv['o'], gain('attn_out_norm'), dmix, "attn_out_norm_bwd", dh_colblock=1)
    dq_rot, dk, dv = mla_bwd(sv['q'], sv['k'], sv['v'], sv['o'], sv['lse'], do)
    dq = rope_q(dq_rot, cosm, sinm_neg, "rope_q_bwd")
    dsmall = dsmall_bwd(dk, dsmall_ssd, cosm, sinm_neg)
    dcqn = matmul([(dq, w['w_uq'])], 'nt', BF16, "uq_bwd_a")
    big['w_uq'] = extract_uq(matmul([(sv['cqn'], dq)], 'tn', BF16, "uq_bwd_w"))
    dckvn = matmul([(dk, w['w_kn']), (dv, w['w_v'])], 'nt', BF16, "ukv_bwd_a")
    big['w_ukv'] = extract_ukv(matmul([(sv['ckvn'], dk)], 'tn', BF16, "kn_bwd_w"),
                               matmul([(sv['ckvn'], dv)], 'tn', BF16, "v_bwd_w"))
    dcq, small['q_norm'] = rmsnorm_bwd(proj, gain('q_norm'), dcqn, "q_norm_bwd", width=Q_LORA,
                                       colblock=OFF_CQ // Q_LORA, dx_dtype=BF16)
    dckv, small['kv_norm'] = rmsnorm_bwd(proj, gain('kv_norm'), dckvn, "kv_norm_bwd", width=KV_LORA,
                                         colblock=OFF_CKV // KV_LORA, dx_dtype=BF16)
    wp = w['w_proj']
    xbc_half = lambda c: Opnd(dxbc, c0=c, shape=(dxbc.shape[0], 1024))
    wwin = lambda off, width: Opnd(wp, c0=off // width, shape=(D_MODEL, width))
    dh1 = matmul([(dz, wwin(OFF_Z, 1024)), (xbc_half(0), wwin(OFF_XBC, 1024)), (xbc_half(1), wwin(OFF_XBC + 1024, 1024)),
                  (dcq, wwin(OFF_CQ, Q_LORA)), (dsmall, wwin(OFF_SMALL, LANES)), (dckv, wwin(OFF_CKV, KV_LORA))],
                 'nt', BF16, "proj_bwd_a")
    h1 = sv['h1']
    big['w_in'] = extract_proj(
        matmul([(h1, dz)], 'tn', BF16, "proj_z_bwd_w"), matmul([(h1, dxbc)], 'tn', BF16, "proj_xbc_bwd_w"),
        matmul([(h1, dcq)], 'tn', BF16, "proj_cq_bwd_w"), matmul([(h1, dsmall)], 'tn', BF16, "proj_small_bwd_w"),
        matmul([(h1, dckv)], 'tn', BF16, "proj_ckv_bwd_w"))
    dx0, small['norm_mix'] = rmsnorm_bwd(sv['x0'], gain('norm_mix'), dh1, "norm_mix_bwd", resid=dx1)
    return dx0, on_grads(l, 'b', big), small


def _small_row(small, final=None):
    pt = small['ptile']
    parts = []
    for n, wd in SMALL_SEGS:
        if n in ('dt_bias', 'a_log', 'd_skip'):
            parts.append(pt[('dt_bias', 'a_log', 'd_skip').index(n)][None, :])
        elif n in SMALL_SHARDED:
            parts.append(small[n].reshape(1, wd))
        elif n == 'final_norm':
            parts.append(final if final is not None else jnp.zeros((1, wd), F32))
        else:
            parts.append(small[n])
    return jnp.concatenate(parts, axis=1)


def _rope_tables(positions):
    inv_freq = 1.0 / (ROPE_THETA ** (jnp.arange(0, QK_ROPE, 2, dtype=F32) / QK_ROPE))
    ang = positions.astype(F32)[:, None] * inv_freq
    cos, sin = jnp.cos(ang), jnp.sin(ang)
    s = positions.shape[0]
    pad = jnp.zeros((s, LANES - ROPE_LANE0 - QK_ROPE), F32)
    cosm = jnp.concatenate([jnp.ones((s, ROPE_LANE0), F32), cos, cos, pad], axis=1)
    sinm = jnp.concatenate([jnp.zeros((s, ROPE_LANE0), F32), -sin, sin, pad], axis=1)
    return cosm, sinm


def _small_views(rep, conv_full):
    sm = {n: rep[n].reshape(DEPTH, 1, -1) for n in ('norm_mix', 'ssm_norm', 'attn_out_norm', 'norm_mem_q',
                                                    'norm_mem_kv', 'norm_ffn', 'q_norm', 'kv_norm', 'ssm_conv_b',
                                                    'ffn_conv_b')}
    sm.update(conv_full)
    rows = jnp.stack([rep['dt_bias'], rep['a_log'], rep['d_skip']], axis=1)
    sm['ptile'] = jnp.pad(rows, ((0, 0), (0, 8 - 3), (0, LANES - SSM_HEADS)))
    return sm


def local_step(x, mem, positions, target, sm, final_norm, weights_of, on_grads):
    cosm, sinm = _rope_tables(positions)
    sinm_neg = -sinm
    saved, ws = [], []
    h = x
    for l in range(DEPTH):
        w, tie = weights_of(l, h)
        ws.append(w)
        h, sv = layer_fwd(h, mem, cosm, sinm, w, sm, l, tie=tie)
        saved.append(sv)
    dx, dfinal, lossv = loss_head(h, (final_norm.reshape(1, 1, -1), 0), target)
    rows = [None] * DEPTH
    tie = None
    for l in reversed(range(DEPTH)):
        dx, tie, small = layer_bwd(dx, mem, cosm, sinm_neg, ws[l], sm, l, saved[l], on_grads, tie=tie)
        rows[l] = _small_row(small, dfinal if l == 0 else None)
    return lossv[0, 0], dx, jnp.concatenate(rows, axis=0)


def kernel(x, mem, positions, norm_mix, w_in, ssm_conv_w, ssm_conv_b, dt_bias, a_log, d_skip, ssm_norm, q_norm, w_uq, kv_norm, w_ukv, attn_out_norm, w_out, norm_mem_q, norm_mem_kv, w_mq, w_mk, w_mv, w_mo, norm_ffn, w_up, ffn_conv_w, ffn_conv_b, w_down, final_norm, loss_target, m_norm_mix, m_w_in, m_ssm_conv_w, m_ssm_conv_b, m_dt_bias, m_a_log, m_d_skip, m_ssm_norm, m_q_norm, m_w_uq, m_kv_norm, m_w_ukv, m_attn_out_norm, m_w_out, m_norm_mem_q, m_norm_mem_kv, m_w_mq, m_w_mk, m_w_mv, m_w_mo, m_norm_ffn, m_w_up, m_ffn_conv_w, m_ffn_conv_b, m_w_down, m_final_norm, v_norm_mix, v_w_in, v_ssm_conv_w, v_ssm_conv_b, v_dt_bias, v_a_log, v_d_skip, v_ssm_norm, v_q_norm, v_w_uq, v_kv_norm, v_w_ukv, v_attn_out_norm, v_w_out, v_norm_mem_q, v_norm_mem_kv, v_w_mq, v_w_mk, v_w_mv, v_w_mo, v_norm_ffn, v_w_up, v_ffn_conv_w, v_ffn_conv_b, v_w_down, v_final_norm):
    args = locals()
    wts = {n: args[n] for n in WEIGHT_NAMES}
    ms = {n: args['m_' + n] for n in WEIGHT_NAMES}
    vs = {n: args['v_' + n] for n in WEIGHT_NAMES}

    st = dict(srcs={n: wts[n].astype(BF16) for n in BIG_NAMES}, exchanges=[],
              lands={n: lax.empty((DEPTH, N_DEV) + BIG[n], BF16) for n in BIG_NAMES})
    first = BIG_NAMES
    got = all_gather_blocks([st['srcs'][n] for n in first] + [wts[n] for n in SMALL_SHARDED],
                            first_only=tuple(range(len(first))))
    conv_full = {}
    for n, g in zip(SMALL_SHARDED, got[len(first):]):
        taps, per, full = SMALL_SHARDED[n]
        conv_full[n] = jnp.moveaxis(g, 1, 2).reshape(DEPTH, taps, full)
    sm = _small_views(wts, conv_full)

    def start(names, l, tag):
        send_sems, recv_sems, thru, lands, tie = gather_start([st['srcs'][n] for n in names], l, tag)
        st['srcs'].update(zip(names, thru))
        return (names, l, tag, send_sems, recv_sems, lands), tie

    def finish(handle, after):
        names, l, tag, send_sems, recv_sems, lands = handle
        thru, lands = gather_wait(l, tag, send_sems, recv_sems, [st['srcs'][n] for n in names], lands, after)
        st['srcs'].update(zip(names, thru))
        return _layer_weights(dict(zip(names, lands)))

    def weights_of(l, h):
        if l == 0:
            w = _layer_weights(dict(zip(first, got[:len(first)])))
        else:
            w = finish(st['next'], h)
        tie = None
        if l + 1 < DEPTH:
            st['next'], tie = start(BIG_NAMES, l + 1, "")
        return w, tie

    def on_grads(l, tag, big):
        names = list(big)
        send_sems, recv_sems, thru, lands, tie = grad_exchange_start(
            [big[n] for n in names], [st['lands'][n] for n in names], l, tag)
        st['lands'].update(zip(names, lands))
        st['exchanges'].append((l, tag, names, send_sems, recv_sems, thru))
        return tie

    loss_local, dx, small_rows = local_step(x[0], mem[0], positions[0], loss_target[0], sm, final_norm, weights_of,
                                            on_grads)
    outs = [{}, {}, {}, {}]

    small_all = all_gather_blocks([small_rows])[0]
    view = lambda d: {n: d[n].reshape(SMALL_VIEW[n]) for n in SMALL_NAMES}
    res = adamw_small(small_all, view(wts), view(ms), view(vs))
    for k in range(4):
        for n in SMALL_NAMES:
            outs[k][n] = res[k][n].reshape(wts[n].shape)

    def wait(exchange, after):
        l, tag, names, send_sems, recv_sems, thru = exchange
        _, lands = grad_exchange_wait(l, tag, send_sems, recv_sems, thru, [st['lands'][n] for n in names], after)
        st['lands'].update(zip(names, lands))

    def update(names):
        for n in names:
            res_n = adamw_big(st['lands'][n], wts[n], ms[n], vs[n], "adamw_" + n)
            for k in range(4):
                outs[k][n] = res_n[k]

    for exchange in st['exchanges'][:-1]:
        wait(exchange, res[0]['final_norm'])
    update(EARLY_GRADS)
    wait(st['exchanges'][-1], outs[0][EARLY_GRADS[-1]])
    update(LATE_GRADS)

    loss = lax.psum(loss_local, ("x", "y", "c"))
    return (loss, dx[None], *[outs[0][n] for n in WEIGHT_NAMES], *[outs[1][n] for n in WEIGHT_NAMES],
            *[outs[2][n] for n in WEIGHT_NAMES], *[outs[3][n] for n in WEIGHT_NAMES])
```

```python
import functools
import math
from typing import Any, NamedTuple, Optional

import jax
import jax.numpy as jnp
from jax import lax
from jax.experimental import pallas as pl
from jax.experimental.pallas import tpu as pltpu

F32 = jnp.float32
BF16 = jnp.bfloat16

D_MODEL = 1024
DEPTH = 4
MEM_LEN = 256
EPS = 1e-6
SSM_HEADS = 16
SSM_HEAD_DIM = 64
D_SSM = 1024
SSM_GROUPS = 4
SSM_STATE = 128
SSM_CONV = 4
SSM_CHUNK = 128
CONV_CH = 2048
MLA_HEADS = 16
QK_NOPE = 64
QK_ROPE = 32
V_DIM = 64
Q_LORA = 384
KV_LORA = 256
ROPE_THETA = 10000.0
MEM_HEADS = 4
MEM_HEAD_DIM = 256
D_FF = 2816
FFN_CONV = 3
D_IN = 3760
D_MIX = 2048
ADAM_LR = 0.001
ADAM_B1 = 0.9
ADAM_B2 = 0.999
ADAM_EPS = 1e-08
ADAM_WD = 0.01
ADAM_STEP = 10

N_DEV = 8
N_CHIP = 4
LANES = 128
HEAD_PAD = 128
PROJ_W = 3840
OFF_Z, OFF_XBC, OFF_CQ, OFF_SMALL, OFF_CKV = 0, 1024, 3072, 3456, 3584
ROPE_LANE0 = 64
VMEM_LIMIT = 56 * 1024 * 1024
MM_BLOCK_BYTES = 4 * 1024 * 1024
WEIGHT_NAMES = ['norm_mix', 'w_in', 'ssm_conv_w', 'ssm_conv_b', 'dt_bias', 'a_log', 'd_skip', 'ssm_norm', 'q_norm',
                'w_uq', 'kv_norm', 'w_ukv', 'attn_out_norm', 'w_out', 'norm_mem_q', 'norm_mem_kv', 'w_mq', 'w_mk',
                'w_mv', 'w_mo', 'norm_ffn', 'w_up', 'ffn_conv_w', 'ffn_conv_b', 'w_down', 'final_norm']
BIG = {'w_in': (1024, 470), 'w_uq': (384, 192), 'w_ukv': (256, 256), 'w_up': (1024, 704), 'w_out': (256, 1024),
       'w_mq': (128, 1024), 'w_mk': (128, 1024), 'w_mv': (128, 1024), 'w_mo': (128, 1024), 'w_down': (352, 1024)}
BIG_NAMES = list(BIG)
PROJ_SEGS = [(0, 1024, OFF_Z), (1024, 3072, OFF_XBC), (3072, 3088, OFF_SMALL), (3088, 3472, OFF_CQ),
             (3472, 3728, OFF_CKV), (3728, 3760, OFF_SMALL + ROPE_LANE0)]
SMALL_SEGS = [('norm_mix', 1024), ('ssm_norm', 1024), ('attn_out_norm', 1024), ('norm_mem_q', 1024),
              ('norm_mem_kv', 1024), ('norm_ffn', 1024), ('q_norm', 384), ('kv_norm', 256), ('ssm_conv_b', 2048),
              ('ffn_conv_b', 5632), ('dt_bias', 128), ('a_log', 128), ('d_skip', 128),
              ('ssm_conv_w', SSM_CONV * CONV_CH), ('ffn_conv_w', FFN_CONV * 2 * D_FF), ('final_norm', 1024)]
SMALL_OFF = {}
_o = 0
for _n, _w in SMALL_SEGS:
    SMALL_OFF[_n] = _o
    _o += _w
SMALL_W = _o


def _params(**kw):
    return pltpu.CompilerParams(vmem_limit_bytes=VMEM_LIMIT, **kw)


def _pick(n, cap):
    if n <= cap:
        return n
    best = None
    for t in range(LANES, cap + 1, LANES):
        if n % t == 0:
            best = t
    assert best is not None, (n, cap)
    return best


def _row_tile(a, cap=256):
    if a <= cap:
        return a
    best = None
    for t in range(16, cap + 1, 16):
        if a % t == 0:
            best = t
    assert best is not None, (a, cap)
    return best


class Opnd(NamedTuple):
    arr: Any
    lead: Optional[int] = None
    r0: int = 0
    c0: int = 0
    shape: Optional[tuple] = None


def _opnd(x):
    return x if isinstance(x, Opnd) else Opnd(x)


def _lshape(o):
    return tuple(o.shape) if o.shape is not None else tuple(o.arr.shape[-2:])


def _spec(o, br, bc, bi, bj):
    rr, cc = _lshape(o)
    assert rr % br == 0 and cc % bc == 0, (rr, cc, br, bc)
    ro, co = o.r0 * (rr // br), o.c0 * (cc // bc)
    if o.lead is None:
        return pl.BlockSpec((br, bc), lambda i, j: (ro + bi(i, j), co + bj(i, j)))
    return pl.BlockSpec((None, br, bc), lambda i, j: (o.lead, ro + bi(i, j), co + bj(i, j)))


_DIMS = {'nn': (((1,), (0,)), ((), ())), 'nt': (((1,), (1,)), ((), ())), 'tn': (((0,), (0,)), ((), ()))}
_ROW = lambda i, j: i
_COL = lambda i, j: j
_ZERO = lambda i, j: 0


def matmul(pairs, mode, out_dtype, name, add=None, tie=None, post=None, rows=(), fulls=(), outs=None, full_n=False):
    pairs = [(_opnd(a), _opnd(b)) for a, b in pairs]
    a0, b0 = pairs[0]
    if mode == 'nn':
        m, n = _lshape(a0)[0], _lshape(b0)[1]
    elif mode == 'nt':
        m, n = _lshape(a0)[0], _lshape(b0)[0]
    else:
        m, n = _lshape(a0)[1], _lshape(b0)[1]
    isz = lambda o: jnp.dtype(o.arr.dtype).itemsize
    osz = jnp.dtype(out_dtype).itemsize
    cap = lambda budget, per: max(LANES, budget // per // LANES * LANES)
    if mode == 'tn':
        ktok = _lshape(a0)[0]
        tm = _pick(m, cap(3 * MM_BLOCK_BYTES // 2, ktok * isz(a0)))
        tn = _pick(n, cap(3 * MM_BLOCK_BYTES // 2, ktok * isz(b0)))
    else:
        tm = _pick(m, min(2048, cap(2 * MM_BLOCK_BYTES, sum(_lshape(a)[1] * isz(a) for a, _ in pairs))))
        tn = _pick(n, min(cap(3 * MM_BLOCK_BYTES // 2, sum(_lshape(a)[1] * isz(b) for a, b in pairs)),
                          cap(MM_BLOCK_BYTES, tm * osz), n // 2 if n >= 1024 else n))
        if full_n:
            tm, tn = _pick(m, min(tm, cap(MM_BLOCK_BYTES // 2, n * osz))), n
    npairs = len(pairs)
    outs = list(outs) if outs is not None else [out_dtype]
    nadd = 1 if add is not None else 0
    nrows, nfulls = len(rows), len(fulls)

    def body(*refs):
        acc = None
        for p in range(npairs):
            a = refs[2 * p][...].astype(BF16)
            b = refs[2 * p + 1][...].astype(BF16)
            d = lax.dot_general(a, b, _DIMS[mode], preferred_element_type=F32)
            acc = d if acc is None else acc + d
        if add is not None:
            acc = acc + refs[2 * npairs][...].astype(F32)
        o_refs = refs[len(refs) - len(outs):]
        if post is None:
            o_refs[0][...] = acc.astype(out_dtype)
        else:
            x0 = 2 * npairs + nadd
            post(acc, [r[...] for r in refs[x0:x0 + nrows]], [r[...] for r in refs[x0 + nrows:x0 + nrows + nfulls]], o_refs)

    tie_specs = [pl.BlockSpec((tm, r.shape[1]), lambda i, j: (i, 0)) for r in rows]
    tie_specs += [pl.BlockSpec((None,) + f.shape[1:], lambda i, j, ld=ld, nd=f.ndim - 1: (ld,) + (0,) * nd) for f, ld in fulls]
    tie_args = list(rows) + [f for f, _ in fulls]
    if tie is not None:
        tie_specs.append(pl.BlockSpec(memory_space=pl.ANY))
        tie_args.append(tie)

    in_specs, args = [], []
    for a, b in pairs:
        if mode == 'nn':
            k = _lshape(a)[1]
            in_specs += [_spec(a, tm, k, _ROW, _ZERO), _spec(b, k, tn, _ZERO, _COL)]
        elif mode == 'nt':
            k = _lshape(a)[1]
            in_specs += [_spec(a, tm, k, _ROW, _ZERO), _spec(b, tn, k, _COL, _ZERO)]
        else:
            k = _lshape(a)[0]
            in_specs += [_spec(a, k, tm, _ZERO, _ROW), _spec(b, k, tn, _ZERO, _COL)]
        args += [a.arr, b.arr]
    if add is not None:
        in_specs.append(pl.BlockSpec((tm, tn), lambda i, j: (i, j)))
        args.append(add)
    res = pl.pallas_call(
        body, name=name, grid=(m // tm, n // tn), in_specs=in_specs + tie_specs,
        out_specs=[pl.BlockSpec((tm, tn), lambda i, j: (i, j))] * len(outs),
        out_shape=[jax.ShapeDtypeStruct((m, n), dt) for dt in outs],
        compiler_params=_params(dimension_semantics=("arbitrary", "arbitrary")),
    )(*args, *tie_args)
    return res[0] if len(outs) == 1 else res


def rowwise(fn, rows, fulls, outs, accs, name, tm=256, into=None, tie=None):
    s = rows[0][0].shape[0]
    nrow, nfull, nout, nacc = len(rows), len(fulls), len(outs), len(accs)
    nin = nrow + nfull

    def body(*refs):
        ins = [r[...] for r in refs[:nin]]
        res = fn(*ins)
        if not isinstance(res, (tuple, list)):
            res = (res,)
        orefs = refs[nin + (1 if into is not None else 0) + (1 if tie is not None else 0):]
        for k in range(nout):
            orefs[k][...] = res[k].astype(orefs[k].dtype)
        if nacc:
            @pl.when(pl.program_id(0) == 0)
            def _():
                for k in range(nacc):
                    orefs[nout + k][...] = jnp.zeros_like(orefs[nout + k])

            for k in range(nacc):
                orefs[nout + k][...] += res[nout + k].astype(orefs[nout + k].dtype)

    in_specs = [pl.BlockSpec((tm, w), lambda i, cb=cb: (i, cb)) for _, w, cb in rows]
    in_specs += [pl.BlockSpec((None,) + f.shape[1:], lambda i, ld=ld, nd=f.ndim - 1: (ld,) + (0,) * nd) for f, ld in fulls]
    args = [r[0] for r in rows] + [f for f, _ in fulls]
    aliases = {}
    if into is not None:
        in_specs.append(pl.BlockSpec(memory_space=pl.ANY))
        args.append(into[0])
        aliases = {nin: into[1]}
    if tie is not None:
        in_specs.append(pl.BlockSpec(memory_space=pl.ANY))
        args.append(tie)
    out_specs, out_shape = [], []
    for o in outs:
        w, dt = o[0], o[1]
        total, cb = (o[2], o[3]) if len(o) == 4 else (w, 0)
        out_specs.append(pl.BlockSpec((tm, w), lambda i, cb=cb: (i, cb)))
        out_shape.append(jax.ShapeDtypeStruct((s, total), dt))
    for shp, dt in accs:
        out_specs.append(pl.BlockSpec(shp, lambda i, nd=len(shp): (0,) * nd))
        out_shape.append(jax.ShapeDtypeStruct(shp, dt))
    return pl.pallas_call(
        body, name=name, grid=(s // tm,), in_specs=in_specs, out_specs=out_specs, out_shape=out_shape,
        input_output_aliases=aliases, compiler_params=_params(dimension_semantics=("arbitrary",)),
    )(*args)


def _rms(x, g):
    xf = x.astype(F32)
    var = jnp.mean(xf * xf, axis=-1, keepdims=True)
    return xf * lax.rsqrt(var + EPS) * g


def rmsnorm_fwd(x, g, name, width=None, colblock=0, out=None, into=None, tie=None):
    w = width or x.shape[1]
    return rowwise(lambda xt, gt: _rms(xt, gt), [(x, w, colblock)], [g], [out or (w, BF16)], [], name, into=into,
                   tie=tie)[0]


def rmsnorm_bwd(x, g, dh, name, resid=None, width=None, colblock=0, dh_colblock=0, dx_dtype=F32):
    w = width or x.shape[1]

    def fn(xt, dht, *rest):
        gt = rest[-1]
        _, vjp = jax.vjp(_rms, xt.astype(F32), gt)
        dx, dg = vjp(dht.astype(F32))
        if resid is not None:
            dx = dx + rest[0]
        return dx, dg

    rows = [(x, w, colblock), (dh, w, dh_colblock)] + ([(resid, w, 0)] if resid is not None else [])
    return rowwise(fn, rows, [g], [(w, dx_dtype)], [((1, w), F32)], name)


CONV_R = 64
HALO = 8


def _ext_rows(ref, i, nchunk, above, below):
    r0 = pl.multiple_of(i * CONV_R, CONV_R)
    s = ref.shape[0]
    parts = []
    if above:
        top = ref[pl.ds(pl.multiple_of(jnp.maximum(r0 - HALO, 0), HALO), HALO), :].astype(F32)
        parts.append(jnp.where(i > 0, top, 0.0))
    parts.append(ref[pl.ds(r0, CONV_R), :].astype(F32))
    if below:
        tile = 2 * HALO if ref.dtype == BF16 else HALO
        bot = ref[pl.ds(pl.multiple_of(jnp.minimum(r0 + CONV_R, s - tile), tile), tile), :].astype(F32)[0:HALO]
        parts.append(jnp.where(i < nchunk - 1, bot, 0.0))
    return jnp.concatenate(parts, axis=0)


def _conv_ext(ext, w_ref, b_ref, kw):
    y = ext[HALO:] * w_ref[kw - 1:kw, :] + b_ref[...]
    for k in range(1, kw):
        y = y + pltpu.roll(ext, k, 0)[HALO:] * w_ref[kw - 1 - k:kw - k, :]
    return y


def _conv_t_ext(d, w_ref, kw):
    n = d.shape[0]
    y = d[:n - HALO] * w_ref[kw - 1:kw, :]
    for k in range(1, kw):
        y = y + pltpu.roll(d, n - k, 0)[:n - HALO] * w_ref[kw - 1 - k:kw - k, :]
    return y


def _conv_wgrad(dp, ext, kw):
    out = [jnp.sum(dp, axis=0, keepdims=True), jnp.sum(dp * ext[HALO:HALO + CONV_R], axis=0, keepdims=True)]
    for k in range(1, kw):
        out.append(jnp.sum(dp * pltpu.roll(ext, k, 0)[HALO:HALO + CONV_R], axis=0, keepdims=True))
    return out


def _store_wgrad(res, dw_ref, db_ref, kw):
    db_ref[...] = res[0]
    for k in range(kw):
        dw_ref[kw - 1 - k:kw - k, :] = res[1 + k]


def _silu(x):
    return x * jax.nn.sigmoid(x)


def _dsilu(x):
    s = jax.nn.sigmoid(x)
    return s * (1.0 + x * (1.0 - s))


SSM_TC = 256


def ssm_conv_fwd(proj, cw, cb, l):
    s = proj.shape[0]
    off = OFF_XBC // SSM_TC

    def body(u_ref, w_ref, b_ref, o_ref):
        nchunk = s // CONV_R

        def step(i, carry):
            ext = _ext_rows(u_ref, i, nchunk, True, False)
            o_ref[pl.ds(pl.multiple_of(i * CONV_R, CONV_R), CONV_R), :] = _silu(_conv_ext(ext, w_ref, b_ref, SSM_CONV))
            return carry

        lax.fori_loop(0, nchunk, step, 0)

    return pl.pallas_call(
        body, name="ssm_conv_fwd", grid=(CONV_CH // SSM_TC,),
        in_specs=[pl.BlockSpec((s, SSM_TC), lambda j: (0, off + j)),
                  pl.BlockSpec((None, SSM_CONV, SSM_TC), lambda j: (l, 0, j)),
                  pl.BlockSpec((None, 1, SSM_TC), lambda j: (l, 0, j))],
        out_specs=pl.BlockSpec((s, SSM_TC), lambda j: (0, j)),
        out_shape=jax.ShapeDtypeStruct((s, CONV_CH), F32),
        compiler_params=_params(dimension_semantics=("arbitrary",)),
    )(proj, cw, cb)


def ssm_conv_bwd(proj, cw, cb, l, dact):
    s = proj.shape[0]
    off = OFF_XBC // SSM_TC

    def body(u_ref, w_ref, b_ref, d_ref, du_ref, dw_ref, db_ref):
        nchunk = s // CONV_R

        def step(i, carry):
            ext = _ext_rows(u_ref, i, nchunk, True, True)
            dpre = _ext_rows(d_ref, i, nchunk, False, True) * _dsilu(_conv_ext(ext, w_ref, b_ref, SSM_CONV))
            du_ref[pl.ds(pl.multiple_of(i * CONV_R, CONV_R), CONV_R), :] = _conv_t_ext(dpre, w_ref, SSM_CONV).astype(du_ref.dtype)
            return tuple(c + g for c, g in zip(carry, _conv_wgrad(dpre[:CONV_R], ext, SSM_CONV)))

        zero = jnp.zeros((1, SSM_TC), F32)
        _store_wgrad(lax.fori_loop(0, nchunk, step, (zero,) * (SSM_CONV + 1)), dw_ref, db_ref, SSM_CONV)

    return pl.pallas_call(
        body, name="ssm_conv_bwd", grid=(CONV_CH // SSM_TC,),
        in_specs=[pl.BlockSpec((s, SSM_TC), lambda j: (0, off + j)),
                  pl.BlockSpec((None, SSM_CONV, SSM_TC), lambda j: (l, 0, j)),
                  pl.BlockSpec((None, 1, SSM_TC), lambda j: (l, 0, j)), pl.BlockSpec((s, SSM_TC), lambda j: (0, j))],
        out_specs=[pl.BlockSpec((s, SSM_TC), lambda j: (0, j)), pl.BlockSpec((SSM_CONV, SSM_TC), lambda j: (0, j)),
                   pl.BlockSpec((1, SSM_TC), lambda j: (0, j))],
        out_shape=[jax.ShapeDtypeStruct((s, CONV_CH), BF16), jax.ShapeDtypeStruct((SSM_CONV, CONV_CH), F32),
                   jax.ShapeDtypeStruct((1, CONV_CH), F32)],
        compiler_params=_params(dimension_semantics=("arbitrary",)),
    )(proj, cw, cb, dact)


FFN_TC = 256
FFN_NT = D_FF // FFN_TC


def _ffn_specs(s, l):
    blk = pl.BlockSpec((s, FFN_TC), lambda j: (0, j))
    wg = pl.BlockSpec((None, FFN_CONV, FFN_TC), lambda j: (l, 0, j))
    wv = pl.BlockSpec((None, FFN_CONV, FFN_TC), lambda j: (l, 0, FFN_NT + j))
    bg = pl.BlockSpec((None, 1, FFN_TC), lambda j: (l, 0, j))
    bv = pl.BlockSpec((None, 1, FFN_TC), lambda j: (l, 0, FFN_NT + j))
    return blk, wg, wv, bg, bv


def ffn_act_fwd(ug, uv, cw, cb, l):
    s = ug.shape[0]

    def body(g_ref, v_ref, wg_ref, wv_ref, bg_ref, bv_ref, o_ref):
        nchunk = s // CONV_R

        def step(i, carry):
            cg = _conv_ext(_ext_rows(g_ref, i, nchunk, True, False), wg_ref, bg_ref, FFN_CONV)
            cv = _conv_ext(_ext_rows(v_ref, i, nchunk, True, False), wv_ref, bv_ref, FFN_CONV)
            o_ref[pl.ds(pl.multiple_of(i * CONV_R, CONV_R), CONV_R), :] = (_silu(cg) * cv).astype(o_ref.dtype)
            return carry

        lax.fori_loop(0, nchunk, step, 0)

    blk, wg, wv, bg, bv = _ffn_specs(s, l)
    return pl.pallas_call(
        body, name="ffn_act_fwd", grid=(FFN_NT,), in_specs=[blk, blk, wg, wv, bg, bv],
        out_specs=blk, out_shape=jax.ShapeDtypeStruct((s, D_FF), BF16),
        compiler_params=_params(dimension_semantics=("arbitrary",)),
    )(ug, uv, cw, cw, cb, cb)


def ffn_act_bwd(ug, uv, cw, cb, l, da):
    s = ug.shape[0]

    def body(g_ref, v_ref, wg_ref, wv_ref, bg_ref, bv_ref, da_ref, dg_ref, dv_ref, dwg_ref, dwv_ref, dbg_ref, dbv_ref):
        nchunk = s // CONV_R

        def step(i, carry):
            rows = pl.ds(pl.multiple_of(i * CONV_R, CONV_R), CONV_R)
            eg = _ext_rows(g_ref, i, nchunk, True, True)
            ev = _ext_rows(v_ref, i, nchunk, True, True)
            cg = _conv_ext(eg, wg_ref, bg_ref, FFN_CONV)
            cv = _conv_ext(ev, wv_ref, bv_ref, FFN_CONV)
            da_t = _ext_rows(da_ref, i, nchunk, False, True)
            sg = jax.nn.sigmoid(cg)
            dcg = da_t * cv * (sg * (1.0 + cg * (1.0 - sg)))
            dcv = da_t * (cg * sg)
            dg_ref[rows, :] = _conv_t_ext(dcg, wg_ref, FFN_CONV).astype(dg_ref.dtype)
            dv_ref[rows, :] = _conv_t_ext(dcv, wv_ref, FFN_CONV).astype(dv_ref.dtype)
            grads = _conv_wgrad(dcg[:CONV_R], eg, FFN_CONV) + _conv_wgrad(dcv[:CONV_R], ev, FFN_CONV)
            return tuple(c + g for c, g in zip(carry, grads))

        zero = jnp.zeros((1, FFN_TC), F32)
        res = lax.fori_loop(0, nchunk, step, (zero,) * (2 * FFN_CONV + 2))
        _store_wgrad(res[:FFN_CONV + 1], dwg_ref, dbg_ref, FFN_CONV)
        _store_wgrad(res[FFN_CONV + 1:], dwv_ref, dbv_ref, FFN_CONV)

    blk, wg, wv, bg, bv = _ffn_specs(s, l)
    wblk = pl.BlockSpec((FFN_CONV, FFN_TC), lambda j: (0, j))
    bblk = pl.BlockSpec((1, FFN_TC), lambda j: (0, j))
    return pl.pallas_call(
        body, name="ffn_act_bwd", grid=(FFN_NT,), in_specs=[blk, blk, wg, wv, bg, bv, blk],
        out_specs=[blk, blk, wblk, wblk, bblk, bblk],
        out_shape=[jax.ShapeDtypeStruct((s, D_FF), BF16), jax.ShapeDtypeStruct((s, D_FF), BF16),
                   jax.ShapeDtypeStruct((FFN_CONV, D_FF), F32), jax.ShapeDtypeStruct((FFN_CONV, D_FF), F32),
                   jax.ShapeDtypeStruct((1, D_FF), F32), jax.ShapeDtypeStruct((1, D_FF), F32)],
        compiler_params=_params(dimension_semantics=("arbitrary",)),
    )(ug, uv, cw, cw, cb, cb, da)


def _dot(a, b, mode):
    return lax.dot_general(a.astype(BF16), b.astype(BF16), _DIMS[mode], preferred_element_type=F32)


@jax.custom_vjp
def mm_nn(a, b):
    return _dot(a, b, 'nn')


@jax.custom_vjp
def mm_nt(a, b):
    return _dot(a, b, 'nt')


@jax.custom_vjp
def mm_tn(a, b):
    return _dot(a, b, 'tn')


mm_nn.defvjp(lambda a, b: (_dot(a, b, 'nn'), (a, b)), lambda r, g: (_dot(g, r[1], 'nt'), _dot(r[0], g, 'tn')))
mm_nt.defvjp(lambda a, b: (_dot(a, b, 'nt'), (a, b)), lambda r, g: (_dot(g, r[1], 'nn'), _dot(g, r[0], 'tn')))
mm_tn.defvjp(lambda a, b: (_dot(a, b, 'tn'), (a, b)), lambda r, g: (_dot(r[1], g, 'nt'), _dot(r[0], g, 'nn')))


def _tri(n, lower):
    r = lax.broadcasted_iota(jnp.int32, (n, n), 0)
    c = lax.broadcasted_iota(jnp.int32, (n, n), 1)
    return jnp.where((r >= c) if lower else (r <= c), 1.0, 0.0).astype(F32)


def _tri_dot(a, lower):
    return jnp.dot(_tri(a.shape[0], lower), a, precision=lax.Precision.HIGHEST, preferred_element_type=F32)


@jax.custom_vjp
def _cumsum_rows(a):
    return _tri_dot(a, True)


_cumsum_rows.defvjp(lambda a: (_tri_dot(a, True), None), lambda _, g: (_tri_dot(g, False),))


def _softplus(x):
    return jnp.maximum(x, 0.0) + jnp.log(1.0 + jnp.exp(-jnp.abs(x)))


def _ssd_chunk(xs, bs, cs, small, dtb, alog, dsk, prev):
    ln = small.shape[0]
    lane = lax.broadcasted_iota(jnp.int32, (ln, LANES), 1)
    lane1 = lax.broadcasted_iota(jnp.int32, (1, LANES), 1)
    sub = lax.broadcasted_iota(jnp.int32, (LANES, ln), 0)
    rowi = lax.broadcasted_iota(jnp.int32, (ln, LANES), 0)
    tril = lax.broadcasted_iota(jnp.int32, (ln, ln), 0) >= lax.broadcasted_iota(jnp.int32, (ln, ln), 1)
    first = lane < SSM_HEAD_DIM
    first1 = lane1 < SSM_HEAD_DIM

    dt = _softplus(small + dtb)
    acs = _cumsum_rows(dt * (-jnp.exp(alog)))
    acs_t = acs.T
    last = jnp.sum(jnp.where(rowi == ln - 1, acs, 0.0), axis=0, keepdims=True)

    def col(a, h):
        return jnp.sum(jnp.where(lane == h, a, 0.0), axis=1, keepdims=True)

    def one(a, h):
        return jnp.sum(jnp.where(lane1 == h, a, 0.0), axis=1, keepdims=True)

    def rowv(at, h):
        return jnp.sum(jnp.where(sub == h, at, 0.0), axis=0, keepdims=True)

    cb = [mm_nt(cs[g], bs[g]) for g in range(SSM_GROUPS)]
    ys, news = [], []
    for j in range(SSM_HEADS // 2):
        g = j // 2
        h0, h1 = 2 * j, 2 * j + 1
        xd = xs[j] * jnp.where(first, col(dt, h0), col(dt, h1))
        yd, st, ea, cd = None, None, [], []
        for h, xdh in ((h0, jnp.where(first, xd, 0.0)), (h1, jnp.where(first, 0.0, xd))):
            ac = col(acs, h)
            la = one(last, h)
            lmat = jnp.exp(jnp.where(tril, ac - rowv(acs_t, h), -jnp.inf))
            yh = mm_nn(cb[g] * lmat, xdh)
            sh = mm_tn(bs[g] * jnp.exp(la - ac), xdh)
            yd = yh if yd is None else yd + yh
            st = sh if st is None else st + sh
            ea.append(jnp.exp(ac))
            cd.append(jnp.exp(la))
        yoff = mm_nn(cs[g], prev[j]) * jnp.where(first, ea[0], ea[1])
        ys.append(yd + yoff + xs[j] * jnp.where(first1, one(dsk, h0), one(dsk, h1)))
        news.append(prev[j] * jnp.where(first1, cd[0], cd[1]) + st)
    return ys, news


N_PAIR = SSM_HEADS // 2


def ssd_fwd(xbc, proj, ptile, l):
    s = xbc.shape[0]
    nch = s // SSM_CHUNK

    def body(xbc_ref, small_ref, p_ref, y_ref, prev_ref, state_ref):
        @pl.when(pl.program_id(0) == 0)
        def _():
            state_ref[...] = jnp.zeros_like(state_ref)

        xs = [xbc_ref[:, LANES * j:LANES * (j + 1)] for j in range(N_PAIR)]
        bs = [xbc_ref[:, D_SSM + LANES * g:D_SSM + LANES * (g + 1)] for g in range(SSM_GROUPS)]
        cs = [xbc_ref[:, D_SSM + 512 + LANES * g:D_SSM + 512 + LANES * (g + 1)] for g in range(SSM_GROUPS)]
        prev = [state_ref[j] for j in range(N_PAIR)]
        ys, news = _ssd_chunk(xs, bs, cs, small_ref[...], p_ref[0:1, :], p_ref[1:2, :], p_ref[2:3, :], prev)
        for j in range(N_PAIR):
            y_ref[:, LANES * j:LANES * (j + 1)] = ys[j]
            prev_ref[0, j] = prev[j]
            state_ref[j] = news[j]

    return pl.pallas_call(
        body, name="ssd_fwd", grid=(nch,),
        in_specs=[pl.BlockSpec((SSM_CHUNK, CONV_CH), lambda c: (c, 0)),
                  pl.BlockSpec((SSM_CHUNK, LANES), lambda c: (c, OFF_SMALL // LANES)),
                  pl.BlockSpec((None, 8, LANES), lambda c: (l, 0, 0))],
        out_specs=[pl.BlockSpec((SSM_CHUNK, D_SSM), lambda c: (c, 0)),
                   pl.BlockSpec((1, N_PAIR, SSM_STATE, LANES), lambda c: (c, 0, 0, 0))],
        out_shape=[jax.ShapeDtypeStruct((s, D_SSM), F32), jax.ShapeDtypeStruct((nch, N_PAIR, SSM_STATE, LANES), F32)],
        scratch_shapes=[pltpu.VMEM((N_PAIR, SSM_STATE, LANES), F32)],
        compiler_params=_params(dimension_semantics=("arbitrary",)),
    )(xbc, proj, ptile)


def ssd_bwd(xbc, proj, ptile, l, prevs, dy):
    s = xbc.shape[0]
    nch = s // SSM_CHUNK

    def body(xbc_ref, small_ref, p_ref, prev_ref, dy_ref, dxbc_ref, dsmall_ref, dp_ref, dstate_ref):
        @pl.when(pl.program_id(0) == 0)
        def _():
            dstate_ref[...] = jnp.zeros_like(dstate_ref)
            dp_ref[...] = jnp.zeros_like(dp_ref)

        xs = [xbc_ref[:, LANES * j:LANES * (j + 1)] for j in range(N_PAIR)]
        bs = [xbc_ref[:, D_SSM + LANES * g:D_SSM + LANES * (g + 1)] for g in range(SSM_GROUPS)]
        cs = [xbc_ref[:, D_SSM + 512 + LANES * g:D_SSM + 512 + LANES * (g + 1)] for g in range(SSM_GROUPS)]
        prev = [prev_ref[0, j] for j in range(N_PAIR)]
        dys = [dy_ref[:, LANES * j:LANES * (j + 1)] for j in range(N_PAIR)]
        dnew = [dstate_ref[j] for j in range(N_PAIR)]
        _, vjp = jax.vjp(_ssd_chunk, xs, bs, cs, small_ref[...], p_ref[0:1, :], p_ref[1:2, :], p_ref[2:3, :], prev)
        dxs, dbs, dcs, dsmall, ddtb, dalog, ddsk, dprev = vjp((dys, dnew))
        for j in range(N_PAIR):
            dxbc_ref[:, LANES * j:LANES * (j + 1)] = dxs[j]
            dstate_ref[j] = dprev[j]
        for g in range(SSM_GROUPS):
            dxbc_ref[:, D_SSM + LANES * g:D_SSM + LANES * (g + 1)] = dbs[g]
            dxbc_ref[:, D_SSM + 512 + LANES * g:D_SSM + 512 + LANES * (g + 1)] = dcs[g]
        dsmall_ref[...] = dsmall
        dp_ref[0:1, :] += ddtb
        dp_ref[1:2, :] += dalog
        dp_ref[2:3, :] += ddsk

    rev = lambda c: nch - 1 - c
    return pl.pallas_call(
        body, name="ssd_bwd", grid=(nch,),
        in_specs=[pl.BlockSpec((SSM_CHUNK, CONV_CH), lambda c: (rev(c), 0)),
                  pl.BlockSpec((SSM_CHUNK, LANES), lambda c: (rev(c), OFF_SMALL // LANES)),
                  pl.BlockSpec((None, 8, LANES), lambda c: (l, 0, 0)),
                  pl.BlockSpec((1, N_PAIR, SSM_STATE, LANES), lambda c: (rev(c), 0, 0, 0)),
                  pl.BlockSpec((SSM_CHUNK, D_SSM), lambda c: (rev(c), 0))],
        out_specs=[pl.BlockSpec((SSM_CHUNK, CONV_CH), lambda c: (rev(c), 0)),
                   pl.BlockSpec((SSM_CHUNK, LANES), lambda c: (rev(c), 0)),
                   pl.BlockSpec((8, LANES), lambda c: (0, 0))],
        out_shape=[jax.ShapeDtypeStruct((s, CONV_CH), F32), jax.ShapeDtypeStruct((s, LANES), F32),
                   jax.ShapeDtypeStruct((8, LANES), F32)],
        scratch_shapes=[pltpu.VMEM((N_PAIR, SSM_STATE, LANES), F32)],
        compiler_params=_params(dimension_semantics=("arbitrary",)),
    )(xbc, proj, ptile, prevs, dy)


ROPE_TM = 256


def _rope_tile(t, cosm, sinm):
    lane = lax.broadcasted_iota(jnp.int32, t.shape, 1)
    half = QK_ROPE // 2
    partner = jnp.where(lane < ROPE_LANE0 + half, pltpu.roll(t, LANES - half, 1), pltpu.roll(t, half, 1))
    return t * cosm + partner * sinm


def _in_rope(shape):
    lane = lax.broadcasted_iota(jnp.int32, shape, 1)
    return jnp.logical_and(lane >= ROPE_LANE0, lane < ROPE_LANE0 + QK_ROPE)


def build_k(kn, proj, cosm, sinm):
    s, w = kn.shape

    def body(k_ref, small_ref, c_ref, s_ref, o_ref):
        small = small_ref[...]
        inrope = _in_rope(small.shape)
        kpe = jnp.where(inrope, _rope_tile(jnp.where(inrope, small, 0.0), c_ref[...], s_ref[...]), 0.0)
        for h in range(MLA_HEADS):
            sl = slice(HEAD_PAD * h, HEAD_PAD * (h + 1))
            o_ref[:, sl] = (k_ref[:, sl].astype(F32) + kpe).astype(o_ref.dtype)

    row = pl.BlockSpec((ROPE_TM, w), lambda i: (i, 0))
    tab = pl.BlockSpec((ROPE_TM, LANES), lambda i: (i, 0))
    return pl.pallas_call(
        body, name="build_k", grid=(s // ROPE_TM,),
        in_specs=[row, pl.BlockSpec((ROPE_TM, LANES), lambda i: (i, OFF_SMALL // LANES)), tab, tab], out_specs=row,
        out_shape=jax.ShapeDtypeStruct((s, w), BF16), compiler_params=_params(dimension_semantics=("arbitrary",)),
    )(kn, proj, cosm, sinm)


def dsmall_bwd(dk, dsmall_ssd, cosm, sinm_neg):
    def fn(dkt, ds, c, sn):
        inrope = _in_rope(ds.shape)
        tot = dkt[:, 0:HEAD_PAD]
        for h in range(1, MLA_HEADS):
            tot = tot + dkt[:, HEAD_PAD * h:HEAD_PAD * (h + 1)]
        tot = jnp.where(inrope, tot, 0.0)
        return ds + jnp.where(inrope, _rope_tile(tot, c, sn), 0.0)

    return rowwise(fn, [(dk, MLA_HEADS * HEAD_PAD, 0), (dsmall_ssd, LANES, 0), (cosm, LANES, 0), (sinm_neg, LANES, 0)],
                   [], [(LANES, BF16)], [], "dsmall_bwd")[0]


ATT_TQ = 512
ATT_SCALE = (QK_NOPE + QK_ROPE) ** -0.5


def _att_scores(qh, kh, q0):
    s = lax.dot_general(qh, kh, _DIMS['nt'], preferred_element_type=F32) * ATT_SCALE
    r = lax.broadcasted_iota(jnp.int32, s.shape, 0) + q0
    c = lax.broadcasted_iota(jnp.int32, s.shape, 1)
    return jnp.where(c <= r, s, -1e30)


def mla_fwd(q, k, v):
    s = q.shape[0]

    def body(q_ref, k_ref, v_ref, o_ref, lse_ref):
        lane = lax.broadcasted_iota(jnp.int32, (ATT_TQ, LANES), 1)

        def block(ib):
            n = ATT_TQ * (ib + 1)
            v_t = v_ref[0:n, :]
            vlane = lax.broadcasted_iota(jnp.int32, v_t.shape, 1)
            o_tot, lse_tot = None, None
            for h in range(2):
                hs = slice(HEAD_PAD * h, HEAD_PAD * (h + 1))
                sc = _att_scores(q_ref[:, hs], k_ref[0:n, hs], ATT_TQ * ib)
                m = jnp.max(sc, axis=1, keepdims=True)
                p = jnp.exp(sc - m)
                l = jnp.sum(p, axis=1, keepdims=True)
                vh = jnp.where((vlane < V_DIM) if h == 0 else (vlane >= V_DIM), v_t, jnp.zeros_like(v_t))
                oh = lax.dot_general(p.astype(BF16), vh, _DIMS['nn'], preferred_element_type=F32) / l
                lse_h = jnp.where((lane < V_DIM) if h == 0 else (lane >= V_DIM), m + jnp.log(l), 0.0)
                o_tot = oh if o_tot is None else o_tot + oh
                lse_tot = lse_h if lse_tot is None else lse_tot + lse_h
            o_ref[...] = o_tot
            lse_ref[...] = lse_tot

        for ib in range(s // ATT_TQ):
            pl.when(pl.program_id(1) == ib)(functools.partial(block, ib))

    tile = pl.BlockSpec((ATT_TQ, LANES), lambda p, i: (i, p))
    return pl.pallas_call(
        body, name="mla_fwd", grid=(MLA_HEADS // 2, s // ATT_TQ),
        in_specs=[pl.BlockSpec((ATT_TQ, 2 * HEAD_PAD), lambda p, i: (i, p)),
                  pl.BlockSpec((s, 2 * HEAD_PAD), lambda p, i: (0, p)),
                  pl.BlockSpec((s, LANES), lambda p, i: (0, p))],
        out_specs=[tile, tile],
        out_shape=[jax.ShapeDtypeStruct((s, MLA_HEADS * V_DIM), F32)] * 2,
        compiler_params=_params(dimension_semantics=("arbitrary", "arbitrary")),
    )(q, k, v)


def mla_bwd(q, k, v, o, lse, do, cosm, sinm_neg):
    s = q.shape[0]

    def body(q_ref, k_ref, v_ref, o_ref, lse_ref, do_ref, c_ref, s_ref, dq_ref, dk_ref, dv_ref):
        i = pl.program_id(1)

        @pl.when(i == 0)
        def _():
            dk_ref[...] = jnp.zeros_like(dk_ref)
            dv_ref[...] = jnp.zeros_like(dv_ref)

        def block(ib):
            n = ATT_TQ * (ib + 1)
            o_t = o_ref[...]
            do_t = do_ref[...]
            lse_t = lse_ref[...]
            v_t = v_ref[0:n, :]
            lane = lax.broadcasted_iota(jnp.int32, do_t.shape, 1)
            for h in range(2):
                hs = slice(HEAD_PAD * h, HEAD_PAD * (h + 1))
                sel = (lane < V_DIM) if h == 0 else (lane >= V_DIM)
                qh = q_ref[:, hs]
                kh = k_ref[0:n, hs]
                doh = jnp.where(sel, do_t, 0.0)
                delta = jnp.sum(doh * o_t, axis=1, keepdims=True)
                lse_h = jnp.max(jnp.where(sel, lse_t, -jnp.inf), axis=1, keepdims=True)
                doh_b = doh.astype(BF16)
                p = jnp.exp(_att_scores(qh, kh, ATT_TQ * ib) - lse_h)
                dv_ref[0:n, :] += lax.dot_general(p.astype(BF16), doh_b, _DIMS['tn'], preferred_element_type=F32)
                dp = lax.dot_general(doh_b, v_t, _DIMS['nt'], preferred_element_type=F32)
                ds = (p * (dp - delta) * ATT_SCALE).astype(BF16)
                dk_ref[0:n, hs] += lax.dot_general(ds, qh, _DIMS['tn'], preferred_element_type=F32)
                dq = lax.dot_general(ds, kh, _DIMS['nn'], preferred_element_type=F32)
                dq_ref[:, hs] = _rope_tile(dq, c_ref[...], s_ref[...]).astype(dq_ref.dtype)

        for ib in range(s // ATT_TQ):
            pl.when(i == ib)(functools.partial(block, ib))

    tile = pl.BlockSpec((ATT_TQ, LANES), lambda p, i: (i, p))
    return pl.pallas_call(
        body, name="mla_bwd", grid=(MLA_HEADS // 2, s // ATT_TQ),
        in_specs=[pl.BlockSpec((ATT_TQ, 2 * HEAD_PAD), lambda p, i: (i, p)),
                  pl.BlockSpec((s, 2 * HEAD_PAD), lambda p, i: (0, p)),
                  pl.BlockSpec((s, LANES), lambda p, i: (0, p)), tile, tile, tile,
                  pl.BlockSpec((ATT_TQ, LANES), lambda p, i: (i, 0)), pl.BlockSpec((ATT_TQ, LANES), lambda p, i: (i, 0))],
        out_specs=[pl.BlockSpec((ATT_TQ, 2 * HEAD_PAD), lambda p, i: (i, p)),
                   pl.BlockSpec((s, 2 * HEAD_PAD), lambda p, i: (0, p)),
                   pl.BlockSpec((s, LANES), lambda p, i: (0, p))],
        out_shape=[jax.ShapeDtypeStruct((s, MLA_HEADS * HEAD_PAD), BF16),
                   jax.ShapeDtypeStruct((s, MLA_HEADS * HEAD_PAD), F32),
                   jax.ShapeDtypeStruct((s, MLA_HEADS * V_DIM), F32)],
        compiler_params=_params(dimension_semantics=("arbitrary", "arbitrary")),
    )(q, k, v, o, lse, do, cosm, sinm_neg)


MEM_TQ = 256
MEM_SCALE = MEM_HEAD_DIM ** -0.5


def _mem_probs(qh, kh):
    s = lax.dot_general(qh, kh, _DIMS['nt'], preferred_element_type=F32) * MEM_SCALE
    p = jnp.exp(s - jnp.max(s, axis=1, keepdims=True))
    return p / jnp.sum(p, axis=1, keepdims=True)


def mem_fwd(q, k, v):
    s = q.shape[0]

    def body(q_ref, k_ref, v_ref, o_ref):
        for h in range(MEM_HEADS):
            sl = slice(MEM_HEAD_DIM * h, MEM_HEAD_DIM * (h + 1))
            p = _mem_probs(q_ref[:, sl], k_ref[:, sl])
            o_ref[:, sl] = lax.dot_general(p.astype(BF16), v_ref[:, sl], _DIMS['nn'],
                                           preferred_element_type=F32).astype(o_ref.dtype)

    full = pl.BlockSpec((MEM_LEN, D_MODEL), lambda i: (0, 0))
    return pl.pallas_call(
        body, name="mem_fwd", grid=(s // MEM_TQ,),
        in_specs=[pl.BlockSpec((MEM_TQ, D_MODEL), lambda i: (i, 0)), full, full],
        out_specs=pl.BlockSpec((MEM_TQ, D_MODEL), lambda i: (i, 0)),
        out_shape=jax.ShapeDtypeStruct((s, D_MODEL), BF16),
        compiler_params=_params(dimension_semantics=("arbitrary",)),
    )(q, k, v)


def mem_bwd(q, k, v, do):
    s = q.shape[0]

    def body(q_ref, k_ref, v_ref, do_ref, dq_ref, dk_ref, dv_ref):
        @pl.when(pl.program_id(0) == 0)
        def _():
            dk_ref[...] = jnp.zeros_like(dk_ref)
            dv_ref[...] = jnp.zeros_like(dv_ref)

        for h in range(MEM_HEADS):
            sl = slice(MEM_HEAD_DIM * h, MEM_HEAD_DIM * (h + 1))
            qh, kh, vh = q_ref[:, sl], k_ref[:, sl], v_ref[:, sl]
            doh = do_ref[:, sl].astype(BF16)
            p = _mem_probs(qh, kh)
            dv_ref[:, sl] += lax.dot_general(p.astype(BF16), doh, _DIMS['tn'], preferred_element_type=F32)
            dp = lax.dot_general(doh, vh, _DIMS['nt'], preferred_element_type=F32)
            ds = (p * (dp - jnp.sum(p * dp, axis=1, keepdims=True)) * MEM_SCALE).astype(BF16)
            dq_ref[:, sl] = lax.dot_general(ds, kh, _DIMS['nn'], preferred_element_type=F32).astype(dq_ref.dtype)
            dk_ref[:, sl] += lax.dot_general(ds, qh, _DIMS['tn'], preferred_element_type=F32)

    full = pl.BlockSpec((MEM_LEN, D_MODEL), lambda i: (0, 0))
    row = pl.BlockSpec((MEM_TQ, D_MODEL), lambda i: (i, 0))
    return pl.pallas_call(
        body, name="mem_bwd", grid=(s // MEM_TQ,),
        in_specs=[row, full, full, row], out_specs=[row, full, full],
        out_shape=[jax.ShapeDtypeStruct((s, D_MODEL), BF16), jax.ShapeDtypeStruct((MEM_LEN, D_MODEL), F32),
                   jax.ShapeDtypeStruct((MEM_LEN, D_MODEL), F32)],
        compiler_params=_params(dimension_semantics=("arbitrary",)),
    )(q, k, v, do)


def _gate_norm(y, z, g):
    return _rms(y * _silu(z), g)


def gate_norm_fwd(y, proj, g):
    return rowwise(_gate_norm, [(y, D_SSM, 0), (proj, D_SSM, OFF_Z // D_SSM)], [g], [(D_SSM, BF16, D_MIX, 0)], [],
                   "gate_norm_fwd")[0]


def gate_norm_bwd(y, proj, g, dmix, tie=None):
    def fn(yt, zt, dt_, gt):
        _, vjp = jax.vjp(_gate_norm, yt, zt, gt)
        return vjp(dt_.astype(F32))

    return rowwise(fn, [(y, D_SSM, 0), (proj, D_SSM, OFF_Z // D_SSM), (dmix, D_SSM, 0)], [g],
                   [(D_SSM, F32), (D_SSM, BF16)], [((1, D_SSM), F32)], "gate_norm_bwd", tie=tie)


def loss_head(x, g, target):
    def fn(xt, tt, gt):
        def f(x_, g_):
            err = _rms(x_, g_) - tt
            return 0.5 * jnp.sum(jnp.mean(err * err, axis=-1))

        lv, (dx, dg) = jax.value_and_grad(f, argnums=(0, 1))(xt, gt)
        return dx, dg, jnp.full((1, LANES), lv, F32)

    return rowwise(fn, [(x, D_MODEL, 0), (target, D_MODEL, 0)], [g], [(D_MODEL, F32)],
                   [((1, D_MODEL), F32), ((1, LANES), F32)], "loss_head")


def _proj_runs(d):
    lo, hi = (D_IN // N_DEV) * d, (D_IN // N_DEV) * (d + 1)
    runs = []
    for a, b, new in PROJ_SEGS:
        s0, s1 = max(a, lo), min(b, hi)
        if s0 < s1:
            runs.append((s0 - lo, new + s0 - a, s1 - s0))
    return runs


LAYOUT_TM = 256


def assemble_proj(g):
    def body(g_ref, o_ref):
        o_ref[:, OFF_SMALL:OFF_SMALL + LANES] = jnp.zeros((LAYOUT_TM, LANES), o_ref.dtype)
        for d in range(N_DEV):
            for src, dst, n in _proj_runs(d):
                o_ref[:, dst:dst + n] = g_ref[d, :, src:src + n]

    return pl.pallas_call(
        body, name="assemble_proj", grid=(D_MODEL // LAYOUT_TM,),
        in_specs=[pl.BlockSpec((N_DEV, LAYOUT_TM, D_IN // N_DEV), lambda i: (0, i, 0))],
        out_specs=pl.BlockSpec((LAYOUT_TM, PROJ_W), lambda i: (i, 0)),
        out_shape=jax.ShapeDtypeStruct((D_MODEL, PROJ_W), g.dtype),
        compiler_params=_params(dimension_semantics=("arbitrary",)),
    )(g)


def extract_proj(dz, dxbc, dcq, dsmall, dckv):
    pieces = [(OFF_Z, 1024), (OFF_XBC, 2048), (OFF_CQ, Q_LORA), (OFF_SMALL, LANES), (OFF_CKV, KV_LORA)]

    def body(*refs):
        o_ref = refs[-1]
        for d in range(N_DEV):
            for src, dst, n in _proj_runs(d):
                for p, (off, w) in enumerate(pieces):
                    if off <= dst < off + w:
                        o_ref[d, :, src:src + n] = refs[p][:, dst - off:dst - off + n].astype(o_ref.dtype)

    return pl.pallas_call(
        body, name="extract_proj", grid=(D_MODEL // LAYOUT_TM,),
        in_specs=[pl.BlockSpec((LAYOUT_TM, w), lambda i: (i, 0)) for _, w in pieces],
        out_specs=pl.BlockSpec((N_DEV, LAYOUT_TM, D_IN // N_DEV), lambda i: (0, i, 0)),
        out_shape=jax.ShapeDtypeStruct((N_DEV, D_MODEL, D_IN // N_DEV), BF16),
        compiler_params=_params(dimension_semantics=("arbitrary",)),
    )(dz, dxbc, dcq, dsmall, dckv)


_QW = QK_NOPE + QK_ROPE


def assemble_uq(g):
    def body(g_ref, o_ref):
        o_ref[...] = jnp.zeros_like(o_ref)
        for d in range(N_DEV):
            for e in range(2):
                dst = HEAD_PAD * (2 * d + e)
                o_ref[:, dst:dst + _QW] = g_ref[d, :, _QW * e:_QW * (e + 1)]

    return pl.pallas_call(
        body, name="assemble_uq", grid=(1,),
        in_specs=[pl.BlockSpec((N_DEV, Q_LORA, 2 * _QW), lambda i: (0, 0, 0))],
        out_specs=pl.BlockSpec((Q_LORA, MLA_HEADS * HEAD_PAD), lambda i: (0, 0)),
        out_shape=jax.ShapeDtypeStruct((Q_LORA, MLA_HEADS * HEAD_PAD), g.dtype),
        compiler_params=_params(dimension_semantics=("arbitrary",)),
    )(g)


def extract_uq(dw):
    def body(w_ref, o_ref):
        for d in range(N_DEV):
            for e in range(2):
                src = HEAD_PAD * (2 * d + e)
                o_ref[d, :, _QW * e:_QW * (e + 1)] = w_ref[:, src:src + _QW].astype(o_ref.dtype)

    return pl.pallas_call(
        body, name="extract_uq", grid=(1,),
        in_specs=[pl.BlockSpec((Q_LORA, MLA_HEADS * HEAD_PAD), lambda i: (0, 0))],
        out_specs=pl.BlockSpec((N_DEV, Q_LORA, 2 * _QW), lambda i: (0, 0, 0)),
        out_shape=jax.ShapeDtypeStruct((N_DEV, Q_LORA, 2 * _QW), BF16),
        compiler_params=_params(dimension_semantics=("arbitrary",)),
    )(dw)


def assemble_ukv(g):
    def body(g_ref, kn_ref, v_ref):
        kn_ref[...] = jnp.zeros_like(kn_ref)
        for d in range(N_DEV):
            for e in range(2):
                h = 2 * d + e
                kn_ref[:, HEAD_PAD * h:HEAD_PAD * h + QK_NOPE] = g_ref[d, :, 128 * e:128 * e + QK_NOPE]
                v_ref[:, V_DIM * h:V_DIM * (h + 1)] = g_ref[d, :, 128 * e + QK_NOPE:128 * (e + 1)]

    return pl.pallas_call(
        body, name="assemble_ukv", grid=(1,),
        in_specs=[pl.BlockSpec((N_DEV, KV_LORA, 256), lambda i: (0, 0, 0))],
        out_specs=[pl.BlockSpec((KV_LORA, MLA_HEADS * HEAD_PAD), lambda i: (0, 0)),
                   pl.BlockSpec((KV_LORA, MLA_HEADS * V_DIM), lambda i: (0, 0))],
        out_shape=[jax.ShapeDtypeStruct((KV_LORA, MLA_HEADS * HEAD_PAD), g.dtype),
                   jax.ShapeDtypeStruct((KV_LORA, MLA_HEADS * V_DIM), g.dtype)],
        compiler_params=_params(dimension_semantics=("arbitrary",)),
    )(g)


def extract_ukv(dkn, dv):
    def body(kn_ref, v_ref, o_ref):
        for d in range(N_DEV):
            for e in range(2):
                h = 2 * d + e
                o_ref[d, :, 128 * e:128 * e + QK_NOPE] = kn_ref[:, HEAD_PAD * h:HEAD_PAD * h + QK_NOPE].astype(o_ref.dtype)
                o_ref[d, :, 128 * e + QK_NOPE:128 * (e + 1)] = v_ref[:, V_DIM * h:V_DIM * (h + 1)].astype(o_ref.dtype)

    return pl.pallas_call(
        body, name="extract_ukv", grid=(1,),
        in_specs=[pl.BlockSpec((KV_LORA, MLA_HEADS * HEAD_PAD), lambda i: (0, 0)),
                  pl.BlockSpec((KV_LORA, MLA_HEADS * V_DIM), lambda i: (0, 0))],
        out_specs=pl.BlockSpec((N_DEV, KV_LORA, 256), lambda i: (0, 0, 0)),
        out_shape=jax.ShapeDtypeStruct((N_DEV, KV_LORA, 256), BF16),
        compiler_params=_params(dimension_semantics=("arbitrary",)),
    )(dkn, dv)


_UPW = 2 * D_FF // N_DEV


def assemble_up(g):
    def body(g_ref, wg_ref, wv_ref):
        for d in range(N_DEV):
            ref = wg_ref if d < N_DEV // 2 else wv_ref
            off = _UPW * (d % (N_DEV // 2))
            ref[:, off:off + _UPW] = g_ref[d]

    half = pl.BlockSpec((LAYOUT_TM, D_FF), lambda i: (i, 0))
    return pl.pallas_call(
        body, name="assemble_up", grid=(D_MODEL // LAYOUT_TM,),
        in_specs=[pl.BlockSpec((N_DEV, LAYOUT_TM, _UPW), lambda i: (0, i, 0))],
        out_specs=[half, half], out_shape=[jax.ShapeDtypeStruct((D_MODEL, D_FF), g.dtype)] * 2,
        compiler_params=_params(dimension_semantics=("arbitrary",)),
    )(g)


def extract_up(dwg, dwv):
    def body(wg_ref, wv_ref, o_ref):
        for d in range(N_DEV):
            ref = wg_ref if d < N_DEV // 2 else wv_ref
            off = _UPW * (d % (N_DEV // 2))
            o_ref[d] = ref[:, off:off + _UPW].astype(o_ref.dtype)

    half = pl.BlockSpec((LAYOUT_TM, D_FF), lambda i: (i, 0))
    return pl.pallas_call(
        body, name="extract_up", grid=(D_MODEL // LAYOUT_TM,), in_specs=[half, half],
        out_specs=pl.BlockSpec((N_DEV, LAYOUT_TM, _UPW), lambda i: (0, i, 0)),
        out_shape=jax.ShapeDtypeStruct((N_DEV, D_MODEL, _UPW), BF16),
        compiler_params=_params(dimension_semantics=("arbitrary",)),
    )(dwg, dwv)


MESH = pl.DeviceIdType.MESH
ANY = pl.BlockSpec(memory_space=pl.ANY)


def _place():
    mx, my, mc = lax.axis_index("x"), lax.axis_index("y"), lax.axis_index("c")
    return mx, my, mc, [(1 - mx, my), (mx, 1 - my), (1 - mx, 1 - my)]


def all_gather_blocks(xs, first_only=()):
    n = len(xs)

    def body(*refs):
        x_refs, out_refs = refs[:n], refs[n:2 * n]
        send_sems, recv_sems, local_sems = refs[2 * n:]
        mx, my, mc, chips = _place()
        me, sibling = (mx, my, mc), (mx, my, 1 - mc)
        x_refs = [x_refs[t].at[0] if t in first_only else x_refs[t] for t in range(n)]

        def rows(t, px, py, pc):
            dev = 4 * px + 2 * py + pc
            return out_refs[t].at[dev] if t in first_only else out_refs[t].at[:, dev]

        def copy(t, k, block, to, src=None):
            return pltpu.make_async_remote_copy(
                src_ref=rows(t, *block) if src is None else src, dst_ref=rows(t, *block),
                send_sem=send_sems.at[t, k], recv_sem=recv_sems.at[t, k], device_id=to, device_id_type=MESH)

        mine = [pltpu.make_async_copy(x_refs[t], rows(t, *me), local_sems.at[t]) for t in range(n)]
        for cp in mine:
            cp.start()
        first = []
        for t in range(n):
            first.append(copy(t, 0, me, sibling, src=x_refs[t]))
            first += [copy(t, 1 + j, me, (*chip, mc), src=x_refs[t]) for j, chip in enumerate(chips)]
        for cp in first:
            cp.start()
        passed = []
        for j, chip in enumerate(chips):
            for t in range(n):
                copy(t, 1 + j, (*chip, mc), me).wait_recv()
                cp = copy(t, 4 + j, (*chip, mc), sibling)
                cp.start()
                passed.append(cp)
        for t in range(n):
            copy(t, 0, sibling, me).wait_recv()
            for j, chip in enumerate(chips):
                copy(t, 4 + j, (*chip, 1 - mc), me).wait_recv()
        for cp in first + passed:
            cp.wait_send()
        for cp in mine:
            cp.wait()

    return pl.pallas_call(
        body, name="all_gather_blocks",
        out_shape=[jax.ShapeDtypeStruct(((N_DEV,) if t in first_only else (x.shape[0], N_DEV)) + x.shape[1:], x.dtype)
                   for t, x in enumerate(xs)],
        in_specs=[ANY] * n, out_specs=[ANY] * n,
        scratch_shapes=[pltpu.SemaphoreType.DMA((n, 7)), pltpu.SemaphoreType.DMA((n, 7)), pltpu.SemaphoreType.DMA((n,))],
    )(*xs)


HBM = pl.BlockSpec(memory_space=pltpu.HBM)
SEM = pl.BlockSpec(memory_space=pltpu.SEMAPHORE)
EFFECT = pltpu.SideEffectType.DATAFLOW_SIDE_EFFECTING
ALL_DEVICES = [(px, py, pc) for px in range(2) for py in range(2) for pc in range(2)]


def _hbm(x):
    return pltpu.with_memory_space_constraint(x, pltpu.HBM)


def _split_start(body, name, srcs, lands):
    n = len(srcs)

    def full_body(*refs):
        body(refs[:n], refs[n:2 * n], refs[2 * n], refs[2 * n + 1])
        refs[-1][...] = jnp.zeros_like(refs[-1])

    res = pl.pallas_call(
        full_body, name=name,
        out_shape=(pltpu.SemaphoreType.DMA((n,)), pltpu.SemaphoreType.DMA((n,)),
                   *[pltpu.HBM(x.shape, x.dtype) for x in srcs], *[pltpu.HBM(x.shape, x.dtype) for x in lands],
                   jax.ShapeDtypeStruct((8, LANES), F32)),
        in_specs=[HBM] * (2 * n), out_specs=(SEM, SEM, *[HBM] * (2 * n), pl.BlockSpec(memory_space=pltpu.VMEM)),
        input_output_aliases={i: 2 + i for i in range(2 * n)},
        compiler_params=pltpu.CompilerParams(has_side_effects=EFFECT),
    )(*[_hbm(x) for x in srcs], *[_hbm(x) for x in lands])
    return res[0], res[1], list(res[2:2 + n]), list(res[2 + n:2 + 2 * n]), res[-1]


def _split_wait(name, send_sems, recv_sems, srcs, lands, after, sent, landed):
    n = len(srcs)

    def body(*refs):
        src_refs, land_refs, ssem, rsem = refs[:n], refs[n:2 * n], refs[2 * n], refs[2 * n + 1]
        mx, my, mc, _ = _place()
        for t in range(n):
            out = sent(src_refs[t], land_refs[t])
            inn = landed(land_refs[t])
            pltpu.make_async_remote_copy(src_ref=out, dst_ref=out, send_sem=ssem.at[t], recv_sem=rsem.at[t],
                                         device_id=(mx, my, mc), device_id_type=MESH).wait_send()
            pltpu.make_async_remote_copy(src_ref=inn, dst_ref=inn, send_sem=ssem.at[t], recv_sem=rsem.at[t],
                                         device_id=(mx, my, mc), device_id_type=MESH).wait_recv()

    res = pl.pallas_call(
        body, name=name,
        out_shape=(*[pltpu.HBM(x.shape, x.dtype) for x in srcs], *[pltpu.HBM(x.shape, x.dtype) for x in lands]),
        in_specs=[HBM] * (2 * n) + [SEM, SEM, ANY], out_specs=[HBM] * (2 * n),
        input_output_aliases={i: i for i in range(2 * n)},
        compiler_params=pltpu.CompilerParams(has_side_effects=EFFECT),
    )(*srcs, *lands, send_sems, recv_sems, after)
    return list(res[:n]), list(res[n:])


def gather_start(srcs, l, tag):
    lands = [lax.empty((N_DEV,) + x.shape[1:], x.dtype) for x in srcs]

    def body(src_refs, land_refs, send_sems, recv_sems):
        mx, my, mc, _ = _place()
        me = 4 * mx + 2 * my + mc
        for t in range(len(srcs)):
            for to in ALL_DEVICES:
                pltpu.make_async_remote_copy(
                    src_ref=src_refs[t].at[l], dst_ref=land_refs[t].at[me], send_sem=send_sems.at[t],
                    recv_sem=recv_sems.at[t], device_id=to, device_id_type=MESH).start()

    return _split_start(body, "gather_start_%d%s" % (l, tag), srcs, lands)


def gather_wait(l, tag, send_sems, recv_sems, srcs, lands, after):
    return _split_wait("gather_wait_%d%s" % (l, tag), send_sems, recv_sems, srcs, lands, after,
                       sent=lambda s, d: d, landed=lambda d: d)


def grad_exchange_start(es, lands, l, tag):
    def body(e_refs, land_refs, send_sems, recv_sems):
        mx, my, mc, _ = _place()
        me = 4 * mx + 2 * my + mc
        for t in range(len(es)):
            for px, py, pc in ALL_DEVICES:
                pltpu.make_async_remote_copy(
                    src_ref=e_refs[t].at[4 * px + 2 * py + pc], dst_ref=land_refs[t].at[l, me], send_sem=send_sems.at[t],
                    recv_sem=recv_sems.at[t], device_id=(px, py, pc), device_id_type=MESH).start()

    return _split_start(body, "grad_exchange_start_%d%s" % (l, tag), es, lands)


def grad_exchange_wait(l, tag, send_sems, recv_sems, es, lands, after):
    return _split_wait("grad_exchange_wait_%d%s" % (l, tag), send_sems, recv_sems, es, lands, after,
                       sent=lambda s, d: s, landed=lambda d: d.at[l])


def _adam(g, w, m, v):
    nm = ADAM_B1 * m + (1.0 - ADAM_B1) * g
    nv = ADAM_B2 * v + (1.0 - ADAM_B2) * jnp.square(g)
    m_hat = nm / (1.0 - ADAM_B1 ** ADAM_STEP)
    v_hat = nv / (1.0 - ADAM_B2 ** ADAM_STEP)
    return -ADAM_LR * (m_hat / (jnp.sqrt(v_hat) + ADAM_EPS) + ADAM_WD * w), nm, nv


def adamw_big(parts, w, m, v, name):
    depth, _, a, b = parts.shape
    ta = _row_tile(a)

    def body(p_ref, w_ref, m_ref, v_ref, g_ref, d_ref, nm_ref, nv_ref):
        g = p_ref[0].astype(F32)
        for k in range(1, N_DEV):
            g = g + p_ref[k].astype(F32)
        g_ref[...] = g
        d_ref[...], nm_ref[...], nv_ref[...] = _adam(g, w_ref[...], m_ref[...], v_ref[...])

    blk = pl.BlockSpec((None, ta, b), lambda l, i: (l, i, 0))
    return pl.pallas_call(
        body, name=name, grid=(depth, a // ta),
        in_specs=[pl.BlockSpec((None, N_DEV, ta, b), lambda l, i: (l, 0, i, 0)), blk, blk, blk], out_specs=[blk] * 4,
        out_shape=[jax.ShapeDtypeStruct((depth, a, b), F32)] * 4,
        compiler_params=_params(dimension_semantics=("arbitrary", "arbitrary")),
    )(parts, w, m, v)


SMALL_VIEW = {'norm_mix': (DEPTH, 1024), 'ssm_norm': (DEPTH, 1024), 'attn_out_norm': (DEPTH, 1024),
              'norm_mem_q': (DEPTH, 1024), 'norm_mem_kv': (DEPTH, 1024), 'norm_ffn': (DEPTH, 1024),
              'q_norm': (DEPTH, 384), 'kv_norm': (DEPTH, 256), 'ssm_conv_b': (DEPTH, 2048), 'ffn_conv_b': (DEPTH, 5632),
              'dt_bias': (DEPTH, SSM_HEADS), 'a_log': (DEPTH, SSM_HEADS), 'd_skip': (DEPTH, SSM_HEADS),
              'ssm_conv_w': (DEPTH, SSM_CONV * CONV_CH // N_DEV), 'ffn_conv_w': (DEPTH, FFN_CONV * 2 * D_FF // N_DEV),
              'final_norm': (1, 1024)}
SMALL_NAMES = list(SMALL_VIEW)
SMALL_SHARDED = {'ssm_conv_w': (SSM_CONV, CONV_CH // N_DEV, CONV_CH), 'ffn_conv_w': (FFN_CONV, 2 * D_FF // N_DEV, 2 * D_FF)}


def adamw_small(gathered, ws, ms, vs):
    nsm = len(SMALL_NAMES)

    def body(*refs):
        g8_ref = refs[0]
        w_refs, m_refs, v_refs = refs[1:1 + nsm], refs[1 + nsm:1 + 2 * nsm], refs[1 + 2 * nsm:1 + 3 * nsm]
        outs = refs[1 + 3 * nsm:1 + 7 * nsm]
        sum_ref = refs[1 + 7 * nsm]
        shard_bufs = refs[2 + 7 * nsm:]
        tot = g8_ref[:, 0, :]
        for d in range(1, N_DEV):
            tot = tot + g8_ref[:, d, :]
        sum_ref[...] = tot
        mx, my, mc, _ = _place()
        dev = 4 * mx + 2 * my + mc

        def update(i, g):
            d, nm, nv = _adam(g, w_refs[i][...], m_refs[i][...], v_refs[i][...])
            outs[i][...] = g
            outs[nsm + i][...] = d
            outs[2 * nsm + i][...] = nm
            outs[3 * nsm + i][...] = nv

        for i, name in enumerate(SMALL_NAMES):
            rows, cols = SMALL_VIEW[name]
            off = SMALL_OFF[name]
            if name in SMALL_SHARDED:
                taps, per, full = SMALL_SHARDED[name]
                buf = shard_bufs[list(SMALL_SHARDED).index(name)]
                for d in range(N_DEV):
                    @pl.when(dev == d)
                    def _(d=d, taps=taps, per=per, full=full, off=off, buf=buf):
                        for k in range(taps):
                            buf[:, per * k:per * (k + 1)] = sum_ref[:, off + full * k + per * d:off + full * k + per * (d + 1)]
                update(i, buf[...])
            else:
                update(i, sum_ref[0:rows, off:off + cols])

    views = [jax.ShapeDtypeStruct(SMALL_VIEW[n], F32) for n in SMALL_NAMES]
    vmem = pl.BlockSpec(memory_space=pltpu.VMEM)
    res = pl.pallas_call(
        body, name="adamw_small", out_shape=views * 4, in_specs=[vmem] * (1 + 3 * nsm), out_specs=[vmem] * (4 * nsm),
        scratch_shapes=[pltpu.VMEM((DEPTH, SMALL_W), F32)] + [pltpu.VMEM(SMALL_VIEW[n], F32) for n in SMALL_SHARDED],
        compiler_params=_params(),
    )(gathered, *[ws[n] for n in SMALL_NAMES], *[ms[n] for n in SMALL_NAMES], *[vs[n] for n in SMALL_NAMES])
    return [dict(zip(SMALL_NAMES, res[k * nsm:(k + 1) * nsm])) for k in range(4)]


def _layer_weights(gathered):
    w = {}
    for n, g in gathered.items():
        if n == 'w_in':
            w['w_proj'] = assemble_proj(g)
        elif n == 'w_uq':
            w['w_uq'] = assemble_uq(g)
        elif n == 'w_ukv':
            w['w_kn'], w['w_v'] = assemble_ukv(g)
        elif n == 'w_up':
            w['w_g'], w['w_vv'] = assemble_up(g)
        else:
            w[n] = g.reshape(N_DEV * BIG[n][0], BIG[n][1])
    return w


def _rope_post(acc, row_tiles, full_tiles, o_refs):
    for h in range(acc.shape[1] // HEAD_PAD):
        sl = slice(HEAD_PAD * h, HEAD_PAD * (h + 1))
        o_refs[0][:, sl] = _rope_tile(acc[:, sl], row_tiles[0], row_tiles[1]).astype(o_refs[0].dtype)


def _norm_post(acc, row_tiles, full_tiles, o_refs):
    o_refs[0][...] = acc
    o_refs[1][...] = _rms(acc, full_tiles[0]).astype(o_refs[1].dtype)


def layer_fwd(x0, h1, mem, cosm, sinm, w, sm, l, tie=None):
    gain = lambda n: (sm[n], l)
    sv = dict(x0=x0)
    sv['h1'] = h1 if h1 is not None else rmsnorm_fwd(x0, gain('norm_mix'), "norm_mix_fwd", tie=tie)
    proj = sv['proj'] = matmul([(sv['h1'], w['w_proj'])], 'nn', F32, "proj_fwd", tie=tie if h1 is not None else None)
    sv['xbc'] = ssm_conv_fwd(proj, sm['ssm_conv_w'], sm['ssm_conv_b'], l)
    sv['y'], sv['prevs'] = ssd_fwd(sv['xbc'], proj, sm['ptile'], l)
    mix = gate_norm_fwd(sv['y'], proj, gain('ssm_norm'))
    sv['cqn'] = rmsnorm_fwd(proj, gain('q_norm'), "q_norm_fwd", Q_LORA, OFF_CQ // Q_LORA)
    sv['ckvn'] = rmsnorm_fwd(proj, gain('kv_norm'), "kv_norm_fwd", KV_LORA, OFF_CKV // KV_LORA)
    sv['q'] = matmul([(sv['cqn'], w['w_uq'])], 'nn', BF16, "uq_fwd", post=_rope_post, rows=[cosm, sinm])
    kn = matmul([(sv['ckvn'], w['w_kn'])], 'nn', BF16, "kn_fwd")
    sv['k'] = build_k(kn, proj, cosm, sinm)
    sv['v'] = matmul([(sv['ckvn'], w['w_v'])], 'nn', BF16, "v_fwd")
    sv['o'], sv['lse'] = mla_fwd(sv['q'], sv['k'], sv['v'])
    mix = sv['mix'] = rmsnorm_fwd(sv['o'], gain('attn_out_norm'), "attn_out_norm_fwd", out=(D_SSM, BF16, D_MIX, 1),
                                  into=(mix, 0))
    x1, sv['hq'] = matmul([(mix, w['w_out'])], 'nn', F32, "out_fwd", add=x0, post=_norm_post,
                          fulls=[gain('norm_mem_q')], outs=[F32, BF16], full_n=True)
    sv['x1'] = x1
    sv['mn'] = rmsnorm_fwd(mem, gain('norm_mem_kv'), "norm_mem_kv_fwd")
    sv['mq'] = matmul([(sv['hq'], w['w_mq'])], 'nn', BF16, "mq_fwd")
    sv['mk'] = matmul([(sv['mn'], w['w_mk'])], 'nn', BF16, "mk_fwd")
    sv['mv'] = matmul([(sv['mn'], w['w_mv'])], 'nn', BF16, "mv_fwd")
    sv['om'] = mem_fwd(sv['mq'], sv['mk'], sv['mv'])
    x2, sv['h3'] = matmul([(sv['om'], w['w_mo'])], 'nn', F32, "mo_fwd", add=x1, post=_norm_post,
                          fulls=[gain('norm_ffn')], outs=[F32, BF16], full_n=True)
    sv['x2'] = x2
    sv['ug'] = matmul([(sv['h3'], w['w_g'])], 'nn', F32, "up_g_fwd")
    sv['uv'] = matmul([(sv['h3'], w['w_vv'])], 'nn', F32, "up_v_fwd")
    sv['a'] = ffn_act_fwd(sv['ug'], sv['uv'], sm['ffn_conv_w'], sm['ffn_conv_b'], l)
    if l + 1 < DEPTH:
        x3, h1_next = matmul([(sv['a'], w['w_down'])], 'nn', F32, "down_fwd", add=x2, post=_norm_post,
                             fulls=[(sm['norm_mix'], l + 1)], outs=[F32, BF16], full_n=True)
    else:
        x3, h1_next = matmul([(sv['a'], w['w_down'])], 'nn', F32, "down_fwd_last", add=x2), None
    return x3, h1_next, sv


EARLY_GRADS = ('w_down', 'w_up', 'w_mo', 'w_mq', 'w_mk', 'w_mv', 'w_out')
LATE_GRADS = ('w_uq', 'w_ukv', 'w_in')


def layer_bwd(dx3, mem, cosm, sinm_neg, w, sm, l, sv, on_grads, tie=None):
    gain = lambda n: (sm[n], l)
    big, small = {}, {}
    proj = sv['proj']
    da = matmul([(dx3, w['w_down'])], 'nt', BF16, "down_bwd_a", tie=tie)
    big['w_down'] = matmul([(sv['a'], dx3)], 'tn', BF16, "down_bwd_w")
    dug, duv, dcwg, dcwv, dcbg, dcbv = ffn_act_bwd(sv['ug'], sv['uv'], sm['ffn_conv_w'], sm['ffn_conv_b'], l, da)
    small['ffn_conv_w'] = jnp.concatenate([dcwg, dcwv], axis=1)
    small['ffn_conv_b'] = jnp.concatenate([dcbg, dcbv], axis=1)
    dh3 = matmul([(dug, w['w_g']), (duv, w['w_vv'])], 'nt', BF16, "up_bwd_h")
    big['w_up'] = extract_up(matmul([(sv['h3'], dug)], 'tn', BF16, "up_g_bwd_w"),
                             matmul([(sv['h3'], duv)], 'tn', BF16, "up_v_bwd_w"))
    dx2, small['norm_ffn'] = rmsnorm_bwd(sv['x2'], gain('norm_ffn'), dh3, "norm_ffn_bwd", resid=dx3)
    dom = matmul([(dx2, w['w_mo'])], 'nt', BF16, "mo_bwd_a")
    big['w_mo'] = matmul([(sv['om'], dx2)], 'tn', BF16, "mo_bwd_w")
    dmq, dmk, dmv = mem_bwd(sv['mq'], sv['mk'], sv['mv'], dom)
    dhq = matmul([(dmq, w['w_mq'])], 'nt', BF16, "mq_bwd_a")
    big['w_mq'] = matmul([(sv['hq'], dmq)], 'tn', BF16, "mq_bwd_w")
    dmn = matmul([(dmk, w['w_mk']), (dmv, w['w_mv'])], 'nt', BF16, "mkv_bwd_a")
    big['w_mk'] = matmul([(sv['mn'], dmk)], 'tn', BF16, "mk_bwd_w")
    big['w_mv'] = matmul([(sv['mn'], dmv)], 'tn', BF16, "mv_bwd_w")
    _, small['norm_mem_kv'] = rmsnorm_bwd(mem, gain('norm_mem_kv'), dmn, "norm_mem_kv_bwd", dx_dtype=BF16)
    dx1, small['norm_mem_q'] = rmsnorm_bwd(sv['x1'], gain('norm_mem_q'), dhq, "norm_mem_q_bwd", resid=dx2)
    dmix = matmul([(dx1, w['w_out'])], 'nt', BF16, "out_bwd_a")
    big['w_out'] = matmul([(sv['mix'], dx1)], 'tn', BF16, "out_bwd_w")
    early = {n: big.pop(n).reshape((N_DEV,) + BIG[n]) if n != 'w_up' else big.pop(n) for n in EARLY_GRADS}
    tie = on_grads(l, 'a', early)
    dy, dz, small['ssm_norm'] = gate_norm_bwd(sv['y'], proj, gain('ssm_norm'), dmix, tie=tie)
    dxbc_act, dsmall_ssd, small['ptile'] = ssd_bwd(sv['xbc'], proj, sm['ptile'], l, sv['prevs'], dy)
    dxbc, small['ssm_conv_w'], small['ssm_conv_b'] = ssm_conv_bwd(proj, sm['ssm_conv_w'], sm['ssm_conv_b'], l, dxbc_act)
    do, small['attn_out_norm'] = rmsnorm_bwd(sv['o'], gain('attn_out_norm'), dmix, "attn_out_norm_bwd", dh_colblock=1)
    dq, dk, dv = mla_bwd(sv['q'], sv['k'], sv['v'], sv['o'], sv['lse'], do, cosm, sinm_neg)
    dsmall = dsmall_bwd(dk, dsmall_ssd, cosm, sinm_neg)
    dcqn = matmul([(dq, w['w_uq'])], 'nt', BF16, "uq_bwd_a")
    big['w_uq'] = extract_uq(matmul([(sv['cqn'], dq)], 'tn', BF16, "uq_bwd_w"))
    dckvn = matmul([(dk, w['w_kn']), (dv, w['w_v'])], 'nt', BF16, "ukv_bwd_a")
    big['w_ukv'] = extract_ukv(matmul([(sv['ckvn'], dk)], 'tn', BF16, "kn_bwd_w"),
                               matmul([(sv['ckvn'], dv)], 'tn', BF16, "v_bwd_w"))
    dcq, small['q_norm'] = rmsnorm_bwd(proj, gain('q_norm'), dcqn, "q_norm_bwd", width=Q_LORA,
                                       colblock=OFF_CQ // Q_LORA, dx_dtype=BF16)
    dckv, small['kv_norm'] = rmsnorm_bwd(proj, gain('kv_norm'), dckvn, "kv_norm_bwd", width=KV_LORA,
                                         colblock=OFF_CKV // KV_LORA, dx_dtype=BF16)
    wp = w['w_proj']
    xbc_half = lambda c: Opnd(dxbc, c0=c, shape=(dxbc.shape[0], 1024))
    wwin = lambda off, width: Opnd(wp, c0=off // width, shape=(D_MODEL, width))
    dh1 = matmul([(dz, wwin(OFF_Z, 1024)), (xbc_half(0), wwin(OFF_XBC, 1024)), (xbc_half(1), wwin(OFF_XBC + 1024, 1024)),
                  (dcq, wwin(OFF_CQ, Q_LORA)), (dsmall, wwin(OFF_SMALL, LANES)), (dckv, wwin(OFF_CKV, KV_LORA))],
                 'nt', BF16, "proj_bwd_a")
    h1 = sv['h1']
    big['w_in'] = extract_proj(
        matmul([(h1, dz)], 'tn', BF16, "proj_z_bwd_w"), matmul([(h1, dxbc)], 'tn', BF16, "proj_xbc_bwd_w"),
        matmul([(h1, dcq)], 'tn', BF16, "proj_cq_bwd_w"), matmul([(h1, dsmall)], 'tn', BF16, "proj_small_bwd_w"),
        matmul([(h1, dckv)], 'tn', BF16, "proj_ckv_bwd_w"))
    dx0, small['norm_mix'] = rmsnorm_bwd(sv['x0'], gain('norm_mix'), dh1, "norm_mix_bwd", resid=dx1)
    return dx0, on_grads(l, 'b', big), small


def _small_row(small, final=None):
    pt = small['ptile']
    parts = []
    for n, wd in SMALL_SEGS:
        if n in ('dt_bias', 'a_log', 'd_skip'):
            parts.append(pt[('dt_bias', 'a_log', 'd_skip').index(n)][None, :])
        elif n in SMALL_SHARDED:
            parts.append(small[n].reshape(1, wd))
        elif n == 'final_norm':
            parts.append(final if final is not None else jnp.zeros((1, wd), F32))
        else:
            parts.append(small[n])
    return jnp.concatenate(parts, axis=1)


def _rope_tables(positions):
    inv_freq = 1.0 / (ROPE_THETA ** (jnp.arange(0, QK_ROPE, 2, dtype=F32) / QK_ROPE))
    ang = positions.astype(F32)[:, None] * inv_freq
    cos, sin = jnp.cos(ang), jnp.sin(ang)
    s = positions.shape[0]
    pad = jnp.zeros((s, LANES - ROPE_LANE0 - QK_ROPE), F32)
    cosm = jnp.concatenate([jnp.ones((s, ROPE_LANE0), F32), cos, cos, pad], axis=1)
    sinm = jnp.concatenate([jnp.zeros((s, ROPE_LANE0), F32), -sin, sin, pad], axis=1)
    return cosm, sinm


def _small_views(rep, conv_full):
    sm = {n: rep[n].reshape(DEPTH, 1, -1) for n in ('norm_mix', 'ssm_norm', 'attn_out_norm', 'norm_mem_q',
                                                    'norm_mem_kv', 'norm_ffn', 'q_norm', 'kv_norm', 'ssm_conv_b',
                                                    'ffn_conv_b')}
    sm.update(conv_full)
    rows = jnp.stack([rep['dt_bias'], rep['a_log'], rep['d_skip']], axis=1)
    sm['ptile'] = jnp.pad(rows, ((0, 0), (0, 8 - 3), (0, LANES - SSM_HEADS)))
    return sm


def local_step(x, mem, positions, target, sm, final_norm, weights_of, on_grads):
    cosm, sinm = _rope_tables(positions)
    sinm_neg = -sinm
    saved, ws = [], []
    h, h1 = x, None
    for l in range(DEPTH):
        w, tie = weights_of(l, h)
        ws.append(w)
        h, h1, sv = layer_fwd(h, h1, mem, cosm, sinm, w, sm, l, tie=tie)
        saved.append(sv)
    dx, dfinal, lossv = loss_head(h, (final_norm.reshape(1, 1, -1), 0), target)
    rows = [None] * DEPTH
    tie = None
    for l in reversed(range(DEPTH)):
        dx, tie, small = layer_bwd(dx, mem, cosm, sinm_neg, ws[l], sm, l, saved[l], on_grads, tie=tie)
        rows[l] = _small_row(small, dfinal if l == 0 else None)
    return lossv[0, 0], dx, jnp.concatenate(rows, axis=0)


def kernel(x, mem, positions, norm_mix, w_in, ssm_conv_w, ssm_conv_b, dt_bias, a_log, d_skip, ssm_norm, q_norm, w_uq, kv_norm, w_ukv, attn_out_norm, w_out, norm_mem_q, norm_mem_kv, w_mq, w_mk, w_mv, w_mo, norm_ffn, w_up, ffn_conv_w, ffn_conv_b, w_down, final_norm, loss_target, m_norm_mix, m_w_in, m_ssm_conv_w, m_ssm_conv_b, m_dt_bias, m_a_log, m_d_skip, m_ssm_norm, m_q_norm, m_w_uq, m_kv_norm, m_w_ukv, m_attn_out_norm, m_w_out, m_norm_mem_q, m_norm_mem_kv, m_w_mq, m_w_mk, m_w_mv, m_w_mo, m_norm_ffn, m_w_up, m_ffn_conv_w, m_ffn_conv_b, m_w_down, m_final_norm, v_norm_mix, v_w_in, v_ssm_conv_w, v_ssm_conv_b, v_dt_bias, v_a_log, v_d_skip, v_ssm_norm, v_q_norm, v_w_uq, v_kv_norm, v_w_ukv, v_attn_out_norm, v_w_out, v_norm_mem_q, v_norm_mem_kv, v_w_mq, v_w_mk, v_w_mv, v_w_mo, v_norm_ffn, v_w_up, v_ffn_conv_w, v_ffn_conv_b, v_w_down, v_final_norm):
    args = locals()
    wts = {n: args[n] for n in WEIGHT_NAMES}
    ms = {n: args['m_' + n] for n in WEIGHT_NAMES}
    vs = {n: args['v_' + n] for n in WEIGHT_NAMES}

    st = dict(srcs={n: wts[n].astype(BF16) for n in BIG_NAMES}, exchanges=[],
              lands={n: lax.empty((DEPTH, N_DEV) + BIG[n], BF16) for n in BIG_NAMES})
    first = BIG_NAMES
    got = all_gather_blocks([st['srcs'][n] for n in first] + [wts[n] for n in SMALL_SHARDED],
                            first_only=tuple(range(len(first))))
    conv_full = {}
    for n, g in zip(SMALL_SHARDED, got[len(first):]):
        taps, per, full = SMALL_SHARDED[n]
        conv_full[n] = jnp.moveaxis(g, 1, 2).reshape(DEPTH, taps, full)
    sm = _small_views(wts, conv_full)

    def start(names, l, tag):
        send_sems, recv_sems, thru, lands, tie = gather_start([st['srcs'][n] for n in names], l, tag)
        st['srcs'].update(zip(names, thru))
        return (names, l, tag, send_sems, recv_sems, lands), tie

    def finish(handle, after):
        names, l, tag, send_sems, recv_sems, lands = handle
        thru, lands = gather_wait(l, tag, send_sems, recv_sems, [st['srcs'][n] for n in names], lands, after)
        st['srcs'].update(zip(names, thru))
        return _layer_weights(dict(zip(names, lands)))

    def weights_of(l, h):
        if l == 0:
            w = _layer_weights(dict(zip(first, got[:len(first)])))
        else:
            w = finish(st['next'], h)
        tie = None
        if l + 1 < DEPTH:
            st['next'], tie = start(BIG_NAMES, l + 1, "")
        return w, tie

    def on_grads(l, tag, big):
        names = list(big)
        send_sems, recv_sems, thru, lands, tie = grad_exchange_start(
            [big[n] for n in names], [st['lands'][n] for n in names], l, tag)
        st['lands'].update(zip(names, lands))
        st['exchanges'].append((l, tag, names, send_sems, recv_sems, thru))
        return tie

    loss_local, dx, small_rows = local_step(x[0], mem[0], positions[0], loss_target[0], sm, final_norm, weights_of,
                                            on_grads)
    outs = [{}, {}, {}, {}]

    small_all = all_gather_blocks([small_rows])[0]
    view = lambda d: {n: d[n].reshape(SMALL_VIEW[n]) for n in SMALL_NAMES}
    res = adamw_small(small_all, view(wts), view(ms), view(vs))
    for k in range(4):
        for n in SMALL_NAMES:
            outs[k][n] = res[k][n].reshape(wts[n].shape)

    def wait(exchange, after):
        l, tag, names, send_sems, recv_sems, thru = exchange
        _, lands = grad_exchange_wait(l, tag, send_sems, recv_sems, thru, [st['lands'][n] for n in names], after)
        st['lands'].update(zip(names, lands))

    def update(names):
        for n in names:
            res_n = adamw_big(st['lands'][n], wts[n], ms[n], vs[n], "adamw_" + n)
            for k in range(4):
                outs[k][n] = res_n[k]

    for exchange in st['exchanges'][:-1]:
        wait(exchange, res[0]['final_norm'])
    update(EARLY_GRADS)
    wait(st['exchanges'][-1], outs[0][EARLY_GRADS[-1]])
    update(LATE_GRADS)

    loss = lax.psum(loss_local, ("x", "y", "c"))
    return (loss, dx[None], *[outs[0][n] for n in WEIGHT_NAMES], *[outs[1][n] for n in WEIGHT_NAMES],
            *[outs[2][n] for n in WEIGHT_NAMES], *[outs[3][n] for n in WEIGHT_NAMES])
```

```python
import functools
import math
from typing import Any, NamedTuple, Optional

import jax
import jax.numpy as jnp
from jax import lax
from jax.experimental import pallas as pl
from jax.experimental.pallas import tpu as pltpu

F32 = jnp.float32
BF16 = jnp.bfloat16

D_MODEL = 1024
DEPTH = 4
MEM_LEN = 256
EPS = 1e-6
SSM_HEADS = 16
SSM_HEAD_DIM = 64
D_SSM = 1024
SSM_GROUPS = 4
SSM_STATE = 128
SSM_CONV = 4
SSM_CHUNK = 128
CONV_CH = 2048
MLA_HEADS = 16
QK_NOPE = 64
QK_ROPE = 32
V_DIM = 64
Q_LORA = 384
KV_LORA = 256
ROPE_THETA = 10000.0
MEM_HEADS = 4
MEM_HEAD_DIM = 256
D_FF = 2816
FFN_CONV = 3
D_IN = 3760
D_MIX = 2048
ADAM_LR = 0.001
ADAM_B1 = 0.9
ADAM_B2 = 0.999
ADAM_EPS = 1e-08
ADAM_WD = 0.01
ADAM_STEP = 10

N_DEV = 8
N_CHIP = 4
LANES = 128
HEAD_PAD = 128
PROJ_W = 3840
OFF_Z, OFF_XBC, OFF_CQ, OFF_SMALL, OFF_CKV = 0, 1024, 3072, 3456, 3584
ROPE_LANE0 = 64
VMEM_LIMIT = 56 * 1024 * 1024
MM_BLOCK_BYTES = 4 * 1024 * 1024
WEIGHT_NAMES = ['norm_mix', 'w_in', 'ssm_conv_w', 'ssm_conv_b', 'dt_bias', 'a_log', 'd_skip', 'ssm_norm', 'q_norm',
                'w_uq', 'kv_norm', 'w_ukv', 'attn_out_norm', 'w_out', 'norm_mem_q', 'norm_mem_kv', 'w_mq', 'w_mk',
                'w_mv', 'w_mo', 'norm_ffn', 'w_up', 'ffn_conv_w', 'ffn_conv_b', 'w_down', 'final_norm']
BIG = {'w_in': (1024, 470), 'w_uq': (384, 192), 'w_ukv': (256, 256), 'w_up': (1024, 704), 'w_out': (256, 1024),
       'w_mq': (128, 1024), 'w_mk': (128, 1024), 'w_mv': (128, 1024), 'w_mo': (128, 1024), 'w_down': (352, 1024)}
BIG_NAMES = list(BIG)
PROJ_SEGS = [(0, 1024, OFF_Z), (1024, 3072, OFF_XBC), (3072, 3088, OFF_SMALL), (3088, 3472, OFF_CQ),
             (3472, 3728, OFF_CKV), (3728, 3760, OFF_SMALL + ROPE_LANE0)]
SMALL_SEGS = [('norm_mix', 1024), ('ssm_norm', 1024), ('attn_out_norm', 1024), ('norm_mem_q', 1024),
              ('norm_mem_kv', 1024), ('norm_ffn', 1024), ('q_norm', 384), ('kv_norm', 256), ('ssm_conv_b', 2048),
              ('ffn_conv_b', 5632), ('dt_bias', 128), ('a_log', 128), ('d_skip', 128),
              ('ssm_conv_w', SSM_CONV * CONV_CH), ('ffn_conv_w', FFN_CONV * 2 * D_FF), ('final_norm', 1024)]
SMALL_OFF = {}
_o = 0
for _n, _w in SMALL_SEGS:
    SMALL_OFF[_n] = _o
    _o += _w
SMALL_W = _o


def _params(**kw):
    return pltpu.CompilerParams(vmem_limit_bytes=VMEM_LIMIT, **kw)


def _pick(n, cap):
    if n <= cap:
        return n
    best = None
    for t in range(LANES, cap + 1, LANES):
        if n % t == 0:
            best = t
    assert best is not None, (n, cap)
    return best


def _row_tile(a, cap=256):
    if a <= cap:
        return a
    best = None
    for t in range(16, cap + 1, 16):
        if a % t == 0:
            best = t
    assert best is not None, (a, cap)
    return best


class Opnd(NamedTuple):
    arr: Any
    lead: Optional[int] = None
    r0: int = 0
    c0: int = 0
    shape: Optional[tuple] = None


def _opnd(x):
    return x if isinstance(x, Opnd) else Opnd(x)


def _lshape(o):
    return tuple(o.shape) if o.shape is not None else tuple(o.arr.shape[-2:])


def _spec(o, br, bc, bi, bj):
    rr, cc = _lshape(o)
    assert rr % br == 0 and cc % bc == 0, (rr, cc, br, bc)
    ro, co = o.r0 * (rr // br), o.c0 * (cc // bc)
    if o.lead is None:
        return pl.BlockSpec((br, bc), lambda i, j: (ro + bi(i, j), co + bj(i, j)))
    return pl.BlockSpec((None, br, bc), lambda i, j: (o.lead, ro + bi(i, j), co + bj(i, j)))


_DIMS = {'nn': (((1,), (0,)), ((), ())), 'nt': (((1,), (1,)), ((), ())), 'tn': (((0,), (0,)), ((), ()))}
_ROW = lambda i, j: i
_COL = lambda i, j: j
_ZERO = lambda i, j: 0


def matmul(pairs, mode, out_dtype, name, add=None, tie=None, post=None, rows=(), fulls=(), outs=None, full_n=False):
    pairs = [(_opnd(a), _opnd(b)) for a, b in pairs]
    a0, b0 = pairs[0]
    if mode == 'nn':
        m, n = _lshape(a0)[0], _lshape(b0)[1]
    elif mode == 'nt':
        m, n = _lshape(a0)[0], _lshape(b0)[0]
    else:
        m, n = _lshape(a0)[1], _lshape(b0)[1]
    isz = lambda o: jnp.dtype(o.arr.dtype).itemsize
    osz = jnp.dtype(out_dtype).itemsize
    cap = lambda budget, per: max(LANES, budget // per // LANES * LANES)
    if mode == 'tn':
        ktok = _lshape(a0)[0]
        tm = _pick(m, cap(3 * MM_BLOCK_BYTES // 2, ktok * isz(a0)))
        tn = _pick(n, cap(3 * MM_BLOCK_BYTES // 2, ktok * isz(b0)))
    else:
        tm = _pick(m, min(2048, cap(2 * MM_BLOCK_BYTES, sum(_lshape(a)[1] * isz(a) for a, _ in pairs))))
        tn = _pick(n, min(cap(3 * MM_BLOCK_BYTES // 2, sum(_lshape(a)[1] * isz(b) for a, b in pairs)),
                          cap(MM_BLOCK_BYTES, tm * osz), n // 2 if n >= 1024 else n))
        if full_n:
            tm, tn = _pick(m, min(tm, cap(MM_BLOCK_BYTES // 2, n * osz))), n
    npairs = len(pairs)
    outs = list(outs) if outs is not None else [out_dtype]
    nadd = 1 if add is not None else 0
    nrows, nfulls = len(rows), len(fulls)

    def body(*refs):
        acc = None
        for p in range(npairs):
            a = refs[2 * p][...].astype(BF16)
            b = refs[2 * p + 1][...].astype(BF16)
            d = lax.dot_general(a, b, _DIMS[mode], preferred_element_type=F32)
            acc = d if acc is None else acc + d
        if add is not None:
            acc = acc + refs[2 * npairs][...].astype(F32)
        o_refs = refs[len(refs) - len(outs):]
        if post is None:
            o_refs[0][...] = acc.astype(out_dtype)
        else:
            x0 = 2 * npairs + nadd
            post(acc, [r[...] for r in refs[x0:x0 + nrows]], [r[...] for r in refs[x0 + nrows:x0 + nrows + nfulls]], o_refs)

    tie_specs = [pl.BlockSpec((tm, r.shape[1]), lambda i, j: (i, 0)) for r in rows]
    tie_specs += [pl.BlockSpec((None,) + f.shape[1:], lambda i, j, ld=ld, nd=f.ndim - 1: (ld,) + (0,) * nd) for f, ld in fulls]
    tie_args = list(rows) + [f for f, _ in fulls]
    if tie is not None:
        tie_specs.append(pl.BlockSpec(memory_space=pl.ANY))
        tie_args.append(tie)

    in_specs, args = [], []
    for a, b in pairs:
        if mode == 'nn':
            k = _lshape(a)[1]
            in_specs += [_spec(a, tm, k, _ROW, _ZERO), _spec(b, k, tn, _ZERO, _COL)]
        elif mode == 'nt':
            k = _lshape(a)[1]
            in_specs += [_spec(a, tm, k, _ROW, _ZERO), _spec(b, tn, k, _COL, _ZERO)]
        else:
            k = _lshape(a)[0]
            in_specs += [_spec(a, k, tm, _ZERO, _ROW), _spec(b, k, tn, _ZERO, _COL)]
        args += [a.arr, b.arr]
    if add is not None:
        in_specs.append(pl.BlockSpec((tm, tn), lambda i, j: (i, j)))
        args.append(add)
    res = pl.pallas_call(
        body, name=name, grid=(m // tm, n // tn), in_specs=in_specs + tie_specs,
        out_specs=[pl.BlockSpec((tm, tn), lambda i, j: (i, j))] * len(outs),
        out_shape=[jax.ShapeDtypeStruct((m, n), dt) for dt in outs],
        compiler_params=_params(dimension_semantics=("arbitrary", "arbitrary")),
    )(*args, *tie_args)
    return res[0] if len(outs) == 1 else res


def rowwise(fn, rows, fulls, outs, accs, name, tm=256, into=None, tie=None):
    s = rows[0][0].shape[0]
    nrow, nfull, nout, nacc = len(rows), len(fulls), len(outs), len(accs)
    nin = nrow + nfull

    def body(*refs):
        ins = [r[...] for r in refs[:nin]]
        res = fn(*ins)
        if not isinstance(res, (tuple, list)):
            res = (res,)
        orefs = refs[nin + (1 if into is not None else 0) + (1 if tie is not None else 0):]
        for k in range(nout):
            orefs[k][...] = res[k].astype(orefs[k].dtype)
        if nacc:
            @pl.when(pl.program_id(0) == 0)
            def _():
                for k in range(nacc):
                    orefs[nout + k][...] = jnp.zeros_like(orefs[nout + k])

            for k in range(nacc):
                orefs[nout + k][...] += res[nout + k].astype(orefs[nout + k].dtype)

    in_specs = [pl.BlockSpec((tm, w), lambda i, cb=cb: (i, cb)) for _, w, cb in rows]
    in_specs += [pl.BlockSpec((None,) + f.shape[1:], lambda i, ld=ld, nd=f.ndim - 1: (ld,) + (0,) * nd) for f, ld in fulls]
    args = [r[0] for r in rows] + [f for f, _ in fulls]
    aliases = {}
    if into is not None:
        in_specs.append(pl.BlockSpec(memory_space=pl.ANY))
        args.append(into[0])
        aliases = {nin: into[1]}
    if tie is not None:
        in_specs.append(pl.BlockSpec(memory_space=pl.ANY))
        args.append(tie)
    out_specs, out_shape = [], []
    for o in outs:
        w, dt = o[0], o[1]
        total, cb = (o[2], o[3]) if len(o) == 4 else (w, 0)
        out_specs.append(pl.BlockSpec((tm, w), lambda i, cb=cb: (i, cb)))
        out_shape.append(jax.ShapeDtypeStruct((s, total), dt))
    for shp, dt in accs:
        out_specs.append(pl.BlockSpec(shp, lambda i, nd=len(shp): (0,) * nd))
        out_shape.append(jax.ShapeDtypeStruct(shp, dt))
    return pl.pallas_call(
        body, name=name, grid=(s // tm,), in_specs=in_specs, out_specs=out_specs, out_shape=out_shape,
        input_output_aliases=aliases, compiler_params=_params(dimension_semantics=("arbitrary",)),
    )(*args)


def _rms(x, g):
    xf = x.astype(F32)
    var = jnp.mean(xf * xf, axis=-1, keepdims=True)
    return xf * lax.rsqrt(var + EPS) * g


def rmsnorm_fwd(x, g, name, width=None, colblock=0, out=None, into=None, tie=None):
    w = width or x.shape[1]
    return rowwise(lambda xt, gt: _rms(xt, gt), [(x, w, colblock)], [g], [out or (w, BF16)], [], name, into=into,
                   tie=tie)[0]


def rmsnorm_bwd(x, g, dh, name, resid=None, width=None, colblock=0, dh_colblock=0, dx_dtype=F32):
    w = width or x.shape[1]

    def fn(xt, dht, *rest):
        gt = rest[-1]
        _, vjp = jax.vjp(_rms, xt.astype(F32), gt)
        dx, dg = vjp(dht.astype(F32))
        if resid is not None:
            dx = dx + rest[0]
        return dx, dg

    rows = [(x, w, colblock), (dh, w, dh_colblock)] + ([(resid, w, 0)] if resid is not None else [])
    return rowwise(fn, rows, [g], [(w, dx_dtype)], [((1, w), F32)], name)


CONV_R = 64
HALO = 8


def _ext_rows(ref, i, nchunk, above, below):
    r0 = pl.multiple_of(i * CONV_R, CONV_R)
    s = ref.shape[0]
    parts = []
    if above:
        top = ref[pl.ds(pl.multiple_of(jnp.maximum(r0 - HALO, 0), HALO), HALO), :].astype(F32)
        parts.append(jnp.where(i > 0, top, 0.0))
    parts.append(ref[pl.ds(r0, CONV_R), :].astype(F32))
    if below:
        tile = 2 * HALO if ref.dtype == BF16 else HALO
        bot = ref[pl.ds(pl.multiple_of(jnp.minimum(r0 + CONV_R, s - tile), tile), tile), :].astype(F32)[0:HALO]
        parts.append(jnp.where(i < nchunk - 1, bot, 0.0))
    return jnp.concatenate(parts, axis=0)


def _conv_ext(ext, w_ref, b_ref, kw):
    y = ext[HALO:] * w_ref[kw - 1:kw, :] + b_ref[...]
    for k in range(1, kw):
        y = y + pltpu.roll(ext, k, 0)[HALO:] * w_ref[kw - 1 - k:kw - k, :]
    return y


def _conv_t_ext(d, w_ref, kw):
    n = d.shape[0]
    y = d[:n - HALO] * w_ref[kw - 1:kw, :]
    for k in range(1, kw):
        y = y + pltpu.roll(d, n - k, 0)[:n - HALO] * w_ref[kw - 1 - k:kw - k, :]
    return y


def _conv_wgrad(dp, ext, kw):
    out = [jnp.sum(dp, axis=0, keepdims=True), jnp.sum(dp * ext[HALO:HALO + CONV_R], axis=0, keepdims=True)]
    for k in range(1, kw):
        out.append(jnp.sum(dp * pltpu.roll(ext, k, 0)[HALO:HALO + CONV_R], axis=0, keepdims=True))
    return out


def _store_wgrad(res, dw_ref, db_ref, kw):
    db_ref[...] = res[0]
    for k in range(kw):
        dw_ref[kw - 1 - k:kw - k, :] = res[1 + k]


def _silu(x):
    return x * jax.nn.sigmoid(x)


def _dsilu(x):
    s = jax.nn.sigmoid(x)
    return s * (1.0 + x * (1.0 - s))


SSM_TC = 256


def ssm_conv_fwd(proj, cw, cb, l):
    s = proj.shape[0]
    off = OFF_XBC // SSM_TC

    def body(u_ref, w_ref, b_ref, o_ref):
        nchunk = s // CONV_R

        def step(i, carry):
            ext = _ext_rows(u_ref, i, nchunk, True, False)
            o_ref[pl.ds(pl.multiple_of(i * CONV_R, CONV_R), CONV_R), :] = _silu(_conv_ext(ext, w_ref, b_ref, SSM_CONV))
            return carry

        lax.fori_loop(0, nchunk, step, 0)

    return pl.pallas_call(
        body, name="ssm_conv_fwd", grid=(CONV_CH // SSM_TC,),
        in_specs=[pl.BlockSpec((s, SSM_TC), lambda j: (0, off + j)),
                  pl.BlockSpec((None, SSM_CONV, SSM_TC), lambda j: (l, 0, j)),
                  pl.BlockSpec((None, 1, SSM_TC), lambda j: (l, 0, j))],
        out_specs=pl.BlockSpec((s, SSM_TC), lambda j: (0, j)),
        out_shape=jax.ShapeDtypeStruct((s, CONV_CH), F32),
        compiler_params=_params(dimension_semantics=("arbitrary",)),
    )(proj, cw, cb)


def ssm_conv_bwd(proj, cw, cb, l, dact):
    s = proj.shape[0]
    off = OFF_XBC // SSM_TC

    def body(u_ref, w_ref, b_ref, d_ref, du_ref, dw_ref, db_ref):
        nchunk = s // CONV_R

        def step(i, carry):
            ext = _ext_rows(u_ref, i, nchunk, True, True)
            dpre = _ext_rows(d_ref, i, nchunk, False, True) * _dsilu(_conv_ext(ext, w_ref, b_ref, SSM_CONV))
            du_ref[pl.ds(pl.multiple_of(i * CONV_R, CONV_R), CONV_R), :] = _conv_t_ext(dpre, w_ref, SSM_CONV).astype(du_ref.dtype)
            return tuple(c + g for c, g in zip(carry, _conv_wgrad(dpre[:CONV_R], ext, SSM_CONV)))

        zero = jnp.zeros((1, SSM_TC), F32)
        _store_wgrad(lax.fori_loop(0, nchunk, step, (zero,) * (SSM_CONV + 1)), dw_ref, db_ref, SSM_CONV)

    return pl.pallas_call(
        body, name="ssm_conv_bwd", grid=(CONV_CH // SSM_TC,),
        in_specs=[pl.BlockSpec((s, SSM_TC), lambda j: (0, off + j)),
                  pl.BlockSpec((None, SSM_CONV, SSM_TC), lambda j: (l, 0, j)),
                  pl.BlockSpec((None, 1, SSM_TC), lambda j: (l, 0, j)), pl.BlockSpec((s, SSM_TC), lambda j: (0, j))],
        out_specs=[pl.BlockSpec((s, SSM_TC), lambda j: (0, j)), pl.BlockSpec((SSM_CONV, SSM_TC), lambda j: (0, j)),
                   pl.BlockSpec((1, SSM_TC), lambda j: (0, j))],
        out_shape=[jax.ShapeDtypeStruct((s, CONV_CH), BF16), jax.ShapeDtypeStruct((SSM_CONV, CONV_CH), F32),
                   jax.ShapeDtypeStruct((1, CONV_CH), F32)],
        compiler_params=_params(dimension_semantics=("arbitrary",)),
    )(proj, cw, cb, dact)


FFN_TC = 256
FFN_NT = D_FF // FFN_TC


def _ffn_specs(s, l):
    blk = pl.BlockSpec((s, FFN_TC), lambda j: (0, j))
    wg = pl.BlockSpec((None, FFN_CONV, FFN_TC), lambda j: (l, 0, j))
    wv = pl.BlockSpec((None, FFN_CONV, FFN_TC), lambda j: (l, 0, FFN_NT + j))
    bg = pl.BlockSpec((None, 1, FFN_TC), lambda j: (l, 0, j))
    bv = pl.BlockSpec((None, 1, FFN_TC), lambda j: (l, 0, FFN_NT + j))
    return blk, wg, wv, bg, bv


def ffn_act_fwd(ug, uv, cw, cb, l):
    s = ug.shape[0]

    def body(g_ref, v_ref, wg_ref, wv_ref, bg_ref, bv_ref, o_ref):
        nchunk = s // CONV_R

        def step(i, carry):
            cg = _conv_ext(_ext_rows(g_ref, i, nchunk, True, False), wg_ref, bg_ref, FFN_CONV)
            cv = _conv_ext(_ext_rows(v_ref, i, nchunk, True, False), wv_ref, bv_ref, FFN_CONV)
            o_ref[pl.ds(pl.multiple_of(i * CONV_R, CONV_R), CONV_R), :] = (_silu(cg) * cv).astype(o_ref.dtype)
            return carry

        lax.fori_loop(0, nchunk, step, 0)

    blk, wg, wv, bg, bv = _ffn_specs(s, l)
    return pl.pallas_call(
        body, name="ffn_act_fwd", grid=(FFN_NT,), in_specs=[blk, blk, wg, wv, bg, bv],
        out_specs=blk, out_shape=jax.ShapeDtypeStruct((s, D_FF), BF16),
        compiler_params=_params(dimension_semantics=("arbitrary",)),
    )(ug, uv, cw, cw, cb, cb)


def ffn_act_bwd(ug, uv, cw, cb, l, da):
    s = ug.shape[0]

    def body(g_ref, v_ref, wg_ref, wv_ref, bg_ref, bv_ref, da_ref, dg_ref, dv_ref, dwg_ref, dwv_ref, dbg_ref, dbv_ref):
        nchunk = s // CONV_R

        def step(i, carry):
            rows = pl.ds(pl.multiple_of(i * CONV_R, CONV_R), CONV_R)
            eg = _ext_rows(g_ref, i, nchunk, True, True)
            ev = _ext_rows(v_ref, i, nchunk, True, True)
            cg = _conv_ext(eg, wg_ref, bg_ref, FFN_CONV)
            cv = _conv_ext(ev, wv_ref, bv_ref, FFN_CONV)
            da_t = _ext_rows(da_ref, i, nchunk, False, True)
            sg = jax.nn.sigmoid(cg)
            dcg = da_t * cv * (sg * (1.0 + cg * (1.0 - sg)))
            dcv = da_t * (cg * sg)
            dg_ref[rows, :] = _conv_t_ext(dcg, wg_ref, FFN_CONV).astype(dg_ref.dtype)
            dv_ref[rows, :] = _conv_t_ext(dcv, wv_ref, FFN_CONV).astype(dv_ref.dtype)
            grads = _conv_wgrad(dcg[:CONV_R], eg, FFN_CONV) + _conv_wgrad(dcv[:CONV_R], ev, FFN_CONV)
            return tuple(c + g for c, g in zip(carry, grads))

        zero = jnp.zeros((1, FFN_TC), F32)
        res = lax.fori_loop(0, nchunk, step, (zero,) * (2 * FFN_CONV + 2))
        _store_wgrad(res[:FFN_CONV + 1], dwg_ref, dbg_ref, FFN_CONV)
        _store_wgrad(res[FFN_CONV + 1:], dwv_ref, dbv_ref, FFN_CONV)

    blk, wg, wv, bg, bv = _ffn_specs(s, l)
    wblk = pl.BlockSpec((FFN_CONV, FFN_TC), lambda j: (0, j))
    bblk = pl.BlockSpec((1, FFN_TC), lambda j: (0, j))
    return pl.pallas_call(
        body, name="ffn_act_bwd", grid=(FFN_NT,), in_specs=[blk, blk, wg, wv, bg, bv, blk],
        out_specs=[blk, blk, wblk, wblk, bblk, bblk],
        out_shape=[jax.ShapeDtypeStruct((s, D_FF), BF16), jax.ShapeDtypeStruct((s, D_FF), BF16),
                   jax.ShapeDtypeStruct((FFN_CONV, D_FF), F32), jax.ShapeDtypeStruct((FFN_CONV, D_FF), F32),
                   jax.ShapeDtypeStruct((1, D_FF), F32), jax.ShapeDtypeStruct((1, D_FF), F32)],
        compiler_params=_params(dimension_semantics=("arbitrary",)),
    )(ug, uv, cw, cw, cb, cb, da)


def _dot(a, b, mode):
    return lax.dot_general(a.astype(BF16), b.astype(BF16), _DIMS[mode], preferred_element_type=F32)


@jax.custom_vjp
def mm_nn(a, b):
    return _dot(a, b, 'nn')


@jax.custom_vjp
def mm_nt(a, b):
    return _dot(a, b, 'nt')


@jax.custom_vjp
def mm_tn(a, b):
    return _dot(a, b, 'tn')


mm_nn.defvjp(lambda a, b: (_dot(a, b, 'nn'), (a, b)), lambda r, g: (_dot(g, r[1], 'nt'), _dot(r[0], g, 'tn')))
mm_nt.defvjp(lambda a, b: (_dot(a, b, 'nt'), (a, b)), lambda r, g: (_dot(g, r[1], 'nn'), _dot(g, r[0], 'tn')))
mm_tn.defvjp(lambda a, b: (_dot(a, b, 'tn'), (a, b)), lambda r, g: (_dot(r[1], g, 'nt'), _dot(r[0], g, 'nn')))


def _tri(n, lower):
    r = lax.broadcasted_iota(jnp.int32, (n, n), 0)
    c = lax.broadcasted_iota(jnp.int32, (n, n), 1)
    return jnp.where((r >= c) if lower else (r <= c), 1.0, 0.0).astype(F32)


def _tri_dot(a, lower):
    return jnp.dot(_tri(a.shape[0], lower), a, precision=lax.Precision.HIGHEST, preferred_element_type=F32)


@jax.custom_vjp
def _cumsum_rows(a):
    return _tri_dot(a, True)


_cumsum_rows.defvjp(lambda a: (_tri_dot(a, True), None), lambda _, g: (_tri_dot(g, False),))


def _softplus(x):
    return jnp.maximum(x, 0.0) + jnp.log(1.0 + jnp.exp(-jnp.abs(x)))


def _ssd_chunk(xs, bs, cs, small, dtb, alog, dsk, prev):
    ln = small.shape[0]
    lane = lax.broadcasted_iota(jnp.int32, (ln, LANES), 1)
    lane1 = lax.broadcasted_iota(jnp.int32, (1, LANES), 1)
    sub = lax.broadcasted_iota(jnp.int32, (LANES, ln), 0)
    rowi = lax.broadcasted_iota(jnp.int32, (ln, LANES), 0)
    tril = lax.broadcasted_iota(jnp.int32, (ln, ln), 0) >= lax.broadcasted_iota(jnp.int32, (ln, ln), 1)
    first = lane < SSM_HEAD_DIM
    first1 = lane1 < SSM_HEAD_DIM

    dt = _softplus(small + dtb)
    acs = _cumsum_rows(dt * (-jnp.exp(alog)))
    acs_t = acs.T
    last = jnp.sum(jnp.where(rowi == ln - 1, acs, 0.0), axis=0, keepdims=True)

    def col(a, h):
        return jnp.sum(jnp.where(lane == h, a, 0.0), axis=1, keepdims=True)

    def one(a, h):
        return jnp.sum(jnp.where(lane1 == h, a, 0.0), axis=1, keepdims=True)

    def rowv(at, h):
        return jnp.sum(jnp.where(sub == h, at, 0.0), axis=0, keepdims=True)

    cb = [mm_nt(cs[g], bs[g]) for g in range(SSM_GROUPS)]
    ys, news = [], []
    for j in range(SSM_HEADS // 2):
        g = j // 2
        h0, h1 = 2 * j, 2 * j + 1
        xd = xs[j] * jnp.where(first, col(dt, h0), col(dt, h1))
        yd, st, ea, cd = None, None, [], []
        for h, xdh in ((h0, jnp.where(first, xd, 0.0)), (h1, jnp.where(first, 0.0, xd))):
            ac = col(acs, h)
            la = one(last, h)
            lmat = jnp.exp(jnp.where(tril, ac - rowv(acs_t, h), -jnp.inf))
            yh = mm_nn(cb[g] * lmat, xdh)
            sh = mm_tn(bs[g] * jnp.exp(la - ac), xdh)
            yd = yh if yd is None else yd + yh
            st = sh if st is None else st + sh
            ea.append(jnp.exp(ac))
            cd.append(jnp.exp(la))
        yoff = mm_nn(cs[g], prev[j]) * jnp.where(first, ea[0], ea[1])
        ys.append(yd + yoff + xs[j] * jnp.where(first1, one(dsk, h0), one(dsk, h1)))
        news.append(prev[j] * jnp.where(first1, cd[0], cd[1]) + st)
    return ys, news


N_PAIR = SSM_HEADS // 2


def ssd_fwd(xbc, proj, ptile, l):
    s = xbc.shape[0]
    nch = s // SSM_CHUNK

    def body(xbc_ref, small_ref, p_ref, y_ref, prev_ref, state_ref):
        @pl.when(pl.program_id(0) == 0)
        def _():
            state_ref[...] = jnp.zeros_like(state_ref)

        xs = [xbc_ref[:, LANES * j:LANES * (j + 1)] for j in range(N_PAIR)]
        bs = [xbc_ref[:, D_SSM + LANES * g:D_SSM + LANES * (g + 1)] for g in range(SSM_GROUPS)]
        cs = [xbc_ref[:, D_SSM + 512 + LANES * g:D_SSM + 512 + LANES * (g + 1)] for g in range(SSM_GROUPS)]
        prev = [state_ref[j] for j in range(N_PAIR)]
        ys, news = _ssd_chunk(xs, bs, cs, small_ref[...], p_ref[0:1, :], p_ref[1:2, :], p_ref[2:3, :], prev)
        for j in range(N_PAIR):
            y_ref[:, LANES * j:LANES * (j + 1)] = ys[j]
            prev_ref[0, j] = prev[j]
            state_ref[j] = news[j]

    return pl.pallas_call(
        body, name="ssd_fwd", grid=(nch,),
        in_specs=[pl.BlockSpec((SSM_CHUNK, CONV_CH), lambda c: (c, 0)),
                  pl.BlockSpec((SSM_CHUNK, LANES), lambda c: (c, OFF_SMALL // LANES)),
                  pl.BlockSpec((None, 8, LANES), lambda c: (l, 0, 0))],
        out_specs=[pl.BlockSpec((SSM_CHUNK, D_SSM), lambda c: (c, 0)),
                   pl.BlockSpec((1, N_PAIR, SSM_STATE, LANES), lambda c: (c, 0, 0, 0))],
        out_shape=[jax.ShapeDtypeStruct((s, D_SSM), F32), jax.ShapeDtypeStruct((nch, N_PAIR, SSM_STATE, LANES), F32)],
        scratch_shapes=[pltpu.VMEM((N_PAIR, SSM_STATE, LANES), F32)],
        compiler_params=_params(dimension_semantics=("arbitrary",)),
    )(xbc, proj, ptile)


def ssd_bwd(xbc, proj, ptile, l, prevs, dy):
    s = xbc.shape[0]
    nch = s // SSM_CHUNK

    def body(xbc_ref, small_ref, p_ref, prev_ref, dy_ref, dxbc_ref, dsmall_ref, dp_ref, dstate_ref):
        @pl.when(pl.program_id(0) == 0)
        def _():
            dstate_ref[...] = jnp.zeros_like(dstate_ref)
            dp_ref[...] = jnp.zeros_like(dp_ref)

        xs = [xbc_ref[:, LANES * j:LANES * (j + 1)] for j in range(N_PAIR)]
        bs = [xbc_ref[:, D_SSM + LANES * g:D_SSM + LANES * (g + 1)] for g in range(SSM_GROUPS)]
        cs = [xbc_ref[:, D_SSM + 512 + LANES * g:D_SSM + 512 + LANES * (g + 1)] for g in range(SSM_GROUPS)]
        prev = [prev_ref[0, j] for j in range(N_PAIR)]
        dys = [dy_ref[:, LANES * j:LANES * (j + 1)] for j in range(N_PAIR)]
        dnew = [dstate_ref[j] for j in range(N_PAIR)]
        _, vjp = jax.vjp(_ssd_chunk, xs, bs, cs, small_ref[...], p_ref[0:1, :], p_ref[1:2, :], p_ref[2:3, :], prev)
        dxs, dbs, dcs, dsmall, ddtb, dalog, ddsk, dprev = vjp((dys, dnew))
        for j in range(N_PAIR):
            dxbc_ref[:, LANES * j:LANES * (j + 1)] = dxs[j]
            dstate_ref[j] = dprev[j]
        for g in range(SSM_GROUPS):
            dxbc_ref[:, D_SSM + LANES * g:D_SSM + LANES * (g + 1)] = dbs[g]
            dxbc_ref[:, D_SSM + 512 + LANES * g:D_SSM + 512 + LANES * (g + 1)] = dcs[g]
        dsmall_ref[...] = dsmall
        dp_ref[0:1, :] += ddtb
        dp_ref[1:2, :] += dalog
        dp_ref[2:3, :] += ddsk

    rev = lambda c: nch - 1 - c
    return pl.pallas_call(
        body, name="ssd_bwd", grid=(nch,),
        in_specs=[pl.BlockSpec((SSM_CHUNK, CONV_CH), lambda c: (rev(c), 0)),
                  pl.BlockSpec((SSM_CHUNK, LANES), lambda c: (rev(c), OFF_SMALL // LANES)),
                  pl.BlockSpec((None, 8, LANES), lambda c: (l, 0, 0)),
                  pl.BlockSpec((1, N_PAIR, SSM_STATE, LANES), lambda c: (rev(c), 0, 0, 0)),
                  pl.BlockSpec((SSM_CHUNK, D_SSM), lambda c: (rev(c), 0))],
        out_specs=[pl.BlockSpec((SSM_CHUNK, CONV_CH), lambda c: (rev(c), 0)),
                   pl.BlockSpec((SSM_CHUNK, LANES), lambda c: (rev(c), 0)),
                   pl.BlockSpec((8, LANES), lambda c: (0, 0))],
        out_shape=[jax.ShapeDtypeStruct((s, CONV_CH), F32), jax.ShapeDtypeStruct((s, LANES), F32),
                   jax.ShapeDtypeStruct((8, LANES), F32)],
        scratch_shapes=[pltpu.VMEM((N_PAIR, SSM_STATE, LANES), F32)],
        compiler_params=_params(dimension_semantics=("arbitrary",)),
    )(xbc, proj, ptile, prevs, dy)


ROPE_TM = 256


def _rope_tile(t, cosm, sinm):
    lane = lax.broadcasted_iota(jnp.int32, t.shape, 1)
    half = QK_ROPE // 2
    partner = jnp.where(lane < ROPE_LANE0 + half, pltpu.roll(t, LANES - half, 1), pltpu.roll(t, half, 1))
    return t * cosm + partner * sinm


def _in_rope(shape):
    lane = lax.broadcasted_iota(jnp.int32, shape, 1)
    return jnp.logical_and(lane >= ROPE_LANE0, lane < ROPE_LANE0 + QK_ROPE)


def build_k(kn, proj, cosm, sinm):
    s, w = kn.shape

    def body(k_ref, small_ref, c_ref, s_ref, o_ref):
        small = small_ref[...]
        inrope = _in_rope(small.shape)
        kpe = jnp.where(inrope, _rope_tile(jnp.where(inrope, small, 0.0), c_ref[...], s_ref[...]), 0.0)
        for h in range(MLA_HEADS):
            sl = slice(HEAD_PAD * h, HEAD_PAD * (h + 1))
            o_ref[:, sl] = (k_ref[:, sl].astype(F32) + kpe).astype(o_ref.dtype)

    row = pl.BlockSpec((ROPE_TM, w), lambda i: (i, 0))
    tab = pl.BlockSpec((ROPE_TM, LANES), lambda i: (i, 0))
    return pl.pallas_call(
        body, name="build_k", grid=(s // ROPE_TM,),
        in_specs=[row, pl.BlockSpec((ROPE_TM, LANES), lambda i: (i, OFF_SMALL // LANES)), tab, tab], out_specs=row,
        out_shape=jax.ShapeDtypeStruct((s, w), BF16), compiler_params=_params(dimension_semantics=("arbitrary",)),
    )(kn, proj, cosm, sinm)


def dsmall_bwd(dk, dsmall_ssd, cosm, sinm_neg):
    def fn(dkt, ds, c, sn):
        inrope = _in_rope(ds.shape)
        tot = dkt[:, 0:HEAD_PAD]
        for h in range(1, MLA_HEADS):
            tot = tot + dkt[:, HEAD_PAD * h:HEAD_PAD * (h + 1)]
        tot = jnp.where(inrope, tot, 0.0)
        return ds + jnp.where(inrope, _rope_tile(tot, c, sn), 0.0)

    return rowwise(fn, [(dk, MLA_HEADS * HEAD_PAD, 0), (dsmall_ssd, LANES, 0), (cosm, LANES, 0), (sinm_neg, LANES, 0)],
                   [], [(LANES, BF16)], [], "dsmall_bwd")[0]


ATT_TQ = 512
ATT_SCALE = (QK_NOPE + QK_ROPE) ** -0.5


def _att_scores(qh, kh, q0):
    s = lax.dot_general(qh, kh, _DIMS['nt'], preferred_element_type=F32) * ATT_SCALE
    r = lax.broadcasted_iota(jnp.int32, s.shape, 0) + q0
    c = lax.broadcasted_iota(jnp.int32, s.shape, 1)
    return jnp.where(c <= r, s, -1e30)


def mla_fwd(q, k, v):
    s = q.shape[0]

    def body(q_ref, k_ref, v_ref, o_ref, lse_ref):
        lane = lax.broadcasted_iota(jnp.int32, (ATT_TQ, LANES), 1)

        def block(ib):
            n = ATT_TQ * (ib + 1)
            v_t = v_ref[0:n, :]
            vlane = lax.broadcasted_iota(jnp.int32, v_t.shape, 1)
            o_tot, lse_tot = None, None
            for h in range(2):
                hs = slice(HEAD_PAD * h, HEAD_PAD * (h + 1))
                sc = _att_scores(q_ref[:, hs], k_ref[0:n, hs], ATT_TQ * ib)
                m = jnp.max(sc, axis=1, keepdims=True)
                p = jnp.exp(sc - m)
                l = jnp.sum(p, axis=1, keepdims=True)
                vh = jnp.where((vlane < V_DIM) if h == 0 else (vlane >= V_DIM), v_t, jnp.zeros_like(v_t))
                oh = lax.dot_general(p.astype(BF16), vh, _DIMS['nn'], preferred_element_type=F32) / l
                lse_h = jnp.where((lane < V_DIM) if h == 0 else (lane >= V_DIM), m + jnp.log(l), 0.0)
                o_tot = oh if o_tot is None else o_tot + oh
                lse_tot = lse_h if lse_tot is None else lse_tot + lse_h
            o_ref[...] = o_tot
            lse_ref[...] = lse_tot

        for ib in range(s // ATT_TQ):
            pl.when(pl.program_id(1) == ib)(functools.partial(block, ib))

    tile = pl.BlockSpec((ATT_TQ, LANES), lambda p, i: (i, p))
    return pl.pallas_call(
        body, name="mla_fwd", grid=(MLA_HEADS // 2, s // ATT_TQ),
        in_specs=[pl.BlockSpec((ATT_TQ, 2 * HEAD_PAD), lambda p, i: (i, p)),
                  pl.BlockSpec((s, 2 * HEAD_PAD), lambda p, i: (0, p)),
                  pl.BlockSpec((s, LANES), lambda p, i: (0, p))],
        out_specs=[tile, tile],
        out_shape=[jax.ShapeDtypeStruct((s, MLA_HEADS * V_DIM), F32)] * 2,
        compiler_params=_params(dimension_semantics=("arbitrary", "arbitrary")),
    )(q, k, v)


def mla_bwd(q, k, v, o, lse, do, cosm, sinm_neg):
    s = q.shape[0]

    def body(q_ref, k_ref, v_ref, o_ref, lse_ref, do_ref, c_ref, s_ref, dq_ref, dk_ref, dv_ref):
        i = pl.program_id(1)

        @pl.when(i == 0)
        def _():
            dk_ref[...] = jnp.zeros_like(dk_ref)
            dv_ref[...] = jnp.zeros_like(dv_ref)

        def block(ib):
            n = ATT_TQ * (ib + 1)
            o_t = o_ref[...]
            do_t = do_ref[...]
            lse_t = lse_ref[...]
            v_t = v_ref[0:n, :]
            lane = lax.broadcasted_iota(jnp.int32, do_t.shape, 1)
            for h in range(2):
                hs = slice(HEAD_PAD * h, HEAD_PAD * (h + 1))
                sel = (lane < V_DIM) if h == 0 else (lane >= V_DIM)
                qh = q_ref[:, hs]
                kh = k_ref[0:n, hs]
                doh = jnp.where(sel, do_t, 0.0)
                delta = jnp.sum(doh * o_t, axis=1, keepdims=True)
                lse_h = jnp.max(jnp.where(sel, lse_t, -jnp.inf), axis=1, keepdims=True)
                doh_b = doh.astype(BF16)
                p = jnp.exp(_att_scores(qh, kh, ATT_TQ * ib) - lse_h)
                dv_ref[0:n, :] += lax.dot_general(p.astype(BF16), doh_b, _DIMS['tn'], preferred_element_type=F32)
                dp = lax.dot_general(doh_b, v_t, _DIMS['nt'], preferred_element_type=F32)
                ds = (p * (dp - delta) * ATT_SCALE).astype(BF16)
                dk_ref[0:n, hs] += lax.dot_general(ds, qh, _DIMS['tn'], preferred_element_type=F32)
                dq = lax.dot_general(ds, kh, _DIMS['nn'], preferred_element_type=F32)
                dq_ref[:, hs] = _rope_tile(dq, c_ref[...], s_ref[...]).astype(dq_ref.dtype)

        for ib in range(s // ATT_TQ):
            pl.when(i == ib)(functools.partial(block, ib))

    tile = pl.BlockSpec((ATT_TQ, LANES), lambda p, i: (i, p))
    return pl.pallas_call(
        body, name="mla_bwd", grid=(MLA_HEADS // 2, s // ATT_TQ),
        in_specs=[pl.BlockSpec((ATT_TQ, 2 * HEAD_PAD), lambda p, i: (i, p)),
                  pl.BlockSpec((s, 2 * HEAD_PAD), lambda p, i: (0, p)),
                  pl.BlockSpec((s, LANES), lambda p, i: (0, p)), tile, tile, tile,
                  pl.BlockSpec((ATT_TQ, LANES), lambda p, i: (i, 0)), pl.BlockSpec((ATT_TQ, LANES), lambda p, i: (i, 0))],
        out_specs=[pl.BlockSpec((ATT_TQ, 2 * HEAD_PAD), lambda p, i: (i, p)),
                   pl.BlockSpec((s, 2 * HEAD_PAD), lambda p, i: (0, p)),
                   pl.BlockSpec((s, LANES), lambda p, i: (0, p))],
        out_shape=[jax.ShapeDtypeStruct((s, MLA_HEADS * HEAD_PAD), BF16),
                   jax.ShapeDtypeStruct((s, MLA_HEADS * HEAD_PAD), F32),
                   jax.ShapeDtypeStruct((s, MLA_HEADS * V_DIM), F32)],
        compiler_params=_params(dimension_semantics=("arbitrary", "arbitrary")),
    )(q, k, v, o, lse, do, cosm, sinm_neg)


MEM_TQ = 256
MEM_SCALE = MEM_HEAD_DIM ** -0.5


def _mem_probs(qh, kh):
    s = lax.dot_general(qh, kh, _DIMS['nt'], preferred_element_type=F32) * MEM_SCALE
    p = jnp.exp(s - jnp.max(s, axis=1, keepdims=True))
    return p / jnp.sum(p, axis=1, keepdims=True)


def mem_fwd(q, k, v):
    s = q.shape[0]

    def body(q_ref, k_ref, v_ref, o_ref):
        for h in range(MEM_HEADS):
            sl = slice(MEM_HEAD_DIM * h, MEM_HEAD_DIM * (h + 1))
            p = _mem_probs(q_ref[:, sl], k_ref[:, sl])
            o_ref[:, sl] = lax.dot_general(p.astype(BF16), v_ref[:, sl], _DIMS['nn'],
                                           preferred_element_type=F32).astype(o_ref.dtype)

    full = pl.BlockSpec((MEM_LEN, D_MODEL), lambda i: (0, 0))
    return pl.pallas_call(
        body, name="mem_fwd", grid=(s // MEM_TQ,),
        in_specs=[pl.BlockSpec((MEM_TQ, D_MODEL), lambda i: (i, 0)), full, full],
        out_specs=pl.BlockSpec((MEM_TQ, D_MODEL), lambda i: (i, 0)),
        out_shape=jax.ShapeDtypeStruct((s, D_MODEL), BF16),
        compiler_params=_params(dimension_semantics=("arbitrary",)),
    )(q, k, v)


def mem_bwd(q, k, v, do):
    s = q.shape[0]

    def body(q_ref, k_ref, v_ref, do_ref, dq_ref, dk_ref, dv_ref):
        @pl.when(pl.program_id(0) == 0)
        def _():
            dk_ref[...] = jnp.zeros_like(dk_ref)
            dv_ref[...] = jnp.zeros_like(dv_ref)

        for h in range(MEM_HEADS):
            sl = slice(MEM_HEAD_DIM * h, MEM_HEAD_DIM * (h + 1))
            qh, kh, vh = q_ref[:, sl], k_ref[:, sl], v_ref[:, sl]
            doh = do_ref[:, sl].astype(BF16)
            p = _mem_probs(qh, kh)
            dv_ref[:, sl] += lax.dot_general(p.astype(BF16), doh, _DIMS['tn'], preferred_element_type=F32)
            dp = lax.dot_general(doh, vh, _DIMS['nt'], preferred_element_type=F32)
            ds = (p * (dp - jnp.sum(p * dp, axis=1, keepdims=True)) * MEM_SCALE).astype(BF16)
            dq_ref[:, sl] = lax.dot_general(ds, kh, _DIMS['nn'], preferred_element_type=F32).astype(dq_ref.dtype)
            dk_ref[:, sl] += lax.dot_general(ds, qh, _DIMS['tn'], preferred_element_type=F32)

    full = pl.BlockSpec((MEM_LEN, D_MODEL), lambda i: (0, 0))
    row = pl.BlockSpec((MEM_TQ, D_MODEL), lambda i: (i, 0))
    return pl.pallas_call(
        body, name="mem_bwd", grid=(s // MEM_TQ,),
        in_specs=[row, full, full, row], out_specs=[row, full, full],
        out_shape=[jax.ShapeDtypeStruct((s, D_MODEL), BF16), jax.ShapeDtypeStruct((MEM_LEN, D_MODEL), F32),
                   jax.ShapeDtypeStruct((MEM_LEN, D_MODEL), F32)],
        compiler_params=_params(dimension_semantics=("arbitrary",)),
    )(q, k, v, do)


def _gate_norm(y, z, g):
    return _rms(y * _silu(z), g)


def gate_norm_fwd(y, proj, g):
    return rowwise(_gate_norm, [(y, D_SSM, 0), (proj, D_SSM, OFF_Z // D_SSM)], [g], [(D_SSM, BF16, D_MIX, 0)], [],
                   "gate_norm_fwd")[0]


def gate_norm_bwd(y, proj, g, dmix, tie=None):
    def fn(yt, zt, dt_, gt):
        _, vjp = jax.vjp(_gate_norm, yt, zt, gt)
        return vjp(dt_.astype(F32))

    return rowwise(fn, [(y, D_SSM, 0), (proj, D_SSM, OFF_Z // D_SSM), (dmix, D_SSM, 0)], [g],
                   [(D_SSM, F32), (D_SSM, BF16)], [((1, D_SSM), F32)], "gate_norm_bwd", tie=tie)


def loss_head(x, g, target):
    def fn(xt, tt, gt):
        def f(x_, g_):
            err = _rms(x_, g_) - tt
            return 0.5 * jnp.sum(jnp.mean(err * err, axis=-1))

        lv, (dx, dg) = jax.value_and_grad(f, argnums=(0, 1))(xt, gt)
        return dx, dg, jnp.full((1, LANES), lv, F32)

    return rowwise(fn, [(x, D_MODEL, 0), (target, D_MODEL, 0)], [g], [(D_MODEL, F32)],
                   [((1, D_MODEL), F32), ((1, LANES), F32)], "loss_head")


def _proj_runs(d):
    lo, hi = (D_IN // N_DEV) * d, (D_IN // N_DEV) * (d + 1)
    runs = []
    for a, b, new in PROJ_SEGS:
        s0, s1 = max(a, lo), min(b, hi)
        if s0 < s1:
            runs.append((s0 - lo, new + s0 - a, s1 - s0))
    return runs


LAYOUT_TM = 256


def assemble_proj(g):
    def body(g_ref, o_ref):
        o_ref[:, OFF_SMALL:OFF_SMALL + LANES] = jnp.zeros((LAYOUT_TM, LANES), o_ref.dtype)
        for d in range(N_DEV):
            for src, dst, n in _proj_runs(d):
                o_ref[:, dst:dst + n] = g_ref[d, :, src:src + n]

    return pl.pallas_call(
        body, name="assemble_proj", grid=(D_MODEL // LAYOUT_TM,),
        in_specs=[pl.BlockSpec((N_DEV, LAYOUT_TM, D_IN // N_DEV), lambda i: (0, i, 0))],
        out_specs=pl.BlockSpec((LAYOUT_TM, PROJ_W), lambda i: (i, 0)),
        out_shape=jax.ShapeDtypeStruct((D_MODEL, PROJ_W), g.dtype),
        compiler_params=_params(dimension_semantics=("arbitrary",)),
    )(g)


def extract_proj(dz, dxbc, dcq, dsmall, dckv):
    pieces = [(OFF_Z, 1024), (OFF_XBC, 2048), (OFF_CQ, Q_LORA), (OFF_SMALL, LANES), (OFF_CKV, KV_LORA)]

    def body(*refs):
        o_ref = refs[-1]
        for d in range(N_DEV):
            for src, dst, n in _proj_runs(d):
                for p, (off, w) in enumerate(pieces):
                    if off <= dst < off + w:
                        o_ref[d, :, src:src + n] = refs[p][:, dst - off:dst - off + n].astype(o_ref.dtype)

    return pl.pallas_call(
        body, name="extract_proj", grid=(D_MODEL // LAYOUT_TM,),
        in_specs=[pl.BlockSpec((LAYOUT_TM, w), lambda i: (i, 0)) for _, w in pieces],
        out_specs=pl.BlockSpec((N_DEV, LAYOUT_TM, D_IN // N_DEV), lambda i: (0, i, 0)),
        out_shape=jax.ShapeDtypeStruct((N_DEV, D_MODEL, D_IN // N_DEV), BF16),
        compiler_params=_params(dimension_semantics=("arbitrary",)),
    )(dz, dxbc, dcq, dsmall, dckv)


_QW = QK_NOPE + QK_ROPE


def assemble_uq(g):
    def body(g_ref, o_ref):
        o_ref[...] = jnp.zeros_like(o_ref)
        for d in range(N_DEV):
            for e in range(2):
                dst = HEAD_PAD * (2 * d + e)
                o_ref[:, dst:dst + _QW] = g_ref[d, :, _QW * e:_QW * (e + 1)]

    return pl.pallas_call(
        body, name="assemble_uq", grid=(1,),
        in_specs=[pl.BlockSpec((N_DEV, Q_LORA, 2 * _QW), lambda i: (0, 0, 0))],
        out_specs=pl.BlockSpec((Q_LORA, MLA_HEADS * HEAD_PAD), lambda i: (0, 0)),
        out_shape=jax.ShapeDtypeStruct((Q_LORA, MLA_HEADS * HEAD_PAD), g.dtype),
        compiler_params=_params(dimension_semantics=("arbitrary",)),
    )(g)


def extract_uq(dw):
    def body(w_ref, o_ref):
        for d in range(N_DEV):
            for e in range(2):
                src = HEAD_PAD * (2 * d + e)
                o_ref[d, :, _QW * e:_QW * (e + 1)] = w_ref[:, src:src + _QW].astype(o_ref.dtype)

    return pl.pallas_call(
        body, name="extract_uq", grid=(1,),
        in_specs=[pl.BlockSpec((Q_LORA, MLA_HEADS * HEAD_PAD), lambda i: (0, 0))],
        out_specs=pl.BlockSpec((N_DEV, Q_LORA, 2 * _QW), lambda i: (0, 0, 0)),
        out_shape=jax.ShapeDtypeStruct((N_DEV, Q_LORA, 2 * _QW), BF16),
        compiler_params=_params(dimension_semantics=("arbitrary",)),
    )(dw)


def assemble_ukv(g):
    def body(g_ref, kn_ref, v_ref):
        kn_ref[...] = jnp.zeros_like(kn_ref)
        for d in range(N_DEV):
            for e in range(2):
                h = 2 * d + e
                kn_ref[:, HEAD_PAD * h:HEAD_PAD * h + QK_NOPE] = g_ref[d, :, 128 * e:128 * e + QK_NOPE]
                v_ref[:, V_DIM * h:V_DIM * (h + 1)] = g_ref[d, :, 128 * e + QK_NOPE:128 * (e + 1)]

    return pl.pallas_call(
        body, name="assemble_ukv", grid=(1,),
        in_specs=[pl.BlockSpec((N_DEV, KV_LORA, 256), lambda i: (0, 0, 0))],
        out_specs=[pl.BlockSpec((KV_LORA, MLA_HEADS * HEAD_PAD), lambda i: (0, 0)),
                   pl.BlockSpec((KV_LORA, MLA_HEADS * V_DIM), lambda i: (0, 0))],
        out_shape=[jax.ShapeDtypeStruct((KV_LORA, MLA_HEADS * HEAD_PAD), g.dtype),
                   jax.ShapeDtypeStruct((KV_LORA, MLA_HEADS * V_DIM), g.dtype)],
        compiler_params=_params(dimension_semantics=("arbitrary",)),
    )(g)


def extract_ukv(dkn, dv):
    def body(kn_ref, v_ref, o_ref):
        for d in range(N_DEV):
            for e in range(2):
                h = 2 * d + e
                o_ref[d, :, 128 * e:128 * e + QK_NOPE] = kn_ref[:, HEAD_PAD * h:HEAD_PAD * h + QK_NOPE].astype(o_ref.dtype)
                o_ref[d, :, 128 * e + QK_NOPE:128 * (e + 1)] = v_ref[:, V_DIM * h:V_DIM * (h + 1)].astype(o_ref.dtype)

    return pl.pallas_call(
        body, name="extract_ukv", grid=(1,),
        in_specs=[pl.BlockSpec((KV_LORA, MLA_HEADS * HEAD_PAD), lambda i: (0, 0)),
                  pl.BlockSpec((KV_LORA, MLA_HEADS * V_DIM), lambda i: (0, 0))],
        out_specs=pl.BlockSpec((N_DEV, KV_LORA, 256), lambda i: (0, 0, 0)),
        out_shape=jax.ShapeDtypeStruct((N_DEV, KV_LORA, 256), BF16),
        compiler_params=_params(dimension_semantics=("arbitrary",)),
    )(dkn, dv)


_UPW = 2 * D_FF // N_DEV


def assemble_up(g):
    def body(g_ref, wg_ref, wv_ref):
        for d in range(N_DEV):
            ref = wg_ref if d < N_DEV // 2 else wv_ref
            off = _UPW * (d % (N_DEV // 2))
            ref[:, off:off + _UPW] = g_ref[d]

    half = pl.BlockSpec((LAYOUT_TM, D_FF), lambda i: (i, 0))
    return pl.pallas_call(
        body, name="assemble_up", grid=(D_MODEL // LAYOUT_TM,),
        in_specs=[pl.BlockSpec((N_DEV, LAYOUT_TM, _UPW), lambda i: (0, i, 0))],
        out_specs=[half, half], out_shape=[jax.ShapeDtypeStruct((D_MODEL, D_FF), g.dtype)] * 2,
        compiler_params=_params(dimension_semantics=("arbitrary",)),
    )(g)


def extract_up(dwg, dwv):
    def body(wg_ref, wv_ref, o_ref):
        for d in range(N_DEV):
            ref = wg_ref if d < N_DEV // 2 else wv_ref
            off = _UPW * (d % (N_DEV // 2))
            o_ref[d] = ref[:, off:off + _UPW].astype(o_ref.dtype)

    half = pl.BlockSpec((LAYOUT_TM, D_FF), lambda i: (i, 0))
    return pl.pallas_call(
        body, name="extract_up", grid=(D_MODEL // LAYOUT_TM,), in_specs=[half, half],
        out_specs=pl.BlockSpec((N_DEV, LAYOUT_TM, _UPW), lambda i: (0, i, 0)),
        out_shape=jax.ShapeDtypeStruct((N_DEV, D_MODEL, _UPW), BF16),
        compiler_params=_params(dimension_semantics=("arbitrary",)),
    )(dwg, dwv)


MESH = pl.DeviceIdType.MESH
ANY = pl.BlockSpec(memory_space=pl.ANY)


def _place():
    mx, my, mc = lax.axis_index("x"), lax.axis_index("y"), lax.axis_index("c")
    return mx, my, mc, [(1 - mx, my), (mx, 1 - my), (1 - mx, 1 - my)]


def all_gather_blocks(xs, first_only=()):
    n = len(xs)

    def body(*refs):
        x_refs, out_refs = refs[:n], refs[n:2 * n]
        send_sems, recv_sems, local_sems = refs[2 * n:]
        mx, my, mc, chips = _place()
        me, sibling = (mx, my, mc), (mx, my, 1 - mc)
        x_refs = [x_refs[t].at[0] if t in first_only else x_refs[t] for t in range(n)]

        def rows(t, px, py, pc):
            dev = 4 * px + 2 * py + pc
            return out_refs[t].at[dev] if t in first_only else out_refs[t].at[:, dev]

        def copy(t, k, block, to, src=None):
            return pltpu.make_async_remote_copy(
                src_ref=rows(t, *block) if src is None else src, dst_ref=rows(t, *block),
                send_sem=send_sems.at[t, k], recv_sem=recv_sems.at[t, k], device_id=to, device_id_type=MESH)

        mine = [pltpu.make_async_copy(x_refs[t], rows(t, *me), local_sems.at[t]) for t in range(n)]
        for cp in mine:
            cp.start()
        first = []
        for t in range(n):
            first.append(copy(t, 0, me, sibling, src=x_refs[t]))
            first += [copy(t, 1 + j, me, (*chip, mc), src=x_refs[t]) for j, chip in enumerate(chips)]
        for cp in first:
            cp.start()
        passed = []
        for j, chip in enumerate(chips):
            for t in range(n):
                copy(t, 1 + j, (*chip, mc), me).wait_recv()
                cp = copy(t, 4 + j, (*chip, mc), sibling)
                cp.start()
                passed.append(cp)
        for t in range(n):
            copy(t, 0, sibling, me).wait_recv()
            for j, chip in enumerate(chips):
                copy(t, 4 + j, (*chip, 1 - mc), me).wait_recv()
        for cp in first + passed:
            cp.wait_send()
        for cp in mine:
            cp.wait()

    return pl.pallas_call(
        body, name="all_gather_blocks",
        out_shape=[jax.ShapeDtypeStruct(((N_DEV,) if t in first_only else (x.shape[0], N_DEV)) + x.shape[1:], x.dtype)
                   for t, x in enumerate(xs)],
        in_specs=[ANY] * n, out_specs=[ANY] * n,
        scratch_shapes=[pltpu.SemaphoreType.DMA((n, 7)), pltpu.SemaphoreType.DMA((n, 7)), pltpu.SemaphoreType.DMA((n,))],
    )(*xs)


HBM = pl.BlockSpec(memory_space=pltpu.HBM)
SEM = pl.BlockSpec(memory_space=pltpu.SEMAPHORE)
EFFECT = pltpu.SideEffectType.DATAFLOW_SIDE_EFFECTING
ALL_DEVICES = [(px, py, pc) for px in range(2) for py in range(2) for pc in range(2)]


def _hbm(x):
    return pltpu.with_memory_space_constraint(x, pltpu.HBM)


def _split_start(body, name, srcs, lands, after=None):
    ns, n = len(srcs), len(lands)
    extra = [after] if after is not None else []

    def full_body(*refs):
        sems = ns + n + len(extra)
        body(refs[:ns], refs[ns:ns + n], refs[sems], refs[sems + 1])
        refs[-1][...] = jnp.zeros_like(refs[-1])

    res = pl.pallas_call(
        full_body, name=name,
        out_shape=(pltpu.SemaphoreType.DMA((n,)), pltpu.SemaphoreType.DMA((n,)),
                   *[pltpu.HBM(x.shape, x.dtype) for x in srcs], *[pltpu.HBM(x.shape, x.dtype) for x in lands],
                   jax.ShapeDtypeStruct((8, LANES), F32)),
        in_specs=[HBM] * (ns + n) + [ANY] * len(extra),
        out_specs=(SEM, SEM, *[HBM] * (ns + n), pl.BlockSpec(memory_space=pltpu.VMEM)),
        input_output_aliases={i: 2 + i for i in range(ns + n)},
        compiler_params=pltpu.CompilerParams(has_side_effects=EFFECT),
    )(*[_hbm(x) for x in srcs], *[_hbm(x) for x in lands], *extra)
    return res[0], res[1], list(res[2:2 + ns]), list(res[2 + ns:2 + ns + n]), res[-1]


def _split_wait(name, send_sems, recv_sems, srcs, lands, after, sent, landed):
    ns, n = len(srcs), len(lands)

    def body(*refs):
        src_refs, land_refs, ssem, rsem = refs[:ns], refs[ns:ns + n], refs[ns + n], refs[ns + n + 1]
        mx, my, mc, _ = _place()
        for t in range(n):
            out = sent(src_refs[t] if ns else None, land_refs[t])
            inn = landed(land_refs[t])
            pltpu.make_async_remote_copy(src_ref=out, dst_ref=out, send_sem=ssem.at[t], recv_sem=rsem.at[t],
                                         device_id=(mx, my, mc), device_id_type=MESH).wait_send()
            pltpu.make_async_remote_copy(src_ref=inn, dst_ref=inn, send_sem=ssem.at[t], recv_sem=rsem.at[t],
                                         device_id=(mx, my, mc), device_id_type=MESH).wait_recv()

    res = pl.pallas_call(
        body, name=name,
        out_shape=(*[pltpu.HBM(x.shape, x.dtype) for x in srcs], *[pltpu.HBM(x.shape, x.dtype) for x in lands]),
        in_specs=[HBM] * (ns + n) + [SEM, SEM, ANY], out_specs=[HBM] * (ns + n),
        input_output_aliases={i: i for i in range(ns + n)},
        compiler_params=pltpu.CompilerParams(has_side_effects=EFFECT),
    )(*srcs, *lands, send_sems, recv_sems, after)
    return list(res[:ns]), list(res[ns:])


FIRST_HOP = 5
SECOND_HOP = 3


def gather_start(srcs, l, tag, after=None):
    lands = [lax.empty((N_DEV,) + x.shape[1:], x.dtype) for x in srcs]

    def body(src_refs, land_refs, send_sems, recv_sems):
        mx, my, mc, chips = _place()
        me = 4 * mx + 2 * my + mc
        for t in range(len(srcs)):
            for to in [(mx, my, mc), (mx, my, 1 - mc)] + [(cx, cy, mc) for cx, cy in chips]:
                pltpu.make_async_remote_copy(
                    src_ref=src_refs[t].at[l], dst_ref=land_refs[t].at[me], send_sem=send_sems.at[t],
                    recv_sem=recv_sems.at[t], device_id=to, device_id_type=MESH).start()

    return _split_start(body, "gather_start_%d%s" % (l, tag), srcs, lands, after=after)


def gather_wait(l, tag, send_sems, recv_sems, srcs, lands, after):
    hop = lambda d: d.at[pl.ds(0, FIRST_HOP)]
    return _split_wait("gather_wait_%d%s" % (l, tag), send_sems, recv_sems, srcs, lands, after,
                       sent=lambda s, d: hop(d), landed=hop)


def gather_pass_start(lands, l, tag):
    def body(src_refs, land_refs, send_sems, recv_sems):
        mx, my, mc, chips = _place()
        for t in range(len(lands)):
            for cx, cy in chips:
                slot = land_refs[t].at[4 * cx + 2 * cy + mc]
                pltpu.make_async_remote_copy(
                    src_ref=slot, dst_ref=slot, send_sem=send_sems.at[t], recv_sem=recv_sems.at[t],
                    device_id=(mx, my, 1 - mc), device_id_type=MESH).start()

    send_sems, recv_sems, _, lands, tie = _split_start(body, "gather_pass_start_%d%s" % (l, tag), [], lands)
    return send_sems, recv_sems, lands, tie


def gather_pass_wait(l, tag, send_sems, recv_sems, lands, after):
    hop = lambda d: d.at[pl.ds(0, SECOND_HOP)]
    return _split_wait("gather_pass_wait_%d%s" % (l, tag), send_sems, recv_sems, [], lands, after,
                       sent=lambda s, d: hop(d), landed=hop)[1]


def grad_exchange_start(es, lands, l, tag, after=None):
    def body(e_refs, land_refs, send_sems, recv_sems):
        mx, my, mc, _ = _place()
        me = 4 * mx + 2 * my + mc
        for t in range(len(es)):
            for px, py, pc in ALL_DEVICES:
                pltpu.make_async_remote_copy(
                    src_ref=e_refs[t].at[4 * px + 2 * py + pc], dst_ref=land_refs[t].at[l, me], send_sem=send_sems.at[t],
                    recv_sem=recv_sems.at[t], device_id=(px, py, pc), device_id_type=MESH).start()

    return _split_start(body, "grad_exchange_start_%d%s" % (l, tag), es, lands, after=after)


def grad_exchange_wait(l, tag, send_sems, recv_sems, es, lands, after):
    return _split_wait("grad_exchange_wait_%d%s" % (l, tag), send_sems, recv_sems, es, lands, after,
                       sent=lambda s, d: s, landed=lambda d: d.at[l])


def _adam(g, w, m, v):
    nm = ADAM_B1 * m + (1.0 - ADAM_B1) * g
    nv = ADAM_B2 * v + (1.0 - ADAM_B2) * jnp.square(g)
    m_hat = nm / (1.0 - ADAM_B1 ** ADAM_STEP)
    v_hat = nv / (1.0 - ADAM_B2 ** ADAM_STEP)
    return -ADAM_LR * (m_hat / (jnp.sqrt(v_hat) + ADAM_EPS) + ADAM_WD * w), nm, nv


def adamw_big(parts, w, m, v, name, tie):
    depth, _, a, b = parts.shape
    ta = _row_tile(a)

    def body(p_ref, w_ref, m_ref, v_ref, tie_ref, g_ref, d_ref, nm_ref, nv_ref):
        g = p_ref[0].astype(F32)
        for k in range(1, N_DEV):
            g = g + p_ref[k].astype(F32)
        g_ref[...] = g
        d_ref[...], nm_ref[...], nv_ref[...] = _adam(g, w_ref[...], m_ref[...], v_ref[...])

    blk = pl.BlockSpec((None, ta, b), lambda l, i: (l, i, 0))
    return pl.pallas_call(
        body, name=name, grid=(depth, a // ta),
        in_specs=[pl.BlockSpec((None, N_DEV, ta, b), lambda l, i: (l, 0, i, 0)), blk, blk, blk, ANY], out_specs=[blk] * 4,
        out_shape=[jax.ShapeDtypeStruct((depth, a, b), F32)] * 4,
        compiler_params=_params(dimension_semantics=("arbitrary", "arbitrary")),
    )(parts, w, m, v, tie)


SMALL_VIEW = {'norm_mix': (DEPTH, 1024), 'ssm_norm': (DEPTH, 1024), 'attn_out_norm': (DEPTH, 1024),
              'norm_mem_q': (DEPTH, 1024), 'norm_mem_kv': (DEPTH, 1024), 'norm_ffn': (DEPTH, 1024),
              'q_norm': (DEPTH, 384), 'kv_norm': (DEPTH, 256), 'ssm_conv_b': (DEPTH, 2048), 'ffn_conv_b': (DEPTH, 5632),
              'dt_bias': (DEPTH, SSM_HEADS), 'a_log': (DEPTH, SSM_HEADS), 'd_skip': (DEPTH, SSM_HEADS),
              'ssm_conv_w': (DEPTH, SSM_CONV * CONV_CH // N_DEV), 'ffn_conv_w': (DEPTH, FFN_CONV * 2 * D_FF // N_DEV),
              'final_norm': (1, 1024)}
SMALL_NAMES = list(SMALL_VIEW)
SMALL_SHARDED = {'ssm_conv_w': (SSM_CONV, CONV_CH // N_DEV, CONV_CH), 'ffn_conv_w': (FFN_CONV, 2 * D_FF // N_DEV, 2 * D_FF)}


def adamw_small(gathered, ws, ms, vs, tie):
    nsm = len(SMALL_NAMES)

    def body(*refs):
        g8_ref = refs[0]
        w_refs, m_refs, v_refs = refs[1:1 + nsm], refs[1 + nsm:1 + 2 * nsm], refs[1 + 2 * nsm:1 + 3 * nsm]
        outs = refs[2 + 3 * nsm:2 + 7 * nsm]
        sum_ref = refs[2 + 7 * nsm]
        shard_bufs = refs[3 + 7 * nsm:]
        tot = g8_ref[:, 0, :]
        for d in range(1, N_DEV):
            tot = tot + g8_ref[:, d, :]
        sum_ref[...] = tot
        mx, my, mc, _ = _place()
        dev = 4 * mx + 2 * my + mc

        def update(i, g):
            d, nm, nv = _adam(g, w_refs[i][...], m_refs[i][...], v_refs[i][...])
            outs[i][...] = g
            outs[nsm + i][...] = d
            outs[2 * nsm + i][...] = nm
            outs[3 * nsm + i][...] = nv

        for i, name in enumerate(SMALL_NAMES):
            rows, cols = SMALL_VIEW[name]
            off = SMALL_OFF[name]
            if name in SMALL_SHARDED:
                taps, per, full = SMALL_SHARDED[name]
                buf = shard_bufs[list(SMALL_SHARDED).index(name)]
                for d in range(N_DEV):
                    @pl.when(dev == d)
                    def _(d=d, taps=taps, per=per, full=full, off=off, buf=buf):
                        for k in range(taps):
                            buf[:, per * k:per * (k + 1)] = sum_ref[:, off + full * k + per * d:off + full * k + per * (d + 1)]
                update(i, buf[...])
            else:
                update(i, sum_ref[0:rows, off:off + cols])

    views = [jax.ShapeDtypeStruct(SMALL_VIEW[n], F32) for n in SMALL_NAMES]
    vmem = pl.BlockSpec(memory_space=pltpu.VMEM)
    res = pl.pallas_call(
        body, name="adamw_small", out_shape=views * 4, in_specs=[vmem] * (1 + 3 * nsm) + [ANY],
        out_specs=[vmem] * (4 * nsm),
        scratch_shapes=[pltpu.VMEM((DEPTH, SMALL_W), F32)] + [pltpu.VMEM(SMALL_VIEW[n], F32) for n in SMALL_SHARDED],
        compiler_params=_params(),
    )(gathered, *[ws[n] for n in SMALL_NAMES], *[ms[n] for n in SMALL_NAMES], *[vs[n] for n in SMALL_NAMES], tie)
    return [dict(zip(SMALL_NAMES, res[k * nsm:(k + 1) * nsm])) for k in range(4)]


def _layer_weights(gathered):
    w = {}
    for n, g in gathered.items():
        if n == 'w_in':
            w['w_proj'] = assemble_proj(g)
        elif n == 'w_uq':
            w['w_uq'] = assemble_uq(g)
        elif n == 'w_ukv':
            w['w_kn'], w['w_v'] = assemble_ukv(g)
        elif n == 'w_up':
            w['w_g'], w['w_vv'] = assemble_up(g)
        else:
            w[n] = g.reshape(N_DEV * BIG[n][0], BIG[n][1])
    return w


def _rope_post(acc, row_tiles, full_tiles, o_refs):
    for h in range(acc.shape[1] // HEAD_PAD):
        sl = slice(HEAD_PAD * h, HEAD_PAD * (h + 1))
        o_refs[0][:, sl] = _rope_tile(acc[:, sl], row_tiles[0], row_tiles[1]).astype(o_refs[0].dtype)


def _norm_post(acc, row_tiles, full_tiles, o_refs):
    o_refs[0][...] = acc
    o_refs[1][...] = _rms(acc, full_tiles[0]).astype(o_refs[1].dtype)


def layer_fwd(x0, h1, mem, cosm, sinm, w, sm, l, tie=None):
    gain = lambda n: (sm[n], l)
    sv = dict(x0=x0)
    sv['h1'] = h1 if h1 is not None else rmsnorm_fwd(x0, gain('norm_mix'), "norm_mix_fwd", tie=tie)
    proj = sv['proj'] = matmul([(sv['h1'], w['w_proj'])], 'nn', F32, "proj_fwd", tie=tie if h1 is not None else None)
    sv['xbc'] = ssm_conv_fwd(proj, sm['ssm_conv_w'], sm['ssm_conv_b'], l)
    sv['y'], sv['prevs'] = ssd_fwd(sv['xbc'], proj, sm['ptile'], l)
    mix = gate_norm_fwd(sv['y'], proj, gain('ssm_norm'))
    sv['cqn'] = rmsnorm_fwd(proj, gain('q_norm'), "q_norm_fwd", Q_LORA, OFF_CQ // Q_LORA)
    sv['ckvn'] = rmsnorm_fwd(proj, gain('kv_norm'), "kv_norm_fwd", KV_LORA, OFF_CKV // KV_LORA)
    sv['q'] = matmul([(sv['cqn'], w['w_uq'])], 'nn', BF16, "uq_fwd", post=_rope_post, rows=[cosm, sinm])
    kn = matmul([(sv['ckvn'], w['w_kn'])], 'nn', BF16, "kn_fwd")
    sv['k'] = build_k(kn, proj, cosm, sinm)
    sv['v'] = matmul([(sv['ckvn'], w['w_v'])], 'nn', BF16, "v_fwd")
    sv['o'], sv['lse'] = mla_fwd(sv['q'], sv['k'], sv['v'])
    mix = sv['mix'] = rmsnorm_fwd(sv['o'], gain('attn_out_norm'), "attn_out_norm_fwd", out=(D_SSM, BF16, D_MIX, 1),
                                  into=(mix, 0))
    x1, sv['hq'] = matmul([(mix, w['w_out'])], 'nn', F32, "out_fwd", add=x0, post=_norm_post,
                          fulls=[gain('norm_mem_q')], outs=[F32, BF16], full_n=True)
    sv['x1'] = x1
    sv['mn'] = rmsnorm_fwd(mem, gain('norm_mem_kv'), "norm_mem_kv_fwd")
    if 'later' in w:
        w.update(w.pop('later')(sv['hq']))
    sv['mq'] = matmul([(sv['hq'], w['w_mq'])], 'nn', BF16, "mq_fwd")
    sv['mk'] = matmul([(sv['mn'], w['w_mk'])], 'nn', BF16, "mk_fwd")
    sv['mv'] = matmul([(sv['mn'], w['w_mv'])], 'nn', BF16, "mv_fwd")
    sv['om'] = mem_fwd(sv['mq'], sv['mk'], sv['mv'])
    x2, sv['h3'] = matmul([(sv['om'], w['w_mo'])], 'nn', F32, "mo_fwd", add=x1, post=_norm_post,
                          fulls=[gain('norm_ffn')], outs=[F32, BF16], full_n=True)
    sv['x2'] = x2
    sv['ug'] = matmul([(sv['h3'], w['w_g'])], 'nn', F32, "up_g_fwd")
    sv['uv'] = matmul([(sv['h3'], w['w_vv'])], 'nn', F32, "up_v_fwd")
    sv['a'] = ffn_act_fwd(sv['ug'], sv['uv'], sm['ffn_conv_w'], sm['ffn_conv_b'], l)
    if l + 1 < DEPTH:
        x3, h1_next = matmul([(sv['a'], w['w_down'])], 'nn', F32, "down_fwd", add=x2, post=_norm_post,
                             fulls=[(sm['norm_mix'], l + 1)], outs=[F32, BF16], full_n=True)
    else:
        x3, h1_next = matmul([(sv['a'], w['w_down'])], 'nn', F32, "down_fwd_last", add=x2), None
    return x3, h1_next, sv


EARLY_GRADS = ('w_down', 'w_up', 'w_mo', 'w_mq', 'w_mk', 'w_mv', 'w_out')
LATE_GRADS = ('w_uq', 'w_ukv', 'w_in')


def layer_bwd(dx3, mem, cosm, sinm_neg, w, sm, l, sv, on_grads, tie=None):
    gain = lambda n: (sm[n], l)
    big, small = {}, {}
    proj = sv['proj']
    da = matmul([(dx3, w['w_down'])], 'nt', BF16, "down_bwd_a", tie=tie)
    big['w_down'] = matmul([(sv['a'], dx3)], 'tn', BF16, "down_bwd_w")
    dug, duv, dcwg, dcwv, dcbg, dcbv = ffn_act_bwd(sv['ug'], sv['uv'], sm['ffn_conv_w'], sm['ffn_conv_b'], l, da)
    small['ffn_conv_w'] = jnp.concatenate([dcwg, dcwv], axis=1)
    small['ffn_conv_b'] = jnp.concatenate([dcbg, dcbv], axis=1)
    dh3 = matmul([(dug, w['w_g']), (duv, w['w_vv'])], 'nt', BF16, "up_bwd_h")
    big['w_up'] = extract_up(matmul([(sv['h3'], dug)], 'tn', BF16, "up_g_bwd_w"),
                             matmul([(sv['h3'], duv)], 'tn', BF16, "up_v_bwd_w"))
    dx2, small['norm_ffn'] = rmsnorm_bwd(sv['x2'], gain('norm_ffn'), dh3, "norm_ffn_bwd", resid=dx3)
    dom = matmul([(dx2, w['w_mo'])], 'nt', BF16, "mo_bwd_a")
    big['w_mo'] = matmul([(sv['om'], dx2)], 'tn', BF16, "mo_bwd_w")
    dmq, dmk, dmv = mem_bwd(sv['mq'], sv['mk'], sv['mv'], dom)
    dhq = matmul([(dmq, w['w_mq'])], 'nt', BF16, "mq_bwd_a")
    big['w_mq'] = matmul([(sv['hq'], dmq)], 'tn', BF16, "mq_bwd_w")
    dmn = matmul([(dmk, w['w_mk']), (dmv, w['w_mv'])], 'nt', BF16, "mkv_bwd_a")
    big['w_mk'] = matmul([(sv['mn'], dmk)], 'tn', BF16, "mk_bwd_w")
    big['w_mv'] = matmul([(sv['mn'], dmv)], 'tn', BF16, "mv_bwd_w")
    _, small['norm_mem_kv'] = rmsnorm_bwd(mem, gain('norm_mem_kv'), dmn, "norm_mem_kv_bwd", dx_dtype=BF16)
    dx1, small['norm_mem_q'] = rmsnorm_bwd(sv['x1'], gain('norm_mem_q'), dhq, "norm_mem_q_bwd", resid=dx2)
    dmix = matmul([(dx1, w['w_out'])], 'nt', BF16, "out_bwd_a")
    big['w_out'] = matmul([(sv['mix'], dx1)], 'tn', BF16, "out_bwd_w")
    early = {n: big.pop(n).reshape((N_DEV,) + BIG[n]) if n != 'w_up' else big.pop(n) for n in EARLY_GRADS}
    tie = on_grads(l, 'a', early)
    dy, dz, small['ssm_norm'] = gate_norm_bwd(sv['y'], proj, gain('ssm_norm'), dmix, tie=tie)
    dxbc_act, dsmall_ssd, small['ptile'] = ssd_bwd(sv['xbc'], proj, sm['ptile'], l, sv['prevs'], dy)
    dxbc, small['ssm_conv_w'], small['ssm_conv_b'] = ssm_conv_bwd(proj, sm['ssm_conv_w'], sm['ssm_conv_b'], l, dxbc_act)
    do, small['attn_out_norm'] = rmsnorm_bwd(sv['o'], gain('attn_out_norm'), dmix, "attn_out_norm_bwd", dh_colblock=1)
    dq, dk, dv = mla_bwd(sv['q'], sv['k'], sv['v'], sv['o'], sv['lse'], do, cosm, sinm_neg)
    dsmall = dsmall_bwd(dk, dsmall_ssd, cosm, sinm_neg)
    dcqn = matmul([(dq, w['w_uq'])], 'nt', BF16, "uq_bwd_a")
    big['w_uq'] = extract_uq(matmul([(sv['cqn'], dq)], 'tn', BF16, "uq_bwd_w"))
    dckvn = matmul([(dk, w['w_kn']), (dv, w['w_v'])], 'nt', BF16, "ukv_bwd_a")
    big['w_ukv'] = extract_ukv(matmul([(sv['ckvn'], dk)], 'tn', BF16, "kn_bwd_w"),
                               matmul([(sv['ckvn'], dv)], 'tn', BF16, "v_bwd_w"))
    dcq, small['q_norm'] = rmsnorm_bwd(proj, gain('q_norm'), dcqn, "q_norm_bwd", width=Q_LORA,
                                       colblock=OFF_CQ // Q_LORA, dx_dtype=BF16)
    dckv, small['kv_norm'] = rmsnorm_bwd(proj, gain('kv_norm'), dckvn, "kv_norm_bwd", width=KV_LORA,
                                         colblock=OFF_CKV // KV_LORA, dx_dtype=BF16)
    wp = w['w_proj']
    xbc_half = lambda c: Opnd(dxbc, c0=c, shape=(dxbc.shape[0], 1024))
    wwin = lambda off, width: Opnd(wp, c0=off // width, shape=(D_MODEL, width))
    dh1 = matmul([(dz, wwin(OFF_Z, 1024)), (xbc_half(0), wwin(OFF_XBC, 1024)), (xbc_half(1), wwin(OFF_XBC + 1024, 1024)),
                  (dcq, wwin(OFF_CQ, Q_LORA)), (dsmall, wwin(OFF_SMALL, LANES)), (dckv, wwin(OFF_CKV, KV_LORA))],
                 'nt', BF16, "proj_bwd_a")
    h1 = sv['h1']
    big['w_in'] = extract_proj(
        matmul([(h1, dz)], 'tn', BF16, "proj_z_bwd_w"), matmul([(h1, dxbc)], 'tn', BF16, "proj_xbc_bwd_w"),
        matmul([(h1, dcq)], 'tn', BF16, "proj_cq_bwd_w"), matmul([(h1, dsmall)], 'tn', BF16, "proj_small_bwd_w"),
        matmul([(h1, dckv)], 'tn', BF16, "proj_ckv_bwd_w"))
    dx0, small['norm_mix'] = rmsnorm_bwd(sv['x0'], gain('norm_mix'), dh1, "norm_mix_bwd", resid=dx1)
    return dx0, on_grads(l, 'b', big), small


def _small_row(small, final=None):
    pt = small['ptile']
    parts = []
    for n, wd in SMALL_SEGS:
        if n in ('dt_bias', 'a_log', 'd_skip'):
            parts.append(pt[('dt_bias', 'a_log', 'd_skip').index(n)][None, :])
        elif n in SMALL_SHARDED:
            parts.append(small[n].reshape(1, wd))
        elif n == 'final_norm':
            parts.append(final if final is not None else jnp.zeros((1, wd), F32))
        else:
            parts.append(small[n])
    return jnp.concatenate(parts, axis=1)


def _rope_tables(positions):
    inv_freq = 1.0 / (ROPE_THETA ** (jnp.arange(0, QK_ROPE, 2, dtype=F32) / QK_ROPE))
    ang = positions.astype(F32)[:, None] * inv_freq
    cos, sin = jnp.cos(ang), jnp.sin(ang)
    s = positions.shape[0]
    pad = jnp.zeros((s, LANES - ROPE_LANE0 - QK_ROPE), F32)
    cosm = jnp.concatenate([jnp.ones((s, ROPE_LANE0), F32), cos, cos, pad], axis=1)
    sinm = jnp.concatenate([jnp.zeros((s, ROPE_LANE0), F32), -sin, sin, pad], axis=1)
    return cosm, sinm


def _small_views(rep, conv_full):
    sm = {n: rep[n].reshape(DEPTH, 1, -1) for n in ('norm_mix', 'ssm_norm', 'attn_out_norm', 'norm_mem_q',
                                                    'norm_mem_kv', 'norm_ffn', 'q_norm', 'kv_norm', 'ssm_conv_b',
                                                    'ffn_conv_b')}
    sm.update(conv_full)
    rows = jnp.stack([rep['dt_bias'], rep['a_log'], rep['d_skip']], axis=1)
    sm['ptile'] = jnp.pad(rows, ((0, 0), (0, 8 - 3), (0, LANES - SSM_HEADS)))
    return sm


def local_step(x, mem, positions, target, sm, final_norm, weights_of, on_grads):
    cosm, sinm = _rope_tables(positions)
    sinm_neg = -sinm
    saved, ws = [], []
    h, h1 = x, None
    for l in range(DEPTH):
        w, tie = weights_of(l, h)
        ws.append(w)
        h, h1, sv = layer_fwd(h, h1, mem, cosm, sinm, w, sm, l, tie=tie)
        saved.append(sv)
    dx, dfinal, lossv = loss_head(h, (final_norm.reshape(1, 1, -1), 0), target)
    rows = [None] * DEPTH
    tie = None
    for l in reversed(range(DEPTH)):
        dx, tie, small = layer_bwd(dx, mem, cosm, sinm_neg, ws[l], sm, l, saved[l], on_grads, tie=tie)
        rows[l] = _small_row(small, dfinal if l == 0 else None)
    return lossv[0, 0], dx, jnp.concatenate(rows, axis=0)


def kernel(x, mem, positions, norm_mix, w_in, ssm_conv_w, ssm_conv_b, dt_bias, a_log, d_skip, ssm_norm, q_norm, w_uq, kv_norm, w_ukv, attn_out_norm, w_out, norm_mem_q, norm_mem_kv, w_mq, w_mk, w_mv, w_mo, norm_ffn, w_up, ffn_conv_w, ffn_conv_b, w_down, final_norm, loss_target, m_norm_mix, m_w_in, m_ssm_conv_w, m_ssm_conv_b, m_dt_bias, m_a_log, m_d_skip, m_ssm_norm, m_q_norm, m_w_uq, m_kv_norm, m_w_ukv, m_attn_out_norm, m_w_out, m_norm_mem_q, m_norm_mem_kv, m_w_mq, m_w_mk, m_w_mv, m_w_mo, m_norm_ffn, m_w_up, m_ffn_conv_w, m_ffn_conv_b, m_w_down, m_final_norm, v_norm_mix, v_w_in, v_ssm_conv_w, v_ssm_conv_b, v_dt_bias, v_a_log, v_d_skip, v_ssm_norm, v_q_norm, v_w_uq, v_kv_norm, v_w_ukv, v_attn_out_norm, v_w_out, v_norm_mem_q, v_norm_mem_kv, v_w_mq, v_w_mk, v_w_mv, v_w_mo, v_norm_ffn, v_w_up, v_ffn_conv_w, v_ffn_conv_b, v_w_down, v_final_norm):
    args = locals()
    wts = {n: args[n] for n in WEIGHT_NAMES}
    ms = {n: args['m_' + n] for n in WEIGHT_NAMES}
    vs = {n: args['v_' + n] for n in WEIGHT_NAMES}

    st = dict(srcs={n: wts[n].astype(BF16) for n in BIG_NAMES}, exchanges=[],
              lands={n: lax.empty((DEPTH, N_DEV) + BIG[n], BF16) for n in BIG_NAMES})
    first = LATE_GRADS + ('w_out',)
    rest = tuple(n for n in BIG_NAMES if n not in first)
    got = all_gather_blocks([st['srcs'][n] for n in first] + [wts[n] for n in SMALL_SHARDED],
                            first_only=tuple(range(len(first))))
    conv_full = {}
    for n, g in zip(SMALL_SHARDED, got[len(first):]):
        taps, per, full = SMALL_SHARDED[n]
        conv_full[n] = jnp.moveaxis(g, 1, 2).reshape(DEPTH, taps, full)
    sm = _small_views(wts, conv_full)

    def start(names, l, tag, after=None):
        send_sems, recv_sems, thru, lands, tie = gather_start([st['srcs'][n] for n in names], l, tag, after)
        st['srcs'].update(zip(names, thru))
        return (names, l, tag, send_sems, recv_sems, lands), tie

    def finish(handle, after):
        names, l, tag, send_sems, recv_sems, lands = handle
        thru, lands = gather_wait(l, tag, send_sems, recv_sems, [st['srcs'][n] for n in names], lands, after)
        st['srcs'].update(zip(names, thru))
        send_sems, recv_sems, lands, _ = gather_pass_start(lands, l, tag)
        lands = gather_pass_wait(l, tag, send_sems, recv_sems, lands, after)
        return _layer_weights(dict(zip(names, lands)))

    later, _ = start(rest, 0, "r", after=got[0])

    def weights_of(l, h):
        if l == 0:
            w = _layer_weights(dict(zip(first, got[:len(first)])))
            w['later'] = functools.partial(finish, later)
        else:
            w = finish(st['next'], h)
        tie = None
        if l + 1 < DEPTH:
            st['next'], tie = start(BIG_NAMES, l + 1, "")
        return w, tie

    def on_grads(l, tag, big, after=None):
        if (l, tag) == (0, 'b') and after is None:
            st['held'] = big
            return None
        names = list(big)
        send_sems, recv_sems, thru, lands, tie = grad_exchange_start(
            [big[n] for n in names], [st['lands'][n] for n in names], l, tag, after)
        st['lands'].update(zip(names, lands))
        st['exchanges'].append((l, tag, names, send_sems, recv_sems, thru))
        return tie

    loss_local, dx, small_rows = local_step(x[0], mem[0], positions[0], loss_target[0], sm, final_norm, weights_of,
                                            on_grads)
    outs = [{}, {}, {}, {}]

    small_all = all_gather_blocks([small_rows])[0]
    tie = on_grads(0, 'b', st['held'], after=small_all)
    view = lambda d: {n: d[n].reshape(SMALL_VIEW[n]) for n in SMALL_NAMES}
    res = adamw_small(small_all, view(wts), view(ms), view(vs), tie)
    for k in range(4):
        for n in SMALL_NAMES:
            outs[k][n] = res[k][n].reshape(wts[n].shape)

    def wait(exchange, after):
        l, tag, names, send_sems, recv_sems, thru = exchange
        _, lands = grad_exchange_wait(l, tag, send_sems, recv_sems, thru, [st['lands'][n] for n in names], after)
        st['lands'].update(zip(names, lands))

    def update(names, tie):
        for n in names:
            res_n = adamw_big(st['lands'][n], wts[n], ms[n], vs[n], "adamw_" + n, tie)
            tie = res_n[0]
            for k in range(4):
                outs[k][n] = res_n[k]
        return tie

    for exchange in st['exchanges'][:-1]:
        wait(exchange, res[0]['final_norm'])
    tie = update(EARLY_GRADS, res[0]['final_norm'])
    wait(st['exchanges'][-1], tie)
    update(LATE_GRADS, tie)

    loss = lax.psum(loss_local, ("x", "y", "c"))
    return (loss, dx[None], *[outs[0][n] for n in WEIGHT_NAMES], *[outs[1][n] for n in WEIGHT_NAMES],
            *[outs[2][n] for n in WEIGHT_NAMES], *[outs[3][n] for n in WEIGHT_NAMES])
```

```python
import functools
import math
from typing import Any, NamedTuple, Optional

import jax
import jax.numpy as jnp
from jax import lax
from jax.experimental import pallas as pl
from jax.experimental.pallas import tpu as pltpu

F32 = jnp.float32
BF16 = jnp.bfloat16

D_MODEL = 1024
DEPTH = 4
MEM_LEN = 256
EPS = 1e-6
SSM_HEADS = 16
SSM_HEAD_DIM = 64
D_SSM = 1024
SSM_GROUPS = 4
SSM_STATE = 128
SSM_CONV = 4
SSM_CHUNK = 128
CONV_CH = 2048
MLA_HEADS = 16
QK_NOPE = 64
QK_ROPE = 32
V_DIM = 64
Q_LORA = 384
KV_LORA = 256
ROPE_THETA = 10000.0
MEM_HEADS = 4
MEM_HEAD_DIM = 256
D_FF = 2816
FFN_CONV = 3
D_IN = 3760
D_MIX = 2048
ADAM_LR = 0.001
ADAM_B1 = 0.9
ADAM_B2 = 0.999
ADAM_EPS = 1e-08
ADAM_WD = 0.01
ADAM_STEP = 10

N_DEV = 8
N_CHIP = 4
LANES = 128
HEAD_PAD = 128
PROJ_W = 3840
OFF_Z, OFF_XBC, OFF_CQ, OFF_SMALL, OFF_CKV = 0, 1024, 3072, 3456, 3584
ROPE_LANE0 = 64
VMEM_LIMIT = 56 * 1024 * 1024
MM_BLOCK_BYTES = 4 * 1024 * 1024
WEIGHT_NAMES = ['norm_mix', 'w_in', 'ssm_conv_w', 'ssm_conv_b', 'dt_bias', 'a_log', 'd_skip', 'ssm_norm', 'q_norm',
                'w_uq', 'kv_norm', 'w_ukv', 'attn_out_norm', 'w_out', 'norm_mem_q', 'norm_mem_kv', 'w_mq', 'w_mk',
                'w_mv', 'w_mo', 'norm_ffn', 'w_up', 'ffn_conv_w', 'ffn_conv_b', 'w_down', 'final_norm']
BIG = {'w_in': (1024, 470), 'w_uq': (384, 192), 'w_ukv': (256, 256), 'w_up': (1024, 704), 'w_out': (256, 1024),
       'w_mq': (128, 1024), 'w_mk': (128, 1024), 'w_mv': (128, 1024), 'w_mo': (128, 1024), 'w_down': (352, 1024)}
BIG_NAMES = list(BIG)
PROJ_SEGS = [(0, 1024, OFF_Z), (1024, 3072, OFF_XBC), (3072, 3088, OFF_SMALL), (3088, 3472, OFF_CQ),
             (3472, 3728, OFF_CKV), (3728, 3760, OFF_SMALL + ROPE_LANE0)]
SMALL_SEGS = [('norm_mix', 1024), ('ssm_norm', 1024), ('attn_out_norm', 1024), ('norm_mem_q', 1024),
              ('norm_mem_kv', 1024), ('norm_ffn', 1024), ('q_norm', 384), ('kv_norm', 256), ('ssm_conv_b', 2048),
              ('ffn_conv_b', 5632), ('dt_bias', 128), ('a_log', 128), ('d_skip', 128),
              ('ssm_conv_w', SSM_CONV * CONV_CH), ('ffn_conv_w', FFN_CONV * 2 * D_FF), ('final_norm', 1024)]
SMALL_OFF = {}
_o = 0
for _n, _w in SMALL_SEGS:
    SMALL_OFF[_n] = _o
    _o += _w
SMALL_W = _o


def _params(**kw):
    return pltpu.CompilerParams(vmem_limit_bytes=VMEM_LIMIT, **kw)


def _pick(n, cap):
    if n <= cap:
        return n
    best = None
    for t in range(LANES, cap + 1, LANES):
        if n % t == 0:
            best = t
    assert best is not None, (n, cap)
    return best


def _row_tile(a, cap=256):
    if a <= cap:
        return a
    best = None
    for t in range(16, cap + 1, 16):
        if a % t == 0:
            best = t
    assert best is not None, (a, cap)
    return best


class Opnd(NamedTuple):
    arr: Any
    lead: Optional[int] = None
    r0: int = 0
    c0: int = 0
    shape: Optional[tuple] = None


def _opnd(x):
    return x if isinstance(x, Opnd) else Opnd(x)


def _lshape(o):
    return tuple(o.shape) if o.shape is not None else tuple(o.arr.shape[-2:])


def _spec(o, br, bc, bi, bj):
    rr, cc = _lshape(o)
    assert rr % br == 0 and cc % bc == 0, (rr, cc, br, bc)
    ro, co = o.r0 * (rr // br), o.c0 * (cc // bc)
    if o.lead is None:
        return pl.BlockSpec((br, bc), lambda i, j: (ro + bi(i, j), co + bj(i, j)))
    return pl.BlockSpec((None, br, bc), lambda i, j: (o.lead, ro + bi(i, j), co + bj(i, j)))


_DIMS = {'nn': (((1,), (0,)), ((), ())), 'nt': (((1,), (1,)), ((), ())), 'tn': (((0,), (0,)), ((), ()))}
_ROW = lambda i, j: i
_COL = lambda i, j: j
_ZERO = lambda i, j: 0


def matmul(pairs, mode, out_dtype, name, add=None, tie=None, post=None, rows=(), fulls=(), outs=None, full_n=False):
    pairs = [(_opnd(a), _opnd(b)) for a, b in pairs]
    a0, b0 = pairs[0]
    if mode == 'nn':
        m, n = _lshape(a0)[0], _lshape(b0)[1]
    elif mode == 'nt':
        m, n = _lshape(a0)[0], _lshape(b0)[0]
    else:
        m, n = _lshape(a0)[1], _lshape(b0)[1]
    isz = lambda o: jnp.dtype(o.arr.dtype).itemsize
    osz = jnp.dtype(out_dtype).itemsize
    cap = lambda budget, per: max(LANES, budget // per // LANES * LANES)
    if mode == 'tn':
        ktok = _lshape(a0)[0]
        tm = _pick(m, cap(3 * MM_BLOCK_BYTES // 2, ktok * isz(a0)))
        tn = _pick(n, cap(3 * MM_BLOCK_BYTES // 2, ktok * isz(b0)))
    else:
        tm = _pick(m, min(2048, cap(2 * MM_BLOCK_BYTES, sum(_lshape(a)[1] * isz(a) for a, _ in pairs))))
        tn = _pick(n, min(cap(3 * MM_BLOCK_BYTES // 2, sum(_lshape(a)[1] * isz(b) for a, b in pairs)),
                          cap(MM_BLOCK_BYTES, tm * osz), n // 2 if n >= 1024 else n))
        if full_n:
            tm, tn = _pick(m, min(tm, cap(MM_BLOCK_BYTES // 2, n * osz))), n
    npairs = len(pairs)
    outs = list(outs) if outs is not None else [out_dtype]
    nadd = 1 if add is not None else 0
    nrows, nfulls = len(rows), len(fulls)

    def body(*refs):
        acc = None
        for p in range(npairs):
            a = refs[2 * p][...].astype(BF16)
            b = refs[2 * p + 1][...].astype(BF16)
            d = lax.dot_general(a, b, _DIMS[mode], preferred_element_type=F32)
            acc = d if acc is None else acc + d
        if add is not None:
            acc = acc + refs[2 * npairs][...].astype(F32)
        o_refs = refs[len(refs) - len(outs):]
        if post is None:
            o_refs[0][...] = acc.astype(out_dtype)
        else:
            x0 = 2 * npairs + nadd
            post(acc, [r[...] for r in refs[x0:x0 + nrows]], [r[...] for r in refs[x0 + nrows:x0 + nrows + nfulls]], o_refs)

    tie_specs = [pl.BlockSpec((tm, r.shape[1]), lambda i, j: (i, 0)) for r in rows]
    tie_specs += [pl.BlockSpec((None,) + f.shape[1:], lambda i, j, ld=ld, nd=f.ndim - 1: (ld,) + (0,) * nd) for f, ld in fulls]
    tie_args = list(rows) + [f for f, _ in fulls]
    if tie is not None:
        tie_specs.append(pl.BlockSpec(memory_space=pl.ANY))
        tie_args.append(tie)

    in_specs, args = [], []
    for a, b in pairs:
        if mode == 'nn':
            k = _lshape(a)[1]
            in_specs += [_spec(a, tm, k, _ROW, _ZERO), _spec(b, k, tn, _ZERO, _COL)]
        elif mode == 'nt':
            k = _lshape(a)[1]
            in_specs += [_spec(a, tm, k, _ROW, _ZERO), _spec(b, tn, k, _COL, _ZERO)]
        else:
            k = _lshape(a)[0]
            in_specs += [_spec(a, k, tm, _ZERO, _ROW), _spec(b, k, tn, _ZERO, _COL)]
        args += [a.arr, b.arr]
    if add is not None:
        in_specs.append(pl.BlockSpec((tm, tn), lambda i, j: (i, j)))
        args.append(add)
    res = pl.pallas_call(
        body, name=name, grid=(m // tm, n // tn), in_specs=in_specs + tie_specs,
        out_specs=[pl.BlockSpec((tm, tn), lambda i, j: (i, j))] * len(outs),
        out_shape=[jax.ShapeDtypeStruct((m, n), dt) for dt in outs],
        compiler_params=_params(dimension_semantics=("arbitrary", "arbitrary")),
    )(*args, *tie_args)
    return res[0] if len(outs) == 1 else res


def rowwise(fn, rows, fulls, outs, accs, name, tm=256, into=None, tie=None):
    s = rows[0][0].shape[0]
    nrow, nfull, nout, nacc = len(rows), len(fulls), len(outs), len(accs)
    nin = nrow + nfull

    def body(*refs):
        ins = [r[...] for r in refs[:nin]]
        res = fn(*ins)
        if not isinstance(res, (tuple, list)):
            res = (res,)
        orefs = refs[nin + (1 if into is not None else 0) + (1 if tie is not None else 0):]
        for k in range(nout):
            orefs[k][...] = res[k].astype(orefs[k].dtype)
        if nacc:
            @pl.when(pl.program_id(0) == 0)
            def _():
                for k in range(nacc):
                    orefs[nout + k][...] = jnp.zeros_like(orefs[nout + k])

            for k in range(nacc):
                orefs[nout + k][...] += res[nout + k].astype(orefs[nout + k].dtype)

    in_specs = [pl.BlockSpec((tm, w), lambda i, cb=cb: (i, cb)) for _, w, cb in rows]
    in_specs += [pl.BlockSpec((None,) + f.shape[1:], lambda i, ld=ld, nd=f.ndim - 1: (ld,) + (0,) * nd) for f, ld in fulls]
    args = [r[0] for r in rows] + [f for f, _ in fulls]
    aliases = {}
    if into is not None:
        in_specs.append(pl.BlockSpec(memory_space=pl.ANY))
        args.append(into[0])
        aliases = {nin: into[1]}
    if tie is not None:
        in_specs.append(pl.BlockSpec(memory_space=pl.ANY))
        args.append(tie)
    out_specs, out_shape = [], []
    for o in outs:
        w, dt = o[0], o[1]
        total, cb = (o[2], o[3]) if len(o) == 4 else (w, 0)
        out_specs.append(pl.BlockSpec((tm, w), lambda i, cb=cb: (i, cb)))
        out_shape.append(jax.ShapeDtypeStruct((s, total), dt))
    for shp, dt in accs:
        out_specs.append(pl.BlockSpec(shp, lambda i, nd=len(shp): (0,) * nd))
        out_shape.append(jax.ShapeDtypeStruct(shp, dt))
    return pl.pallas_call(
        body, name=name, grid=(s // tm,), in_specs=in_specs, out_specs=out_specs, out_shape=out_shape,
        input_output_aliases=aliases, compiler_params=_params(dimension_semantics=("arbitrary",)),
    )(*args)


def _rms(x, g):
    xf = x.astype(F32)
    var = jnp.mean(xf * xf, axis=-1, keepdims=True)
    return xf * lax.rsqrt(var + EPS) * g


def rmsnorm_fwd(x, g, name, width=None, colblock=0, out=None, into=None, tie=None):
    w = width or x.shape[1]
    return rowwise(lambda xt, gt: _rms(xt, gt), [(x, w, colblock)], [g], [out or (w, BF16)], [], name, into=into,
                   tie=tie)[0]


def rmsnorm_bwd(x, g, dh, name, resid=None, width=None, colblock=0, dh_colblock=0, dx_dtype=F32):
    w = width or x.shape[1]

    def fn(xt, dht, *rest):
        gt = rest[-1]
        _, vjp = jax.vjp(_rms, xt.astype(F32), gt)
        dx, dg = vjp(dht.astype(F32))
        if resid is not None:
            dx = dx + rest[0]
        return dx, dg

    rows = [(x, w, colblock), (dh, w, dh_colblock)] + ([(resid, w, 0)] if resid is not None else [])
    return rowwise(fn, rows, [g], [(w, dx_dtype)], [((1, w), F32)], name)


CONV_R = 64
HALO = 8


def _ext_rows(ref, i, nchunk, above, below):
    r0 = pl.multiple_of(i * CONV_R, CONV_R)
    s = ref.shape[0]
    parts = []
    if above:
        top = ref[pl.ds(pl.multiple_of(jnp.maximum(r0 - HALO, 0), HALO), HALO), :].astype(F32)
        parts.append(jnp.where(i > 0, top, 0.0))
    parts.append(ref[pl.ds(r0, CONV_R), :].astype(F32))
    if below:
        tile = 2 * HALO if ref.dtype == BF16 else HALO
        bot = ref[pl.ds(pl.multiple_of(jnp.minimum(r0 + CONV_R, s - tile), tile), tile), :].astype(F32)[0:HALO]
        parts.append(jnp.where(i < nchunk - 1, bot, 0.0))
    return jnp.concatenate(parts, axis=0)


def _conv_ext(ext, w_ref, b_ref, kw):
    y = ext[HALO:] * w_ref[kw - 1:kw, :] + b_ref[...]
    for k in range(1, kw):
        y = y + pltpu.roll(ext, k, 0)[HALO:] * w_ref[kw - 1 - k:kw - k, :]
    return y


def _conv_t_ext(d, w_ref, kw):
    n = d.shape[0]
    y = d[:n - HALO] * w_ref[kw - 1:kw, :]
    for k in range(1, kw):
        y = y + pltpu.roll(d, n - k, 0)[:n - HALO] * w_ref[kw - 1 - k:kw - k, :]
    return y


def _conv_wgrad(dp, ext, kw):
    out = [jnp.sum(dp, axis=0, keepdims=True), jnp.sum(dp * ext[HALO:HALO + CONV_R], axis=0, keepdims=True)]
    for k in range(1, kw):
        out.append(jnp.sum(dp * pltpu.roll(ext, k, 0)[HALO:HALO + CONV_R], axis=0, keepdims=True))
    return out


def _store_wgrad(res, dw_ref, db_ref, kw):
    db_ref[...] = res[0]
    for k in range(kw):
        dw_ref[kw - 1 - k:kw - k, :] = res[1 + k]


def _silu(x):
    return x * jax.nn.sigmoid(x)


def _dsilu(x):
    s = jax.nn.sigmoid(x)
    return s * (1.0 + x * (1.0 - s))


SSM_TC = 256


def ssm_conv_fwd(proj, cw, cb, l):
    s = proj.shape[0]
    off = OFF_XBC // SSM_TC

    def body(u_ref, w_ref, b_ref, o_ref):
        nchunk = s // CONV_R

        def step(i, carry):
            ext = _ext_rows(u_ref, i, nchunk, True, False)
            o_ref[pl.ds(pl.multiple_of(i * CONV_R, CONV_R), CONV_R), :] = _silu(_conv_ext(ext, w_ref, b_ref, SSM_CONV))
            return carry

        lax.fori_loop(0, nchunk, step, 0)

    return pl.pallas_call(
        body, name="ssm_conv_fwd", grid=(CONV_CH // SSM_TC,),
        in_specs=[pl.BlockSpec((s, SSM_TC), lambda j: (0, off + j)),
                  pl.BlockSpec((None, SSM_CONV, SSM_TC), lambda j: (l, 0, j)),
                  pl.BlockSpec((None, 1, SSM_TC), lambda j: (l, 0, j))],
        out_specs=pl.BlockSpec((s, SSM_TC), lambda j: (0, j)),
        out_shape=jax.ShapeDtypeStruct((s, CONV_CH), F32),
        compiler_params=_params(dimension_semantics=("arbitrary",)),
    )(proj, cw, cb)


def ssm_conv_bwd(proj, cw, cb, l, dact):
    s = proj.shape[0]
    off = OFF_XBC // SSM_TC

    def body(u_ref, w_ref, b_ref, d_ref, du_ref, dw_ref, db_ref):
        nchunk = s // CONV_R

        def step(i, carry):
            ext = _ext_rows(u_ref, i, nchunk, True, True)
            dpre = _ext_rows(d_ref, i, nchunk, False, True) * _dsilu(_conv_ext(ext, w_ref, b_ref, SSM_CONV))
            du_ref[pl.ds(pl.multiple_of(i * CONV_R, CONV_R), CONV_R), :] = _conv_t_ext(dpre, w_ref, SSM_CONV).astype(du_ref.dtype)
            return tuple(c + g for c, g in zip(carry, _conv_wgrad(dpre[:CONV_R], ext, SSM_CONV)))

        zero = jnp.zeros((1, SSM_TC), F32)
        _store_wgrad(lax.fori_loop(0, nchunk, step, (zero,) * (SSM_CONV + 1)), dw_ref, db_ref, SSM_CONV)

    return pl.pallas_call(
        body, name="ssm_conv_bwd", grid=(CONV_CH // SSM_TC,),
        in_specs=[pl.BlockSpec((s, SSM_TC), lambda j: (0, off + j)),
                  pl.BlockSpec((None, SSM_CONV, SSM_TC), lambda j: (l, 0, j)),
                  pl.BlockSpec((None, 1, SSM_TC), lambda j: (l, 0, j)), pl.BlockSpec((s, SSM_TC), lambda j: (0, j))],
        out_specs=[pl.BlockSpec((s, SSM_TC), lambda j: (0, j)), pl.BlockSpec((SSM_CONV, SSM_TC), lambda j: (0, j)),
                   pl.BlockSpec((1, SSM_TC), lambda j: (0, j))],
        out_shape=[jax.ShapeDtypeStruct((s, CONV_CH), BF16), jax.ShapeDtypeStruct((SSM_CONV, CONV_CH), F32),
                   jax.ShapeDtypeStruct((1, CONV_CH), F32)],
        compiler_params=_params(dimension_semantics=("arbitrary",)),
    )(proj, cw, cb, dact)


FFN_TC = 256
FFN_NT = D_FF // FFN_TC


def _ffn_specs(s, l):
    blk = pl.BlockSpec((s, FFN_TC), lambda j: (0, j))
    wg = pl.BlockSpec((None, FFN_CONV, FFN_TC), lambda j: (l, 0, j))
    wv = pl.BlockSpec((None, FFN_CONV, FFN_TC), lambda j: (l, 0, FFN_NT + j))
    bg = pl.BlockSpec((None, 1, FFN_TC), lambda j: (l, 0, j))
    bv = pl.BlockSpec((None, 1, FFN_TC), lambda j: (l, 0, FFN_NT + j))
    return blk, wg, wv, bg, bv


def ffn_act_fwd(ug, uv, cw, cb, l):
    s = ug.shape[0]

    def body(g_ref, v_ref, wg_ref, wv_ref, bg_ref, bv_ref, o_ref):
        nchunk = s // CONV_R

        def step(i, carry):
            cg = _conv_ext(_ext_rows(g_ref, i, nchunk, True, False), wg_ref, bg_ref, FFN_CONV)
            cv = _conv_ext(_ext_rows(v_ref, i, nchunk, True, False), wv_ref, bv_ref, FFN_CONV)
            o_ref[pl.ds(pl.multiple_of(i * CONV_R, CONV_R), CONV_R), :] = (_silu(cg) * cv).astype(o_ref.dtype)
            return carry

        lax.fori_loop(0, nchunk, step, 0)

    blk, wg, wv, bg, bv = _ffn_specs(s, l)
    return pl.pallas_call(
        body, name="ffn_act_fwd", grid=(FFN_NT,), in_specs=[blk, blk, wg, wv, bg, bv],
        out_specs=blk, out_shape=jax.ShapeDtypeStruct((s, D_FF), BF16),
        compiler_params=_params(dimension_semantics=("arbitrary",)),
    )(ug, uv, cw, cw, cb, cb)


def ffn_act_bwd(ug, uv, cw, cb, l, da):
    s = ug.shape[0]

    def body(g_ref, v_ref, wg_ref, wv_ref, bg_ref, bv_ref, da_ref, dg_ref, dv_ref, dwg_ref, dwv_ref, dbg_ref, dbv_ref):
        nchunk = s // CONV_R

        def step(i, carry):
            rows = pl.ds(pl.multiple_of(i * CONV_R, CONV_R), CONV_R)
            eg = _ext_rows(g_ref, i, nchunk, True, True)
            ev = _ext_rows(v_ref, i, nchunk, True, True)
            cg = _conv_ext(eg, wg_ref, bg_ref, FFN_CONV)
            cv = _conv_ext(ev, wv_ref, bv_ref, FFN_CONV)
            da_t = _ext_rows(da_ref, i, nchunk, False, True)
            sg = jax.nn.sigmoid(cg)
            dcg = da_t * cv * (sg * (1.0 + cg * (1.0 - sg)))
            dcv = da_t * (cg * sg)
            dg_ref[rows, :] = _conv_t_ext(dcg, wg_ref, FFN_CONV).astype(dg_ref.dtype)
            dv_ref[rows, :] = _conv_t_ext(dcv, wv_ref, FFN_CONV).astype(dv_ref.dtype)
            grads = _conv_wgrad(dcg[:CONV_R], eg, FFN_CONV) + _conv_wgrad(dcv[:CONV_R], ev, FFN_CONV)
            return tuple(c + g for c, g in zip(carry, grads))

        zero = jnp.zeros((1, FFN_TC), F32)
        res = lax.fori_loop(0, nchunk, step, (zero,) * (2 * FFN_CONV + 2))
        _store_wgrad(res[:FFN_CONV + 1], dwg_ref, dbg_ref, FFN_CONV)
        _store_wgrad(res[FFN_CONV + 1:], dwv_ref, dbv_ref, FFN_CONV)

    blk, wg, wv, bg, bv = _ffn_specs(s, l)
    wblk = pl.BlockSpec((FFN_CONV, FFN_TC), lambda j: (0, j))
    bblk = pl.BlockSpec((1, FFN_TC), lambda j: (0, j))
    return pl.pallas_call(
        body, name="ffn_act_bwd", grid=(FFN_NT,), in_specs=[blk, blk, wg, wv, bg, bv, blk],
        out_specs=[blk, blk, wblk, wblk, bblk, bblk],
        out_shape=[jax.ShapeDtypeStruct((s, D_FF), BF16), jax.ShapeDtypeStruct((s, D_FF), BF16),
                   jax.ShapeDtypeStruct((FFN_CONV, D_FF), F32), jax.ShapeDtypeStruct((FFN_CONV, D_FF), F32),
                   jax.ShapeDtypeStruct((1, D_FF), F32), jax.ShapeDtypeStruct((1, D_FF), F32)],
        compiler_params=_params(dimension_semantics=("arbitrary",)),
    )(ug, uv, cw, cw, cb, cb, da)


def _dot(a, b, mode):
    return lax.dot_general(a.astype(BF16), b.astype(BF16), _DIMS[mode], preferred_element_type=F32)


@jax.custom_vjp
def mm_nn(a, b):
    return _dot(a, b, 'nn')


@jax.custom_vjp
def mm_nt(a, b):
    return _dot(a, b, 'nt')


@jax.custom_vjp
def mm_tn(a, b):
    return _dot(a, b, 'tn')


mm_nn.defvjp(lambda a, b: (_dot(a, b, 'nn'), (a, b)), lambda r, g: (_dot(g, r[1], 'nt'), _dot(r[0], g, 'tn')))
mm_nt.defvjp(lambda a, b: (_dot(a, b, 'nt'), (a, b)), lambda r, g: (_dot(g, r[1], 'nn'), _dot(g, r[0], 'tn')))
mm_tn.defvjp(lambda a, b: (_dot(a, b, 'tn'), (a, b)), lambda r, g: (_dot(r[1], g, 'nt'), _dot(r[0], g, 'nn')))


def _tri(n, lower):
    r = lax.broadcasted_iota(jnp.int32, (n, n), 0)
    c = lax.broadcasted_iota(jnp.int32, (n, n), 1)
    return jnp.where((r >= c) if lower else (r <= c), 1.0, 0.0).astype(F32)


def _tri_dot(a, lower):
    return jnp.dot(_tri(a.shape[0], lower), a, precision=lax.Precision.HIGHEST, preferred_element_type=F32)


@jax.custom_vjp
def _cumsum_rows(a):
    return _tri_dot(a, True)


_cumsum_rows.defvjp(lambda a: (_tri_dot(a, True), None), lambda _, g: (_tri_dot(g, False),))


def _softplus(x):
    return jnp.maximum(x, 0.0) + jnp.log(1.0 + jnp.exp(-jnp.abs(x)))


def _ssd_chunk(xs, bs, cs, small, dtb, alog, dsk, prev):
    ln = small.shape[0]
    lane = lax.broadcasted_iota(jnp.int32, (ln, LANES), 1)
    lane1 = lax.broadcasted_iota(jnp.int32, (1, LANES), 1)
    sub = lax.broadcasted_iota(jnp.int32, (LANES, ln), 0)
    rowi = lax.broadcasted_iota(jnp.int32, (ln, LANES), 0)
    tril = lax.broadcasted_iota(jnp.int32, (ln, ln), 0) >= lax.broadcasted_iota(jnp.int32, (ln, ln), 1)
    first = lane < SSM_HEAD_DIM
    first1 = lane1 < SSM_HEAD_DIM

    dt = _softplus(small + dtb)
    acs = _cumsum_rows(dt * (-jnp.exp(alog)))
    acs_t = acs.T
    last = jnp.sum(jnp.where(rowi == ln - 1, acs, 0.0), axis=0, keepdims=True)

    def col(a, h):
        return jnp.sum(jnp.where(lane == h, a, 0.0), axis=1, keepdims=True)

    def one(a, h):
        return jnp.sum(jnp.where(lane1 == h, a, 0.0), axis=1, keepdims=True)

    def rowv(at, h):
        return jnp.sum(jnp.where(sub == h, at, 0.0), axis=0, keepdims=True)

    cb = [mm_nt(cs[g], bs[g]) for g in range(SSM_GROUPS)]
    ys, news = [], []
    for j in range(SSM_HEADS // 2):
        g = j // 2
        h0, h1 = 2 * j, 2 * j + 1
        xd = xs[j] * jnp.where(first, col(dt, h0), col(dt, h1))
        yd, st, ea, cd = None, None, [], []
        for h, xdh in ((h0, jnp.where(first, xd, 0.0)), (h1, jnp.where(first, 0.0, xd))):
            ac = col(acs, h)
            la = one(last, h)
            lmat = jnp.exp(jnp.where(tril, ac - rowv(acs_t, h), -jnp.inf))
            yh = mm_nn(cb[g] * lmat, xdh)
            sh = mm_tn(bs[g] * jnp.exp(la - ac), xdh)
            yd = yh if yd is None else yd + yh
            st = sh if st is None else st + sh
            ea.append(jnp.exp(ac))
            cd.append(jnp.exp(la))
        yoff = mm_nn(cs[g], prev[j]) * jnp.where(first, ea[0], ea[1])
        ys.append(yd + yoff + xs[j] * jnp.where(first1, one(dsk, h0), one(dsk, h1)))
        news.append(prev[j] * jnp.where(first1, cd[0], cd[1]) + st)
    return ys, news


N_PAIR = SSM_HEADS // 2


def ssd_fwd(xbc, proj, ptile, l):
    s = xbc.shape[0]
    nch = s // SSM_CHUNK

    def body(xbc_ref, small_ref, p_ref, y_ref, prev_ref, state_ref):
        @pl.when(pl.program_id(0) == 0)
        def _():
            state_ref[...] = jnp.zeros_like(state_ref)

        xs = [xbc_ref[:, LANES * j:LANES * (j + 1)] for j in range(N_PAIR)]
        bs = [xbc_ref[:, D_SSM + LANES * g:D_SSM + LANES * (g + 1)] for g in range(SSM_GROUPS)]
        cs = [xbc_ref[:, D_SSM + 512 + LANES * g:D_SSM + 512 + LANES * (g + 1)] for g in range(SSM_GROUPS)]
        prev = [state_ref[j] for j in range(N_PAIR)]
        ys, news = _ssd_chunk(xs, bs, cs, small_ref[...], p_ref[0:1, :], p_ref[1:2, :], p_ref[2:3, :], prev)
        for j in range(N_PAIR):
            y_ref[:, LANES * j:LANES * (j + 1)] = ys[j]
            prev_ref[0, j] = prev[j]
            state_ref[j] = news[j]

    return pl.pallas_call(
        body, name="ssd_fwd", grid=(nch,),
        in_specs=[pl.BlockSpec((SSM_CHUNK, CONV_CH), lambda c: (c, 0)),
                  pl.BlockSpec((SSM_CHUNK, LANES), lambda c: (c, OFF_SMALL // LANES)),
                  pl.BlockSpec((None, 8, LANES), lambda c: (l, 0, 0))],
        out_specs=[pl.BlockSpec((SSM_CHUNK, D_SSM), lambda c: (c, 0)),
                   pl.BlockSpec((1, N_PAIR, SSM_STATE, LANES), lambda c: (c, 0, 0, 0))],
        out_shape=[jax.ShapeDtypeStruct((s, D_SSM), F32), jax.ShapeDtypeStruct((nch, N_PAIR, SSM_STATE, LANES), F32)],
        scratch_shapes=[pltpu.VMEM((N_PAIR, SSM_STATE, LANES), F32)],
        compiler_params=_params(dimension_semantics=("arbitrary",)),
    )(xbc, proj, ptile)


def ssd_bwd(xbc, proj, ptile, l, prevs, dy):
    s = xbc.shape[0]
    nch = s // SSM_CHUNK

    def body(xbc_ref, small_ref, p_ref, prev_ref, dy_ref, dxbc_ref, dsmall_ref, dp_ref, dstate_ref):
        @pl.when(pl.program_id(0) == 0)
        def _():
            dstate_ref[...] = jnp.zeros_like(dstate_ref)
            dp_ref[...] = jnp.zeros_like(dp_ref)

        xs = [xbc_ref[:, LANES * j:LANES * (j + 1)] for j in range(N_PAIR)]
        bs = [xbc_ref[:, D_SSM + LANES * g:D_SSM + LANES * (g + 1)] for g in range(SSM_GROUPS)]
        cs = [xbc_ref[:, D_SSM + 512 + LANES * g:D_SSM + 512 + LANES * (g + 1)] for g in range(SSM_GROUPS)]
        prev = [prev_ref[0, j] for j in range(N_PAIR)]
        dys = [dy_ref[:, LANES * j:LANES * (j + 1)] for j in range(N_PAIR)]
        dnew = [dstate_ref[j] for j in range(N_PAIR)]
        _, vjp = jax.vjp(_ssd_chunk, xs, bs, cs, small_ref[...], p_ref[0:1, :], p_ref[1:2, :], p_ref[2:3, :], prev)
        dxs, dbs, dcs, dsmall, ddtb, dalog, ddsk, dprev = vjp((dys, dnew))
        for j in range(N_PAIR):
            dxbc_ref[:, LANES * j:LANES * (j + 1)] = dxs[j]
            dstate_ref[j] = dprev[j]
        for g in range(SSM_GROUPS):
            dxbc_ref[:, D_SSM + LANES * g:D_SSM + LANES * (g + 1)] = dbs[g]
            dxbc_ref[:, D_SSM + 512 + LANES * g:D_SSM + 512 + LANES * (g + 1)] = dcs[g]
        dsmall_ref[...] = dsmall
        dp_ref[0:1, :] += ddtb
        dp_ref[1:2, :] += dalog
        dp_ref[2:3, :] += ddsk

    rev = lambda c: nch - 1 - c
    return pl.pallas_call(
        body, name="ssd_bwd", grid=(nch,),
        in_specs=[pl.BlockSpec((SSM_CHUNK, CONV_CH), lambda c: (rev(c), 0)),
                  pl.BlockSpec((SSM_CHUNK, LANES), lambda c: (rev(c), OFF_SMALL // LANES)),
                  pl.BlockSpec((None, 8, LANES), lambda c: (l, 0, 0)),
                  pl.BlockSpec((1, N_PAIR, SSM_STATE, LANES), lambda c: (rev(c), 0, 0, 0)),
                  pl.BlockSpec((SSM_CHUNK, D_SSM), lambda c: (rev(c), 0))],
        out_specs=[pl.BlockSpec((SSM_CHUNK, CONV_CH), lambda c: (rev(c), 0)),
                   pl.BlockSpec((SSM_CHUNK, LANES), lambda c: (rev(c), 0)),
                   pl.BlockSpec((8, LANES), lambda c: (0, 0))],
        out_shape=[jax.ShapeDtypeStruct((s, CONV_CH), F32), jax.ShapeDtypeStruct((s, LANES), F32),
                   jax.ShapeDtypeStruct((8, LANES), F32)],
        scratch_shapes=[pltpu.VMEM((N_PAIR, SSM_STATE, LANES), F32)],
        compiler_params=_params(dimension_semantics=("arbitrary",)),
    )(xbc, proj, ptile, prevs, dy)


ROPE_TM = 256


def _rope_tile(t, cosm, sinm):
    lane = lax.broadcasted_iota(jnp.int32, t.shape, 1)
    half = QK_ROPE // 2
    partner = jnp.where(lane < ROPE_LANE0 + half, pltpu.roll(t, LANES - half, 1), pltpu.roll(t, half, 1))
    return t * cosm + partner * sinm


def _in_rope(shape):
    lane = lax.broadcasted_iota(jnp.int32, shape, 1)
    return jnp.logical_and(lane >= ROPE_LANE0, lane < ROPE_LANE0 + QK_ROPE)


def build_k(kn, proj, cosm, sinm):
    s, w = kn.shape

    def body(k_ref, small_ref, c_ref, s_ref, o_ref):
        small = small_ref[...]
        inrope = _in_rope(small.shape)
        kpe = jnp.where(inrope, _rope_tile(jnp.where(inrope, small, 0.0), c_ref[...], s_ref[...]), 0.0)
        for h in range(MLA_HEADS):
            sl = slice(HEAD_PAD * h, HEAD_PAD * (h + 1))
            o_ref[:, sl] = (k_ref[:, sl].astype(F32) + kpe).astype(o_ref.dtype)

    row = pl.BlockSpec((ROPE_TM, w), lambda i: (i, 0))
    tab = pl.BlockSpec((ROPE_TM, LANES), lambda i: (i, 0))
    return pl.pallas_call(
        body, name="build_k", grid=(s // ROPE_TM,),
        in_specs=[row, pl.BlockSpec((ROPE_TM, LANES), lambda i: (i, OFF_SMALL // LANES)), tab, tab], out_specs=row,
        out_shape=jax.ShapeDtypeStruct((s, w), BF16), compiler_params=_params(dimension_semantics=("arbitrary",)),
    )(kn, proj, cosm, sinm)


def dsmall_bwd(dk, dsmall_ssd, cosm, sinm_neg):
    def fn(dkt, ds, c, sn):
        inrope = _in_rope(ds.shape)
        tot = dkt[:, 0:HEAD_PAD]
        for h in range(1, MLA_HEADS):
            tot = tot + dkt[:, HEAD_PAD * h:HEAD_PAD * (h + 1)]
        tot = jnp.where(inrope, tot, 0.0)
        return ds + jnp.where(inrope, _rope_tile(tot, c, sn), 0.0)

    return rowwise(fn, [(dk, MLA_HEADS * HEAD_PAD, 0), (dsmall_ssd, LANES, 0), (cosm, LANES, 0), (sinm_neg, LANES, 0)],
                   [], [(LANES, BF16)], [], "dsmall_bwd")[0]


ATT_TQ = 512
ATT_SCALE = (QK_NOPE + QK_ROPE) ** -0.5


def _att_scores(qh, kh, q0):
    s = lax.dot_general(qh, kh, _DIMS['nt'], preferred_element_type=F32) * ATT_SCALE
    r = lax.broadcasted_iota(jnp.int32, s.shape, 0) + q0
    c = lax.broadcasted_iota(jnp.int32, s.shape, 1)
    return jnp.where(c <= r, s, -1e30)


def mla_fwd(q, k, v):
    s = q.shape[0]

    def body(q_ref, k_ref, v_ref, o_ref, lse_ref):
        lane = lax.broadcasted_iota(jnp.int32, (ATT_TQ, LANES), 1)

        def block(ib):
            n = ATT_TQ * (ib + 1)
            v_t = v_ref[0:n, :]
            vlane = lax.broadcasted_iota(jnp.int32, v_t.shape, 1)
            o_tot, lse_tot = None, None
            for h in range(2):
                hs = slice(HEAD_PAD * h, HEAD_PAD * (h + 1))
                sc = _att_scores(q_ref[:, hs], k_ref[0:n, hs], ATT_TQ * ib)
                m = jnp.max(sc, axis=1, keepdims=True)
                p = jnp.exp(sc - m)
                l = jnp.sum(p, axis=1, keepdims=True)
                vh = jnp.where((vlane < V_DIM) if h == 0 else (vlane >= V_DIM), v_t, jnp.zeros_like(v_t))
                oh = lax.dot_general(p.astype(BF16), vh, _DIMS['nn'], preferred_element_type=F32) / l
                lse_h = jnp.where((lane < V_DIM) if h == 0 else (lane >= V_DIM), m + jnp.log(l), 0.0)
                o_tot = oh if o_tot is None else o_tot + oh
                lse_tot = lse_h if lse_tot is None else lse_tot + lse_h
            o_ref[...] = o_tot
            lse_ref[...] = lse_tot

        for ib in range(s // ATT_TQ):
            pl.when(pl.program_id(1) == ib)(functools.partial(block, ib))

    tile = pl.BlockSpec((ATT_TQ, LANES), lambda p, i: (i, p))
    return pl.pallas_call(
        body, name="mla_fwd", grid=(MLA_HEADS // 2, s // ATT_TQ),
        in_specs=[pl.BlockSpec((ATT_TQ, 2 * HEAD_PAD), lambda p, i: (i, p)),
                  pl.BlockSpec((s, 2 * HEAD_PAD), lambda p, i: (0, p)),
                  pl.BlockSpec((s, LANES), lambda p, i: (0, p))],
        out_specs=[tile, tile],
        out_shape=[jax.ShapeDtypeStruct((s, MLA_HEADS * V_DIM), F32)] * 2,
        compiler_params=_params(dimension_semantics=("arbitrary", "arbitrary")),
    )(q, k, v)


def mla_bwd(q, k, v, o, lse, do, cosm, sinm_neg):
    s = q.shape[0]

    def body(q_ref, k_ref, v_ref, o_ref, lse_ref, do_ref, c_ref, s_ref, dq_ref, dk_ref, dv_ref):
        i = pl.program_id(1)

        @pl.when(i == 0)
        def _():
            dk_ref[...] = jnp.zeros_like(dk_ref)
            dv_ref[...] = jnp.zeros_like(dv_ref)

        def block(ib):
            n = ATT_TQ * (ib + 1)
            o_t = o_ref[...]
            do_t = do_ref[...]
            lse_t = lse_ref[...]
            v_t = v_ref[0:n, :]
            lane = lax.broadcasted_iota(jnp.int32, do_t.shape, 1)
            for h in range(2):
                hs = slice(HEAD_PAD * h, HEAD_PAD * (h + 1))
                sel = (lane < V_DIM) if h == 0 else (lane >= V_DIM)
                qh = q_ref[:, hs]
                kh = k_ref[0:n, hs]
                doh = jnp.where(sel, do_t, 0.0)
                delta = jnp.sum(doh * o_t, axis=1, keepdims=True)
                lse_h = jnp.max(jnp.where(sel, lse_t, -jnp.inf), axis=1, keepdims=True)
                doh_b = doh.astype(BF16)
                p = jnp.exp(_att_scores(qh, kh, ATT_TQ * ib) - lse_h)
                dv_ref[0:n, :] += lax.dot_general(p.astype(BF16), doh_b, _DIMS['tn'], preferred_element_type=F32)
                dp = lax.dot_general(doh_b, v_t, _DIMS['nt'], preferred_element_type=F32)
                ds = (p * (dp - delta) * ATT_SCALE).astype(BF16)
                dk_ref[0:n, hs] += lax.dot_general(ds, qh, _DIMS['tn'], preferred_element_type=F32)
                dq = lax.dot_general(ds, kh, _DIMS['nn'], preferred_element_type=F32)
                dq_ref[:, hs] = _rope_tile(dq, c_ref[...], s_ref[...]).astype(dq_ref.dtype)

        for ib in range(s // ATT_TQ):
            pl.when(i == ib)(functools.partial(block, ib))

    tile = pl.BlockSpec((ATT_TQ, LANES), lambda p, i: (i, p))
    return pl.pallas_call(
        body, name="mla_bwd", grid=(MLA_HEADS // 2, s // ATT_TQ),
        in_specs=[pl.BlockSpec((ATT_TQ, 2 * HEAD_PAD), lambda p, i: (i, p)),
                  pl.BlockSpec((s, 2 * HEAD_PAD), lambda p, i: (0, p)),
                  pl.BlockSpec((s, LANES), lambda p, i: (0, p)), tile, tile, tile,
                  pl.BlockSpec((ATT_TQ, LANES), lambda p, i: (i, 0)), pl.BlockSpec((ATT_TQ, LANES), lambda p, i: (i, 0))],
        out_specs=[pl.BlockSpec((ATT_TQ, 2 * HEAD_PAD), lambda p, i: (i, p)),
                   pl.BlockSpec((s, 2 * HEAD_PAD), lambda p, i: (0, p)),
                   pl.BlockSpec((s, LANES), lambda p, i: (0, p))],
        out_shape=[jax.ShapeDtypeStruct((s, MLA_HEADS * HEAD_PAD), BF16),
                   jax.ShapeDtypeStruct((s, MLA_HEADS * HEAD_PAD), F32),
                   jax.ShapeDtypeStruct((s, MLA_HEADS * V_DIM), F32)],
        compiler_params=_params(dimension_semantics=("arbitrary", "arbitrary")),
    )(q, k, v, o, lse, do, cosm, sinm_neg)


MEM_TQ = 256
MEM_SCALE = MEM_HEAD_DIM ** -0.5


def _mem_probs(qh, kh):
    s = lax.dot_general(qh, kh, _DIMS['nt'], preferred_element_type=F32) * MEM_SCALE
    p = jnp.exp(s - jnp.max(s, axis=1, keepdims=True))
    return p / jnp.sum(p, axis=1, keepdims=True)


def mem_fwd(q, k, v):
    s = q.shape[0]

    def body(q_ref, k_ref, v_ref, o_ref):
        for h in range(MEM_HEADS):
            sl = slice(MEM_HEAD_DIM * h, MEM_HEAD_DIM * (h + 1))
            p = _mem_probs(q_ref[:, sl], k_ref[:, sl])
            o_ref[:, sl] = lax.dot_general(p.astype(BF16), v_ref[:, sl], _DIMS['nn'],
                                           preferred_element_type=F32).astype(o_ref.dtype)

    full = pl.BlockSpec((MEM_LEN, D_MODEL), lambda i: (0, 0))
    return pl.pallas_call(
        body, name="mem_fwd", grid=(s // MEM_TQ,),
        in_specs=[pl.BlockSpec((MEM_TQ, D_MODEL), lambda i: (i, 0)), full, full],
        out_specs=pl.BlockSpec((MEM_TQ, D_MODEL), lambda i: (i, 0)),
        out_shape=jax.ShapeDtypeStruct((s, D_MODEL), BF16),
        compiler_params=_params(dimension_semantics=("arbitrary",)),
    )(q, k, v)


def mem_bwd(q, k, v, do):
    s = q.shape[0]

    def body(q_ref, k_ref, v_ref, do_ref, dq_ref, dk_ref, dv_ref):
        @pl.when(pl.program_id(0) == 0)
        def _():
            dk_ref[...] = jnp.zeros_like(dk_ref)
            dv_ref[...] = jnp.zeros_like(dv_ref)

        for h in range(MEM_HEADS):
            sl = slice(MEM_HEAD_DIM * h, MEM_HEAD_DIM * (h + 1))
            qh, kh, vh = q_ref[:, sl], k_ref[:, sl], v_ref[:, sl]
            doh = do_ref[:, sl].astype(BF16)
            p = _mem_probs(qh, kh)
            dv_ref[:, sl] += lax.dot_general(p.astype(BF16), doh, _DIMS['tn'], preferred_element_type=F32)
            dp = lax.dot_general(doh, vh, _DIMS['nt'], preferred_element_type=F32)
            ds = (p * (dp - jnp.sum(p * dp, axis=1, keepdims=True)) * MEM_SCALE).astype(BF16)
            dq_ref[:, sl] = lax.dot_general(ds, kh, _DIMS['nn'], preferred_element_type=F32).astype(dq_ref.dtype)
            dk_ref[:, sl] += lax.dot_general(ds, qh, _DIMS['tn'], preferred_element_type=F32)

    full = pl.BlockSpec((MEM_LEN, D_MODEL), lambda i: (0, 0))
    row = pl.BlockSpec((MEM_TQ, D_MODEL), lambda i: (i, 0))
    return pl.pallas_call(
        body, name="mem_bwd", grid=(s // MEM_TQ,),
        in_specs=[row, full, full, row], out_specs=[row, full, full],
        out_shape=[jax.ShapeDtypeStruct((s, D_MODEL), BF16), jax.ShapeDtypeStruct((MEM_LEN, D_MODEL), F32),
                   jax.ShapeDtypeStruct((MEM_LEN, D_MODEL), F32)],
        compiler_params=_params(dimension_semantics=("arbitrary",)),
    )(q, k, v, do)


def _gate_norm(y, z, g):
    return _rms(y * _silu(z), g)


def gate_norm_fwd(y, proj, g):
    return rowwise(_gate_norm, [(y, D_SSM, 0), (proj, D_SSM, OFF_Z // D_SSM)], [g], [(D_SSM, BF16, D_MIX, 0)], [],
                   "gate_norm_fwd")[0]


def gate_norm_bwd(y, proj, g, dmix, tie=None):
    def fn(yt, zt, dt_, gt):
        _, vjp = jax.vjp(_gate_norm, yt, zt, gt)
        return vjp(dt_.astype(F32))

    return rowwise(fn, [(y, D_SSM, 0), (proj, D_SSM, OFF_Z // D_SSM), (dmix, D_SSM, 0)], [g],
                   [(D_SSM, F32), (D_SSM, BF16)], [((1, D_SSM), F32)], "gate_norm_bwd", tie=tie)


def loss_head(x, g, target):
    def fn(xt, tt, gt):
        def f(x_, g_):
            err = _rms(x_, g_) - tt
            return 0.5 * jnp.sum(jnp.mean(err * err, axis=-1))

        lv, (dx, dg) = jax.value_and_grad(f, argnums=(0, 1))(xt, gt)
        return dx, dg, jnp.full((1, LANES), lv, F32)

    return rowwise(fn, [(x, D_MODEL, 0), (target, D_MODEL, 0)], [g], [(D_MODEL, F32)],
                   [((1, D_MODEL), F32), ((1, LANES), F32)], "loss_head")


def _proj_runs(d):
    lo, hi = (D_IN // N_DEV) * d, (D_IN // N_DEV) * (d + 1)
    runs = []
    for a, b, new in PROJ_SEGS:
        s0, s1 = max(a, lo), min(b, hi)
        if s0 < s1:
            runs.append((s0 - lo, new + s0 - a, s1 - s0))
    return runs


LAYOUT_TM = 256


def assemble_proj(g):
    def body(g_ref, o_ref):
        o_ref[:, OFF_SMALL:OFF_SMALL + LANES] = jnp.zeros((LAYOUT_TM, LANES), o_ref.dtype)
        for d in range(N_DEV):
            for src, dst, n in _proj_runs(d):
                o_ref[:, dst:dst + n] = g_ref[d, :, src:src + n]

    return pl.pallas_call(
        body, name="assemble_proj", grid=(D_MODEL // LAYOUT_TM,),
        in_specs=[pl.BlockSpec((N_DEV, LAYOUT_TM, D_IN // N_DEV), lambda i: (0, i, 0))],
        out_specs=pl.BlockSpec((LAYOUT_TM, PROJ_W), lambda i: (i, 0)),
        out_shape=jax.ShapeDtypeStruct((D_MODEL, PROJ_W), g.dtype),
        compiler_params=_params(dimension_semantics=("arbitrary",)),
    )(g)


def extract_proj(dz, dxbc, dcq, dsmall, dckv):
    pieces = [(OFF_Z, 1024), (OFF_XBC, 2048), (OFF_CQ, Q_LORA), (OFF_SMALL, LANES), (OFF_CKV, KV_LORA)]

    def body(*refs):
        o_ref = refs[-1]
        for d in range(N_DEV):
            for src, dst, n in _proj_runs(d):
                for p, (off, w) in enumerate(pieces):
                    if off <= dst < off + w:
                        o_ref[d, :, src:src + n] = refs[p][:, dst - off:dst - off + n].astype(o_ref.dtype)

    return pl.pallas_call(
        body, name="extract_proj", grid=(D_MODEL // LAYOUT_TM,),
        in_specs=[pl.BlockSpec((LAYOUT_TM, w), lambda i: (i, 0)) for _, w in pieces],
        out_specs=pl.BlockSpec((N_DEV, LAYOUT_TM, D_IN // N_DEV), lambda i: (0, i, 0)),
        out_shape=jax.ShapeDtypeStruct((N_DEV, D_MODEL, D_IN // N_DEV), BF16),
        compiler_params=_params(dimension_semantics=("arbitrary",)),
    )(dz, dxbc, dcq, dsmall, dckv)


_QW = QK_NOPE + QK_ROPE


def assemble_uq(g):
    def body(g_ref, o_ref):
        o_ref[...] = jnp.zeros_like(o_ref)
        for d in range(N_DEV):
            for e in range(2):
                dst = HEAD_PAD * (2 * d + e)
                o_ref[:, dst:dst + _QW] = g_ref[d, :, _QW * e:_QW * (e + 1)]

    return pl.pallas_call(
        body, name="assemble_uq", grid=(1,),
        in_specs=[pl.BlockSpec((N_DEV, Q_LORA, 2 * _QW), lambda i: (0, 0, 0))],
        out_specs=pl.BlockSpec((Q_LORA, MLA_HEADS * HEAD_PAD), lambda i: (0, 0)),
        out_shape=jax.ShapeDtypeStruct((Q_LORA, MLA_HEADS * HEAD_PAD), g.dtype),
        compiler_params=_params(dimension_semantics=("arbitrary",)),
    )(g)


def extract_uq(dw):
    def body(w_ref, o_ref):
        for d in range(N_DEV):
            for e in range(2):
                src = HEAD_PAD * (2 * d + e)
                o_ref[d, :, _QW * e:_QW * (e + 1)] = w_ref[:, src:src + _QW].astype(o_ref.dtype)

    return pl.pallas_call(
        body, name="extract_uq", grid=(1,),
        in_specs=[pl.BlockSpec((Q_LORA, MLA_HEADS * HEAD_PAD), lambda i: (0, 0))],
        out_specs=pl.BlockSpec((N_DEV, Q_LORA, 2 * _QW), lambda i: (0, 0, 0)),
        out_shape=jax.ShapeDtypeStruct((N_DEV, Q_LORA, 2 * _QW), BF16),
        compiler_params=_params(dimension_semantics=("arbitrary",)),
    )(dw)


def assemble_ukv(g):
    def body(g_ref, kn_ref, v_ref):
        kn_ref[...] = jnp.zeros_like(kn_ref)
        for d in range(N_DEV):
            for e in range(2):
                h = 2 * d + e
                kn_ref[:, HEAD_PAD * h:HEAD_PAD * h + QK_NOPE] = g_ref[d, :, 128 * e:128 * e + QK_NOPE]
                v_ref[:, V_DIM * h:V_DIM * (h + 1)] = g_ref[d, :, 128 * e + QK_NOPE:128 * (e + 1)]

    return pl.pallas_call(
        body, name="assemble_ukv", grid=(1,),
        in_specs=[pl.BlockSpec((N_DEV, KV_LORA, 256), lambda i: (0, 0, 0))],
        out_specs=[pl.BlockSpec((KV_LORA, MLA_HEADS * HEAD_PAD), lambda i: (0, 0)),
                   pl.BlockSpec((KV_LORA, MLA_HEADS * V_DIM), lambda i: (0, 0))],
        out_shape=[jax.ShapeDtypeStruct((KV_LORA, MLA_HEADS * HEAD_PAD), g.dtype),
                   jax.ShapeDtypeStruct((KV_LORA, MLA_HEADS * V_DIM), g.dtype)],
        compiler_params=_params(dimension_semantics=("arbitrary",)),
    )(g)


def extract_ukv(dkn, dv):
    def body(kn_ref, v_ref, o_ref):
        for d in range(N_DEV):
            for e in range(2):
                h = 2 * d + e
                o_ref[d, :, 128 * e:128 * e + QK_NOPE] = kn_ref[:, HEAD_PAD * h:HEAD_PAD * h + QK_NOPE].astype(o_ref.dtype)
                o_ref[d, :, 128 * e + QK_NOPE:128 * (e + 1)] = v_ref[:, V_DIM * h:V_DIM * (h + 1)].astype(o_ref.dtype)

    return pl.pallas_call(
        body, name="extract_ukv", grid=(1,),
        in_specs=[pl.BlockSpec((KV_LORA, MLA_HEADS * HEAD_PAD), lambda i: (0, 0)),
                  pl.BlockSpec((KV_LORA, MLA_HEADS * V_DIM), lambda i: (0, 0))],
        out_specs=pl.BlockSpec((N_DEV, KV_LORA, 256), lambda i: (0, 0, 0)),
        out_shape=jax.ShapeDtypeStruct((N_DEV, KV_LORA, 256), BF16),
        compiler_params=_params(dimension_semantics=("arbitrary",)),
    )(dkn, dv)


_UPW = 2 * D_FF // N_DEV


def assemble_up(g):
    def body(g_ref, wg_ref, wv_ref):
        for d in range(N_DEV):
            ref = wg_ref if d < N_DEV // 2 else wv_ref
            off = _UPW * (d % (N_DEV // 2))
            ref[:, off:off + _UPW] = g_ref[d]

    half = pl.BlockSpec((LAYOUT_TM, D_FF), lambda i: (i, 0))
    return pl.pallas_call(
        body, name="assemble_up", grid=(D_MODEL // LAYOUT_TM,),
        in_specs=[pl.BlockSpec((N_DEV, LAYOUT_TM, _UPW), lambda i: (0, i, 0))],
        out_specs=[half, half], out_shape=[jax.ShapeDtypeStruct((D_MODEL, D_FF), g.dtype)] * 2,
        compiler_params=_params(dimension_semantics=("arbitrary",)),
    )(g)


def extract_up(dwg, dwv):
    def body(wg_ref, wv_ref, o_ref):
        for d in range(N_DEV):
            ref = wg_ref if d < N_DEV // 2 else wv_ref
            off = _UPW * (d % (N_DEV // 2))
            o_ref[d] = ref[:, off:off + _UPW].astype(o_ref.dtype)

    half = pl.BlockSpec((LAYOUT_TM, D_FF), lambda i: (i, 0))
    return pl.pallas_call(
        body, name="extract_up", grid=(D_MODEL // LAYOUT_TM,), in_specs=[half, half],
        out_specs=pl.BlockSpec((N_DEV, LAYOUT_TM, _UPW), lambda i: (0, i, 0)),
        out_shape=jax.ShapeDtypeStruct((N_DEV, D_MODEL, _UPW), BF16),
        compiler_params=_params(dimension_semantics=("arbitrary",)),
    )(dwg, dwv)


MESH = pl.DeviceIdType.MESH
ANY = pl.BlockSpec(memory_space=pl.ANY)


def _place():
    mx, my, mc = lax.axis_index("x"), lax.axis_index("y"), lax.axis_index("c")
    return mx, my, mc, [(1 - mx, my), (mx, 1 - my), (1 - mx, 1 - my)]


def all_gather_blocks(xs, first_only=()):
    n = len(xs)

    def body(*refs):
        x_refs, out_refs = refs[:n], refs[n:2 * n]
        send_sems, recv_sems, local_sems = refs[2 * n:]
        mx, my, mc, chips = _place()
        me, sibling = (mx, my, mc), (mx, my, 1 - mc)
        x_refs = [x_refs[t].at[0] if t in first_only else x_refs[t] for t in range(n)]

        def rows(t, px, py, pc):
            dev = 4 * px + 2 * py + pc
            return out_refs[t].at[dev] if t in first_only else out_refs[t].at[:, dev]

        def copy(t, k, block, to, src=None):
            return pltpu.make_async_remote_copy(
                src_ref=rows(t, *block) if src is None else src, dst_ref=rows(t, *block),
                send_sem=send_sems.at[t, k], recv_sem=recv_sems.at[t, k], device_id=to, device_id_type=MESH)

        mine = [pltpu.make_async_copy(x_refs[t], rows(t, *me), local_sems.at[t]) for t in range(n)]
        for cp in mine:
            cp.start()
        first = []
        for t in range(n):
            first.append(copy(t, 0, me, sibling, src=x_refs[t]))
            first += [copy(t, 1 + j, me, (*chip, mc), src=x_refs[t]) for j, chip in enumerate(chips)]
        for cp in first:
            cp.start()
        passed = []
        for j, chip in enumerate(chips):
            for t in range(n):
                copy(t, 1 + j, (*chip, mc), me).wait_recv()
                cp = copy(t, 4 + j, (*chip, mc), sibling)
                cp.start()
                passed.append(cp)
        for t in range(n):
            copy(t, 0, sibling, me).wait_recv()
            for j, chip in enumerate(chips):
                copy(t, 4 + j, (*chip, 1 - mc), me).wait_recv()
        for cp in first + passed:
            cp.wait_send()
        for cp in mine:
            cp.wait()

    return pl.pallas_call(
        body, name="all_gather_blocks",
        out_shape=[jax.ShapeDtypeStruct(((N_DEV,) if t in first_only else (x.shape[0], N_DEV)) + x.shape[1:], x.dtype)
                   for t, x in enumerate(xs)],
        in_specs=[ANY] * n, out_specs=[ANY] * n,
        scratch_shapes=[pltpu.SemaphoreType.DMA((n, 7)), pltpu.SemaphoreType.DMA((n, 7)), pltpu.SemaphoreType.DMA((n,))],
    )(*xs)


HBM = pl.BlockSpec(memory_space=pltpu.HBM)
SEM = pl.BlockSpec(memory_space=pltpu.SEMAPHORE)
EFFECT = pltpu.SideEffectType.DATAFLOW_SIDE_EFFECTING
ALL_DEVICES = [(px, py, pc) for px in range(2) for py in range(2) for pc in range(2)]


def _hbm(x):
    return pltpu.with_memory_space_constraint(x, pltpu.HBM)


def _split_start(body, name, srcs, lands, after=None):
    ns, n = len(srcs), len(lands)
    extra = [after] if after is not None else []

    def full_body(*refs):
        sems = ns + n + len(extra)
        body(refs[:ns], refs[ns:ns + n], refs[sems], refs[sems + 1])
        refs[-1][...] = jnp.zeros_like(refs[-1])

    res = pl.pallas_call(
        full_body, name=name,
        out_shape=(pltpu.SemaphoreType.DMA((n,)), pltpu.SemaphoreType.DMA((n,)),
                   *[pltpu.HBM(x.shape, x.dtype) for x in srcs], *[pltpu.HBM(x.shape, x.dtype) for x in lands],
                   jax.ShapeDtypeStruct((8, LANES), F32)),
        in_specs=[HBM] * (ns + n) + [ANY] * len(extra),
        out_specs=(SEM, SEM, *[HBM] * (ns + n), pl.BlockSpec(memory_space=pltpu.VMEM)),
        input_output_aliases={i: 2 + i for i in range(ns + n)},
        compiler_params=pltpu.CompilerParams(has_side_effects=EFFECT),
    )(*[_hbm(x) for x in srcs], *[_hbm(x) for x in lands], *extra)
    return res[0], res[1], list(res[2:2 + ns]), list(res[2 + ns:2 + ns + n]), res[-1]


def _split_wait(name, send_sems, recv_sems, srcs, lands, after, sent, landed):
    ns, n = len(srcs), len(lands)

    def body(*refs):
        src_refs, land_refs, ssem, rsem = refs[:ns], refs[ns:ns + n], refs[ns + n], refs[ns + n + 1]
        mx, my, mc, _ = _place()
        for t in range(n):
            out = sent(src_refs[t] if ns else None, land_refs[t])
            inn = landed(land_refs[t])
            pltpu.make_async_remote_copy(src_ref=out, dst_ref=out, send_sem=ssem.at[t], recv_sem=rsem.at[t],
                                         device_id=(mx, my, mc), device_id_type=MESH).wait_send()
            pltpu.make_async_remote_copy(src_ref=inn, dst_ref=inn, send_sem=ssem.at[t], recv_sem=rsem.at[t],
                                         device_id=(mx, my, mc), device_id_type=MESH).wait_recv()

    res = pl.pallas_call(
        body, name=name,
        out_shape=(*[pltpu.HBM(x.shape, x.dtype) for x in srcs], *[pltpu.HBM(x.shape, x.dtype) for x in lands]),
        in_specs=[HBM] * (ns + n) + [SEM, SEM, ANY], out_specs=[HBM] * (ns + n),
        input_output_aliases={i: i for i in range(ns + n)},
        compiler_params=pltpu.CompilerParams(has_side_effects=EFFECT),
    )(*srcs, *lands, send_sems, recv_sems, after)
    return list(res[:ns]), list(res[ns:])


FIRST_HOP = 5
SECOND_HOP = 3


def gather_start(srcs, l, tag, after=None):
    lands = [lax.empty((N_DEV,) + x.shape[1:], x.dtype) for x in srcs]

    def body(src_refs, land_refs, send_sems, recv_sems):
        mx, my, mc, chips = _place()
        me = 4 * mx + 2 * my + mc
        for t in range(len(srcs)):
            for to in [(mx, my, mc), (mx, my, 1 - mc)] + [(cx, cy, mc) for cx, cy in chips]:
                pltpu.make_async_remote_copy(
                    src_ref=src_refs[t].at[l], dst_ref=land_refs[t].at[me], send_sem=send_sems.at[t],
                    recv_sem=recv_sems.at[t], device_id=to, device_id_type=MESH).start()

    return _split_start(body, "gather_start_%d%s" % (l, tag), srcs, lands, after=after)


def gather_wait(l, tag, send_sems, recv_sems, srcs, lands, after):
    hop = lambda d: d.at[pl.ds(0, FIRST_HOP)]
    return _split_wait("gather_wait_%d%s" % (l, tag), send_sems, recv_sems, srcs, lands, after,
                       sent=lambda s, d: hop(d), landed=hop)


def gather_pass_start(lands, l, tag):
    def body(src_refs, land_refs, send_sems, recv_sems):
        mx, my, mc, chips = _place()
        for t in range(len(lands)):
            for cx, cy in chips:
                slot = land_refs[t].at[4 * cx + 2 * cy + mc]
                pltpu.make_async_remote_copy(
                    src_ref=slot, dst_ref=slot, send_sem=send_sems.at[t], recv_sem=recv_sems.at[t],
                    device_id=(mx, my, 1 - mc), device_id_type=MESH).start()

    send_sems, recv_sems, _, lands, tie = _split_start(body, "gather_pass_start_%d%s" % (l, tag), [], lands)
    return send_sems, recv_sems, lands, tie


def gather_pass_wait(l, tag, send_sems, recv_sems, lands, after):
    hop = lambda d: d.at[pl.ds(0, SECOND_HOP)]
    return _split_wait("gather_pass_wait_%d%s" % (l, tag), send_sems, recv_sems, [], lands, after,
                       sent=lambda s, d: hop(d), landed=hop)[1]


def small_gather_start(rows):
    def body(src_refs, land_refs, send_sems, recv_sems):
        mx, my, mc, _ = _place()
        for to in ALL_DEVICES:
            pltpu.make_async_remote_copy(
                src_ref=src_refs[0], dst_ref=land_refs[0].at[4 * mx + 2 * my + mc], send_sem=send_sems.at[0],
                recv_sem=recv_sems.at[0], device_id=to, device_id_type=MESH).start()

    return _split_start(body, "small_gather_start", [rows], [lax.empty((N_DEV,) + rows.shape, rows.dtype)])


def small_gather_wait(send_sems, recv_sems, srcs, lands, after):
    return _split_wait("small_gather_wait", send_sems, recv_sems, srcs, lands, after,
                       sent=lambda s, d: d, landed=lambda d: d)[1][0]


def grad_exchange_start(es, lands, l, tag, after=None):
    def body(e_refs, land_refs, send_sems, recv_sems):
        mx, my, mc, _ = _place()
        me = 4 * mx + 2 * my + mc
        for t in range(len(es)):
            for px, py, pc in ALL_DEVICES:
                pltpu.make_async_remote_copy(
                    src_ref=e_refs[t].at[4 * px + 2 * py + pc], dst_ref=land_refs[t].at[l, me], send_sem=send_sems.at[t],
                    recv_sem=recv_sems.at[t], device_id=(px, py, pc), device_id_type=MESH).start()

    return _split_start(body, "grad_exchange_start_%d%s" % (l, tag), es, lands, after=after)


def grad_exchange_wait(l, tag, send_sems, recv_sems, es, lands, after):
    return _split_wait("grad_exchange_wait_%d%s" % (l, tag), send_sems, recv_sems, es, lands, after,
                       sent=lambda s, d: s, landed=lambda d: d.at[l])


def _adam(g, w, m, v):
    nm = ADAM_B1 * m + (1.0 - ADAM_B1) * g
    nv = ADAM_B2 * v + (1.0 - ADAM_B2) * jnp.square(g)
    m_hat = nm / (1.0 - ADAM_B1 ** ADAM_STEP)
    v_hat = nv / (1.0 - ADAM_B2 ** ADAM_STEP)
    return -ADAM_LR * (m_hat / (jnp.sqrt(v_hat) + ADAM_EPS) + ADAM_WD * w), nm, nv


def adamw_big(parts, w, m, v, name, tie):
    depth, _, a, b = parts.shape
    ta = _row_tile(a)

    def body(p_ref, w_ref, m_ref, v_ref, tie_ref, g_ref, d_ref, nm_ref, nv_ref):
        g = p_ref[0].astype(F32)
        for k in range(1, N_DEV):
            g = g + p_ref[k].astype(F32)
        g_ref[...] = g
        d_ref[...], nm_ref[...], nv_ref[...] = _adam(g, w_ref[...], m_ref[...], v_ref[...])

    blk = pl.BlockSpec((None, ta, b), lambda l, i: (l, i, 0))
    return pl.pallas_call(
        body, name=name, grid=(depth, a // ta),
        in_specs=[pl.BlockSpec((None, N_DEV, ta, b), lambda l, i: (l, 0, i, 0)), blk, blk, blk, ANY], out_specs=[blk] * 4,
        out_shape=[jax.ShapeDtypeStruct((depth, a, b), F32)] * 4,
        compiler_params=_params(dimension_semantics=("arbitrary", "arbitrary")),
    )(parts, w, m, v, tie)


SMALL_VIEW = {'norm_mix': (DEPTH, 1024), 'ssm_norm': (DEPTH, 1024), 'attn_out_norm': (DEPTH, 1024),
              'norm_mem_q': (DEPTH, 1024), 'norm_mem_kv': (DEPTH, 1024), 'norm_ffn': (DEPTH, 1024),
              'q_norm': (DEPTH, 384), 'kv_norm': (DEPTH, 256), 'ssm_conv_b': (DEPTH, 2048), 'ffn_conv_b': (DEPTH, 5632),
              'dt_bias': (DEPTH, SSM_HEADS), 'a_log': (DEPTH, SSM_HEADS), 'd_skip': (DEPTH, SSM_HEADS),
              'ssm_conv_w': (DEPTH, SSM_CONV * CONV_CH // N_DEV), 'ffn_conv_w': (DEPTH, FFN_CONV * 2 * D_FF // N_DEV),
              'final_norm': (1, 1024)}
SMALL_NAMES = list(SMALL_VIEW)
SMALL_SHARDED = {'ssm_conv_w': (SSM_CONV, CONV_CH // N_DEV, CONV_CH), 'ffn_conv_w': (FFN_CONV, 2 * D_FF // N_DEV, 2 * D_FF)}


def adamw_small(gathered, ws, ms, vs, tie):
    nsm = len(SMALL_NAMES)

    def body(*refs):
        g8_ref = refs[0]
        w_refs, m_refs, v_refs = refs[1:1 + nsm], refs[1 + nsm:1 + 2 * nsm], refs[1 + 2 * nsm:1 + 3 * nsm]
        outs = refs[2 + 3 * nsm:2 + 7 * nsm]
        sum_ref = refs[2 + 7 * nsm]
        shard_bufs = refs[3 + 7 * nsm:]
        tot = g8_ref[0]
        for d in range(1, N_DEV):
            tot = tot + g8_ref[d]
        sum_ref[...] = tot
        mx, my, mc, _ = _place()
        dev = 4 * mx + 2 * my + mc

        def update(i, g):
            d, nm, nv = _adam(g, w_refs[i][...], m_refs[i][...], v_refs[i][...])
            outs[i][...] = g
            outs[nsm + i][...] = d
            outs[2 * nsm + i][...] = nm
            outs[3 * nsm + i][...] = nv

        for i, name in enumerate(SMALL_NAMES):
            rows, cols = SMALL_VIEW[name]
            off = SMALL_OFF[name]
            if name in SMALL_SHARDED:
                taps, per, full = SMALL_SHARDED[name]
                buf = shard_bufs[list(SMALL_SHARDED).index(name)]
                for d in range(N_DEV):
                    @pl.when(dev == d)
                    def _(d=d, taps=taps, per=per, full=full, off=off, buf=buf):
                        for k in range(taps):
                            buf[:, per * k:per * (k + 1)] = sum_ref[:, off + full * k + per * d:off + full * k + per * (d + 1)]
                update(i, buf[...])
            else:
                update(i, sum_ref[0:rows, off:off + cols])

    views = [jax.ShapeDtypeStruct(SMALL_VIEW[n], F32) for n in SMALL_NAMES]
    vmem = pl.BlockSpec(memory_space=pltpu.VMEM)
    res = pl.pallas_call(
        body, name="adamw_small", out_shape=views * 4, in_specs=[vmem] * (1 + 3 * nsm) + [ANY],
        out_specs=[vmem] * (4 * nsm),
        scratch_shapes=[pltpu.VMEM((DEPTH, SMALL_W), F32)] + [pltpu.VMEM(SMALL_VIEW[n], F32) for n in SMALL_SHARDED],
        compiler_params=_params(),
    )(gathered, *[ws[n] for n in SMALL_NAMES], *[ms[n] for n in SMALL_NAMES], *[vs[n] for n in SMALL_NAMES], tie)
    return [dict(zip(SMALL_NAMES, res[k * nsm:(k + 1) * nsm])) for k in range(4)]


def _layer_weights(gathered):
    w = {}
    for n, g in gathered.items():
        if n == 'w_in':
            w['w_proj'] = assemble_proj(g)
        elif n == 'w_uq':
            w['w_uq'] = assemble_uq(g)
        elif n == 'w_ukv':
            w['w_kn'], w['w_v'] = assemble_ukv(g)
        elif n == 'w_up':
            w['w_g'], w['w_vv'] = assemble_up(g)
        else:
            w[n] = g.reshape(N_DEV * BIG[n][0], BIG[n][1])
    return w


def _rope_post(acc, row_tiles, full_tiles, o_refs):
    for h in range(acc.shape[1] // HEAD_PAD):
        sl = slice(HEAD_PAD * h, HEAD_PAD * (h + 1))
        o_refs[0][:, sl] = _rope_tile(acc[:, sl], row_tiles[0], row_tiles[1]).astype(o_refs[0].dtype)


def _norm_post(acc, row_tiles, full_tiles, o_refs):
    o_refs[0][...] = acc
    o_refs[1][...] = _rms(acc, full_tiles[0]).astype(o_refs[1].dtype)


def layer_fwd(x0, h1, mem, cosm, sinm, w, sm, l, tie=None):
    gain = lambda n: (sm[n], l)
    sv = dict(x0=x0)
    sv['h1'] = h1 if h1 is not None else rmsnorm_fwd(x0, gain('norm_mix'), "norm_mix_fwd", tie=tie)
    proj = sv['proj'] = matmul([(sv['h1'], w['w_proj'])], 'nn', F32, "proj_fwd", tie=tie if h1 is not None else None)
    sv['xbc'] = ssm_conv_fwd(proj, sm['ssm_conv_w'], sm['ssm_conv_b'], l)
    sv['y'], sv['prevs'] = ssd_fwd(sv['xbc'], proj, sm['ptile'], l)
    mix = gate_norm_fwd(sv['y'], proj, gain('ssm_norm'))
    sv['cqn'] = rmsnorm_fwd(proj, gain('q_norm'), "q_norm_fwd", Q_LORA, OFF_CQ // Q_LORA)
    sv['ckvn'] = rmsnorm_fwd(proj, gain('kv_norm'), "kv_norm_fwd", KV_LORA, OFF_CKV // KV_LORA)
    sv['q'] = matmul([(sv['cqn'], w['w_uq'])], 'nn', BF16, "uq_fwd", post=_rope_post, rows=[cosm, sinm])
    kn = matmul([(sv['ckvn'], w['w_kn'])], 'nn', BF16, "kn_fwd")
    sv['k'] = build_k(kn, proj, cosm, sinm)
    sv['v'] = matmul([(sv['ckvn'], w['w_v'])], 'nn', BF16, "v_fwd")
    sv['o'], sv['lse'] = mla_fwd(sv['q'], sv['k'], sv['v'])
    mix = sv['mix'] = rmsnorm_fwd(sv['o'], gain('attn_out_norm'), "attn_out_norm_fwd", out=(D_SSM, BF16, D_MIX, 1),
                                  into=(mix, 0))
    x1, sv['hq'] = matmul([(mix, w['w_out'])], 'nn', F32, "out_fwd", add=x0, post=_norm_post,
                          fulls=[gain('norm_mem_q')], outs=[F32, BF16], full_n=True)
    sv['x1'] = x1
    sv['mn'] = rmsnorm_fwd(mem, gain('norm_mem_kv'), "norm_mem_kv_fwd")
    if 'later' in w:
        w.update(w.pop('later')(sv['hq']))
    sv['mq'] = matmul([(sv['hq'], w['w_mq'])], 'nn', BF16, "mq_fwd")
    sv['mk'] = matmul([(sv['mn'], w['w_mk'])], 'nn', BF16, "mk_fwd")
    sv['mv'] = matmul([(sv['mn'], w['w_mv'])], 'nn', BF16, "mv_fwd")
    sv['om'] = mem_fwd(sv['mq'], sv['mk'], sv['mv'])
    x2, sv['h3'] = matmul([(sv['om'], w['w_mo'])], 'nn', F32, "mo_fwd", add=x1, post=_norm_post,
                          fulls=[gain('norm_ffn')], outs=[F32, BF16], full_n=True)
    sv['x2'] = x2
    tie_ffn = w.pop('prefetch')(sv['h3']) if 'prefetch' in w else None
    sv['ug'] = matmul([(sv['h3'], w['w_g'])], 'nn', F32, "up_g_fwd", tie=tie_ffn)
    sv['uv'] = matmul([(sv['h3'], w['w_vv'])], 'nn', F32, "up_v_fwd")
    sv['a'] = ffn_act_fwd(sv['ug'], sv['uv'], sm['ffn_conv_w'], sm['ffn_conv_b'], l)
    if l + 1 < DEPTH:
        x3, h1_next = matmul([(sv['a'], w['w_down'])], 'nn', F32, "down_fwd", add=x2, post=_norm_post,
                             fulls=[(sm['norm_mix'], l + 1)], outs=[F32, BF16], full_n=True)
    else:
        x3, h1_next = matmul([(sv['a'], w['w_down'])], 'nn', F32, "down_fwd_last", add=x2), None
    return x3, h1_next, sv


EARLY_GRADS = ('w_down', 'w_up', 'w_mo', 'w_mq', 'w_mk', 'w_mv', 'w_out')
LATE_GRADS = ('w_uq', 'w_ukv', 'w_in')


def layer_bwd(dx3, mem, cosm, sinm_neg, w, sm, l, sv, on_grads, tie=None):
    gain = lambda n: (sm[n], l)
    big, small = {}, {}
    proj = sv['proj']
    da = matmul([(dx3, w['w_down'])], 'nt', BF16, "down_bwd_a", tie=tie)
    big['w_down'] = matmul([(sv['a'], dx3)], 'tn', BF16, "down_bwd_w")
    dug, duv, dcwg, dcwv, dcbg, dcbv = ffn_act_bwd(sv['ug'], sv['uv'], sm['ffn_conv_w'], sm['ffn_conv_b'], l, da)
    small['ffn_conv_w'] = jnp.concatenate([dcwg, dcwv], axis=1)
    small['ffn_conv_b'] = jnp.concatenate([dcbg, dcbv], axis=1)
    dh3 = matmul([(dug, w['w_g']), (duv, w['w_vv'])], 'nt', BF16, "up_bwd_h")
    big['w_up'] = extract_up(matmul([(sv['h3'], dug)], 'tn', BF16, "up_g_bwd_w"),
                             matmul([(sv['h3'], duv)], 'tn', BF16, "up_v_bwd_w"))
    dx2, small['norm_ffn'] = rmsnorm_bwd(sv['x2'], gain('norm_ffn'), dh3, "norm_ffn_bwd", resid=dx3)
    dom = matmul([(dx2, w['w_mo'])], 'nt', BF16, "mo_bwd_a")
    big['w_mo'] = matmul([(sv['om'], dx2)], 'tn', BF16, "mo_bwd_w")
    dmq, dmk, dmv = mem_bwd(sv['mq'], sv['mk'], sv['mv'], dom)
    dhq = matmul([(dmq, w['w_mq'])], 'nt', BF16, "mq_bwd_a")
    big['w_mq'] = matmul([(sv['hq'], dmq)], 'tn', BF16, "mq_bwd_w")
    dmn = matmul([(dmk, w['w_mk']), (dmv, w['w_mv'])], 'nt', BF16, "mkv_bwd_a")
    big['w_mk'] = matmul([(sv['mn'], dmk)], 'tn', BF16, "mk_bwd_w")
    big['w_mv'] = matmul([(sv['mn'], dmv)], 'tn', BF16, "mv_bwd_w")
    _, small['norm_mem_kv'] = rmsnorm_bwd(mem, gain('norm_mem_kv'), dmn, "norm_mem_kv_bwd", dx_dtype=BF16)
    dx1, small['norm_mem_q'] = rmsnorm_bwd(sv['x1'], gain('norm_mem_q'), dhq, "norm_mem_q_bwd", resid=dx2)
    dmix = matmul([(dx1, w['w_out'])], 'nt', BF16, "out_bwd_a")
    big['w_out'] = matmul([(sv['mix'], dx1)], 'tn', BF16, "out_bwd_w")
    early = {n: big.pop(n).reshape((N_DEV,) + BIG[n]) if n != 'w_up' else big.pop(n) for n in EARLY_GRADS}
    tie = on_grads(l, 'a', early)
    dy, dz, small['ssm_norm'] = gate_norm_bwd(sv['y'], proj, gain('ssm_norm'), dmix, tie=tie)
    dxbc_act, dsmall_ssd, small['ptile'] = ssd_bwd(sv['xbc'], proj, sm['ptile'], l, sv['prevs'], dy)
    dxbc, small['ssm_conv_w'], small['ssm_conv_b'] = ssm_conv_bwd(proj, sm['ssm_conv_w'], sm['ssm_conv_b'], l, dxbc_act)
    do, small['attn_out_norm'] = rmsnorm_bwd(sv['o'], gain('attn_out_norm'), dmix, "attn_out_norm_bwd", dh_colblock=1)
    dq, dk, dv = mla_bwd(sv['q'], sv['k'], sv['v'], sv['o'], sv['lse'], do, cosm, sinm_neg)
    dsmall = dsmall_bwd(dk, dsmall_ssd, cosm, sinm_neg)
    dcqn = matmul([(dq, w['w_uq'])], 'nt', BF16, "uq_bwd_a")
    big['w_uq'] = extract_uq(matmul([(sv['cqn'], dq)], 'tn', BF16, "uq_bwd_w"))
    dckvn = matmul([(dk, w['w_kn']), (dv, w['w_v'])], 'nt', BF16, "ukv_bwd_a")
    big['w_ukv'] = extract_ukv(matmul([(sv['ckvn'], dk)], 'tn', BF16, "kn_bwd_w"),
                               matmul([(sv['ckvn'], dv)], 'tn', BF16, "v_bwd_w"))
    dcq, small['q_norm'] = rmsnorm_bwd(proj, gain('q_norm'), dcqn, "q_norm_bwd", width=Q_LORA,
                                       colblock=OFF_CQ // Q_LORA, dx_dtype=BF16)
    dckv, small['kv_norm'] = rmsnorm_bwd(proj, gain('kv_norm'), dckvn, "kv_norm_bwd", width=KV_LORA,
                                         colblock=OFF_CKV // KV_LORA, dx_dtype=BF16)
    wp = w['w_proj']
    xbc_half = lambda c: Opnd(dxbc, c0=c, shape=(dxbc.shape[0], 1024))
    wwin = lambda off, width: Opnd(wp, c0=off // width, shape=(D_MODEL, width))
    dh1 = matmul([(dz, wwin(OFF_Z, 1024)), (xbc_half(0), wwin(OFF_XBC, 1024)), (xbc_half(1), wwin(OFF_XBC + 1024, 1024)),
                  (dcq, wwin(OFF_CQ, Q_LORA)), (dsmall, wwin(OFF_SMALL, LANES)), (dckv, wwin(OFF_CKV, KV_LORA))],
                 'nt', BF16, "proj_bwd_a")
    h1 = sv['h1']
    big['w_in'] = extract_proj(
        matmul([(h1, dz)], 'tn', BF16, "proj_z_bwd_w"), matmul([(h1, dxbc)], 'tn', BF16, "proj_xbc_bwd_w"),
        matmul([(h1, dcq)], 'tn', BF16, "proj_cq_bwd_w"), matmul([(h1, dsmall)], 'tn', BF16, "proj_small_bwd_w"),
        matmul([(h1, dckv)], 'tn', BF16, "proj_ckv_bwd_w"))
    dx0, small['norm_mix'] = rmsnorm_bwd(sv['x0'], gain('norm_mix'), dh1, "norm_mix_bwd", resid=dx1)
    return dx0, on_grads(l, 'b', big), small


def _small_row(small, final=None):
    pt = small['ptile']
    parts = []
    for n, wd in SMALL_SEGS:
        if n in ('dt_bias', 'a_log', 'd_skip'):
            parts.append(pt[('dt_bias', 'a_log', 'd_skip').index(n)][None, :])
        elif n in SMALL_SHARDED:
            parts.append(small[n].reshape(1, wd))
        elif n == 'final_norm':
            parts.append(final if final is not None else jnp.zeros((1, wd), F32))
        else:
            parts.append(small[n])
    return jnp.concatenate(parts, axis=1)


def _rope_tables(positions):
    inv_freq = 1.0 / (ROPE_THETA ** (jnp.arange(0, QK_ROPE, 2, dtype=F32) / QK_ROPE))
    ang = positions.astype(F32)[:, None] * inv_freq
    cos, sin = jnp.cos(ang), jnp.sin(ang)
    s = positions.shape[0]
    pad = jnp.zeros((s, LANES - ROPE_LANE0 - QK_ROPE), F32)
    cosm = jnp.concatenate([jnp.ones((s, ROPE_LANE0), F32), cos, cos, pad], axis=1)
    sinm = jnp.concatenate([jnp.zeros((s, ROPE_LANE0), F32), -sin, sin, pad], axis=1)
    return cosm, sinm


def _small_views(rep, conv_full):
    sm = {n: rep[n].reshape(DEPTH, 1, -1) for n in ('norm_mix', 'ssm_norm', 'attn_out_norm', 'norm_mem_q',
                                                    'norm_mem_kv', 'norm_ffn', 'q_norm', 'kv_norm', 'ssm_conv_b',
                                                    'ffn_conv_b')}
    sm.update(conv_full)
    rows = jnp.stack([rep['dt_bias'], rep['a_log'], rep['d_skip']], axis=1)
    sm['ptile'] = jnp.pad(rows, ((0, 0), (0, 8 - 3), (0, LANES - SSM_HEADS)))
    return sm


def local_step(x, mem, positions, target, sm, final_norm, weights_of, on_grads):
    cosm, sinm = _rope_tables(positions)
    sinm_neg = -sinm
    saved, ws = [], []
    h, h1 = x, None
    for l in range(DEPTH):
        w, tie = weights_of(l, h)
        ws.append(w)
        h, h1, sv = layer_fwd(h, h1, mem, cosm, sinm, w, sm, l, tie=tie)
        saved.append(sv)
    dx, dfinal, lossv = loss_head(h, (final_norm.reshape(1, 1, -1), 0), target)
    rows = [None] * DEPTH
    tie = None
    for l in reversed(range(DEPTH)):
        dx, tie, small = layer_bwd(dx, mem, cosm, sinm_neg, ws[l], sm, l, saved[l], on_grads, tie=tie)
        rows[l] = _small_row(small, dfinal if l == 0 else None)
    return lossv[0, 0], dx, jnp.concatenate(rows, axis=0)


def kernel(x, mem, positions, norm_mix, w_in, ssm_conv_w, ssm_conv_b, dt_bias, a_log, d_skip, ssm_norm, q_norm, w_uq, kv_norm, w_ukv, attn_out_norm, w_out, norm_mem_q, norm_mem_kv, w_mq, w_mk, w_mv, w_mo, norm_ffn, w_up, ffn_conv_w, ffn_conv_b, w_down, final_norm, loss_target, m_norm_mix, m_w_in, m_ssm_conv_w, m_ssm_conv_b, m_dt_bias, m_a_log, m_d_skip, m_ssm_norm, m_q_norm, m_w_uq, m_kv_norm, m_w_ukv, m_attn_out_norm, m_w_out, m_norm_mem_q, m_norm_mem_kv, m_w_mq, m_w_mk, m_w_mv, m_w_mo, m_norm_ffn, m_w_up, m_ffn_conv_w, m_ffn_conv_b, m_w_down, m_final_norm, v_norm_mix, v_w_in, v_ssm_conv_w, v_ssm_conv_b, v_dt_bias, v_a_log, v_d_skip, v_ssm_norm, v_q_norm, v_w_uq, v_kv_norm, v_w_ukv, v_attn_out_norm, v_w_out, v_norm_mem_q, v_norm_mem_kv, v_w_mq, v_w_mk, v_w_mv, v_w_mo, v_norm_ffn, v_w_up, v_ffn_conv_w, v_ffn_conv_b, v_w_down, v_final_norm):
    args = locals()
    wts = {n: args[n] for n in WEIGHT_NAMES}
    ms = {n: args['m_' + n] for n in WEIGHT_NAMES}
    vs = {n: args['v_' + n] for n in WEIGHT_NAMES}

    st = dict(srcs={n: wts[n].astype(BF16) for n in BIG_NAMES}, exchanges=[],
              lands={n: lax.empty((DEPTH, N_DEV) + BIG[n], BF16) for n in BIG_NAMES})
    first = LATE_GRADS + ('w_out',)
    rest = tuple(n for n in BIG_NAMES if n not in first)
    got = all_gather_blocks([st['srcs'][n] for n in first] + [wts[n] for n in SMALL_SHARDED],
                            first_only=tuple(range(len(first))))
    conv_full = {}
    for n, g in zip(SMALL_SHARDED, got[len(first):]):
        taps, per, full = SMALL_SHARDED[n]
        conv_full[n] = jnp.moveaxis(g, 1, 2).reshape(DEPTH, taps, full)
    sm = _small_views(wts, conv_full)

    def start(names, l, tag, after=None):
        send_sems, recv_sems, thru, lands, tie = gather_start([st['srcs'][n] for n in names], l, tag, after)
        st['srcs'].update(zip(names, thru))
        return (names, l, tag, send_sems, recv_sems, lands), tie

    def pass_on(handle, after):
        names, l, tag, send_sems, recv_sems, lands = handle
        thru, lands = gather_wait(l, tag, send_sems, recv_sems, [st['srcs'][n] for n in names], lands, after)
        st['srcs'].update(zip(names, thru))
        send_sems, recv_sems, lands, tie = gather_pass_start(lands, l, tag)
        return (names, l, tag, send_sems, recv_sems, lands), tie

    def finish(handle, after):
        names, l, tag, send_sems, recv_sems, lands = handle
        return _layer_weights(dict(zip(names, gather_pass_wait(l, tag, send_sems, recv_sems, lands, after))))

    later, _ = start(rest, 0, "r", after=got[0])

    def weights_of(l, h):
        if l == 0:
            w = _layer_weights(dict(zip(first, got[:len(first)])))
            w['later'] = lambda after: finish(pass_on(later, after)[0], after)
        else:
            w = finish(st['next'], h)
        tie = None
        if l + 1 < DEPTH:
            st['next'], tie = start(BIG_NAMES, l + 1, "")

            def prefetch(after):
                st['next'], tie2 = pass_on(st['next'], after)
                return tie2

            w['prefetch'] = prefetch
        return w, tie

    def on_grads(l, tag, big, after=None):
        if (l, tag) == (0, 'b') and after is None:
            st['held'] = big
            return None
        names = list(big)
        send_sems, recv_sems, thru, lands, tie = grad_exchange_start(
            [big[n] for n in names], [st['lands'][n] for n in names], l, tag, after)
        st['lands'].update(zip(names, lands))
        st['exchanges'].append((l, tag, names, send_sems, recv_sems, thru))
        return tie

    loss_local, dx, small_rows = local_step(x[0], mem[0], positions[0], loss_target[0], sm, final_norm, weights_of,
                                            on_grads)
    outs = [{}, {}, {}, {}]

    sg_send, sg_recv, sg_src, sg_land, tok = small_gather_start(small_rows)
    tie = on_grads(0, 'b', st['held'], after=tok)

    def wait(exchange, after):
        l, tag, names, send_sems, recv_sems, thru = exchange
        _, lands = grad_exchange_wait(l, tag, send_sems, recv_sems, thru, [st['lands'][n] for n in names], after)
        st['lands'].update(zip(names, lands))

    def update(names, tie):
        for n in names:
            res_n = adamw_big(st['lands'][n], wts[n], ms[n], vs[n], "adamw_" + n, tie)
            tie = res_n[0]
            for k in range(4):
                outs[k][n] = res_n[k]
        return tie

    for exchange in st['exchanges'][:-1]:
        wait(exchange, tie)
    tie = update(EARLY_GRADS, tie)

    small_all = small_gather_wait(sg_send, sg_recv, sg_src, sg_land, tie)
    view = lambda d: {n: d[n].reshape(SMALL_VIEW[n]) for n in SMALL_NAMES}
    res = adamw_small(small_all, view(wts), view(ms), view(vs), tie)
    for k in range(4):
        for n in SMALL_NAMES:
            outs[k][n] = res[k][n].reshape(wts[n].shape)

    wait(st['exchanges'][-1], res[0]['final_norm'])
    update(LATE_GRADS, res[0]['final_norm'])

    loss = lax.psum(loss_local, ("x", "y", "c"))
    return (loss, dx[None], *[outs[0][n] for n in WEIGHT_NAMES], *[outs[1][n] for n in WEIGHT_NAMES],
            *[outs[2][n] for n in WEIGHT_NAMES], *[outs[3][n] for n in WEIGHT_NAMES])
```

```python
import functools
import math
from typing import Any, NamedTuple, Optional

import jax
import jax.numpy as jnp
from jax import lax
from jax.experimental import pallas as pl
from jax.experimental.pallas import tpu as pltpu

F32 = jnp.float32
BF16 = jnp.bfloat16

D_MODEL = 1024
DEPTH = 4
MEM_LEN = 256
EPS = 1e-6
SSM_HEADS = 16
SSM_HEAD_DIM = 64
D_SSM = 1024
SSM_GROUPS = 4
SSM_STATE = 128
SSM_CONV = 4
SSM_CHUNK = 128
CONV_CH = 2048
MLA_HEADS = 16
QK_NOPE = 64
QK_ROPE = 32
V_DIM = 64
Q_LORA = 384
KV_LORA = 256
ROPE_THETA = 10000.0
MEM_HEADS = 4
MEM_HEAD_DIM = 256
D_FF = 2816
FFN_CONV = 3
D_IN = 3760
D_MIX = 2048
ADAM_LR = 0.001
ADAM_B1 = 0.9
ADAM_B2 = 0.999
ADAM_EPS = 1e-08
ADAM_WD = 0.01
ADAM_STEP = 10

N_DEV = 8
N_CHIP = 4
LANES = 128
HEAD_PAD = 128
PROJ_W = 3840
OFF_Z, OFF_XBC, OFF_CQ, OFF_SMALL, OFF_CKV = 0, 1024, 3072, 3456, 3584
ROPE_LANE0 = 64
VMEM_LIMIT = 56 * 1024 * 1024
MM_BLOCK_BYTES = 4 * 1024 * 1024
WEIGHT_NAMES = ['norm_mix', 'w_in', 'ssm_conv_w', 'ssm_conv_b', 'dt_bias', 'a_log', 'd_skip', 'ssm_norm', 'q_norm',
                'w_uq', 'kv_norm', 'w_ukv', 'attn_out_norm', 'w_out', 'norm_mem_q', 'norm_mem_kv', 'w_mq', 'w_mk',
                'w_mv', 'w_mo', 'norm_ffn', 'w_up', 'ffn_conv_w', 'ffn_conv_b', 'w_down', 'final_norm']
BIG = {'w_in': (1024, 470), 'w_uq': (384, 192), 'w_ukv': (256, 256), 'w_up': (1024, 704), 'w_out': (256, 1024),
       'w_mq': (128, 1024), 'w_mk': (128, 1024), 'w_mv': (128, 1024), 'w_mo': (128, 1024), 'w_down': (352, 1024)}
BIG_NAMES = list(BIG)
PROJ_SEGS = [(0, 1024, OFF_Z), (1024, 3072, OFF_XBC), (3072, 3088, OFF_SMALL), (3088, 3472, OFF_CQ),
             (3472, 3728, OFF_CKV), (3728, 3760, OFF_SMALL + ROPE_LANE0)]
SMALL_SEGS = [('norm_mix', 1024), ('ssm_norm', 1024), ('attn_out_norm', 1024), ('norm_mem_q', 1024),
              ('norm_mem_kv', 1024), ('norm_ffn', 1024), ('q_norm', 384), ('kv_norm', 256), ('ssm_conv_b', 2048),
              ('ffn_conv_b', 5632), ('dt_bias', 128), ('a_log', 128), ('d_skip', 128),
              ('ssm_conv_w', SSM_CONV * CONV_CH), ('ffn_conv_w', FFN_CONV * 2 * D_FF), ('final_norm', 1024)]
SMALL_OFF = {}
_o = 0
for _n, _w in SMALL_SEGS:
    SMALL_OFF[_n] = _o
    _o += _w
SMALL_W = _o


def _params(**kw):
    return pltpu.CompilerParams(vmem_limit_bytes=VMEM_LIMIT, **kw)


def _pick(n, cap):
    if n <= cap:
        return n
    best = None
    for t in range(LANES, cap + 1, LANES):
        if n % t == 0:
            best = t
    assert best is not None, (n, cap)
    return best


def _row_tile(a, cap=256):
    if a <= cap:
        return a
    best = None
    for t in range(16, cap + 1, 16):
        if a % t == 0:
            best = t
    assert best is not None, (a, cap)
    return best


class Opnd(NamedTuple):
    arr: Any
    lead: Optional[int] = None
    r0: int = 0
    c0: int = 0
    shape: Optional[tuple] = None


def _opnd(x):
    return x if isinstance(x, Opnd) else Opnd(x)


def _lshape(o):
    return tuple(o.shape) if o.shape is not None else tuple(o.arr.shape[-2:])


def _spec(o, br, bc, bi, bj):
    rr, cc = _lshape(o)
    assert rr % br == 0 and cc % bc == 0, (rr, cc, br, bc)
    ro, co = o.r0 * (rr // br), o.c0 * (cc // bc)
    if o.lead is None:
        return pl.BlockSpec((br, bc), lambda i, j: (ro + bi(i, j), co + bj(i, j)))
    return pl.BlockSpec((None, br, bc), lambda i, j: (o.lead, ro + bi(i, j), co + bj(i, j)))


_DIMS = {'nn': (((1,), (0,)), ((), ())), 'nt': (((1,), (1,)), ((), ())), 'tn': (((0,), (0,)), ((), ()))}
_ROW = lambda i, j: i
_COL = lambda i, j: j
_ZERO = lambda i, j: 0


def matmul(pairs, mode, out_dtype, name, add=None, tie=None, post=None, rows=(), fulls=(), outs=None, full_n=False,
           accs=(), tm_cap=None):
    pairs = [(_opnd(a), _opnd(b)) for a, b in pairs]
    a0, b0 = pairs[0]
    if mode == 'nn':
        m, n = _lshape(a0)[0], _lshape(b0)[1]
    elif mode == 'nt':
        m, n = _lshape(a0)[0], _lshape(b0)[0]
    else:
        m, n = _lshape(a0)[1], _lshape(b0)[1]
    isz = lambda o: jnp.dtype(o.arr.dtype).itemsize
    osz = jnp.dtype(out_dtype).itemsize
    cap = lambda budget, per: max(LANES, budget // per // LANES * LANES)
    if mode == 'tn':
        ktok = _lshape(a0)[0]
        tm = _pick(m, cap(3 * MM_BLOCK_BYTES // 2, ktok * isz(a0)))
        tn = _pick(n, cap(3 * MM_BLOCK_BYTES // 2, ktok * isz(b0)))
    else:
        tm = _pick(m, min(2048, cap(2 * MM_BLOCK_BYTES, sum(_lshape(a)[1] * isz(a) for a, _ in pairs))))
        tn = _pick(n, min(cap(3 * MM_BLOCK_BYTES // 2, sum(_lshape(a)[1] * isz(b) for a, b in pairs)),
                          cap(MM_BLOCK_BYTES, tm * osz), n // 2 if n >= 1024 else n))
        if full_n:
            tm, tn = _pick(m, min(tm, tm_cap or tm, cap(MM_BLOCK_BYTES // 2, n * osz))), n
    npairs = len(pairs)
    outs = list(outs) if outs is not None else [out_dtype]
    nadd = 1 if add is not None else 0
    nrows, nfulls = len(rows), len(fulls)

    def body(*refs):
        o_refs = refs[len(refs) - len(outs) - len(accs):]
        if accs:
            @pl.when(jnp.logical_and(pl.program_id(0) == 0, pl.program_id(1) == 0))
            def _():
                for r in o_refs[len(outs):]:
                    r[...] = jnp.zeros_like(r)

        acc = None
        for p in range(npairs):
            a = refs[2 * p][...].astype(BF16)
            b = refs[2 * p + 1][...].astype(BF16)
            d = lax.dot_general(a, b, _DIMS[mode], preferred_element_type=F32)
            acc = d if acc is None else acc + d
        if add is not None:
            acc = acc + refs[2 * npairs][...].astype(F32)
        if post is None:
            o_refs[0][...] = acc.astype(out_dtype)
        else:
            x0 = 2 * npairs + nadd
            post(acc, [r[...] for r in refs[x0:x0 + nrows]], [r[...] for r in refs[x0 + nrows:x0 + nrows + nfulls]], o_refs)

    tie_specs = [pl.BlockSpec((tm, r.shape[1]), lambda i, j: (i, 0)) for r in rows]
    tie_specs += [pl.BlockSpec((None,) + f.shape[1:], lambda i, j, ld=ld, nd=f.ndim - 1: (ld,) + (0,) * nd) for f, ld in fulls]
    tie_args = list(rows) + [f for f, _ in fulls]
    if tie is not None:
        tie_specs.append(pl.BlockSpec(memory_space=pl.ANY))
        tie_args.append(tie)

    in_specs, args = [], []
    for a, b in pairs:
        if mode == 'nn':
            k = _lshape(a)[1]
            in_specs += [_spec(a, tm, k, _ROW, _ZERO), _spec(b, k, tn, _ZERO, _COL)]
        elif mode == 'nt':
            k = _lshape(a)[1]
            in_specs += [_spec(a, tm, k, _ROW, _ZERO), _spec(b, tn, k, _COL, _ZERO)]
        else:
            k = _lshape(a)[0]
            in_specs += [_spec(a, k, tm, _ZERO, _ROW), _spec(b, k, tn, _ZERO, _COL)]
        args += [a.arr, b.arr]
    if add is not None:
        in_specs.append(pl.BlockSpec((tm, tn), lambda i, j: (i, j)))
        args.append(add)
    res = pl.pallas_call(
        body, name=name, grid=(m // tm, n // tn), in_specs=in_specs + tie_specs,
        out_specs=[pl.BlockSpec((tm, tn), lambda i, j: (i, j))] * len(outs) +
                  [pl.BlockSpec(shp, lambda i, j, nd=len(shp): (0,) * nd) for shp, _ in accs],
        out_shape=[jax.ShapeDtypeStruct((m, n), dt) for dt in outs] + [jax.ShapeDtypeStruct(shp, dt) for shp, dt in accs],
        compiler_params=_params(dimension_semantics=("arbitrary", "arbitrary")),
    )(*args, *tie_args)
    return res[0] if len(outs) + len(accs) == 1 else res


def rowwise(fn, rows, fulls, outs, accs, name, tm=256, into=None, tie=None):
    s = rows[0][0].shape[0]
    nrow, nfull, nout, nacc = len(rows), len(fulls), len(outs), len(accs)
    nin = nrow + nfull

    def body(*refs):
        ins = [r[...] for r in refs[:nin]]
        res = fn(*ins)
        if not isinstance(res, (tuple, list)):
            res = (res,)
        orefs = refs[nin + (1 if into is not None else 0) + (1 if tie is not None else 0):]
        for k in range(nout):
            orefs[k][...] = res[k].astype(orefs[k].dtype)
        if nacc:
            @pl.when(pl.program_id(0) == 0)
            def _():
                for k in range(nacc):
                    orefs[nout + k][...] = jnp.zeros_like(orefs[nout + k])

            for k in range(nacc):
                orefs[nout + k][...] += res[nout + k].astype(orefs[nout + k].dtype)

    in_specs = [pl.BlockSpec((tm, w), lambda i, cb=cb: (i, cb)) for _, w, cb in rows]
    in_specs += [pl.BlockSpec((None,) + f.shape[1:], lambda i, ld=ld, nd=f.ndim - 1: (ld,) + (0,) * nd) for f, ld in fulls]
    args = [r[0] for r in rows] + [f for f, _ in fulls]
    aliases = {}
    if into is not None:
        in_specs.append(pl.BlockSpec(memory_space=pl.ANY))
        args.append(into[0])
        aliases = {nin: into[1]}
    if tie is not None:
        in_specs.append(pl.BlockSpec(memory_space=pl.ANY))
        args.append(tie)
    out_specs, out_shape = [], []
    for o in outs:
        w, dt = o[0], o[1]
        total, cb = (o[2], o[3]) if len(o) == 4 else (w, 0)
        out_specs.append(pl.BlockSpec((tm, w), lambda i, cb=cb: (i, cb)))
        out_shape.append(jax.ShapeDtypeStruct((s, total), dt))
    for shp, dt in accs:
        out_specs.append(pl.BlockSpec(shp, lambda i, nd=len(shp): (0,) * nd))
        out_shape.append(jax.ShapeDtypeStruct(shp, dt))
    return pl.pallas_call(
        body, name=name, grid=(s // tm,), in_specs=in_specs, out_specs=out_specs, out_shape=out_shape,
        input_output_aliases=aliases, compiler_params=_params(dimension_semantics=("arbitrary",)),
    )(*args)


def _rms(x, g):
    xf = x.astype(F32)
    var = jnp.mean(xf * xf, axis=-1, keepdims=True)
    return xf * lax.rsqrt(var + EPS) * g


def rmsnorm_fwd(x, g, name, width=None, colblock=0, out=None, into=None, tie=None):
    w = width or x.shape[1]
    return rowwise(lambda xt, gt: _rms(xt, gt), [(x, w, colblock)], [g], [out or (w, BF16)], [], name, into=into,
                   tie=tie)[0]


def rmsnorm_bwd(x, g, dh, name, resid=None, width=None, colblock=0, dh_colblock=0, dx_dtype=F32):
    w = width or x.shape[1]

    def fn(xt, dht, *rest):
        gt = rest[-1]
        _, vjp = jax.vjp(_rms, xt.astype(F32), gt)
        dx, dg = vjp(dht.astype(F32))
        if resid is not None:
            dx = dx + rest[0]
        return dx, dg

    rows = [(x, w, colblock), (dh, w, dh_colblock)] + ([(resid, w, 0)] if resid is not None else [])
    return rowwise(fn, rows, [g], [(w, dx_dtype)], [((1, w), F32)], name)


CONV_R = 64
HALO = 8


def _ext_rows(ref, i, nchunk, above, below):
    r0 = pl.multiple_of(i * CONV_R, CONV_R)
    s = ref.shape[0]
    parts = []
    if above:
        top = ref[pl.ds(pl.multiple_of(jnp.maximum(r0 - HALO, 0), HALO), HALO), :].astype(F32)
        parts.append(jnp.where(i > 0, top, 0.0))
    parts.append(ref[pl.ds(r0, CONV_R), :].astype(F32))
    if below:
        tile = 2 * HALO if ref.dtype == BF16 else HALO
        bot = ref[pl.ds(pl.multiple_of(jnp.minimum(r0 + CONV_R, s - tile), tile), tile), :].astype(F32)[0:HALO]
        parts.append(jnp.where(i < nchunk - 1, bot, 0.0))
    return jnp.concatenate(parts, axis=0)


def _conv_ext(ext, w_ref, b_ref, kw):
    y = ext[HALO:] * w_ref[kw - 1:kw, :] + b_ref[...]
    for k in range(1, kw):
        y = y + pltpu.roll(ext, k, 0)[HALO:] * w_ref[kw - 1 - k:kw - k, :]
    return y


def _conv_t_ext(d, w_ref, kw):
    n = d.shape[0]
    y = d[:n - HALO] * w_ref[kw - 1:kw, :]
    for k in range(1, kw):
        y = y + pltpu.roll(d, n - k, 0)[:n - HALO] * w_ref[kw - 1 - k:kw - k, :]
    return y


def _conv_wgrad(dp, ext, kw):
    out = [jnp.sum(dp, axis=0, keepdims=True), jnp.sum(dp * ext[HALO:HALO + CONV_R], axis=0, keepdims=True)]
    for k in range(1, kw):
        out.append(jnp.sum(dp * pltpu.roll(ext, k, 0)[HALO:HALO + CONV_R], axis=0, keepdims=True))
    return out


def _store_wgrad(res, dw_ref, db_ref, kw):
    db_ref[...] = res[0]
    for k in range(kw):
        dw_ref[kw - 1 - k:kw - k, :] = res[1 + k]


def _silu(x):
    return x * jax.nn.sigmoid(x)


def _dsilu(x):
    s = jax.nn.sigmoid(x)
    return s * (1.0 + x * (1.0 - s))


SSM_TC = 256


def ssm_conv_fwd(proj, cw, cb, l):
    s = proj.shape[0]
    off = OFF_XBC // SSM_TC

    def body(u_ref, w_ref, b_ref, o_ref):
        nchunk = s // CONV_R

        def step(i, carry):
            ext = _ext_rows(u_ref, i, nchunk, True, False)
            o_ref[pl.ds(pl.multiple_of(i * CONV_R, CONV_R), CONV_R), :] = _silu(_conv_ext(ext, w_ref, b_ref, SSM_CONV))
            return carry

        lax.fori_loop(0, nchunk, step, 0)

    return pl.pallas_call(
        body, name="ssm_conv_fwd", grid=(CONV_CH // SSM_TC,),
        in_specs=[pl.BlockSpec((s, SSM_TC), lambda j: (0, off + j)),
                  pl.BlockSpec((None, SSM_CONV, SSM_TC), lambda j: (l, 0, j)),
                  pl.BlockSpec((None, 1, SSM_TC), lambda j: (l, 0, j))],
        out_specs=pl.BlockSpec((s, SSM_TC), lambda j: (0, j)),
        out_shape=jax.ShapeDtypeStruct((s, CONV_CH), F32),
        compiler_params=_params(dimension_semantics=("arbitrary",)),
    )(proj, cw, cb)


def ssm_conv_bwd(proj, cw, cb, l, dact):
    s = proj.shape[0]
    off = OFF_XBC // SSM_TC

    def body(u_ref, w_ref, b_ref, d_ref, du_ref, dw_ref, db_ref):
        nchunk = s // CONV_R

        def step(i, carry):
            ext = _ext_rows(u_ref, i, nchunk, True, True)
            dpre = _ext_rows(d_ref, i, nchunk, False, True) * _dsilu(_conv_ext(ext, w_ref, b_ref, SSM_CONV))
            du_ref[pl.ds(pl.multiple_of(i * CONV_R, CONV_R), CONV_R), :] = _conv_t_ext(dpre, w_ref, SSM_CONV).astype(du_ref.dtype)
            return tuple(c + g for c, g in zip(carry, _conv_wgrad(dpre[:CONV_R], ext, SSM_CONV)))

        zero = jnp.zeros((1, SSM_TC), F32)
        _store_wgrad(lax.fori_loop(0, nchunk, step, (zero,) * (SSM_CONV + 1)), dw_ref, db_ref, SSM_CONV)

    return pl.pallas_call(
        body, name="ssm_conv_bwd", grid=(CONV_CH // SSM_TC,),
        in_specs=[pl.BlockSpec((s, SSM_TC), lambda j: (0, off + j)),
                  pl.BlockSpec((None, SSM_CONV, SSM_TC), lambda j: (l, 0, j)),
                  pl.BlockSpec((None, 1, SSM_TC), lambda j: (l, 0, j)), pl.BlockSpec((s, SSM_TC), lambda j: (0, j))],
        out_specs=[pl.BlockSpec((s, SSM_TC), lambda j: (0, j)), pl.BlockSpec((SSM_CONV, SSM_TC), lambda j: (0, j)),
                   pl.BlockSpec((1, SSM_TC), lambda j: (0, j))],
        out_shape=[jax.ShapeDtypeStruct((s, CONV_CH), BF16), jax.ShapeDtypeStruct((SSM_CONV, CONV_CH), F32),
                   jax.ShapeDtypeStruct((1, CONV_CH), F32)],
        compiler_params=_params(dimension_semantics=("arbitrary",)),
    )(proj, cw, cb, dact)


FFN_TC = 256
FFN_NT = D_FF // FFN_TC


def _ffn_specs(s, l):
    blk = pl.BlockSpec((s, FFN_TC), lambda j: (0, j))
    wg = pl.BlockSpec((None, FFN_CONV, FFN_TC), lambda j: (l, 0, j))
    wv = pl.BlockSpec((None, FFN_CONV, FFN_TC), lambda j: (l, 0, FFN_NT + j))
    bg = pl.BlockSpec((None, 1, FFN_TC), lambda j: (l, 0, j))
    bv = pl.BlockSpec((None, 1, FFN_TC), lambda j: (l, 0, FFN_NT + j))
    return blk, wg, wv, bg, bv


def ffn_act_fwd(ug, uv, cw, cb, l):
    s = ug.shape[0]

    def body(g_ref, v_ref, wg_ref, wv_ref, bg_ref, bv_ref, o_ref):
        nchunk = s // CONV_R

        def step(i, carry):
            cg = _conv_ext(_ext_rows(g_ref, i, nchunk, True, False), wg_ref, bg_ref, FFN_CONV)
            cv = _conv_ext(_ext_rows(v_ref, i, nchunk, True, False), wv_ref, bv_ref, FFN_CONV)
            o_ref[pl.ds(pl.multiple_of(i * CONV_R, CONV_R), CONV_R), :] = (_silu(cg) * cv).astype(o_ref.dtype)
            return carry

        lax.fori_loop(0, nchunk, step, 0)

    blk, wg, wv, bg, bv = _ffn_specs(s, l)
    return pl.pallas_call(
        body, name="ffn_act_fwd", grid=(FFN_NT,), in_specs=[blk, blk, wg, wv, bg, bv],
        out_specs=blk, out_shape=jax.ShapeDtypeStruct((s, D_FF), BF16),
        compiler_params=_params(dimension_semantics=("arbitrary",)),
    )(ug, uv, cw, cw, cb, cb)


def ffn_act_bwd(ug, uv, cw, cb, l, da):
    s = ug.shape[0]

    def body(g_ref, v_ref, wg_ref, wv_ref, bg_ref, bv_ref, da_ref, dg_ref, dv_ref, dwg_ref, dwv_ref, dbg_ref, dbv_ref):
        nchunk = s // CONV_R

        def step(i, carry):
            rows = pl.ds(pl.multiple_of(i * CONV_R, CONV_R), CONV_R)
            eg = _ext_rows(g_ref, i, nchunk, True, True)
            ev = _ext_rows(v_ref, i, nchunk, True, True)
            cg = _conv_ext(eg, wg_ref, bg_ref, FFN_CONV)
            cv = _conv_ext(ev, wv_ref, bv_ref, FFN_CONV)
            da_t = _ext_rows(da_ref, i, nchunk, False, True)
            sg = jax.nn.sigmoid(cg)
            dcg = da_t * cv * (sg * (1.0 + cg * (1.0 - sg)))
            dcv = da_t * (cg * sg)
            dg_ref[rows, :] = _conv_t_ext(dcg, wg_ref, FFN_CONV).astype(dg_ref.dtype)
            dv_ref[rows, :] = _conv_t_ext(dcv, wv_ref, FFN_CONV).astype(dv_ref.dtype)
            grads = _conv_wgrad(dcg[:CONV_R], eg, FFN_CONV) + _conv_wgrad(dcv[:CONV_R], ev, FFN_CONV)
            return tuple(c + g for c, g in zip(carry, grads))

        zero = jnp.zeros((1, FFN_TC), F32)
        res = lax.fori_loop(0, nchunk, step, (zero,) * (2 * FFN_CONV + 2))
        _store_wgrad(res[:FFN_CONV + 1], dwg_ref, dbg_ref, FFN_CONV)
        _store_wgrad(res[FFN_CONV + 1:], dwv_ref, dbv_ref, FFN_CONV)

    blk, wg, wv, bg, bv = _ffn_specs(s, l)
    wblk = pl.BlockSpec((FFN_CONV, FFN_TC), lambda j: (0, j))
    bblk = pl.BlockSpec((1, FFN_TC), lambda j: (0, j))
    return pl.pallas_call(
        body, name="ffn_act_bwd", grid=(FFN_NT,), in_specs=[blk, blk, wg, wv, bg, bv, blk],
        out_specs=[blk, blk, wblk, wblk, bblk, bblk],
        out_shape=[jax.ShapeDtypeStruct((s, D_FF), BF16), jax.ShapeDtypeStruct((s, D_FF), BF16),
                   jax.ShapeDtypeStruct((FFN_CONV, D_FF), F32), jax.ShapeDtypeStruct((FFN_CONV, D_FF), F32),
                   jax.ShapeDtypeStruct((1, D_FF), F32), jax.ShapeDtypeStruct((1, D_FF), F32)],
        compiler_params=_params(dimension_semantics=("arbitrary",)),
    )(ug, uv, cw, cw, cb, cb, da)


def _dot(a, b, mode):
    return lax.dot_general(a.astype(BF16), b.astype(BF16), _DIMS[mode], preferred_element_type=F32)


@jax.custom_vjp
def mm_nn(a, b):
    return _dot(a, b, 'nn')


@jax.custom_vjp
def mm_nt(a, b):
    return _dot(a, b, 'nt')


@jax.custom_vjp
def mm_tn(a, b):
    return _dot(a, b, 'tn')


mm_nn.defvjp(lambda a, b: (_dot(a, b, 'nn'), (a, b)), lambda r, g: (_dot(g, r[1], 'nt'), _dot(r[0], g, 'tn')))
mm_nt.defvjp(lambda a, b: (_dot(a, b, 'nt'), (a, b)), lambda r, g: (_dot(g, r[1], 'nn'), _dot(g, r[0], 'tn')))
mm_tn.defvjp(lambda a, b: (_dot(a, b, 'tn'), (a, b)), lambda r, g: (_dot(r[1], g, 'nt'), _dot(r[0], g, 'nn')))


def _tri(n, lower):
    r = lax.broadcasted_iota(jnp.int32, (n, n), 0)
    c = lax.broadcasted_iota(jnp.int32, (n, n), 1)
    return jnp.where((r >= c) if lower else (r <= c), 1.0, 0.0).astype(F32)


def _tri_dot(a, lower):
    return jnp.dot(_tri(a.shape[0], lower), a, precision=lax.Precision.HIGHEST, preferred_element_type=F32)


@jax.custom_vjp
def _cumsum_rows(a):
    return _tri_dot(a, True)


_cumsum_rows.defvjp(lambda a: (_tri_dot(a, True), None), lambda _, g: (_tri_dot(g, False),))


def _softplus(x):
    return jnp.maximum(x, 0.0) + jnp.log(1.0 + jnp.exp(-jnp.abs(x)))


def _ssd_chunk(xs, bs, cs, small, dtb, alog, dsk, prev):
    ln = small.shape[0]
    lane = lax.broadcasted_iota(jnp.int32, (ln, LANES), 1)
    lane1 = lax.broadcasted_iota(jnp.int32, (1, LANES), 1)
    sub = lax.broadcasted_iota(jnp.int32, (LANES, ln), 0)
    rowi = lax.broadcasted_iota(jnp.int32, (ln, LANES), 0)
    tril = lax.broadcasted_iota(jnp.int32, (ln, ln), 0) >= lax.broadcasted_iota(jnp.int32, (ln, ln), 1)
    first = lane < SSM_HEAD_DIM
    first1 = lane1 < SSM_HEAD_DIM

    dt = _softplus(small + dtb)
    acs = _cumsum_rows(dt * (-jnp.exp(alog)))
    acs_t = acs.T
    last = jnp.sum(jnp.where(rowi == ln - 1, acs, 0.0), axis=0, keepdims=True)

    def col(a, h):
        return jnp.sum(jnp.where(lane == h, a, 0.0), axis=1, keepdims=True)

    def one(a, h):
        return jnp.sum(jnp.where(lane1 == h, a, 0.0), axis=1, keepdims=True)

    def rowv(at, h):
        return jnp.sum(jnp.where(sub == h, at, 0.0), axis=0, keepdims=True)

    cb = [mm_nt(cs[g], bs[g]) for g in range(SSM_GROUPS)]
    ys, news = [], []
    for j in range(SSM_HEADS // 2):
        g = j // 2
        h0, h1 = 2 * j, 2 * j + 1
        xd = xs[j] * jnp.where(first, col(dt, h0), col(dt, h1))
        yd, st, ea, cd = None, None, [], []
        for h, xdh in ((h0, jnp.where(first, xd, 0.0)), (h1, jnp.where(first, 0.0, xd))):
            ac = col(acs, h)
            la = one(last, h)
            lmat = jnp.exp(jnp.where(tril, ac - rowv(acs_t, h), -jnp.inf))
            yh = mm_nn(cb[g] * lmat, xdh)
            sh = mm_tn(bs[g] * jnp.exp(la - ac), xdh)
            yd = yh if yd is None else yd + yh
            st = sh if st is None else st + sh
            ea.append(jnp.exp(ac))
            cd.append(jnp.exp(la))
        yoff = mm_nn(cs[g], prev[j]) * jnp.where(first, ea[0], ea[1])
        ys.append(yd + yoff + xs[j] * jnp.where(first1, one(dsk, h0), one(dsk, h1)))
        news.append(prev[j] * jnp.where(first1, cd[0], cd[1]) + st)
    return ys, news


N_PAIR = SSM_HEADS // 2


def ssd_fwd(xbc, proj, ptile, l):
    s = xbc.shape[0]
    nch = s // SSM_CHUNK

    def body(xbc_ref, small_ref, p_ref, y_ref, prev_ref, state_ref):
        @pl.when(pl.program_id(0) == 0)
        def _():
            state_ref[...] = jnp.zeros_like(state_ref)

        xs = [xbc_ref[:, LANES * j:LANES * (j + 1)] for j in range(N_PAIR)]
        bs = [xbc_ref[:, D_SSM + LANES * g:D_SSM + LANES * (g + 1)] for g in range(SSM_GROUPS)]
        cs = [xbc_ref[:, D_SSM + 512 + LANES * g:D_SSM + 512 + LANES * (g + 1)] for g in range(SSM_GROUPS)]
        prev = [state_ref[j] for j in range(N_PAIR)]
        ys, news = _ssd_chunk(xs, bs, cs, small_ref[...], p_ref[0:1, :], p_ref[1:2, :], p_ref[2:3, :], prev)
        for j in range(N_PAIR):
            y_ref[:, LANES * j:LANES * (j + 1)] = ys[j]
            prev_ref[0, j] = prev[j]
            state_ref[j] = news[j]

    return pl.pallas_call(
        body, name="ssd_fwd", grid=(nch,),
        in_specs=[pl.BlockSpec((SSM_CHUNK, CONV_CH), lambda c: (c, 0)),
                  pl.BlockSpec((SSM_CHUNK, LANES), lambda c: (c, OFF_SMALL // LANES)),
                  pl.BlockSpec((None, 8, LANES), lambda c: (l, 0, 0))],
        out_specs=[pl.BlockSpec((SSM_CHUNK, D_SSM), lambda c: (c, 0)),
                   pl.BlockSpec((1, N_PAIR, SSM_STATE, LANES), lambda c: (c, 0, 0, 0))],
        out_shape=[jax.ShapeDtypeStruct((s, D_SSM), F32), jax.ShapeDtypeStruct((nch, N_PAIR, SSM_STATE, LANES), F32)],
        scratch_shapes=[pltpu.VMEM((N_PAIR, SSM_STATE, LANES), F32)],
        compiler_params=_params(dimension_semantics=("arbitrary",)),
    )(xbc, proj, ptile)


def ssd_bwd(xbc, proj, ptile, l, prevs, dy):
    s = xbc.shape[0]
    nch = s // SSM_CHUNK

    def body(xbc_ref, small_ref, p_ref, prev_ref, dy_ref, dxbc_ref, dsmall_ref, dp_ref, dstate_ref):
        @pl.when(pl.program_id(0) == 0)
        def _():
            dstate_ref[...] = jnp.zeros_like(dstate_ref)
            dp_ref[...] = jnp.zeros_like(dp_ref)

        xs = [xbc_ref[:, LANES * j:LANES * (j + 1)] for j in range(N_PAIR)]
        bs = [xbc_ref[:, D_SSM + LANES * g:D_SSM + LANES * (g + 1)] for g in range(SSM_GROUPS)]
        cs = [xbc_ref[:, D_SSM + 512 + LANES * g:D_SSM + 512 + LANES * (g + 1)] for g in range(SSM_GROUPS)]
        prev = [prev_ref[0, j] for j in range(N_PAIR)]
        dys = [dy_ref[:, LANES * j:LANES * (j + 1)] for j in range(N_PAIR)]
        dnew = [dstate_ref[j] for j in range(N_PAIR)]
        _, vjp = jax.vjp(_ssd_chunk, xs, bs, cs, small_ref[...], p_ref[0:1, :], p_ref[1:2, :], p_ref[2:3, :], prev)
        dxs, dbs, dcs, dsmall, ddtb, dalog, ddsk, dprev = vjp((dys, dnew))
        for j in range(N_PAIR):
            dxbc_ref[:, LANES * j:LANES * (j + 1)] = dxs[j]
            dstate_ref[j] = dprev[j]
        for g in range(SSM_GROUPS):
            dxbc_ref[:, D_SSM + LANES * g:D_SSM + LANES * (g + 1)] = dbs[g]
            dxbc_ref[:, D_SSM + 512 + LANES * g:D_SSM + 512 + LANES * (g + 1)] = dcs[g]
        dsmall_ref[...] = dsmall
        dp_ref[0:1, :] += ddtb
        dp_ref[1:2, :] += dalog
        dp_ref[2:3, :] += ddsk

    rev = lambda c: nch - 1 - c
    return pl.pallas_call(
        body, name="ssd_bwd", grid=(nch,),
        in_specs=[pl.BlockSpec((SSM_CHUNK, CONV_CH), lambda c: (rev(c), 0)),
                  pl.BlockSpec((SSM_CHUNK, LANES), lambda c: (rev(c), OFF_SMALL // LANES)),
                  pl.BlockSpec((None, 8, LANES), lambda c: (l, 0, 0)),
                  pl.BlockSpec((1, N_PAIR, SSM_STATE, LANES), lambda c: (rev(c), 0, 0, 0)),
                  pl.BlockSpec((SSM_CHUNK, D_SSM), lambda c: (rev(c), 0))],
        out_specs=[pl.BlockSpec((SSM_CHUNK, CONV_CH), lambda c: (rev(c), 0)),
                   pl.BlockSpec((SSM_CHUNK, LANES), lambda c: (rev(c), 0)),
                   pl.BlockSpec((8, LANES), lambda c: (0, 0))],
        out_shape=[jax.ShapeDtypeStruct((s, CONV_CH), F32), jax.ShapeDtypeStruct((s, LANES), F32),
                   jax.ShapeDtypeStruct((8, LANES), F32)],
        scratch_shapes=[pltpu.VMEM((N_PAIR, SSM_STATE, LANES), F32)],
        compiler_params=_params(dimension_semantics=("arbitrary",)),
    )(xbc, proj, ptile, prevs, dy)


ROPE_TM = 256


def _rope_tile(t, cosm, sinm):
    lane = lax.broadcasted_iota(jnp.int32, t.shape, 1)
    half = QK_ROPE // 2
    partner = jnp.where(lane < ROPE_LANE0 + half, pltpu.roll(t, LANES - half, 1), pltpu.roll(t, half, 1))
    return t * cosm + partner * sinm


def _in_rope(shape):
    lane = lax.broadcasted_iota(jnp.int32, shape, 1)
    return jnp.logical_and(lane >= ROPE_LANE0, lane < ROPE_LANE0 + QK_ROPE)


def build_k(kn, proj, cosm, sinm):
    s, w = kn.shape

    def body(k_ref, small_ref, c_ref, s_ref, o_ref):
        small = small_ref[...]
        inrope = _in_rope(small.shape)
        kpe = jnp.where(inrope, _rope_tile(jnp.where(inrope, small, 0.0), c_ref[...], s_ref[...]), 0.0)
        for h in range(MLA_HEADS):
            sl = slice(HEAD_PAD * h, HEAD_PAD * (h + 1))
            o_ref[:, sl] = (k_ref[:, sl].astype(F32) + kpe).astype(o_ref.dtype)

    row = pl.BlockSpec((ROPE_TM, w), lambda i: (i, 0))
    tab = pl.BlockSpec((ROPE_TM, LANES), lambda i: (i, 0))
    return pl.pallas_call(
        body, name="build_k", grid=(s // ROPE_TM,),
        in_specs=[row, pl.BlockSpec((ROPE_TM, LANES), lambda i: (i, OFF_SMALL // LANES)), tab, tab], out_specs=row,
        out_shape=jax.ShapeDtypeStruct((s, w), BF16), compiler_params=_params(dimension_semantics=("arbitrary",)),
    )(kn, proj, cosm, sinm)


def dsmall_bwd(dk, dsmall_ssd, cosm, sinm_neg):
    def fn(dkt, ds, c, sn):
        inrope = _in_rope(ds.shape)
        tot = dkt[:, 0:HEAD_PAD]
        for h in range(1, MLA_HEADS):
            tot = tot + dkt[:, HEAD_PAD * h:HEAD_PAD * (h + 1)]
        tot = jnp.where(inrope, tot, 0.0)
        return ds + jnp.where(inrope, _rope_tile(tot, c, sn), 0.0)

    return rowwise(fn, [(dk, MLA_HEADS * HEAD_PAD, 0), (dsmall_ssd, LANES, 0), (cosm, LANES, 0), (sinm_neg, LANES, 0)],
                   [], [(LANES, BF16)], [], "dsmall_bwd")[0]


ATT_TQ = 512
ATT_SCALE = (QK_NOPE + QK_ROPE) ** -0.5


def _att_scores(qh, kh, q0):
    s = lax.dot_general(qh, kh, _DIMS['nt'], preferred_element_type=F32) * ATT_SCALE
    r = lax.broadcasted_iota(jnp.int32, s.shape, 0) + q0
    c = lax.broadcasted_iota(jnp.int32, s.shape, 1)
    return jnp.where(c <= r, s, -1e30)


def mla_fwd(q, k, v):
    s = q.shape[0]

    def body(q_ref, k_ref, v_ref, o_ref, lse_ref):
        lane = lax.broadcasted_iota(jnp.int32, (ATT_TQ, LANES), 1)

        def block(ib):
            n = ATT_TQ * (ib + 1)
            v_t = v_ref[0:n, :]
            vlane = lax.broadcasted_iota(jnp.int32, v_t.shape, 1)
            o_tot, lse_tot = None, None
            for h in range(2):
                hs = slice(HEAD_PAD * h, HEAD_PAD * (h + 1))
                sc = _att_scores(q_ref[:, hs], k_ref[0:n, hs], ATT_TQ * ib)
                m = jnp.max(sc, axis=1, keepdims=True)
                p = jnp.exp(sc - m)
                l = jnp.sum(p, axis=1, keepdims=True)
                vh = jnp.where((vlane < V_DIM) if h == 0 else (vlane >= V_DIM), v_t, jnp.zeros_like(v_t))
                oh = lax.dot_general(p.astype(BF16), vh, _DIMS['nn'], preferred_element_type=F32) / l
                lse_h = jnp.where((lane < V_DIM) if h == 0 else (lane >= V_DIM), m + jnp.log(l), 0.0)
                o_tot = oh if o_tot is None else o_tot + oh
                lse_tot = lse_h if lse_tot is None else lse_tot + lse_h
            o_ref[...] = o_tot
            lse_ref[...] = lse_tot

        for ib in range(s // ATT_TQ):
            pl.when(pl.program_id(1) == ib)(functools.partial(block, ib))

    tile = pl.BlockSpec((ATT_TQ, LANES), lambda p, i: (i, p))
    return pl.pallas_call(
        body, name="mla_fwd", grid=(MLA_HEADS // 2, s // ATT_TQ),
        in_specs=[pl.BlockSpec((ATT_TQ, 2 * HEAD_PAD), lambda p, i: (i, p)),
                  pl.BlockSpec((s, 2 * HEAD_PAD), lambda p, i: (0, p)),
                  pl.BlockSpec((s, LANES), lambda p, i: (0, p))],
        out_specs=[tile, tile],
        out_shape=[jax.ShapeDtypeStruct((s, MLA_HEADS * V_DIM), F32)] * 2,
        compiler_params=_params(dimension_semantics=("arbitrary", "arbitrary")),
    )(q, k, v)


def mla_bwd(q, k, v, o, lse, do, cosm, sinm_neg):
    s = q.shape[0]

    def body(q_ref, k_ref, v_ref, o_ref, lse_ref, do_ref, c_ref, s_ref, dq_ref, dk_ref, dv_ref):
        i = pl.program_id(1)

        @pl.when(i == 0)
        def _():
            dk_ref[...] = jnp.zeros_like(dk_ref)
            dv_ref[...] = jnp.zeros_like(dv_ref)

        def block(ib):
            n = ATT_TQ * (ib + 1)
            o_t = o_ref[...]
            do_t = do_ref[...]
            lse_t = lse_ref[...]
            v_t = v_ref[0:n, :]
            lane = lax.broadcasted_iota(jnp.int32, do_t.shape, 1)
            for h in range(2):
                hs = slice(HEAD_PAD * h, HEAD_PAD * (h + 1))
                sel = (lane < V_DIM) if h == 0 else (lane >= V_DIM)
                qh = q_ref[:, hs]
                kh = k_ref[0:n, hs]
                doh = jnp.where(sel, do_t, 0.0)
                delta = jnp.sum(doh * o_t, axis=1, keepdims=True)
                lse_h = jnp.max(jnp.where(sel, lse_t, -jnp.inf), axis=1, keepdims=True)
                doh_b = doh.astype(BF16)
                p = jnp.exp(_att_scores(qh, kh, ATT_TQ * ib) - lse_h)
                dv_ref[0:n, :] += lax.dot_general(p.astype(BF16), doh_b, _DIMS['tn'], preferred_element_type=F32)
                dp = lax.dot_general(doh_b, v_t, _DIMS['nt'], preferred_element_type=F32)
                ds = (p * (dp - delta) * ATT_SCALE).astype(BF16)
                dk_ref[0:n, hs] += lax.dot_general(ds, qh, _DIMS['tn'], preferred_element_type=F32)
                dq = lax.dot_general(ds, kh, _DIMS['nn'], preferred_element_type=F32)
                dq_ref[:, hs] = _rope_tile(dq, c_ref[...], s_ref[...]).astype(dq_ref.dtype)

        for ib in range(s // ATT_TQ):
            pl.when(i == ib)(functools.partial(block, ib))

    tile = pl.BlockSpec((ATT_TQ, LANES), lambda p, i: (i, p))
    return pl.pallas_call(
        body, name="mla_bwd", grid=(MLA_HEADS // 2, s // ATT_TQ),
        in_specs=[pl.BlockSpec((ATT_TQ, 2 * HEAD_PAD), lambda p, i: (i, p)),
                  pl.BlockSpec((s, 2 * HEAD_PAD), lambda p, i: (0, p)),
                  pl.BlockSpec((s, LANES), lambda p, i: (0, p)), tile, tile, tile,
                  pl.BlockSpec((ATT_TQ, LANES), lambda p, i: (i, 0)), pl.BlockSpec((ATT_TQ, LANES), lambda p, i: (i, 0))],
        out_specs=[pl.BlockSpec((ATT_TQ, 2 * HEAD_PAD), lambda p, i: (i, p)),
                   pl.BlockSpec((s, 2 * HEAD_PAD), lambda p, i: (0, p)),
                   pl.BlockSpec((s, LANES), lambda p, i: (0, p))],
        out_shape=[jax.ShapeDtypeStruct((s, MLA_HEADS * HEAD_PAD), BF16),
                   jax.ShapeDtypeStruct((s, MLA_HEADS * HEAD_PAD), F32),
                   jax.ShapeDtypeStruct((s, MLA_HEADS * V_DIM), F32)],
        compiler_params=_params(dimension_semantics=("arbitrary", "arbitrary")),
    )(q, k, v, o, lse, do, cosm, sinm_neg)


MEM_TQ = 256
MEM_SCALE = MEM_HEAD_DIM ** -0.5


def _mem_probs(qh, kh):
    s = lax.dot_general(qh, kh, _DIMS['nt'], preferred_element_type=F32) * MEM_SCALE
    p = jnp.exp(s - jnp.max(s, axis=1, keepdims=True))
    return p / jnp.sum(p, axis=1, keepdims=True)


def mem_fwd(q, k, v):
    s = q.shape[0]

    def body(q_ref, k_ref, v_ref, o_ref):
        for h in range(MEM_HEADS):
            sl = slice(MEM_HEAD_DIM * h, MEM_HEAD_DIM * (h + 1))
            p = _mem_probs(q_ref[:, sl], k_ref[:, sl])
            o_ref[:, sl] = lax.dot_general(p.astype(BF16), v_ref[:, sl], _DIMS['nn'],
                                           preferred_element_type=F32).astype(o_ref.dtype)

    full = pl.BlockSpec((MEM_LEN, D_MODEL), lambda i: (0, 0))
    return pl.pallas_call(
        body, name="mem_fwd", grid=(s // MEM_TQ,),
        in_specs=[pl.BlockSpec((MEM_TQ, D_MODEL), lambda i: (i, 0)), full, full],
        out_specs=pl.BlockSpec((MEM_TQ, D_MODEL), lambda i: (i, 0)),
        out_shape=jax.ShapeDtypeStruct((s, D_MODEL), BF16),
        compiler_params=_params(dimension_semantics=("arbitrary",)),
    )(q, k, v)


def mem_bwd(q, k, v, do):
    s = q.shape[0]

    def body(q_ref, k_ref, v_ref, do_ref, dq_ref, dk_ref, dv_ref):
        @pl.when(pl.program_id(0) == 0)
        def _():
            dk_ref[...] = jnp.zeros_like(dk_ref)
            dv_ref[...] = jnp.zeros_like(dv_ref)

        for h in range(MEM_HEADS):
            sl = slice(MEM_HEAD_DIM * h, MEM_HEAD_DIM * (h + 1))
            qh, kh, vh = q_ref[:, sl], k_ref[:, sl], v_ref[:, sl]
            doh = do_ref[:, sl].astype(BF16)
            p = _mem_probs(qh, kh)
            dv_ref[:, sl] += lax.dot_general(p.astype(BF16), doh, _DIMS['tn'], preferred_element_type=F32)
            dp = lax.dot_general(doh, vh, _DIMS['nt'], preferred_element_type=F32)
            ds = (p * (dp - jnp.sum(p * dp, axis=1, keepdims=True)) * MEM_SCALE).astype(BF16)
            dq_ref[:, sl] = lax.dot_general(ds, kh, _DIMS['nn'], preferred_element_type=F32).astype(dq_ref.dtype)
            dk_ref[:, sl] += lax.dot_general(ds, qh, _DIMS['tn'], preferred_element_type=F32)

    full = pl.BlockSpec((MEM_LEN, D_MODEL), lambda i: (0, 0))
    row = pl.BlockSpec((MEM_TQ, D_MODEL), lambda i: (i, 0))
    return pl.pallas_call(
        body, name="mem_bwd", grid=(s // MEM_TQ,),
        in_specs=[row, full, full, row], out_specs=[row, full, full],
        out_shape=[jax.ShapeDtypeStruct((s, D_MODEL), BF16), jax.ShapeDtypeStruct((MEM_LEN, D_MODEL), F32),
                   jax.ShapeDtypeStruct((MEM_LEN, D_MODEL), F32)],
        compiler_params=_params(dimension_semantics=("arbitrary",)),
    )(q, k, v, do)


def _gate_norm(y, z, g):
    return _rms(y * _silu(z), g)


def gate_norm_fwd(y, proj, g):
    return rowwise(_gate_norm, [(y, D_SSM, 0), (proj, D_SSM, OFF_Z // D_SSM)], [g], [(D_SSM, BF16, D_MIX, 0)], [],
                   "gate_norm_fwd")[0]


def gate_norm_bwd(y, proj, g, dmix, tie=None):
    def fn(yt, zt, dt_, gt):
        _, vjp = jax.vjp(_gate_norm, yt, zt, gt)
        return vjp(dt_.astype(F32))

    return rowwise(fn, [(y, D_SSM, 0), (proj, D_SSM, OFF_Z // D_SSM), (dmix, D_SSM, 0)], [g],
                   [(D_SSM, F32), (D_SSM, BF16)], [((1, D_SSM), F32)], "gate_norm_bwd", tie=tie)


def loss_head(x, g, target):
    def fn(xt, tt, gt):
        def f(x_, g_):
            err = _rms(x_, g_) - tt
            return 0.5 * jnp.sum(jnp.mean(err * err, axis=-1))

        lv, (dx, dg) = jax.value_and_grad(f, argnums=(0, 1))(xt, gt)
        return dx, dg, jnp.full((1, LANES), lv, F32)

    return rowwise(fn, [(x, D_MODEL, 0), (target, D_MODEL, 0)], [g], [(D_MODEL, F32)],
                   [((1, D_MODEL), F32), ((1, LANES), F32)], "loss_head")


def _proj_runs(d):
    lo, hi = (D_IN // N_DEV) * d, (D_IN // N_DEV) * (d + 1)
    runs = []
    for a, b, new in PROJ_SEGS:
        s0, s1 = max(a, lo), min(b, hi)
        if s0 < s1:
            runs.append((s0 - lo, new + s0 - a, s1 - s0))
    return runs


LAYOUT_TM = 256


def assemble_proj(g):
    def body(g_ref, o_ref):
        o_ref[:, OFF_SMALL:OFF_SMALL + LANES] = jnp.zeros((LAYOUT_TM, LANES), o_ref.dtype)
        for d in range(N_DEV):
            for src, dst, n in _proj_runs(d):
                o_ref[:, dst:dst + n] = g_ref[d, :, src:src + n]

    return pl.pallas_call(
        body, name="assemble_proj", grid=(D_MODEL // LAYOUT_TM,),
        in_specs=[pl.BlockSpec((N_DEV, LAYOUT_TM, D_IN // N_DEV), lambda i: (0, i, 0))],
        out_specs=pl.BlockSpec((LAYOUT_TM, PROJ_W), lambda i: (i, 0)),
        out_shape=jax.ShapeDtypeStruct((D_MODEL, PROJ_W), g.dtype),
        compiler_params=_params(dimension_semantics=("arbitrary",)),
    )(g)


def extract_proj(dz, dxbc, dcq, dsmall, dckv):
    pieces = [(OFF_Z, 1024), (OFF_XBC, 2048), (OFF_CQ, Q_LORA), (OFF_SMALL, LANES), (OFF_CKV, KV_LORA)]

    def body(*refs):
        o_ref = refs[-1]
        for d in range(N_DEV):
            for src, dst, n in _proj_runs(d):
                for p, (off, w) in enumerate(pieces):
                    if off <= dst < off + w:
                        o_ref[d, :, src:src + n] = refs[p][:, dst - off:dst - off + n].astype(o_ref.dtype)

    return pl.pallas_call(
        body, name="extract_proj", grid=(D_MODEL // LAYOUT_TM,),
        in_specs=[pl.BlockSpec((LAYOUT_TM, w), lambda i: (i, 0)) for _, w in pieces],
        out_specs=pl.BlockSpec((N_DEV, LAYOUT_TM, D_IN // N_DEV), lambda i: (0, i, 0)),
        out_shape=jax.ShapeDtypeStruct((N_DEV, D_MODEL, D_IN // N_DEV), BF16),
        compiler_params=_params(dimension_semantics=("arbitrary",)),
    )(dz, dxbc, dcq, dsmall, dckv)


_QW = QK_NOPE + QK_ROPE


def assemble_uq(g):
    def body(g_ref, o_ref):
        o_ref[...] = jnp.zeros_like(o_ref)
        for d in range(N_DEV):
            for e in range(2):
                dst = HEAD_PAD * (2 * d + e)
                o_ref[:, dst:dst + _QW] = g_ref[d, :, _QW * e:_QW * (e + 1)]

    return pl.pallas_call(
        body, name="assemble_uq", grid=(1,),
        in_specs=[pl.BlockSpec((N_DEV, Q_LORA, 2 * _QW), lambda i: (0, 0, 0))],
        out_specs=pl.BlockSpec((Q_LORA, MLA_HEADS * HEAD_PAD), lambda i: (0, 0)),
        out_shape=jax.ShapeDtypeStruct((Q_LORA, MLA_HEADS * HEAD_PAD), g.dtype),
        compiler_params=_params(dimension_semantics=("arbitrary",)),
    )(g)


def extract_uq(dw):
    def body(w_ref, o_ref):
        for d in range(N_DEV):
            for e in range(2):
                src = HEAD_PAD * (2 * d + e)
                o_ref[d, :, _QW * e:_QW * (e + 1)] = w_ref[:, src:src + _QW].astype(o_ref.dtype)

    return pl.pallas_call(
        body, name="extract_uq", grid=(1,),
        in_specs=[pl.BlockSpec((Q_LORA, MLA_HEADS * HEAD_PAD), lambda i: (0, 0))],
        out_specs=pl.BlockSpec((N_DEV, Q_LORA, 2 * _QW), lambda i: (0, 0, 0)),
        out_shape=jax.ShapeDtypeStruct((N_DEV, Q_LORA, 2 * _QW), BF16),
        compiler_params=_params(dimension_semantics=("arbitrary",)),
    )(dw)


def assemble_ukv(g):
    def body(g_ref, kn_ref, v_ref):
        kn_ref[...] = jnp.zeros_like(kn_ref)
        for d in range(N_DEV):
            for e in range(2):
                h = 2 * d + e
                kn_ref[:, HEAD_PAD * h:HEAD_PAD * h + QK_NOPE] = g_ref[d, :, 128 * e:128 * e + QK_NOPE]
                v_ref[:, V_DIM * h:V_DIM * (h + 1)] = g_ref[d, :, 128 * e + QK_NOPE:128 * (e + 1)]

    return pl.pallas_call(
        body, name="assemble_ukv", grid=(1,),
        in_specs=[pl.BlockSpec((N_DEV, KV_LORA, 256), lambda i: (0, 0, 0))],
        out_specs=[pl.BlockSpec((KV_LORA, MLA_HEADS * HEAD_PAD), lambda i: (0, 0)),
                   pl.BlockSpec((KV_LORA, MLA_HEADS * V_DIM), lambda i: (0, 0))],
        out_shape=[jax.ShapeDtypeStruct((KV_LORA, MLA_HEADS * HEAD_PAD), g.dtype),
                   jax.ShapeDtypeStruct((KV_LORA, MLA_HEADS * V_DIM), g.dtype)],
        compiler_params=_params(dimension_semantics=("arbitrary",)),
    )(g)


def extract_ukv(dkn, dv):
    def body(kn_ref, v_ref, o_ref):
        for d in range(N_DEV):
            for e in range(2):
                h = 2 * d + e
                o_ref[d, :, 128 * e:128 * e + QK_NOPE] = kn_ref[:, HEAD_PAD * h:HEAD_PAD * h + QK_NOPE].astype(o_ref.dtype)
                o_ref[d, :, 128 * e + QK_NOPE:128 * (e + 1)] = v_ref[:, V_DIM * h:V_DIM * (h + 1)].astype(o_ref.dtype)

    return pl.pallas_call(
        body, name="extract_ukv", grid=(1,),
        in_specs=[pl.BlockSpec((KV_LORA, MLA_HEADS * HEAD_PAD), lambda i: (0, 0)),
                  pl.BlockSpec((KV_LORA, MLA_HEADS * V_DIM), lambda i: (0, 0))],
        out_specs=pl.BlockSpec((N_DEV, KV_LORA, 256), lambda i: (0, 0, 0)),
        out_shape=jax.ShapeDtypeStruct((N_DEV, KV_LORA, 256), BF16),
        compiler_params=_params(dimension_semantics=("arbitrary",)),
    )(dkn, dv)


_UPW = 2 * D_FF // N_DEV


def assemble_up(g):
    def body(g_ref, wg_ref, wv_ref):
        for d in range(N_DEV):
            ref = wg_ref if d < N_DEV // 2 else wv_ref
            off = _UPW * (d % (N_DEV // 2))
            ref[:, off:off + _UPW] = g_ref[d]

    half = pl.BlockSpec((LAYOUT_TM, D_FF), lambda i: (i, 0))
    return pl.pallas_call(
        body, name="assemble_up", grid=(D_MODEL // LAYOUT_TM,),
        in_specs=[pl.BlockSpec((N_DEV, LAYOUT_TM, _UPW), lambda i: (0, i, 0))],
        out_specs=[half, half], out_shape=[jax.ShapeDtypeStruct((D_MODEL, D_FF), g.dtype)] * 2,
        compiler_params=_params(dimension_semantics=("arbitrary",)),
    )(g)


def extract_up(dwg, dwv):
    def body(wg_ref, wv_ref, o_ref):
        for d in range(N_DEV):
            ref = wg_ref if d < N_DEV // 2 else wv_ref
            off = _UPW * (d % (N_DEV // 2))
            o_ref[d] = ref[:, off:off + _UPW].astype(o_ref.dtype)

    half = pl.BlockSpec((LAYOUT_TM, D_FF), lambda i: (i, 0))
    return pl.pallas_call(
        body, name="extract_up", grid=(D_MODEL // LAYOUT_TM,), in_specs=[half, half],
        out_specs=pl.BlockSpec((N_DEV, LAYOUT_TM, _UPW), lambda i: (0, i, 0)),
        out_shape=jax.ShapeDtypeStruct((N_DEV, D_MODEL, _UPW), BF16),
        compiler_params=_params(dimension_semantics=("arbitrary",)),
    )(dwg, dwv)


MESH = pl.DeviceIdType.MESH
ANY = pl.BlockSpec(memory_space=pl.ANY)


def _place():
    mx, my, mc = lax.axis_index("x"), lax.axis_index("y"), lax.axis_index("c")
    return mx, my, mc, [(1 - mx, my), (mx, 1 - my), (1 - mx, 1 - my)]


def all_gather_blocks(xs, first_only=()):
    n = len(xs)

    def body(*refs):
        x_refs, out_refs = refs[:n], refs[n:2 * n]
        send_sems, recv_sems, local_sems = refs[2 * n:]
        mx, my, mc, chips = _place()
        me, sibling = (mx, my, mc), (mx, my, 1 - mc)
        x_refs = [x_refs[t].at[0] if t in first_only else x_refs[t] for t in range(n)]

        def rows(t, px, py, pc):
            dev = 4 * px + 2 * py + pc
            return out_refs[t].at[dev] if t in first_only else out_refs[t].at[:, dev]

        def copy(t, k, block, to, src=None):
            return pltpu.make_async_remote_copy(
                src_ref=rows(t, *block) if src is None else src, dst_ref=rows(t, *block),
                send_sem=send_sems.at[t, k], recv_sem=recv_sems.at[t, k], device_id=to, device_id_type=MESH)

        mine = [pltpu.make_async_copy(x_refs[t], rows(t, *me), local_sems.at[t]) for t in range(n)]
        for cp in mine:
            cp.start()
        first = []
        for t in range(n):
            first.append(copy(t, 0, me, sibling, src=x_refs[t]))
            first += [copy(t, 1 + j, me, (*chip, mc), src=x_refs[t]) for j, chip in enumerate(chips)]
        for cp in first:
            cp.start()
        passed = []
        for j, chip in enumerate(chips):
            for t in range(n):
                copy(t, 1 + j, (*chip, mc), me).wait_recv()
                cp = copy(t, 4 + j, (*chip, mc), sibling)
                cp.start()
                passed.append(cp)
        for t in range(n):
            copy(t, 0, sibling, me).wait_recv()
            for j, chip in enumerate(chips):
                copy(t, 4 + j, (*chip, 1 - mc), me).wait_recv()
        for cp in first + passed:
            cp.wait_send()
        for cp in mine:
            cp.wait()

    return pl.pallas_call(
        body, name="all_gather_blocks",
        out_shape=[jax.ShapeDtypeStruct(((N_DEV,) if t in first_only else (x.shape[0], N_DEV)) + x.shape[1:], x.dtype)
                   for t, x in enumerate(xs)],
        in_specs=[ANY] * n, out_specs=[ANY] * n,
        scratch_shapes=[pltpu.SemaphoreType.DMA((n, 7)), pltpu.SemaphoreType.DMA((n, 7)), pltpu.SemaphoreType.DMA((n,))],
    )(*xs)


HBM = pl.BlockSpec(memory_space=pltpu.HBM)
SEM = pl.BlockSpec(memory_space=pltpu.SEMAPHORE)
EFFECT = pltpu.SideEffectType.DATAFLOW_SIDE_EFFECTING
ALL_DEVICES = [(px, py, pc) for px in range(2) for py in range(2) for pc in range(2)]


def _hbm(x):
    return pltpu.with_memory_space_constraint(x, pltpu.HBM)


def _split_start(body, name, srcs, lands, after=None):
    ns, n = len(srcs), len(lands)
    extra = [after] if after is not None else []

    def full_body(*refs):
        sems = ns + n + len(extra)
        body(refs[:ns], refs[ns:ns + n], refs[sems], refs[sems + 1])
        refs[-1][...] = jnp.zeros_like(refs[-1])

    res = pl.pallas_call(
        full_body, name=name,
        out_shape=(pltpu.SemaphoreType.DMA((n,)), pltpu.SemaphoreType.DMA((n,)),
                   *[pltpu.HBM(x.shape, x.dtype) for x in srcs], *[pltpu.HBM(x.shape, x.dtype) for x in lands],
                   jax.ShapeDtypeStruct((8, LANES), F32)),
        in_specs=[HBM] * (ns + n) + [ANY] * len(extra),
        out_specs=(SEM, SEM, *[HBM] * (ns + n), pl.BlockSpec(memory_space=pltpu.VMEM)),
        input_output_aliases={i: 2 + i for i in range(ns + n)},
        compiler_params=pltpu.CompilerParams(has_side_effects=EFFECT),
    )(*[_hbm(x) for x in srcs], *[_hbm(x) for x in lands], *extra)
    return res[0], res[1], list(res[2:2 + ns]), list(res[2 + ns:2 + ns + n]), res[-1]


def _split_wait(name, send_sems, recv_sems, srcs, lands, after, sent, landed):
    ns, n = len(srcs), len(lands)

    def body(*refs):
        src_refs, land_refs, ssem, rsem = refs[:ns], refs[ns:ns + n], refs[ns + n], refs[ns + n + 1]
        mx, my, mc, _ = _place()
        for t in range(n):
            out = sent(src_refs[t] if ns else None, land_refs[t])
            inn = landed(land_refs[t])
            pltpu.make_async_remote_copy(src_ref=out, dst_ref=out, send_sem=ssem.at[t], recv_sem=rsem.at[t],
                                         device_id=(mx, my, mc), device_id_type=MESH).wait_send()
            pltpu.make_async_remote_copy(src_ref=inn, dst_ref=inn, send_sem=ssem.at[t], recv_sem=rsem.at[t],
                                         device_id=(mx, my, mc), device_id_type=MESH).wait_recv()

    res = pl.pallas_call(
        body, name=name,
        out_shape=(*[pltpu.HBM(x.shape, x.dtype) for x in srcs], *[pltpu.HBM(x.shape, x.dtype) for x in lands]),
        in_specs=[HBM] * (ns + n) + [SEM, SEM, ANY], out_specs=[HBM] * (ns + n),
        input_output_aliases={i: i for i in range(ns + n)},
        compiler_params=pltpu.CompilerParams(has_side_effects=EFFECT),
    )(*srcs, *lands, send_sems, recv_sems, after)
    return list(res[:ns]), list(res[ns:])


FIRST_HOP = 5
SECOND_HOP = 3


def gather_start(srcs, l, tag, after=None):
    lands = [lax.empty((N_DEV,) + x.shape[1:], x.dtype) for x in srcs]

    def body(src_refs, land_refs, send_sems, recv_sems):
        mx, my, mc, chips = _place()
        me = 4 * mx + 2 * my + mc
        for t in range(len(srcs)):
            for to in [(mx, my, mc), (mx, my, 1 - mc)] + [(cx, cy, mc) for cx, cy in chips]:
                pltpu.make_async_remote_copy(
                    src_ref=src_refs[t].at[l], dst_ref=land_refs[t].at[me], send_sem=send_sems.at[t],
                    recv_sem=recv_sems.at[t], device_id=to, device_id_type=MESH).start()

    return _split_start(body, "gather_start_%d%s" % (l, tag), srcs, lands, after=after)


def gather_wait(l, tag, send_sems, recv_sems, srcs, lands, after):
    hop = lambda d: d.at[pl.ds(0, FIRST_HOP)]
    return _split_wait("gather_wait_%d%s" % (l, tag), send_sems, recv_sems, srcs, lands, after,
                       sent=lambda s, d: hop(d), landed=hop)


def gather_pass_start(lands, l, tag):
    def body(src_refs, land_refs, send_sems, recv_sems):
        mx, my, mc, chips = _place()
        for t in range(len(lands)):
            for cx, cy in chips:
                slot = land_refs[t].at[4 * cx + 2 * cy + mc]
                pltpu.make_async_remote_copy(
                    src_ref=slot, dst_ref=slot, send_sem=send_sems.at[t], recv_sem=recv_sems.at[t],
                    device_id=(mx, my, 1 - mc), device_id_type=MESH).start()

    send_sems, recv_sems, _, lands, tie = _split_start(body, "gather_pass_start_%d%s" % (l, tag), [], lands)
    return send_sems, recv_sems, lands, tie


def gather_pass_wait(l, tag, send_sems, recv_sems, lands, after):
    hop = lambda d: d.at[pl.ds(0, SECOND_HOP)]
    return _split_wait("gather_pass_wait_%d%s" % (l, tag), send_sems, recv_sems, [], lands, after,
                       sent=lambda s, d: hop(d), landed=hop)[1]


def small_gather_start(rows):
    def body(src_refs, land_refs, send_sems, recv_sems):
        mx, my, mc, _ = _place()
        for to in ALL_DEVICES:
            pltpu.make_async_remote_copy(
                src_ref=src_refs[0], dst_ref=land_refs[0].at[4 * mx + 2 * my + mc], send_sem=send_sems.at[0],
                recv_sem=recv_sems.at[0], device_id=to, device_id_type=MESH).start()

    return _split_start(body, "small_gather_start", [rows], [lax.empty((N_DEV,) + rows.shape, rows.dtype)])


def small_gather_wait(send_sems, recv_sems, srcs, lands, after):
    return _split_wait("small_gather_wait", send_sems, recv_sems, srcs, lands, after,
                       sent=lambda s, d: d, landed=lambda d: d)[1][0]


def grad_exchange_start(es, lands, l, tag, after=None):
    def body(e_refs, land_refs, send_sems, recv_sems):
        mx, my, mc, _ = _place()
        me = 4 * mx + 2 * my + mc
        for t in range(len(es)):
            for px, py, pc in ALL_DEVICES:
                pltpu.make_async_remote_copy(
                    src_ref=e_refs[t].at[4 * px + 2 * py + pc], dst_ref=land_refs[t].at[l, me], send_sem=send_sems.at[t],
                    recv_sem=recv_sems.at[t], device_id=(px, py, pc), device_id_type=MESH).start()

    return _split_start(body, "grad_exchange_start_%d%s" % (l, tag), es, lands, after=after)


def grad_exchange_wait(l, tag, send_sems, recv_sems, es, lands, after):
    return _split_wait("grad_exchange_wait_%d%s" % (l, tag), send_sems, recv_sems, es, lands, after,
                       sent=lambda s, d: s, landed=lambda d: d.at[l])


def _adam(g, w, m, v):
    nm = ADAM_B1 * m + (1.0 - ADAM_B1) * g
    nv = ADAM_B2 * v + (1.0 - ADAM_B2) * jnp.square(g)
    m_hat = nm / (1.0 - ADAM_B1 ** ADAM_STEP)
    v_hat = nv / (1.0 - ADAM_B2 ** ADAM_STEP)
    return -ADAM_LR * (m_hat / (jnp.sqrt(v_hat) + ADAM_EPS) + ADAM_WD * w), nm, nv


def adamw_big(parts, w, m, v, name, tie):
    depth, _, a, b = parts.shape
    ta = _row_tile(a)

    def body(p_ref, w_ref, m_ref, v_ref, tie_ref, g_ref, d_ref, nm_ref, nv_ref):
        g = p_ref[0].astype(F32)
        for k in range(1, N_DEV):
            g = g + p_ref[k].astype(F32)
        g_ref[...] = g
        d_ref[...], nm_ref[...], nv_ref[...] = _adam(g, w_ref[...], m_ref[...], v_ref[...])

    blk = pl.BlockSpec((None, ta, b), lambda l, i: (l, i, 0))
    return pl.pallas_call(
        body, name=name, grid=(depth, a // ta),
        in_specs=[pl.BlockSpec((None, N_DEV, ta, b), lambda l, i: (l, 0, i, 0)), blk, blk, blk, ANY], out_specs=[blk] * 4,
        out_shape=[jax.ShapeDtypeStruct((depth, a, b), F32)] * 4,
        compiler_params=_params(dimension_semantics=("arbitrary", "arbitrary")),
    )(parts, w, m, v, tie)


SMALL_VIEW = {'norm_mix': (DEPTH, 1024), 'ssm_norm': (DEPTH, 1024), 'attn_out_norm': (DEPTH, 1024),
              'norm_mem_q': (DEPTH, 1024), 'norm_mem_kv': (DEPTH, 1024), 'norm_ffn': (DEPTH, 1024),
              'q_norm': (DEPTH, 384), 'kv_norm': (DEPTH, 256), 'ssm_conv_b': (DEPTH, 2048), 'ffn_conv_b': (DEPTH, 5632),
              'dt_bias': (DEPTH, SSM_HEADS), 'a_log': (DEPTH, SSM_HEADS), 'd_skip': (DEPTH, SSM_HEADS),
              'ssm_conv_w': (DEPTH, SSM_CONV * CONV_CH // N_DEV), 'ffn_conv_w': (DEPTH, FFN_CONV * 2 * D_FF // N_DEV),
              'final_norm': (1, 1024)}
SMALL_NAMES = list(SMALL_VIEW)
SMALL_SHARDED = {'ssm_conv_w': (SSM_CONV, CONV_CH // N_DEV, CONV_CH), 'ffn_conv_w': (FFN_CONV, 2 * D_FF // N_DEV, 2 * D_FF)}


def adamw_small(gathered, ws, ms, vs, tie):
    nsm = len(SMALL_NAMES)

    def body(*refs):
        g8_ref = refs[0]
        w_refs, m_refs, v_refs = refs[1:1 + nsm], refs[1 + nsm:1 + 2 * nsm], refs[1 + 2 * nsm:1 + 3 * nsm]
        outs = refs[2 + 3 * nsm:2 + 7 * nsm]
        sum_ref = refs[2 + 7 * nsm]
        shard_bufs = refs[3 + 7 * nsm:]
        tot = g8_ref[0]
        for d in range(1, N_DEV):
            tot = tot + g8_ref[d]
        sum_ref[...] = tot
        mx, my, mc, _ = _place()
        dev = 4 * mx + 2 * my + mc

        def update(i, g):
            d, nm, nv = _adam(g, w_refs[i][...], m_refs[i][...], v_refs[i][...])
            outs[i][...] = g
            outs[nsm + i][...] = d
            outs[2 * nsm + i][...] = nm
            outs[3 * nsm + i][...] = nv

        for i, name in enumerate(SMALL_NAMES):
            rows, cols = SMALL_VIEW[name]
            off = SMALL_OFF[name]
            if name in SMALL_SHARDED:
                taps, per, full = SMALL_SHARDED[name]
                buf = shard_bufs[list(SMALL_SHARDED).index(name)]
                for d in range(N_DEV):
                    @pl.when(dev == d)
                    def _(d=d, taps=taps, per=per, full=full, off=off, buf=buf):
                        for k in range(taps):
                            buf[:, per * k:per * (k + 1)] = sum_ref[:, off + full * k + per * d:off + full * k + per * (d + 1)]
                update(i, buf[...])
            else:
                update(i, sum_ref[0:rows, off:off + cols])

    views = [jax.ShapeDtypeStruct(SMALL_VIEW[n], F32) for n in SMALL_NAMES]
    vmem = pl.BlockSpec(memory_space=pltpu.VMEM)
    res = pl.pallas_call(
        body, name="adamw_small", out_shape=views * 4, in_specs=[vmem] * (1 + 3 * nsm) + [ANY],
        out_specs=[vmem] * (4 * nsm),
        scratch_shapes=[pltpu.VMEM((DEPTH, SMALL_W), F32)] + [pltpu.VMEM(SMALL_VIEW[n], F32) for n in SMALL_SHARDED],
        compiler_params=_params(),
    )(gathered, *[ws[n] for n in SMALL_NAMES], *[ms[n] for n in SMALL_NAMES], *[vs[n] for n in SMALL_NAMES], tie)
    return [dict(zip(SMALL_NAMES, res[k * nsm:(k + 1) * nsm])) for k in range(4)]


def _layer_weights(gathered):
    w = {}
    for n, g in gathered.items():
        if n == 'w_in':
            w['w_proj'] = assemble_proj(g)
        elif n == 'w_uq':
            w['w_uq'] = assemble_uq(g)
        elif n == 'w_ukv':
            w['w_kn'], w['w_v'] = assemble_ukv(g)
        elif n == 'w_up':
            w['w_g'], w['w_vv'] = assemble_up(g)
        else:
            w[n] = g.reshape(N_DEV * BIG[n][0], BIG[n][1])
    return w


def _rope_post(acc, row_tiles, full_tiles, o_refs):
    for h in range(acc.shape[1] // HEAD_PAD):
        sl = slice(HEAD_PAD * h, HEAD_PAD * (h + 1))
        o_refs[0][:, sl] = _rope_tile(acc[:, sl], row_tiles[0], row_tiles[1]).astype(o_refs[0].dtype)


def _norm_post(acc, row_tiles, full_tiles, o_refs):
    o_refs[0][...] = acc
    o_refs[1][...] = _rms(acc, full_tiles[0]).astype(o_refs[1].dtype)


def _norm_bwd_post(acc, row_tiles, full_tiles, o_refs):
    _, vjp = jax.vjp(_rms, row_tiles[0], full_tiles[0])
    dx, dg = vjp(acc)
    o_refs[0][...] = dx + row_tiles[1]
    o_refs[1][...] += dg


def layer_fwd(x0, h1, mem, cosm, sinm, w, sm, l, tie=None):
    gain = lambda n: (sm[n], l)
    sv = dict(x0=x0)
    sv['h1'] = h1 if h1 is not None else rmsnorm_fwd(x0, gain('norm_mix'), "norm_mix_fwd", tie=tie)
    proj = sv['proj'] = matmul([(sv['h1'], w['w_proj'])], 'nn', F32, "proj_fwd", tie=tie if h1 is not None else None)
    sv['xbc'] = ssm_conv_fwd(proj, sm['ssm_conv_w'], sm['ssm_conv_b'], l)
    sv['y'], sv['prevs'] = ssd_fwd(sv['xbc'], proj, sm['ptile'], l)
    mix = gate_norm_fwd(sv['y'], proj, gain('ssm_norm'))
    sv['cqn'] = rmsnorm_fwd(proj, gain('q_norm'), "q_norm_fwd", Q_LORA, OFF_CQ // Q_LORA)
    sv['ckvn'] = rmsnorm_fwd(proj, gain('kv_norm'), "kv_norm_fwd", KV_LORA, OFF_CKV // KV_LORA)
    sv['q'] = matmul([(sv['cqn'], w['w_uq'])], 'nn', BF16, "uq_fwd", post=_rope_post, rows=[cosm, sinm])
    kn = matmul([(sv['ckvn'], w['w_kn'])], 'nn', BF16, "kn_fwd")
    sv['k'] = build_k(kn, proj, cosm, sinm)
    sv['v'] = matmul([(sv['ckvn'], w['w_v'])], 'nn', BF16, "v_fwd")
    sv['o'], sv['lse'] = mla_fwd(sv['q'], sv['k'], sv['v'])
    mix = sv['mix'] = rmsnorm_fwd(sv['o'], gain('attn_out_norm'), "attn_out_norm_fwd", out=(D_SSM, BF16, D_MIX, 1),
                                  into=(mix, 0))
    x1, sv['hq'] = matmul([(mix, w['w_out'])], 'nn', F32, "out_fwd", add=x0, post=_norm_post,
                          fulls=[gain('norm_mem_q')], outs=[F32, BF16], full_n=True)
    sv['x1'] = x1
    sv['mn'] = rmsnorm_fwd(mem, gain('norm_mem_kv'), "norm_mem_kv_fwd")
    if 'later' in w:
        w.update(w.pop('later')(sv['hq']))
    sv['mq'] = matmul([(sv['hq'], w['w_mq'])], 'nn', BF16, "mq_fwd")
    sv['mk'] = matmul([(sv['mn'], w['w_mk'])], 'nn', BF16, "mk_fwd")
    sv['mv'] = matmul([(sv['mn'], w['w_mv'])], 'nn', BF16, "mv_fwd")
    sv['om'] = mem_fwd(sv['mq'], sv['mk'], sv['mv'])
    x2, sv['h3'] = matmul([(sv['om'], w['w_mo'])], 'nn', F32, "mo_fwd", add=x1, post=_norm_post,
                          fulls=[gain('norm_ffn')], outs=[F32, BF16], full_n=True)
    sv['x2'] = x2
    tie_ffn = w.pop('prefetch')(sv['h3']) if 'prefetch' in w else None
    sv['ug'] = matmul([(sv['h3'], w['w_g'])], 'nn', F32, "up_g_fwd", tie=tie_ffn)
    sv['uv'] = matmul([(sv['h3'], w['w_vv'])], 'nn', F32, "up_v_fwd")
    sv['a'] = ffn_act_fwd(sv['ug'], sv['uv'], sm['ffn_conv_w'], sm['ffn_conv_b'], l)
    if l + 1 < DEPTH:
        x3, h1_next = matmul([(sv['a'], w['w_down'])], 'nn', F32, "down_fwd", add=x2, post=_norm_post,
                             fulls=[(sm['norm_mix'], l + 1)], outs=[F32, BF16], full_n=True)
    else:
        x3, h1_next = matmul([(sv['a'], w['w_down'])], 'nn', F32, "down_fwd_last", add=x2), None
    return x3, h1_next, sv


EARLY_GRADS = ('w_down', 'w_up', 'w_mo', 'w_mq', 'w_mk', 'w_mv', 'w_out')
LATE_GRADS = ('w_uq', 'w_ukv', 'w_in')


def layer_bwd(dx3, mem, cosm, sinm_neg, w, sm, l, sv, on_grads, tie=None):
    gain = lambda n: (sm[n], l)
    big, small = {}, {}
    proj = sv['proj']
    da = matmul([(dx3, w['w_down'])], 'nt', BF16, "down_bwd_a", tie=tie)
    big['w_down'] = matmul([(sv['a'], dx3)], 'tn', BF16, "down_bwd_w")
    dug, duv, dcwg, dcwv, dcbg, dcbv = ffn_act_bwd(sv['ug'], sv['uv'], sm['ffn_conv_w'], sm['ffn_conv_b'], l, da)
    small['ffn_conv_w'] = jnp.concatenate([dcwg, dcwv], axis=1)
    small['ffn_conv_b'] = jnp.concatenate([dcbg, dcbv], axis=1)
    gacc = [((1, D_MODEL), F32)]
    dx2, small['norm_ffn'] = matmul([(dug, w['w_g']), (duv, w['w_vv'])], 'nt', F32, "up_bwd_h", post=_norm_bwd_post,
                                    rows=[sv['x2'], dx3], fulls=[gain('norm_ffn')], accs=gacc, full_n=True, tm_cap=256)
    big['w_up'] = extract_up(matmul([(sv['h3'], dug)], 'tn', BF16, "up_g_bwd_w"),
                             matmul([(sv['h3'], duv)], 'tn', BF16, "up_v_bwd_w"))
    dom = matmul([(dx2, w['w_mo'])], 'nt', BF16, "mo_bwd_a")
    big['w_mo'] = matmul([(sv['om'], dx2)], 'tn', BF16, "mo_bwd_w")
    dmq, dmk, dmv = mem_bwd(sv['mq'], sv['mk'], sv['mv'], dom)
    dx1, small['norm_mem_q'] = matmul([(dmq, w['w_mq'])], 'nt', F32, "mq_bwd_a", post=_norm_bwd_post,
                                      rows=[sv['x1'], dx2], fulls=[gain('norm_mem_q')], accs=gacc, full_n=True)
    big['w_mq'] = matmul([(sv['hq'], dmq)], 'tn', BF16, "mq_bwd_w")
    dmn = matmul([(dmk, w['w_mk']), (dmv, w['w_mv'])], 'nt', BF16, "mkv_bwd_a")
    big['w_mk'] = matmul([(sv['mn'], dmk)], 'tn', BF16, "mk_bwd_w")
    big['w_mv'] = matmul([(sv['mn'], dmv)], 'tn', BF16, "mv_bwd_w")
    _, small['norm_mem_kv'] = rmsnorm_bwd(mem, gain('norm_mem_kv'), dmn, "norm_mem_kv_bwd", dx_dtype=BF16)
    dmix = matmul([(dx1, w['w_out'])], 'nt', BF16, "out_bwd_a")
    big['w_out'] = matmul([(sv['mix'], dx1)], 'tn', BF16, "out_bwd_w")
    early = {n: big.pop(n).reshape((N_DEV,) + BIG[n]) if n != 'w_up' else big.pop(n) for n in EARLY_GRADS}
    tie = on_grads(l, 'a', early)
    dy, dz, small['ssm_norm'] = gate_norm_bwd(sv['y'], proj, gain('ssm_norm'), dmix, tie=tie)
    dxbc_act, dsmall_ssd, small['ptile'] = ssd_bwd(sv['xbc'], proj, sm['ptile'], l, sv['prevs'], dy)
    dxbc, small['ssm_conv_w'], small['ssm_conv_b'] = ssm_conv_bwd(proj, sm['ssm_conv_w'], sm['ssm_conv_b'], l, dxbc_act)
    do, small['attn_out_norm'] = rmsnorm_bwd(sv['o'], gain('attn_out_norm'), dmix, "attn_out_norm_bwd", dh_colblock=1)
    dq, dk, dv = mla_bwd(sv['q'], sv['k'], sv['v'], sv['o'], sv['lse'], do, cosm, sinm_neg)
    dsmall = dsmall_bwd(dk, dsmall_ssd, cosm, sinm_neg)
    dcqn = matmul([(dq, w['w_uq'])], 'nt', BF16, "uq_bwd_a")
    big['w_uq'] = extract_uq(matmul([(sv['cqn'], dq)], 'tn', BF16, "uq_bwd_w"))
    dckvn = matmul([(dk, w['w_kn']), (dv, w['w_v'])], 'nt', BF16, "ukv_bwd_a")
    big['w_ukv'] = extract_ukv(matmul([(sv['ckvn'], dk)], 'tn', BF16, "kn_bwd_w"),
                               matmul([(sv['ckvn'], dv)], 'tn', BF16, "v_bwd_w"))
    dcq, small['q_norm'] = rmsnorm_bwd(proj, gain('q_norm'), dcqn, "q_norm_bwd", width=Q_LORA,
                                       colblock=OFF_CQ // Q_LORA, dx_dtype=BF16)
    dckv, small['kv_norm'] = rmsnorm_bwd(proj, gain('kv_norm'), dckvn, "kv_norm_bwd", width=KV_LORA,
                                         colblock=OFF_CKV // KV_LORA, dx_dtype=BF16)
    wp = w['w_proj']
    xbc_half = lambda c: Opnd(dxbc, c0=c, shape=(dxbc.shape[0], 1024))
    wwin = lambda off, width: Opnd(wp, c0=off // width, shape=(D_MODEL, width))
    dx0, small['norm_mix'] = matmul(
        [(dz, wwin(OFF_Z, 1024)), (xbc_half(0), wwin(OFF_XBC, 1024)), (xbc_half(1), wwin(OFF_XBC + 1024, 1024)),
         (dcq, wwin(OFF_CQ, Q_LORA)), (dsmall, wwin(OFF_SMALL, LANES)), (dckv, wwin(OFF_CKV, KV_LORA))],
        'nt', F32, "proj_bwd_a", post=_norm_bwd_post, rows=[sv['x0'], dx1], fulls=[gain('norm_mix')], accs=gacc,
        full_n=True, tm_cap=256)
    h1 = sv['h1']
    big['w_in'] = extract_proj(
        matmul([(h1, dz)], 'tn', BF16, "proj_z_bwd_w"), matmul([(h1, dxbc)], 'tn', BF16, "proj_xbc_bwd_w"),
        matmul([(h1, dcq)], 'tn', BF16, "proj_cq_bwd_w"), matmul([(h1, dsmall)], 'tn', BF16, "proj_small_bwd_w"),
        matmul([(h1, dckv)], 'tn', BF16, "proj_ckv_bwd_w"))
    return dx0, on_grads(l, 'b', big), small


def _small_row(small, final=None):
    pt = small['ptile']
    parts = []
    for n, wd in SMALL_SEGS:
        if n in ('dt_bias', 'a_log', 'd_skip'):
            parts.append(pt[('dt_bias', 'a_log', 'd_skip').index(n)][None, :])
        elif n in SMALL_SHARDED:
            parts.append(small[n].reshape(1, wd))
        elif n == 'final_norm':
            parts.append(final if final is not None else jnp.zeros((1, wd), F32))
        else:
            parts.append(small[n])
    return jnp.concatenate(parts, axis=1)


def _rope_tables(positions):
    inv_freq = 1.0 / (ROPE_THETA ** (jnp.arange(0, QK_ROPE, 2, dtype=F32) / QK_ROPE))
    ang = positions.astype(F32)[:, None] * inv_freq
    cos, sin = jnp.cos(ang), jnp.sin(ang)
    s = positions.shape[0]
    pad = jnp.zeros((s, LANES - ROPE_LANE0 - QK_ROPE), F32)
    cosm = jnp.concatenate([jnp.ones((s, ROPE_LANE0), F32), cos, cos, pad], axis=1)
    sinm = jnp.concatenate([jnp.zeros((s, ROPE_LANE0), F32), -sin, sin, pad], axis=1)
    return cosm, sinm


def _small_views(rep, conv_full):
    sm = {n: rep[n].reshape(DEPTH, 1, -1) for n in ('norm_mix', 'ssm_norm', 'attn_out_norm', 'norm_mem_q',
                                                    'norm_mem_kv', 'norm_ffn', 'q_norm', 'kv_norm', 'ssm_conv_b',
                                                    'ffn_conv_b')}
    sm.update(conv_full)
    rows = jnp.stack([rep['dt_bias'], rep['a_log'], rep['d_skip']], axis=1)
    sm['ptile'] = jnp.pad(rows, ((0, 0), (0, 8 - 3), (0, LANES - SSM_HEADS)))
    return sm


def local_step(x, mem, positions, target, sm, final_norm, weights_of, on_grads):
    cosm, sinm = _rope_tables(positions)
    sinm_neg = -sinm
    saved, ws = [], []
    h, h1 = x, None
    for l in range(DEPTH):
        w, tie = weights_of(l, h)
        ws.append(w)
        h, h1, sv = layer_fwd(h, h1, mem, cosm, sinm, w, sm, l, tie=tie)
        saved.append(sv)
    dx, dfinal, lossv = loss_head(h, (final_norm.reshape(1, 1, -1), 0), target)
    rows = [None] * DEPTH
    tie = None
    for l in reversed(range(DEPTH)):
        dx, tie, small = layer_bwd(dx, mem, cosm, sinm_neg, ws[l], sm, l, saved[l], on_grads, tie=tie)
        rows[l] = _small_row(small, dfinal if l == 0 else None)
    return lossv[0, 0], dx, jnp.concatenate(rows, axis=0)


def kernel(x, mem, positions, norm_mix, w_in, ssm_conv_w, ssm_conv_b, dt_bias, a_log, d_skip, ssm_norm, q_norm, w_uq, kv_norm, w_ukv, attn_out_norm, w_out, norm_mem_q, norm_mem_kv, w_mq, w_mk, w_mv, w_mo, norm_ffn, w_up, ffn_conv_w, ffn_conv_b, w_down, final_norm, loss_target, m_norm_mix, m_w_in, m_ssm_conv_w, m_ssm_conv_b, m_dt_bias, m_a_log, m_d_skip, m_ssm_norm, m_q_norm, m_w_uq, m_kv_norm, m_w_ukv, m_attn_out_norm, m_w_out, m_norm_mem_q, m_norm_mem_kv, m_w_mq, m_w_mk, m_w_mv, m_w_mo, m_norm_ffn, m_w_up, m_ffn_conv_w, m_ffn_conv_b, m_w_down, m_final_norm, v_norm_mix, v_w_in, v_ssm_conv_w, v_ssm_conv_b, v_dt_bias, v_a_log, v_d_skip, v_ssm_norm, v_q_norm, v_w_uq, v_kv_norm, v_w_ukv, v_attn_out_norm, v_w_out, v_norm_mem_q, v_norm_mem_kv, v_w_mq, v_w_mk, v_w_mv, v_w_mo, v_norm_ffn, v_w_up, v_ffn_conv_w, v_ffn_conv_b, v_w_down, v_final_norm):
    args = locals()
    wts = {n: args[n] for n in WEIGHT_NAMES}
    ms = {n: args['m_' + n] for n in WEIGHT_NAMES}
    vs = {n: args['v_' + n] for n in WEIGHT_NAMES}

    st = dict(srcs={n: wts[n].astype(BF16) for n in BIG_NAMES}, exchanges=[],
              lands={n: lax.empty((DEPTH, N_DEV) + BIG[n], BF16) for n in BIG_NAMES})
    first = LATE_GRADS + ('w_out',)
    rest = tuple(n for n in BIG_NAMES if n not in first)
    got = all_gather_blocks([st['srcs'][n] for n in first] + [wts[n] for n in SMALL_SHARDED],
                            first_only=tuple(range(len(first))))
    conv_full = {}
    for n, g in zip(SMALL_SHARDED, got[len(first):]):
        taps, per, full = SMALL_SHARDED[n]
        conv_full[n] = jnp.moveaxis(g, 1, 2).reshape(DEPTH, taps, full)
    sm = _small_views(wts, conv_full)

    def start(names, l, tag, after=None):
        send_sems, recv_sems, thru, lands, tie = gather_start([st['srcs'][n] for n in names], l, tag, after)
        st['srcs'].update(zip(names, thru))
        return (names, l, tag, send_sems, recv_sems, lands), tie

    def pass_on(handle, after):
        names, l, tag, send_sems, recv_sems, lands = handle
        thru, lands = gather_wait(l, tag, send_sems, recv_sems, [st['srcs'][n] for n in names], lands, after)
        st['srcs'].update(zip(names, thru))
        send_sems, recv_sems, lands, tie = gather_pass_start(lands, l, tag)
        return (names, l, tag, send_sems, recv_sems, lands), tie

    def finish(handle, after):
        names, l, tag, send_sems, recv_sems, lands = handle
        return _layer_weights(dict(zip(names, gather_pass_wait(l, tag, send_sems, recv_sems, lands, after))))

    later, _ = start(rest, 0, "r", after=got[0])

    def weights_of(l, h):
        if l == 0:
            w = _layer_weights(dict(zip(first, got[:len(first)])))
            w['later'] = lambda after: finish(pass_on(later, after)[0], after)
        else:
            w = finish(st['next'], h)
        tie = None
        if l + 1 < DEPTH:
            st['next'], tie = start(BIG_NAMES, l + 1, "")

            def prefetch(after):
                st['next'], tie2 = pass_on(st['next'], after)
                return tie2

            w['prefetch'] = prefetch
        return w, tie

    def on_grads(l, tag, big, after=None):
        if (l, tag) == (0, 'b') and after is None:
            st['held'] = big
            return None
        names = list(big)
        send_sems, recv_sems, thru, lands, tie = grad_exchange_start(
            [big[n] for n in names], [st['lands'][n] for n in names], l, tag, after)
        st['lands'].update(zip(names, lands))
        st['exchanges'].append((l, tag, names, send_sems, recv_sems, thru))
        return tie

    loss_local, dx, small_rows = local_step(x[0], mem[0], positions[0], loss_target[0], sm, final_norm, weights_of,
                                            on_grads)
    outs = [{}, {}, {}, {}]

    sg_send, sg_recv, sg_src, sg_land, tok = small_gather_start(small_rows)
    tie = on_grads(0, 'b', st['held'], after=tok)

    def wait(exchange, after):
        l, tag, names, send_sems, recv_sems, thru = exchange
        _, lands = grad_exchange_wait(l, tag, send_sems, recv_sems, thru, [st['lands'][n] for n in names], after)
        st['lands'].update(zip(names, lands))

    def update(names, tie):
        for n in names:
            res_n = adamw_big(st['lands'][n], wts[n], ms[n], vs[n], "adamw_" + n, tie)
            tie = res_n[0]
            for k in range(4):
                outs[k][n] = res_n[k]
        return tie

    for exchange in st['exchanges'][:-1]:
        wait(exchange, tie)
    tie = update(EARLY_GRADS, tie)

    small_all = small_gather_wait(sg_send, sg_recv, sg_src, sg_land, tie)
    view = lambda d: {n: d[n].reshape(SMALL_VIEW[n]) for n in SMALL_NAMES}
    res = adamw_small(small_all, view(wts), view(ms), view(vs), tie)
    for k in range(4):
        for n in SMALL_NAMES:
            outs[k][n] = res[k][n].reshape(wts[n].shape)

    wait(st['exchanges'][-1], res[0]['final_norm'])
    update(LATE_GRADS, res[0]['final_norm'])

    loss = lax.psum(loss_local, ("x", "y", "c"))
    return (loss, dx[None], *[outs[0][n] for n in WEIGHT_NAMES], *[outs[1][n] for n in WEIGHT_NAMES],
            *[outs[2][n] for n in WEIGHT_NAMES], *[outs[3][n] for n in WEIGHT_NAMES])
```

```python
import functools
import math
from typing import Any, NamedTuple, Optional

import jax
import jax.numpy as jnp
from jax import lax
from jax.experimental import pallas as pl
from jax.experimental.pallas import tpu as pltpu

F32 = jnp.float32
BF16 = jnp.bfloat16

D_MODEL = 1024
DEPTH = 4
MEM_LEN = 256
EPS = 1e-6
SSM_HEADS = 16
SSM_HEAD_DIM = 64
D_SSM = 1024
SSM_GROUPS = 4
SSM_STATE = 128
SSM_CONV = 4
SSM_CHUNK = 128
CONV_CH = 2048
MLA_HEADS = 16
QK_NOPE = 64
QK_ROPE = 32
V_DIM = 64
Q_LORA = 384
KV_LORA = 256
ROPE_THETA = 10000.0
MEM_HEADS = 4
MEM_HEAD_DIM = 256
D_FF = 2816
FFN_CONV = 3
D_IN = 3760
D_MIX = 2048
ADAM_LR = 0.001
ADAM_B1 = 0.9
ADAM_B2 = 0.999
ADAM_EPS = 1e-08
ADAM_WD = 0.01
ADAM_STEP = 10

N_DEV = 8
N_CHIP = 4
LANES = 128
HEAD_PAD = 128
PROJ_W = 3840
OFF_Z, OFF_XBC, OFF_CQ, OFF_SMALL, OFF_CKV = 0, 1024, 3072, 3456, 3584
ROPE_LANE0 = 64
VMEM_LIMIT = 56 * 1024 * 1024
MM_BLOCK_BYTES = 4 * 1024 * 1024
WEIGHT_NAMES = ['norm_mix', 'w_in', 'ssm_conv_w', 'ssm_conv_b', 'dt_bias', 'a_log', 'd_skip', 'ssm_norm', 'q_norm',
                'w_uq', 'kv_norm', 'w_ukv', 'attn_out_norm', 'w_out', 'norm_mem_q', 'norm_mem_kv', 'w_mq', 'w_mk',
                'w_mv', 'w_mo', 'norm_ffn', 'w_up', 'ffn_conv_w', 'ffn_conv_b', 'w_down', 'final_norm']
BIG = {'w_in': (1024, 470), 'w_uq': (384, 192), 'w_ukv': (256, 256), 'w_up': (1024, 704), 'w_out': (256, 1024),
       'w_mq': (128, 1024), 'w_mk': (128, 1024), 'w_mv': (128, 1024), 'w_mo': (128, 1024), 'w_down': (352, 1024)}
BIG_NAMES = list(BIG)
PROJ_SEGS = [(0, 1024, OFF_Z), (1024, 3072, OFF_XBC), (3072, 3088, OFF_SMALL), (3088, 3472, OFF_CQ),
             (3472, 3728, OFF_CKV), (3728, 3760, OFF_SMALL + ROPE_LANE0)]
SMALL_SEGS = [('norm_mix', 1024), ('ssm_norm', 1024), ('attn_out_norm', 1024), ('norm_mem_q', 1024),
              ('norm_mem_kv', 1024), ('norm_ffn', 1024), ('q_norm', 384), ('kv_norm', 256), ('ssm_conv_b', 2048),
              ('ffn_conv_b', 5632), ('dt_bias', 128), ('a_log', 128), ('d_skip', 128),
              ('ssm_conv_w', SSM_CONV * CONV_CH), ('ffn_conv_w', FFN_CONV * 2 * D_FF), ('final_norm', 1024)]
SMALL_OFF = {}
_o = 0
for _n, _w in SMALL_SEGS:
    SMALL_OFF[_n] = _o
    _o += _w
SMALL_W = _o


def _params(**kw):
    return pltpu.CompilerParams(vmem_limit_bytes=VMEM_LIMIT, **kw)


def _pick(n, cap):
    if n <= cap:
        return n
    best = None
    for t in range(LANES, cap + 1, LANES):
        if n % t == 0:
            best = t
    assert best is not None, (n, cap)
    return best


def _row_tile(a, cap=256):
    if a <= cap:
        return a
    best = None
    for t in range(16, cap + 1, 16):
        if a % t == 0:
            best = t
    assert best is not None, (a, cap)
    return best


class Opnd(NamedTuple):
    arr: Any
    lead: Optional[int] = None
    r0: int = 0
    c0: int = 0
    shape: Optional[tuple] = None


def _opnd(x):
    return x if isinstance(x, Opnd) else Opnd(x)


def _lshape(o):
    return tuple(o.shape) if o.shape is not None else tuple(o.arr.shape[-2:])


def _spec(o, br, bc, bi, bj):
    rr, cc = _lshape(o)
    assert rr % br == 0 and cc % bc == 0, (rr, cc, br, bc)
    ro, co = o.r0 * (rr // br), o.c0 * (cc // bc)
    if o.lead is None:
        return pl.BlockSpec((br, bc), lambda i, j: (ro + bi(i, j), co + bj(i, j)))
    return pl.BlockSpec((None, br, bc), lambda i, j: (o.lead, ro + bi(i, j), co + bj(i, j)))


_DIMS = {'nn': (((1,), (0,)), ((), ())), 'nt': (((1,), (1,)), ((), ())), 'tn': (((0,), (0,)), ((), ()))}
_ROW = lambda i, j: i
_COL = lambda i, j: j
_ZERO = lambda i, j: 0


def matmul(pairs, mode, out_dtype, name, add=None, tie=None, post=None, rows=(), fulls=(), outs=None, full_n=False,
           accs=(), tm_cap=None):
    pairs = [(_opnd(a), _opnd(b)) for a, b in pairs]
    a0, b0 = pairs[0]
    if mode == 'nn':
        m, n = _lshape(a0)[0], _lshape(b0)[1]
    elif mode == 'nt':
        m, n = _lshape(a0)[0], _lshape(b0)[0]
    else:
        m, n = _lshape(a0)[1], _lshape(b0)[1]
    isz = lambda o: jnp.dtype(o.arr.dtype).itemsize
    osz = jnp.dtype(out_dtype).itemsize
    cap = lambda budget, per: max(LANES, budget // per // LANES * LANES)
    if mode == 'tn':
        ktok = _lshape(a0)[0]
        tm = _pick(m, cap(3 * MM_BLOCK_BYTES // 2, ktok * isz(a0)))
        tn = _pick(n, cap(3 * MM_BLOCK_BYTES // 2, ktok * isz(b0)))
    else:
        tm = _pick(m, min(2048, cap(2 * MM_BLOCK_BYTES, sum(_lshape(a)[1] * isz(a) for a, _ in pairs))))
        tn = _pick(n, min(cap(3 * MM_BLOCK_BYTES // 2, sum(_lshape(a)[1] * isz(b) for a, b in pairs)),
                          cap(MM_BLOCK_BYTES, tm * osz), n // 2 if n >= 1024 else n))
        if full_n:
            tm, tn = _pick(m, min(tm, tm_cap or tm, cap(MM_BLOCK_BYTES // 2, n * osz))), n
    npairs = len(pairs)
    outs = list(outs) if outs is not None else [out_dtype]
    nadd = 1 if add is not None else 0
    nrows, nfulls = len(rows), len(fulls)

    def body(*refs):
        o_refs = refs[len(refs) - len(outs) - len(accs):]
        if accs:
            @pl.when(jnp.logical_and(pl.program_id(0) == 0, pl.program_id(1) == 0))
            def _():
                for r in o_refs[len(outs):]:
                    r[...] = jnp.zeros_like(r)

        acc = None
        for p in range(npairs):
            a = refs[2 * p][...].astype(BF16)
            b = refs[2 * p + 1][...].astype(BF16)
            d = lax.dot_general(a, b, _DIMS[mode], preferred_element_type=F32)
            acc = d if acc is None else acc + d
        if add is not None:
            acc = acc + refs[2 * npairs][...].astype(F32)
        if post is None:
            o_refs[0][...] = acc.astype(out_dtype)
        else:
            x0 = 2 * npairs + nadd
            post(acc, [r[...] for r in refs[x0:x0 + nrows]], [r[...] for r in refs[x0 + nrows:x0 + nrows + nfulls]], o_refs)

    rows = [r if isinstance(r, tuple) else (r, r.shape[1], 0) for r in rows]
    tie_specs = [pl.BlockSpec((tm, wd), lambda i, j, cb=cb: (i, cb)) for _, wd, cb in rows]
    tie_specs += [pl.BlockSpec((None,) + f.shape[1:], lambda i, j, ld=ld, nd=f.ndim - 1: (ld,) + (0,) * nd) for f, ld in fulls]
    tie_args = [r for r, _, _ in rows] + [f for f, _ in fulls]
    if tie is not None:
        tie_specs.append(pl.BlockSpec(memory_space=pl.ANY))
        tie_args.append(tie)

    in_specs, args = [], []
    for a, b in pairs:
        if mode == 'nn':
            k = _lshape(a)[1]
            in_specs += [_spec(a, tm, k, _ROW, _ZERO), _spec(b, k, tn, _ZERO, _COL)]
        elif mode == 'nt':
            k = _lshape(a)[1]
            in_specs += [_spec(a, tm, k, _ROW, _ZERO), _spec(b, tn, k, _COL, _ZERO)]
        else:
            k = _lshape(a)[0]
            in_specs += [_spec(a, k, tm, _ZERO, _ROW), _spec(b, k, tn, _ZERO, _COL)]
        args += [a.arr, b.arr]
    if add is not None:
        in_specs.append(pl.BlockSpec((tm, tn), lambda i, j: (i, j)))
        args.append(add)
    res = pl.pallas_call(
        body, name=name, grid=(m // tm, n // tn), in_specs=in_specs + tie_specs,
        out_specs=[pl.BlockSpec((tm, o[0]), lambda i, j: (i, 0)) if isinstance(o, tuple) else
                   pl.BlockSpec((tm, tn), lambda i, j: (i, j)) for o in outs] +
                  [pl.BlockSpec(shp, lambda i, j, nd=len(shp): (0,) * nd) for shp, _ in accs],
        out_shape=[jax.ShapeDtypeStruct((m, o[0]), o[1]) if isinstance(o, tuple) else jax.ShapeDtypeStruct((m, n), o)
                   for o in outs] + [jax.ShapeDtypeStruct(shp, dt) for shp, dt in accs],
        compiler_params=_params(dimension_semantics=("arbitrary", "arbitrary")),
    )(*args, *tie_args)
    return res[0] if len(outs) + len(accs) == 1 else res


def rowwise(fn, rows, fulls, outs, accs, name, tm=256, into=None, tie=None):
    s = rows[0][0].shape[0]
    nrow, nfull, nout, nacc = len(rows), len(fulls), len(outs), len(accs)
    nin = nrow + nfull

    def body(*refs):
        ins = [r[...] for r in refs[:nin]]
        res = fn(*ins)
        if not isinstance(res, (tuple, list)):
            res = (res,)
        orefs = refs[nin + (1 if into is not None else 0) + (1 if tie is not None else 0):]
        for k in range(nout):
            orefs[k][...] = res[k].astype(orefs[k].dtype)
        if nacc:
            @pl.when(pl.program_id(0) == 0)
            def _():
                for k in range(nacc):
                    orefs[nout + k][...] = jnp.zeros_like(orefs[nout + k])

            for k in range(nacc):
                orefs[nout + k][...] += res[nout + k].astype(orefs[nout + k].dtype)

    in_specs = [pl.BlockSpec((tm, w), lambda i, cb=cb: (i, cb)) for _, w, cb in rows]
    in_specs += [pl.BlockSpec((None,) + f.shape[1:], lambda i, ld=ld, nd=f.ndim - 1: (ld,) + (0,) * nd) for f, ld in fulls]
    args = [r[0] for r in rows] + [f for f, _ in fulls]
    aliases = {}
    if into is not None:
        in_specs.append(pl.BlockSpec(memory_space=pl.ANY))
        args.append(into[0])
        aliases = {nin: into[1]}
    if tie is not None:
        in_specs.append(pl.BlockSpec(memory_space=pl.ANY))
        args.append(tie)
    out_specs, out_shape = [], []
    for o in outs:
        w, dt = o[0], o[1]
        total, cb = (o[2], o[3]) if len(o) == 4 else (w, 0)
        out_specs.append(pl.BlockSpec((tm, w), lambda i, cb=cb: (i, cb)))
        out_shape.append(jax.ShapeDtypeStruct((s, total), dt))
    for shp, dt in accs:
        out_specs.append(pl.BlockSpec(shp, lambda i, nd=len(shp): (0,) * nd))
        out_shape.append(jax.ShapeDtypeStruct(shp, dt))
    return pl.pallas_call(
        body, name=name, grid=(s // tm,), in_specs=in_specs, out_specs=out_specs, out_shape=out_shape,
        input_output_aliases=aliases, compiler_params=_params(dimension_semantics=("arbitrary",)),
    )(*args)


def _rms(x, g):
    xf = x.astype(F32)
    var = jnp.mean(xf * xf, axis=-1, keepdims=True)
    return xf * lax.rsqrt(var + EPS) * g


def rmsnorm_fwd(x, g, name, width=None, colblock=0, out=None, into=None, tie=None):
    w = width or x.shape[1]
    return rowwise(lambda xt, gt: _rms(xt, gt), [(x, w, colblock)], [g], [out or (w, BF16)], [], name, into=into,
                   tie=tie)[0]


def rmsnorm_bwd(x, g, dh, name, resid=None, width=None, colblock=0, dh_colblock=0, dx_dtype=F32):
    w = width or x.shape[1]

    def fn(xt, dht, *rest):
        gt = rest[-1]
        _, vjp = jax.vjp(_rms, xt.astype(F32), gt)
        dx, dg = vjp(dht.astype(F32))
        if resid is not None:
            dx = dx + rest[0]
        return dx, dg

    rows = [(x, w, colblock), (dh, w, dh_colblock)] + ([(resid, w, 0)] if resid is not None else [])
    return rowwise(fn, rows, [g], [(w, dx_dtype)], [((1, w), F32)], name)


CONV_R = 64
HALO = 8


def _ext_rows(ref, i, nchunk, above, below):
    r0 = pl.multiple_of(i * CONV_R, CONV_R)
    s = ref.shape[0]
    parts = []
    if above:
        top = ref[pl.ds(pl.multiple_of(jnp.maximum(r0 - HALO, 0), HALO), HALO), :].astype(F32)
        parts.append(jnp.where(i > 0, top, 0.0))
    parts.append(ref[pl.ds(r0, CONV_R), :].astype(F32))
    if below:
        tile = 2 * HALO if ref.dtype == BF16 else HALO
        bot = ref[pl.ds(pl.multiple_of(jnp.minimum(r0 + CONV_R, s - tile), tile), tile), :].astype(F32)[0:HALO]
        parts.append(jnp.where(i < nchunk - 1, bot, 0.0))
    return jnp.concatenate(parts, axis=0)


def _conv_ext(ext, w_ref, b_ref, kw):
    y = ext[HALO:] * w_ref[kw - 1:kw, :] + b_ref[...]
    for k in range(1, kw):
        y = y + pltpu.roll(ext, k, 0)[HALO:] * w_ref[kw - 1 - k:kw - k, :]
    return y


def _conv_t_ext(d, w_ref, kw):
    n = d.shape[0]
    y = d[:n - HALO] * w_ref[kw - 1:kw, :]
    for k in range(1, kw):
        y = y + pltpu.roll(d, n - k, 0)[:n - HALO] * w_ref[kw - 1 - k:kw - k, :]
    return y


def _conv_wgrad(dp, ext, kw):
    out = [jnp.sum(dp, axis=0, keepdims=True), jnp.sum(dp * ext[HALO:HALO + CONV_R], axis=0, keepdims=True)]
    for k in range(1, kw):
        out.append(jnp.sum(dp * pltpu.roll(ext, k, 0)[HALO:HALO + CONV_R], axis=0, keepdims=True))
    return out


def _store_wgrad(res, dw_ref, db_ref, kw):
    db_ref[...] = res[0]
    for k in range(kw):
        dw_ref[kw - 1 - k:kw - k, :] = res[1 + k]


def _silu(x):
    return x * jax.nn.sigmoid(x)


def _dsilu(x):
    s = jax.nn.sigmoid(x)
    return s * (1.0 + x * (1.0 - s))


SSM_TC = 256


def ssm_conv_fwd(proj, cw, cb, l):
    s = proj.shape[0]
    off = OFF_XBC // SSM_TC

    def body(u_ref, w_ref, b_ref, o_ref):
        nchunk = s // CONV_R

        def step(i, carry):
            ext = _ext_rows(u_ref, i, nchunk, True, False)
            o_ref[pl.ds(pl.multiple_of(i * CONV_R, CONV_R), CONV_R), :] = _silu(_conv_ext(ext, w_ref, b_ref, SSM_CONV))
            return carry

        lax.fori_loop(0, nchunk, step, 0)

    return pl.pallas_call(
        body, name="ssm_conv_fwd", grid=(CONV_CH // SSM_TC,),
        in_specs=[pl.BlockSpec((s, SSM_TC), lambda j: (0, off + j)),
                  pl.BlockSpec((None, SSM_CONV, SSM_TC), lambda j: (l, 0, j)),
                  pl.BlockSpec((None, 1, SSM_TC), lambda j: (l, 0, j))],
        out_specs=pl.BlockSpec((s, SSM_TC), lambda j: (0, j)),
        out_shape=jax.ShapeDtypeStruct((s, CONV_CH), F32),
        compiler_params=_params(dimension_semantics=("arbitrary",)),
    )(proj, cw, cb)


def ssm_conv_bwd(proj, cw, cb, l, dact):
    s = proj.shape[0]
    off = OFF_XBC // SSM_TC

    def body(u_ref, w_ref, b_ref, d_ref, du_ref, dw_ref, db_ref):
        nchunk = s // CONV_R

        def step(i, carry):
            ext = _ext_rows(u_ref, i, nchunk, True, True)
            dpre = _ext_rows(d_ref, i, nchunk, False, True) * _dsilu(_conv_ext(ext, w_ref, b_ref, SSM_CONV))
            du_ref[pl.ds(pl.multiple_of(i * CONV_R, CONV_R), CONV_R), :] = _conv_t_ext(dpre, w_ref, SSM_CONV).astype(du_ref.dtype)
            return tuple(c + g for c, g in zip(carry, _conv_wgrad(dpre[:CONV_R], ext, SSM_CONV)))

        zero = jnp.zeros((1, SSM_TC), F32)
        _store_wgrad(lax.fori_loop(0, nchunk, step, (zero,) * (SSM_CONV + 1)), dw_ref, db_ref, SSM_CONV)

    return pl.pallas_call(
        body, name="ssm_conv_bwd", grid=(CONV_CH // SSM_TC,),
        in_specs=[pl.BlockSpec((s, SSM_TC), lambda j: (0, off + j)),
                  pl.BlockSpec((None, SSM_CONV, SSM_TC), lambda j: (l, 0, j)),
                  pl.BlockSpec((None, 1, SSM_TC), lambda j: (l, 0, j)), pl.BlockSpec((s, SSM_TC), lambda j: (0, j))],
        out_specs=[pl.BlockSpec((s, SSM_TC), lambda j: (0, j)), pl.BlockSpec((SSM_CONV, SSM_TC), lambda j: (0, j)),
                   pl.BlockSpec((1, SSM_TC), lambda j: (0, j))],
        out_shape=[jax.ShapeDtypeStruct((s, CONV_CH), BF16), jax.ShapeDtypeStruct((SSM_CONV, CONV_CH), F32),
                   jax.ShapeDtypeStruct((1, CONV_CH), F32)],
        compiler_params=_params(dimension_semantics=("arbitrary",)),
    )(proj, cw, cb, dact)


FFN_TC = 256
FFN_NT = D_FF // FFN_TC


def _ffn_specs(s, l):
    blk = pl.BlockSpec((s, FFN_TC), lambda j: (0, j))
    wg = pl.BlockSpec((None, FFN_CONV, FFN_TC), lambda j: (l, 0, j))
    wv = pl.BlockSpec((None, FFN_CONV, FFN_TC), lambda j: (l, 0, FFN_NT + j))
    bg = pl.BlockSpec((None, 1, FFN_TC), lambda j: (l, 0, j))
    bv = pl.BlockSpec((None, 1, FFN_TC), lambda j: (l, 0, FFN_NT + j))
    return blk, wg, wv, bg, bv


def ffn_act_fwd(ug, uv, cw, cb, l):
    s = ug.shape[0]

    def body(g_ref, v_ref, wg_ref, wv_ref, bg_ref, bv_ref, o_ref):
        nchunk = s // CONV_R

        def step(i, carry):
            cg = _conv_ext(_ext_rows(g_ref, i, nchunk, True, False), wg_ref, bg_ref, FFN_CONV)
            cv = _conv_ext(_ext_rows(v_ref, i, nchunk, True, False), wv_ref, bv_ref, FFN_CONV)
            o_ref[pl.ds(pl.multiple_of(i * CONV_R, CONV_R), CONV_R), :] = (_silu(cg) * cv).astype(o_ref.dtype)
            return carry

        lax.fori_loop(0, nchunk, step, 0)

    blk, wg, wv, bg, bv = _ffn_specs(s, l)
    return pl.pallas_call(
        body, name="ffn_act_fwd", grid=(FFN_NT,), in_specs=[blk, blk, wg, wv, bg, bv],
        out_specs=blk, out_shape=jax.ShapeDtypeStruct((s, D_FF), BF16),
        compiler_params=_params(dimension_semantics=("arbitrary",)),
    )(ug, uv, cw, cw, cb, cb)


def ffn_act_bwd(ug, uv, cw, cb, l, da):
    s = ug.shape[0]

    def body(g_ref, v_ref, wg_ref, wv_ref, bg_ref, bv_ref, da_ref, dg_ref, dv_ref, dwg_ref, dwv_ref, dbg_ref, dbv_ref):
        nchunk = s // CONV_R

        def step(i, carry):
            rows = pl.ds(pl.multiple_of(i * CONV_R, CONV_R), CONV_R)
            eg = _ext_rows(g_ref, i, nchunk, True, True)
            ev = _ext_rows(v_ref, i, nchunk, True, True)
            cg = _conv_ext(eg, wg_ref, bg_ref, FFN_CONV)
            cv = _conv_ext(ev, wv_ref, bv_ref, FFN_CONV)
            da_t = _ext_rows(da_ref, i, nchunk, False, True)
            sg = jax.nn.sigmoid(cg)
            dcg = da_t * cv * (sg * (1.0 + cg * (1.0 - sg)))
            dcv = da_t * (cg * sg)
            dg_ref[rows, :] = _conv_t_ext(dcg, wg_ref, FFN_CONV).astype(dg_ref.dtype)
            dv_ref[rows, :] = _conv_t_ext(dcv, wv_ref, FFN_CONV).astype(dv_ref.dtype)
            grads = _conv_wgrad(dcg[:CONV_R], eg, FFN_CONV) + _conv_wgrad(dcv[:CONV_R], ev, FFN_CONV)
            return tuple(c + g for c, g in zip(carry, grads))

        zero = jnp.zeros((1, FFN_TC), F32)
        res = lax.fori_loop(0, nchunk, step, (zero,) * (2 * FFN_CONV + 2))
        _store_wgrad(res[:FFN_CONV + 1], dwg_ref, dbg_ref, FFN_CONV)
        _store_wgrad(res[FFN_CONV + 1:], dwv_ref, dbv_ref, FFN_CONV)

    blk, wg, wv, bg, bv = _ffn_specs(s, l)
    wblk = pl.BlockSpec((FFN_CONV, FFN_TC), lambda j: (0, j))
    bblk = pl.BlockSpec((1, FFN_TC), lambda j: (0, j))
    return pl.pallas_call(
        body, name="ffn_act_bwd", grid=(FFN_NT,), in_specs=[blk, blk, wg, wv, bg, bv, blk],
        out_specs=[blk, blk, wblk, wblk, bblk, bblk],
        out_shape=[jax.ShapeDtypeStruct((s, D_FF), BF16), jax.ShapeDtypeStruct((s, D_FF), BF16),
                   jax.ShapeDtypeStruct((FFN_CONV, D_FF), F32), jax.ShapeDtypeStruct((FFN_CONV, D_FF), F32),
                   jax.ShapeDtypeStruct((1, D_FF), F32), jax.ShapeDtypeStruct((1, D_FF), F32)],
        compiler_params=_params(dimension_semantics=("arbitrary",)),
    )(ug, uv, cw, cw, cb, cb, da)


def _dot(a, b, mode):
    return lax.dot_general(a.astype(BF16), b.astype(BF16), _DIMS[mode], preferred_element_type=F32)


@jax.custom_vjp
def mm_nn(a, b):
    return _dot(a, b, 'nn')


@jax.custom_vjp
def mm_nt(a, b):
    return _dot(a, b, 'nt')


@jax.custom_vjp
def mm_tn(a, b):
    return _dot(a, b, 'tn')


mm_nn.defvjp(lambda a, b: (_dot(a, b, 'nn'), (a, b)), lambda r, g: (_dot(g, r[1], 'nt'), _dot(r[0], g, 'tn')))
mm_nt.defvjp(lambda a, b: (_dot(a, b, 'nt'), (a, b)), lambda r, g: (_dot(g, r[1], 'nn'), _dot(g, r[0], 'tn')))
mm_tn.defvjp(lambda a, b: (_dot(a, b, 'tn'), (a, b)), lambda r, g: (_dot(r[1], g, 'nt'), _dot(r[0], g, 'nn')))


def _tri(n, lower):
    r = lax.broadcasted_iota(jnp.int32, (n, n), 0)
    c = lax.broadcasted_iota(jnp.int32, (n, n), 1)
    return jnp.where((r >= c) if lower else (r <= c), 1.0, 0.0).astype(F32)


def _tri_dot(a, lower):
    return jnp.dot(_tri(a.shape[0], lower), a, precision=lax.Precision.HIGHEST, preferred_element_type=F32)


@jax.custom_vjp
def _cumsum_rows(a):
    return _tri_dot(a, True)


_cumsum_rows.defvjp(lambda a: (_tri_dot(a, True), None), lambda _, g: (_tri_dot(g, False),))


def _softplus(x):
    return jnp.maximum(x, 0.0) + jnp.log(1.0 + jnp.exp(-jnp.abs(x)))


def _ssd_chunk(xs, bs, cs, small, dtb, alog, dsk, prev):
    ln = small.shape[0]
    lane = lax.broadcasted_iota(jnp.int32, (ln, LANES), 1)
    lane1 = lax.broadcasted_iota(jnp.int32, (1, LANES), 1)
    sub = lax.broadcasted_iota(jnp.int32, (LANES, ln), 0)
    rowi = lax.broadcasted_iota(jnp.int32, (ln, LANES), 0)
    tril = lax.broadcasted_iota(jnp.int32, (ln, ln), 0) >= lax.broadcasted_iota(jnp.int32, (ln, ln), 1)
    first = lane < SSM_HEAD_DIM
    first1 = lane1 < SSM_HEAD_DIM

    dt = _softplus(small + dtb)
    acs = _cumsum_rows(dt * (-jnp.exp(alog)))
    acs_t = acs.T
    last = jnp.sum(jnp.where(rowi == ln - 1, acs, 0.0), axis=0, keepdims=True)

    def col(a, h):
        return jnp.sum(jnp.where(lane == h, a, 0.0), axis=1, keepdims=True)

    def one(a, h):
        return jnp.sum(jnp.where(lane1 == h, a, 0.0), axis=1, keepdims=True)

    def rowv(at, h):
        return jnp.sum(jnp.where(sub == h, at, 0.0), axis=0, keepdims=True)

    cb = [mm_nt(cs[g], bs[g]) for g in range(SSM_GROUPS)]
    ys, news = [], []
    for j in range(SSM_HEADS // 2):
        g = j // 2
        h0, h1 = 2 * j, 2 * j + 1
        xd = xs[j] * jnp.where(first, col(dt, h0), col(dt, h1))
        yd, st, ea, cd = None, None, [], []
        for h, xdh in ((h0, jnp.where(first, xd, 0.0)), (h1, jnp.where(first, 0.0, xd))):
            ac = col(acs, h)
            la = one(last, h)
            lmat = jnp.exp(jnp.where(tril, ac - rowv(acs_t, h), -jnp.inf))
            yh = mm_nn(cb[g] * lmat, xdh)
            sh = mm_tn(bs[g] * jnp.exp(la - ac), xdh)
            yd = yh if yd is None else yd + yh
            st = sh if st is None else st + sh
            ea.append(jnp.exp(ac))
            cd.append(jnp.exp(la))
        yoff = mm_nn(cs[g], prev[j]) * jnp.where(first, ea[0], ea[1])
        ys.append(yd + yoff + xs[j] * jnp.where(first1, one(dsk, h0), one(dsk, h1)))
        news.append(prev[j] * jnp.where(first1, cd[0], cd[1]) + st)
    return ys, news


N_PAIR = SSM_HEADS // 2


def ssd_fwd(xbc, proj, ptile, l):
    s = xbc.shape[0]
    nch = s // SSM_CHUNK

    def body(xbc_ref, small_ref, p_ref, y_ref, prev_ref, state_ref):
        @pl.when(pl.program_id(0) == 0)
        def _():
            state_ref[...] = jnp.zeros_like(state_ref)

        xs = [xbc_ref[:, LANES * j:LANES * (j + 1)] for j in range(N_PAIR)]
        bs = [xbc_ref[:, D_SSM + LANES * g:D_SSM + LANES * (g + 1)] for g in range(SSM_GROUPS)]
        cs = [xbc_ref[:, D_SSM + 512 + LANES * g:D_SSM + 512 + LANES * (g + 1)] for g in range(SSM_GROUPS)]
        prev = [state_ref[j] for j in range(N_PAIR)]
        ys, news = _ssd_chunk(xs, bs, cs, small_ref[...], p_ref[0:1, :], p_ref[1:2, :], p_ref[2:3, :], prev)
        for j in range(N_PAIR):
            y_ref[:, LANES * j:LANES * (j + 1)] = ys[j]
            prev_ref[0, j] = prev[j]
            state_ref[j] = news[j]

    return pl.pallas_call(
        body, name="ssd_fwd", grid=(nch,),
        in_specs=[pl.BlockSpec((SSM_CHUNK, CONV_CH), lambda c: (c, 0)),
                  pl.BlockSpec((SSM_CHUNK, LANES), lambda c: (c, OFF_SMALL // LANES)),
                  pl.BlockSpec((None, 8, LANES), lambda c: (l, 0, 0))],
        out_specs=[pl.BlockSpec((SSM_CHUNK, D_SSM), lambda c: (c, 0)),
                   pl.BlockSpec((1, N_PAIR, SSM_STATE, LANES), lambda c: (c, 0, 0, 0))],
        out_shape=[jax.ShapeDtypeStruct((s, D_SSM), F32), jax.ShapeDtypeStruct((nch, N_PAIR, SSM_STATE, LANES), F32)],
        scratch_shapes=[pltpu.VMEM((N_PAIR, SSM_STATE, LANES), F32)],
        compiler_params=_params(dimension_semantics=("arbitrary",)),
    )(xbc, proj, ptile)


def ssd_bwd(xbc, proj, ptile, l, prevs, dy):
    s = xbc.shape[0]
    nch = s // SSM_CHUNK

    def body(xbc_ref, small_ref, p_ref, prev_ref, dy_ref, dxbc_ref, dsmall_ref, dp_ref, dstate_ref):
        @pl.when(pl.program_id(0) == 0)
        def _():
            dstate_ref[...] = jnp.zeros_like(dstate_ref)
            dp_ref[...] = jnp.zeros_like(dp_ref)

        xs = [xbc_ref[:, LANES * j:LANES * (j + 1)] for j in range(N_PAIR)]
        bs = [xbc_ref[:, D_SSM + LANES * g:D_SSM + LANES * (g + 1)] for g in range(SSM_GROUPS)]
        cs = [xbc_ref[:, D_SSM + 512 + LANES * g:D_SSM + 512 + LANES * (g + 1)] for g in range(SSM_GROUPS)]
        prev = [prev_ref[0, j] for j in range(N_PAIR)]
        dys = [dy_ref[:, LANES * j:LANES * (j + 1)] for j in range(N_PAIR)]
        dnew = [dstate_ref[j] for j in range(N_PAIR)]
        _, vjp = jax.vjp(_ssd_chunk, xs, bs, cs, small_ref[...], p_ref[0:1, :], p_ref[1:2, :], p_ref[2:3, :], prev)
        dxs, dbs, dcs, dsmall, ddtb, dalog, ddsk, dprev = vjp((dys, dnew))
        for j in range(N_PAIR):
            dxbc_ref[:, LANES * j:LANES * (j + 1)] = dxs[j]
            dstate_ref[j] = dprev[j]
        for g in range(SSM_GROUPS):
            dxbc_ref[:, D_SSM + LANES * g:D_SSM + LANES * (g + 1)] = dbs[g]
            dxbc_ref[:, D_SSM + 512 + LANES * g:D_SSM + 512 + LANES * (g + 1)] = dcs[g]
        dsmall_ref[...] = dsmall
        dp_ref[0:1, :] += ddtb
        dp_ref[1:2, :] += dalog
        dp_ref[2:3, :] += ddsk

    rev = lambda c: nch - 1 - c
    return pl.pallas_call(
        body, name="ssd_bwd", grid=(nch,),
        in_specs=[pl.BlockSpec((SSM_CHUNK, CONV_CH), lambda c: (rev(c), 0)),
                  pl.BlockSpec((SSM_CHUNK, LANES), lambda c: (rev(c), OFF_SMALL // LANES)),
                  pl.BlockSpec((None, 8, LANES), lambda c: (l, 0, 0)),
                  pl.BlockSpec((1, N_PAIR, SSM_STATE, LANES), lambda c: (rev(c), 0, 0, 0)),
                  pl.BlockSpec((SSM_CHUNK, D_SSM), lambda c: (rev(c), 0))],
        out_specs=[pl.BlockSpec((SSM_CHUNK, CONV_CH), lambda c: (rev(c), 0)),
                   pl.BlockSpec((SSM_CHUNK, LANES), lambda c: (rev(c), 0)),
                   pl.BlockSpec((8, LANES), lambda c: (0, 0))],
        out_shape=[jax.ShapeDtypeStruct((s, CONV_CH), F32), jax.ShapeDtypeStruct((s, LANES), F32),
                   jax.ShapeDtypeStruct((8, LANES), F32)],
        scratch_shapes=[pltpu.VMEM((N_PAIR, SSM_STATE, LANES), F32)],
        compiler_params=_params(dimension_semantics=("arbitrary",)),
    )(xbc, proj, ptile, prevs, dy)


ROPE_TM = 256


def _rope_tile(t, cosm, sinm):
    lane = lax.broadcasted_iota(jnp.int32, t.shape, 1)
    half = QK_ROPE // 2
    partner = jnp.where(lane < ROPE_LANE0 + half, pltpu.roll(t, LANES - half, 1), pltpu.roll(t, half, 1))
    return t * cosm + partner * sinm


def _in_rope(shape):
    lane = lax.broadcasted_iota(jnp.int32, shape, 1)
    return jnp.logical_and(lane >= ROPE_LANE0, lane < ROPE_LANE0 + QK_ROPE)


def build_k(kn, proj, cosm, sinm):
    s, w = kn.shape

    def body(k_ref, small_ref, c_ref, s_ref, o_ref):
        small = small_ref[...]
        inrope = _in_rope(small.shape)
        kpe = jnp.where(inrope, _rope_tile(jnp.where(inrope, small, 0.0), c_ref[...], s_ref[...]), 0.0)
        for h in range(MLA_HEADS):
            sl = slice(HEAD_PAD * h, HEAD_PAD * (h + 1))
            o_ref[:, sl] = (k_ref[:, sl].astype(F32) + kpe).astype(o_ref.dtype)

    row = pl.BlockSpec((ROPE_TM, w), lambda i: (i, 0))
    tab = pl.BlockSpec((ROPE_TM, LANES), lambda i: (i, 0))
    return pl.pallas_call(
        body, name="build_k", grid=(s // ROPE_TM,),
        in_specs=[row, pl.BlockSpec((ROPE_TM, LANES), lambda i: (i, OFF_SMALL // LANES)), tab, tab], out_specs=row,
        out_shape=jax.ShapeDtypeStruct((s, w), BF16), compiler_params=_params(dimension_semantics=("arbitrary",)),
    )(kn, proj, cosm, sinm)


def dsmall_bwd(dk, dsmall_ssd, cosm, sinm_neg):
    def fn(dkt, ds, c, sn):
        inrope = _in_rope(ds.shape)
        tot = dkt[:, 0:HEAD_PAD]
        for h in range(1, MLA_HEADS):
            tot = tot + dkt[:, HEAD_PAD * h:HEAD_PAD * (h + 1)]
        tot = jnp.where(inrope, tot, 0.0)
        return ds + jnp.where(inrope, _rope_tile(tot, c, sn), 0.0)

    return rowwise(fn, [(dk, MLA_HEADS * HEAD_PAD, 0), (dsmall_ssd, LANES, 0), (cosm, LANES, 0), (sinm_neg, LANES, 0)],
                   [], [(LANES, BF16)], [], "dsmall_bwd")[0]


ATT_TQ = 512
ATT_SCALE = (QK_NOPE + QK_ROPE) ** -0.5


def _att_scores(qh, kh, q0):
    s = lax.dot_general(qh, kh, _DIMS['nt'], preferred_element_type=F32) * ATT_SCALE
    r = lax.broadcasted_iota(jnp.int32, s.shape, 0) + q0
    c = lax.broadcasted_iota(jnp.int32, s.shape, 1)
    return jnp.where(c <= r, s, -1e30)


def mla_fwd(q, k, v):
    s = q.shape[0]

    def body(q_ref, k_ref, v_ref, o_ref, lse_ref):
        lane = lax.broadcasted_iota(jnp.int32, (ATT_TQ, LANES), 1)

        def block(ib):
            n = ATT_TQ * (ib + 1)
            v_t = v_ref[0:n, :]
            vlane = lax.broadcasted_iota(jnp.int32, v_t.shape, 1)
            o_tot, lse_tot = None, None
            for h in range(2):
                hs = slice(HEAD_PAD * h, HEAD_PAD * (h + 1))
                sc = _att_scores(q_ref[:, hs], k_ref[0:n, hs], ATT_TQ * ib)
                m = jnp.max(sc, axis=1, keepdims=True)
                p = jnp.exp(sc - m)
                l = jnp.sum(p, axis=1, keepdims=True)
                vh = jnp.where((vlane < V_DIM) if h == 0 else (vlane >= V_DIM), v_t, jnp.zeros_like(v_t))
                oh = lax.dot_general(p.astype(BF16), vh, _DIMS['nn'], preferred_element_type=F32) / l
                lse_h = jnp.where((lane < V_DIM) if h == 0 else (lane >= V_DIM), m + jnp.log(l), 0.0)
                o_tot = oh if o_tot is None else o_tot + oh
                lse_tot = lse_h if lse_tot is None else lse_tot + lse_h
            o_ref[...] = o_tot
            lse_ref[...] = lse_tot

        for ib in range(s // ATT_TQ):
            pl.when(pl.program_id(1) == ib)(functools.partial(block, ib))

    tile = pl.BlockSpec((ATT_TQ, LANES), lambda p, i: (i, p))
    return pl.pallas_call(
        body, name="mla_fwd", grid=(MLA_HEADS // 2, s // ATT_TQ),
        in_specs=[pl.BlockSpec((ATT_TQ, 2 * HEAD_PAD), lambda p, i: (i, p)),
                  pl.BlockSpec((s, 2 * HEAD_PAD), lambda p, i: (0, p)),
                  pl.BlockSpec((s, LANES), lambda p, i: (0, p))],
        out_specs=[tile, tile],
        out_shape=[jax.ShapeDtypeStruct((s, MLA_HEADS * V_DIM), F32)] * 2,
        compiler_params=_params(dimension_semantics=("arbitrary", "arbitrary")),
    )(q, k, v)


def mla_bwd(q, k, v, o, lse, do, cosm, sinm_neg):
    s = q.shape[0]

    def body(q_ref, k_ref, v_ref, o_ref, lse_ref, do_ref, c_ref, s_ref, dq_ref, dk_ref, dv_ref):
        i = pl.program_id(1)

        @pl.when(i == 0)
        def _():
            dk_ref[...] = jnp.zeros_like(dk_ref)
            dv_ref[...] = jnp.zeros_like(dv_ref)

        def block(ib):
            n = ATT_TQ * (ib + 1)
            o_t = o_ref[...]
            do_t = do_ref[...]
            lse_t = lse_ref[...]
            v_t = v_ref[0:n, :]
            lane = lax.broadcasted_iota(jnp.int32, do_t.shape, 1)
            for h in range(2):
                hs = slice(HEAD_PAD * h, HEAD_PAD * (h + 1))
                sel = (lane < V_DIM) if h == 0 else (lane >= V_DIM)
                qh = q_ref[:, hs]
                kh = k_ref[0:n, hs]
                doh = jnp.where(sel, do_t, 0.0)
                delta = jnp.sum(doh * o_t, axis=1, keepdims=True)
                lse_h = jnp.max(jnp.where(sel, lse_t, -jnp.inf), axis=1, keepdims=True)
                doh_b = doh.astype(BF16)
                p = jnp.exp(_att_scores(qh, kh, ATT_TQ * ib) - lse_h)
                dv_ref[0:n, :] += lax.dot_general(p.astype(BF16), doh_b, _DIMS['tn'], preferred_element_type=F32)
                dp = lax.dot_general(doh_b, v_t, _DIMS['nt'], preferred_element_type=F32)
                ds = (p * (dp - delta) * ATT_SCALE).astype(BF16)
                dk_ref[0:n, hs] += lax.dot_general(ds, qh, _DIMS['tn'], preferred_element_type=F32)
                dq = lax.dot_general(ds, kh, _DIMS['nn'], preferred_element_type=F32)
                dq_ref[:, hs] = _rope_tile(dq, c_ref[...], s_ref[...]).astype(dq_ref.dtype)

        for ib in range(s // ATT_TQ):
            pl.when(i == ib)(functools.partial(block, ib))

    tile = pl.BlockSpec((ATT_TQ, LANES), lambda p, i: (i, p))
    return pl.pallas_call(
        body, name="mla_bwd", grid=(MLA_HEADS // 2, s // ATT_TQ),
        in_specs=[pl.BlockSpec((ATT_TQ, 2 * HEAD_PAD), lambda p, i: (i, p)),
                  pl.BlockSpec((s, 2 * HEAD_PAD), lambda p, i: (0, p)),
                  pl.BlockSpec((s, LANES), lambda p, i: (0, p)), tile, tile, tile,
                  pl.BlockSpec((ATT_TQ, LANES), lambda p, i: (i, 0)), pl.BlockSpec((ATT_TQ, LANES), lambda p, i: (i, 0))],
        out_specs=[pl.BlockSpec((ATT_TQ, 2 * HEAD_PAD), lambda p, i: (i, p)),
                   pl.BlockSpec((s, 2 * HEAD_PAD), lambda p, i: (0, p)),
                   pl.BlockSpec((s, LANES), lambda p, i: (0, p))],
        out_shape=[jax.ShapeDtypeStruct((s, MLA_HEADS * HEAD_PAD), BF16),
                   jax.ShapeDtypeStruct((s, MLA_HEADS * HEAD_PAD), F32),
                   jax.ShapeDtypeStruct((s, MLA_HEADS * V_DIM), F32)],
        compiler_params=_params(dimension_semantics=("arbitrary", "arbitrary")),
    )(q, k, v, o, lse, do, cosm, sinm_neg)


MEM_TQ = 256
MEM_SCALE = MEM_HEAD_DIM ** -0.5


def _mem_probs(qh, kh):
    s = lax.dot_general(qh, kh, _DIMS['nt'], preferred_element_type=F32) * MEM_SCALE
    p = jnp.exp(s - jnp.max(s, axis=1, keepdims=True))
    return p / jnp.sum(p, axis=1, keepdims=True)


def mem_fwd(q, k, v):
    s = q.shape[0]

    def body(q_ref, k_ref, v_ref, o_ref):
        for h in range(MEM_HEADS):
            sl = slice(MEM_HEAD_DIM * h, MEM_HEAD_DIM * (h + 1))
            p = _mem_probs(q_ref[:, sl], k_ref[:, sl])
            o_ref[:, sl] = lax.dot_general(p.astype(BF16), v_ref[:, sl], _DIMS['nn'],
                                           preferred_element_type=F32).astype(o_ref.dtype)

    full = pl.BlockSpec((MEM_LEN, D_MODEL), lambda i: (0, 0))
    return pl.pallas_call(
        body, name="mem_fwd", grid=(s // MEM_TQ,),
        in_specs=[pl.BlockSpec((MEM_TQ, D_MODEL), lambda i: (i, 0)), full, full],
        out_specs=pl.BlockSpec((MEM_TQ, D_MODEL), lambda i: (i, 0)),
        out_shape=jax.ShapeDtypeStruct((s, D_MODEL), BF16),
        compiler_params=_params(dimension_semantics=("arbitrary",)),
    )(q, k, v)


def mem_bwd(q, k, v, do):
    s = q.shape[0]

    def body(q_ref, k_ref, v_ref, do_ref, dq_ref, dk_ref, dv_ref):
        @pl.when(pl.program_id(0) == 0)
        def _():
            dk_ref[...] = jnp.zeros_like(dk_ref)
            dv_ref[...] = jnp.zeros_like(dv_ref)

        for h in range(MEM_HEADS):
            sl = slice(MEM_HEAD_DIM * h, MEM_HEAD_DIM * (h + 1))
            qh, kh, vh = q_ref[:, sl], k_ref[:, sl], v_ref[:, sl]
            doh = do_ref[:, sl].astype(BF16)
            p = _mem_probs(qh, kh)
            dv_ref[:, sl] += lax.dot_general(p.astype(BF16), doh, _DIMS['tn'], preferred_element_type=F32)
            dp = lax.dot_general(doh, vh, _DIMS['nt'], preferred_element_type=F32)
            ds = (p * (dp - jnp.sum(p * dp, axis=1, keepdims=True)) * MEM_SCALE).astype(BF16)
            dq_ref[:, sl] = lax.dot_general(ds, kh, _DIMS['nn'], preferred_element_type=F32).astype(dq_ref.dtype)
            dk_ref[:, sl] += lax.dot_general(ds, qh, _DIMS['tn'], preferred_element_type=F32)

    full = pl.BlockSpec((MEM_LEN, D_MODEL), lambda i: (0, 0))
    row = pl.BlockSpec((MEM_TQ, D_MODEL), lambda i: (i, 0))
    return pl.pallas_call(
        body, name="mem_bwd", grid=(s // MEM_TQ,),
        in_specs=[row, full, full, row], out_specs=[row, full, full],
        out_shape=[jax.ShapeDtypeStruct((s, D_MODEL), BF16), jax.ShapeDtypeStruct((MEM_LEN, D_MODEL), F32),
                   jax.ShapeDtypeStruct((MEM_LEN, D_MODEL), F32)],
        compiler_params=_params(dimension_semantics=("arbitrary",)),
    )(q, k, v, do)


def _gate_norm(y, z, g):
    return _rms(y * _silu(z), g)


def gate_norm_fwd(y, proj, g):
    return rowwise(_gate_norm, [(y, D_SSM, 0), (proj, D_SSM, OFF_Z // D_SSM)], [g], [(D_SSM, BF16, D_MIX, 0)], [],
                   "gate_norm_fwd")[0]


def loss_head(x, g, target):
    def fn(xt, tt, gt):
        def f(x_, g_):
            err = _rms(x_, g_) - tt
            return 0.5 * jnp.sum(jnp.mean(err * err, axis=-1))

        lv, (dx, dg) = jax.value_and_grad(f, argnums=(0, 1))(xt, gt)
        return dx, dg, jnp.full((1, LANES), lv, F32)

    return rowwise(fn, [(x, D_MODEL, 0), (target, D_MODEL, 0)], [g], [(D_MODEL, F32)],
                   [((1, D_MODEL), F32), ((1, LANES), F32)], "loss_head")


def _proj_runs(d):
    lo, hi = (D_IN // N_DEV) * d, (D_IN // N_DEV) * (d + 1)
    runs = []
    for a, b, new in PROJ_SEGS:
        s0, s1 = max(a, lo), min(b, hi)
        if s0 < s1:
            runs.append((s0 - lo, new + s0 - a, s1 - s0))
    return runs


LAYOUT_TM = 256


def assemble_proj(g):
    def body(g_ref, o_ref):
        o_ref[:, OFF_SMALL:OFF_SMALL + LANES] = jnp.zeros((LAYOUT_TM, LANES), o_ref.dtype)
        for d in range(N_DEV):
            for src, dst, n in _proj_runs(d):
                o_ref[:, dst:dst + n] = g_ref[d, :, src:src + n]

    return pl.pallas_call(
        body, name="assemble_proj", grid=(D_MODEL // LAYOUT_TM,),
        in_specs=[pl.BlockSpec((N_DEV, LAYOUT_TM, D_IN // N_DEV), lambda i: (0, i, 0))],
        out_specs=pl.BlockSpec((LAYOUT_TM, PROJ_W), lambda i: (i, 0)),
        out_shape=jax.ShapeDtypeStruct((D_MODEL, PROJ_W), g.dtype),
        compiler_params=_params(dimension_semantics=("arbitrary",)),
    )(g)


def extract_proj(dz, dxbc, dcq, dsmall, dckv):
    pieces = [(OFF_Z, 1024), (OFF_XBC, 2048), (OFF_CQ, Q_LORA), (OFF_SMALL, LANES), (OFF_CKV, KV_LORA)]

    def body(*refs):
        o_ref = refs[-1]
        for d in range(N_DEV):
            for src, dst, n in _proj_runs(d):
                for p, (off, w) in enumerate(pieces):
                    if off <= dst < off + w:
                        o_ref[d, :, src:src + n] = refs[p][:, dst - off:dst - off + n].astype(o_ref.dtype)

    return pl.pallas_call(
        body, name="extract_proj", grid=(D_MODEL // LAYOUT_TM,),
        in_specs=[pl.BlockSpec((LAYOUT_TM, w), lambda i: (i, 0)) for _, w in pieces],
        out_specs=pl.BlockSpec((N_DEV, LAYOUT_TM, D_IN // N_DEV), lambda i: (0, i, 0)),
        out_shape=jax.ShapeDtypeStruct((N_DEV, D_MODEL, D_IN // N_DEV), BF16),
        compiler_params=_params(dimension_semantics=("arbitrary",)),
    )(dz, dxbc, dcq, dsmall, dckv)


_QW = QK_NOPE + QK_ROPE


def assemble_uq(g):
    def body(g_ref, o_ref):
        o_ref[...] = jnp.zeros_like(o_ref)
        for d in range(N_DEV):
            for e in range(2):
                dst = HEAD_PAD * (2 * d + e)
                o_ref[:, dst:dst + _QW] = g_ref[d, :, _QW * e:_QW * (e + 1)]

    return pl.pallas_call(
        body, name="assemble_uq", grid=(1,),
        in_specs=[pl.BlockSpec((N_DEV, Q_LORA, 2 * _QW), lambda i: (0, 0, 0))],
        out_specs=pl.BlockSpec((Q_LORA, MLA_HEADS * HEAD_PAD), lambda i: (0, 0)),
        out_shape=jax.ShapeDtypeStruct((Q_LORA, MLA_HEADS * HEAD_PAD), g.dtype),
        compiler_params=_params(dimension_semantics=("arbitrary",)),
    )(g)


def extract_uq(dw):
    def body(w_ref, o_ref):
        for d in range(N_DEV):
            for e in range(2):
                src = HEAD_PAD * (2 * d + e)
                o_ref[d, :, _QW * e:_QW * (e + 1)] = w_ref[:, src:src + _QW].astype(o_ref.dtype)

    return pl.pallas_call(
        body, name="extract_uq", grid=(1,),
        in_specs=[pl.BlockSpec((Q_LORA, MLA_HEADS * HEAD_PAD), lambda i: (0, 0))],
        out_specs=pl.BlockSpec((N_DEV, Q_LORA, 2 * _QW), lambda i: (0, 0, 0)),
        out_shape=jax.ShapeDtypeStruct((N_DEV, Q_LORA, 2 * _QW), BF16),
        compiler_params=_params(dimension_semantics=("arbitrary",)),
    )(dw)


def assemble_ukv(g):
    def body(g_ref, kn_ref, v_ref):
        kn_ref[...] = jnp.zeros_like(kn_ref)
        for d in range(N_DEV):
            for e in range(2):
                h = 2 * d + e
                kn_ref[:, HEAD_PAD * h:HEAD_PAD * h + QK_NOPE] = g_ref[d, :, 128 * e:128 * e + QK_NOPE]
                v_ref[:, V_DIM * h:V_DIM * (h + 1)] = g_ref[d, :, 128 * e + QK_NOPE:128 * (e + 1)]

    return pl.pallas_call(
        body, name="assemble_ukv", grid=(1,),
        in_specs=[pl.BlockSpec((N_DEV, KV_LORA, 256), lambda i: (0, 0, 0))],
        out_specs=[pl.BlockSpec((KV_LORA, MLA_HEADS * HEAD_PAD), lambda i: (0, 0)),
                   pl.BlockSpec((KV_LORA, MLA_HEADS * V_DIM), lambda i: (0, 0))],
        out_shape=[jax.ShapeDtypeStruct((KV_LORA, MLA_HEADS * HEAD_PAD), g.dtype),
                   jax.ShapeDtypeStruct((KV_LORA, MLA_HEADS * V_DIM), g.dtype)],
        compiler_params=_params(dimension_semantics=("arbitrary",)),
    )(g)


def extract_ukv(dkn, dv):
    def body(kn_ref, v_ref, o_ref):
        for d in range(N_DEV):
            for e in range(2):
                h = 2 * d + e
                o_ref[d, :, 128 * e:128 * e + QK_NOPE] = kn_ref[:, HEAD_PAD * h:HEAD_PAD * h + QK_NOPE].astype(o_ref.dtype)
                o_ref[d, :, 128 * e + QK_NOPE:128 * (e + 1)] = v_ref[:, V_DIM * h:V_DIM * (h + 1)].astype(o_ref.dtype)

    return pl.pallas_call(
        body, name="extract_ukv", grid=(1,),
        in_specs=[pl.BlockSpec((KV_LORA, MLA_HEADS * HEAD_PAD), lambda i: (0, 0)),
                  pl.BlockSpec((KV_LORA, MLA_HEADS * V_DIM), lambda i: (0, 0))],
        out_specs=pl.BlockSpec((N_DEV, KV_LORA, 256), lambda i: (0, 0, 0)),
        out_shape=jax.ShapeDtypeStruct((N_DEV, KV_LORA, 256), BF16),
        compiler_params=_params(dimension_semantics=("arbitrary",)),
    )(dkn, dv)


_UPW = 2 * D_FF // N_DEV


def assemble_up(g):
    def body(g_ref, wg_ref, wv_ref):
        for d in range(N_DEV):
            ref = wg_ref if d < N_DEV // 2 else wv_ref
            off = _UPW * (d % (N_DEV // 2))
            ref[:, off:off + _UPW] = g_ref[d]

    half = pl.BlockSpec((LAYOUT_TM, D_FF), lambda i: (i, 0))
    return pl.pallas_call(
        body, name="assemble_up", grid=(D_MODEL // LAYOUT_TM,),
        in_specs=[pl.BlockSpec((N_DEV, LAYOUT_TM, _UPW), lambda i: (0, i, 0))],
        out_specs=[half, half], out_shape=[jax.ShapeDtypeStruct((D_MODEL, D_FF), g.dtype)] * 2,
        compiler_params=_params(dimension_semantics=("arbitrary",)),
    )(g)


def extract_up(dwg, dwv):
    def body(wg_ref, wv_ref, o_ref):
        for d in range(N_DEV):
            ref = wg_ref if d < N_DEV // 2 else wv_ref
            off = _UPW * (d % (N_DEV // 2))
            o_ref[d] = ref[:, off:off + _UPW].astype(o_ref.dtype)

    half = pl.BlockSpec((LAYOUT_TM, D_FF), lambda i: (i, 0))
    return pl.pallas_call(
        body, name="extract_up", grid=(D_MODEL // LAYOUT_TM,), in_specs=[half, half],
        out_specs=pl.BlockSpec((N_DEV, LAYOUT_TM, _UPW), lambda i: (0, i, 0)),
        out_shape=jax.ShapeDtypeStruct((N_DEV, D_MODEL, _UPW), BF16),
        compiler_params=_params(dimension_semantics=("arbitrary",)),
    )(dwg, dwv)


MESH = pl.DeviceIdType.MESH
ANY = pl.BlockSpec(memory_space=pl.ANY)


def _place():
    mx, my, mc = lax.axis_index("x"), lax.axis_index("y"), lax.axis_index("c")
    return mx, my, mc, [(1 - mx, my), (mx, 1 - my), (1 - mx, 1 - my)]


def all_gather_blocks(xs, first_only=()):
    n = len(xs)

    def body(*refs):
        x_refs, out_refs = refs[:n], refs[n:2 * n]
        send_sems, recv_sems, local_sems = refs[2 * n:]
        mx, my, mc, chips = _place()
        me, sibling = (mx, my, mc), (mx, my, 1 - mc)
        x_refs = [x_refs[t].at[0] if t in first_only else x_refs[t] for t in range(n)]

        def rows(t, px, py, pc):
            dev = 4 * px + 2 * py + pc
            return out_refs[t].at[dev] if t in first_only else out_refs[t].at[:, dev]

        def copy(t, k, block, to, src=None):
            return pltpu.make_async_remote_copy(
                src_ref=rows(t, *block) if src is None else src, dst_ref=rows(t, *block),
                send_sem=send_sems.at[t, k], recv_sem=recv_sems.at[t, k], device_id=to, device_id_type=MESH)

        mine = [pltpu.make_async_copy(x_refs[t], rows(t, *me), local_sems.at[t]) for t in range(n)]
        for cp in mine:
            cp.start()
        first = []
        for t in range(n):
            first.append(copy(t, 0, me, sibling, src=x_refs[t]))
            first += [copy(t, 1 + j, me, (*chip, mc), src=x_refs[t]) for j, chip in enumerate(chips)]
        for cp in first:
            cp.start()
        passed = []
        for j, chip in enumerate(chips):
            for t in range(n):
                copy(t, 1 + j, (*chip, mc), me).wait_recv()
                cp = copy(t, 4 + j, (*chip, mc), sibling)
                cp.start()
                passed.append(cp)
        for t in range(n):
            copy(t, 0, sibling, me).wait_recv()
            for j, chip in enumerate(chips):
                copy(t, 4 + j, (*chip, 1 - mc), me).wait_recv()
        for cp in first + passed:
            cp.wait_send()
        for cp in mine:
            cp.wait()

    return pl.pallas_call(
        body, name="all_gather_blocks",
        out_shape=[jax.ShapeDtypeStruct(((N_DEV,) if t in first_only else (x.shape[0], N_DEV)) + x.shape[1:], x.dtype)
                   for t, x in enumerate(xs)],
        in_specs=[ANY] * n, out_specs=[ANY] * n,
        scratch_shapes=[pltpu.SemaphoreType.DMA((n, 7)), pltpu.SemaphoreType.DMA((n, 7)), pltpu.SemaphoreType.DMA((n,))],
    )(*xs)


HBM = pl.BlockSpec(memory_space=pltpu.HBM)
SEM = pl.BlockSpec(memory_space=pltpu.SEMAPHORE)
EFFECT = pltpu.SideEffectType.DATAFLOW_SIDE_EFFECTING
ALL_DEVICES = [(px, py, pc) for px in range(2) for py in range(2) for pc in range(2)]


def _hbm(x):
    return pltpu.with_memory_space_constraint(x, pltpu.HBM)


def _split_start(body, name, srcs, lands, after=None):
    ns, n = len(srcs), len(lands)
    extra = [after] if after is not None else []

    def full_body(*refs):
        sems = ns + n + len(extra)
        body(refs[:ns], refs[ns:ns + n], refs[sems], refs[sems + 1])
        refs[-1][...] = jnp.zeros_like(refs[-1])

    res = pl.pallas_call(
        full_body, name=name,
        out_shape=(pltpu.SemaphoreType.DMA((n,)), pltpu.SemaphoreType.DMA((n,)),
                   *[pltpu.HBM(x.shape, x.dtype) for x in srcs], *[pltpu.HBM(x.shape, x.dtype) for x in lands],
                   jax.ShapeDtypeStruct((8, LANES), F32)),
        in_specs=[HBM] * (ns + n) + [ANY] * len(extra),
        out_specs=(SEM, SEM, *[HBM] * (ns + n), pl.BlockSpec(memory_space=pltpu.VMEM)),
        input_output_aliases={i: 2 + i for i in range(ns + n)},
        compiler_params=pltpu.CompilerParams(has_side_effects=EFFECT),
    )(*[_hbm(x) for x in srcs], *[_hbm(x) for x in lands], *extra)
    return res[0], res[1], list(res[2:2 + ns]), list(res[2 + ns:2 + ns + n]), res[-1]


def _split_wait(name, send_sems, recv_sems, srcs, lands, after, sent, landed):
    ns, n = len(srcs), len(lands)

    def body(*refs):
        src_refs, land_refs, ssem, rsem = refs[:ns], refs[ns:ns + n], refs[ns + n], refs[ns + n + 1]
        mx, my, mc, _ = _place()
        for t in range(n):
            out = sent(src_refs[t] if ns else None, land_refs[t])
            inn = landed(land_refs[t])
            pltpu.make_async_remote_copy(src_ref=out, dst_ref=out, send_sem=ssem.at[t], recv_sem=rsem.at[t],
                                         device_id=(mx, my, mc), device_id_type=MESH).wait_send()
            pltpu.make_async_remote_copy(src_ref=inn, dst_ref=inn, send_sem=ssem.at[t], recv_sem=rsem.at[t],
                                         device_id=(mx, my, mc), device_id_type=MESH).wait_recv()

    res = pl.pallas_call(
        body, name=name,
        out_shape=(*[pltpu.HBM(x.shape, x.dtype) for x in srcs], *[pltpu.HBM(x.shape, x.dtype) for x in lands]),
        in_specs=[HBM] * (ns + n) + [SEM, SEM, ANY], out_specs=[HBM] * (ns + n),
        input_output_aliases={i: i for i in range(ns + n)},
        compiler_params=pltpu.CompilerParams(has_side_effects=EFFECT),
    )(*srcs, *lands, send_sems, recv_sems, after)
    return list(res[:ns]), list(res[ns:])


FIRST_HOP = 5
SECOND_HOP = 3


def gather_start(srcs, l, tag, after=None):
    lands = [lax.empty((N_DEV,) + x.shape[1:], x.dtype) for x in srcs]

    def body(src_refs, land_refs, send_sems, recv_sems):
        mx, my, mc, chips = _place()
        me = 4 * mx + 2 * my + mc
        for t in range(len(srcs)):
            for to in [(mx, my, mc), (mx, my, 1 - mc)] + [(cx, cy, mc) for cx, cy in chips]:
                pltpu.make_async_remote_copy(
                    src_ref=src_refs[t].at[l], dst_ref=land_refs[t].at[me], send_sem=send_sems.at[t],
                    recv_sem=recv_sems.at[t], device_id=to, device_id_type=MESH).start()

    return _split_start(body, "gather_start_%d%s" % (l, tag), srcs, lands, after=after)


def gather_wait(l, tag, send_sems, recv_sems, srcs, lands, after):
    hop = lambda d: d.at[pl.ds(0, FIRST_HOP)]
    return _split_wait("gather_wait_%d%s" % (l, tag), send_sems, recv_sems, srcs, lands, after,
                       sent=lambda s, d: hop(d), landed=hop)


def gather_pass_start(lands, l, tag):
    def body(src_refs, land_refs, send_sems, recv_sems):
        mx, my, mc, chips = _place()
        for t in range(len(lands)):
            for cx, cy in chips:
                slot = land_refs[t].at[4 * cx + 2 * cy + mc]
                pltpu.make_async_remote_copy(
                    src_ref=slot, dst_ref=slot, send_sem=send_sems.at[t], recv_sem=recv_sems.at[t],
                    device_id=(mx, my, 1 - mc), device_id_type=MESH).start()

    send_sems, recv_sems, _, lands, tie = _split_start(body, "gather_pass_start_%d%s" % (l, tag), [], lands)
    return send_sems, recv_sems, lands, tie


def gather_pass_wait(l, tag, send_sems, recv_sems, lands, after):
    hop = lambda d: d.at[pl.ds(0, SECOND_HOP)]
    return _split_wait("gather_pass_wait_%d%s" % (l, tag), send_sems, recv_sems, [], lands, after,
                       sent=lambda s, d: hop(d), landed=hop)[1]


def small_gather_start(rows):
    def body(src_refs, land_refs, send_sems, recv_sems):
        mx, my, mc, _ = _place()
        for to in ALL_DEVICES:
            pltpu.make_async_remote_copy(
                src_ref=src_refs[0], dst_ref=land_refs[0].at[4 * mx + 2 * my + mc], send_sem=send_sems.at[0],
                recv_sem=recv_sems.at[0], device_id=to, device_id_type=MESH).start()

    return _split_start(body, "small_gather_start", [rows], [lax.empty((N_DEV,) + rows.shape, rows.dtype)])


def small_gather_wait(send_sems, recv_sems, srcs, lands, after):
    return _split_wait("small_gather_wait", send_sems, recv_sems, srcs, lands, after,
                       sent=lambda s, d: d, landed=lambda d: d)[1][0]


def grad_exchange_start(es, lands, l, tag, after=None):
    def body(e_refs, land_refs, send_sems, recv_sems):
        mx, my, mc, _ = _place()
        me = 4 * mx + 2 * my + mc
        for t in range(len(es)):
            for px, py, pc in ALL_DEVICES:
                pltpu.make_async_remote_copy(
                    src_ref=e_refs[t].at[4 * px + 2 * py + pc], dst_ref=land_refs[t].at[l, me], send_sem=send_sems.at[t],
                    recv_sem=recv_sems.at[t], device_id=(px, py, pc), device_id_type=MESH).start()

    return _split_start(body, "grad_exchange_start_%d%s" % (l, tag), es, lands, after=after)


def grad_exchange_wait(l, tag, send_sems, recv_sems, es, lands, after):
    return _split_wait("grad_exchange_wait_%d%s" % (l, tag), send_sems, recv_sems, es, lands, after,
                       sent=lambda s, d: s, landed=lambda d: d.at[l])


def _adam(g, w, m, v):
    nm = ADAM_B1 * m + (1.0 - ADAM_B1) * g
    nv = ADAM_B2 * v + (1.0 - ADAM_B2) * jnp.square(g)
    m_hat = nm / (1.0 - ADAM_B1 ** ADAM_STEP)
    v_hat = nv / (1.0 - ADAM_B2 ** ADAM_STEP)
    return -ADAM_LR * (m_hat / (jnp.sqrt(v_hat) + ADAM_EPS) + ADAM_WD * w), nm, nv


def adamw_big(parts, w, m, v, name, tie):
    depth, _, a, b = parts.shape
    ta = _row_tile(a)

    def body(p_ref, w_ref, m_ref, v_ref, tie_ref, g_ref, d_ref, nm_ref, nv_ref):
        g = p_ref[0].astype(F32)
        for k in range(1, N_DEV):
            g = g + p_ref[k].astype(F32)
        g_ref[...] = g
        d_ref[...], nm_ref[...], nv_ref[...] = _adam(g, w_ref[...], m_ref[...], v_ref[...])

    blk = pl.BlockSpec((None, ta, b), lambda l, i: (l, i, 0))
    return pl.pallas_call(
        body, name=name, grid=(depth, a // ta),
        in_specs=[pl.BlockSpec((None, N_DEV, ta, b), lambda l, i: (l, 0, i, 0)), blk, blk, blk, ANY], out_specs=[blk] * 4,
        out_shape=[jax.ShapeDtypeStruct((depth, a, b), F32)] * 4,
        compiler_params=_params(dimension_semantics=("arbitrary", "arbitrary")),
    )(parts, w, m, v, tie)


SMALL_VIEW = {'norm_mix': (DEPTH, 1024), 'ssm_norm': (DEPTH, 1024), 'attn_out_norm': (DEPTH, 1024),
              'norm_mem_q': (DEPTH, 1024), 'norm_mem_kv': (DEPTH, 1024), 'norm_ffn': (DEPTH, 1024),
              'q_norm': (DEPTH, 384), 'kv_norm': (DEPTH, 256), 'ssm_conv_b': (DEPTH, 2048), 'ffn_conv_b': (DEPTH, 5632),
              'dt_bias': (DEPTH, SSM_HEADS), 'a_log': (DEPTH, SSM_HEADS), 'd_skip': (DEPTH, SSM_HEADS),
              'ssm_conv_w': (DEPTH, SSM_CONV * CONV_CH // N_DEV), 'ffn_conv_w': (DEPTH, FFN_CONV * 2 * D_FF // N_DEV),
              'final_norm': (1, 1024)}
SMALL_NAMES = list(SMALL_VIEW)
SMALL_SHARDED = {'ssm_conv_w': (SSM_CONV, CONV_CH // N_DEV, CONV_CH), 'ffn_conv_w': (FFN_CONV, 2 * D_FF // N_DEV, 2 * D_FF)}


def adamw_small(gathered, ws, ms, vs, tie):
    nsm = len(SMALL_NAMES)

    def body(*refs):
        g8_ref = refs[0]
        w_refs, m_refs, v_refs = refs[1:1 + nsm], refs[1 + nsm:1 + 2 * nsm], refs[1 + 2 * nsm:1 + 3 * nsm]
        outs = refs[2 + 3 * nsm:2 + 7 * nsm]
        sum_ref = refs[2 + 7 * nsm]
        shard_bufs = refs[3 + 7 * nsm:]
        tot = g8_ref[0]
        for d in range(1, N_DEV):
            tot = tot + g8_ref[d]
        sum_ref[...] = tot
        mx, my, mc, _ = _place()
        dev = 4 * mx + 2 * my + mc

        def update(i, g):
            d, nm, nv = _adam(g, w_refs[i][...], m_refs[i][...], v_refs[i][...])
            outs[i][...] = g
            outs[nsm + i][...] = d
            outs[2 * nsm + i][...] = nm
            outs[3 * nsm + i][...] = nv

        for i, name in enumerate(SMALL_NAMES):
            rows, cols = SMALL_VIEW[name]
            off = SMALL_OFF[name]
            if name in SMALL_SHARDED:
                taps, per, full = SMALL_SHARDED[name]
                buf = shard_bufs[list(SMALL_SHARDED).index(name)]
                for d in range(N_DEV):
                    @pl.when(dev == d)
                    def _(d=d, taps=taps, per=per, full=full, off=off, buf=buf):
                        for k in range(taps):
                            buf[:, per * k:per * (k + 1)] = sum_ref[:, off + full * k + per * d:off + full * k + per * (d + 1)]
                update(i, buf[...])
            else:
                update(i, sum_ref[0:rows, off:off + cols])

    views = [jax.ShapeDtypeStruct(SMALL_VIEW[n], F32) for n in SMALL_NAMES]
    vmem = pl.BlockSpec(memory_space=pltpu.VMEM)
    res = pl.pallas_call(
        body, name="adamw_small", out_shape=views * 4, in_specs=[vmem] * (1 + 3 * nsm) + [ANY],
        out_specs=[vmem] * (4 * nsm),
        scratch_shapes=[pltpu.VMEM((DEPTH, SMALL_W), F32)] + [pltpu.VMEM(SMALL_VIEW[n], F32) for n in SMALL_SHARDED],
        compiler_params=_params(),
    )(gathered, *[ws[n] for n in SMALL_NAMES], *[ms[n] for n in SMALL_NAMES], *[vs[n] for n in SMALL_NAMES], tie)
    return [dict(zip(SMALL_NAMES, res[k * nsm:(k + 1) * nsm])) for k in range(4)]


def _layer_weights(gathered):
    w = {}
    for n, g in gathered.items():
        if n == 'w_in':
            w['w_proj'] = assemble_proj(g)
        elif n == 'w_uq':
            w['w_uq'] = assemble_uq(g)
        elif n == 'w_ukv':
            w['w_kn'], w['w_v'] = assemble_ukv(g)
        elif n == 'w_up':
            w['w_g'], w['w_vv'] = assemble_up(g)
        else:
            w[n] = g.reshape(N_DEV * BIG[n][0], BIG[n][1])
    return w


def _rope_post(acc, row_tiles, full_tiles, o_refs):
    for h in range(acc.shape[1] // HEAD_PAD):
        sl = slice(HEAD_PAD * h, HEAD_PAD * (h + 1))
        o_refs[0][:, sl] = _rope_tile(acc[:, sl], row_tiles[0], row_tiles[1]).astype(o_refs[0].dtype)


def _norm_post(acc, row_tiles, full_tiles, o_refs):
    o_refs[0][...] = acc
    o_refs[1][...] = _rms(acc, full_tiles[0]).astype(o_refs[1].dtype)


def _norm_bwd_post(acc, row_tiles, full_tiles, o_refs):
    _, vjp = jax.vjp(_rms, row_tiles[0], full_tiles[0])
    dx, dg = vjp(acc)
    o_refs[0][...] = dx + row_tiles[1]
    o_refs[1][...] += dg


def _latent_norm_bwd_post(acc, row_tiles, full_tiles, o_refs):
    _, vjp = jax.vjp(_rms, row_tiles[0], full_tiles[0])
    dx, dg = vjp(acc)
    o_refs[0][...] = dx.astype(o_refs[0].dtype)
    o_refs[1][...] += dg


def _mix_bwd_post(acc, row_tiles, full_tiles, o_refs):
    y, z, o = row_tiles
    _, vjp = jax.vjp(_gate_norm, y, z, full_tiles[0])
    dy, dz, dg = vjp(acc[:, :D_SSM])
    _, vjp_o = jax.vjp(_rms, o, full_tiles[1])
    do, dg_o = vjp_o(acc[:, D_SSM:])
    o_refs[0][...] = dy
    o_refs[1][...] = dz.astype(o_refs[1].dtype)
    o_refs[2][...] = do
    o_refs[3][...] += dg
    o_refs[4][...] += dg_o


def layer_fwd(x0, h1, mem, cosm, sinm, w, sm, l, tie=None):
    gain = lambda n: (sm[n], l)
    sv = dict(x0=x0)
    sv['h1'] = h1 if h1 is not None else rmsnorm_fwd(x0, gain('norm_mix'), "norm_mix_fwd", tie=tie)
    proj = sv['proj'] = matmul([(sv['h1'], w['w_proj'])], 'nn', F32, "proj_fwd", tie=tie if h1 is not None else None)
    sv['xbc'] = ssm_conv_fwd(proj, sm['ssm_conv_w'], sm['ssm_conv_b'], l)
    sv['y'], sv['prevs'] = ssd_fwd(sv['xbc'], proj, sm['ptile'], l)
    mix = gate_norm_fwd(sv['y'], proj, gain('ssm_norm'))
    sv['cqn'] = rmsnorm_fwd(proj, gain('q_norm'), "q_norm_fwd", Q_LORA, OFF_CQ // Q_LORA)
    sv['ckvn'] = rmsnorm_fwd(proj, gain('kv_norm'), "kv_norm_fwd", KV_LORA, OFF_CKV // KV_LORA)
    sv['q'] = matmul([(sv['cqn'], w['w_uq'])], 'nn', BF16, "uq_fwd", post=_rope_post, rows=[cosm, sinm])
    kn = matmul([(sv['ckvn'], w['w_kn'])], 'nn', BF16, "kn_fwd")
    sv['k'] = build_k(kn, proj, cosm, sinm)
    sv['v'] = matmul([(sv['ckvn'], w['w_v'])], 'nn', BF16, "v_fwd")
    sv['o'], sv['lse'] = mla_fwd(sv['q'], sv['k'], sv['v'])
    mix = sv['mix'] = rmsnorm_fwd(sv['o'], gain('attn_out_norm'), "attn_out_norm_fwd", out=(D_SSM, BF16, D_MIX, 1),
                                  into=(mix, 0))
    x1, sv['hq'] = matmul([(mix, w['w_out'])], 'nn', F32, "out_fwd", add=x0, post=_norm_post,
                          fulls=[gain('norm_mem_q')], outs=[F32, BF16], full_n=True)
    sv['x1'] = x1
    sv['mn'] = rmsnorm_fwd(mem, gain('norm_mem_kv'), "norm_mem_kv_fwd")
    if 'later' in w:
        w.update(w.pop('later')(sv['hq']))
    sv['mq'] = matmul([(sv['hq'], w['w_mq'])], 'nn', BF16, "mq_fwd")
    sv['mk'] = matmul([(sv['mn'], w['w_mk'])], 'nn', BF16, "mk_fwd")
    sv['mv'] = matmul([(sv['mn'], w['w_mv'])], 'nn', BF16, "mv_fwd")
    sv['om'] = mem_fwd(sv['mq'], sv['mk'], sv['mv'])
    x2, sv['h3'] = matmul([(sv['om'], w['w_mo'])], 'nn', F32, "mo_fwd", add=x1, post=_norm_post,
                          fulls=[gain('norm_ffn')], outs=[F32, BF16], full_n=True)
    sv['x2'] = x2
    tie_ffn = w.pop('prefetch')(sv['h3']) if 'prefetch' in w else None
    sv['ug'] = matmul([(sv['h3'], w['w_g'])], 'nn', F32, "up_g_fwd", tie=tie_ffn)
    sv['uv'] = matmul([(sv['h3'], w['w_vv'])], 'nn', F32, "up_v_fwd")
    sv['a'] = ffn_act_fwd(sv['ug'], sv['uv'], sm['ffn_conv_w'], sm['ffn_conv_b'], l)
    if l + 1 < DEPTH:
        x3, h1_next = matmul([(sv['a'], w['w_down'])], 'nn', F32, "down_fwd", add=x2, post=_norm_post,
                             fulls=[(sm['norm_mix'], l + 1)], outs=[F32, BF16], full_n=True)
    else:
        x3, h1_next = matmul([(sv['a'], w['w_down'])], 'nn', F32, "down_fwd_last", add=x2), None
    return x3, h1_next, sv


EARLY_GRADS = ('w_down', 'w_up', 'w_mo', 'w_mq', 'w_mk', 'w_mv', 'w_out')
LATE_GRADS = ('w_uq', 'w_ukv', 'w_in')


def layer_bwd(dx3, mem, cosm, sinm_neg, w, sm, l, sv, on_grads, tie=None):
    gain = lambda n: (sm[n], l)
    big, small = {}, {}
    proj = sv['proj']
    da = matmul([(dx3, w['w_down'])], 'nt', BF16, "down_bwd_a", tie=tie)
    big['w_down'] = matmul([(sv['a'], dx3)], 'tn', BF16, "down_bwd_w")
    dug, duv, dcwg, dcwv, dcbg, dcbv = ffn_act_bwd(sv['ug'], sv['uv'], sm['ffn_conv_w'], sm['ffn_conv_b'], l, da)
    small['ffn_conv_w'] = jnp.concatenate([dcwg, dcwv], axis=1)
    small['ffn_conv_b'] = jnp.concatenate([dcbg, dcbv], axis=1)
    gacc = [((1, D_MODEL), F32)]
    dx2, small['norm_ffn'] = matmul([(dug, w['w_g']), (duv, w['w_vv'])], 'nt', F32, "up_bwd_h", post=_norm_bwd_post,
                                    rows=[sv['x2'], dx3], fulls=[gain('norm_ffn')], accs=gacc, full_n=True, tm_cap=256)
    big['w_up'] = extract_up(matmul([(sv['h3'], dug)], 'tn', BF16, "up_g_bwd_w"),
                             matmul([(sv['h3'], duv)], 'tn', BF16, "up_v_bwd_w"))
    dom = matmul([(dx2, w['w_mo'])], 'nt', BF16, "mo_bwd_a")
    big['w_mo'] = matmul([(sv['om'], dx2)], 'tn', BF16, "mo_bwd_w")
    dmq, dmk, dmv = mem_bwd(sv['mq'], sv['mk'], sv['mv'], dom)
    dx1, small['norm_mem_q'] = matmul([(dmq, w['w_mq'])], 'nt', F32, "mq_bwd_a", post=_norm_bwd_post,
                                      rows=[sv['x1'], dx2], fulls=[gain('norm_mem_q')], accs=gacc, full_n=True)
    big['w_mq'] = matmul([(sv['hq'], dmq)], 'tn', BF16, "mq_bwd_w")
    dmn = matmul([(dmk, w['w_mk']), (dmv, w['w_mv'])], 'nt', BF16, "mkv_bwd_a")
    big['w_mk'] = matmul([(sv['mn'], dmk)], 'tn', BF16, "mk_bwd_w")
    big['w_mv'] = matmul([(sv['mn'], dmv)], 'tn', BF16, "mv_bwd_w")
    _, small['norm_mem_kv'] = rmsnorm_bwd(mem, gain('norm_mem_kv'), dmn, "norm_mem_kv_bwd", dx_dtype=BF16)
    big['w_out'] = matmul([(sv['mix'], dx1)], 'tn', BF16, "out_bwd_w")
    early = {n: big.pop(n).reshape((N_DEV,) + BIG[n]) if n != 'w_up' else big.pop(n) for n in EARLY_GRADS}
    tie = on_grads(l, 'a', early)
    dy, dz, do, small['ssm_norm'], small['attn_out_norm'] = matmul(
        [(dx1, w['w_out'])], 'nt', F32, "out_bwd_a", post=_mix_bwd_post, tie=tie,
        rows=[sv['y'], (proj, D_SSM, OFF_Z // D_SSM), sv['o']], fulls=[gain('ssm_norm'), gain('attn_out_norm')],
        outs=[(D_SSM, F32), (D_SSM, BF16), (D_SSM, F32)], accs=[((1, D_SSM), F32)] * 2, full_n=True)
    dxbc_act, dsmall_ssd, small['ptile'] = ssd_bwd(sv['xbc'], proj, sm['ptile'], l, sv['prevs'], dy)
    dxbc, small['ssm_conv_w'], small['ssm_conv_b'] = ssm_conv_bwd(proj, sm['ssm_conv_w'], sm['ssm_conv_b'], l, dxbc_act)
    dq, dk, dv = mla_bwd(sv['q'], sv['k'], sv['v'], sv['o'], sv['lse'], do, cosm, sinm_neg)
    dsmall = dsmall_bwd(dk, dsmall_ssd, cosm, sinm_neg)
    dcq, small['q_norm'] = matmul(
        [(dq, w['w_uq'])], 'nt', F32, "uq_bwd_a", post=_latent_norm_bwd_post, rows=[(proj, Q_LORA, OFF_CQ // Q_LORA)],
        fulls=[gain('q_norm')], outs=[BF16], accs=[((1, Q_LORA), F32)], full_n=True)
    big['w_uq'] = extract_uq(matmul([(sv['cqn'], dq)], 'tn', BF16, "uq_bwd_w"))
    dckv, small['kv_norm'] = matmul(
        [(dk, w['w_kn']), (dv, w['w_v'])], 'nt', F32, "ukv_bwd_a", post=_latent_norm_bwd_post,
        rows=[(proj, KV_LORA, OFF_CKV // KV_LORA)], fulls=[gain('kv_norm')], outs=[BF16], accs=[((1, KV_LORA), F32)],
        full_n=True)
    big['w_ukv'] = extract_ukv(matmul([(sv['ckvn'], dk)], 'tn', BF16, "kn_bwd_w"),
                               matmul([(sv['ckvn'], dv)], 'tn', BF16, "v_bwd_w"))
    wp = w['w_proj']
    xbc_half = lambda c: Opnd(dxbc, c0=c, shape=(dxbc.shape[0], 1024))
    wwin = lambda off, width: Opnd(wp, c0=off // width, shape=(D_MODEL, width))
    dx0, small['norm_mix'] = matmul(
        [(dz, wwin(OFF_Z, 1024)), (xbc_half(0), wwin(OFF_XBC, 1024)), (xbc_half(1), wwin(OFF_XBC + 1024, 1024)),
         (dcq, wwin(OFF_CQ, Q_LORA)), (dsmall, wwin(OFF_SMALL, LANES)), (dckv, wwin(OFF_CKV, KV_LORA))],
        'nt', F32, "proj_bwd_a", post=_norm_bwd_post, rows=[sv['x0'], dx1], fulls=[gain('norm_mix')], accs=gacc,
        full_n=True, tm_cap=256)
    h1 = sv['h1']
    big['w_in'] = extract_proj(
        matmul([(h1, dz)], 'tn', BF16, "proj_z_bwd_w"), matmul([(h1, dxbc)], 'tn', BF16, "proj_xbc_bwd_w"),
        matmul([(h1, dcq)], 'tn', BF16, "proj_cq_bwd_w"), matmul([(h1, dsmall)], 'tn', BF16, "proj_small_bwd_w"),
        matmul([(h1, dckv)], 'tn', BF16, "proj_ckv_bwd_w"))
    return dx0, on_grads(l, 'b', big), small


def _small_row(small, final=None):
    pt = small['ptile']
    parts = []
    for n, wd in SMALL_SEGS:
        if n in ('dt_bias', 'a_log', 'd_skip'):
            parts.append(pt[('dt_bias', 'a_log', 'd_skip').index(n)][None, :])
        elif n in SMALL_SHARDED:
            parts.append(small[n].reshape(1, wd))
        elif n == 'final_norm':
            parts.append(final if final is not None else jnp.zeros((1, wd), F32))
        else:
            parts.append(small[n])
    return jnp.concatenate(parts, axis=1)


def _rope_tables(positions):
    inv_freq = 1.0 / (ROPE_THETA ** (jnp.arange(0, QK_ROPE, 2, dtype=F32) / QK_ROPE))
    ang = positions.astype(F32)[:, None] * inv_freq
    cos, sin = jnp.cos(ang), jnp.sin(ang)
    s = positions.shape[0]
    pad = jnp.zeros((s, LANES - ROPE_LANE0 - QK_ROPE), F32)
    cosm = jnp.concatenate([jnp.ones((s, ROPE_LANE0), F32), cos, cos, pad], axis=1)
    sinm = jnp.concatenate([jnp.zeros((s, ROPE_LANE0), F32), -sin, sin, pad], axis=1)
    return cosm, sinm


def _small_views(rep, conv_full):
    sm = {n: rep[n].reshape(DEPTH, 1, -1) for n in ('norm_mix', 'ssm_norm', 'attn_out_norm', 'norm_mem_q',
                                                    'norm_mem_kv', 'norm_ffn', 'q_norm', 'kv_norm', 'ssm_conv_b',
                                                    'ffn_conv_b')}
    sm.update(conv_full)
    rows = jnp.stack([rep['dt_bias'], rep['a_log'], rep['d_skip']], axis=1)
    sm['ptile'] = jnp.pad(rows, ((0, 0), (0, 8 - 3), (0, LANES - SSM_HEADS)))
    return sm


def local_step(x, mem, positions, target, sm, final_norm, weights_of, on_grads):
    cosm, sinm = _rope_tables(positions)
    sinm_neg = -sinm
    saved, ws = [], []
    h, h1 = x, None
    for l in range(DEPTH):
        w, tie = weights_of(l, h)
        ws.append(w)
        h, h1, sv = layer_fwd(h, h1, mem, cosm, sinm, w, sm, l, tie=tie)
        saved.append(sv)
    dx, dfinal, lossv = loss_head(h, (final_norm.reshape(1, 1, -1), 0), target)
    rows = [None] * DEPTH
    tie = None
    for l in reversed(range(DEPTH)):
        dx, tie, small = layer_bwd(dx, mem, cosm, sinm_neg, ws[l], sm, l, saved[l], on_grads, tie=tie)
        rows[l] = _small_row(small, dfinal if l == 0 else None)
    return lossv[0, 0], dx, jnp.concatenate(rows, axis=0)


def kernel(x, mem, positions, norm_mix, w_in, ssm_conv_w, ssm_conv_b, dt_bias, a_log, d_skip, ssm_norm, q_norm, w_uq, kv_norm, w_ukv, attn_out_norm, w_out, norm_mem_q, norm_mem_kv, w_mq, w_mk, w_mv, w_mo, norm_ffn, w_up, ffn_conv_w, ffn_conv_b, w_down, final_norm, loss_target, m_norm_mix, m_w_in, m_ssm_conv_w, m_ssm_conv_b, m_dt_bias, m_a_log, m_d_skip, m_ssm_norm, m_q_norm, m_w_uq, m_kv_norm, m_w_ukv, m_attn_out_norm, m_w_out, m_norm_mem_q, m_norm_mem_kv, m_w_mq, m_w_mk, m_w_mv, m_w_mo, m_norm_ffn, m_w_up, m_ffn_conv_w, m_ffn_conv_b, m_w_down, m_final_norm, v_norm_mix, v_w_in, v_ssm_conv_w, v_ssm_conv_b, v_dt_bias, v_a_log, v_d_skip, v_ssm_norm, v_q_norm, v_w_uq, v_kv_norm, v_w_ukv, v_attn_out_norm, v_w_out, v_norm_mem_q, v_norm_mem_kv, v_w_mq, v_w_mk, v_w_mv, v_w_mo, v_norm_ffn, v_w_up, v_ffn_conv_w, v_ffn_conv_b, v_w_down, v_final_norm):
    args = locals()
    wts = {n: args[n] for n in WEIGHT_NAMES}
    ms = {n: args['m_' + n] for n in WEIGHT_NAMES}
    vs = {n: args['v_' + n] for n in WEIGHT_NAMES}

    st = dict(srcs={n: wts[n].astype(BF16) for n in BIG_NAMES}, exchanges=[],
              lands={n: lax.empty((DEPTH, N_DEV) + BIG[n], BF16) for n in BIG_NAMES})
    first = LATE_GRADS + ('w_out',)
    rest = tuple(n for n in BIG_NAMES if n not in first)
    got = all_gather_blocks([st['srcs'][n] for n in first] + [wts[n] for n in SMALL_SHARDED],
                            first_only=tuple(range(len(first))))
    conv_full = {}
    for n, g in zip(SMALL_SHARDED, got[len(first):]):
        taps, per, full = SMALL_SHARDED[n]
        conv_full[n] = jnp.moveaxis(g, 1, 2).reshape(DEPTH, taps, full)
    sm = _small_views(wts, conv_full)

    def start(names, l, tag, after=None):
        send_sems, recv_sems, thru, lands, tie = gather_start([st['srcs'][n] for n in names], l, tag, after)
        st['srcs'].update(zip(names, thru))
        return (names, l, tag, send_sems, recv_sems, lands), tie

    def pass_on(handle, after):
        names, l, tag, send_sems, recv_sems, lands = handle
        thru, lands = gather_wait(l, tag, send_sems, recv_sems, [st['srcs'][n] for n in names], lands, after)
        st['srcs'].update(zip(names, thru))
        send_sems, recv_sems, lands, tie = gather_pass_start(lands, l, tag)
        return (names, l, tag, send_sems, recv_sems, lands), tie

    def finish(handle, after):
        names, l, tag, send_sems, recv_sems, lands = handle
        return _layer_weights(dict(zip(names, gather_pass_wait(l, tag, send_sems, recv_sems, lands, after))))

    later, _ = start(rest, 0, "r", after=got[0])

    def weights_of(l, h):
        if l == 0:
            w = _layer_weights(dict(zip(first, got[:len(first)])))
            w['later'] = lambda after: finish(pass_on(later, after)[0], after)
        else:
            w = finish(st['next'], h)
        tie = None
        if l + 1 < DEPTH:
            st['next'], tie = start(BIG_NAMES, l + 1, "")

            def prefetch(after):
                st['next'], tie2 = pass_on(st['next'], after)
                return tie2

            w['prefetch'] = prefetch
        return w, tie

    def on_grads(l, tag, big, after=None):
        if (l, tag) == (0, 'b') and after is None:
            st['held'] = big
            return None
        names = list(big)
        send_sems, recv_sems, thru, lands, tie = grad_exchange_start(
            [big[n] for n in names], [st['lands'][n] for n in names], l, tag, after)
        st['lands'].update(zip(names, lands))
        st['exchanges'].append((l, tag, names, send_sems, recv_sems, thru))
        return tie

    loss_local, dx, small_rows = local_step(x[0], mem[0], positions[0], loss_target[0], sm, final_norm, weights_of,
                                            on_grads)
    outs = [{}, {}, {}, {}]

    sg_send, sg_recv, sg_src, sg_land, tok = small_gather_start(small_rows)
    tie = on_grads(0, 'b', st['held'], after=tok)

    def wait(exchange, after):
        l, tag, names, send_sems, recv_sems, thru = exchange
        _, lands = grad_exchange_wait(l, tag, send_sems, recv_sems, thru, [st['lands'][n] for n in names], after)
        st['lands'].update(zip(names, lands))

    def update(names, tie):
        for n in names:
            res_n = adamw_big(st['lands'][n], wts[n], ms[n], vs[n], "adamw_" + n, tie)
            tie = res_n[0]
            for k in range(4):
                outs[k][n] = res_n[k]
        return tie

    for exchange in st['exchanges'][:-1]:
        wait(exchange, tie)
    tie = update(EARLY_GRADS, tie)

    small_all = small_gather_wait(sg_send, sg_recv, sg_src, sg_land, tie)
    view = lambda d: {n: d[n].reshape(SMALL_VIEW[n]) for n in SMALL_NAMES}
    res = adamw_small(small_all, view(wts), view(ms), view(vs), tie)
    for k in range(4):
        for n in SMALL_NAMES:
            outs[k][n] = res[k][n].reshape(wts[n].shape)

    wait(st['exchanges'][-1], res[0]['final_norm'])
    update(LATE_GRADS, res[0]['final_norm'])

    loss = lax.psum(loss_local, ("x", "y", "c"))
    return (loss, dx[None], *[outs[0][n] for n in WEIGHT_NAMES], *[outs[1][n] for n in WEIGHT_NAMES],
            *[outs[2][n] for n in WEIGHT_NAMES], *[outs[3][n] for n in WEIGHT_NAMES])
```

```python
import functools
import math
from typing import Any, NamedTuple, Optional

import jax
import jax.numpy as jnp
from jax import lax
from jax.experimental import pallas as pl
from jax.experimental.pallas import tpu as pltpu

F32 = jnp.float32
BF16 = jnp.bfloat16

D_MODEL = 1024
DEPTH = 4
MEM_LEN = 256
EPS = 1e-6
SSM_HEADS = 16
SSM_HEAD_DIM = 64
D_SSM = 1024
SSM_GROUPS = 4
SSM_STATE = 128
SSM_CONV = 4
SSM_CHUNK = 128
CONV_CH = 2048
MLA_HEADS = 16
QK_NOPE = 64
QK_ROPE = 32
V_DIM = 64
Q_LORA = 384
KV_LORA = 256
ROPE_THETA = 10000.0
MEM_HEADS = 4
MEM_HEAD_DIM = 256
D_FF = 2816
FFN_CONV = 3
D_IN = 3760
D_MIX = 2048
ADAM_LR = 0.001
ADAM_B1 = 0.9
ADAM_B2 = 0.999
ADAM_EPS = 1e-08
ADAM_WD = 0.01
ADAM_STEP = 10

N_DEV = 8
N_CHIP = 4
LANES = 128
HEAD_PAD = 128
PROJ_W = 3840
OFF_Z, OFF_XBC, OFF_CQ, OFF_SMALL, OFF_CKV = 0, 1024, 3072, 3456, 3584
ROPE_LANE0 = 64
VMEM_LIMIT = 56 * 1024 * 1024
MM_BLOCK_BYTES = 4 * 1024 * 1024
WEIGHT_NAMES = ['norm_mix', 'w_in', 'ssm_conv_w', 'ssm_conv_b', 'dt_bias', 'a_log', 'd_skip', 'ssm_norm', 'q_norm',
                'w_uq', 'kv_norm', 'w_ukv', 'attn_out_norm', 'w_out', 'norm_mem_q', 'norm_mem_kv', 'w_mq', 'w_mk',
                'w_mv', 'w_mo', 'norm_ffn', 'w_up', 'ffn_conv_w', 'ffn_conv_b', 'w_down', 'final_norm']
BIG = {'w_in': (1024, 470), 'w_uq': (384, 192), 'w_ukv': (256, 256), 'w_up': (1024, 704), 'w_out': (256, 1024),
       'w_mq': (128, 1024), 'w_mk': (128, 1024), 'w_mv': (128, 1024), 'w_mo': (128, 1024), 'w_down': (352, 1024)}
BIG_NAMES = list(BIG)
PROJ_SEGS = [(0, 1024, OFF_Z), (1024, 3072, OFF_XBC), (3072, 3088, OFF_SMALL), (3088, 3472, OFF_CQ),
             (3472, 3728, OFF_CKV), (3728, 3760, OFF_SMALL + ROPE_LANE0)]
SMALL_SEGS = [('norm_mix', 1024), ('ssm_norm', 1024), ('attn_out_norm', 1024), ('norm_mem_q', 1024),
              ('norm_mem_kv', 1024), ('norm_ffn', 1024), ('q_norm', 384), ('kv_norm', 256), ('ssm_conv_b', 2048),
              ('ffn_conv_b', 5632), ('dt_bias', 128), ('a_log', 128), ('d_skip', 128),
              ('ssm_conv_w', SSM_CONV * CONV_CH), ('ffn_conv_w', FFN_CONV * 2 * D_FF), ('final_norm', 1024)]
SMALL_OFF = {}
_o = 0
for _n, _w in SMALL_SEGS:
    SMALL_OFF[_n] = _o
    _o += _w
SMALL_W = _o


def _params(**kw):
    return pltpu.CompilerParams(vmem_limit_bytes=VMEM_LIMIT, **kw)


def _pick(n, cap):
    if n <= cap:
        return n
    best = None
    for t in range(LANES, cap + 1, LANES):
        if n % t == 0:
            best = t
    assert best is not None, (n, cap)
    return best


def _row_tile(a, cap=256):
    if a <= cap:
        return a
    best = None
    for t in range(16, cap + 1, 16):
        if a % t == 0:
            best = t
    assert best is not None, (a, cap)
    return best


class Opnd(NamedTuple):
    arr: Any
    lead: Optional[int] = None
    r0: int = 0
    c0: int = 0
    shape: Optional[tuple] = None


def _opnd(x):
    return x if isinstance(x, Opnd) else Opnd(x)


def _lshape(o):
    return tuple(o.shape) if o.shape is not None else tuple(o.arr.shape[-2:])


def _spec(o, br, bc, bi, bj):
    rr, cc = _lshape(o)
    assert rr % br == 0 and cc % bc == 0, (rr, cc, br, bc)
    ro, co = o.r0 * (rr // br), o.c0 * (cc // bc)
    if o.lead is None:
        return pl.BlockSpec((br, bc), lambda i, j: (ro + bi(i, j), co + bj(i, j)))
    return pl.BlockSpec((None, br, bc), lambda i, j: (o.lead, ro + bi(i, j), co + bj(i, j)))


_DIMS = {'nn': (((1,), (0,)), ((), ())), 'nt': (((1,), (1,)), ((), ())), 'tn': (((0,), (0,)), ((), ()))}
_ROW = lambda i, j: i
_COL = lambda i, j: j
_ZERO = lambda i, j: 0


def matmul(pairs, mode, out_dtype, name, add=None, tie=None, post=None, rows=(), fulls=(), outs=None, full_n=False,
           accs=(), tm_cap=None, tn_fixed=None):
    pairs = [(_opnd(a), _opnd(b)) for a, b in pairs]
    a0, b0 = pairs[0]
    if mode == 'nn':
        m, n = _lshape(a0)[0], _lshape(b0)[1]
    elif mode == 'nt':
        m, n = _lshape(a0)[0], _lshape(b0)[0]
    else:
        m, n = _lshape(a0)[1], _lshape(b0)[1]
    isz = lambda o: jnp.dtype(o.arr.dtype).itemsize
    osz = jnp.dtype(out_dtype).itemsize
    cap = lambda budget, per: max(LANES, budget // per // LANES * LANES)
    if mode == 'tn':
        ktok = _lshape(a0)[0]
        tm = _pick(m, cap(3 * MM_BLOCK_BYTES // 2, ktok * isz(a0)))
        tn = _pick(n, cap(3 * MM_BLOCK_BYTES // 2, ktok * isz(b0)))
    else:
        tm = _pick(m, min(2048, cap(2 * MM_BLOCK_BYTES, sum(_lshape(a)[1] * isz(a) for a, _ in pairs))))
        tn = _pick(n, min(cap(3 * MM_BLOCK_BYTES // 2, sum(_lshape(a)[1] * isz(b) for a, b in pairs)),
                          cap(MM_BLOCK_BYTES, tm * osz), n // 2 if n >= 1024 else n))
        if full_n:
            tm, tn = _pick(m, min(tm, tm_cap or tm, cap(MM_BLOCK_BYTES // 2, n * osz))), n
        if tn_fixed is not None:
            tn = tn_fixed
    npairs = len(pairs)
    outs = list(outs) if outs is not None else [out_dtype]
    nadd = 1 if add is not None else 0
    nrows, nfulls = len(rows), len(fulls)

    def body(*refs):
        o_refs = refs[len(refs) - len(outs) - len(accs):]
        if accs:
            @pl.when(jnp.logical_and(pl.program_id(0) == 0, pl.program_id(1) == 0))
            def _():
                for r in o_refs[len(outs):]:
                    r[...] = jnp.zeros_like(r)

        acc = None
        for p in range(npairs):
            a = refs[2 * p][...].astype(BF16)
            b = refs[2 * p + 1][...].astype(BF16)
            d = lax.dot_general(a, b, _DIMS[mode], preferred_element_type=F32)
            acc = d if acc is None else acc + d
        if add is not None:
            acc = acc + refs[2 * npairs][...].astype(F32)
        if post is None:
            o_refs[0][...] = acc.astype(out_dtype)
        else:
            x0 = 2 * npairs + nadd
            post(acc, [r[...] for r in refs[x0:x0 + nrows]], [r[...] for r in refs[x0 + nrows:x0 + nrows + nfulls]], o_refs)

    rows = [r if isinstance(r, tuple) else (r, r.shape[1], 0) for r in rows]
    tie_specs = [pl.BlockSpec((tm, wd), lambda i, j, cb=cb: (i, cb)) for _, wd, cb in rows]
    tie_specs += [pl.BlockSpec((None,) + f.shape[1:], lambda i, j, ld=ld, nd=f.ndim - 1: (ld,) + (0,) * nd) for f, ld in fulls]
    tie_args = [r for r, _, _ in rows] + [f for f, _ in fulls]
    if tie is not None:
        tie_specs.append(pl.BlockSpec(memory_space=pl.ANY))
        tie_args.append(tie)

    in_specs, args = [], []
    for a, b in pairs:
        if mode == 'nn':
            k = _lshape(a)[1]
            in_specs += [_spec(a, tm, k, _ROW, _ZERO), _spec(b, k, tn, _ZERO, _COL)]
        elif mode == 'nt':
            k = _lshape(a)[1]
            in_specs += [_spec(a, tm, k, _ROW, _ZERO), _spec(b, tn, k, _COL, _ZERO)]
        else:
            k = _lshape(a)[0]
            in_specs += [_spec(a, k, tm, _ZERO, _ROW), _spec(b, k, tn, _ZERO, _COL)]
        args += [a.arr, b.arr]
    if add is not None:
        in_specs.append(pl.BlockSpec((tm, tn), lambda i, j: (i, j)))
        args.append(add)
    res = pl.pallas_call(
        body, name=name, grid=(m // tm, n // tn), in_specs=in_specs + tie_specs,
        out_specs=[pl.BlockSpec((tm, o[0]), lambda i, j: (i, 0)) if isinstance(o, tuple) else
                   pl.BlockSpec((tm, tn), lambda i, j: (i, j)) for o in outs] +
                  [pl.BlockSpec(shp, lambda i, j, nd=len(shp): (0,) * nd) for shp, _ in accs],
        out_shape=[jax.ShapeDtypeStruct((m, o[0]), o[1]) if isinstance(o, tuple) else jax.ShapeDtypeStruct((m, n), o)
                   for o in outs] + [jax.ShapeDtypeStruct(shp, dt) for shp, dt in accs],
        compiler_params=_params(dimension_semantics=("arbitrary", "arbitrary")),
    )(*args, *tie_args)
    return res[0] if len(outs) + len(accs) == 1 else res


def rowwise(fn, rows, fulls, outs, accs, name, tm=256, into=None, tie=None):
    s = rows[0][0].shape[0]
    nrow, nfull, nout, nacc = len(rows), len(fulls), len(outs), len(accs)
    nin = nrow + nfull

    def body(*refs):
        ins = [r[...] for r in refs[:nin]]
        res = fn(*ins)
        if not isinstance(res, (tuple, list)):
            res = (res,)
        orefs = refs[nin + (1 if into is not None else 0) + (1 if tie is not None else 0):]
        for k in range(nout):
            orefs[k][...] = res[k].astype(orefs[k].dtype)
        if nacc:
            @pl.when(pl.program_id(0) == 0)
            def _():
                for k in range(nacc):
                    orefs[nout + k][...] = jnp.zeros_like(orefs[nout + k])

            for k in range(nacc):
                orefs[nout + k][...] += res[nout + k].astype(orefs[nout + k].dtype)

    in_specs = [pl.BlockSpec((tm, w), lambda i, cb=cb: (i, cb)) for _, w, cb in rows]
    in_specs += [pl.BlockSpec((None,) + f.shape[1:], lambda i, ld=ld, nd=f.ndim - 1: (ld,) + (0,) * nd) for f, ld in fulls]
    args = [r[0] for r in rows] + [f for f, _ in fulls]
    aliases = {}
    if into is not None:
        in_specs.append(pl.BlockSpec(memory_space=pl.ANY))
        args.append(into[0])
        aliases = {nin: into[1]}
    if tie is not None:
        in_specs.append(pl.BlockSpec(memory_space=pl.ANY))
        args.append(tie)
    out_specs, out_shape = [], []
    for o in outs:
        w, dt = o[0], o[1]
        total, cb = (o[2], o[3]) if len(o) == 4 else (w, 0)
        out_specs.append(pl.BlockSpec((tm, w), lambda i, cb=cb: (i, cb)))
        out_shape.append(jax.ShapeDtypeStruct((s, total), dt))
    for shp, dt in accs:
        out_specs.append(pl.BlockSpec(shp, lambda i, nd=len(shp): (0,) * nd))
        out_shape.append(jax.ShapeDtypeStruct(shp, dt))
    return pl.pallas_call(
        body, name=name, grid=(s // tm,), in_specs=in_specs, out_specs=out_specs, out_shape=out_shape,
        input_output_aliases=aliases, compiler_params=_params(dimension_semantics=("arbitrary",)),
    )(*args)


def _rms(x, g):
    xf = x.astype(F32)
    var = jnp.mean(xf * xf, axis=-1, keepdims=True)
    return xf * lax.rsqrt(var + EPS) * g


def rmsnorm_fwd(x, g, name, width=None, colblock=0, out=None, into=None, tie=None):
    w = width or x.shape[1]
    return rowwise(lambda xt, gt: _rms(xt, gt), [(x, w, colblock)], [g], [out or (w, BF16)], [], name, into=into,
                   tie=tie)[0]


def rmsnorm_bwd(x, g, dh, name, resid=None, width=None, colblock=0, dh_colblock=0, dx_dtype=F32):
    w = width or x.shape[1]

    def fn(xt, dht, *rest):
        gt = rest[-1]
        _, vjp = jax.vjp(_rms, xt.astype(F32), gt)
        dx, dg = vjp(dht.astype(F32))
        if resid is not None:
            dx = dx + rest[0]
        return dx, dg

    rows = [(x, w, colblock), (dh, w, dh_colblock)] + ([(resid, w, 0)] if resid is not None else [])
    return rowwise(fn, rows, [g], [(w, dx_dtype)], [((1, w), F32)], name)


CONV_R = 64
HALO = 8


def _ext_rows(ref, i, nchunk, above, below):
    r0 = pl.multiple_of(i * CONV_R, CONV_R)
    s = ref.shape[0]
    parts = []
    if above:
        top = ref[pl.ds(pl.multiple_of(jnp.maximum(r0 - HALO, 0), HALO), HALO), :].astype(F32)
        parts.append(jnp.where(i > 0, top, 0.0))
    parts.append(ref[pl.ds(r0, CONV_R), :].astype(F32))
    if below:
        tile = 2 * HALO if ref.dtype == BF16 else HALO
        bot = ref[pl.ds(pl.multiple_of(jnp.minimum(r0 + CONV_R, s - tile), tile), tile), :].astype(F32)[0:HALO]
        parts.append(jnp.where(i < nchunk - 1, bot, 0.0))
    return jnp.concatenate(parts, axis=0)


def _conv_ext(ext, w_ref, b_ref, kw):
    y = ext[HALO:] * w_ref[kw - 1:kw, :] + b_ref[...]
    for k in range(1, kw):
        y = y + pltpu.roll(ext, k, 0)[HALO:] * w_ref[kw - 1 - k:kw - k, :]
    return y


def _conv_t_ext(d, w_ref, kw):
    n = d.shape[0]
    y = d[:n - HALO] * w_ref[kw - 1:kw, :]
    for k in range(1, kw):
        y = y + pltpu.roll(d, n - k, 0)[:n - HALO] * w_ref[kw - 1 - k:kw - k, :]
    return y


def _conv_wgrad(dp, ext, kw):
    out = [jnp.sum(dp, axis=0, keepdims=True), jnp.sum(dp * ext[HALO:HALO + CONV_R], axis=0, keepdims=True)]
    for k in range(1, kw):
        out.append(jnp.sum(dp * pltpu.roll(ext, k, 0)[HALO:HALO + CONV_R], axis=0, keepdims=True))
    return out


def _store_wgrad(res, dw_ref, db_ref, kw):
    db_ref[...] = res[0]
    for k in range(kw):
        dw_ref[kw - 1 - k:kw - k, :] = res[1 + k]


def _silu(x):
    return x * jax.nn.sigmoid(x)


def _dsilu(x):
    s = jax.nn.sigmoid(x)
    return s * (1.0 + x * (1.0 - s))


SSM_TC = 256


def ssm_conv_fwd(proj, cw, cb, l):
    s = proj.shape[0]
    off = OFF_XBC // SSM_TC

    def body(u_ref, w_ref, b_ref, o_ref):
        nchunk = s // CONV_R

        def step(i, carry):
            ext = _ext_rows(u_ref, i, nchunk, True, False)
            o_ref[pl.ds(pl.multiple_of(i * CONV_R, CONV_R), CONV_R), :] = _silu(_conv_ext(ext, w_ref, b_ref, SSM_CONV))
            return carry

        lax.fori_loop(0, nchunk, step, 0)

    return pl.pallas_call(
        body, name="ssm_conv_fwd", grid=(CONV_CH // SSM_TC,),
        in_specs=[pl.BlockSpec((s, SSM_TC), lambda j: (0, off + j)),
                  pl.BlockSpec((None, SSM_CONV, SSM_TC), lambda j: (l, 0, j)),
                  pl.BlockSpec((None, 1, SSM_TC), lambda j: (l, 0, j))],
        out_specs=pl.BlockSpec((s, SSM_TC), lambda j: (0, j)),
        out_shape=jax.ShapeDtypeStruct((s, CONV_CH), F32),
        compiler_params=_params(dimension_semantics=("arbitrary",)),
    )(proj, cw, cb)


def ssm_conv_bwd(proj, cw, cb, l, dact):
    s = proj.shape[0]
    off = OFF_XBC // SSM_TC

    def body(u_ref, w_ref, b_ref, d_ref, du_ref, dw_ref, db_ref):
        nchunk = s // CONV_R

        def step(i, carry):
            ext = _ext_rows(u_ref, i, nchunk, True, True)
            dpre = _ext_rows(d_ref, i, nchunk, False, True) * _dsilu(_conv_ext(ext, w_ref, b_ref, SSM_CONV))
            du_ref[pl.ds(pl.multiple_of(i * CONV_R, CONV_R), CONV_R), :] = _conv_t_ext(dpre, w_ref, SSM_CONV).astype(du_ref.dtype)
            return tuple(c + g for c, g in zip(carry, _conv_wgrad(dpre[:CONV_R], ext, SSM_CONV)))

        zero = jnp.zeros((1, SSM_TC), F32)
        _store_wgrad(lax.fori_loop(0, nchunk, step, (zero,) * (SSM_CONV + 1)), dw_ref, db_ref, SSM_CONV)

    return pl.pallas_call(
        body, name="ssm_conv_bwd", grid=(CONV_CH // SSM_TC,),
        in_specs=[pl.BlockSpec((s, SSM_TC), lambda j: (0, off + j)),
                  pl.BlockSpec((None, SSM_CONV, SSM_TC), lambda j: (l, 0, j)),
                  pl.BlockSpec((None, 1, SSM_TC), lambda j: (l, 0, j)), pl.BlockSpec((s, SSM_TC), lambda j: (0, j))],
        out_specs=[pl.BlockSpec((s, SSM_TC), lambda j: (0, j)), pl.BlockSpec((SSM_CONV, SSM_TC), lambda j: (0, j)),
                   pl.BlockSpec((1, SSM_TC), lambda j: (0, j))],
        out_shape=[jax.ShapeDtypeStruct((s, CONV_CH), BF16), jax.ShapeDtypeStruct((SSM_CONV, CONV_CH), F32),
                   jax.ShapeDtypeStruct((1, CONV_CH), F32)],
        compiler_params=_params(dimension_semantics=("arbitrary",)),
    )(proj, cw, cb, dact)


FFN_TC = 256
FFN_NT = D_FF // FFN_TC


def _ffn_specs(s, l):
    blk = pl.BlockSpec((s, FFN_TC), lambda j: (0, j))
    wg = pl.BlockSpec((None, FFN_CONV, FFN_TC), lambda j: (l, 0, j))
    wv = pl.BlockSpec((None, FFN_CONV, FFN_TC), lambda j: (l, 0, FFN_NT + j))
    bg = pl.BlockSpec((None, 1, FFN_TC), lambda j: (l, 0, j))
    bv = pl.BlockSpec((None, 1, FFN_TC), lambda j: (l, 0, FFN_NT + j))
    return blk, wg, wv, bg, bv


def ffn_act_fwd(ug, uv, cw, cb, l):
    s = ug.shape[0]

    def body(g_ref, v_ref, wg_ref, wv_ref, bg_ref, bv_ref, o_ref):
        nchunk = s // CONV_R

        def step(i, carry):
            cg = _conv_ext(_ext_rows(g_ref, i, nchunk, True, False), wg_ref, bg_ref, FFN_CONV)
            cv = _conv_ext(_ext_rows(v_ref, i, nchunk, True, False), wv_ref, bv_ref, FFN_CONV)
            o_ref[pl.ds(pl.multiple_of(i * CONV_R, CONV_R), CONV_R), :] = (_silu(cg) * cv).astype(o_ref.dtype)
            return carry

        lax.fori_loop(0, nchunk, step, 0)

    blk, wg, wv, bg, bv = _ffn_specs(s, l)
    return pl.pallas_call(
        body, name="ffn_act_fwd", grid=(FFN_NT,), in_specs=[blk, blk, wg, wv, bg, bv],
        out_specs=blk, out_shape=jax.ShapeDtypeStruct((s, D_FF), BF16),
        compiler_params=_params(dimension_semantics=("arbitrary",)),
    )(ug, uv, cw, cw, cb, cb)


def ffn_act_bwd(ug, uv, cw, cb, l, da):
    s = ug.shape[0]

    def body(g_ref, v_ref, wg_ref, wv_ref, bg_ref, bv_ref, da_ref, dg_ref, dv_ref, dwg_ref, dwv_ref, dbg_ref, dbv_ref):
        nchunk = s // CONV_R

        def step(i, carry):
            rows = pl.ds(pl.multiple_of(i * CONV_R, CONV_R), CONV_R)
            eg = _ext_rows(g_ref, i, nchunk, True, True)
            ev = _ext_rows(v_ref, i, nchunk, True, True)
            cg = _conv_ext(eg, wg_ref, bg_ref, FFN_CONV)
            cv = _conv_ext(ev, wv_ref, bv_ref, FFN_CONV)
            da_t = _ext_rows(da_ref, i, nchunk, False, True)
            sg = jax.nn.sigmoid(cg)
            dcg = da_t * cv * (sg * (1.0 + cg * (1.0 - sg)))
            dcv = da_t * (cg * sg)
            dg_ref[rows, :] = _conv_t_ext(dcg, wg_ref, FFN_CONV).astype(dg_ref.dtype)
            dv_ref[rows, :] = _conv_t_ext(dcv, wv_ref, FFN_CONV).astype(dv_ref.dtype)
            grads = _conv_wgrad(dcg[:CONV_R], eg, FFN_CONV) + _conv_wgrad(dcv[:CONV_R], ev, FFN_CONV)
            return tuple(c + g for c, g in zip(carry, grads))

        zero = jnp.zeros((1, FFN_TC), F32)
        res = lax.fori_loop(0, nchunk, step, (zero,) * (2 * FFN_CONV + 2))
        _store_wgrad(res[:FFN_CONV + 1], dwg_ref, dbg_ref, FFN_CONV)
        _store_wgrad(res[FFN_CONV + 1:], dwv_ref, dbv_ref, FFN_CONV)

    blk, wg, wv, bg, bv = _ffn_specs(s, l)
    wblk = pl.BlockSpec((FFN_CONV, FFN_TC), lambda j: (0, j))
    bblk = pl.BlockSpec((1, FFN_TC), lambda j: (0, j))
    return pl.pallas_call(
        body, name="ffn_act_bwd", grid=(FFN_NT,), in_specs=[blk, blk, wg, wv, bg, bv, blk],
        out_specs=[blk, blk, wblk, wblk, bblk, bblk],
        out_shape=[jax.ShapeDtypeStruct((s, D_FF), BF16), jax.ShapeDtypeStruct((s, D_FF), BF16),
                   jax.ShapeDtypeStruct((FFN_CONV, D_FF), F32), jax.ShapeDtypeStruct((FFN_CONV, D_FF), F32),
                   jax.ShapeDtypeStruct((1, D_FF), F32), jax.ShapeDtypeStruct((1, D_FF), F32)],
        compiler_params=_params(dimension_semantics=("arbitrary",)),
    )(ug, uv, cw, cw, cb, cb, da)


def _dot(a, b, mode):
    return lax.dot_general(a.astype(BF16), b.astype(BF16), _DIMS[mode], preferred_element_type=F32)


@jax.custom_vjp
def mm_nn(a, b):
    return _dot(a, b, 'nn')


@jax.custom_vjp
def mm_nt(a, b):
    return _dot(a, b, 'nt')


@jax.custom_vjp
def mm_tn(a, b):
    return _dot(a, b, 'tn')


mm_nn.defvjp(lambda a, b: (_dot(a, b, 'nn'), (a, b)), lambda r, g: (_dot(g, r[1], 'nt'), _dot(r[0], g, 'tn')))
mm_nt.defvjp(lambda a, b: (_dot(a, b, 'nt'), (a, b)), lambda r, g: (_dot(g, r[1], 'nn'), _dot(g, r[0], 'tn')))
mm_tn.defvjp(lambda a, b: (_dot(a, b, 'tn'), (a, b)), lambda r, g: (_dot(r[1], g, 'nt'), _dot(r[0], g, 'nn')))


def _tri(n, lower):
    r = lax.broadcasted_iota(jnp.int32, (n, n), 0)
    c = lax.broadcasted_iota(jnp.int32, (n, n), 1)
    return jnp.where((r >= c) if lower else (r <= c), 1.0, 0.0).astype(F32)


def _tri_dot(a, lower):
    return jnp.dot(_tri(a.shape[0], lower), a, precision=lax.Precision.HIGHEST, preferred_element_type=F32)


@jax.custom_vjp
def _cumsum_rows(a):
    return _tri_dot(a, True)


_cumsum_rows.defvjp(lambda a: (_tri_dot(a, True), None), lambda _, g: (_tri_dot(g, False),))


def _softplus(x):
    return jnp.maximum(x, 0.0) + jnp.log(1.0 + jnp.exp(-jnp.abs(x)))


def _ssd_chunk(xs, bs, cs, small, dtb, alog, dsk, prev):
    ln = small.shape[0]
    lane = lax.broadcasted_iota(jnp.int32, (ln, LANES), 1)
    lane1 = lax.broadcasted_iota(jnp.int32, (1, LANES), 1)
    sub = lax.broadcasted_iota(jnp.int32, (LANES, ln), 0)
    rowi = lax.broadcasted_iota(jnp.int32, (ln, LANES), 0)
    tril = lax.broadcasted_iota(jnp.int32, (ln, ln), 0) >= lax.broadcasted_iota(jnp.int32, (ln, ln), 1)
    first = lane < SSM_HEAD_DIM
    first1 = lane1 < SSM_HEAD_DIM

    dt = _softplus(small + dtb)
    acs = _cumsum_rows(dt * (-jnp.exp(alog)))
    acs_t = acs.T
    last = jnp.sum(jnp.where(rowi == ln - 1, acs, 0.0), axis=0, keepdims=True)

    def col(a, h):
        return jnp.sum(jnp.where(lane == h, a, 0.0), axis=1, keepdims=True)

    def one(a, h):
        return jnp.sum(jnp.where(lane1 == h, a, 0.0), axis=1, keepdims=True)

    def rowv(at, h):
        return jnp.sum(jnp.where(sub == h, at, 0.0), axis=0, keepdims=True)

    cb = [mm_nt(cs[g], bs[g]) for g in range(SSM_GROUPS)]
    ys, news = [], []
    for j in range(SSM_HEADS // 2):
        g = j // 2
        h0, h1 = 2 * j, 2 * j + 1
        xd = xs[j] * jnp.where(first, col(dt, h0), col(dt, h1))
        yd, st, ea, cd = None, None, [], []
        for h, xdh in ((h0, jnp.where(first, xd, 0.0)), (h1, jnp.where(first, 0.0, xd))):
            ac = col(acs, h)
            la = one(last, h)
            lmat = jnp.exp(jnp.where(tril, ac - rowv(acs_t, h), -jnp.inf))
            yh = mm_nn(cb[g] * lmat, xdh)
            sh = mm_tn(bs[g] * jnp.exp(la - ac), xdh)
            yd = yh if yd is None else yd + yh
            st = sh if st is None else st + sh
            ea.append(jnp.exp(ac))
            cd.append(jnp.exp(la))
        yoff = mm_nn(cs[g], prev[j]) * jnp.where(first, ea[0], ea[1])
        ys.append(yd + yoff + xs[j] * jnp.where(first1, one(dsk, h0), one(dsk, h1)))
        news.append(prev[j] * jnp.where(first1, cd[0], cd[1]) + st)
    return ys, news


N_PAIR = SSM_HEADS // 2


def ssd_fwd(xbc, proj, ptile, gain, l):
    s = xbc.shape[0]
    nch = s // SSM_CHUNK

    def body(xbc_ref, small_ref, p_ref, z_ref, g_ref, y_ref, prev_ref, mix_ref, state_ref):
        @pl.when(pl.program_id(0) == 0)
        def _():
            state_ref[...] = jnp.zeros_like(state_ref)

        xs = [xbc_ref[:, LANES * j:LANES * (j + 1)] for j in range(N_PAIR)]
        bs = [xbc_ref[:, D_SSM + LANES * g:D_SSM + LANES * (g + 1)] for g in range(SSM_GROUPS)]
        cs = [xbc_ref[:, D_SSM + 512 + LANES * g:D_SSM + 512 + LANES * (g + 1)] for g in range(SSM_GROUPS)]
        prev = [state_ref[j] for j in range(N_PAIR)]
        ys, news = _ssd_chunk(xs, bs, cs, small_ref[...], p_ref[0:1, :], p_ref[1:2, :], p_ref[2:3, :], prev)
        gated, ssq = [], None
        for j in range(N_PAIR):
            y_ref[:, LANES * j:LANES * (j + 1)] = ys[j]
            prev_ref[0, j] = prev[j]
            state_ref[j] = news[j]
            t = ys[j] * _silu(z_ref[:, LANES * j:LANES * (j + 1)])
            sq = jnp.sum(t * t, axis=1, keepdims=True)
            gated.append(t)
            ssq = sq if ssq is None else ssq + sq
        inv = lax.rsqrt(ssq / D_SSM + EPS)
        for j in range(N_PAIR):
            mix_ref[:, LANES * j:LANES * (j + 1)] = (gated[j] * inv * g_ref[:, LANES * j:LANES * (j + 1)]).astype(mix_ref.dtype)

    return pl.pallas_call(
        body, name="ssd_fwd", grid=(nch,),
        in_specs=[pl.BlockSpec((SSM_CHUNK, CONV_CH), lambda c: (c, 0)),
                  pl.BlockSpec((SSM_CHUNK, LANES), lambda c: (c, OFF_SMALL // LANES)),
                  pl.BlockSpec((None, 8, LANES), lambda c: (l, 0, 0)),
                  pl.BlockSpec((SSM_CHUNK, D_SSM), lambda c: (c, OFF_Z // D_SSM)),
                  pl.BlockSpec((None, 1, D_SSM), lambda c: (l, 0, 0))],
        out_specs=[pl.BlockSpec((SSM_CHUNK, D_SSM), lambda c: (c, 0)),
                   pl.BlockSpec((1, N_PAIR, SSM_STATE, LANES), lambda c: (c, 0, 0, 0)),
                   pl.BlockSpec((SSM_CHUNK, D_SSM), lambda c: (c, 0))],
        out_shape=[jax.ShapeDtypeStruct((s, D_SSM), F32), jax.ShapeDtypeStruct((nch, N_PAIR, SSM_STATE, LANES), F32),
                   jax.ShapeDtypeStruct((s, D_MIX), BF16)],
        scratch_shapes=[pltpu.VMEM((N_PAIR, SSM_STATE, LANES), F32)],
        compiler_params=_params(dimension_semantics=("arbitrary",)),
    )(xbc, proj, ptile, proj, gain)


def ssd_bwd(xbc, proj, ptile, l, prevs, dy):
    s = xbc.shape[0]
    nch = s // SSM_CHUNK

    def body(xbc_ref, small_ref, p_ref, prev_ref, dy_ref, dxbc_ref, dsmall_ref, dp_ref, dstate_ref):
        @pl.when(pl.program_id(0) == 0)
        def _():
            dstate_ref[...] = jnp.zeros_like(dstate_ref)
            dp_ref[...] = jnp.zeros_like(dp_ref)

        xs = [xbc_ref[:, LANES * j:LANES * (j + 1)] for j in range(N_PAIR)]
        bs = [xbc_ref[:, D_SSM + LANES * g:D_SSM + LANES * (g + 1)] for g in range(SSM_GROUPS)]
        cs = [xbc_ref[:, D_SSM + 512 + LANES * g:D_SSM + 512 + LANES * (g + 1)] for g in range(SSM_GROUPS)]
        prev = [prev_ref[0, j] for j in range(N_PAIR)]
        dys = [dy_ref[:, LANES * j:LANES * (j + 1)] for j in range(N_PAIR)]
        dnew = [dstate_ref[j] for j in range(N_PAIR)]
        _, vjp = jax.vjp(_ssd_chunk, xs, bs, cs, small_ref[...], p_ref[0:1, :], p_ref[1:2, :], p_ref[2:3, :], prev)
        dxs, dbs, dcs, dsmall, ddtb, dalog, ddsk, dprev = vjp((dys, dnew))
        for j in range(N_PAIR):
            dxbc_ref[:, LANES * j:LANES * (j + 1)] = dxs[j]
            dstate_ref[j] = dprev[j]
        for g in range(SSM_GROUPS):
            dxbc_ref[:, D_SSM + LANES * g:D_SSM + LANES * (g + 1)] = dbs[g]
            dxbc_ref[:, D_SSM + 512 + LANES * g:D_SSM + 512 + LANES * (g + 1)] = dcs[g]
        dsmall_ref[...] = dsmall
        dp_ref[0:1, :] += ddtb
        dp_ref[1:2, :] += dalog
        dp_ref[2:3, :] += ddsk

    rev = lambda c: nch - 1 - c
    return pl.pallas_call(
        body, name="ssd_bwd", grid=(nch,),
        in_specs=[pl.BlockSpec((SSM_CHUNK, CONV_CH), lambda c: (rev(c), 0)),
                  pl.BlockSpec((SSM_CHUNK, LANES), lambda c: (rev(c), OFF_SMALL // LANES)),
                  pl.BlockSpec((None, 8, LANES), lambda c: (l, 0, 0)),
                  pl.BlockSpec((1, N_PAIR, SSM_STATE, LANES), lambda c: (rev(c), 0, 0, 0)),
                  pl.BlockSpec((SSM_CHUNK, D_SSM), lambda c: (rev(c), 0))],
        out_specs=[pl.BlockSpec((SSM_CHUNK, CONV_CH), lambda c: (rev(c), 0)),
                   pl.BlockSpec((SSM_CHUNK, LANES), lambda c: (rev(c), 0)),
                   pl.BlockSpec((8, LANES), lambda c: (0, 0))],
        out_shape=[jax.ShapeDtypeStruct((s, CONV_CH), F32), jax.ShapeDtypeStruct((s, LANES), F32),
                   jax.ShapeDtypeStruct((8, LANES), F32)],
        scratch_shapes=[pltpu.VMEM((N_PAIR, SSM_STATE, LANES), F32)],
        compiler_params=_params(dimension_semantics=("arbitrary",)),
    )(xbc, proj, ptile, prevs, dy)


ROPE_TM = 256


def _rope_tile(t, cosm, sinm):
    lane = lax.broadcasted_iota(jnp.int32, t.shape, 1)
    half = QK_ROPE // 2
    partner = jnp.where(lane < ROPE_LANE0 + half, pltpu.roll(t, LANES - half, 1), pltpu.roll(t, half, 1))
    return t * cosm + partner * sinm


def _in_rope(shape):
    lane = lax.broadcasted_iota(jnp.int32, shape, 1)
    return jnp.logical_and(lane >= ROPE_LANE0, lane < ROPE_LANE0 + QK_ROPE)


def build_k(kn, proj, cosm, sinm):
    s, w = kn.shape

    def body(k_ref, small_ref, c_ref, s_ref, o_ref):
        small = small_ref[...]
        inrope = _in_rope(small.shape)
        kpe = jnp.where(inrope, _rope_tile(jnp.where(inrope, small, 0.0), c_ref[...], s_ref[...]), 0.0)
        for h in range(MLA_HEADS):
            sl = slice(HEAD_PAD * h, HEAD_PAD * (h + 1))
            o_ref[:, sl] = (k_ref[:, sl].astype(F32) + kpe).astype(o_ref.dtype)

    row = pl.BlockSpec((ROPE_TM, w), lambda i: (i, 0))
    tab = pl.BlockSpec((ROPE_TM, LANES), lambda i: (i, 0))
    return pl.pallas_call(
        body, name="build_k", grid=(s // ROPE_TM,),
        in_specs=[row, pl.BlockSpec((ROPE_TM, LANES), lambda i: (i, OFF_SMALL // LANES)), tab, tab], out_specs=row,
        out_shape=jax.ShapeDtypeStruct((s, w), BF16), compiler_params=_params(dimension_semantics=("arbitrary",)),
    )(kn, proj, cosm, sinm)


def dsmall_bwd(dk, dsmall_ssd, cosm, sinm_neg):
    def fn(dkt, ds, c, sn):
        inrope = _in_rope(ds.shape)
        tot = dkt[:, 0:HEAD_PAD]
        for h in range(1, MLA_HEADS):
            tot = tot + dkt[:, HEAD_PAD * h:HEAD_PAD * (h + 1)]
        tot = jnp.where(inrope, tot, 0.0)
        return ds + jnp.where(inrope, _rope_tile(tot, c, sn), 0.0)

    return rowwise(fn, [(dk, MLA_HEADS * HEAD_PAD, 0), (dsmall_ssd, LANES, 0), (cosm, LANES, 0), (sinm_neg, LANES, 0)],
                   [], [(LANES, BF16)], [], "dsmall_bwd")[0]


ATT_TQ = 512
ATT_SCALE = (QK_NOPE + QK_ROPE) ** -0.5


def _att_scores(qh, kh, q0):
    s = lax.dot_general(qh, kh, _DIMS['nt'], preferred_element_type=F32) * ATT_SCALE
    r = lax.broadcasted_iota(jnp.int32, s.shape, 0) + q0
    c = lax.broadcasted_iota(jnp.int32, s.shape, 1)
    return jnp.where(c <= r, s, -1e30)


def mla_fwd(q, k, v):
    s = q.shape[0]

    def body(q_ref, k_ref, v_ref, o_ref, lse_ref):
        lane = lax.broadcasted_iota(jnp.int32, (ATT_TQ, LANES), 1)

        def block(ib):
            n = ATT_TQ * (ib + 1)
            v_t = v_ref[0:n, :]
            vlane = lax.broadcasted_iota(jnp.int32, v_t.shape, 1)
            o_tot, lse_tot = None, None
            for h in range(2):
                hs = slice(HEAD_PAD * h, HEAD_PAD * (h + 1))
                sc = _att_scores(q_ref[:, hs], k_ref[0:n, hs], ATT_TQ * ib)
                m = jnp.max(sc, axis=1, keepdims=True)
                p = jnp.exp(sc - m)
                l = jnp.sum(p, axis=1, keepdims=True)
                vh = jnp.where((vlane < V_DIM) if h == 0 else (vlane >= V_DIM), v_t, jnp.zeros_like(v_t))
                oh = lax.dot_general(p.astype(BF16), vh, _DIMS['nn'], preferred_element_type=F32) / l
                lse_h = jnp.where((lane < V_DIM) if h == 0 else (lane >= V_DIM), m + jnp.log(l), 0.0)
                o_tot = oh if o_tot is None else o_tot + oh
                lse_tot = lse_h if lse_tot is None else lse_tot + lse_h
            o_ref[...] = o_tot
            lse_ref[...] = lse_tot

        for ib in range(s // ATT_TQ):
            pl.when(pl.program_id(1) == ib)(functools.partial(block, ib))

    tile = pl.BlockSpec((ATT_TQ, LANES), lambda p, i: (i, p))
    return pl.pallas_call(
        body, name="mla_fwd", grid=(MLA_HEADS // 2, s // ATT_TQ),
        in_specs=[pl.BlockSpec((ATT_TQ, 2 * HEAD_PAD), lambda p, i: (i, p)),
                  pl.BlockSpec((s, 2 * HEAD_PAD), lambda p, i: (0, p)),
                  pl.BlockSpec((s, LANES), lambda p, i: (0, p))],
        out_specs=[tile, tile],
        out_shape=[jax.ShapeDtypeStruct((s, MLA_HEADS * V_DIM), F32)] * 2,
        compiler_params=_params(dimension_semantics=("arbitrary", "arbitrary")),
    )(q, k, v)


def mla_bwd(q, k, v, o, lse, do, cosm, sinm_neg):
    s = q.shape[0]

    def body(q_ref, k_ref, v_ref, o_ref, lse_ref, do_ref, c_ref, s_ref, dq_ref, dk_ref, dv_ref):
        i = pl.program_id(1)

        @pl.when(i == 0)
        def _():
            dk_ref[...] = jnp.zeros_like(dk_ref)
            dv_ref[...] = jnp.zeros_like(dv_ref)

        def block(ib):
            n = ATT_TQ * (ib + 1)
            o_t = o_ref[...]
            do_t = do_ref[...]
            lse_t = lse_ref[...]
            v_t = v_ref[0:n, :]
            lane = lax.broadcasted_iota(jnp.int32, do_t.shape, 1)
            for h in range(2):
                hs = slice(HEAD_PAD * h, HEAD_PAD * (h + 1))
                sel = (lane < V_DIM) if h == 0 else (lane >= V_DIM)
                qh = q_ref[:, hs]
                kh = k_ref[0:n, hs]
                doh = jnp.where(sel, do_t, 0.0)
                delta = jnp.sum(doh * o_t, axis=1, keepdims=True)
                lse_h = jnp.max(jnp.where(sel, lse_t, -jnp.inf), axis=1, keepdims=True)
                doh_b = doh.astype(BF16)
                p = jnp.exp(_att_scores(qh, kh, ATT_TQ * ib) - lse_h)
                dv_ref[0:n, :] += lax.dot_general(p.astype(BF16), doh_b, _DIMS['tn'], preferred_element_type=F32)
                dp = lax.dot_general(doh_b, v_t, _DIMS['nt'], preferred_element_type=F32)
                ds = (p * (dp - delta) * ATT_SCALE).astype(BF16)
                dk_ref[0:n, hs] += lax.dot_general(ds, qh, _DIMS['tn'], preferred_element_type=F32)
                dq = lax.dot_general(ds, kh, _DIMS['nn'], preferred_element_type=F32)
                dq_ref[:, hs] = _rope_tile(dq, c_ref[...], s_ref[...]).astype(dq_ref.dtype)

        for ib in range(s // ATT_TQ):
            pl.when(i == ib)(functools.partial(block, ib))

    tile = pl.BlockSpec((ATT_TQ, LANES), lambda p, i: (i, p))
    return pl.pallas_call(
        body, name="mla_bwd", grid=(MLA_HEADS // 2, s // ATT_TQ),
        in_specs=[pl.BlockSpec((ATT_TQ, 2 * HEAD_PAD), lambda p, i: (i, p)),
                  pl.BlockSpec((s, 2 * HEAD_PAD), lambda p, i: (0, p)),
                  pl.BlockSpec((s, LANES), lambda p, i: (0, p)), tile, tile, tile,
                  pl.BlockSpec((ATT_TQ, LANES), lambda p, i: (i, 0)), pl.BlockSpec((ATT_TQ, LANES), lambda p, i: (i, 0))],
        out_specs=[pl.BlockSpec((ATT_TQ, 2 * HEAD_PAD), lambda p, i: (i, p)),
                   pl.BlockSpec((s, 2 * HEAD_PAD), lambda p, i: (0, p)),
                   pl.BlockSpec((s, LANES), lambda p, i: (0, p))],
        out_shape=[jax.ShapeDtypeStruct((s, MLA_HEADS * HEAD_PAD), BF16),
                   jax.ShapeDtypeStruct((s, MLA_HEADS * HEAD_PAD), F32),
                   jax.ShapeDtypeStruct((s, MLA_HEADS * V_DIM), F32)],
        compiler_params=_params(dimension_semantics=("arbitrary", "arbitrary")),
    )(q, k, v, o, lse, do, cosm, sinm_neg)


MEM_TQ = 256
MEM_SCALE = MEM_HEAD_DIM ** -0.5


def _mem_probs(qh, kh):
    s = lax.dot_general(qh, kh, _DIMS['nt'], preferred_element_type=F32) * MEM_SCALE
    p = jnp.exp(s - jnp.max(s, axis=1, keepdims=True))
    return p / jnp.sum(p, axis=1, keepdims=True)


def mem_fwd(q, k, v):
    s = q.shape[0]

    def body(q_ref, k_ref, v_ref, o_ref):
        for h in range(MEM_HEADS):
            sl = slice(MEM_HEAD_DIM * h, MEM_HEAD_DIM * (h + 1))
            p = _mem_probs(q_ref[:, sl], k_ref[:, sl])
            o_ref[:, sl] = lax.dot_general(p.astype(BF16), v_ref[:, sl], _DIMS['nn'],
                                           preferred_element_type=F32).astype(o_ref.dtype)

    full = pl.BlockSpec((MEM_LEN, D_MODEL), lambda i: (0, 0))
    return pl.pallas_call(
        body, name="mem_fwd", grid=(s // MEM_TQ,),
        in_specs=[pl.BlockSpec((MEM_TQ, D_MODEL), lambda i: (i, 0)), full, full],
        out_specs=pl.BlockSpec((MEM_TQ, D_MODEL), lambda i: (i, 0)),
        out_shape=jax.ShapeDtypeStruct((s, D_MODEL), BF16),
        compiler_params=_params(dimension_semantics=("arbitrary",)),
    )(q, k, v)


def mem_bwd(q, k, v, do):
    s = q.shape[0]

    def body(q_ref, k_ref, v_ref, do_ref, dq_ref, dk_ref, dv_ref):
        @pl.when(pl.program_id(0) == 0)
        def _():
            dk_ref[...] = jnp.zeros_like(dk_ref)
            dv_ref[...] = jnp.zeros_like(dv_ref)

        for h in range(MEM_HEADS):
            sl = slice(MEM_HEAD_DIM * h, MEM_HEAD_DIM * (h + 1))
            qh, kh, vh = q_ref[:, sl], k_ref[:, sl], v_ref[:, sl]
            doh = do_ref[:, sl].astype(BF16)
            p = _mem_probs(qh, kh)
            dv_ref[:, sl] += lax.dot_general(p.astype(BF16), doh, _DIMS['tn'], preferred_element_type=F32)
            dp = lax.dot_general(doh, vh, _DIMS['nt'], preferred_element_type=F32)
            ds = (p * (dp - jnp.sum(p * dp, axis=1, keepdims=True)) * MEM_SCALE).astype(BF16)
            dq_ref[:, sl] = lax.dot_general(ds, kh, _DIMS['nn'], preferred_element_type=F32).astype(dq_ref.dtype)
            dk_ref[:, sl] += lax.dot_general(ds, qh, _DIMS['tn'], preferred_element_type=F32)

    full = pl.BlockSpec((MEM_LEN, D_MODEL), lambda i: (0, 0))
    row = pl.BlockSpec((MEM_TQ, D_MODEL), lambda i: (i, 0))
    return pl.pallas_call(
        body, name="mem_bwd", grid=(s // MEM_TQ,),
        in_specs=[row, full, full, row], out_specs=[row, full, full],
        out_shape=[jax.ShapeDtypeStruct((s, D_MODEL), BF16), jax.ShapeDtypeStruct((MEM_LEN, D_MODEL), F32),
                   jax.ShapeDtypeStruct((MEM_LEN, D_MODEL), F32)],
        compiler_params=_params(dimension_semantics=("arbitrary",)),
    )(q, k, v, do)


def _gate_norm(y, z, g):
    return _rms(y * _silu(z), g)


def loss_head(x, g, target):
    def fn(xt, tt, gt):
        def f(x_, g_):
            err = _rms(x_, g_) - tt
            return 0.5 * jnp.sum(jnp.mean(err * err, axis=-1))

        lv, (dx, dg) = jax.value_and_grad(f, argnums=(0, 1))(xt, gt)
        return dx, dg, jnp.full((1, LANES), lv, F32)

    return rowwise(fn, [(x, D_MODEL, 0), (target, D_MODEL, 0)], [g], [(D_MODEL, F32)],
                   [((1, D_MODEL), F32), ((1, LANES), F32)], "loss_head")


def _proj_runs(d):
    lo, hi = (D_IN // N_DEV) * d, (D_IN // N_DEV) * (d + 1)
    runs = []
    for a, b, new in PROJ_SEGS:
        s0, s1 = max(a, lo), min(b, hi)
        if s0 < s1:
            runs.append((s0 - lo, new + s0 - a, s1 - s0))
    return runs


LAYOUT_TM = 256


def assemble_proj(g):
    def body(g_ref, o_ref):
        o_ref[:, OFF_SMALL:OFF_SMALL + LANES] = jnp.zeros((LAYOUT_TM, LANES), o_ref.dtype)
        for d in range(N_DEV):
            for src, dst, n in _proj_runs(d):
                o_ref[:, dst:dst + n] = g_ref[d, :, src:src + n]

    return pl.pallas_call(
        body, name="assemble_proj", grid=(D_MODEL // LAYOUT_TM,),
        in_specs=[pl.BlockSpec((N_DEV, LAYOUT_TM, D_IN // N_DEV), lambda i: (0, i, 0))],
        out_specs=pl.BlockSpec((LAYOUT_TM, PROJ_W), lambda i: (i, 0)),
        out_shape=jax.ShapeDtypeStruct((D_MODEL, PROJ_W), g.dtype),
        compiler_params=_params(dimension_semantics=("arbitrary",)),
    )(g)


def extract_proj(dz, dxbc, dcq, dsmall, dckv):
    pieces = [(OFF_Z, 1024), (OFF_XBC, 2048), (OFF_CQ, Q_LORA), (OFF_SMALL, LANES), (OFF_CKV, KV_LORA)]

    def body(*refs):
        o_ref = refs[-1]
        for d in range(N_DEV):
            for src, dst, n in _proj_runs(d):
                for p, (off, w) in enumerate(pieces):
                    if off <= dst < off + w:
                        o_ref[d, :, src:src + n] = refs[p][:, dst - off:dst - off + n].astype(o_ref.dtype)

    return pl.pallas_call(
        body, name="extract_proj", grid=(D_MODEL // LAYOUT_TM,),
        in_specs=[pl.BlockSpec((LAYOUT_TM, w), lambda i: (i, 0)) for _, w in pieces],
        out_specs=pl.BlockSpec((N_DEV, LAYOUT_TM, D_IN // N_DEV), lambda i: (0, i, 0)),
        out_shape=jax.ShapeDtypeStruct((N_DEV, D_MODEL, D_IN // N_DEV), BF16),
        compiler_params=_params(dimension_semantics=("arbitrary",)),
    )(dz, dxbc, dcq, dsmall, dckv)


_QW = QK_NOPE + QK_ROPE


def assemble_uq(g):
    def body(g_ref, o_ref):
        o_ref[...] = jnp.zeros_like(o_ref)
        for d in range(N_DEV):
            for e in range(2):
                dst = HEAD_PAD * (2 * d + e)
                o_ref[:, dst:dst + _QW] = g_ref[d, :, _QW * e:_QW * (e + 1)]

    return pl.pallas_call(
        body, name="assemble_uq", grid=(1,),
        in_specs=[pl.BlockSpec((N_DEV, Q_LORA, 2 * _QW), lambda i: (0, 0, 0))],
        out_specs=pl.BlockSpec((Q_LORA, MLA_HEADS * HEAD_PAD), lambda i: (0, 0)),
        out_shape=jax.ShapeDtypeStruct((Q_LORA, MLA_HEADS * HEAD_PAD), g.dtype),
        compiler_params=_params(dimension_semantics=("arbitrary",)),
    )(g)


def extract_uq(dw):
    def body(w_ref, o_ref):
        for d in range(N_DEV):
            for e in range(2):
                src = HEAD_PAD * (2 * d + e)
                o_ref[d, :, _QW * e:_QW * (e + 1)] = w_ref[:, src:src + _QW].astype(o_ref.dtype)

    return pl.pallas_call(
        body, name="extract_uq", grid=(1,),
        in_specs=[pl.BlockSpec((Q_LORA, MLA_HEADS * HEAD_PAD), lambda i: (0, 0))],
        out_specs=pl.BlockSpec((N_DEV, Q_LORA, 2 * _QW), lambda i: (0, 0, 0)),
        out_shape=jax.ShapeDtypeStruct((N_DEV, Q_LORA, 2 * _QW), BF16),
        compiler_params=_params(dimension_semantics=("arbitrary",)),
    )(dw)


def assemble_ukv(g):
    def body(g_ref, kn_ref, v_ref):
        kn_ref[...] = jnp.zeros_like(kn_ref)
        for d in range(N_DEV):
            for e in range(2):
                h = 2 * d + e
                kn_ref[:, HEAD_PAD * h:HEAD_PAD * h + QK_NOPE] = g_ref[d, :, 128 * e:128 * e + QK_NOPE]
                v_ref[:, V_DIM * h:V_DIM * (h + 1)] = g_ref[d, :, 128 * e + QK_NOPE:128 * (e + 1)]

    return pl.pallas_call(
        body, name="assemble_ukv", grid=(1,),
        in_specs=[pl.BlockSpec((N_DEV, KV_LORA, 256), lambda i: (0, 0, 0))],
        out_specs=[pl.BlockSpec((KV_LORA, MLA_HEADS * HEAD_PAD), lambda i: (0, 0)),
                   pl.BlockSpec((KV_LORA, MLA_HEADS * V_DIM), lambda i: (0, 0))],
        out_shape=[jax.ShapeDtypeStruct((KV_LORA, MLA_HEADS * HEAD_PAD), g.dtype),
                   jax.ShapeDtypeStruct((KV_LORA, MLA_HEADS * V_DIM), g.dtype)],
        compiler_params=_params(dimension_semantics=("arbitrary",)),
    )(g)


def extract_ukv(dkn, dv):
    def body(kn_ref, v_ref, o_ref):
        for d in range(N_DEV):
            for e in range(2):
                h = 2 * d + e
                o_ref[d, :, 128 * e:128 * e + QK_NOPE] = kn_ref[:, HEAD_PAD * h:HEAD_PAD * h + QK_NOPE].astype(o_ref.dtype)
                o_ref[d, :, 128 * e + QK_NOPE:128 * (e + 1)] = v_ref[:, V_DIM * h:V_DIM * (h + 1)].astype(o_ref.dtype)

    return pl.pallas_call(
        body, name="extract_ukv", grid=(1,),
        in_specs=[pl.BlockSpec((KV_LORA, MLA_HEADS * HEAD_PAD), lambda i: (0, 0)),
                  pl.BlockSpec((KV_LORA, MLA_HEADS * V_DIM), lambda i: (0, 0))],
        out_specs=pl.BlockSpec((N_DEV, KV_LORA, 256), lambda i: (0, 0, 0)),
        out_shape=jax.ShapeDtypeStruct((N_DEV, KV_LORA, 256), BF16),
        compiler_params=_params(dimension_semantics=("arbitrary",)),
    )(dkn, dv)


_UPW = 2 * D_FF // N_DEV


def assemble_up(g):
    def body(g_ref, wg_ref, wv_ref):
        for d in range(N_DEV):
            ref = wg_ref if d < N_DEV // 2 else wv_ref
            off = _UPW * (d % (N_DEV // 2))
            ref[:, off:off + _UPW] = g_ref[d]

    half = pl.BlockSpec((LAYOUT_TM, D_FF), lambda i: (i, 0))
    return pl.pallas_call(
        body, name="assemble_up", grid=(D_MODEL // LAYOUT_TM,),
        in_specs=[pl.BlockSpec((N_DEV, LAYOUT_TM, _UPW), lambda i: (0, i, 0))],
        out_specs=[half, half], out_shape=[jax.ShapeDtypeStruct((D_MODEL, D_FF), g.dtype)] * 2,
        compiler_params=_params(dimension_semantics=("arbitrary",)),
    )(g)


def extract_up(dwg, dwv):
    def body(wg_ref, wv_ref, o_ref):
        for d in range(N_DEV):
            ref = wg_ref if d < N_DEV // 2 else wv_ref
            off = _UPW * (d % (N_DEV // 2))
            o_ref[d] = ref[:, off:off + _UPW].astype(o_ref.dtype)

    half = pl.BlockSpec((LAYOUT_TM, D_FF), lambda i: (i, 0))
    return pl.pallas_call(
        body, name="extract_up", grid=(D_MODEL // LAYOUT_TM,), in_specs=[half, half],
        out_specs=pl.BlockSpec((N_DEV, LAYOUT_TM, _UPW), lambda i: (0, i, 0)),
        out_shape=jax.ShapeDtypeStruct((N_DEV, D_MODEL, _UPW), BF16),
        compiler_params=_params(dimension_semantics=("arbitrary",)),
    )(dwg, dwv)


MESH = pl.DeviceIdType.MESH
ANY = pl.BlockSpec(memory_space=pl.ANY)


def _place():
    mx, my, mc = lax.axis_index("x"), lax.axis_index("y"), lax.axis_index("c")
    return mx, my, mc, [(1 - mx, my), (mx, 1 - my), (1 - mx, 1 - my)]


def all_gather_blocks(xs, first_only=()):
    n = len(xs)

    def body(*refs):
        x_refs, out_refs = refs[:n], refs[n:2 * n]
        send_sems, recv_sems, local_sems = refs[2 * n:]
        mx, my, mc, chips = _place()
        me, sibling = (mx, my, mc), (mx, my, 1 - mc)
        x_refs = [x_refs[t].at[0] if t in first_only else x_refs[t] for t in range(n)]

        def rows(t, px, py, pc):
            dev = 4 * px + 2 * py + pc
            return out_refs[t].at[dev] if t in first_only else out_refs[t].at[:, dev]

        def copy(t, k, block, to, src=None):
            return pltpu.make_async_remote_copy(
                src_ref=rows(t, *block) if src is None else src, dst_ref=rows(t, *block),
                send_sem=send_sems.at[t, k], recv_sem=recv_sems.at[t, k], device_id=to, device_id_type=MESH)

        mine = [pltpu.make_async_copy(x_refs[t], rows(t, *me), local_sems.at[t]) for t in range(n)]
        for cp in mine:
            cp.start()
        first = []
        for t in range(n):
            first.append(copy(t, 0, me, sibling, src=x_refs[t]))
            first += [copy(t, 1 + j, me, (*chip, mc), src=x_refs[t]) for j, chip in enumerate(chips)]
        for cp in first:
            cp.start()
        passed = []
        for j, chip in enumerate(chips):
            for t in range(n):
                copy(t, 1 + j, (*chip, mc), me).wait_recv()
                cp = copy(t, 4 + j, (*chip, mc), sibling)
                cp.start()
                passed.append(cp)
        for t in range(n):
            copy(t, 0, sibling, me).wait_recv()
            for j, chip in enumerate(chips):
                copy(t, 4 + j, (*chip, 1 - mc), me).wait_recv()
        for cp in first + passed:
            cp.wait_send()
        for cp in mine:
            cp.wait()

    return pl.pallas_call(
        body, name="all_gather_blocks",
        out_shape=[jax.ShapeDtypeStruct(((N_DEV,) if t in first_only else (x.shape[0], N_DEV)) + x.shape[1:], x.dtype)
                   for t, x in enumerate(xs)],
        in_specs=[ANY] * n, out_specs=[ANY] * n,
        scratch_shapes=[pltpu.SemaphoreType.DMA((n, 7)), pltpu.SemaphoreType.DMA((n, 7)), pltpu.SemaphoreType.DMA((n,))],
    )(*xs)


HBM = pl.BlockSpec(memory_space=pltpu.HBM)
SEM = pl.BlockSpec(memory_space=pltpu.SEMAPHORE)
EFFECT = pltpu.SideEffectType.DATAFLOW_SIDE_EFFECTING
ALL_DEVICES = [(px, py, pc) for px in range(2) for py in range(2) for pc in range(2)]


def _hbm(x):
    return pltpu.with_memory_space_constraint(x, pltpu.HBM)


def _split_start(body, name, srcs, lands, after=None):
    ns, n = len(srcs), len(lands)
    extra = [after] if after is not None else []

    def full_body(*refs):
        sems = ns + n + len(extra)
        body(refs[:ns], refs[ns:ns + n], refs[sems], refs[sems + 1])
        refs[-1][...] = jnp.zeros_like(refs[-1])

    res = pl.pallas_call(
        full_body, name=name,
        out_shape=(pltpu.SemaphoreType.DMA((n,)), pltpu.SemaphoreType.DMA((n,)),
                   *[pltpu.HBM(x.shape, x.dtype) for x in srcs], *[pltpu.HBM(x.shape, x.dtype) for x in lands],
                   jax.ShapeDtypeStruct((8, LANES), F32)),
        in_specs=[HBM] * (ns + n) + [ANY] * len(extra),
        out_specs=(SEM, SEM, *[HBM] * (ns + n), pl.BlockSpec(memory_space=pltpu.VMEM)),
        input_output_aliases={i: 2 + i for i in range(ns + n)},
        compiler_params=pltpu.CompilerParams(has_side_effects=EFFECT),
    )(*[_hbm(x) for x in srcs], *[_hbm(x) for x in lands], *extra)
    return res[0], res[1], list(res[2:2 + ns]), list(res[2 + ns:2 + ns + n]), res[-1]


def _split_wait(name, send_sems, recv_sems, srcs, lands, after, sent, landed):
    ns, n = len(srcs), len(lands)

    def body(*refs):
        src_refs, land_refs, ssem, rsem = refs[:ns], refs[ns:ns + n], refs[ns + n], refs[ns + n + 1]
        mx, my, mc, _ = _place()
        for t in range(n):
            out = sent(src_refs[t] if ns else None, land_refs[t])
            inn = landed(land_refs[t])
            pltpu.make_async_remote_copy(src_ref=out, dst_ref=out, send_sem=ssem.at[t], recv_sem=rsem.at[t],
                                         device_id=(mx, my, mc), device_id_type=MESH).wait_send()
            pltpu.make_async_remote_copy(src_ref=inn, dst_ref=inn, send_sem=ssem.at[t], recv_sem=rsem.at[t],
                                         device_id=(mx, my, mc), device_id_type=MESH).wait_recv()

    res = pl.pallas_call(
        body, name=name,
        out_shape=(*[pltpu.HBM(x.shape, x.dtype) for x in srcs], *[pltpu.HBM(x.shape, x.dtype) for x in lands]),
        in_specs=[HBM] * (ns + n) + [SEM, SEM, ANY], out_specs=[HBM] * (ns + n),
        input_output_aliases={i: i for i in range(ns + n)},
        compiler_params=pltpu.CompilerParams(has_side_effects=EFFECT),
    )(*srcs, *lands, send_sems, recv_sems, after)
    return list(res[:ns]), list(res[ns:])


FIRST_HOP = 5
SECOND_HOP = 3


def gather_start(srcs, l, tag, after=None):
    lands = [lax.empty((N_DEV,) + x.shape[1:], x.dtype) for x in srcs]

    def body(src_refs, land_refs, send_sems, recv_sems):
        mx, my, mc, chips = _place()
        me = 4 * mx + 2 * my + mc
        for t in range(len(srcs)):
            for to in [(mx, my, mc), (mx, my, 1 - mc)] + [(cx, cy, mc) for cx, cy in chips]:
                pltpu.make_async_remote_copy(
                    src_ref=src_refs[t].at[l], dst_ref=land_refs[t].at[me], send_sem=send_sems.at[t],
                    recv_sem=recv_sems.at[t], device_id=to, device_id_type=MESH).start()

    return _split_start(body, "gather_start_%d%s" % (l, tag), srcs, lands, after=after)


def gather_wait(l, tag, send_sems, recv_sems, srcs, lands, after):
    hop = lambda d: d.at[pl.ds(0, FIRST_HOP)]
    return _split_wait("gather_wait_%d%s" % (l, tag), send_sems, recv_sems, srcs, lands, after,
                       sent=lambda s, d: hop(d), landed=hop)


def gather_pass_start(lands, l, tag):
    def body(src_refs, land_refs, send_sems, recv_sems):
        mx, my, mc, chips = _place()
        for t in range(len(lands)):
            for cx, cy in chips:
                slot = land_refs[t].at[4 * cx + 2 * cy + mc]
                pltpu.make_async_remote_copy(
                    src_ref=slot, dst_ref=slot, send_sem=send_sems.at[t], recv_sem=recv_sems.at[t],
                    device_id=(mx, my, 1 - mc), device_id_type=MESH).start()

    send_sems, recv_sems, _, lands, tie = _split_start(body, "gather_pass_start_%d%s" % (l, tag), [], lands)
    return send_sems, recv_sems, lands, tie


def gather_pass_wait(l, tag, send_sems, recv_sems, lands, after):
    hop = lambda d: d.at[pl.ds(0, SECOND_HOP)]
    return _split_wait("gather_pass_wait_%d%s" % (l, tag), send_sems, recv_sems, [], lands, after,
                       sent=lambda s, d: hop(d), landed=hop)[1]


def small_gather_start(rows):
    def body(src_refs, land_refs, send_sems, recv_sems):
        mx, my, mc, _ = _place()
        for to in ALL_DEVICES:
            pltpu.make_async_remote_copy(
                src_ref=src_refs[0], dst_ref=land_refs[0].at[4 * mx + 2 * my + mc], send_sem=send_sems.at[0],
                recv_sem=recv_sems.at[0], device_id=to, device_id_type=MESH).start()

    return _split_start(body, "small_gather_start", [rows], [lax.empty((N_DEV,) + rows.shape, rows.dtype)])


def small_gather_wait(send_sems, recv_sems, srcs, lands, after):
    return _split_wait("small_gather_wait", send_sems, recv_sems, srcs, lands, after,
                       sent=lambda s, d: d, landed=lambda d: d)[1][0]


def grad_exchange_start(es, lands, l, tag, after=None):
    def body(e_refs, land_refs, send_sems, recv_sems):
        mx, my, mc, _ = _place()
        me = 4 * mx + 2 * my + mc
        for t in range(len(es)):
            for px, py, pc in ALL_DEVICES:
                pltpu.make_async_remote_copy(
                    src_ref=e_refs[t].at[4 * px + 2 * py + pc], dst_ref=land_refs[t].at[l, me], send_sem=send_sems.at[t],
                    recv_sem=recv_sems.at[t], device_id=(px, py, pc), device_id_type=MESH).start()

    return _split_start(body, "grad_exchange_start_%d%s" % (l, tag), es, lands, after=after)


def grad_exchange_wait(l, tag, send_sems, recv_sems, es, lands, after):
    return _split_wait("grad_exchange_wait_%d%s" % (l, tag), send_sems, recv_sems, es, lands, after,
                       sent=lambda s, d: s, landed=lambda d: d.at[l])


def _adam(g, w, m, v):
    nm = ADAM_B1 * m + (1.0 - ADAM_B1) * g
    nv = ADAM_B2 * v + (1.0 - ADAM_B2) * jnp.square(g)
    m_hat = nm / (1.0 - ADAM_B1 ** ADAM_STEP)
    v_hat = nv / (1.0 - ADAM_B2 ** ADAM_STEP)
    return -ADAM_LR * (m_hat / (jnp.sqrt(v_hat) + ADAM_EPS) + ADAM_WD * w), nm, nv


def adamw_big(parts, w, m, v, name, tie):
    depth, _, a, b = parts.shape
    ta = _row_tile(a)

    def body(p_ref, w_ref, m_ref, v_ref, tie_ref, g_ref, d_ref, nm_ref, nv_ref):
        g = p_ref[0].astype(F32)
        for k in range(1, N_DEV):
            g = g + p_ref[k].astype(F32)
        g_ref[...] = g
        d_ref[...], nm_ref[...], nv_ref[...] = _adam(g, w_ref[...], m_ref[...], v_ref[...])

    blk = pl.BlockSpec((None, ta, b), lambda l, i: (l, i, 0))
    return pl.pallas_call(
        body, name=name, grid=(depth, a // ta),
        in_specs=[pl.BlockSpec((None, N_DEV, ta, b), lambda l, i: (l, 0, i, 0)), blk, blk, blk, ANY], out_specs=[blk] * 4,
        out_shape=[jax.ShapeDtypeStruct((depth, a, b), F32)] * 4,
        compiler_params=_params(dimension_semantics=("arbitrary", "arbitrary")),
    )(parts, w, m, v, tie)


SMALL_VIEW = {'norm_mix': (DEPTH, 1024), 'ssm_norm': (DEPTH, 1024), 'attn_out_norm': (DEPTH, 1024),
              'norm_mem_q': (DEPTH, 1024), 'norm_mem_kv': (DEPTH, 1024), 'norm_ffn': (DEPTH, 1024),
              'q_norm': (DEPTH, 384), 'kv_norm': (DEPTH, 256), 'ssm_conv_b': (DEPTH, 2048), 'ffn_conv_b': (DEPTH, 5632),
              'dt_bias': (DEPTH, SSM_HEADS), 'a_log': (DEPTH, SSM_HEADS), 'd_skip': (DEPTH, SSM_HEADS),
              'ssm_conv_w': (DEPTH, SSM_CONV * CONV_CH // N_DEV), 'ffn_conv_w': (DEPTH, FFN_CONV * 2 * D_FF // N_DEV),
              'final_norm': (1, 1024)}
SMALL_NAMES = list(SMALL_VIEW)
SMALL_SHARDED = {'ssm_conv_w': (SSM_CONV, CONV_CH // N_DEV, CONV_CH), 'ffn_conv_w': (FFN_CONV, 2 * D_FF // N_DEV, 2 * D_FF)}


def adamw_small(gathered, ws, ms, vs, tie):
    nsm = len(SMALL_NAMES)

    def body(*refs):
        g8_ref = refs[0]
        w_refs, m_refs, v_refs = refs[1:1 + nsm], refs[1 + nsm:1 + 2 * nsm], refs[1 + 2 * nsm:1 + 3 * nsm]
        outs = refs[2 + 3 * nsm:2 + 7 * nsm]
        sum_ref = refs[2 + 7 * nsm]
        shard_bufs = refs[3 + 7 * nsm:]
        tot = g8_ref[0]
        for d in range(1, N_DEV):
            tot = tot + g8_ref[d]
        sum_ref[...] = tot
        mx, my, mc, _ = _place()
        dev = 4 * mx + 2 * my + mc

        def update(i, g):
            d, nm, nv = _adam(g, w_refs[i][...], m_refs[i][...], v_refs[i][...])
            outs[i][...] = g
            outs[nsm + i][...] = d
            outs[2 * nsm + i][...] = nm
            outs[3 * nsm + i][...] = nv

        for i, name in enumerate(SMALL_NAMES):
            rows, cols = SMALL_VIEW[name]
            off = SMALL_OFF[name]
            if name in SMALL_SHARDED:
                taps, per, full = SMALL_SHARDED[name]
                buf = shard_bufs[list(SMALL_SHARDED).index(name)]
                for d in range(N_DEV):
                    @pl.when(dev == d)
                    def _(d=d, taps=taps, per=per, full=full, off=off, buf=buf):
                        for k in range(taps):
                            buf[:, per * k:per * (k + 1)] = sum_ref[:, off + full * k + per * d:off + full * k + per * (d + 1)]
                update(i, buf[...])
            else:
                update(i, sum_ref[0:rows, off:off + cols])

    views = [jax.ShapeDtypeStruct(SMALL_VIEW[n], F32) for n in SMALL_NAMES]
    vmem = pl.BlockSpec(memory_space=pltpu.VMEM)
    res = pl.pallas_call(
        body, name="adamw_small", out_shape=views * 4, in_specs=[vmem] * (1 + 3 * nsm) + [ANY],
        out_specs=[vmem] * (4 * nsm),
        scratch_shapes=[pltpu.VMEM((DEPTH, SMALL_W), F32)] + [pltpu.VMEM(SMALL_VIEW[n], F32) for n in SMALL_SHARDED],
        compiler_params=_params(),
    )(gathered, *[ws[n] for n in SMALL_NAMES], *[ms[n] for n in SMALL_NAMES], *[vs[n] for n in SMALL_NAMES], tie)
    return [dict(zip(SMALL_NAMES, res[k * nsm:(k + 1) * nsm])) for k in range(4)]


def _layer_weights(gathered):
    w = {}
    for n, g in gathered.items():
        if n == 'w_in':
            w['w_proj'] = assemble_proj(g)
        elif n == 'w_uq':
            w['w_uq'] = assemble_uq(g)
        elif n == 'w_ukv':
            w['w_kn'], w['w_v'] = assemble_ukv(g)
        elif n == 'w_up':
            w['w_g'], w['w_vv'] = assemble_up(g)
        else:
            w[n] = g.reshape(N_DEV * BIG[n][0], BIG[n][1])
    return w


def _rope_post(acc, row_tiles, full_tiles, o_refs):
    for h in range(acc.shape[1] // HEAD_PAD):
        sl = slice(HEAD_PAD * h, HEAD_PAD * (h + 1))
        o_refs[0][:, sl] = _rope_tile(acc[:, sl], row_tiles[0], row_tiles[1]).astype(o_refs[0].dtype)


PROJ_TAIL = PROJ_W - OFF_CQ


def _proj_post(acc, row_tiles, full_tiles, o_refs):
    o_refs[0][...] = acc

    @pl.when(pl.program_id(1) == PROJ_W // PROJ_TAIL - 1)
    def _():
        o_refs[1][...] = _rms(acc[:, 0:Q_LORA], full_tiles[0]).astype(o_refs[1].dtype)
        o_refs[2][...] = _rms(acc[:, OFF_CKV - OFF_CQ:OFF_CKV - OFF_CQ + KV_LORA], full_tiles[1]).astype(o_refs[2].dtype)


def _norm_post(acc, row_tiles, full_tiles, o_refs):
    o_refs[0][...] = acc
    o_refs[1][...] = _rms(acc, full_tiles[0]).astype(o_refs[1].dtype)


def _norm_bwd_post(acc, row_tiles, full_tiles, o_refs):
    _, vjp = jax.vjp(_rms, row_tiles[0], full_tiles[0])
    dx, dg = vjp(acc)
    o_refs[0][...] = dx + row_tiles[1]
    o_refs[1][...] += dg


def _latent_norm_bwd_post(acc, row_tiles, full_tiles, o_refs):
    _, vjp = jax.vjp(_rms, row_tiles[0], full_tiles[0])
    dx, dg = vjp(acc)
    o_refs[0][...] = dx.astype(o_refs[0].dtype)
    o_refs[1][...] += dg


def _mix_bwd_post(acc, row_tiles, full_tiles, o_refs):
    y, z, o = row_tiles
    _, vjp = jax.vjp(_gate_norm, y, z, full_tiles[0])
    dy, dz, dg = vjp(acc[:, :D_SSM])
    _, vjp_o = jax.vjp(_rms, o, full_tiles[1])
    do, dg_o = vjp_o(acc[:, D_SSM:])
    o_refs[0][...] = dy
    o_refs[1][...] = dz.astype(o_refs[1].dtype)
    o_refs[2][...] = do
    o_refs[3][...] += dg
    o_refs[4][...] += dg_o


def layer_fwd(x0, h1, mem, cosm, sinm, w, sm, l, tie=None):
    gain = lambda n: (sm[n], l)
    sv = dict(x0=x0)
    sv['h1'] = h1 if h1 is not None else rmsnorm_fwd(x0, gain('norm_mix'), "norm_mix_fwd", tie=tie)
    proj, sv['cqn'], sv['ckvn'] = matmul(
        [(sv['h1'], w['w_proj'])], 'nn', F32, "proj_fwd", tie=tie if h1 is not None else None, post=_proj_post,
        fulls=[gain('q_norm'), gain('kv_norm')], outs=[F32, (Q_LORA, BF16), (KV_LORA, BF16)], tn_fixed=PROJ_TAIL)
    sv['proj'] = proj
    sv['xbc'] = ssm_conv_fwd(proj, sm['ssm_conv_w'], sm['ssm_conv_b'], l)
    sv['y'], sv['prevs'], mix = ssd_fwd(sv['xbc'], proj, sm['ptile'], sm['ssm_norm'], l)
    sv['q'] = matmul([(sv['cqn'], w['w_uq'])], 'nn', BF16, "uq_fwd", post=_rope_post, rows=[cosm, sinm])
    kn = matmul([(sv['ckvn'], w['w_kn'])], 'nn', BF16, "kn_fwd")
    sv['k'] = build_k(kn, proj, cosm, sinm)
    sv['v'] = matmul([(sv['ckvn'], w['w_v'])], 'nn', BF16, "v_fwd")
    sv['o'], sv['lse'] = mla_fwd(sv['q'], sv['k'], sv['v'])
    mix = sv['mix'] = rmsnorm_fwd(sv['o'], gain('attn_out_norm'), "attn_out_norm_fwd", out=(D_SSM, BF16, D_MIX, 1),
                                  into=(mix, 0))
    x1, sv['hq'] = matmul([(mix, w['w_out'])], 'nn', F32, "out_fwd", add=x0, post=_norm_post,
                          fulls=[gain('norm_mem_q')], outs=[F32, BF16], full_n=True)
    sv['x1'] = x1
    sv['mn'] = rmsnorm_fwd(mem, gain('norm_mem_kv'), "norm_mem_kv_fwd")
    if 'later' in w:
        w.update(w.pop('later')(sv['hq']))
    sv['mq'] = matmul([(sv['hq'], w['w_mq'])], 'nn', BF16, "mq_fwd")
    sv['mk'] = matmul([(sv['mn'], w['w_mk'])], 'nn', BF16, "mk_fwd")
    sv['mv'] = matmul([(sv['mn'], w['w_mv'])], 'nn', BF16, "mv_fwd")
    sv['om'] = mem_fwd(sv['mq'], sv['mk'], sv['mv'])
    x2, sv['h3'] = matmul([(sv['om'], w['w_mo'])], 'nn', F32, "mo_fwd", add=x1, post=_norm_post,
                          fulls=[gain('norm_ffn')], outs=[F32, BF16], full_n=True)
    sv['x2'] = x2
    tie_ffn = w.pop('prefetch')(sv['h3']) if 'prefetch' in w else None
    sv['ug'] = matmul([(sv['h3'], w['w_g'])], 'nn', F32, "up_g_fwd", tie=tie_ffn)
    sv['uv'] = matmul([(sv['h3'], w['w_vv'])], 'nn', F32, "up_v_fwd")
    sv['a'] = ffn_act_fwd(sv['ug'], sv['uv'], sm['ffn_conv_w'], sm['ffn_conv_b'], l)
    if l + 1 < DEPTH:
        x3, h1_next = matmul([(sv['a'], w['w_down'])], 'nn', F32, "down_fwd", add=x2, post=_norm_post,
                             fulls=[(sm['norm_mix'], l + 1)], outs=[F32, BF16], full_n=True)
    else:
        x3, h1_next = matmul([(sv['a'], w['w_down'])], 'nn', F32, "down_fwd_last", add=x2), None
    return x3, h1_next, sv


EARLY_GRADS = ('w_down', 'w_up', 'w_mo', 'w_mq', 'w_mk', 'w_mv', 'w_out')
LATE_GRADS = ('w_uq', 'w_ukv', 'w_in')


def layer_bwd(dx3, mem, cosm, sinm_neg, w, sm, l, sv, on_grads, tie=None):
    gain = lambda n: (sm[n], l)
    big, small = {}, {}
    proj = sv['proj']
    da = matmul([(dx3, w['w_down'])], 'nt', BF16, "down_bwd_a", tie=tie)
    big['w_down'] = matmul([(sv['a'], dx3)], 'tn', BF16, "down_bwd_w")
    dug, duv, dcwg, dcwv, dcbg, dcbv = ffn_act_bwd(sv['ug'], sv['uv'], sm['ffn_conv_w'], sm['ffn_conv_b'], l, da)
    small['ffn_conv_w'] = jnp.concatenate([dcwg, dcwv], axis=1)
    small['ffn_conv_b'] = jnp.concatenate([dcbg, dcbv], axis=1)
    gacc = [((1, D_MODEL), F32)]
    dx2, small['norm_ffn'] = matmul([(dug, w['w_g']), (duv, w['w_vv'])], 'nt', F32, "up_bwd_h", post=_norm_bwd_post,
                                    rows=[sv['x2'], dx3], fulls=[gain('norm_ffn')], accs=gacc, full_n=True, tm_cap=256)
    big['w_up'] = extract_up(matmul([(sv['h3'], dug)], 'tn', BF16, "up_g_bwd_w"),
                             matmul([(sv['h3'], duv)], 'tn', BF16, "up_v_bwd_w"))
    dom = matmul([(dx2, w['w_mo'])], 'nt', BF16, "mo_bwd_a")
    big['w_mo'] = matmul([(sv['om'], dx2)], 'tn', BF16, "mo_bwd_w")
    dmq, dmk, dmv = mem_bwd(sv['mq'], sv['mk'], sv['mv'], dom)
    dx1, small['norm_mem_q'] = matmul([(dmq, w['w_mq'])], 'nt', F32, "mq_bwd_a", post=_norm_bwd_post,
                                      rows=[sv['x1'], dx2], fulls=[gain('norm_mem_q')], accs=gacc, full_n=True)
    big['w_mq'] = matmul([(sv['hq'], dmq)], 'tn', BF16, "mq_bwd_w")
    dmn = matmul([(dmk, w['w_mk']), (dmv, w['w_mv'])], 'nt', BF16, "mkv_bwd_a")
    big['w_mk'] = matmul([(sv['mn'], dmk)], 'tn', BF16, "mk_bwd_w")
    big['w_mv'] = matmul([(sv['mn'], dmv)], 'tn', BF16, "mv_bwd_w")
    _, small['norm_mem_kv'] = rmsnorm_bwd(mem, gain('norm_mem_kv'), dmn, "norm_mem_kv_bwd", dx_dtype=BF16)
    big['w_out'] = matmul([(sv['mix'], dx1)], 'tn', BF16, "out_bwd_w")
    early = {n: big.pop(n).reshape((N_DEV,) + BIG[n]) if n != 'w_up' else big.pop(n) for n in EARLY_GRADS}
    tie = on_grads(l, 'a', early)
    dy, dz, do, small['ssm_norm'], small['attn_out_norm'] = matmul(
        [(dx1, w['w_out'])], 'nt', F32, "out_bwd_a", post=_mix_bwd_post, tie=tie,
        rows=[sv['y'], (proj, D_SSM, OFF_Z // D_SSM), sv['o']], fulls=[gain('ssm_norm'), gain('attn_out_norm')],
        outs=[(D_SSM, F32), (D_SSM, BF16), (D_SSM, F32)], accs=[((1, D_SSM), F32)] * 2, full_n=True)
    dxbc_act, dsmall_ssd, small['ptile'] = ssd_bwd(sv['xbc'], proj, sm['ptile'], l, sv['prevs'], dy)
    dxbc, small['ssm_conv_w'], small['ssm_conv_b'] = ssm_conv_bwd(proj, sm['ssm_conv_w'], sm['ssm_conv_b'], l, dxbc_act)
    dq, dk, dv = mla_bwd(sv['q'], sv['k'], sv['v'], sv['o'], sv['lse'], do, cosm, sinm_neg)
    dsmall = dsmall_bwd(dk, dsmall_ssd, cosm, sinm_neg)
    dcq, small['q_norm'] = matmul(
        [(dq, w['w_uq'])], 'nt', F32, "uq_bwd_a", post=_latent_norm_bwd_post, rows=[(proj, Q_LORA, OFF_CQ // Q_LORA)],
        fulls=[gain('q_norm')], outs=[BF16], accs=[((1, Q_LORA), F32)], full_n=True)
    big['w_uq'] = extract_uq(matmul([(sv['cqn'], dq)], 'tn', BF16, "uq_bwd_w"))
    dckv, small['kv_norm'] = matmul(
        [(dk, w['w_kn']), (dv, w['w_v'])], 'nt', F32, "ukv_bwd_a", post=_latent_norm_bwd_post,
        rows=[(proj, KV_LORA, OFF_CKV // KV_LORA)], fulls=[gain('kv_norm')], outs=[BF16], accs=[((1, KV_LORA), F32)],
        full_n=True)
    big['w_ukv'] = extract_ukv(matmul([(sv['ckvn'], dk)], 'tn', BF16, "kn_bwd_w"),
                               matmul([(sv['ckvn'], dv)], 'tn', BF16, "v_bwd_w"))
    wp = w['w_proj']
    xbc_half = lambda c: Opnd(dxbc, c0=c, shape=(dxbc.shape[0], 1024))
    wwin = lambda off, width: Opnd(wp, c0=off // width, shape=(D_MODEL, width))
    dx0, small['norm_mix'] = matmul(
        [(dz, wwin(OFF_Z, 1024)), (xbc_half(0), wwin(OFF_XBC, 1024)), (xbc_half(1), wwin(OFF_XBC + 1024, 1024)),
         (dcq, wwin(OFF_CQ, Q_LORA)), (dsmall, wwin(OFF_SMALL, LANES)), (dckv, wwin(OFF_CKV, KV_LORA))],
        'nt', F32, "proj_bwd_a", post=_norm_bwd_post, rows=[sv['x0'], dx1], fulls=[gain('norm_mix')], accs=gacc,
        full_n=True, tm_cap=256)
    h1 = sv['h1']
    big['w_in'] = extract_proj(
        matmul([(h1, dz)], 'tn', BF16, "proj_z_bwd_w"), matmul([(h1, dxbc)], 'tn', BF16, "proj_xbc_bwd_w"),
        matmul([(h1, dcq)], 'tn', BF16, "proj_cq_bwd_w"), matmul([(h1, dsmall)], 'tn', BF16, "proj_small_bwd_w"),
        matmul([(h1, dckv)], 'tn', BF16, "proj_ckv_bwd_w"))
    return dx0, on_grads(l, 'b', big), small


def _small_row(small, final=None):
    pt = small['ptile']
    parts = []
    for n, wd in SMALL_SEGS:
        if n in ('dt_bias', 'a_log', 'd_skip'):
            parts.append(pt[('dt_bias', 'a_log', 'd_skip').index(n)][None, :])
        elif n in SMALL_SHARDED:
            parts.append(small[n].reshape(1, wd))
        elif n == 'final_norm':
            parts.append(final if final is not None else jnp.zeros((1, wd), F32))
        else:
            parts.append(small[n])
    return jnp.concatenate(parts, axis=1)


def _rope_tables(positions):
    inv_freq = 1.0 / (ROPE_THETA ** (jnp.arange(0, QK_ROPE, 2, dtype=F32) / QK_ROPE))
    ang = positions.astype(F32)[:, None] * inv_freq
    cos, sin = jnp.cos(ang), jnp.sin(ang)
    s = positions.shape[0]
    pad = jnp.zeros((s, LANES - ROPE_LANE0 - QK_ROPE), F32)
    cosm = jnp.concatenate([jnp.ones((s, ROPE_LANE0), F32), cos, cos, pad], axis=1)
    sinm = jnp.concatenate([jnp.zeros((s, ROPE_LANE0), F32), -sin, sin, pad], axis=1)
    return cosm, sinm


def _small_views(rep, conv_full):
    sm = {n: rep[n].reshape(DEPTH, 1, -1) for n in ('norm_mix', 'ssm_norm', 'attn_out_norm', 'norm_mem_q',
                                                    'norm_mem_kv', 'norm_ffn', 'q_norm', 'kv_norm', 'ssm_conv_b',
                                                    'ffn_conv_b')}
    sm.update(conv_full)
    rows = jnp.stack([rep['dt_bias'], rep['a_log'], rep['d_skip']], axis=1)
    sm['ptile'] = jnp.pad(rows, ((0, 0), (0, 8 - 3), (0, LANES - SSM_HEADS)))
    return sm


def local_step(x, mem, positions, target, sm, final_norm, weights_of, on_grads):
    cosm, sinm = _rope_tables(positions)
    sinm_neg = -sinm
    saved, ws = [], []
    h, h1 = x, None
    for l in range(DEPTH):
        w, tie = weights_of(l, h)
        ws.append(w)
        h, h1, sv = layer_fwd(h, h1, mem, cosm, sinm, w, sm, l, tie=tie)
        saved.append(sv)
    dx, dfinal, lossv = loss_head(h, (final_norm.reshape(1, 1, -1), 0), target)
    rows = [None] * DEPTH
    tie = None
    for l in reversed(range(DEPTH)):
        dx, tie, small = layer_bwd(dx, mem, cosm, sinm_neg, ws[l], sm, l, saved[l], on_grads, tie=tie)
        rows[l] = _small_row(small, dfinal if l == 0 else None)
    return lossv[0, 0], dx, jnp.concatenate(rows, axis=0)


def kernel(x, mem, positions, norm_mix, w_in, ssm_conv_w, ssm_conv_b, dt_bias, a_log, d_skip, ssm_norm, q_norm, w_uq, kv_norm, w_ukv, attn_out_norm, w_out, norm_mem_q, norm_mem_kv, w_mq, w_mk, w_mv, w_mo, norm_ffn, w_up, ffn_conv_w, ffn_conv_b, w_down, final_norm, loss_target, m_norm_mix, m_w_in, m_ssm_conv_w, m_ssm_conv_b, m_dt_bias, m_a_log, m_d_skip, m_ssm_norm, m_q_norm, m_w_uq, m_kv_norm, m_w_ukv, m_attn_out_norm, m_w_out, m_norm_mem_q, m_norm_mem_kv, m_w_mq, m_w_mk, m_w_mv, m_w_mo, m_norm_ffn, m_w_up, m_ffn_conv_w, m_ffn_conv_b, m_w_down, m_final_norm, v_norm_mix, v_w_in, v_ssm_conv_w, v_ssm_conv_b, v_dt_bias, v_a_log, v_d_skip, v_ssm_norm, v_q_norm, v_w_uq, v_kv_norm, v_w_ukv, v_attn_out_norm, v_w_out, v_norm_mem_q, v_norm_mem_kv, v_w_mq, v_w_mk, v_w_mv, v_w_mo, v_norm_ffn, v_w_up, v_ffn_conv_w, v_ffn_conv_b, v_w_down, v_final_norm):
    args = locals()
    wts = {n: args[n] for n in WEIGHT_NAMES}
    ms = {n: args['m_' + n] for n in WEIGHT_NAMES}
    vs = {n: args['v_' + n] for n in WEIGHT_NAMES}

    st = dict(srcs={n: wts[n].astype(BF16) for n in BIG_NAMES}, exchanges=[],
              lands={n: lax.empty((DEPTH, N_DEV) + BIG[n], BF16) for n in BIG_NAMES})
    first = LATE_GRADS + ('w_out',)
    rest = tuple(n for n in BIG_NAMES if n not in first)
    got = all_gather_blocks([st['srcs'][n] for n in first] + [wts[n] for n in SMALL_SHARDED],
                            first_only=tuple(range(len(first))))
    conv_full = {}
    for n, g in zip(SMALL_SHARDED, got[len(first):]):
        taps, per, full = SMALL_SHARDED[n]
        conv_full[n] = jnp.moveaxis(g, 1, 2).reshape(DEPTH, taps, full)
    sm = _small_views(wts, conv_full)

    def start(names, l, tag, after=None):
        send_sems, recv_sems, thru, lands, tie = gather_start([st['srcs'][n] for n in names], l, tag, after)
        st['srcs'].update(zip(names, thru))
        return (names, l, tag, send_sems, recv_sems, lands), tie

    def pass_on(handle, after):
        names, l, tag, send_sems, recv_sems, lands = handle
        thru, lands = gather_wait(l, tag, send_sems, recv_sems, [st['srcs'][n] for n in names], lands, after)
        st['srcs'].update(zip(names, thru))
        send_sems, recv_sems, lands, tie = gather_pass_start(lands, l, tag)
        return (names, l, tag, send_sems, recv_sems, lands), tie

    def finish(handle, after):
        names, l, tag, send_sems, recv_sems, lands = handle
        return _layer_weights(dict(zip(names, gather_pass_wait(l, tag, send_sems, recv_sems, lands, after))))

    later, _ = start(rest, 0, "r", after=got[0])

    def weights_of(l, h):
        if l == 0:
            w = _layer_weights(dict(zip(first, got[:len(first)])))
            w['later'] = lambda after: finish(pass_on(later, after)[0], after)
        else:
            w = finish(st['next'], h)
        tie = None
        if l + 1 < DEPTH:
            st['next'], tie = start(BIG_NAMES, l + 1, "")

            def prefetch(after):
                st['next'], tie2 = pass_on(st['next'], after)
                return tie2

            w['prefetch'] = prefetch
        return w, tie

    def on_grads(l, tag, big, after=None):
        if (l, tag) == (0, 'b') and after is None:
            st['held'] = big
            return None
        names = list(big)
        send_sems, recv_sems, thru, lands, tie = grad_exchange_start(
            [big[n] for n in names], [st['lands'][n] for n in names], l, tag, after)
        st['lands'].update(zip(names, lands))
        st['exchanges'].append((l, tag, names, send_sems, recv_sems, thru))
        return tie

    loss_local, dx, small_rows = local_step(x[0], mem[0], positions[0], loss_target[0], sm, final_norm, weights_of,
                                            on_grads)
    outs = [{}, {}, {}, {}]

    sg_send, sg_recv, sg_src, sg_land, tok = small_gather_start(small_rows)
    tie = on_grads(0, 'b', st['held'], after=tok)

    def wait(exchange, after):
        l, tag, names, send_sems, recv_sems, thru = exchange
        _, lands = grad_exchange_wait(l, tag, send_sems, recv_sems, thru, [st['lands'][n] for n in names], after)
        st['lands'].update(zip(names, lands))

    def update(names, tie):
        for n in names:
            res_n = adamw_big(st['lands'][n], wts[n], ms[n], vs[n], "adamw_" + n, tie)
            tie = res_n[0]
            for k in range(4):
                outs[k][n] = res_n[k]
        return tie

    for exchange in st['exchanges'][:-1]:
        wait(exchange, tie)
    tie = update(EARLY_GRADS, tie)

    small_all = small_gather_wait(sg_send, sg_recv, sg_src, sg_land, tie)
    view = lambda d: {n: d[n].reshape(SMALL_VIEW[n]) for n in SMALL_NAMES}
    res = adamw_small(small_all, view(wts), view(ms), view(vs), tie)
    for k in range(4):
        for n in SMALL_NAMES:
            outs[k][n] = res[k][n].reshape(wts[n].shape)

    wait(st['exchanges'][-1], res[0]['final_norm'])
    update(LATE_GRADS, res[0]['final_norm'])

    loss = lax.psum(loss_local, ("x", "y", "c"))
    return (loss, dx[None], *[outs[0][n] for n in WEIGHT_NAMES], *[outs[1][n] for n in WEIGHT_NAMES],
            *[outs[2][n] for n in WEIGHT_NAMES], *[outs[3][n] for n in WEIGHT_NAMES])
```

```python
import functools
import math
from typing import Any, NamedTuple, Optional

import jax
import jax.numpy as jnp
from jax import lax
from jax.experimental import pallas as pl
from jax.experimental.pallas import tpu as pltpu

F32 = jnp.float32
BF16 = jnp.bfloat16

D_MODEL = 1024
DEPTH = 4
MEM_LEN = 256
EPS = 1e-6
SSM_HEADS = 16
SSM_HEAD_DIM = 64
D_SSM = 1024
SSM_GROUPS = 4
SSM_STATE = 128
SSM_CONV = 4
SSM_CHUNK = 128
CONV_CH = 2048
MLA_HEADS = 16
QK_NOPE = 64
QK_ROPE = 32
V_DIM = 64
Q_LORA = 384
KV_LORA = 256
ROPE_THETA = 10000.0
MEM_HEADS = 4
MEM_HEAD_DIM = 256
D_FF = 2816
FFN_CONV = 3
D_IN = 3760
D_MIX = 2048
ADAM_LR = 0.001
ADAM_B1 = 0.9
ADAM_B2 = 0.999
ADAM_EPS = 1e-08
ADAM_WD = 0.01
ADAM_STEP = 10

N_DEV = 8
N_CHIP = 4
LANES = 128
HEAD_PAD = 128
PROJ_W = 3840
OFF_Z, OFF_XBC, OFF_CQ, OFF_SMALL, OFF_CKV = 0, 1024, 3072, 3456, 3584
ROPE_LANE0 = 64
VMEM_LIMIT = 56 * 1024 * 1024
MM_BLOCK_BYTES = 4 * 1024 * 1024
WEIGHT_NAMES = ['norm_mix', 'w_in', 'ssm_conv_w', 'ssm_conv_b', 'dt_bias', 'a_log', 'd_skip', 'ssm_norm', 'q_norm',
                'w_uq', 'kv_norm', 'w_ukv', 'attn_out_norm', 'w_out', 'norm_mem_q', 'norm_mem_kv', 'w_mq', 'w_mk',
                'w_mv', 'w_mo', 'norm_ffn', 'w_up', 'ffn_conv_w', 'ffn_conv_b', 'w_down', 'final_norm']
BIG = {'w_in': (1024, 470), 'w_uq': (384, 192), 'w_ukv': (256, 256), 'w_up': (1024, 704), 'w_out': (256, 1024),
       'w_mq': (128, 1024), 'w_mk': (128, 1024), 'w_mv': (128, 1024), 'w_mo': (128, 1024), 'w_down': (352, 1024)}
BIG_NAMES = list(BIG)
PROJ_SEGS = [(0, 1024, OFF_Z), (1024, 3072, OFF_XBC), (3072, 3088, OFF_SMALL), (3088, 3472, OFF_CQ),
             (3472, 3728, OFF_CKV), (3728, 3760, OFF_SMALL + ROPE_LANE0)]
SMALL_SEGS = [('norm_mix', 1024), ('ssm_norm', 1024), ('attn_out_norm', 1024), ('norm_mem_q', 1024),
              ('norm_mem_kv', 1024), ('norm_ffn', 1024), ('q_norm', 384), ('kv_norm', 256), ('ssm_conv_b', 2048),
              ('ffn_conv_b', 5632), ('dt_bias', 128), ('a_log', 128), ('d_skip', 128),
              ('ssm_conv_w', SSM_CONV * CONV_CH), ('ffn_conv_w', FFN_CONV * 2 * D_FF), ('final_norm', 1024)]
SMALL_OFF = {}
_o = 0
for _n, _w in SMALL_SEGS:
    SMALL_OFF[_n] = _o
    _o += _w
SMALL_W = _o


def _params(**kw):
    return pltpu.CompilerParams(vmem_limit_bytes=VMEM_LIMIT, **kw)


def _pick(n, cap):
    if n <= cap:
        return n
    best = None
    for t in range(LANES, cap + 1, LANES):
        if n % t == 0:
            best = t
    assert best is not None, (n, cap)
    return best


def _row_tile(a, cap=256):
    if a <= cap:
        return a
    best = None
    for t in range(16, cap + 1, 16):
        if a % t == 0:
            best = t
    assert best is not None, (a, cap)
    return best


class Opnd(NamedTuple):
    arr: Any
    lead: Optional[int] = None
    r0: int = 0
    c0: int = 0
    shape: Optional[tuple] = None


def _opnd(x):
    return x if isinstance(x, Opnd) else Opnd(x)


def _lshape(o):
    return tuple(o.shape) if o.shape is not None else tuple(o.arr.shape[-2:])


def _spec(o, br, bc, bi, bj):
    rr, cc = _lshape(o)
    assert rr % br == 0 and cc % bc == 0, (rr, cc, br, bc)
    ro, co = o.r0 * (rr // br), o.c0 * (cc // bc)
    if o.lead is None:
        return pl.BlockSpec((br, bc), lambda i, j: (ro + bi(i, j), co + bj(i, j)))
    return pl.BlockSpec((None, br, bc), lambda i, j: (o.lead, ro + bi(i, j), co + bj(i, j)))


_DIMS = {'nn': (((1,), (0,)), ((), ())), 'nt': (((1,), (1,)), ((), ())), 'tn': (((0,), (0,)), ((), ()))}
_ROW = lambda i, j: i
_COL = lambda i, j: j
_ZERO = lambda i, j: 0


def matmul(pairs, mode, out_dtype, name, add=None, tie=None, post=None, rows=(), fulls=(), outs=None, full_n=False,
           accs=(), tm_cap=None, tn_fixed=None):
    pairs = [(_opnd(a), _opnd(b)) for a, b in pairs]
    a0, b0 = pairs[0]
    if mode == 'nn':
        m, n = _lshape(a0)[0], _lshape(b0)[1]
    elif mode == 'nt':
        m, n = _lshape(a0)[0], _lshape(b0)[0]
    else:
        m, n = _lshape(a0)[1], _lshape(b0)[1]
    isz = lambda o: jnp.dtype(o.arr.dtype).itemsize
    osz = jnp.dtype(out_dtype).itemsize
    cap = lambda budget, per: max(LANES, budget // per // LANES * LANES)
    if mode == 'tn':
        ktok = _lshape(a0)[0]
        tm = _pick(m, cap(3 * MM_BLOCK_BYTES // 2, ktok * isz(a0)))
        tn = _pick(n, cap(3 * MM_BLOCK_BYTES // 2, ktok * isz(b0)))
    else:
        tm = _pick(m, min(2048, cap(2 * MM_BLOCK_BYTES, sum(_lshape(a)[1] * isz(a) for a, _ in pairs))))
        tn = _pick(n, min(cap(3 * MM_BLOCK_BYTES // 2, sum(_lshape(a)[1] * isz(b) for a, b in pairs)),
                          cap(MM_BLOCK_BYTES, tm * osz), n // 2 if n >= 1024 else n))
        if full_n:
            tm, tn = _pick(m, min(tm, tm_cap or tm, cap(MM_BLOCK_BYTES // 2, n * osz))), n
        if tn_fixed is not None:
            tn = tn_fixed
    npairs = len(pairs)
    outs = list(outs) if outs is not None else [out_dtype]
    nadd = 1 if add is not None else 0
    nrows, nfulls = len(rows), len(fulls)

    def body(*refs):
        o_refs = refs[len(refs) - len(outs) - len(accs):]
        if accs:
            @pl.when(jnp.logical_and(pl.program_id(0) == 0, pl.program_id(1) == 0))
            def _():
                for r in o_refs[len(outs):]:
                    r[...] = jnp.zeros_like(r)

        acc = None
        for p in range(npairs):
            a = refs[2 * p][...].astype(BF16)
            b = refs[2 * p + 1][...].astype(BF16)
            d = lax.dot_general(a, b, _DIMS[mode], preferred_element_type=F32)
            acc = d if acc is None else acc + d
        if add is not None:
            acc = acc + refs[2 * npairs][...].astype(F32)
        if post is None:
            o_refs[0][...] = acc.astype(out_dtype)
        else:
            x0 = 2 * npairs + nadd
            post(acc, [r[...] for r in refs[x0:x0 + nrows]], [r[...] for r in refs[x0 + nrows:x0 + nrows + nfulls]], o_refs)

    rows = [r if isinstance(r, tuple) else (r, r.shape[1], 0) for r in rows]
    tie_specs = [pl.BlockSpec((tm, wd), lambda i, j, cb=cb: (i, cb)) for _, wd, cb in rows]
    tie_specs += [pl.BlockSpec((None,) + f.shape[1:], lambda i, j, ld=ld, nd=f.ndim - 1: (ld,) + (0,) * nd) for f, ld in fulls]
    tie_args = [r for r, _, _ in rows] + [f for f, _ in fulls]
    if tie is not None:
        tie_specs.append(pl.BlockSpec(memory_space=pl.ANY))
        tie_args.append(tie)

    in_specs, args = [], []
    for a, b in pairs:
        if mode == 'nn':
            k = _lshape(a)[1]
            in_specs += [_spec(a, tm, k, _ROW, _ZERO), _spec(b, k, tn, _ZERO, _COL)]
        elif mode == 'nt':
            k = _lshape(a)[1]
            in_specs += [_spec(a, tm, k, _ROW, _ZERO), _spec(b, tn, k, _COL, _ZERO)]
        else:
            k = _lshape(a)[0]
            in_specs += [_spec(a, k, tm, _ZERO, _ROW), _spec(b, k, tn, _ZERO, _COL)]
        args += [a.arr, b.arr]
    if add is not None:
        in_specs.append(pl.BlockSpec((tm, tn), lambda i, j: (i, j)))
        args.append(add)
    res = pl.pallas_call(
        body, name=name, grid=(m // tm, n // tn), in_specs=in_specs + tie_specs,
        out_specs=[pl.BlockSpec((tm, o[0]), lambda i, j: (i, 0)) if isinstance(o, tuple) else
                   pl.BlockSpec((tm, tn), lambda i, j: (i, j)) for o in outs] +
                  [pl.BlockSpec(shp, lambda i, j, nd=len(shp): (0,) * nd) for shp, _ in accs],
        out_shape=[jax.ShapeDtypeStruct((m, o[0]), o[1]) if isinstance(o, tuple) else jax.ShapeDtypeStruct((m, n), o)
                   for o in outs] + [jax.ShapeDtypeStruct(shp, dt) for shp, dt in accs],
        compiler_params=_params(dimension_semantics=("arbitrary", "arbitrary")),
    )(*args, *tie_args)
    return res[0] if len(outs) + len(accs) == 1 else res


def rowwise(fn, rows, fulls, outs, accs, name, tm=256, into=None, tie=None):
    s = rows[0][0].shape[0]
    nrow, nfull, nout, nacc = len(rows), len(fulls), len(outs), len(accs)
    nin = nrow + nfull

    def body(*refs):
        ins = [r[...] for r in refs[:nin]]
        res = fn(*ins)
        if not isinstance(res, (tuple, list)):
            res = (res,)
        orefs = refs[nin + (1 if into is not None else 0) + (1 if tie is not None else 0):]
        for k in range(nout):
            orefs[k][...] = res[k].astype(orefs[k].dtype)
        if nacc:
            @pl.when(pl.program_id(0) == 0)
            def _():
                for k in range(nacc):
                    orefs[nout + k][...] = jnp.zeros_like(orefs[nout + k])

            for k in range(nacc):
                orefs[nout + k][...] += res[nout + k].astype(orefs[nout + k].dtype)

    in_specs = [pl.BlockSpec((tm, w), lambda i, cb=cb: (i, cb)) for _, w, cb in rows]
    in_specs += [pl.BlockSpec((None,) + f.shape[1:], lambda i, ld=ld, nd=f.ndim - 1: (ld,) + (0,) * nd) for f, ld in fulls]
    args = [r[0] for r in rows] + [f for f, _ in fulls]
    aliases = {}
    if into is not None:
        in_specs.append(pl.BlockSpec(memory_space=pl.ANY))
        args.append(into[0])
        aliases = {nin: into[1]}
    if tie is not None:
        in_specs.append(pl.BlockSpec(memory_space=pl.ANY))
        args.append(tie)
    out_specs, out_shape = [], []
    for o in outs:
        w, dt = o[0], o[1]
        total, cb = (o[2], o[3]) if len(o) == 4 else (w, 0)
        out_specs.append(pl.BlockSpec((tm, w), lambda i, cb=cb: (i, cb)))
        out_shape.append(jax.ShapeDtypeStruct((s, total), dt))
    for shp, dt in accs:
        out_specs.append(pl.BlockSpec(shp, lambda i, nd=len(shp): (0,) * nd))
        out_shape.append(jax.ShapeDtypeStruct(shp, dt))
    return pl.pallas_call(
        body, name=name, grid=(s // tm,), in_specs=in_specs, out_specs=out_specs, out_shape=out_shape,
        input_output_aliases=aliases, compiler_params=_params(dimension_semantics=("arbitrary",)),
    )(*args)


def _rms(x, g):
    xf = x.astype(F32)
    var = jnp.mean(xf * xf, axis=-1, keepdims=True)
    return xf * lax.rsqrt(var + EPS) * g


def rmsnorm_fwd(x, g, name, width=None, colblock=0, out=None, into=None, tie=None):
    w = width or x.shape[1]
    return rowwise(lambda xt, gt: _rms(xt, gt), [(x, w, colblock)], [g], [out or (w, BF16)], [], name, into=into,
                   tie=tie)[0]


def rmsnorm_bwd(x, g, dh, name, resid=None, width=None, colblock=0, dh_colblock=0, dx_dtype=F32):
    w = width or x.shape[1]

    def fn(xt, dht, *rest):
        gt = rest[-1]
        _, vjp = jax.vjp(_rms, xt.astype(F32), gt)
        dx, dg = vjp(dht.astype(F32))
        if resid is not None:
            dx = dx + rest[0]
        return dx, dg

    rows = [(x, w, colblock), (dh, w, dh_colblock)] + ([(resid, w, 0)] if resid is not None else [])
    return rowwise(fn, rows, [g], [(w, dx_dtype)], [((1, w), F32)], name)


CONV_R = 64
HALO = 8


def _ext_rows(ref, i, nchunk, above, below):
    r0 = pl.multiple_of(i * CONV_R, CONV_R)
    s = ref.shape[0]
    parts = []
    if above:
        top = ref[pl.ds(pl.multiple_of(jnp.maximum(r0 - HALO, 0), HALO), HALO), :].astype(F32)
        parts.append(jnp.where(i > 0, top, 0.0))
    parts.append(ref[pl.ds(r0, CONV_R), :].astype(F32))
    if below:
        tile = 2 * HALO if ref.dtype == BF16 else HALO
        bot = ref[pl.ds(pl.multiple_of(jnp.minimum(r0 + CONV_R, s - tile), tile), tile), :].astype(F32)[0:HALO]
        parts.append(jnp.where(i < nchunk - 1, bot, 0.0))
    return jnp.concatenate(parts, axis=0)


def _conv_ext(ext, w_ref, b_ref, kw):
    y = ext[HALO:] * w_ref[kw - 1:kw, :] + b_ref[...]
    for k in range(1, kw):
        y = y + pltpu.roll(ext, k, 0)[HALO:] * w_ref[kw - 1 - k:kw - k, :]
    return y


def _conv_t_ext(d, w_ref, kw):
    n = d.shape[0]
    y = d[:n - HALO] * w_ref[kw - 1:kw, :]
    for k in range(1, kw):
        y = y + pltpu.roll(d, n - k, 0)[:n - HALO] * w_ref[kw - 1 - k:kw - k, :]
    return y


def _conv_wgrad(dp, ext, kw):
    out = [jnp.sum(dp, axis=0, keepdims=True), jnp.sum(dp * ext[HALO:HALO + CONV_R], axis=0, keepdims=True)]
    for k in range(1, kw):
        out.append(jnp.sum(dp * pltpu.roll(ext, k, 0)[HALO:HALO + CONV_R], axis=0, keepdims=True))
    return out


def _store_wgrad(res, dw_ref, db_ref, kw):
    db_ref[...] = res[0]
    for k in range(kw):
        dw_ref[kw - 1 - k:kw - k, :] = res[1 + k]


def _silu(x):
    return x * jax.nn.sigmoid(x)


def _dsilu(x):
    s = jax.nn.sigmoid(x)
    return s * (1.0 + x * (1.0 - s))


SSM_TC = 256


def ssm_conv_fwd(proj, cw, cb, l):
    s = proj.shape[0]
    off = OFF_XBC // SSM_TC

    def body(u_ref, w_ref, b_ref, o_ref):
        nchunk = s // CONV_R

        def step(i, carry):
            ext = _ext_rows(u_ref, i, nchunk, True, False)
            o_ref[pl.ds(pl.multiple_of(i * CONV_R, CONV_R), CONV_R), :] = _silu(_conv_ext(ext, w_ref, b_ref, SSM_CONV))
            return carry

        lax.fori_loop(0, nchunk, step, 0)

    return pl.pallas_call(
        body, name="ssm_conv_fwd", grid=(CONV_CH // SSM_TC,),
        in_specs=[pl.BlockSpec((s, SSM_TC), lambda j: (0, off + j)),
                  pl.BlockSpec((None, SSM_CONV, SSM_TC), lambda j: (l, 0, j)),
                  pl.BlockSpec((None, 1, SSM_TC), lambda j: (l, 0, j))],
        out_specs=pl.BlockSpec((s, SSM_TC), lambda j: (0, j)),
        out_shape=jax.ShapeDtypeStruct((s, CONV_CH), F32),
        compiler_params=_params(dimension_semantics=("arbitrary",)),
    )(proj, cw, cb)


def ssm_conv_bwd(proj, cw, cb, l, dact):
    s = proj.shape[0]
    off = OFF_XBC // SSM_TC

    def body(u_ref, w_ref, b_ref, d_ref, du_ref, dw_ref, db_ref):
        nchunk = s // CONV_R

        def step(i, carry):
            ext = _ext_rows(u_ref, i, nchunk, True, True)
            dpre = _ext_rows(d_ref, i, nchunk, False, True) * _dsilu(_conv_ext(ext, w_ref, b_ref, SSM_CONV))
            du_ref[pl.ds(pl.multiple_of(i * CONV_R, CONV_R), CONV_R), :] = _conv_t_ext(dpre, w_ref, SSM_CONV).astype(du_ref.dtype)
            return tuple(c + g for c, g in zip(carry, _conv_wgrad(dpre[:CONV_R], ext, SSM_CONV)))

        zero = jnp.zeros((1, SSM_TC), F32)
        _store_wgrad(lax.fori_loop(0, nchunk, step, (zero,) * (SSM_CONV + 1)), dw_ref, db_ref, SSM_CONV)

    return pl.pallas_call(
        body, name="ssm_conv_bwd", grid=(CONV_CH // SSM_TC,),
        in_specs=[pl.BlockSpec((s, SSM_TC), lambda j: (0, off + j)),
                  pl.BlockSpec((None, SSM_CONV, SSM_TC), lambda j: (l, 0, j)),
                  pl.BlockSpec((None, 1, SSM_TC), lambda j: (l, 0, j)), pl.BlockSpec((s, SSM_TC), lambda j: (0, j))],
        out_specs=[pl.BlockSpec((s, SSM_TC), lambda j: (0, j)), pl.BlockSpec((SSM_CONV, SSM_TC), lambda j: (0, j)),
                   pl.BlockSpec((1, SSM_TC), lambda j: (0, j))],
        out_shape=[jax.ShapeDtypeStruct((s, CONV_CH), BF16), jax.ShapeDtypeStruct((SSM_CONV, CONV_CH), F32),
                   jax.ShapeDtypeStruct((1, CONV_CH), F32)],
        compiler_params=_params(dimension_semantics=("arbitrary",)),
    )(proj, cw, cb, dact)


FFN_TC = 256
FFN_NT = D_FF // FFN_TC


def _ffn_specs(s, l):
    blk = pl.BlockSpec((s, FFN_TC), lambda j: (0, j))
    wg = pl.BlockSpec((None, FFN_CONV, FFN_TC), lambda j: (l, 0, j))
    wv = pl.BlockSpec((None, FFN_CONV, FFN_TC), lambda j: (l, 0, FFN_NT + j))
    bg = pl.BlockSpec((None, 1, FFN_TC), lambda j: (l, 0, j))
    bv = pl.BlockSpec((None, 1, FFN_TC), lambda j: (l, 0, FFN_NT + j))
    return blk, wg, wv, bg, bv


def ffn_act_fwd(ug, uv, cw, cb, l):
    s = ug.shape[0]

    def body(g_ref, v_ref, wg_ref, wv_ref, bg_ref, bv_ref, o_ref):
        nchunk = s // CONV_R

        def step(i, carry):
            cg = _conv_ext(_ext_rows(g_ref, i, nchunk, True, False), wg_ref, bg_ref, FFN_CONV)
            cv = _conv_ext(_ext_rows(v_ref, i, nchunk, True, False), wv_ref, bv_ref, FFN_CONV)
            o_ref[pl.ds(pl.multiple_of(i * CONV_R, CONV_R), CONV_R), :] = (_silu(cg) * cv).astype(o_ref.dtype)
            return carry

        lax.fori_loop(0, nchunk, step, 0)

    blk, wg, wv, bg, bv = _ffn_specs(s, l)
    return pl.pallas_call(
        body, name="ffn_act_fwd", grid=(FFN_NT,), in_specs=[blk, blk, wg, wv, bg, bv],
        out_specs=blk, out_shape=jax.ShapeDtypeStruct((s, D_FF), BF16),
        compiler_params=_params(dimension_semantics=("arbitrary",)),
    )(ug, uv, cw, cw, cb, cb)


def ffn_act_bwd(ug, uv, cw, cb, l, da):
    s = ug.shape[0]

    def body(g_ref, v_ref, wg_ref, wv_ref, bg_ref, bv_ref, da_ref, dg_ref, dv_ref, dwg_ref, dwv_ref, dbg_ref, dbv_ref):
        nchunk = s // CONV_R

        def step(i, carry):
            rows = pl.ds(pl.multiple_of(i * CONV_R, CONV_R), CONV_R)
            eg = _ext_rows(g_ref, i, nchunk, True, True)
            ev = _ext_rows(v_ref, i, nchunk, True, True)
            cg = _conv_ext(eg, wg_ref, bg_ref, FFN_CONV)
            cv = _conv_ext(ev, wv_ref, bv_ref, FFN_CONV)
            da_t = _ext_rows(da_ref, i, nchunk, False, True)
            sg = jax.nn.sigmoid(cg)
            dcg = da_t * cv * (sg * (1.0 + cg * (1.0 - sg)))
            dcv = da_t * (cg * sg)
            dg_ref[rows, :] = _conv_t_ext(dcg, wg_ref, FFN_CONV).astype(dg_ref.dtype)
            dv_ref[rows, :] = _conv_t_ext(dcv, wv_ref, FFN_CONV).astype(dv_ref.dtype)
            grads = _conv_wgrad(dcg[:CONV_R], eg, FFN_CONV) + _conv_wgrad(dcv[:CONV_R], ev, FFN_CONV)
            return tuple(c + g for c, g in zip(carry, grads))

        zero = jnp.zeros((1, FFN_TC), F32)
        res = lax.fori_loop(0, nchunk, step, (zero,) * (2 * FFN_CONV + 2))
        _store_wgrad(res[:FFN_CONV + 1], dwg_ref, dbg_ref, FFN_CONV)
        _store_wgrad(res[FFN_CONV + 1:], dwv_ref, dbv_ref, FFN_CONV)

    blk, wg, wv, bg, bv = _ffn_specs(s, l)
    wblk = pl.BlockSpec((FFN_CONV, FFN_TC), lambda j: (0, j))
    bblk = pl.BlockSpec((1, FFN_TC), lambda j: (0, j))
    return pl.pallas_call(
        body, name="ffn_act_bwd", grid=(FFN_NT,), in_specs=[blk, blk, wg, wv, bg, bv, blk],
        out_specs=[blk, blk, wblk, wblk, bblk, bblk],
        out_shape=[jax.ShapeDtypeStruct((s, D_FF), BF16), jax.ShapeDtypeStruct((s, D_FF), BF16),
                   jax.ShapeDtypeStruct((FFN_CONV, D_FF), F32), jax.ShapeDtypeStruct((FFN_CONV, D_FF), F32),
                   jax.ShapeDtypeStruct((1, D_FF), F32), jax.ShapeDtypeStruct((1, D_FF), F32)],
        compiler_params=_params(dimension_semantics=("arbitrary",)),
    )(ug, uv, cw, cw, cb, cb, da)


def _dot(a, b, mode):
    return lax.dot_general(a.astype(BF16), b.astype(BF16), _DIMS[mode], preferred_element_type=F32)


@jax.custom_vjp
def mm_nn(a, b):
    return _dot(a, b, 'nn')


@jax.custom_vjp
def mm_nt(a, b):
    return _dot(a, b, 'nt')


@jax.custom_vjp
def mm_tn(a, b):
    return _dot(a, b, 'tn')


mm_nn.defvjp(lambda a, b: (_dot(a, b, 'nn'), (a, b)), lambda r, g: (_dot(g, r[1], 'nt'), _dot(r[0], g, 'tn')))
mm_nt.defvjp(lambda a, b: (_dot(a, b, 'nt'), (a, b)), lambda r, g: (_dot(g, r[1], 'nn'), _dot(g, r[0], 'tn')))
mm_tn.defvjp(lambda a, b: (_dot(a, b, 'tn'), (a, b)), lambda r, g: (_dot(r[1], g, 'nt'), _dot(r[0], g, 'nn')))


def _tri(n, lower):
    r = lax.broadcasted_iota(jnp.int32, (n, n), 0)
    c = lax.broadcasted_iota(jnp.int32, (n, n), 1)
    return jnp.where((r >= c) if lower else (r <= c), 1.0, 0.0).astype(F32)


def _tri_dot(a, lower):
    return jnp.dot(_tri(a.shape[0], lower), a, precision=lax.Precision.HIGHEST, preferred_element_type=F32)


@jax.custom_vjp
def _cumsum_rows(a):
    return _tri_dot(a, True)


_cumsum_rows.defvjp(lambda a: (_tri_dot(a, True), None), lambda _, g: (_tri_dot(g, False),))


def _softplus(x):
    return jnp.maximum(x, 0.0) + jnp.log(1.0 + jnp.exp(-jnp.abs(x)))


def _ssd_chunk(xs, bs, cs, small, dtb, alog, dsk, prev):
    ln = small.shape[0]
    lane = lax.broadcasted_iota(jnp.int32, (ln, LANES), 1)
    lane1 = lax.broadcasted_iota(jnp.int32, (1, LANES), 1)
    sub = lax.broadcasted_iota(jnp.int32, (LANES, ln), 0)
    rowi = lax.broadcasted_iota(jnp.int32, (ln, LANES), 0)
    tril = lax.broadcasted_iota(jnp.int32, (ln, ln), 0) >= lax.broadcasted_iota(jnp.int32, (ln, ln), 1)
    first = lane < SSM_HEAD_DIM
    first1 = lane1 < SSM_HEAD_DIM

    dt = _softplus(small + dtb)
    acs = _cumsum_rows(dt * (-jnp.exp(alog)))
    acs_t = acs.T
    last = jnp.sum(jnp.where(rowi == ln - 1, acs, 0.0), axis=0, keepdims=True)

    def col(a, h):
        return jnp.sum(jnp.where(lane == h, a, 0.0), axis=1, keepdims=True)

    def one(a, h):
        return jnp.sum(jnp.where(lane1 == h, a, 0.0), axis=1, keepdims=True)

    def rowv(at, h):
        return jnp.sum(jnp.where(sub == h, at, 0.0), axis=0, keepdims=True)

    cb = [mm_nt(cs[g], bs[g]) for g in range(SSM_GROUPS)]
    ys, news = [], []
    for j in range(SSM_HEADS // 2):
        g = j // 2
        h0, h1 = 2 * j, 2 * j + 1
        xd = xs[j] * jnp.where(first, col(dt, h0), col(dt, h1))
        yd, st, ea, cd = None, None, [], []
        for h, xdh in ((h0, jnp.where(first, xd, 0.0)), (h1, jnp.where(first, 0.0, xd))):
            ac = col(acs, h)
            la = one(last, h)
            lmat = jnp.exp(jnp.where(tril, ac - rowv(acs_t, h), -jnp.inf))
            yh = mm_nn(cb[g] * lmat, xdh)
            sh = mm_tn(bs[g] * jnp.exp(la - ac), xdh)
            yd = yh if yd is None else yd + yh
            st = sh if st is None else st + sh
            ea.append(jnp.exp(ac))
            cd.append(jnp.exp(la))
        yoff = mm_nn(cs[g], prev[j]) * jnp.where(first, ea[0], ea[1])
        ys.append(yd + yoff + xs[j] * jnp.where(first1, one(dsk, h0), one(dsk, h1)))
        news.append(prev[j] * jnp.where(first1, cd[0], cd[1]) + st)
    return ys, news


N_PAIR = SSM_HEADS // 2


def ssd_fwd(xbc, proj, ptile, gain, l):
    s = xbc.shape[0]
    nch = s // SSM_CHUNK

    def body(xbc_ref, small_ref, p_ref, z_ref, g_ref, y_ref, prev_ref, mix_ref, state_ref):
        @pl.when(pl.program_id(0) == 0)
        def _():
            state_ref[...] = jnp.zeros_like(state_ref)

        xs = [xbc_ref[:, LANES * j:LANES * (j + 1)] for j in range(N_PAIR)]
        bs = [xbc_ref[:, D_SSM + LANES * g:D_SSM + LANES * (g + 1)] for g in range(SSM_GROUPS)]
        cs = [xbc_ref[:, D_SSM + 512 + LANES * g:D_SSM + 512 + LANES * (g + 1)] for g in range(SSM_GROUPS)]
        prev = [state_ref[j] for j in range(N_PAIR)]
        ys, news = _ssd_chunk(xs, bs, cs, small_ref[...], p_ref[0:1, :], p_ref[1:2, :], p_ref[2:3, :], prev)
        gated, ssq = [], None
        for j in range(N_PAIR):
            y_ref[:, LANES * j:LANES * (j + 1)] = ys[j]
            prev_ref[0, j] = prev[j]
            state_ref[j] = news[j]
            t = ys[j] * _silu(z_ref[:, LANES * j:LANES * (j + 1)])
            sq = jnp.sum(t * t, axis=1, keepdims=True)
            gated.append(t)
            ssq = sq if ssq is None else ssq + sq
        inv = lax.rsqrt(ssq / D_SSM + EPS)
        for j in range(N_PAIR):
            mix_ref[:, LANES * j:LANES * (j + 1)] = (gated[j] * inv * g_ref[:, LANES * j:LANES * (j + 1)]).astype(mix_ref.dtype)

    return pl.pallas_call(
        body, name="ssd_fwd", grid=(nch,),
        in_specs=[pl.BlockSpec((SSM_CHUNK, CONV_CH), lambda c: (c, 0)),
                  pl.BlockSpec((SSM_CHUNK, LANES), lambda c: (c, OFF_SMALL // LANES)),
                  pl.BlockSpec((None, 8, LANES), lambda c: (l, 0, 0)),
                  pl.BlockSpec((SSM_CHUNK, D_SSM), lambda c: (c, OFF_Z // D_SSM)),
                  pl.BlockSpec((None, 1, D_SSM), lambda c: (l, 0, 0))],
        out_specs=[pl.BlockSpec((SSM_CHUNK, D_SSM), lambda c: (c, 0)),
                   pl.BlockSpec((1, N_PAIR, SSM_STATE, LANES), lambda c: (c, 0, 0, 0)),
                   pl.BlockSpec((SSM_CHUNK, D_SSM), lambda c: (c, 0))],
        out_shape=[jax.ShapeDtypeStruct((s, D_SSM), F32), jax.ShapeDtypeStruct((nch, N_PAIR, SSM_STATE, LANES), F32),
                   jax.ShapeDtypeStruct((s, D_MIX), BF16)],
        scratch_shapes=[pltpu.VMEM((N_PAIR, SSM_STATE, LANES), F32)],
        compiler_params=_params(dimension_semantics=("arbitrary",)),
    )(xbc, proj, ptile, proj, gain)


def ssd_bwd(xbc, proj, ptile, l, prevs, dy):
    s = xbc.shape[0]
    nch = s // SSM_CHUNK

    def body(xbc_ref, small_ref, p_ref, prev_ref, dy_ref, dxbc_ref, dsmall_ref, dp_ref, dstate_ref):
        @pl.when(pl.program_id(0) == 0)
        def _():
            dstate_ref[...] = jnp.zeros_like(dstate_ref)
            dp_ref[...] = jnp.zeros_like(dp_ref)

        xs = [xbc_ref[:, LANES * j:LANES * (j + 1)] for j in range(N_PAIR)]
        bs = [xbc_ref[:, D_SSM + LANES * g:D_SSM + LANES * (g + 1)] for g in range(SSM_GROUPS)]
        cs = [xbc_ref[:, D_SSM + 512 + LANES * g:D_SSM + 512 + LANES * (g + 1)] for g in range(SSM_GROUPS)]
        prev = [prev_ref[0, j] for j in range(N_PAIR)]
        dys = [dy_ref[:, LANES * j:LANES * (j + 1)] for j in range(N_PAIR)]
        dnew = [dstate_ref[j] for j in range(N_PAIR)]
        _, vjp = jax.vjp(_ssd_chunk, xs, bs, cs, small_ref[...], p_ref[0:1, :], p_ref[1:2, :], p_ref[2:3, :], prev)
        dxs, dbs, dcs, dsmall, ddtb, dalog, ddsk, dprev = vjp((dys, dnew))
        for j in range(N_PAIR):
            dxbc_ref[:, LANES * j:LANES * (j + 1)] = dxs[j]
            dstate_ref[j] = dprev[j]
        for g in range(SSM_GROUPS):
            dxbc_ref[:, D_SSM + LANES * g:D_SSM + LANES * (g + 1)] = dbs[g]
            dxbc_ref[:, D_SSM + 512 + LANES * g:D_SSM + 512 + LANES * (g + 1)] = dcs[g]
        dsmall_ref[...] = dsmall
        dp_ref[0:1, :] += ddtb
        dp_ref[1:2, :] += dalog
        dp_ref[2:3, :] += ddsk

    rev = lambda c: nch - 1 - c
    return pl.pallas_call(
        body, name="ssd_bwd", grid=(nch,),
        in_specs=[pl.BlockSpec((SSM_CHUNK, CONV_CH), lambda c: (rev(c), 0)),
                  pl.BlockSpec((SSM_CHUNK, LANES), lambda c: (rev(c), OFF_SMALL // LANES)),
                  pl.BlockSpec((None, 8, LANES), lambda c: (l, 0, 0)),
                  pl.BlockSpec((1, N_PAIR, SSM_STATE, LANES), lambda c: (rev(c), 0, 0, 0)),
                  pl.BlockSpec((SSM_CHUNK, D_SSM), lambda c: (rev(c), 0))],
        out_specs=[pl.BlockSpec((SSM_CHUNK, CONV_CH), lambda c: (rev(c), 0)),
                   pl.BlockSpec((SSM_CHUNK, LANES), lambda c: (rev(c), 0)),
                   pl.BlockSpec((8, LANES), lambda c: (0, 0))],
        out_shape=[jax.ShapeDtypeStruct((s, CONV_CH), F32), jax.ShapeDtypeStruct((s, LANES), F32),
                   jax.ShapeDtypeStruct((8, LANES), F32)],
        scratch_shapes=[pltpu.VMEM((N_PAIR, SSM_STATE, LANES), F32)],
        compiler_params=_params(dimension_semantics=("arbitrary",)),
    )(xbc, proj, ptile, prevs, dy)


def _rope_tile(t, cosm, sinm):
    lane = lax.broadcasted_iota(jnp.int32, t.shape, 1)
    half = QK_ROPE // 2
    partner = jnp.where(lane < ROPE_LANE0 + half, pltpu.roll(t, LANES - half, 1), pltpu.roll(t, half, 1))
    return t * cosm + partner * sinm


def _in_rope(shape):
    lane = lax.broadcasted_iota(jnp.int32, shape, 1)
    return jnp.logical_and(lane >= ROPE_LANE0, lane < ROPE_LANE0 + QK_ROPE)


def dsmall_bwd(dk, dsmall_ssd, cosm, sinm_neg):
    def fn(dkt, ds, c, sn):
        inrope = _in_rope(ds.shape)
        tot = dkt[:, 0:HEAD_PAD]
        for h in range(1, MLA_HEADS):
            tot = tot + dkt[:, HEAD_PAD * h:HEAD_PAD * (h + 1)]
        tot = jnp.where(inrope, tot, 0.0)
        return ds + jnp.where(inrope, _rope_tile(tot, c, sn), 0.0)

    return rowwise(fn, [(dk, MLA_HEADS * HEAD_PAD, 0), (dsmall_ssd, LANES, 0), (cosm, LANES, 0), (sinm_neg, LANES, 0)],
                   [], [(LANES, BF16)], [], "dsmall_bwd")[0]


ATT_TQ = 512
ATT_SCALE = (QK_NOPE + QK_ROPE) ** -0.5


def _att_scores(qh, kh, q0):
    s = lax.dot_general(qh, kh, _DIMS['nt'], preferred_element_type=F32) * ATT_SCALE
    r = lax.broadcasted_iota(jnp.int32, s.shape, 0) + q0
    c = lax.broadcasted_iota(jnp.int32, s.shape, 1)
    return jnp.where(c <= r, s, -1e30)


def mla_fwd(q, k, v):
    s = q.shape[0]

    def body(q_ref, k_ref, v_ref, o_ref, lse_ref):
        lane = lax.broadcasted_iota(jnp.int32, (ATT_TQ, LANES), 1)

        def block(ib):
            n = ATT_TQ * (ib + 1)
            v_t = v_ref[0:n, :]
            vlane = lax.broadcasted_iota(jnp.int32, v_t.shape, 1)
            o_tot, lse_tot = None, None
            for h in range(2):
                hs = slice(HEAD_PAD * h, HEAD_PAD * (h + 1))
                sc = _att_scores(q_ref[:, hs], k_ref[0:n, hs], ATT_TQ * ib)
                m = jnp.max(sc, axis=1, keepdims=True)
                p = jnp.exp(sc - m)
                l = jnp.sum(p, axis=1, keepdims=True)
                vh = jnp.where((vlane < V_DIM) if h == 0 else (vlane >= V_DIM), v_t, jnp.zeros_like(v_t))
                oh = lax.dot_general(p.astype(BF16), vh, _DIMS['nn'], preferred_element_type=F32) / l
                lse_h = jnp.where((lane < V_DIM) if h == 0 else (lane >= V_DIM), m + jnp.log(l), 0.0)
                o_tot = oh if o_tot is None else o_tot + oh
                lse_tot = lse_h if lse_tot is None else lse_tot + lse_h
            o_ref[...] = o_tot
            lse_ref[...] = lse_tot

        for ib in range(s // ATT_TQ):
            pl.when(pl.program_id(1) == ib)(functools.partial(block, ib))

    tile = pl.BlockSpec((ATT_TQ, LANES), lambda p, i: (i, p))
    return pl.pallas_call(
        body, name="mla_fwd", grid=(MLA_HEADS // 2, s // ATT_TQ),
        in_specs=[pl.BlockSpec((ATT_TQ, 2 * HEAD_PAD), lambda p, i: (i, p)),
                  pl.BlockSpec((s, 2 * HEAD_PAD), lambda p, i: (0, p)),
                  pl.BlockSpec((s, LANES), lambda p, i: (0, p))],
        out_specs=[tile, tile],
        out_shape=[jax.ShapeDtypeStruct((s, MLA_HEADS * V_DIM), F32)] * 2,
        compiler_params=_params(dimension_semantics=("arbitrary", "arbitrary")),
    )(q, k, v)


def mla_bwd(q, k, v, o, lse, do, cosm, sinm_neg):
    s = q.shape[0]

    def body(q_ref, k_ref, v_ref, o_ref, lse_ref, do_ref, c_ref, s_ref, dq_ref, dk_ref, dv_ref):
        i = pl.program_id(1)

        @pl.when(i == 0)
        def _():
            dk_ref[...] = jnp.zeros_like(dk_ref)
            dv_ref[...] = jnp.zeros_like(dv_ref)

        def block(ib):
            n = ATT_TQ * (ib + 1)
            o_t = o_ref[...]
            do_t = do_ref[...]
            lse_t = lse_ref[...]
            v_t = v_ref[0:n, :]
            lane = lax.broadcasted_iota(jnp.int32, do_t.shape, 1)
            for h in range(2):
                hs = slice(HEAD_PAD * h, HEAD_PAD * (h + 1))
                sel = (lane < V_DIM) if h == 0 else (lane >= V_DIM)
                qh = q_ref[:, hs]
                kh = k_ref[0:n, hs]
                doh = jnp.where(sel, do_t, 0.0)
                delta = jnp.sum(doh * o_t, axis=1, keepdims=True)
                lse_h = jnp.max(jnp.where(sel, lse_t, -jnp.inf), axis=1, keepdims=True)
                doh_b = doh.astype(BF16)
                p = jnp.exp(_att_scores(qh, kh, ATT_TQ * ib) - lse_h)
                dv_ref[0:n, :] += lax.dot_general(p.astype(BF16), doh_b, _DIMS['tn'], preferred_element_type=F32)
                dp = lax.dot_general(doh_b, v_t, _DIMS['nt'], preferred_element_type=F32)
                ds = (p * (dp - delta) * ATT_SCALE).astype(BF16)
                dk_ref[0:n, hs] += lax.dot_general(ds, qh, _DIMS['tn'], preferred_element_type=F32)
                dq = lax.dot_general(ds, kh, _DIMS['nn'], preferred_element_type=F32)
                dq_ref[:, hs] = _rope_tile(dq, c_ref[...], s_ref[...]).astype(dq_ref.dtype)

        for ib in range(s // ATT_TQ):
            pl.when(i == ib)(functools.partial(block, ib))

    tile = pl.BlockSpec((ATT_TQ, LANES), lambda p, i: (i, p))
    return pl.pallas_call(
        body, name="mla_bwd", grid=(MLA_HEADS // 2, s // ATT_TQ),
        in_specs=[pl.BlockSpec((ATT_TQ, 2 * HEAD_PAD), lambda p, i: (i, p)),
                  pl.BlockSpec((s, 2 * HEAD_PAD), lambda p, i: (0, p)),
                  pl.BlockSpec((s, LANES), lambda p, i: (0, p)), tile, tile, tile,
                  pl.BlockSpec((ATT_TQ, LANES), lambda p, i: (i, 0)), pl.BlockSpec((ATT_TQ, LANES), lambda p, i: (i, 0))],
        out_specs=[pl.BlockSpec((ATT_TQ, 2 * HEAD_PAD), lambda p, i: (i, p)),
                   pl.BlockSpec((s, 2 * HEAD_PAD), lambda p, i: (0, p)),
                   pl.BlockSpec((s, LANES), lambda p, i: (0, p))],
        out_shape=[jax.ShapeDtypeStruct((s, MLA_HEADS * HEAD_PAD), BF16),
                   jax.ShapeDtypeStruct((s, MLA_HEADS * HEAD_PAD), F32),
                   jax.ShapeDtypeStruct((s, MLA_HEADS * V_DIM), F32)],
        compiler_params=_params(dimension_semantics=("arbitrary", "arbitrary")),
    )(q, k, v, o, lse, do, cosm, sinm_neg)


MEM_TQ = 256
MEM_SCALE = MEM_HEAD_DIM ** -0.5


def _mem_probs(qh, kh):
    s = lax.dot_general(qh, kh, _DIMS['nt'], preferred_element_type=F32) * MEM_SCALE
    p = jnp.exp(s - jnp.max(s, axis=1, keepdims=True))
    return p / jnp.sum(p, axis=1, keepdims=True)


def mem_fwd(q, k, v):
    s = q.shape[0]

    def body(q_ref, k_ref, v_ref, o_ref):
        for h in range(MEM_HEADS):
            sl = slice(MEM_HEAD_DIM * h, MEM_HEAD_DIM * (h + 1))
            p = _mem_probs(q_ref[:, sl], k_ref[:, sl])
            o_ref[:, sl] = lax.dot_general(p.astype(BF16), v_ref[:, sl], _DIMS['nn'],
                                           preferred_element_type=F32).astype(o_ref.dtype)

    full = pl.BlockSpec((MEM_LEN, D_MODEL), lambda i: (0, 0))
    return pl.pallas_call(
        body, name="mem_fwd", grid=(s // MEM_TQ,),
        in_specs=[pl.BlockSpec((MEM_TQ, D_MODEL), lambda i: (i, 0)), full, full],
        out_specs=pl.BlockSpec((MEM_TQ, D_MODEL), lambda i: (i, 0)),
        out_shape=jax.ShapeDtypeStruct((s, D_MODEL), BF16),
        compiler_params=_params(dimension_semantics=("arbitrary",)),
    )(q, k, v)


def mem_bwd(q, k, v, do):
    s = q.shape[0]

    def body(q_ref, k_ref, v_ref, do_ref, dq_ref, dk_ref, dv_ref):
        @pl.when(pl.program_id(0) == 0)
        def _():
            dk_ref[...] = jnp.zeros_like(dk_ref)
            dv_ref[...] = jnp.zeros_like(dv_ref)

        for h in range(MEM_HEADS):
            sl = slice(MEM_HEAD_DIM * h, MEM_HEAD_DIM * (h + 1))
            qh, kh, vh = q_ref[:, sl], k_ref[:, sl], v_ref[:, sl]
            doh = do_ref[:, sl].astype(BF16)
            p = _mem_probs(qh, kh)
            dv_ref[:, sl] += lax.dot_general(p.astype(BF16), doh, _DIMS['tn'], preferred_element_type=F32)
            dp = lax.dot_general(doh, vh, _DIMS['nt'], preferred_element_type=F32)
            ds = (p * (dp - jnp.sum(p * dp, axis=1, keepdims=True)) * MEM_SCALE).astype(BF16)
            dq_ref[:, sl] = lax.dot_general(ds, kh, _DIMS['nn'], preferred_element_type=F32).astype(dq_ref.dtype)
            dk_ref[:, sl] += lax.dot_general(ds, qh, _DIMS['tn'], preferred_element_type=F32)

    full = pl.BlockSpec((MEM_LEN, D_MODEL), lambda i: (0, 0))
    row = pl.BlockSpec((MEM_TQ, D_MODEL), lambda i: (i, 0))
    return pl.pallas_call(
        body, name="mem_bwd", grid=(s // MEM_TQ,),
        in_specs=[row, full, full, row], out_specs=[row, full, full],
        out_shape=[jax.ShapeDtypeStruct((s, D_MODEL), BF16), jax.ShapeDtypeStruct((MEM_LEN, D_MODEL), F32),
                   jax.ShapeDtypeStruct((MEM_LEN, D_MODEL), F32)],
        compiler_params=_params(dimension_semantics=("arbitrary",)),
    )(q, k, v, do)


def _gate_norm(y, z, g):
    return _rms(y * _silu(z), g)


def loss_head(x, g, target):
    def fn(xt, tt, gt):
        def f(x_, g_):
            err = _rms(x_, g_) - tt
            return 0.5 * jnp.sum(jnp.mean(err * err, axis=-1))

        lv, (dx, dg) = jax.value_and_grad(f, argnums=(0, 1))(xt, gt)
        return dx, dg, jnp.full((1, LANES), lv, F32)

    return rowwise(fn, [(x, D_MODEL, 0), (target, D_MODEL, 0)], [g], [(D_MODEL, F32)],
                   [((1, D_MODEL), F32), ((1, LANES), F32)], "loss_head")


def _proj_runs(d):
    lo, hi = (D_IN // N_DEV) * d, (D_IN // N_DEV) * (d + 1)
    runs = []
    for a, b, new in PROJ_SEGS:
        s0, s1 = max(a, lo), min(b, hi)
        if s0 < s1:
            runs.append((s0 - lo, new + s0 - a, s1 - s0))
    return runs


LAYOUT_TM = 256


def assemble_proj(g):
    def body(g_ref, o_ref):
        o_ref[:, OFF_SMALL:OFF_SMALL + LANES] = jnp.zeros((LAYOUT_TM, LANES), o_ref.dtype)
        for d in range(N_DEV):
            for src, dst, n in _proj_runs(d):
                o_ref[:, dst:dst + n] = g_ref[d, :, src:src + n]

    return pl.pallas_call(
        body, name="assemble_proj", grid=(D_MODEL // LAYOUT_TM,),
        in_specs=[pl.BlockSpec((N_DEV, LAYOUT_TM, D_IN // N_DEV), lambda i: (0, i, 0))],
        out_specs=pl.BlockSpec((LAYOUT_TM, PROJ_W), lambda i: (i, 0)),
        out_shape=jax.ShapeDtypeStruct((D_MODEL, PROJ_W), g.dtype),
        compiler_params=_params(dimension_semantics=("arbitrary",)),
    )(g)


def extract_proj(dz, dxbc, dcq, dsmall, dckv):
    pieces = [(OFF_Z, 1024), (OFF_XBC, 2048), (OFF_CQ, Q_LORA), (OFF_SMALL, LANES), (OFF_CKV, KV_LORA)]

    def body(*refs):
        o_ref = refs[-1]
        for d in range(N_DEV):
            for src, dst, n in _proj_runs(d):
                for p, (off, w) in enumerate(pieces):
                    if off <= dst < off + w:
                        o_ref[d, :, src:src + n] = refs[p][:, dst - off:dst - off + n].astype(o_ref.dtype)

    return pl.pallas_call(
        body, name="extract_proj", grid=(D_MODEL // LAYOUT_TM,),
        in_specs=[pl.BlockSpec((LAYOUT_TM, w), lambda i: (i, 0)) for _, w in pieces],
        out_specs=pl.BlockSpec((N_DEV, LAYOUT_TM, D_IN // N_DEV), lambda i: (0, i, 0)),
        out_shape=jax.ShapeDtypeStruct((N_DEV, D_MODEL, D_IN // N_DEV), BF16),
        compiler_params=_params(dimension_semantics=("arbitrary",)),
    )(dz, dxbc, dcq, dsmall, dckv)


_QW = QK_NOPE + QK_ROPE


def assemble_uq(g):
    def body(g_ref, o_ref):
        o_ref[...] = jnp.zeros_like(o_ref)
        for d in range(N_DEV):
            for e in range(2):
                dst = HEAD_PAD * (2 * d + e)
                o_ref[:, dst:dst + _QW] = g_ref[d, :, _QW * e:_QW * (e + 1)]

    return pl.pallas_call(
        body, name="assemble_uq", grid=(1,),
        in_specs=[pl.BlockSpec((N_DEV, Q_LORA, 2 * _QW), lambda i: (0, 0, 0))],
        out_specs=pl.BlockSpec((Q_LORA, MLA_HEADS * HEAD_PAD), lambda i: (0, 0)),
        out_shape=jax.ShapeDtypeStruct((Q_LORA, MLA_HEADS * HEAD_PAD), g.dtype),
        compiler_params=_params(dimension_semantics=("arbitrary",)),
    )(g)


def extract_uq(dw):
    def body(w_ref, o_ref):
        for d in range(N_DEV):
            for e in range(2):
                src = HEAD_PAD * (2 * d + e)
                o_ref[d, :, _QW * e:_QW * (e + 1)] = w_ref[:, src:src + _QW].astype(o_ref.dtype)

    return pl.pallas_call(
        body, name="extract_uq", grid=(1,),
        in_specs=[pl.BlockSpec((Q_LORA, MLA_HEADS * HEAD_PAD), lambda i: (0, 0))],
        out_specs=pl.BlockSpec((N_DEV, Q_LORA, 2 * _QW), lambda i: (0, 0, 0)),
        out_shape=jax.ShapeDtypeStruct((N_DEV, Q_LORA, 2 * _QW), BF16),
        compiler_params=_params(dimension_semantics=("arbitrary",)),
    )(dw)


def assemble_ukv(g):
    def body(g_ref, kn_ref, v_ref):
        kn_ref[...] = jnp.zeros_like(kn_ref)
        for d in range(N_DEV):
            for e in range(2):
                h = 2 * d + e
                kn_ref[:, HEAD_PAD * h:HEAD_PAD * h + QK_NOPE] = g_ref[d, :, 128 * e:128 * e + QK_NOPE]
                v_ref[:, V_DIM * h:V_DIM * (h + 1)] = g_ref[d, :, 128 * e + QK_NOPE:128 * (e + 1)]

    return pl.pallas_call(
        body, name="assemble_ukv", grid=(1,),
        in_specs=[pl.BlockSpec((N_DEV, KV_LORA, 256), lambda i: (0, 0, 0))],
        out_specs=[pl.BlockSpec((KV_LORA, MLA_HEADS * HEAD_PAD), lambda i: (0, 0)),
                   pl.BlockSpec((KV_LORA, MLA_HEADS * V_DIM), lambda i: (0, 0))],
        out_shape=[jax.ShapeDtypeStruct((KV_LORA, MLA_HEADS * HEAD_PAD), g.dtype),
                   jax.ShapeDtypeStruct((KV_LORA, MLA_HEADS * V_DIM), g.dtype)],
        compiler_params=_params(dimension_semantics=("arbitrary",)),
    )(g)


def extract_ukv(dkn, dv):
    def body(kn_ref, v_ref, o_ref):
        for d in range(N_DEV):
            for e in range(2):
                h = 2 * d + e
                o_ref[d, :, 128 * e:128 * e + QK_NOPE] = kn_ref[:, HEAD_PAD * h:HEAD_PAD * h + QK_NOPE].astype(o_ref.dtype)
                o_ref[d, :, 128 * e + QK_NOPE:128 * (e + 1)] = v_ref[:, V_DIM * h:V_DIM * (h + 1)].astype(o_ref.dtype)

    return pl.pallas_call(
        body, name="extract_ukv", grid=(1,),
        in_specs=[pl.BlockSpec((KV_LORA, MLA_HEADS * HEAD_PAD), lambda i: (0, 0)),
                  pl.BlockSpec((KV_LORA, MLA_HEADS * V_DIM), lambda i: (0, 0))],
        out_specs=pl.BlockSpec((N_DEV, KV_LORA, 256), lambda i: (0, 0, 0)),
        out_shape=jax.ShapeDtypeStruct((N_DEV, KV_LORA, 256), BF16),
        compiler_params=_params(dimension_semantics=("arbitrary",)),
    )(dkn, dv)


_UPW = 2 * D_FF // N_DEV


def assemble_up(g):
    def body(g_ref, wg_ref, wv_ref):
        for d in range(N_DEV):
            ref = wg_ref if d < N_DEV // 2 else wv_ref
            off = _UPW * (d % (N_DEV // 2))
            ref[:, off:off + _UPW] = g_ref[d]

    half = pl.BlockSpec((LAYOUT_TM, D_FF), lambda i: (i, 0))
    return pl.pallas_call(
        body, name="assemble_up", grid=(D_MODEL // LAYOUT_TM,),
        in_specs=[pl.BlockSpec((N_DEV, LAYOUT_TM, _UPW), lambda i: (0, i, 0))],
        out_specs=[half, half], out_shape=[jax.ShapeDtypeStruct((D_MODEL, D_FF), g.dtype)] * 2,
        compiler_params=_params(dimension_semantics=("arbitrary",)),
    )(g)


def extract_up(dwg, dwv):
    def body(wg_ref, wv_ref, o_ref):
        for d in range(N_DEV):
            ref = wg_ref if d < N_DEV // 2 else wv_ref
            off = _UPW * (d % (N_DEV // 2))
            o_ref[d] = ref[:, off:off + _UPW].astype(o_ref.dtype)

    half = pl.BlockSpec((LAYOUT_TM, D_FF), lambda i: (i, 0))
    return pl.pallas_call(
        body, name="extract_up", grid=(D_MODEL // LAYOUT_TM,), in_specs=[half, half],
        out_specs=pl.BlockSpec((N_DEV, LAYOUT_TM, _UPW), lambda i: (0, i, 0)),
        out_shape=jax.ShapeDtypeStruct((N_DEV, D_MODEL, _UPW), BF16),
        compiler_params=_params(dimension_semantics=("arbitrary",)),
    )(dwg, dwv)


MESH = pl.DeviceIdType.MESH
ANY = pl.BlockSpec(memory_space=pl.ANY)


def _place():
    mx, my, mc = lax.axis_index("x"), lax.axis_index("y"), lax.axis_index("c")
    return mx, my, mc, [(1 - mx, my), (mx, 1 - my), (1 - mx, 1 - my)]


def all_gather_blocks(xs, first_only=()):
    n = len(xs)

    def body(*refs):
        x_refs, out_refs = refs[:n], refs[n:2 * n]
        send_sems, recv_sems, local_sems = refs[2 * n:]
        mx, my, mc, chips = _place()
        me, sibling = (mx, my, mc), (mx, my, 1 - mc)
        x_refs = [x_refs[t].at[0] if t in first_only else x_refs[t] for t in range(n)]

        def rows(t, px, py, pc):
            dev = 4 * px + 2 * py + pc
            return out_refs[t].at[dev] if t in first_only else out_refs[t].at[:, dev]

        def copy(t, k, block, to, src=None):
            return pltpu.make_async_remote_copy(
                src_ref=rows(t, *block) if src is None else src, dst_ref=rows(t, *block),
                send_sem=send_sems.at[t, k], recv_sem=recv_sems.at[t, k], device_id=to, device_id_type=MESH)

        mine = [pltpu.make_async_copy(x_refs[t], rows(t, *me), local_sems.at[t]) for t in range(n)]
        for cp in mine:
            cp.start()
        first = []
        for t in range(n):
            first.append(copy(t, 0, me, sibling, src=x_refs[t]))
            first += [copy(t, 1 + j, me, (*chip, mc), src=x_refs[t]) for j, chip in enumerate(chips)]
        for cp in first:
            cp.start()
        passed = []
        for j, chip in enumerate(chips):
            for t in range(n):
                copy(t, 1 + j, (*chip, mc), me).wait_recv()
                cp = copy(t, 4 + j, (*chip, mc), sibling)
                cp.start()
                passed.append(cp)
        for t in range(n):
            copy(t, 0, sibling, me).wait_recv()
            for j, chip in enumerate(chips):
                copy(t, 4 + j, (*chip, 1 - mc), me).wait_recv()
        for cp in first + passed:
            cp.wait_send()
        for cp in mine:
            cp.wait()

    return pl.pallas_call(
        body, name="all_gather_blocks",
        out_shape=[jax.ShapeDtypeStruct(((N_DEV,) if t in first_only else (x.shape[0], N_DEV)) + x.shape[1:], x.dtype)
                   for t, x in enumerate(xs)],
        in_specs=[ANY] * n, out_specs=[ANY] * n,
        scratch_shapes=[pltpu.SemaphoreType.DMA((n, 7)), pltpu.SemaphoreType.DMA((n, 7)), pltpu.SemaphoreType.DMA((n,))],
    )(*xs)


HBM = pl.BlockSpec(memory_space=pltpu.HBM)
SEM = pl.BlockSpec(memory_space=pltpu.SEMAPHORE)
EFFECT = pltpu.SideEffectType.DATAFLOW_SIDE_EFFECTING
ALL_DEVICES = [(px, py, pc) for px in range(2) for py in range(2) for pc in range(2)]


def _hbm(x):
    return pltpu.with_memory_space_constraint(x, pltpu.HBM)


def _split_start(body, name, srcs, lands, after=None):
    ns, n = len(srcs), len(lands)
    extra = [after] if after is not None else []

    def full_body(*refs):
        sems = ns + n + len(extra)
        body(refs[:ns], refs[ns:ns + n], refs[sems], refs[sems + 1])
        refs[-1][...] = jnp.zeros_like(refs[-1])

    res = pl.pallas_call(
        full_body, name=name,
        out_shape=(pltpu.SemaphoreType.DMA((n,)), pltpu.SemaphoreType.DMA((n,)),
                   *[pltpu.HBM(x.shape, x.dtype) for x in srcs], *[pltpu.HBM(x.shape, x.dtype) for x in lands],
                   jax.ShapeDtypeStruct((8, LANES), F32)),
        in_specs=[HBM] * (ns + n) + [ANY] * len(extra),
        out_specs=(SEM, SEM, *[HBM] * (ns + n), pl.BlockSpec(memory_space=pltpu.VMEM)),
        input_output_aliases={i: 2 + i for i in range(ns + n)},
        compiler_params=pltpu.CompilerParams(has_side_effects=EFFECT),
    )(*[_hbm(x) for x in srcs], *[_hbm(x) for x in lands], *extra)
    return res[0], res[1], list(res[2:2 + ns]), list(res[2 + ns:2 + ns + n]), res[-1]


def _split_wait(name, send_sems, recv_sems, srcs, lands, after, sent, landed):
    ns, n = len(srcs), len(lands)

    def body(*refs):
        src_refs, land_refs, ssem, rsem = refs[:ns], refs[ns:ns + n], refs[ns + n], refs[ns + n + 1]
        mx, my, mc, _ = _place()
        for t in range(n):
            out = sent(src_refs[t] if ns else None, land_refs[t])
            inn = landed(land_refs[t])
            pltpu.make_async_remote_copy(src_ref=out, dst_ref=out, send_sem=ssem.at[t], recv_sem=rsem.at[t],
                                         device_id=(mx, my, mc), device_id_type=MESH).wait_send()
            pltpu.make_async_remote_copy(src_ref=inn, dst_ref=inn, send_sem=ssem.at[t], recv_sem=rsem.at[t],
                                         device_id=(mx, my, mc), device_id_type=MESH).wait_recv()

    res = pl.pallas_call(
        body, name=name,
        out_shape=(*[pltpu.HBM(x.shape, x.dtype) for x in srcs], *[pltpu.HBM(x.shape, x.dtype) for x in lands]),
        in_specs=[HBM] * (ns + n) + [SEM, SEM, ANY], out_specs=[HBM] * (ns + n),
        input_output_aliases={i: i for i in range(ns + n)},
        compiler_params=pltpu.CompilerParams(has_side_effects=EFFECT),
    )(*srcs, *lands, send_sems, recv_sems, after)
    return list(res[:ns]), list(res[ns:])


FIRST_HOP = 5
SECOND_HOP = 3


def gather_start(srcs, l, tag, after=None):
    lands = [lax.empty((N_DEV,) + x.shape[1:], x.dtype) for x in srcs]

    def body(src_refs, land_refs, send_sems, recv_sems):
        mx, my, mc, chips = _place()
        me = 4 * mx + 2 * my + mc
        for t in range(len(srcs)):
            for to in [(mx, my, mc), (mx, my, 1 - mc)] + [(cx, cy, mc) for cx, cy in chips]:
                pltpu.make_async_remote_copy(
                    src_ref=src_refs[t].at[l], dst_ref=land_refs[t].at[me], send_sem=send_sems.at[t],
                    recv_sem=recv_sems.at[t], device_id=to, device_id_type=MESH).start()

    return _split_start(body, "gather_start_%d%s" % (l, tag), srcs, lands, after=after)


def gather_wait(l, tag, send_sems, recv_sems, srcs, lands, after):
    hop = lambda d: d.at[pl.ds(0, FIRST_HOP)]
    return _split_wait("gather_wait_%d%s" % (l, tag), send_sems, recv_sems, srcs, lands, after,
                       sent=lambda s, d: hop(d), landed=hop)


def gather_pass_start(lands, l, tag):
    def body(src_refs, land_refs, send_sems, recv_sems):
        mx, my, mc, chips = _place()
        for t in range(len(lands)):
            for cx, cy in chips:
                slot = land_refs[t].at[4 * cx + 2 * cy + mc]
                pltpu.make_async_remote_copy(
                    src_ref=slot, dst_ref=slot, send_sem=send_sems.at[t], recv_sem=recv_sems.at[t],
                    device_id=(mx, my, 1 - mc), device_id_type=MESH).start()

    send_sems, recv_sems, _, lands, tie = _split_start(body, "gather_pass_start_%d%s" % (l, tag), [], lands)
    return send_sems, recv_sems, lands, tie


def gather_pass_wait(l, tag, send_sems, recv_sems, lands, after):
    hop = lambda d: d.at[pl.ds(0, SECOND_HOP)]
    return _split_wait("gather_pass_wait_%d%s" % (l, tag), send_sems, recv_sems, [], lands, after,
                       sent=lambda s, d: hop(d), landed=hop)[1]


def small_gather_start(rows):
    def body(src_refs, land_refs, send_sems, recv_sems):
        mx, my, mc, _ = _place()
        for to in ALL_DEVICES:
            pltpu.make_async_remote_copy(
                src_ref=src_refs[0], dst_ref=land_refs[0].at[4 * mx + 2 * my + mc], send_sem=send_sems.at[0],
                recv_sem=recv_sems.at[0], device_id=to, device_id_type=MESH).start()

    return _split_start(body, "small_gather_start", [rows], [lax.empty((N_DEV,) + rows.shape, rows.dtype)])


def small_gather_wait(send_sems, recv_sems, srcs, lands, after):
    return _split_wait("small_gather_wait", send_sems, recv_sems, srcs, lands, after,
                       sent=lambda s, d: d, landed=lambda d: d)[1][0]


def grad_exchange_start(es, lands, l, tag, after=None):
    def body(e_refs, land_refs, send_sems, recv_sems):
        mx, my, mc, _ = _place()
        me = 4 * mx + 2 * my + mc
        for t in range(len(es)):
            for px, py, pc in ALL_DEVICES:
                pltpu.make_async_remote_copy(
                    src_ref=e_refs[t].at[4 * px + 2 * py + pc], dst_ref=land_refs[t].at[l, me], send_sem=send_sems.at[t],
                    recv_sem=recv_sems.at[t], device_id=(px, py, pc), device_id_type=MESH).start()

    return _split_start(body, "grad_exchange_start_%d%s" % (l, tag), es, lands, after=after)


def grad_exchange_wait(l, tag, send_sems, recv_sems, es, lands, after):
    return _split_wait("grad_exchange_wait_%d%s" % (l, tag), send_sems, recv_sems, es, lands, after,
                       sent=lambda s, d: s, landed=lambda d: d.at[l])


def _adam(g, w, m, v):
    nm = ADAM_B1 * m + (1.0 - ADAM_B1) * g
    nv = ADAM_B2 * v + (1.0 - ADAM_B2) * jnp.square(g)
    m_hat = nm / (1.0 - ADAM_B1 ** ADAM_STEP)
    v_hat = nv / (1.0 - ADAM_B2 ** ADAM_STEP)
    return -ADAM_LR * (m_hat / (jnp.sqrt(v_hat) + ADAM_EPS) + ADAM_WD * w), nm, nv


def adamw_big(parts, w, m, v, name, tie):
    depth, _, a, b = parts.shape
    ta = _row_tile(a)

    def body(p_ref, w_ref, m_ref, v_ref, tie_ref, g_ref, d_ref, nm_ref, nv_ref):
        g = p_ref[0].astype(F32)
        for k in range(1, N_DEV):
            g = g + p_ref[k].astype(F32)
        g_ref[...] = g
        d_ref[...], nm_ref[...], nv_ref[...] = _adam(g, w_ref[...], m_ref[...], v_ref[...])

    blk = pl.BlockSpec((None, ta, b), lambda l, i: (l, i, 0))
    return pl.pallas_call(
        body, name=name, grid=(depth, a // ta),
        in_specs=[pl.BlockSpec((None, N_DEV, ta, b), lambda l, i: (l, 0, i, 0)), blk, blk, blk, ANY], out_specs=[blk] * 4,
        out_shape=[jax.ShapeDtypeStruct((depth, a, b), F32)] * 4,
        compiler_params=_params(dimension_semantics=("arbitrary", "arbitrary")),
    )(parts, w, m, v, tie)


SMALL_VIEW = {'norm_mix': (DEPTH, 1024), 'ssm_norm': (DEPTH, 1024), 'attn_out_norm': (DEPTH, 1024),
              'norm_mem_q': (DEPTH, 1024), 'norm_mem_kv': (DEPTH, 1024), 'norm_ffn': (DEPTH, 1024),
              'q_norm': (DEPTH, 384), 'kv_norm': (DEPTH, 256), 'ssm_conv_b': (DEPTH, 2048), 'ffn_conv_b': (DEPTH, 5632),
              'dt_bias': (DEPTH, SSM_HEADS), 'a_log': (DEPTH, SSM_HEADS), 'd_skip': (DEPTH, SSM_HEADS),
              'ssm_conv_w': (DEPTH, SSM_CONV * CONV_CH // N_DEV), 'ffn_conv_w': (DEPTH, FFN_CONV * 2 * D_FF // N_DEV),
              'final_norm': (1, 1024)}
SMALL_NAMES = list(SMALL_VIEW)
SMALL_SHARDED = {'ssm_conv_w': (SSM_CONV, CONV_CH // N_DEV, CONV_CH), 'ffn_conv_w': (FFN_CONV, 2 * D_FF // N_DEV, 2 * D_FF)}


def adamw_small(gathered, ws, ms, vs, tie):
    nsm = len(SMALL_NAMES)

    def body(*refs):
        g8_ref = refs[0]
        w_refs, m_refs, v_refs = refs[1:1 + nsm], refs[1 + nsm:1 + 2 * nsm], refs[1 + 2 * nsm:1 + 3 * nsm]
        outs = refs[2 + 3 * nsm:2 + 7 * nsm]
        sum_ref = refs[2 + 7 * nsm]
        shard_bufs = refs[3 + 7 * nsm:]
        tot = g8_ref[0]
        for d in range(1, N_DEV):
            tot = tot + g8_ref[d]
        sum_ref[...] = tot
        mx, my, mc, _ = _place()
        dev = 4 * mx + 2 * my + mc

        def update(i, g):
            d, nm, nv = _adam(g, w_refs[i][...], m_refs[i][...], v_refs[i][...])
            outs[i][...] = g
            outs[nsm + i][...] = d
            outs[2 * nsm + i][...] = nm
            outs[3 * nsm + i][...] = nv

        for i, name in enumerate(SMALL_NAMES):
            rows, cols = SMALL_VIEW[name]
            off = SMALL_OFF[name]
            if name in SMALL_SHARDED:
                taps, per, full = SMALL_SHARDED[name]
                buf = shard_bufs[list(SMALL_SHARDED).index(name)]
                for d in range(N_DEV):
                    @pl.when(dev == d)
                    def _(d=d, taps=taps, per=per, full=full, off=off, buf=buf):
                        for k in range(taps):
                            buf[:, per * k:per * (k + 1)] = sum_ref[:, off + full * k + per * d:off + full * k + per * (d + 1)]
                update(i, buf[...])
            else:
                update(i, sum_ref[0:rows, off:off + cols])

    views = [jax.ShapeDtypeStruct(SMALL_VIEW[n], F32) for n in SMALL_NAMES]
    vmem = pl.BlockSpec(memory_space=pltpu.VMEM)
    res = pl.pallas_call(
        body, name="adamw_small", out_shape=views * 4, in_specs=[vmem] * (1 + 3 * nsm) + [ANY],
        out_specs=[vmem] * (4 * nsm),
        scratch_shapes=[pltpu.VMEM((DEPTH, SMALL_W), F32)] + [pltpu.VMEM(SMALL_VIEW[n], F32) for n in SMALL_SHARDED],
        compiler_params=_params(),
    )(gathered, *[ws[n] for n in SMALL_NAMES], *[ms[n] for n in SMALL_NAMES], *[vs[n] for n in SMALL_NAMES], tie)
    return [dict(zip(SMALL_NAMES, res[k * nsm:(k + 1) * nsm])) for k in range(4)]


def _layer_weights(gathered):
    w = {}
    for n, g in gathered.items():
        if n == 'w_in':
            w['w_proj'] = assemble_proj(g)
        elif n == 'w_uq':
            w['w_uq'] = assemble_uq(g)
        elif n == 'w_ukv':
            w['w_kn'], w['w_v'] = assemble_ukv(g)
        elif n == 'w_up':
            w['w_g'], w['w_vv'] = assemble_up(g)
        else:
            w[n] = g.reshape(N_DEV * BIG[n][0], BIG[n][1])
    return w


def _rope_post(acc, row_tiles, full_tiles, o_refs):
    for h in range(acc.shape[1] // HEAD_PAD):
        sl = slice(HEAD_PAD * h, HEAD_PAD * (h + 1))
        o_refs[0][:, sl] = _rope_tile(acc[:, sl], row_tiles[0], row_tiles[1]).astype(o_refs[0].dtype)


def _k_post(acc, row_tiles, full_tiles, o_refs):
    small, c, sn = row_tiles
    inrope = _in_rope(small.shape)
    kpe = jnp.where(inrope, _rope_tile(jnp.where(inrope, small, 0.0), c, sn), 0.0)
    for h in range(acc.shape[1] // HEAD_PAD):
        sl = slice(HEAD_PAD * h, HEAD_PAD * (h + 1))
        o_refs[0][:, sl] = (acc[:, sl] + kpe).astype(o_refs[0].dtype)


PROJ_TAIL = PROJ_W - OFF_CQ


def _proj_post(acc, row_tiles, full_tiles, o_refs):
    o_refs[0][...] = acc

    @pl.when(pl.program_id(1) == PROJ_W // PROJ_TAIL - 1)
    def _():
        o_refs[1][...] = _rms(acc[:, 0:Q_LORA], full_tiles[0]).astype(o_refs[1].dtype)
        o_refs[2][...] = _rms(acc[:, OFF_CKV - OFF_CQ:OFF_CKV - OFF_CQ + KV_LORA], full_tiles[1]).astype(o_refs[2].dtype)


def _norm_post(acc, row_tiles, full_tiles, o_refs):
    o_refs[0][...] = acc
    o_refs[1][...] = _rms(acc, full_tiles[0]).astype(o_refs[1].dtype)


def _norm_bwd_post(acc, row_tiles, full_tiles, o_refs):
    _, vjp = jax.vjp(_rms, row_tiles[0], full_tiles[0])
    dx, dg = vjp(acc)
    o_refs[0][...] = dx + row_tiles[1]
    o_refs[1][...] += dg


def _latent_norm_bwd_post(acc, row_tiles, full_tiles, o_refs):
    _, vjp = jax.vjp(_rms, row_tiles[0], full_tiles[0])
    dx, dg = vjp(acc)
    o_refs[0][...] = dx.astype(o_refs[0].dtype)
    o_refs[1][...] += dg


def _mix_bwd_post(acc, row_tiles, full_tiles, o_refs):
    y, z, o = row_tiles
    _, vjp = jax.vjp(_gate_norm, y, z, full_tiles[0])
    dy, dz, dg = vjp(acc[:, :D_SSM])
    _, vjp_o = jax.vjp(_rms, o, full_tiles[1])
    do, dg_o = vjp_o(acc[:, D_SSM:])
    o_refs[0][...] = dy
    o_refs[1][...] = dz.astype(o_refs[1].dtype)
    o_refs[2][...] = do
    o_refs[3][...] += dg
    o_refs[4][...] += dg_o


def layer_fwd(x0, h1, mem, cosm, sinm, w, sm, l, tie=None):
    gain = lambda n: (sm[n], l)
    sv = dict(x0=x0)
    sv['h1'] = h1 if h1 is not None else rmsnorm_fwd(x0, gain('norm_mix'), "norm_mix_fwd", tie=tie)
    proj, sv['cqn'], sv['ckvn'] = matmul(
        [(sv['h1'], w['w_proj'])], 'nn', F32, "proj_fwd", tie=tie if h1 is not None else None, post=_proj_post,
        fulls=[gain('q_norm'), gain('kv_norm')], outs=[F32, (Q_LORA, BF16), (KV_LORA, BF16)], tn_fixed=PROJ_TAIL)
    sv['proj'] = proj
    sv['xbc'] = ssm_conv_fwd(proj, sm['ssm_conv_w'], sm['ssm_conv_b'], l)
    sv['y'], sv['prevs'], mix = ssd_fwd(sv['xbc'], proj, sm['ptile'], sm['ssm_norm'], l)
    sv['q'] = matmul([(sv['cqn'], w['w_uq'])], 'nn', BF16, "uq_fwd", post=_rope_post, rows=[cosm, sinm])
    sv['k'] = matmul([(sv['ckvn'], w['w_kn'])], 'nn', BF16, "kn_fwd", post=_k_post,
                     rows=[(proj, LANES, OFF_SMALL // LANES), cosm, sinm])
    sv['v'] = matmul([(sv['ckvn'], w['w_v'])], 'nn', BF16, "v_fwd")
    sv['o'], sv['lse'] = mla_fwd(sv['q'], sv['k'], sv['v'])
    mix = sv['mix'] = rmsnorm_fwd(sv['o'], gain('attn_out_norm'), "attn_out_norm_fwd", out=(D_SSM, BF16, D_MIX, 1),
                                  into=(mix, 0))
    x1, sv['hq'] = matmul([(mix, w['w_out'])], 'nn', F32, "out_fwd", add=x0, post=_norm_post,
                          fulls=[gain('norm_mem_q')], outs=[F32, BF16], full_n=True)
    sv['x1'] = x1
    sv['mn'] = rmsnorm_fwd(mem, gain('norm_mem_kv'), "norm_mem_kv_fwd")
    if 'later' in w:
        w.update(w.pop('later')(sv['hq']))
    sv['mq'] = matmul([(sv['hq'], w['w_mq'])], 'nn', BF16, "mq_fwd")
    sv['mk'] = matmul([(sv['mn'], w['w_mk'])], 'nn', BF16, "mk_fwd")
    sv['mv'] = matmul([(sv['mn'], w['w_mv'])], 'nn', BF16, "mv_fwd")
    sv['om'] = mem_fwd(sv['mq'], sv['mk'], sv['mv'])
    x2, sv['h3'] = matmul([(sv['om'], w['w_mo'])], 'nn', F32, "mo_fwd", add=x1, post=_norm_post,
                          fulls=[gain('norm_ffn')], outs=[F32, BF16], full_n=True)
    sv['x2'] = x2
    tie_ffn = w.pop('prefetch')(sv['h3']) if 'prefetch' in w else None
    sv['ug'] = matmul([(sv['h3'], w['w_g'])], 'nn', F32, "up_g_fwd", tie=tie_ffn)
    sv['uv'] = matmul([(sv['h3'], w['w_vv'])], 'nn', F32, "up_v_fwd")
    sv['a'] = ffn_act_fwd(sv['ug'], sv['uv'], sm['ffn_conv_w'], sm['ffn_conv_b'], l)
    if l + 1 < DEPTH:
        x3, h1_next = matmul([(sv['a'], w['w_down'])], 'nn', F32, "down_fwd", add=x2, post=_norm_post,
                             fulls=[(sm['norm_mix'], l + 1)], outs=[F32, BF16], full_n=True)
    else:
        x3, h1_next = matmul([(sv['a'], w['w_down'])], 'nn', F32, "down_fwd_last", add=x2), None
    return x3, h1_next, sv


EARLY_GRADS = ('w_down', 'w_up', 'w_mo', 'w_mq', 'w_mk', 'w_mv', 'w_out')
LATE_GRADS = ('w_uq', 'w_ukv', 'w_in')


def layer_bwd(dx3, mem, cosm, sinm_neg, w, sm, l, sv, on_grads, tie=None):
    gain = lambda n: (sm[n], l)
    big, small = {}, {}
    proj = sv['proj']
    da = matmul([(dx3, w['w_down'])], 'nt', BF16, "down_bwd_a", tie=tie)
    big['w_down'] = matmul([(sv['a'], dx3)], 'tn', BF16, "down_bwd_w")
    dug, duv, dcwg, dcwv, dcbg, dcbv = ffn_act_bwd(sv['ug'], sv['uv'], sm['ffn_conv_w'], sm['ffn_conv_b'], l, da)
    small['ffn_conv_w'] = jnp.concatenate([dcwg, dcwv], axis=1)
    small['ffn_conv_b'] = jnp.concatenate([dcbg, dcbv], axis=1)
    gacc = [((1, D_MODEL), F32)]
    dx2, small['norm_ffn'] = matmul([(dug, w['w_g']), (duv, w['w_vv'])], 'nt', F32, "up_bwd_h", post=_norm_bwd_post,
                                    rows=[sv['x2'], dx3], fulls=[gain('norm_ffn')], accs=gacc, full_n=True, tm_cap=256)
    big['w_up'] = extract_up(matmul([(sv['h3'], dug)], 'tn', BF16, "up_g_bwd_w"),
                             matmul([(sv['h3'], duv)], 'tn', BF16, "up_v_bwd_w"))
    dom = matmul([(dx2, w['w_mo'])], 'nt', BF16, "mo_bwd_a")
    big['w_mo'] = matmul([(sv['om'], dx2)], 'tn', BF16, "mo_bwd_w")
    dmq, dmk, dmv = mem_bwd(sv['mq'], sv['mk'], sv['mv'], dom)
    dx1, small['norm_mem_q'] = matmul([(dmq, w['w_mq'])], 'nt', F32, "mq_bwd_a", post=_norm_bwd_post,
                                      rows=[sv['x1'], dx2], fulls=[gain('norm_mem_q')], accs=gacc, full_n=True)
    big['w_mq'] = matmul([(sv['hq'], dmq)], 'tn', BF16, "mq_bwd_w")
    dmn = matmul([(dmk, w['w_mk']), (dmv, w['w_mv'])], 'nt', BF16, "mkv_bwd_a")
    big['w_mk'] = matmul([(sv['mn'], dmk)], 'tn', BF16, "mk_bwd_w")
    big['w_mv'] = matmul([(sv['mn'], dmv)], 'tn', BF16, "mv_bwd_w")
    _, small['norm_mem_kv'] = rmsnorm_bwd(mem, gain('norm_mem_kv'), dmn, "norm_mem_kv_bwd", dx_dtype=BF16)
    big['w_out'] = matmul([(sv['mix'], dx1)], 'tn', BF16, "out_bwd_w")
    early = {n: big.pop(n).reshape((N_DEV,) + BIG[n]) if n != 'w_up' else big.pop(n) for n in EARLY_GRADS}
    tie = on_grads(l, 'a', early)
    dy, dz, do, small['ssm_norm'], small['attn_out_norm'] = matmul(
        [(dx1, w['w_out'])], 'nt', F32, "out_bwd_a", post=_mix_bwd_post, tie=tie,
        rows=[sv['y'], (proj, D_SSM, OFF_Z // D_SSM), sv['o']], fulls=[gain('ssm_norm'), gain('attn_out_norm')],
        outs=[(D_SSM, F32), (D_SSM, BF16), (D_SSM, F32)], accs=[((1, D_SSM), F32)] * 2, full_n=True)
    dxbc_act, dsmall_ssd, small['ptile'] = ssd_bwd(sv['xbc'], proj, sm['ptile'], l, sv['prevs'], dy)
    dxbc, small['ssm_conv_w'], small['ssm_conv_b'] = ssm_conv_bwd(proj, sm['ssm_conv_w'], sm['ssm_conv_b'], l, dxbc_act)
    dq, dk, dv = mla_bwd(sv['q'], sv['k'], sv['v'], sv['o'], sv['lse'], do, cosm, sinm_neg)
    dsmall = dsmall_bwd(dk, dsmall_ssd, cosm, sinm_neg)
    dcq, small['q_norm'] = matmul(
        [(dq, w['w_uq'])], 'nt', F32, "uq_bwd_a", post=_latent_norm_bwd_post, rows=[(proj, Q_LORA, OFF_CQ // Q_LORA)],
        fulls=[gain('q_norm')], outs=[BF16], accs=[((1, Q_LORA), F32)], full_n=True)
    big['w_uq'] = extract_uq(matmul([(sv['cqn'], dq)], 'tn', BF16, "uq_bwd_w"))
    dckv, small['kv_norm'] = matmul(
        [(dk, w['w_kn']), (dv, w['w_v'])], 'nt', F32, "ukv_bwd_a", post=_latent_norm_bwd_post,
        rows=[(proj, KV_LORA, OFF_CKV // KV_LORA)], fulls=[gain('kv_norm')], outs=[BF16], accs=[((1, KV_LORA), F32)],
        full_n=True)
    big['w_ukv'] = extract_ukv(matmul([(sv['ckvn'], dk)], 'tn', BF16, "kn_bwd_w"),
                               matmul([(sv['ckvn'], dv)], 'tn', BF16, "v_bwd_w"))
    wp = w['w_proj']
    xbc_half = lambda c: Opnd(dxbc, c0=c, shape=(dxbc.shape[0], 1024))
    wwin = lambda off, width: Opnd(wp, c0=off // width, shape=(D_MODEL, width))
    dx0, small['norm_mix'] = matmul(
        [(dz, wwin(OFF_Z, 1024)), (xbc_half(0), wwin(OFF_XBC, 1024)), (xbc_half(1), wwin(OFF_XBC + 1024, 1024)),
         (dcq, wwin(OFF_CQ, Q_LORA)), (dsmall, wwin(OFF_SMALL, LANES)), (dckv, wwin(OFF_CKV, KV_LORA))],
        'nt', F32, "proj_bwd_a", post=_norm_bwd_post, rows=[sv['x0'], dx1], fulls=[gain('norm_mix')], accs=gacc,
        full_n=True, tm_cap=256)
    h1 = sv['h1']
    big['w_in'] = extract_proj(
        matmul([(h1, dz)], 'tn', BF16, "proj_z_bwd_w"), matmul([(h1, dxbc)], 'tn', BF16, "proj_xbc_bwd_w"),
        matmul([(h1, dcq)], 'tn', BF16, "proj_cq_bwd_w"), matmul([(h1, dsmall)], 'tn', BF16, "proj_small_bwd_w"),
        matmul([(h1, dckv)], 'tn', BF16, "proj_ckv_bwd_w"))
    return dx0, on_grads(l, 'b', big), small


def _small_row(small, final=None):
    pt = small['ptile']
    parts = []
    for n, wd in SMALL_SEGS:
        if n in ('dt_bias', 'a_log', 'd_skip'):
            parts.append(pt[('dt_bias', 'a_log', 'd_skip').index(n)][None, :])
        elif n in SMALL_SHARDED:
            parts.append(small[n].reshape(1, wd))
        elif n == 'final_norm':
            parts.append(final if final is not None else jnp.zeros((1, wd), F32))
        else:
            parts.append(small[n])
    return jnp.concatenate(parts, axis=1)


def _rope_tables(positions):
    inv_freq = 1.0 / (ROPE_THETA ** (jnp.arange(0, QK_ROPE, 2, dtype=F32) / QK_ROPE))
    ang = positions.astype(F32)[:, None] * inv_freq
    cos, sin = jnp.cos(ang), jnp.sin(ang)
    s = positions.shape[0]
    pad = jnp.zeros((s, LANES - ROPE_LANE0 - QK_ROPE), F32)
    cosm = jnp.concatenate([jnp.ones((s, ROPE_LANE0), F32), cos, cos, pad], axis=1)
    sinm = jnp.concatenate([jnp.zeros((s, ROPE_LANE0), F32), -sin, sin, pad], axis=1)
    return cosm, sinm


def _small_views(rep, conv_full):
    sm = {n: rep[n].reshape(DEPTH, 1, -1) for n in ('norm_mix', 'ssm_norm', 'attn_out_norm', 'norm_mem_q',
                                                    'norm_mem_kv', 'norm_ffn', 'q_norm', 'kv_norm', 'ssm_conv_b',
                                                    'ffn_conv_b')}
    sm.update(conv_full)
    rows = jnp.stack([rep['dt_bias'], rep['a_log'], rep['d_skip']], axis=1)
    sm['ptile'] = jnp.pad(rows, ((0, 0), (0, 8 - 3), (0, LANES - SSM_HEADS)))
    return sm


def local_step(x, mem, positions, target, sm, final_norm, weights_of, on_grads):
    cosm, sinm = _rope_tables(positions)
    sinm_neg = -sinm
    saved, ws = [], []
    h, h1 = x, None
    for l in range(DEPTH):
        w, tie = weights_of(l, h)
        ws.append(w)
        h, h1, sv = layer_fwd(h, h1, mem, cosm, sinm, w, sm, l, tie=tie)
        saved.append(sv)
    dx, dfinal, lossv = loss_head(h, (final_norm.reshape(1, 1, -1), 0), target)
    rows = [None] * DEPTH
    tie = None
    for l in reversed(range(DEPTH)):
        dx, tie, small = layer_bwd(dx, mem, cosm, sinm_neg, ws[l], sm, l, saved[l], on_grads, tie=tie)
        rows[l] = _small_row(small, dfinal if l == 0 else None)
    return lossv[0, 0], dx, jnp.concatenate(rows, axis=0)


def kernel(x, mem, positions, norm_mix, w_in, ssm_conv_w, ssm_conv_b, dt_bias, a_log, d_skip, ssm_norm, q_norm, w_uq, kv_norm, w_ukv, attn_out_norm, w_out, norm_mem_q, norm_mem_kv, w_mq, w_mk, w_mv, w_mo, norm_ffn, w_up, ffn_conv_w, ffn_conv_b, w_down, final_norm, loss_target, m_norm_mix, m_w_in, m_ssm_conv_w, m_ssm_conv_b, m_dt_bias, m_a_log, m_d_skip, m_ssm_norm, m_q_norm, m_w_uq, m_kv_norm, m_w_ukv, m_attn_out_norm, m_w_out, m_norm_mem_q, m_norm_mem_kv, m_w_mq, m_w_mk, m_w_mv, m_w_mo, m_norm_ffn, m_w_up, m_ffn_conv_w, m_ffn_conv_b, m_w_down, m_final_norm, v_norm_mix, v_w_in, v_ssm_conv_w, v_ssm_conv_b, v_dt_bias, v_a_log, v_d_skip, v_ssm_norm, v_q_norm, v_w_uq, v_kv_norm, v_w_ukv, v_attn_out_norm, v_w_out, v_norm_mem_q, v_norm_mem_kv, v_w_mq, v_w_mk, v_w_mv, v_w_mo, v_norm_ffn, v_w_up, v_ffn_conv_w, v_ffn_conv_b, v_w_down, v_final_norm):
    args = locals()
    wts = {n: args[n] for n in WEIGHT_NAMES}
    ms = {n: args['m_' + n] for n in WEIGHT_NAMES}
    vs = {n: args['v_' + n] for n in WEIGHT_NAMES}

    st = dict(srcs={n: wts[n].astype(BF16) for n in BIG_NAMES}, exchanges=[],
              lands={n: lax.empty((DEPTH, N_DEV) + BIG[n], BF16) for n in BIG_NAMES})
    first = LATE_GRADS + ('w_out',)
    rest = tuple(n for n in BIG_NAMES if n not in first)
    got = all_gather_blocks([st['srcs'][n] for n in first] + [wts[n] for n in SMALL_SHARDED],
                            first_only=tuple(range(len(first))))
    conv_full = {}
    for n, g in zip(SMALL_SHARDED, got[len(first):]):
        taps, per, full = SMALL_SHARDED[n]
        conv_full[n] = jnp.moveaxis(g, 1, 2).reshape(DEPTH, taps, full)
    sm = _small_views(wts, conv_full)

    def start(names, l, tag, after=None):
        send_sems, recv_sems, thru, lands, tie = gather_start([st['srcs'][n] for n in names], l, tag, after)
        st['srcs'].update(zip(names, thru))
        return (names, l, tag, send_sems, recv_sems, lands), tie

    def pass_on(handle, after):
        names, l, tag, send_sems, recv_sems, lands = handle
        thru, lands = gather_wait(l, tag, send_sems, recv_sems, [st['srcs'][n] for n in names], lands, after)
        st['srcs'].update(zip(names, thru))
        send_sems, recv_sems, lands, tie = gather_pass_start(lands, l, tag)
        return (names, l, tag, send_sems, recv_sems, lands), tie

    def finish(handle, after):
        names, l, tag, send_sems, recv_sems, lands = handle
        return _layer_weights(dict(zip(names, gather_pass_wait(l, tag, send_sems, recv_sems, lands, after))))

    later, _ = start(rest, 0, "r", after=got[0])

    def weights_of(l, h):
        if l == 0:
            w = _layer_weights(dict(zip(first, got[:len(first)])))
            w['later'] = lambda after: finish(pass_on(later, after)[0], after)
        else:
            w = finish(st['next'], h)
        tie = None
        if l + 1 < DEPTH:
            st['next'], tie = start(BIG_NAMES, l + 1, "")

            def prefetch(after):
                st['next'], tie2 = pass_on(st['next'], after)
                return tie2

            w['prefetch'] = prefetch
        return w, tie

    def on_grads(l, tag, big, after=None):
        if (l, tag) == (0, 'b') and after is None:
            st['held'] = big
            return None
        names = list(big)
        send_sems, recv_sems, thru, lands, tie = grad_exchange_start(
            [big[n] for n in names], [st['lands'][n] for n in names], l, tag, after)
        st['lands'].update(zip(names, lands))
        st['exchanges'].append((l, tag, names, send_sems, recv_sems, thru))
        return tie

    loss_local, dx, small_rows = local_step(x[0], mem[0], positions[0], loss_target[0], sm, final_norm, weights_of,
                                            on_grads)
    outs = [{}, {}, {}, {}]

    sg_send, sg_recv, sg_src, sg_land, tok = small_gather_start(small_rows)
    tie = on_grads(0, 'b', st['held'], after=tok)

    def wait(exchange, after):
        l, tag, names, send_sems, recv_sems, thru = exchange
        _, lands = grad_exchange_wait(l, tag, send_sems, recv_sems, thru, [st['lands'][n] for n in names], after)
        st['lands'].update(zip(names, lands))

    def update(names, tie):
        for n in names:
            res_n = adamw_big(st['lands'][n], wts[n], ms[n], vs[n], "adamw_" + n, tie)
            tie = res_n[0]
            for k in range(4):
                outs[k][n] = res_n[k]
        return tie

    for exchange in st['exchanges'][:-1]:
        wait(exchange, tie)
    tie = update(EARLY_GRADS, tie)

    small_all = small_gather_wait(sg_send, sg_recv, sg_src, sg_land, tie)
    view = lambda d: {n: d[n].reshape(SMALL_VIEW[n]) for n in SMALL_NAMES}
    res = adamw_small(small_all, view(wts), view(ms), view(vs), tie)
    for k in range(4):
        for n in SMALL_NAMES:
            outs[k][n] = res[k][n].reshape(wts[n].shape)

    wait(st['exchanges'][-1], res[0]['final_norm'])
    update(LATE_GRADS, res[0]['final_norm'])

    loss = lax.psum(loss_local, ("x", "y", "c"))
    return (loss, dx[None], *[outs[0][n] for n in WEIGHT_NAMES], *[outs[1][n] for n in WEIGHT_NAMES],
            *[outs[2][n] for n in WEIGHT_NAMES], *[outs[3][n] for n in WEIGHT_NAMES])
```

```python
import functools
import math
from typing import Any, NamedTuple, Optional

import jax
import jax.numpy as jnp
from jax import lax
from jax.experimental import pallas as pl
from jax.experimental.pallas import tpu as pltpu

F32 = jnp.float32
BF16 = jnp.bfloat16

D_MODEL = 1024
DEPTH = 4
MEM_LEN = 256
EPS = 1e-6
SSM_HEADS = 16
SSM_HEAD_DIM = 64
D_SSM = 1024
SSM_GROUPS = 4
SSM_STATE = 128
SSM_CONV = 4
SSM_CHUNK = 128
CONV_CH = 2048
MLA_HEADS = 16
QK_NOPE = 64
QK_ROPE = 32
V_DIM = 64
Q_LORA = 384
KV_LORA = 256
ROPE_THETA = 10000.0
MEM_HEADS = 4
MEM_HEAD_DIM = 256
D_FF = 2816
FFN_CONV = 3
D_IN = 3760
D_MIX = 2048
ADAM_LR = 0.001
ADAM_B1 = 0.9
ADAM_B2 = 0.999
ADAM_EPS = 1e-08
ADAM_WD = 0.01
ADAM_STEP = 10

N_DEV = 8
N_CHIP = 4
LANES = 128
HEAD_PAD = 128
PROJ_W = 3840
OFF_Z, OFF_XBC, OFF_CQ, OFF_SMALL, OFF_CKV = 0, 1024, 3072, 3456, 3584
ROPE_LANE0 = 64
VMEM_LIMIT = 56 * 1024 * 1024
MM_BLOCK_BYTES = 4 * 1024 * 1024
WEIGHT_NAMES = ['norm_mix', 'w_in', 'ssm_conv_w', 'ssm_conv_b', 'dt_bias', 'a_log', 'd_skip', 'ssm_norm', 'q_norm',
                'w_uq', 'kv_norm', 'w_ukv', 'attn_out_norm', 'w_out', 'norm_mem_q', 'norm_mem_kv', 'w_mq', 'w_mk',
                'w_mv', 'w_mo', 'norm_ffn', 'w_up', 'ffn_conv_w', 'ffn_conv_b', 'w_down', 'final_norm']
BIG = {'w_in': (1024, 470), 'w_uq': (384, 192), 'w_ukv': (256, 256), 'w_up': (1024, 704), 'w_out': (256, 1024),
       'w_mq': (128, 1024), 'w_mk': (128, 1024), 'w_mv': (128, 1024), 'w_mo': (128, 1024), 'w_down': (352, 1024)}
BIG_NAMES = list(BIG)
PROJ_SEGS = [(0, 1024, OFF_Z), (1024, 3072, OFF_XBC), (3072, 3088, OFF_SMALL), (3088, 3472, OFF_CQ),
             (3472, 3728, OFF_CKV), (3728, 3760, OFF_SMALL + ROPE_LANE0)]
SMALL_SEGS = [('norm_mix', 1024), ('ssm_norm', 1024), ('attn_out_norm', 1024), ('norm_mem_q', 1024),
              ('norm_mem_kv', 1024), ('norm_ffn', 1024), ('q_norm', 384), ('kv_norm', 256), ('ssm_conv_b', 2048),
              ('ffn_conv_b', 5632), ('dt_bias', 128), ('a_log', 128), ('d_skip', 128),
              ('ssm_conv_w', SSM_CONV * CONV_CH), ('ffn_conv_w', FFN_CONV * 2 * D_FF), ('final_norm', 1024)]
SMALL_OFF = {}
_o = 0
for _n, _w in SMALL_SEGS:
    SMALL_OFF[_n] = _o
    _o += _w
SMALL_W = _o


def _params(**kw):
    return pltpu.CompilerParams(vmem_limit_bytes=VMEM_LIMIT, **kw)


def _pick(n, cap):
    if n <= cap:
        return n
    best = None
    for t in range(LANES, cap + 1, LANES):
        if n % t == 0:
            best = t
    assert best is not None, (n, cap)
    return best


def _row_tile(a, cap=256):
    if a <= cap:
        return a
    best = None
    for t in range(16, cap + 1, 16):
        if a % t == 0:
            best = t
    assert best is not None, (a, cap)
    return best


class Opnd(NamedTuple):
    arr: Any
    lead: Optional[int] = None
    r0: int = 0
    c0: int = 0
    shape: Optional[tuple] = None


def _opnd(x):
    return x if isinstance(x, Opnd) else Opnd(x)


def _lshape(o):
    return tuple(o.shape) if o.shape is not None else tuple(o.arr.shape[-2:])


def _spec(o, br, bc, bi, bj):
    rr, cc = _lshape(o)
    assert rr % br == 0 and cc % bc == 0, (rr, cc, br, bc)
    ro, co = o.r0 * (rr // br), o.c0 * (cc // bc)
    if o.lead is None:
        return pl.BlockSpec((br, bc), lambda i, j: (ro + bi(i, j), co + bj(i, j)))
    return pl.BlockSpec((None, br, bc), lambda i, j: (o.lead, ro + bi(i, j), co + bj(i, j)))


_DIMS = {'nn': (((1,), (0,)), ((), ())), 'nt': (((1,), (1,)), ((), ())), 'tn': (((0,), (0,)), ((), ()))}
_ROW = lambda i, j: i
_COL = lambda i, j: j
_ZERO = lambda i, j: 0


def matmul(pairs, mode, out_dtype, name, add=None, tie=None, post=None, rows=(), fulls=(), outs=None, full_n=False,
           accs=(), tm_cap=None, tn_fixed=None):
    pairs = [(_opnd(a), _opnd(b)) for a, b in pairs]
    a0, b0 = pairs[0]
    if mode == 'nn':
        m, n = _lshape(a0)[0], _lshape(b0)[1]
    elif mode == 'nt':
        m, n = _lshape(a0)[0], _lshape(b0)[0]
    else:
        m, n = _lshape(a0)[1], _lshape(b0)[1]
    isz = lambda o: jnp.dtype(o.arr.dtype).itemsize
    osz = jnp.dtype(out_dtype).itemsize
    cap = lambda budget, per: max(LANES, budget // per // LANES * LANES)
    if mode == 'tn':
        ktok = _lshape(a0)[0]
        tm = _pick(m, cap(3 * MM_BLOCK_BYTES // 2, ktok * isz(a0)))
        tn = _pick(n, cap(3 * MM_BLOCK_BYTES // 2, ktok * isz(b0)))
    else:
        tm = _pick(m, min(2048, cap(2 * MM_BLOCK_BYTES, sum(_lshape(a)[1] * isz(a) for a, _ in pairs))))
        tn = _pick(n, min(cap(3 * MM_BLOCK_BYTES // 2, sum(_lshape(a)[1] * isz(b) for a, b in pairs)),
                          cap(MM_BLOCK_BYTES, tm * osz), n // 2 if n >= 1024 else n))
        if full_n:
            tm, tn = _pick(m, min(tm, tm_cap or tm, cap(MM_BLOCK_BYTES // 2, n * osz))), n
        if tn_fixed is not None:
            tn = tn_fixed
    npairs = len(pairs)
    outs = list(outs) if outs is not None else [out_dtype]
    nadd = 1 if add is not None else 0
    nrows, nfulls = len(rows), len(fulls)

    def body(*refs):
        o_refs = refs[len(refs) - len(outs) - len(accs):]
        if accs:
            @pl.when(jnp.logical_and(pl.program_id(0) == 0, pl.program_id(1) == 0))
            def _():
                for r in o_refs[len(outs):]:
                    r[...] = jnp.zeros_like(r)

        acc = None
        for p in range(npairs):
            a = refs[2 * p][...].astype(BF16)
            b = refs[2 * p + 1][...].astype(BF16)
            d = lax.dot_general(a, b, _DIMS[mode], preferred_element_type=F32)
            acc = d if acc is None else acc + d
        if add is not None:
            acc = acc + refs[2 * npairs][...].astype(F32)
        if post is None:
            o_refs[0][...] = acc.astype(out_dtype)
        else:
            x0 = 2 * npairs + nadd
            post(acc, [r[...] for r in refs[x0:x0 + nrows]], [r[...] for r in refs[x0 + nrows:x0 + nrows + nfulls]], o_refs)

    rows = [r if isinstance(r, tuple) else (r, r.shape[1], 0) for r in rows]
    tie_specs = [pl.BlockSpec((tm, wd), lambda i, j, cb=cb: (i, cb)) for _, wd, cb in rows]
    tie_specs += [pl.BlockSpec((None,) + f.shape[1:], lambda i, j, ld=ld, nd=f.ndim - 1: (ld,) + (0,) * nd) for f, ld in fulls]
    tie_args = [r for r, _, _ in rows] + [f for f, _ in fulls]
    if tie is not None:
        tie_specs.append(pl.BlockSpec(memory_space=pl.ANY))
        tie_args.append(tie)

    in_specs, args = [], []
    for a, b in pairs:
        if mode == 'nn':
            k = _lshape(a)[1]
            in_specs += [_spec(a, tm, k, _ROW, _ZERO), _spec(b, k, tn, _ZERO, _COL)]
        elif mode == 'nt':
            k = _lshape(a)[1]
            in_specs += [_spec(a, tm, k, _ROW, _ZERO), _spec(b, tn, k, _COL, _ZERO)]
        else:
            k = _lshape(a)[0]
            in_specs += [_spec(a, k, tm, _ZERO, _ROW), _spec(b, k, tn, _ZERO, _COL)]
        args += [a.arr, b.arr]
    if add is not None:
        in_specs.append(pl.BlockSpec((tm, tn), lambda i, j: (i, j)))
        args.append(add)
    res = pl.pallas_call(
        body, name=name, grid=(m // tm, n // tn), in_specs=in_specs + tie_specs,
        out_specs=[pl.BlockSpec((tm, o[0]), lambda i, j: (i, 0)) if isinstance(o, tuple) else
                   pl.BlockSpec((tm, tn), lambda i, j: (i, j)) for o in outs] +
                  [pl.BlockSpec(shp, lambda i, j, nd=len(shp): (0,) * nd) for shp, _ in accs],
        out_shape=[jax.ShapeDtypeStruct((m, o[0]), o[1]) if isinstance(o, tuple) else jax.ShapeDtypeStruct((m, n), o)
                   for o in outs] + [jax.ShapeDtypeStruct(shp, dt) for shp, dt in accs],
        compiler_params=_params(dimension_semantics=("arbitrary", "arbitrary")),
    )(*args, *tie_args)
    return res[0] if len(outs) + len(accs) == 1 else res


def rowwise(fn, rows, fulls, outs, accs, name, tm=512, into=None, tie=None):
    s = rows[0][0].shape[0]
    tm = min(tm, s)
    nrow, nfull, nout, nacc = len(rows), len(fulls), len(outs), len(accs)
    nin = nrow + nfull

    def body(*refs):
        ins = [r[...] for r in refs[:nin]]
        res = fn(*ins)
        if not isinstance(res, (tuple, list)):
            res = (res,)
        orefs = refs[nin + (1 if into is not None else 0) + (1 if tie is not None else 0):]
        for k in range(nout):
            orefs[k][...] = res[k].astype(orefs[k].dtype)
        if nacc:
            @pl.when(pl.program_id(0) == 0)
            def _():
                for k in range(nacc):
                    orefs[nout + k][...] = jnp.zeros_like(orefs[nout + k])

            for k in range(nacc):
                orefs[nout + k][...] += res[nout + k].astype(orefs[nout + k].dtype)

    in_specs = [pl.BlockSpec((tm, w), lambda i, cb=cb: (i, cb)) for _, w, cb in rows]
    in_specs += [pl.BlockSpec((None,) + f.shape[1:], lambda i, ld=ld, nd=f.ndim - 1: (ld,) + (0,) * nd) for f, ld in fulls]
    args = [r[0] for r in rows] + [f for f, _ in fulls]
    aliases = {}
    if into is not None:
        in_specs.append(pl.BlockSpec(memory_space=pl.ANY))
        args.append(into[0])
        aliases = {nin: into[1]}
    if tie is not None:
        in_specs.append(pl.BlockSpec(memory_space=pl.ANY))
        args.append(tie)
    out_specs, out_shape = [], []
    for o in outs:
        w, dt = o[0], o[1]
        total, cb = (o[2], o[3]) if len(o) == 4 else (w, 0)
        out_specs.append(pl.BlockSpec((tm, w), lambda i, cb=cb: (i, cb)))
        out_shape.append(jax.ShapeDtypeStruct((s, total), dt))
    for shp, dt in accs:
        out_specs.append(pl.BlockSpec(shp, lambda i, nd=len(shp): (0,) * nd))
        out_shape.append(jax.ShapeDtypeStruct(shp, dt))
    return pl.pallas_call(
        body, name=name, grid=(s // tm,), in_specs=in_specs, out_specs=out_specs, out_shape=out_shape,
        input_output_aliases=aliases, compiler_params=_params(dimension_semantics=("arbitrary",)),
    )(*args)


def _rms(x, g):
    xf = x.astype(F32)
    var = jnp.mean(xf * xf, axis=-1, keepdims=True)
    return xf * lax.rsqrt(var + EPS) * g


def rmsnorm_fwd(x, g, name, width=None, colblock=0, out=None, into=None, tie=None):
    w = width or x.shape[1]
    return rowwise(lambda xt, gt: _rms(xt, gt), [(x, w, colblock)], [g], [out or (w, BF16)], [], name, into=into,
                   tie=tie)[0]


def rmsnorm_bwd(x, g, dh, name, resid=None, width=None, colblock=0, dh_colblock=0, dx_dtype=F32):
    w = width or x.shape[1]

    def fn(xt, dht, *rest):
        gt = rest[-1]
        _, vjp = jax.vjp(_rms, xt.astype(F32), gt)
        dx, dg = vjp(dht.astype(F32))
        if resid is not None:
            dx = dx + rest[0]
        return dx, dg

    rows = [(x, w, colblock), (dh, w, dh_colblock)] + ([(resid, w, 0)] if resid is not None else [])
    return rowwise(fn, rows, [g], [(w, dx_dtype)], [((1, w), F32)], name)


CONV_R = 64
HALO = 8


def _ext_rows(ref, i, nchunk, above, below):
    r0 = pl.multiple_of(i * CONV_R, CONV_R)
    s = ref.shape[0]
    parts = []
    if above:
        top = ref[pl.ds(pl.multiple_of(jnp.maximum(r0 - HALO, 0), HALO), HALO), :].astype(F32)
        parts.append(jnp.where(i > 0, top, 0.0))
    parts.append(ref[pl.ds(r0, CONV_R), :].astype(F32))
    if below:
        tile = 2 * HALO if ref.dtype == BF16 else HALO
        bot = ref[pl.ds(pl.multiple_of(jnp.minimum(r0 + CONV_R, s - tile), tile), tile), :].astype(F32)[0:HALO]
        parts.append(jnp.where(i < nchunk - 1, bot, 0.0))
    return jnp.concatenate(parts, axis=0)


def _conv_ext(ext, w_ref, b_ref, kw):
    y = ext[HALO:] * w_ref[kw - 1:kw, :] + b_ref[...]
    for k in range(1, kw):
        y = y + pltpu.roll(ext, k, 0)[HALO:] * w_ref[kw - 1 - k:kw - k, :]
    return y


def _conv_t_ext(d, w_ref, kw):
    n = d.shape[0]
    y = d[:n - HALO] * w_ref[kw - 1:kw, :]
    for k in range(1, kw):
        y = y + pltpu.roll(d, n - k, 0)[:n - HALO] * w_ref[kw - 1 - k:kw - k, :]
    return y


def _conv_wgrad(dp, ext, kw):
    out = [jnp.sum(dp, axis=0, keepdims=True), jnp.sum(dp * ext[HALO:HALO + CONV_R], axis=0, keepdims=True)]
    for k in range(1, kw):
        out.append(jnp.sum(dp * pltpu.roll(ext, k, 0)[HALO:HALO + CONV_R], axis=0, keepdims=True))
    return out


def _store_wgrad(res, dw_ref, db_ref, kw):
    db_ref[...] = res[0]
    for k in range(kw):
        dw_ref[kw - 1 - k:kw - k, :] = res[1 + k]


def _silu(x):
    return x * jax.nn.sigmoid(x)


def _dsilu(x):
    s = jax.nn.sigmoid(x)
    return s * (1.0 + x * (1.0 - s))


SSM_TC = 256


def ssm_conv_fwd(proj, cw, cb, l):
    s = proj.shape[0]
    off = OFF_XBC // SSM_TC

    def body(u_ref, w_ref, b_ref, o_ref):
        nchunk = s // CONV_R

        def step(i, carry):
            ext = _ext_rows(u_ref, i, nchunk, True, False)
            o_ref[pl.ds(pl.multiple_of(i * CONV_R, CONV_R), CONV_R), :] = _silu(_conv_ext(ext, w_ref, b_ref, SSM_CONV))
            return carry

        lax.fori_loop(0, nchunk, step, 0)

    return pl.pallas_call(
        body, name="ssm_conv_fwd", grid=(CONV_CH // SSM_TC,),
        in_specs=[pl.BlockSpec((s, SSM_TC), lambda j: (0, off + j)),
                  pl.BlockSpec((None, SSM_CONV, SSM_TC), lambda j: (l, 0, j)),
                  pl.BlockSpec((None, 1, SSM_TC), lambda j: (l, 0, j))],
        out_specs=pl.BlockSpec((s, SSM_TC), lambda j: (0, j)),
        out_shape=jax.ShapeDtypeStruct((s, CONV_CH), F32),
        compiler_params=_params(dimension_semantics=("arbitrary",)),
    )(proj, cw, cb)


def ssm_conv_bwd(proj, cw, cb, l, dact):
    s = proj.shape[0]
    off = OFF_XBC // SSM_TC

    def body(u_ref, w_ref, b_ref, d_ref, du_ref, dw_ref, db_ref):
        nchunk = s // CONV_R

        def step(i, carry):
            ext = _ext_rows(u_ref, i, nchunk, True, True)
            dpre = _ext_rows(d_ref, i, nchunk, False, True) * _dsilu(_conv_ext(ext, w_ref, b_ref, SSM_CONV))
            du_ref[pl.ds(pl.multiple_of(i * CONV_R, CONV_R), CONV_R), :] = _conv_t_ext(dpre, w_ref, SSM_CONV).astype(du_ref.dtype)
            return tuple(c + g for c, g in zip(carry, _conv_wgrad(dpre[:CONV_R], ext, SSM_CONV)))

        zero = jnp.zeros((1, SSM_TC), F32)
        _store_wgrad(lax.fori_loop(0, nchunk, step, (zero,) * (SSM_CONV + 1)), dw_ref, db_ref, SSM_CONV)

    return pl.pallas_call(
        body, name="ssm_conv_bwd", grid=(CONV_CH // SSM_TC,),
        in_specs=[pl.BlockSpec((s, SSM_TC), lambda j: (0, off + j)),
                  pl.BlockSpec((None, SSM_CONV, SSM_TC), lambda j: (l, 0, j)),
                  pl.BlockSpec((None, 1, SSM_TC), lambda j: (l, 0, j)), pl.BlockSpec((s, SSM_TC), lambda j: (0, j))],
        out_specs=[pl.BlockSpec((s, SSM_TC), lambda j: (0, j)), pl.BlockSpec((SSM_CONV, SSM_TC), lambda j: (0, j)),
                   pl.BlockSpec((1, SSM_TC), lambda j: (0, j))],
        out_shape=[jax.ShapeDtypeStruct((s, CONV_CH), BF16), jax.ShapeDtypeStruct((SSM_CONV, CONV_CH), F32),
                   jax.ShapeDtypeStruct((1, CONV_CH), F32)],
        compiler_params=_params(dimension_semantics=("arbitrary",)),
    )(proj, cw, cb, dact)


FFN_TC = 256
FFN_NT = D_FF // FFN_TC


def _ffn_specs(s, l):
    blk = pl.BlockSpec((s, FFN_TC), lambda j: (0, j))
    wg = pl.BlockSpec((None, FFN_CONV, FFN_TC), lambda j: (l, 0, j))
    wv = pl.BlockSpec((None, FFN_CONV, FFN_TC), lambda j: (l, 0, FFN_NT + j))
    bg = pl.BlockSpec((None, 1, FFN_TC), lambda j: (l, 0, j))
    bv = pl.BlockSpec((None, 1, FFN_TC), lambda j: (l, 0, FFN_NT + j))
    return blk, wg, wv, bg, bv


def ffn_act_fwd(ug, uv, cw, cb, l):
    s = ug.shape[0]

    def body(g_ref, v_ref, wg_ref, wv_ref, bg_ref, bv_ref, o_ref):
        nchunk = s // CONV_R

        def step(i, carry):
            cg = _conv_ext(_ext_rows(g_ref, i, nchunk, True, False), wg_ref, bg_ref, FFN_CONV)
            cv = _conv_ext(_ext_rows(v_ref, i, nchunk, True, False), wv_ref, bv_ref, FFN_CONV)
            o_ref[pl.ds(pl.multiple_of(i * CONV_R, CONV_R), CONV_R), :] = (_silu(cg) * cv).astype(o_ref.dtype)
            return carry

        lax.fori_loop(0, nchunk, step, 0)

    blk, wg, wv, bg, bv = _ffn_specs(s, l)
    return pl.pallas_call(
        body, name="ffn_act_fwd", grid=(FFN_NT,), in_specs=[blk, blk, wg, wv, bg, bv],
        out_specs=blk, out_shape=jax.ShapeDtypeStruct((s, D_FF), BF16),
        compiler_params=_params(dimension_semantics=("arbitrary",)),
    )(ug, uv, cw, cw, cb, cb)


def ffn_act_bwd(ug, uv, cw, cb, l, da):
    s = ug.shape[0]

    def body(g_ref, v_ref, wg_ref, wv_ref, bg_ref, bv_ref, da_ref, dg_ref, dv_ref, dwg_ref, dwv_ref, dbg_ref, dbv_ref):
        nchunk = s // CONV_R

        def step(i, carry):
            rows = pl.ds(pl.multiple_of(i * CONV_R, CONV_R), CONV_R)
            eg = _ext_rows(g_ref, i, nchunk, True, True)
            ev = _ext_rows(v_ref, i, nchunk, True, True)
            cg = _conv_ext(eg, wg_ref, bg_ref, FFN_CONV)
            cv = _conv_ext(ev, wv_ref, bv_ref, FFN_CONV)
            da_t = _ext_rows(da_ref, i, nchunk, False, True)
            sg = jax.nn.sigmoid(cg)
            dcg = da_t * cv * (sg * (1.0 + cg * (1.0 - sg)))
            dcv = da_t * (cg * sg)
            dg_ref[rows, :] = _conv_t_ext(dcg, wg_ref, FFN_CONV).astype(dg_ref.dtype)
            dv_ref[rows, :] = _conv_t_ext(dcv, wv_ref, FFN_CONV).astype(dv_ref.dtype)
            grads = _conv_wgrad(dcg[:CONV_R], eg, FFN_CONV) + _conv_wgrad(dcv[:CONV_R], ev, FFN_CONV)
            return tuple(c + g for c, g in zip(carry, grads))

        zero = jnp.zeros((1, FFN_TC), F32)
        res = lax.fori_loop(0, nchunk, step, (zero,) * (2 * FFN_CONV + 2))
        _store_wgrad(res[:FFN_CONV + 1], dwg_ref, dbg_ref, FFN_CONV)
        _store_wgrad(res[FFN_CONV + 1:], dwv_ref, dbv_ref, FFN_CONV)

    blk, wg, wv, bg, bv = _ffn_specs(s, l)
    wblk = pl.BlockSpec((FFN_CONV, FFN_TC), lambda j: (0, j))
    bblk = pl.BlockSpec((1, FFN_TC), lambda j: (0, j))
    return pl.pallas_call(
        body, name="ffn_act_bwd", grid=(FFN_NT,), in_specs=[blk, blk, wg, wv, bg, bv, blk],
        out_specs=[blk, blk, wblk, wblk, bblk, bblk],
        out_shape=[jax.ShapeDtypeStruct((s, D_FF), BF16), jax.ShapeDtypeStruct((s, D_FF), BF16),
                   jax.ShapeDtypeStruct((FFN_CONV, D_FF), F32), jax.ShapeDtypeStruct((FFN_CONV, D_FF), F32),
                   jax.ShapeDtypeStruct((1, D_FF), F32), jax.ShapeDtypeStruct((1, D_FF), F32)],
        compiler_params=_params(dimension_semantics=("arbitrary",)),
    )(ug, uv, cw, cw, cb, cb, da)


def _dot(a, b, mode):
    return lax.dot_general(a.astype(BF16), b.astype(BF16), _DIMS[mode], preferred_element_type=F32)


@jax.custom_vjp
def mm_nn(a, b):
    return _dot(a, b, 'nn')


@jax.custom_vjp
def mm_nt(a, b):
    return _dot(a, b, 'nt')


@jax.custom_vjp
def mm_tn(a, b):
    return _dot(a, b, 'tn')


mm_nn.defvjp(lambda a, b: (_dot(a, b, 'nn'), (a, b)), lambda r, g: (_dot(g, r[1], 'nt'), _dot(r[0], g, 'tn')))
mm_nt.defvjp(lambda a, b: (_dot(a, b, 'nt'), (a, b)), lambda r, g: (_dot(g, r[1], 'nn'), _dot(g, r[0], 'tn')))
mm_tn.defvjp(lambda a, b: (_dot(a, b, 'tn'), (a, b)), lambda r, g: (_dot(r[1], g, 'nt'), _dot(r[0], g, 'nn')))


def _tri(n, lower):
    r = lax.broadcasted_iota(jnp.int32, (n, n), 0)
    c = lax.broadcasted_iota(jnp.int32, (n, n), 1)
    return jnp.where((r >= c) if lower else (r <= c), 1.0, 0.0).astype(F32)


def _tri_dot(a, lower):
    return jnp.dot(_tri(a.shape[0], lower), a, precision=lax.Precision.HIGHEST, preferred_element_type=F32)


@jax.custom_vjp
def _cumsum_rows(a):
    return _tri_dot(a, True)


_cumsum_rows.defvjp(lambda a: (_tri_dot(a, True), None), lambda _, g: (_tri_dot(g, False),))


def _softplus(x):
    return jnp.maximum(x, 0.0) + jnp.log(1.0 + jnp.exp(-jnp.abs(x)))


def _ssd_chunk(xs, bs, cs, small, dtb, alog, dsk, prev):
    ln = small.shape[0]
    lane = lax.broadcasted_iota(jnp.int32, (ln, LANES), 1)
    lane1 = lax.broadcasted_iota(jnp.int32, (1, LANES), 1)
    sub = lax.broadcasted_iota(jnp.int32, (LANES, ln), 0)
    rowi = lax.broadcasted_iota(jnp.int32, (ln, LANES), 0)
    tril = lax.broadcasted_iota(jnp.int32, (ln, ln), 0) >= lax.broadcasted_iota(jnp.int32, (ln, ln), 1)
    first = lane < SSM_HEAD_DIM
    first1 = lane1 < SSM_HEAD_DIM

    dt = _softplus(small + dtb)
    acs = _cumsum_rows(dt * (-jnp.exp(alog)))
    acs_t = acs.T
    last = jnp.sum(jnp.where(rowi == ln - 1, acs, 0.0), axis=0, keepdims=True)

    def col(a, h):
        return jnp.sum(jnp.where(lane == h, a, 0.0), axis=1, keepdims=True)

    def one(a, h):
        return jnp.sum(jnp.where(lane1 == h, a, 0.0), axis=1, keepdims=True)

    def rowv(at, h):
        return jnp.sum(jnp.where(sub == h, at, 0.0), axis=0, keepdims=True)

    cb = [mm_nt(cs[g], bs[g]) for g in range(SSM_GROUPS)]
    ys, news = [], []
    for j in range(SSM_HEADS // 2):
        g = j // 2
        h0, h1 = 2 * j, 2 * j + 1
        xd = xs[j] * jnp.where(first, col(dt, h0), col(dt, h1))
        yd, st, ea, cd = None, None, [], []
        for h, xdh in ((h0, jnp.where(first, xd, 0.0)), (h1, jnp.where(first, 0.0, xd))):
            ac = col(acs, h)
            la = one(last, h)
            lmat = jnp.exp(jnp.where(tril, ac - rowv(acs_t, h), -jnp.inf))
            yh = mm_nn(cb[g] * lmat, xdh)
            sh = mm_tn(bs[g] * jnp.exp(la - ac), xdh)
            yd = yh if yd is None else yd + yh
            st = sh if st is None else st + sh
            ea.append(jnp.exp(ac))
            cd.append(jnp.exp(la))
        yoff = mm_nn(cs[g], prev[j]) * jnp.where(first, ea[0], ea[1])
        ys.append(yd + yoff + xs[j] * jnp.where(first1, one(dsk, h0), one(dsk, h1)))
        news.append(prev[j] * jnp.where(first1, cd[0], cd[1]) + st)
    return ys, news


N_PAIR = SSM_HEADS // 2


def ssd_fwd(xbc, proj, ptile, gain, l):
    s = xbc.shape[0]
    nch = s // SSM_CHUNK

    def body(xbc_ref, small_ref, p_ref, z_ref, g_ref, y_ref, prev_ref, mix_ref, state_ref):
        @pl.when(pl.program_id(0) == 0)
        def _():
            state_ref[...] = jnp.zeros_like(state_ref)

        xs = [xbc_ref[:, LANES * j:LANES * (j + 1)] for j in range(N_PAIR)]
        bs = [xbc_ref[:, D_SSM + LANES * g:D_SSM + LANES * (g + 1)] for g in range(SSM_GROUPS)]
        cs = [xbc_ref[:, D_SSM + 512 + LANES * g:D_SSM + 512 + LANES * (g + 1)] for g in range(SSM_GROUPS)]
        prev = [state_ref[j] for j in range(N_PAIR)]
        ys, news = _ssd_chunk(xs, bs, cs, small_ref[...], p_ref[0:1, :], p_ref[1:2, :], p_ref[2:3, :], prev)
        gated, ssq = [], None
        for j in range(N_PAIR):
            y_ref[:, LANES * j:LANES * (j + 1)] = ys[j]
            prev_ref[0, j] = prev[j]
            state_ref[j] = news[j]
            t = ys[j] * _silu(z_ref[:, LANES * j:LANES * (j + 1)])
            sq = jnp.sum(t * t, axis=1, keepdims=True)
            gated.append(t)
            ssq = sq if ssq is None else ssq + sq
        inv = lax.rsqrt(ssq / D_SSM + EPS)
        for j in range(N_PAIR):
            mix_ref[:, LANES * j:LANES * (j + 1)] = (gated[j] * inv * g_ref[:, LANES * j:LANES * (j + 1)]).astype(mix_ref.dtype)

    return pl.pallas_call(
        body, name="ssd_fwd", grid=(nch,),
        in_specs=[pl.BlockSpec((SSM_CHUNK, CONV_CH), lambda c: (c, 0)),
                  pl.BlockSpec((SSM_CHUNK, LANES), lambda c: (c, OFF_SMALL // LANES)),
                  pl.BlockSpec((None, 8, LANES), lambda c: (l, 0, 0)),
                  pl.BlockSpec((SSM_CHUNK, D_SSM), lambda c: (c, OFF_Z // D_SSM)),
                  pl.BlockSpec((None, 1, D_SSM), lambda c: (l, 0, 0))],
        out_specs=[pl.BlockSpec((SSM_CHUNK, D_SSM), lambda c: (c, 0)),
                   pl.BlockSpec((1, N_PAIR, SSM_STATE, LANES), lambda c: (c, 0, 0, 0)),
                   pl.BlockSpec((SSM_CHUNK, D_SSM), lambda c: (c, 0))],
        out_shape=[jax.ShapeDtypeStruct((s, D_SSM), F32), jax.ShapeDtypeStruct((nch, N_PAIR, SSM_STATE, LANES), F32),
                   jax.ShapeDtypeStruct((s, D_MIX), BF16)],
        scratch_shapes=[pltpu.VMEM((N_PAIR, SSM_STATE, LANES), F32)],
        compiler_params=_params(dimension_semantics=("arbitrary",)),
    )(xbc, proj, ptile, proj, gain)


def ssd_bwd(xbc, proj, ptile, l, prevs, dy):
    s = xbc.shape[0]
    nch = s // SSM_CHUNK

    def body(xbc_ref, small_ref, p_ref, prev_ref, dy_ref, dxbc_ref, dsmall_ref, dp_ref, dstate_ref):
        @pl.when(pl.program_id(0) == 0)
        def _():
            dstate_ref[...] = jnp.zeros_like(dstate_ref)
            dp_ref[...] = jnp.zeros_like(dp_ref)

        xs = [xbc_ref[:, LANES * j:LANES * (j + 1)] for j in range(N_PAIR)]
        bs = [xbc_ref[:, D_SSM + LANES * g:D_SSM + LANES * (g + 1)] for g in range(SSM_GROUPS)]
        cs = [xbc_ref[:, D_SSM + 512 + LANES * g:D_SSM + 512 + LANES * (g + 1)] for g in range(SSM_GROUPS)]
        prev = [prev_ref[0, j] for j in range(N_PAIR)]
        dys = [dy_ref[:, LANES * j:LANES * (j + 1)] for j in range(N_PAIR)]
        dnew = [dstate_ref[j] for j in range(N_PAIR)]
        _, vjp = jax.vjp(_ssd_chunk, xs, bs, cs, small_ref[...], p_ref[0:1, :], p_ref[1:2, :], p_ref[2:3, :], prev)
        dxs, dbs, dcs, dsmall, ddtb, dalog, ddsk, dprev = vjp((dys, dnew))
        for j in range(N_PAIR):
            dxbc_ref[:, LANES * j:LANES * (j + 1)] = dxs[j]
            dstate_ref[j] = dprev[j]
        for g in range(SSM_GROUPS):
            dxbc_ref[:, D_SSM + LANES * g:D_SSM + LANES * (g + 1)] = dbs[g]
            dxbc_ref[:, D_SSM + 512 + LANES * g:D_SSM + 512 + LANES * (g + 1)] = dcs[g]
        dsmall_ref[...] = dsmall
        dp_ref[0:1, :] += ddtb
        dp_ref[1:2, :] += dalog
        dp_ref[2:3, :] += ddsk

    rev = lambda c: nch - 1 - c
    return pl.pallas_call(
        body, name="ssd_bwd", grid=(nch,),
        in_specs=[pl.BlockSpec((SSM_CHUNK, CONV_CH), lambda c: (rev(c), 0)),
                  pl.BlockSpec((SSM_CHUNK, LANES), lambda c: (rev(c), OFF_SMALL // LANES)),
                  pl.BlockSpec((None, 8, LANES), lambda c: (l, 0, 0)),
                  pl.BlockSpec((1, N_PAIR, SSM_STATE, LANES), lambda c: (rev(c), 0, 0, 0)),
                  pl.BlockSpec((SSM_CHUNK, D_SSM), lambda c: (rev(c), 0))],
        out_specs=[pl.BlockSpec((SSM_CHUNK, CONV_CH), lambda c: (rev(c), 0)),
                   pl.BlockSpec((SSM_CHUNK, LANES), lambda c: (rev(c), 0)),
                   pl.BlockSpec((8, LANES), lambda c: (0, 0))],
        out_shape=[jax.ShapeDtypeStruct((s, CONV_CH), F32), jax.ShapeDtypeStruct((s, LANES), F32),
                   jax.ShapeDtypeStruct((8, LANES), F32)],
        scratch_shapes=[pltpu.VMEM((N_PAIR, SSM_STATE, LANES), F32)],
        compiler_params=_params(dimension_semantics=("arbitrary",)),
    )(xbc, proj, ptile, prevs, dy)


def _rope_tile(t, cosm, sinm):
    lane = lax.broadcasted_iota(jnp.int32, t.shape, 1)
    half = QK_ROPE // 2
    partner = jnp.where(lane < ROPE_LANE0 + half, pltpu.roll(t, LANES - half, 1), pltpu.roll(t, half, 1))
    return t * cosm + partner * sinm


def _in_rope(shape):
    lane = lax.broadcasted_iota(jnp.int32, shape, 1)
    return jnp.logical_and(lane >= ROPE_LANE0, lane < ROPE_LANE0 + QK_ROPE)


def dsmall_bwd(dk, dsmall_ssd, cosm, sinm_neg):
    def fn(dkt, ds, c, sn):
        inrope = _in_rope(ds.shape)
        tot = dkt[:, 0:HEAD_PAD]
        for h in range(1, MLA_HEADS):
            tot = tot + dkt[:, HEAD_PAD * h:HEAD_PAD * (h + 1)]
        tot = jnp.where(inrope, tot, 0.0)
        return ds + jnp.where(inrope, _rope_tile(tot, c, sn), 0.0)

    return rowwise(fn, [(dk, MLA_HEADS * HEAD_PAD, 0), (dsmall_ssd, LANES, 0), (cosm, LANES, 0), (sinm_neg, LANES, 0)],
                   [], [(LANES, BF16)], [], "dsmall_bwd")[0]


ATT_TQ = 512
ATT_SCALE = (QK_NOPE + QK_ROPE) ** -0.5


def _att_scores(qh, kh, q0):
    s = lax.dot_general(qh, kh, _DIMS['nt'], preferred_element_type=F32) * ATT_SCALE
    r = lax.broadcasted_iota(jnp.int32, s.shape, 0) + q0
    c = lax.broadcasted_iota(jnp.int32, s.shape, 1)
    return jnp.where(c <= r, s, -1e30)


def mla_fwd(q, k, v):
    s = q.shape[0]

    def body(q_ref, k_ref, v_ref, o_ref, lse_ref):
        lane = lax.broadcasted_iota(jnp.int32, (ATT_TQ, LANES), 1)

        def block(ib):
            n = ATT_TQ * (ib + 1)
            v_t = v_ref[0:n, :]
            vlane = lax.broadcasted_iota(jnp.int32, v_t.shape, 1)
            o_tot, lse_tot = None, None
            for h in range(2):
                hs = slice(HEAD_PAD * h, HEAD_PAD * (h + 1))
                sc = _att_scores(q_ref[:, hs], k_ref[0:n, hs], ATT_TQ * ib)
                m = jnp.max(sc, axis=1, keepdims=True)
                p = jnp.exp(sc - m)
                l = jnp.sum(p, axis=1, keepdims=True)
                vh = jnp.where((vlane < V_DIM) if h == 0 else (vlane >= V_DIM), v_t, jnp.zeros_like(v_t))
                oh = lax.dot_general(p.astype(BF16), vh, _DIMS['nn'], preferred_element_type=F32) / l
                lse_h = jnp.where((lane < V_DIM) if h == 0 else (lane >= V_DIM), m + jnp.log(l), 0.0)
                o_tot = oh if o_tot is None else o_tot + oh
                lse_tot = lse_h if lse_tot is None else lse_tot + lse_h
            o_ref[...] = o_tot
            lse_ref[...] = lse_tot

        for ib in range(s // ATT_TQ):
            pl.when(pl.program_id(1) == ib)(functools.partial(block, ib))

    tile = pl.BlockSpec((ATT_TQ, LANES), lambda p, i: (i, p))
    return pl.pallas_call(
        body, name="mla_fwd", grid=(MLA_HEADS // 2, s // ATT_TQ),
        in_specs=[pl.BlockSpec((ATT_TQ, 2 * HEAD_PAD), lambda p, i: (i, p)),
                  pl.BlockSpec((s, 2 * HEAD_PAD), lambda p, i: (0, p)),
                  pl.BlockSpec((s, LANES), lambda p, i: (0, p))],
        out_specs=[tile, tile],
        out_shape=[jax.ShapeDtypeStruct((s, MLA_HEADS * V_DIM), F32)] * 2,
        compiler_params=_params(dimension_semantics=("arbitrary", "arbitrary")),
    )(q, k, v)


def mla_bwd(q, k, v, o, lse, do, cosm, sinm_neg):
    s = q.shape[0]

    def body(q_ref, k_ref, v_ref, o_ref, lse_ref, do_ref, c_ref, s_ref, dq_ref, dk_ref, dv_ref):
        i = pl.program_id(1)

        @pl.when(i == 0)
        def _():
            dk_ref[...] = jnp.zeros_like(dk_ref)
            dv_ref[...] = jnp.zeros_like(dv_ref)

        def block(ib):
            n = ATT_TQ * (ib + 1)
            o_t = o_ref[...]
            do_t = do_ref[...]
            lse_t = lse_ref[...]
            v_t = v_ref[0:n, :]
            lane = lax.broadcasted_iota(jnp.int32, do_t.shape, 1)
            for h in range(2):
                hs = slice(HEAD_PAD * h, HEAD_PAD * (h + 1))
                sel = (lane < V_DIM) if h == 0 else (lane >= V_DIM)
                qh = q_ref[:, hs]
                kh = k_ref[0:n, hs]
                doh = jnp.where(sel, do_t, 0.0)
                delta = jnp.sum(doh * o_t, axis=1, keepdims=True)
                lse_h = jnp.max(jnp.where(sel, lse_t, -jnp.inf), axis=1, keepdims=True)
                doh_b = doh.astype(BF16)
                p = jnp.exp(_att_scores(qh, kh, ATT_TQ * ib) - lse_h)
                dv_ref[0:n, :] += lax.dot_general(p.astype(BF16), doh_b, _DIMS['tn'], preferred_element_type=F32)
                dp = lax.dot_general(doh_b, v_t, _DIMS['nt'], preferred_element_type=F32)
                ds = (p * (dp - delta) * ATT_SCALE).astype(BF16)
                dk_ref[0:n, hs] += lax.dot_general(ds, qh, _DIMS['tn'], preferred_element_type=F32)
                dq = lax.dot_general(ds, kh, _DIMS['nn'], preferred_element_type=F32)
                dq_ref[:, hs] = _rope_tile(dq, c_ref[...], s_ref[...]).astype(dq_ref.dtype)

        for ib in range(s // ATT_TQ):
            pl.when(i == ib)(functools.partial(block, ib))

    tile = pl.BlockSpec((ATT_TQ, LANES), lambda p, i: (i, p))
    return pl.pallas_call(
        body, name="mla_bwd", grid=(MLA_HEADS // 2, s // ATT_TQ),
        in_specs=[pl.BlockSpec((ATT_TQ, 2 * HEAD_PAD), lambda p, i: (i, p)),
                  pl.BlockSpec((s, 2 * HEAD_PAD), lambda p, i: (0, p)),
                  pl.BlockSpec((s, LANES), lambda p, i: (0, p)), tile, tile, tile,
                  pl.BlockSpec((ATT_TQ, LANES), lambda p, i: (i, 0)), pl.BlockSpec((ATT_TQ, LANES), lambda p, i: (i, 0))],
        out_specs=[pl.BlockSpec((ATT_TQ, 2 * HEAD_PAD), lambda p, i: (i, p)),
                   pl.BlockSpec((s, 2 * HEAD_PAD), lambda p, i: (0, p)),
                   pl.BlockSpec((s, LANES), lambda p, i: (0, p))],
        out_shape=[jax.ShapeDtypeStruct((s, MLA_HEADS * HEAD_PAD), BF16),
                   jax.ShapeDtypeStruct((s, MLA_HEADS * HEAD_PAD), F32),
                   jax.ShapeDtypeStruct((s, MLA_HEADS * V_DIM), F32)],
        compiler_params=_params(dimension_semantics=("arbitrary", "arbitrary")),
    )(q, k, v, o, lse, do, cosm, sinm_neg)


MEM_TQ = 256
MEM_SCALE = MEM_HEAD_DIM ** -0.5


def _mem_probs(qh, kh):
    s = lax.dot_general(qh, kh, _DIMS['nt'], preferred_element_type=F32) * MEM_SCALE
    p = jnp.exp(s - jnp.max(s, axis=1, keepdims=True))
    return p / jnp.sum(p, axis=1, keepdims=True)


def mem_fwd(q, k, v):
    s = q.shape[0]

    def body(q_ref, k_ref, v_ref, o_ref):
        for h in range(MEM_HEADS):
            sl = slice(MEM_HEAD_DIM * h, MEM_HEAD_DIM * (h + 1))
            p = _mem_probs(q_ref[:, sl], k_ref[:, sl])
            o_ref[:, sl] = lax.dot_general(p.astype(BF16), v_ref[:, sl], _DIMS['nn'],
                                           preferred_element_type=F32).astype(o_ref.dtype)

    full = pl.BlockSpec((MEM_LEN, D_MODEL), lambda i: (0, 0))
    return pl.pallas_call(
        body, name="mem_fwd", grid=(s // MEM_TQ,),
        in_specs=[pl.BlockSpec((MEM_TQ, D_MODEL), lambda i: (i, 0)), full, full],
        out_specs=pl.BlockSpec((MEM_TQ, D_MODEL), lambda i: (i, 0)),
        out_shape=jax.ShapeDtypeStruct((s, D_MODEL), BF16),
        compiler_params=_params(dimension_semantics=("arbitrary",)),
    )(q, k, v)


def mem_bwd(q, k, v, do):
    s = q.shape[0]

    def body(q_ref, k_ref, v_ref, do_ref, dq_ref, dk_ref, dv_ref):
        @pl.when(pl.program_id(0) == 0)
        def _():
            dk_ref[...] = jnp.zeros_like(dk_ref)
            dv_ref[...] = jnp.zeros_like(dv_ref)

        for h in range(MEM_HEADS):
            sl = slice(MEM_HEAD_DIM * h, MEM_HEAD_DIM * (h + 1))
            qh, kh, vh = q_ref[:, sl], k_ref[:, sl], v_ref[:, sl]
            doh = do_ref[:, sl].astype(BF16)
            p = _mem_probs(qh, kh)
            dv_ref[:, sl] += lax.dot_general(p.astype(BF16), doh, _DIMS['tn'], preferred_element_type=F32)
            dp = lax.dot_general(doh, vh, _DIMS['nt'], preferred_element_type=F32)
            ds = (p * (dp - jnp.sum(p * dp, axis=1, keepdims=True)) * MEM_SCALE).astype(BF16)
            dq_ref[:, sl] = lax.dot_general(ds, kh, _DIMS['nn'], preferred_element_type=F32).astype(dq_ref.dtype)
            dk_ref[:, sl] += lax.dot_general(ds, qh, _DIMS['tn'], preferred_element_type=F32)

    full = pl.BlockSpec((MEM_LEN, D_MODEL), lambda i: (0, 0))
    row = pl.BlockSpec((MEM_TQ, D_MODEL), lambda i: (i, 0))
    return pl.pallas_call(
        body, name="mem_bwd", grid=(s // MEM_TQ,),
        in_specs=[row, full, full, row], out_specs=[row, full, full],
        out_shape=[jax.ShapeDtypeStruct((s, D_MODEL), BF16), jax.ShapeDtypeStruct((MEM_LEN, D_MODEL), F32),
                   jax.ShapeDtypeStruct((MEM_LEN, D_MODEL), F32)],
        compiler_params=_params(dimension_semantics=("arbitrary",)),
    )(q, k, v, do)


def _gate_norm(y, z, g):
    return _rms(y * _silu(z), g)


def loss_head(x, g, target):
    def fn(xt, tt, gt):
        def f(x_, g_):
            err = _rms(x_, g_) - tt
            return 0.5 * jnp.sum(jnp.mean(err * err, axis=-1))

        lv, (dx, dg) = jax.value_and_grad(f, argnums=(0, 1))(xt, gt)
        return dx, dg, jnp.full((1, LANES), lv, F32)

    return rowwise(fn, [(x, D_MODEL, 0), (target, D_MODEL, 0)], [g], [(D_MODEL, F32)],
                   [((1, D_MODEL), F32), ((1, LANES), F32)], "loss_head")


def _proj_runs(d):
    lo, hi = (D_IN // N_DEV) * d, (D_IN // N_DEV) * (d + 1)
    runs = []
    for a, b, new in PROJ_SEGS:
        s0, s1 = max(a, lo), min(b, hi)
        if s0 < s1:
            runs.append((s0 - lo, new + s0 - a, s1 - s0))
    return runs


LAYOUT_TM = 256


def assemble_proj(g):
    def body(g_ref, o_ref):
        o_ref[:, OFF_SMALL:OFF_SMALL + LANES] = jnp.zeros((LAYOUT_TM, LANES), o_ref.dtype)
        for d in range(N_DEV):
            for src, dst, n in _proj_runs(d):
                o_ref[:, dst:dst + n] = g_ref[d, :, src:src + n]

    return pl.pallas_call(
        body, name="assemble_proj", grid=(D_MODEL // LAYOUT_TM,),
        in_specs=[pl.BlockSpec((N_DEV, LAYOUT_TM, D_IN // N_DEV), lambda i: (0, i, 0))],
        out_specs=pl.BlockSpec((LAYOUT_TM, PROJ_W), lambda i: (i, 0)),
        out_shape=jax.ShapeDtypeStruct((D_MODEL, PROJ_W), g.dtype),
        compiler_params=_params(dimension_semantics=("arbitrary",)),
    )(g)


def extract_proj(dz, dxbc, dcq, dsmall, dckv):
    pieces = [(OFF_Z, 1024), (OFF_XBC, 2048), (OFF_CQ, Q_LORA), (OFF_SMALL, LANES), (OFF_CKV, KV_LORA)]

    def body(*refs):
        o_ref = refs[-1]
        for d in range(N_DEV):
            for src, dst, n in _proj_runs(d):
                for p, (off, w) in enumerate(pieces):
                    if off <= dst < off + w:
                        o_ref[d, :, src:src + n] = refs[p][:, dst - off:dst - off + n].astype(o_ref.dtype)

    return pl.pallas_call(
        body, name="extract_proj", grid=(D_MODEL // LAYOUT_TM,),
        in_specs=[pl.BlockSpec((LAYOUT_TM, w), lambda i: (i, 0)) for _, w in pieces],
        out_specs=pl.BlockSpec((N_DEV, LAYOUT_TM, D_IN // N_DEV), lambda i: (0, i, 0)),
        out_shape=jax.ShapeDtypeStruct((N_DEV, D_MODEL, D_IN // N_DEV), BF16),
        compiler_params=_params(dimension_semantics=("arbitrary",)),
    )(dz, dxbc, dcq, dsmall, dckv)


_QW = QK_NOPE + QK_ROPE


def assemble_uq(g):
    def body(g_ref, o_ref):
        o_ref[...] = jnp.zeros_like(o_ref)
        for d in range(N_DEV):
            for e in range(2):
                dst = HEAD_PAD * (2 * d + e)
                o_ref[:, dst:dst + _QW] = g_ref[d, :, _QW * e:_QW * (e + 1)]

    return pl.pallas_call(
        body, name="assemble_uq", grid=(1,),
        in_specs=[pl.BlockSpec((N_DEV, Q_LORA, 2 * _QW), lambda i: (0, 0, 0))],
        out_specs=pl.BlockSpec((Q_LORA, MLA_HEADS * HEAD_PAD), lambda i: (0, 0)),
        out_shape=jax.ShapeDtypeStruct((Q_LORA, MLA_HEADS * HEAD_PAD), g.dtype),
        compiler_params=_params(dimension_semantics=("arbitrary",)),
    )(g)


def extract_uq(dw):
    def body(w_ref, o_ref):
        for d in range(N_DEV):
            for e in range(2):
                src = HEAD_PAD * (2 * d + e)
                o_ref[d, :, _QW * e:_QW * (e + 1)] = w_ref[:, src:src + _QW].astype(o_ref.dtype)

    return pl.pallas_call(
        body, name="extract_uq", grid=(1,),
        in_specs=[pl.BlockSpec((Q_LORA, MLA_HEADS * HEAD_PAD), lambda i: (0, 0))],
        out_specs=pl.BlockSpec((N_DEV, Q_LORA, 2 * _QW), lambda i: (0, 0, 0)),
        out_shape=jax.ShapeDtypeStruct((N_DEV, Q_LORA, 2 * _QW), BF16),
        compiler_params=_params(dimension_semantics=("arbitrary",)),
    )(dw)


def assemble_ukv(g):
    def body(g_ref, kn_ref, v_ref):
        kn_ref[...] = jnp.zeros_like(kn_ref)
        for d in range(N_DEV):
            for e in range(2):
                h = 2 * d + e
                kn_ref[:, HEAD_PAD * h:HEAD_PAD * h + QK_NOPE] = g_ref[d, :, 128 * e:128 * e + QK_NOPE]
                v_ref[:, V_DIM * h:V_DIM * (h + 1)] = g_ref[d, :, 128 * e + QK_NOPE:128 * (e + 1)]

    return pl.pallas_call(
        body, name="assemble_ukv", grid=(1,),
        in_specs=[pl.BlockSpec((N_DEV, KV_LORA, 256), lambda i: (0, 0, 0))],
        out_specs=[pl.BlockSpec((KV_LORA, MLA_HEADS * HEAD_PAD), lambda i: (0, 0)),
                   pl.BlockSpec((KV_LORA, MLA_HEADS * V_DIM), lambda i: (0, 0))],
        out_shape=[jax.ShapeDtypeStruct((KV_LORA, MLA_HEADS * HEAD_PAD), g.dtype),
                   jax.ShapeDtypeStruct((KV_LORA, MLA_HEADS * V_DIM), g.dtype)],
        compiler_params=_params(dimension_semantics=("arbitrary",)),
    )(g)


def extract_ukv(dkn, dv):
    def body(kn_ref, v_ref, o_ref):
        for d in range(N_DEV):
            for e in range(2):
                h = 2 * d + e
                o_ref[d, :, 128 * e:128 * e + QK_NOPE] = kn_ref[:, HEAD_PAD * h:HEAD_PAD * h + QK_NOPE].astype(o_ref.dtype)
                o_ref[d, :, 128 * e + QK_NOPE:128 * (e + 1)] = v_ref[:, V_DIM * h:V_DIM * (h + 1)].astype(o_ref.dtype)

    return pl.pallas_call(
        body, name="extract_ukv", grid=(1,),
        in_specs=[pl.BlockSpec((KV_LORA, MLA_HEADS * HEAD_PAD), lambda i: (0, 0)),
                  pl.BlockSpec((KV_LORA, MLA_HEADS * V_DIM), lambda i: (0, 0))],
        out_specs=pl.BlockSpec((N_DEV, KV_LORA, 256), lambda i: (0, 0, 0)),
        out_shape=jax.ShapeDtypeStruct((N_DEV, KV_LORA, 256), BF16),
        compiler_params=_params(dimension_semantics=("arbitrary",)),
    )(dkn, dv)


_UPW = 2 * D_FF // N_DEV


def assemble_up(g):
    def body(g_ref, wg_ref, wv_ref):
        for d in range(N_DEV):
            ref = wg_ref if d < N_DEV // 2 else wv_ref
            off = _UPW * (d % (N_DEV // 2))
            ref[:, off:off + _UPW] = g_ref[d]

    half = pl.BlockSpec((LAYOUT_TM, D_FF), lambda i: (i, 0))
    return pl.pallas_call(
        body, name="assemble_up", grid=(D_MODEL // LAYOUT_TM,),
        in_specs=[pl.BlockSpec((N_DEV, LAYOUT_TM, _UPW), lambda i: (0, i, 0))],
        out_specs=[half, half], out_shape=[jax.ShapeDtypeStruct((D_MODEL, D_FF), g.dtype)] * 2,
        compiler_params=_params(dimension_semantics=("arbitrary",)),
    )(g)


def extract_up(dwg, dwv):
    def body(wg_ref, wv_ref, o_ref):
        for d in range(N_DEV):
            ref = wg_ref if d < N_DEV // 2 else wv_ref
            off = _UPW * (d % (N_DEV // 2))
            o_ref[d] = ref[:, off:off + _UPW].astype(o_ref.dtype)

    half = pl.BlockSpec((LAYOUT_TM, D_FF), lambda i: (i, 0))
    return pl.pallas_call(
        body, name="extract_up", grid=(D_MODEL // LAYOUT_TM,), in_specs=[half, half],
        out_specs=pl.BlockSpec((N_DEV, LAYOUT_TM, _UPW), lambda i: (0, i, 0)),
        out_shape=jax.ShapeDtypeStruct((N_DEV, D_MODEL, _UPW), BF16),
        compiler_params=_params(dimension_semantics=("arbitrary",)),
    )(dwg, dwv)


MESH = pl.DeviceIdType.MESH
ANY = pl.BlockSpec(memory_space=pl.ANY)


def _place():
    mx, my, mc = lax.axis_index("x"), lax.axis_index("y"), lax.axis_index("c")
    return mx, my, mc, [(1 - mx, my), (mx, 1 - my), (1 - mx, 1 - my)]


def all_gather_blocks(xs, first_only=()):
    n = len(xs)

    def body(*refs):
        x_refs, out_refs = refs[:n], refs[n:2 * n]
        send_sems, recv_sems, local_sems = refs[2 * n:]
        mx, my, mc, chips = _place()
        me, sibling = (mx, my, mc), (mx, my, 1 - mc)
        x_refs = [x_refs[t].at[0] if t in first_only else x_refs[t] for t in range(n)]

        def rows(t, px, py, pc):
            dev = 4 * px + 2 * py + pc
            return out_refs[t].at[dev] if t in first_only else out_refs[t].at[:, dev]

        def copy(t, k, block, to, src=None):
            return pltpu.make_async_remote_copy(
                src_ref=rows(t, *block) if src is None else src, dst_ref=rows(t, *block),
                send_sem=send_sems.at[t, k], recv_sem=recv_sems.at[t, k], device_id=to, device_id_type=MESH)

        mine = [pltpu.make_async_copy(x_refs[t], rows(t, *me), local_sems.at[t]) for t in range(n)]
        for cp in mine:
            cp.start()
        first = []
        for t in range(n):
            first.append(copy(t, 0, me, sibling, src=x_refs[t]))
            first += [copy(t, 1 + j, me, (*chip, mc), src=x_refs[t]) for j, chip in enumerate(chips)]
        for cp in first:
            cp.start()
        passed = []
        for j, chip in enumerate(chips):
            for t in range(n):
                copy(t, 1 + j, (*chip, mc), me).wait_recv()
                cp = copy(t, 4 + j, (*chip, mc), sibling)
                cp.start()
                passed.append(cp)
        for t in range(n):
            copy(t, 0, sibling, me).wait_recv()
            for j, chip in enumerate(chips):
                copy(t, 4 + j, (*chip, 1 - mc), me).wait_recv()
        for cp in first + passed:
            cp.wait_send()
        for cp in mine:
            cp.wait()

    return pl.pallas_call(
        body, name="all_gather_blocks",
        out_shape=[jax.ShapeDtypeStruct(((N_DEV,) if t in first_only else (x.shape[0], N_DEV)) + x.shape[1:], x.dtype)
                   for t, x in enumerate(xs)],
        in_specs=[ANY] * n, out_specs=[ANY] * n,
        scratch_shapes=[pltpu.SemaphoreType.DMA((n, 7)), pltpu.SemaphoreType.DMA((n, 7)), pltpu.SemaphoreType.DMA((n,))],
    )(*xs)


HBM = pl.BlockSpec(memory_space=pltpu.HBM)
SEM = pl.BlockSpec(memory_space=pltpu.SEMAPHORE)
EFFECT = pltpu.SideEffectType.DATAFLOW_SIDE_EFFECTING
ALL_DEVICES = [(px, py, pc) for px in range(2) for py in range(2) for pc in range(2)]


def _hbm(x):
    return pltpu.with_memory_space_constraint(x, pltpu.HBM)


def _split_start(body, name, srcs, lands, after=None):
    ns, n = len(srcs), len(lands)
    extra = [after] if after is not None else []

    def full_body(*refs):
        sems = ns + n + len(extra)
        body(refs[:ns], refs[ns:ns + n], refs[sems], refs[sems + 1])
        refs[-1][...] = jnp.zeros_like(refs[-1])

    res = pl.pallas_call(
        full_body, name=name,
        out_shape=(pltpu.SemaphoreType.DMA((n,)), pltpu.SemaphoreType.DMA((n,)),
                   *[pltpu.HBM(x.shape, x.dtype) for x in srcs], *[pltpu.HBM(x.shape, x.dtype) for x in lands],
                   jax.ShapeDtypeStruct((8, LANES), F32)),
        in_specs=[HBM] * (ns + n) + [ANY] * len(extra),
        out_specs=(SEM, SEM, *[HBM] * (ns + n), pl.BlockSpec(memory_space=pltpu.VMEM)),
        input_output_aliases={i: 2 + i for i in range(ns + n)},
        compiler_params=pltpu.CompilerParams(has_side_effects=EFFECT),
    )(*[_hbm(x) for x in srcs], *[_hbm(x) for x in lands], *extra)
    return res[0], res[1], list(res[2:2 + ns]), list(res[2 + ns:2 + ns + n]), res[-1]


def _split_wait(name, send_sems, recv_sems, srcs, lands, after, sent, landed):
    ns, n = len(srcs), len(lands)

    def body(*refs):
        src_refs, land_refs, ssem, rsem = refs[:ns], refs[ns:ns + n], refs[ns + n], refs[ns + n + 1]
        mx, my, mc, _ = _place()
        for t in range(n):
            out = sent(src_refs[t] if ns else None, land_refs[t])
            inn = landed(land_refs[t])
            pltpu.make_async_remote_copy(src_ref=out, dst_ref=out, send_sem=ssem.at[t], recv_sem=rsem.at[t],
                                         device_id=(mx, my, mc), device_id_type=MESH).wait_send()
            pltpu.make_async_remote_copy(src_ref=inn, dst_ref=inn, send_sem=ssem.at[t], recv_sem=rsem.at[t],
                                         device_id=(mx, my, mc), device_id_type=MESH).wait_recv()

    res = pl.pallas_call(
        body, name=name,
        out_shape=(*[pltpu.HBM(x.shape, x.dtype) for x in srcs], *[pltpu.HBM(x.shape, x.dtype) for x in lands]),
        in_specs=[HBM] * (ns + n) + [SEM, SEM, ANY], out_specs=[HBM] * (ns + n),
        input_output_aliases={i: i for i in range(ns + n)},
        compiler_params=pltpu.CompilerParams(has_side_effects=EFFECT),
    )(*srcs, *lands, send_sems, recv_sems, after)
    return list(res[:ns]), list(res[ns:])


FIRST_HOP = 5
SECOND_HOP = 3


def gather_start(srcs, l, tag, after=None):
    lands = [lax.empty((N_DEV,) + x.shape[1:], x.dtype) for x in srcs]

    def body(src_refs, land_refs, send_sems, recv_sems):
        mx, my, mc, chips = _place()
        me = 4 * mx + 2 * my + mc
        for t in range(len(srcs)):
            for to in [(mx, my, mc), (mx, my, 1 - mc)] + [(cx, cy, mc) for cx, cy in chips]:
                pltpu.make_async_remote_copy(
                    src_ref=src_refs[t].at[l], dst_ref=land_refs[t].at[me], send_sem=send_sems.at[t],
                    recv_sem=recv_sems.at[t], device_id=to, device_id_type=MESH).start()

    return _split_start(body, "gather_start_%d%s" % (l, tag), srcs, lands, after=after)


def gather_wait(l, tag, send_sems, recv_sems, srcs, lands, after):
    hop = lambda d: d.at[pl.ds(0, FIRST_HOP)]
    return _split_wait("gather_wait_%d%s" % (l, tag), send_sems, recv_sems, srcs, lands, after,
                       sent=lambda s, d: hop(d), landed=hop)


def gather_pass_start(lands, l, tag):
    def body(src_refs, land_refs, send_sems, recv_sems):
        mx, my, mc, chips = _place()
        for t in range(len(lands)):
            for cx, cy in chips:
                slot = land_refs[t].at[4 * cx + 2 * cy + mc]
                pltpu.make_async_remote_copy(
                    src_ref=slot, dst_ref=slot, send_sem=send_sems.at[t], recv_sem=recv_sems.at[t],
                    device_id=(mx, my, 1 - mc), device_id_type=MESH).start()

    send_sems, recv_sems, _, lands, tie = _split_start(body, "gather_pass_start_%d%s" % (l, tag), [], lands)
    return send_sems, recv_sems, lands, tie


def gather_pass_wait(l, tag, send_sems, recv_sems, lands, after):
    hop = lambda d: d.at[pl.ds(0, SECOND_HOP)]
    return _split_wait("gather_pass_wait_%d%s" % (l, tag), send_sems, recv_sems, [], lands, after,
                       sent=lambda s, d: hop(d), landed=hop)[1]


def small_gather_start(rows):
    def body(src_refs, land_refs, send_sems, recv_sems):
        mx, my, mc, _ = _place()
        for to in ALL_DEVICES:
            pltpu.make_async_remote_copy(
                src_ref=src_refs[0], dst_ref=land_refs[0].at[4 * mx + 2 * my + mc], send_sem=send_sems.at[0],
                recv_sem=recv_sems.at[0], device_id=to, device_id_type=MESH).start()

    return _split_start(body, "small_gather_start", [rows], [lax.empty((N_DEV,) + rows.shape, rows.dtype)])


def small_gather_wait(send_sems, recv_sems, srcs, lands, after):
    return _split_wait("small_gather_wait", send_sems, recv_sems, srcs, lands, after,
                       sent=lambda s, d: d, landed=lambda d: d)[1][0]


def grad_exchange_start(es, lands, l, tag, after=None):
    def body(e_refs, land_refs, send_sems, recv_sems):
        mx, my, mc, _ = _place()
        me = 4 * mx + 2 * my + mc
        for t in range(len(es)):
            for px, py, pc in ALL_DEVICES:
                pltpu.make_async_remote_copy(
                    src_ref=e_refs[t].at[4 * px + 2 * py + pc], dst_ref=land_refs[t].at[l, me], send_sem=send_sems.at[t],
                    recv_sem=recv_sems.at[t], device_id=(px, py, pc), device_id_type=MESH).start()

    return _split_start(body, "grad_exchange_start_%d%s" % (l, tag), es, lands, after=after)


def grad_exchange_wait(l, tag, send_sems, recv_sems, es, lands, after):
    return _split_wait("grad_exchange_wait_%d%s" % (l, tag), send_sems, recv_sems, es, lands, after,
                       sent=lambda s, d: s, landed=lambda d: d.at[l])


def _adam(g, w, m, v):
    nm = ADAM_B1 * m + (1.0 - ADAM_B1) * g
    nv = ADAM_B2 * v + (1.0 - ADAM_B2) * jnp.square(g)
    m_hat = nm / (1.0 - ADAM_B1 ** ADAM_STEP)
    v_hat = nv / (1.0 - ADAM_B2 ** ADAM_STEP)
    return -ADAM_LR * (m_hat / (jnp.sqrt(v_hat) + ADAM_EPS) + ADAM_WD * w), nm, nv


def adamw_big(parts, w, m, v, name, tie):
    depth, _, a, b = parts.shape
    ta = _row_tile(a)

    def body(p_ref, w_ref, m_ref, v_ref, tie_ref, g_ref, d_ref, nm_ref, nv_ref):
        g = p_ref[0].astype(F32)
        for k in range(1, N_DEV):
            g = g + p_ref[k].astype(F32)
        g_ref[...] = g
        d_ref[...], nm_ref[...], nv_ref[...] = _adam(g, w_ref[...], m_ref[...], v_ref[...])

    blk = pl.BlockSpec((None, ta, b), lambda l, i: (l, i, 0))
    return pl.pallas_call(
        body, name=name, grid=(depth, a // ta),
        in_specs=[pl.BlockSpec((None, N_DEV, ta, b), lambda l, i: (l, 0, i, 0)), blk, blk, blk, ANY], out_specs=[blk] * 4,
        out_shape=[jax.ShapeDtypeStruct((depth, a, b), F32)] * 4,
        compiler_params=_params(dimension_semantics=("arbitrary", "arbitrary")),
    )(parts, w, m, v, tie)


SMALL_VIEW = {'norm_mix': (DEPTH, 1024), 'ssm_norm': (DEPTH, 1024), 'attn_out_norm': (DEPTH, 1024),
              'norm_mem_q': (DEPTH, 1024), 'norm_mem_kv': (DEPTH, 1024), 'norm_ffn': (DEPTH, 1024),
              'q_norm': (DEPTH, 384), 'kv_norm': (DEPTH, 256), 'ssm_conv_b': (DEPTH, 2048), 'ffn_conv_b': (DEPTH, 5632),
              'dt_bias': (DEPTH, SSM_HEADS), 'a_log': (DEPTH, SSM_HEADS), 'd_skip': (DEPTH, SSM_HEADS),
              'ssm_conv_w': (DEPTH, SSM_CONV * CONV_CH // N_DEV), 'ffn_conv_w': (DEPTH, FFN_CONV * 2 * D_FF // N_DEV),
              'final_norm': (1, 1024)}
SMALL_NAMES = list(SMALL_VIEW)
SMALL_SHARDED = {'ssm_conv_w': (SSM_CONV, CONV_CH // N_DEV, CONV_CH), 'ffn_conv_w': (FFN_CONV, 2 * D_FF // N_DEV, 2 * D_FF)}


def adamw_small(gathered, ws, ms, vs, tie):
    nsm = len(SMALL_NAMES)

    def body(*refs):
        g8_ref = refs[0]
        w_refs, m_refs, v_refs = refs[1:1 + nsm], refs[1 + nsm:1 + 2 * nsm], refs[1 + 2 * nsm:1 + 3 * nsm]
        outs = refs[2 + 3 * nsm:2 + 7 * nsm]
        sum_ref = refs[2 + 7 * nsm]
        shard_bufs = refs[3 + 7 * nsm:]
        tot = g8_ref[0]
        for d in range(1, N_DEV):
            tot = tot + g8_ref[d]
        sum_ref[...] = tot
        mx, my, mc, _ = _place()
        dev = 4 * mx + 2 * my + mc

        def update(i, g):
            d, nm, nv = _adam(g, w_refs[i][...], m_refs[i][...], v_refs[i][...])
            outs[i][...] = g
            outs[nsm + i][...] = d
            outs[2 * nsm + i][...] = nm
            outs[3 * nsm + i][...] = nv

        for i, name in enumerate(SMALL_NAMES):
            rows, cols = SMALL_VIEW[name]
            off = SMALL_OFF[name]
            if name in SMALL_SHARDED:
                taps, per, full = SMALL_SHARDED[name]
                buf = shard_bufs[list(SMALL_SHARDED).index(name)]
                for d in range(N_DEV):
                    @pl.when(dev == d)
                    def _(d=d, taps=taps, per=per, full=full, off=off, buf=buf):
                        for k in range(taps):
                            buf[:, per * k:per * (k + 1)] = sum_ref[:, off + full * k + per * d:off + full * k + per * (d + 1)]
                update(i, buf[...])
            else:
                update(i, sum_ref[0:rows, off:off + cols])

    views = [jax.ShapeDtypeStruct(SMALL_VIEW[n], F32) for n in SMALL_NAMES]
    vmem = pl.BlockSpec(memory_space=pltpu.VMEM)
    res = pl.pallas_call(
        body, name="adamw_small", out_shape=views * 4, in_specs=[vmem] * (1 + 3 * nsm) + [ANY],
        out_specs=[vmem] * (4 * nsm),
        scratch_shapes=[pltpu.VMEM((DEPTH, SMALL_W), F32)] + [pltpu.VMEM(SMALL_VIEW[n], F32) for n in SMALL_SHARDED],
        compiler_params=_params(),
    )(gathered, *[ws[n] for n in SMALL_NAMES], *[ms[n] for n in SMALL_NAMES], *[vs[n] for n in SMALL_NAMES], tie)
    return [dict(zip(SMALL_NAMES, res[k * nsm:(k + 1) * nsm])) for k in range(4)]


def _layer_weights(gathered):
    w = {}
    for n, g in gathered.items():
        if n == 'w_in':
            w['w_proj'] = assemble_proj(g)
        elif n == 'w_uq':
            w['w_uq'] = assemble_uq(g)
        elif n == 'w_ukv':
            w['w_kn'], w['w_v'] = assemble_ukv(g)
        elif n == 'w_up':
            w['w_g'], w['w_vv'] = assemble_up(g)
        else:
            w[n] = g.reshape(N_DEV * BIG[n][0], BIG[n][1])
    return w


def _rope_post(acc, row_tiles, full_tiles, o_refs):
    for h in range(acc.shape[1] // HEAD_PAD):
        sl = slice(HEAD_PAD * h, HEAD_PAD * (h + 1))
        o_refs[0][:, sl] = _rope_tile(acc[:, sl], row_tiles[0], row_tiles[1]).astype(o_refs[0].dtype)


def _k_post(acc, row_tiles, full_tiles, o_refs):
    small, c, sn = row_tiles
    inrope = _in_rope(small.shape)
    kpe = jnp.where(inrope, _rope_tile(jnp.where(inrope, small, 0.0), c, sn), 0.0)
    for h in range(acc.shape[1] // HEAD_PAD):
        sl = slice(HEAD_PAD * h, HEAD_PAD * (h + 1))
        o_refs[0][:, sl] = (acc[:, sl] + kpe).astype(o_refs[0].dtype)


PROJ_TAIL = PROJ_W - OFF_CQ


def _proj_post(acc, row_tiles, full_tiles, o_refs):
    o_refs[0][...] = acc

    @pl.when(pl.program_id(1) == PROJ_W // PROJ_TAIL - 1)
    def _():
        o_refs[1][...] = _rms(acc[:, 0:Q_LORA], full_tiles[0]).astype(o_refs[1].dtype)
        o_refs[2][...] = _rms(acc[:, OFF_CKV - OFF_CQ:OFF_CKV - OFF_CQ + KV_LORA], full_tiles[1]).astype(o_refs[2].dtype)


def _norm_post(acc, row_tiles, full_tiles, o_refs):
    o_refs[0][...] = acc
    o_refs[1][...] = _rms(acc, full_tiles[0]).astype(o_refs[1].dtype)


def _norm_bwd_post(acc, row_tiles, full_tiles, o_refs):
    _, vjp = jax.vjp(_rms, row_tiles[0], full_tiles[0])
    dx, dg = vjp(acc)
    o_refs[0][...] = dx + row_tiles[1]
    o_refs[1][...] += dg


def _latent_norm_bwd_post(acc, row_tiles, full_tiles, o_refs):
    _, vjp = jax.vjp(_rms, row_tiles[0], full_tiles[0])
    dx, dg = vjp(acc)
    o_refs[0][...] = dx.astype(o_refs[0].dtype)
    o_refs[1][...] += dg


def _mix_bwd_post(acc, row_tiles, full_tiles, o_refs):
    y, z, o = row_tiles
    _, vjp = jax.vjp(_gate_norm, y, z, full_tiles[0])
    dy, dz, dg = vjp(acc[:, :D_SSM])
    _, vjp_o = jax.vjp(_rms, o, full_tiles[1])
    do, dg_o = vjp_o(acc[:, D_SSM:])
    o_refs[0][...] = dy
    o_refs[1][...] = dz.astype(o_refs[1].dtype)
    o_refs[2][...] = do
    o_refs[3][...] += dg
    o_refs[4][...] += dg_o


def layer_fwd(x0, h1, mem, cosm, sinm, w, sm, l, tie=None):
    gain = lambda n: (sm[n], l)
    sv = dict(x0=x0)
    sv['h1'] = h1 if h1 is not None else rmsnorm_fwd(x0, gain('norm_mix'), "norm_mix_fwd", tie=tie)
    proj, sv['cqn'], sv['ckvn'] = matmul(
        [(sv['h1'], w['w_proj'])], 'nn', F32, "proj_fwd", tie=tie if h1 is not None else None, post=_proj_post,
        fulls=[gain('q_norm'), gain('kv_norm')], outs=[F32, (Q_LORA, BF16), (KV_LORA, BF16)], tn_fixed=PROJ_TAIL)
    sv['proj'] = proj
    sv['xbc'] = ssm_conv_fwd(proj, sm['ssm_conv_w'], sm['ssm_conv_b'], l)
    sv['y'], sv['prevs'], mix = ssd_fwd(sv['xbc'], proj, sm['ptile'], sm['ssm_norm'], l)
    sv['q'] = matmul([(sv['cqn'], w['w_uq'])], 'nn', BF16, "uq_fwd", post=_rope_post, rows=[cosm, sinm])
    sv['k'] = matmul([(sv['ckvn'], w['w_kn'])], 'nn', BF16, "kn_fwd", post=_k_post,
                     rows=[(proj, LANES, OFF_SMALL // LANES), cosm, sinm])
    sv['v'] = matmul([(sv['ckvn'], w['w_v'])], 'nn', BF16, "v_fwd")
    sv['o'], sv['lse'] = mla_fwd(sv['q'], sv['k'], sv['v'])
    mix = sv['mix'] = rmsnorm_fwd(sv['o'], gain('attn_out_norm'), "attn_out_norm_fwd", out=(D_SSM, BF16, D_MIX, 1),
                                  into=(mix, 0))
    x1, sv['hq'] = matmul([(mix, w['w_out'])], 'nn', F32, "out_fwd", add=x0, post=_norm_post,
                          fulls=[gain('norm_mem_q')], outs=[F32, BF16], full_n=True)
    sv['x1'] = x1
    sv['mn'] = rmsnorm_fwd(mem, gain('norm_mem_kv'), "norm_mem_kv_fwd")
    if 'later' in w:
        w.update(w.pop('later')(sv['hq']))
    sv['mq'] = matmul([(sv['hq'], w['w_mq'])], 'nn', BF16, "mq_fwd")
    sv['mk'] = matmul([(sv['mn'], w['w_mk'])], 'nn', BF16, "mk_fwd")
    sv['mv'] = matmul([(sv['mn'], w['w_mv'])], 'nn', BF16, "mv_fwd")
    sv['om'] = mem_fwd(sv['mq'], sv['mk'], sv['mv'])
    x2, sv['h3'] = matmul([(sv['om'], w['w_mo'])], 'nn', F32, "mo_fwd", add=x1, post=_norm_post,
                          fulls=[gain('norm_ffn')], outs=[F32, BF16], full_n=True)
    sv['x2'] = x2
    tie_ffn = w.pop('prefetch')(sv['h3']) if 'prefetch' in w else None
    sv['ug'] = matmul([(sv['h3'], w['w_g'])], 'nn', F32, "up_g_fwd", tie=tie_ffn)
    sv['uv'] = matmul([(sv['h3'], w['w_vv'])], 'nn', F32, "up_v_fwd")
    sv['a'] = ffn_act_fwd(sv['ug'], sv['uv'], sm['ffn_conv_w'], sm['ffn_conv_b'], l)
    if l + 1 < DEPTH:
        x3, h1_next = matmul([(sv['a'], w['w_down'])], 'nn', F32, "down_fwd", add=x2, post=_norm_post,
                             fulls=[(sm['norm_mix'], l + 1)], outs=[F32, BF16], full_n=True)
    else:
        x3, h1_next = matmul([(sv['a'], w['w_down'])], 'nn', F32, "down_fwd_last", add=x2), None
    return x3, h1_next, sv


EARLY_GRADS = ('w_down', 'w_up', 'w_mo', 'w_mq', 'w_mk', 'w_mv', 'w_out')
LATE_GRADS = ('w_uq', 'w_ukv', 'w_in')


def layer_bwd(dx3, mem, cosm, sinm_neg, w, sm, l, sv, on_grads, tie=None):
    gain = lambda n: (sm[n], l)
    big, small = {}, {}
    proj = sv['proj']
    da = matmul([(dx3, w['w_down'])], 'nt', BF16, "down_bwd_a", tie=tie)
    big['w_down'] = matmul([(sv['a'], dx3)], 'tn', BF16, "down_bwd_w")
    dug, duv, dcwg, dcwv, dcbg, dcbv = ffn_act_bwd(sv['ug'], sv['uv'], sm['ffn_conv_w'], sm['ffn_conv_b'], l, da)
    small['ffn_conv_w'] = jnp.concatenate([dcwg, dcwv], axis=1)
    small['ffn_conv_b'] = jnp.concatenate([dcbg, dcbv], axis=1)
    gacc = [((1, D_MODEL), F32)]
    dx2, small['norm_ffn'] = matmul([(dug, w['w_g']), (duv, w['w_vv'])], 'nt', F32, "up_bwd_h", post=_norm_bwd_post,
                                    rows=[sv['x2'], dx3], fulls=[gain('norm_ffn')], accs=gacc, full_n=True, tm_cap=256)
    big['w_up'] = extract_up(matmul([(sv['h3'], dug)], 'tn', BF16, "up_g_bwd_w"),
                             matmul([(sv['h3'], duv)], 'tn', BF16, "up_v_bwd_w"))
    dom = matmul([(dx2, w['w_mo'])], 'nt', BF16, "mo_bwd_a")
    big['w_mo'] = matmul([(sv['om'], dx2)], 'tn', BF16, "mo_bwd_w")
    dmq, dmk, dmv = mem_bwd(sv['mq'], sv['mk'], sv['mv'], dom)
    dx1, small['norm_mem_q'] = matmul([(dmq, w['w_mq'])], 'nt', F32, "mq_bwd_a", post=_norm_bwd_post,
                                      rows=[sv['x1'], dx2], fulls=[gain('norm_mem_q')], accs=gacc, full_n=True)
    big['w_mq'] = matmul([(sv['hq'], dmq)], 'tn', BF16, "mq_bwd_w")
    dmn = matmul([(dmk, w['w_mk']), (dmv, w['w_mv'])], 'nt', BF16, "mkv_bwd_a")
    big['w_mk'] = matmul([(sv['mn'], dmk)], 'tn', BF16, "mk_bwd_w")
    big['w_mv'] = matmul([(sv['mn'], dmv)], 'tn', BF16, "mv_bwd_w")
    _, small['norm_mem_kv'] = rmsnorm_bwd(mem, gain('norm_mem_kv'), dmn, "norm_mem_kv_bwd", dx_dtype=BF16)
    big['w_out'] = matmul([(sv['mix'], dx1)], 'tn', BF16, "out_bwd_w")
    early = {n: big.pop(n).reshape((N_DEV,) + BIG[n]) if n != 'w_up' else big.pop(n) for n in EARLY_GRADS}
    tie = on_grads(l, 'a', early)
    dy, dz, do, small['ssm_norm'], small['attn_out_norm'] = matmul(
        [(dx1, w['w_out'])], 'nt', F32, "out_bwd_a", post=_mix_bwd_post, tie=tie,
        rows=[sv['y'], (proj, D_SSM, OFF_Z // D_SSM), sv['o']], fulls=[gain('ssm_norm'), gain('attn_out_norm')],
        outs=[(D_SSM, F32), (D_SSM, BF16), (D_SSM, F32)], accs=[((1, D_SSM), F32)] * 2, full_n=True)
    dxbc_act, dsmall_ssd, small['ptile'] = ssd_bwd(sv['xbc'], proj, sm['ptile'], l, sv['prevs'], dy)
    dxbc, small['ssm_conv_w'], small['ssm_conv_b'] = ssm_conv_bwd(proj, sm['ssm_conv_w'], sm['ssm_conv_b'], l, dxbc_act)
    dq, dk, dv = mla_bwd(sv['q'], sv['k'], sv['v'], sv['o'], sv['lse'], do, cosm, sinm_neg)
    dsmall = dsmall_bwd(dk, dsmall_ssd, cosm, sinm_neg)
    dcq, small['q_norm'] = matmul(
        [(dq, w['w_uq'])], 'nt', F32, "uq_bwd_a", post=_latent_norm_bwd_post, rows=[(proj, Q_LORA, OFF_CQ // Q_LORA)],
        fulls=[gain('q_norm')], outs=[BF16], accs=[((1, Q_LORA), F32)], full_n=True)
    big['w_uq'] = extract_uq(matmul([(sv['cqn'], dq)], 'tn', BF16, "uq_bwd_w"))
    dckv, small['kv_norm'] = matmul(
        [(dk, w['w_kn']), (dv, w['w_v'])], 'nt', F32, "ukv_bwd_a", post=_latent_norm_bwd_post,
        rows=[(proj, KV_LORA, OFF_CKV // KV_LORA)], fulls=[gain('kv_norm')], outs=[BF16], accs=[((1, KV_LORA), F32)],
        full_n=True)
    big['w_ukv'] = extract_ukv(matmul([(sv['ckvn'], dk)], 'tn', BF16, "kn_bwd_w"),
                               matmul([(sv['ckvn'], dv)], 'tn', BF16, "v_bwd_w"))
    wp = w['w_proj']
    xbc_half = lambda c: Opnd(dxbc, c0=c, shape=(dxbc.shape[0], 1024))
    wwin = lambda off, width: Opnd(wp, c0=off // width, shape=(D_MODEL, width))
    dx0, small['norm_mix'] = matmul(
        [(dz, wwin(OFF_Z, 1024)), (xbc_half(0), wwin(OFF_XBC, 1024)), (xbc_half(1), wwin(OFF_XBC + 1024, 1024)),
         (dcq, wwin(OFF_CQ, Q_LORA)), (dsmall, wwin(OFF_SMALL, LANES)), (dckv, wwin(OFF_CKV, KV_LORA))],
        'nt', F32, "proj_bwd_a", post=_norm_bwd_post, rows=[sv['x0'], dx1], fulls=[gain('norm_mix')], accs=gacc,
        full_n=True, tm_cap=256)
    h1 = sv['h1']
    big['w_in'] = extract_proj(
        matmul([(h1, dz)], 'tn', BF16, "proj_z_bwd_w"), matmul([(h1, dxbc)], 'tn', BF16, "proj_xbc_bwd_w"),
        matmul([(h1, dcq)], 'tn', BF16, "proj_cq_bwd_w"), matmul([(h1, dsmall)], 'tn', BF16, "proj_small_bwd_w"),
        matmul([(h1, dckv)], 'tn', BF16, "proj_ckv_bwd_w"))
    return dx0, on_grads(l, 'b', big), small


def _small_row(small, final=None):
    pt = small['ptile']
    parts = []
    for n, wd in SMALL_SEGS:
        if n in ('dt_bias', 'a_log', 'd_skip'):
            parts.append(pt[('dt_bias', 'a_log', 'd_skip').index(n)][None, :])
        elif n in SMALL_SHARDED:
            parts.append(small[n].reshape(1, wd))
        elif n == 'final_norm':
            parts.append(final if final is not None else jnp.zeros((1, wd), F32))
        else:
            parts.append(small[n])
    return jnp.concatenate(parts, axis=1)


def _rope_tables(positions):
    inv_freq = 1.0 / (ROPE_THETA ** (jnp.arange(0, QK_ROPE, 2, dtype=F32) / QK_ROPE))
    ang = positions.astype(F32)[:, None] * inv_freq
    cos, sin = jnp.cos(ang), jnp.sin(ang)
    s = positions.shape[0]
    pad = jnp.zeros((s, LANES - ROPE_LANE0 - QK_ROPE), F32)
    cosm = jnp.concatenate([jnp.ones((s, ROPE_LANE0), F32), cos, cos, pad], axis=1)
    sinm = jnp.concatenate([jnp.zeros((s, ROPE_LANE0), F32), -sin, sin, pad], axis=1)
    return cosm, sinm


def _small_views(rep, conv_full):
    sm = {n: rep[n].reshape(DEPTH, 1, -1) for n in ('norm_mix', 'ssm_norm', 'attn_out_norm', 'norm_mem_q',
                                                    'norm_mem_kv', 'norm_ffn', 'q_norm', 'kv_norm', 'ssm_conv_b',
                                                    'ffn_conv_b')}
    sm.update(conv_full)
    rows = jnp.stack([rep['dt_bias'], rep['a_log'], rep['d_skip']], axis=1)
    sm['ptile'] = jnp.pad(rows, ((0, 0), (0, 8 - 3), (0, LANES - SSM_HEADS)))
    return sm


def local_step(x, mem, positions, target, sm, final_norm, weights_of, on_grads):
    cosm, sinm = _rope_tables(positions)
    sinm_neg = -sinm
    saved, ws = [], []
    h, h1 = x, None
    for l in range(DEPTH):
        w, tie = weights_of(l, h)
        ws.append(w)
        h, h1, sv = layer_fwd(h, h1, mem, cosm, sinm, w, sm, l, tie=tie)
        saved.append(sv)
    dx, dfinal, lossv = loss_head(h, (final_norm.reshape(1, 1, -1), 0), target)
    rows = [None] * DEPTH
    tie = None
    for l in reversed(range(DEPTH)):
        dx, tie, small = layer_bwd(dx, mem, cosm, sinm_neg, ws[l], sm, l, saved[l], on_grads, tie=tie)
        rows[l] = _small_row(small, dfinal if l == 0 else None)
    return lossv[0, 0], dx, jnp.concatenate(rows, axis=0)


def kernel(x, mem, positions, norm_mix, w_in, ssm_conv_w, ssm_conv_b, dt_bias, a_log, d_skip, ssm_norm, q_norm, w_uq, kv_norm, w_ukv, attn_out_norm, w_out, norm_mem_q, norm_mem_kv, w_mq, w_mk, w_mv, w_mo, norm_ffn, w_up, ffn_conv_w, ffn_conv_b, w_down, final_norm, loss_target, m_norm_mix, m_w_in, m_ssm_conv_w, m_ssm_conv_b, m_dt_bias, m_a_log, m_d_skip, m_ssm_norm, m_q_norm, m_w_uq, m_kv_norm, m_w_ukv, m_attn_out_norm, m_w_out, m_norm_mem_q, m_norm_mem_kv, m_w_mq, m_w_mk, m_w_mv, m_w_mo, m_norm_ffn, m_w_up, m_ffn_conv_w, m_ffn_conv_b, m_w_down, m_final_norm, v_norm_mix, v_w_in, v_ssm_conv_w, v_ssm_conv_b, v_dt_bias, v_a_log, v_d_skip, v_ssm_norm, v_q_norm, v_w_uq, v_kv_norm, v_w_ukv, v_attn_out_norm, v_w_out, v_norm_mem_q, v_norm_mem_kv, v_w_mq, v_w_mk, v_w_mv, v_w_mo, v_norm_ffn, v_w_up, v_ffn_conv_w, v_ffn_conv_b, v_w_down, v_final_norm):
    args = locals()
    wts = {n: args[n] for n in WEIGHT_NAMES}
    ms = {n: args['m_' + n] for n in WEIGHT_NAMES}
    vs = {n: args['v_' + n] for n in WEIGHT_NAMES}

    st = dict(srcs={n: wts[n].astype(BF16) for n in BIG_NAMES}, exchanges=[],
              lands={n: lax.empty((DEPTH, N_DEV) + BIG[n], BF16) for n in BIG_NAMES})
    first = LATE_GRADS + ('w_out',)
    rest = tuple(n for n in BIG_NAMES if n not in first)
    got = all_gather_blocks([st['srcs'][n] for n in first] + [wts[n] for n in SMALL_SHARDED],
                            first_only=tuple(range(len(first))))
    conv_full = {}
    for n, g in zip(SMALL_SHARDED, got[len(first):]):
        taps, per, full = SMALL_SHARDED[n]
        conv_full[n] = jnp.moveaxis(g, 1, 2).reshape(DEPTH, taps, full)
    sm = _small_views(wts, conv_full)

    def start(names, l, tag, after=None):
        send_sems, recv_sems, thru, lands, tie = gather_start([st['srcs'][n] for n in names], l, tag, after)
        st['srcs'].update(zip(names, thru))
        return (names, l, tag, send_sems, recv_sems, lands), tie

    def pass_on(handle, after):
        names, l, tag, send_sems, recv_sems, lands = handle
        thru, lands = gather_wait(l, tag, send_sems, recv_sems, [st['srcs'][n] for n in names], lands, after)
        st['srcs'].update(zip(names, thru))
        send_sems, recv_sems, lands, tie = gather_pass_start(lands, l, tag)
        return (names, l, tag, send_sems, recv_sems, lands), tie

    def finish(handle, after):
        names, l, tag, send_sems, recv_sems, lands = handle
        return _layer_weights(dict(zip(names, gather_pass_wait(l, tag, send_sems, recv_sems, lands, after))))

    later, _ = start(rest, 0, "r", after=got[0])

    def weights_of(l, h):
        if l == 0:
            w = _layer_weights(dict(zip(first, got[:len(first)])))
            w['later'] = lambda after: finish(pass_on(later, after)[0], after)
        else:
            w = finish(st['next'], h)
        tie = None
        if l + 1 < DEPTH:
            st['next'], tie = start(BIG_NAMES, l + 1, "")

            def prefetch(after):
                st['next'], tie2 = pass_on(st['next'], after)
                return tie2

            w['prefetch'] = prefetch
        return w, tie

    def on_grads(l, tag, big, after=None):
        if (l, tag) == (0, 'b') and after is None:
            st['held'] = big
            return None
        names = list(big)
        send_sems, recv_sems, thru, lands, tie = grad_exchange_start(
            [big[n] for n in names], [st['lands'][n] for n in names], l, tag, after)
        st['lands'].update(zip(names, lands))
        st['exchanges'].append((l, tag, names, send_sems, recv_sems, thru))
        return tie

    loss_local, dx, small_rows = local_step(x[0], mem[0], positions[0], loss_target[0], sm, final_norm, weights_of,
                                            on_grads)
    outs = [{}, {}, {}, {}]

    sg_send, sg_recv, sg_src, sg_land, tok = small_gather_start(small_rows)
    tie = on_grads(0, 'b', st['held'], after=tok)

    def wait(exchange, after):
        l, tag, names, send_sems, recv_sems, thru = exchange
        _, lands = grad_exchange_wait(l, tag, send_sems, recv_sems, thru, [st['lands'][n] for n in names], after)
        st['lands'].update(zip(names, lands))

    def update(names, tie):
        for n in names:
            res_n = adamw_big(st['lands'][n], wts[n], ms[n], vs[n], "adamw_" + n, tie)
            tie = res_n[0]
            for k in range(4):
                outs[k][n] = res_n[k]
        return tie

    for exchange in st['exchanges'][:-1]:
        wait(exchange, tie)
    tie = update(EARLY_GRADS, tie)

    small_all = small_gather_wait(sg_send, sg_recv, sg_src, sg_land, tie)
    view = lambda d: {n: d[n].reshape(SMALL_VIEW[n]) for n in SMALL_NAMES}
    res = adamw_small(small_all, view(wts), view(ms), view(vs), tie)
    for k in range(4):
        for n in SMALL_NAMES:
            outs[k][n] = res[k][n].reshape(wts[n].shape)

    wait(st['exchanges'][-1], res[0]['final_norm'])
    update(LATE_GRADS, res[0]['final_norm'])

    loss = lax.psum(loss_local, ("x", "y", "c"))
    return (loss, dx[None], *[outs[0][n] for n in WEIGHT_NAMES], *[outs[1][n] for n in WEIGHT_NAMES],
            *[outs[2][n] for n in WEIGHT_NAMES], *[outs[3][n] for n in WEIGHT_NAMES])
```
